```python
import math
import jax, jax.numpy as jnp
from jax import lax
import numpy as np

D_MODEL = 1024
BATCH = 8
SEQ = 8192
DEPTH = 1

CONV_DIM = D_MODEL
CONV_WIDTH = 3
HEAD_DIM = 64
ATTN_SLOTS = 8
WINDOWS = (128, 512, 2048)
DILATIONS = (1, 4, 16)
N_GROUPS = len(WINDOWS)
ATTN_HEADS = N_GROUPS * ATTN_SLOTS
ATTN_DIM = ATTN_HEADS * HEAD_DIM
ATTN_OUT = ATTN_SLOTS * HEAD_DIM
Q_BLOCK = 128
EPS = 1e-6
NEG_INF = -1e30

IN_SIZES = (CONV_DIM, CONV_DIM, CONV_DIM, CONV_DIM,
            ATTN_DIM, ATTN_DIM, ATTN_DIM,
            ATTN_OUT,
            D_MODEL, D_MODEL)
IN_TOTAL = sum(IN_SIZES)
IN_SPLITS = tuple(int(v) for v in np.cumsum(IN_SIZES)[:-1])

kernel_name = "hybrid_shortconv_dilated_swa_gated_merge"


def rms_norm(t, w):
    tf = t.astype(jnp.float32)
    tf = tf * lax.rsqrt(jnp.mean(tf * tf, axis=-1, keepdims=True) + EPS)
    return (tf * w.astype(jnp.float32)).astype(t.dtype)


def window_attend(q, k, v, w_sub):
    n, length, hd = q.shape
    blk = math.gcd(Q_BLOCK, length)
    nblk = length // blk
    span = blk + w_sub
    kp = jnp.pad(k, ((0, 0), (w_sub, 0), (0, 0)))
    vp = jnp.pad(v, ((0, 0), (w_sub, 0), (0, 0)))
    idx = jnp.arange(nblk)[:, None] * blk + jnp.arange(span)[None, :]
    kb = kp[:, idx]
    vb = vp[:, idx]
    qb = q.reshape(n, nblk, blk, hd)
    scores = jnp.einsum('nbqd,nbkd->nbqk', qb, kb, preferred_element_type=jnp.float32) * (hd ** -0.5)
    dist = jnp.arange(blk)[:, None] + w_sub - jnp.arange(span)[None, :]
    valid = (dist >= 0) & (dist <= w_sub)
    valid = valid[None, :, :] & (idx >= w_sub)[:, None, :]
    scores = jnp.where(valid, scores, NEG_INF)
    mx = jnp.max(scores, axis=-1)
    p = jnp.exp(scores - mx[..., None])
    s = jnp.sum(p, axis=-1)
    o = jnp.einsum('nbqk,nbkd->nbqd', p, vb.astype(jnp.float32))
    return o.reshape(n, length, hd), mx.reshape(n, length), s.reshape(n, length)


def dilated_attend(q, k, v, window, dilation):
    b, s_len, h, hd = q.shape
    length = s_len // dilation
    w_sub = window // dilation

    def to_sub(t):
        return t.reshape(b, length, dilation, h, hd).transpose(0, 2, 3, 1, 4).reshape(b * dilation * h, length, hd)

    o, m, s = window_attend(to_sub(q), to_sub(k), to_sub(v), w_sub)
    o = o.reshape(b, dilation, h, length, hd).transpose(0, 3, 1, 2, 4).reshape(b, s_len, h, hd)
    m = m.reshape(b, dilation, h, length).transpose(0, 3, 1, 2).reshape(b, s_len, h)
    s = s.reshape(b, dilation, h, length).transpose(0, 3, 1, 2).reshape(b, s_len, h)
    return o, m, s


def _fwd_setup_inputs(seed: int = 0) -> dict:
    key = jax.random.key(seed)
    ks = jax.random.split(key, 14)
    f32 = jnp.float32
    nrm = lambda k, shape: jax.random.normal(k, shape, f32)
    x = nrm(ks[0], (BATCH, SEQ, D_MODEL))
    c = nrm(ks[1], (BATCH, D_MODEL))
    w_ada = nrm(ks[2], (DEPTH, D_MODEL, 3 * D_MODEL)) * D_MODEL ** -0.5
    b_ada = 0.02 * nrm(ks[3], (DEPTH, 3 * D_MODEL))
    norm_w = 1.0 + 0.1 * nrm(ks[4], (DEPTH, D_MODEL))
    w_in = nrm(ks[5], (DEPTH, D_MODEL, IN_TOTAL)) * D_MODEL ** -0.5
    conv_w = nrm(ks[6], (DEPTH, CONV_WIDTH, CONV_DIM)) * CONV_WIDTH ** -0.5
    q_norm_w = 1.0 + 0.1 * nrm(ks[7], (DEPTH, HEAD_DIM))
    k_norm_w = 1.0 + 0.1 * nrm(ks[8], (DEPTH, HEAD_DIM))
    w_br_conv = nrm(ks[9], (DEPTH, CONV_DIM, D_MODEL)) * CONV_DIM ** -0.5
    w_br_attn = nrm(ks[10], (DEPTH, ATTN_OUT, D_MODEL)) * ATTN_OUT ** -0.5
    w_out = nrm(ks[11], (DEPTH, D_MODEL, D_MODEL)) * D_MODEL ** -0.5
    return {"x": x, "c": c, "w_ada": w_ada, "b_ada": b_ada, "norm_w": norm_w, "w_in": w_in,
            "conv_w": conv_w, "q_norm_w": q_norm_w, "k_norm_w": k_norm_w,
            "w_br_conv": w_br_conv, "w_br_attn": w_br_attn, "w_out": w_out}


def _fwd_reference(x, c, w_ada, b_ada, norm_w, w_in, conv_w, q_norm_w, k_norm_w, w_br_conv, w_br_attn, w_out):
    b, s_len, _ = x.shape
    for l in range(DEPTH):
        mod = jax.nn.silu(c) @ w_ada[l] + b_ada[l]
        shift, scale, gate = jnp.split(mod[:, None, :], 3, axis=-1)
        h = rms_norm(x, norm_w[l]) * (1 + scale) + shift

        proj = h @ w_in[l]
        b_a, c_a, x_a, z_a, q, k, v, z_b, g_a, g_b = jnp.split(proj, IN_SPLITS, axis=-1)

        u = c_a * x_a
        conv = lax.conv_general_dilated(u, conv_w[l][:, None, :], window_strides=(1,),
                                        padding=[(CONV_WIDTH - 1, 0)],
                                        dimension_numbers=('NWC', 'WIO', 'NWC'),
                                        feature_group_count=CONV_DIM)
        y_a = b_a * conv * jax.nn.silu(z_a)

        q = rms_norm(q.reshape(b, s_len, N_GROUPS, ATTN_SLOTS, HEAD_DIM), q_norm_w[l])
        k = rms_norm(k.reshape(b, s_len, N_GROUPS, ATTN_SLOTS, HEAD_DIM), k_norm_w[l])
        v = v.reshape(b, s_len, N_GROUPS, ATTN_SLOTS, HEAD_DIM)
        outs, maxes, sums = [], [], []
        for g in range(N_GROUPS):
            o_g, m_g, s_g = dilated_attend(q[:, :, g], k[:, :, g], v[:, :, g], WINDOWS[g], DILATIONS[g])
            outs.append(o_g)
            maxes.append(m_g)
            sums.append(s_g)
        o_all = jnp.stack(outs, axis=0)
        m_all = jnp.stack(maxes, axis=0)
        s_all = jnp.stack(sums, axis=0)
        wgt = jnp.exp(m_all - jnp.max(m_all, axis=0, keepdims=True))
        attn = jnp.sum(wgt[..., None] * o_all, axis=0) / jnp.sum(wgt * s_all, axis=0)[..., None]
        y_b = attn.reshape(b, s_len, ATTN_OUT).astype(x.dtype) * jax.nn.silu(z_b)

        merged = jax.nn.sigmoid(g_a) * (y_a @ w_br_conv[l]) + jax.nn.sigmoid(g_b) * (y_b @ w_br_attn[l])
        x = x + gate * (merged @ w_out[l])
    return x


import jax as _jax
import jax.numpy as _jnp

TWIN_FORMAT = 'train_step'
FWD_PARAMS = ['x', 'c', 'w_ada', 'b_ada', 'norm_w', 'w_in', 'conv_w', 'q_norm_w', 'k_norm_w', 'w_br_conv', 'w_br_attn', 'w_out']
TWIN_WEIGHTS = ['w_ada', 'b_ada', 'norm_w', 'w_in', 'conv_w', 'q_norm_w', 'k_norm_w', 'w_br_conv', 'w_br_attn', 'w_out']
TWIN_DIFF_INPUT = 'x'
TWIN_INPUTS = ['x', 'c', 'w_ada', 'b_ada', 'norm_w', 'w_in', 'conv_w', 'q_norm_w', 'k_norm_w', 'w_br_conv', 'w_br_attn', 'w_out', 'loss_target', 'm_w_ada', 'm_b_ada', 'm_norm_w', 'm_w_in', 'm_conv_w', 'm_q_norm_w', 'm_k_norm_w', 'm_w_br_conv', 'm_w_br_attn', 'm_w_out', 'v_w_ada', 'v_b_ada', 'v_norm_w', 'v_w_in', 'v_conv_w', 'v_q_norm_w', 'v_k_norm_w', 'v_w_br_conv', 'v_w_br_attn', 'v_w_out']
TWIN_OUTPUTS = ['loss', 'grad_x', 'grad_w_ada', 'grad_b_ada', 'grad_norm_w', 'grad_w_in', 'grad_conv_w', 'grad_q_norm_w', 'grad_k_norm_w', 'grad_w_br_conv', 'grad_w_br_attn', 'grad_w_out', 'delta_w_ada', 'delta_b_ada', 'delta_norm_w', 'delta_w_in', 'delta_conv_w', 'delta_q_norm_w', 'delta_k_norm_w', 'delta_w_br_conv', 'delta_w_br_attn', 'delta_w_out', 'new_m_w_ada', 'new_m_b_ada', 'new_m_norm_w', 'new_m_w_in', 'new_m_conv_w', 'new_m_q_norm_w', 'new_m_k_norm_w', 'new_m_w_br_conv', 'new_m_w_br_attn', 'new_m_w_out', 'new_v_w_ada', 'new_v_b_ada', 'new_v_norm_w', 'new_v_w_in', 'new_v_conv_w', 'new_v_q_norm_w', 'new_v_k_norm_w', 'new_v_w_br_conv', 'new_v_w_br_attn', 'new_v_w_out']
TWIN_LEAF_KINDS = {'loss': 'loss', 'grad_x': 'grad_x', 'grad_w_ada': 'grad_w', 'grad_b_ada': 'grad_w', 'grad_norm_w': 'grad_w', 'grad_w_in': 'grad_w', 'grad_conv_w': 'grad_w', 'grad_q_norm_w': 'grad_w', 'grad_k_norm_w': 'grad_w', 'grad_w_br_conv': 'grad_w', 'grad_w_br_attn': 'grad_w', 'grad_w_out': 'grad_w', 'delta_w_ada': 'delta_w', 'delta_b_ada': 'delta_w', 'delta_norm_w': 'delta_w', 'delta_w_in': 'delta_w', 'delta_conv_w': 'delta_w', 'delta_q_norm_w': 'delta_w', 'delta_k_norm_w': 'delta_w', 'delta_w_br_conv': 'delta_w', 'delta_w_br_attn': 'delta_w', 'delta_w_out': 'delta_w', 'new_m_w_ada': 'new_m', 'new_m_b_ada': 'new_m', 'new_m_norm_w': 'new_m', 'new_m_w_in': 'new_m', 'new_m_conv_w': 'new_m', 'new_m_q_norm_w': 'new_m', 'new_m_k_norm_w': 'new_m', 'new_m_w_br_conv': 'new_m', 'new_m_w_br_attn': 'new_m', 'new_m_w_out': 'new_m', 'new_v_w_ada': 'new_v', 'new_v_b_ada': 'new_v', 'new_v_norm_w': 'new_v', 'new_v_w_in': 'new_v', 'new_v_conv_w': 'new_v', 'new_v_q_norm_w': 'new_v', 'new_v_k_norm_w': 'new_v', 'new_v_w_br_conv': 'new_v', 'new_v_w_br_attn': 'new_v', 'new_v_w_out': 'new_v'}


def _forward(args):
    return _fwd_reference(*[args[k] for k in FWD_PARAMS])


def _output_shape():
    def fwd():
        inp = _fwd_setup_inputs(0)
        return _fwd_reference(*[inp[k] for k in FWD_PARAMS])
    out = _jax.eval_shape(fwd)
    return out.shape, out.dtype

N_MICROBATCH = 1
ADAM_LR = 0.001
ADAM_B1 = 0.9
ADAM_B2 = 0.999
ADAM_EPS = 1e-08
ADAM_WD = 0.01
ADAM_STEP = 10
PER_EXAMPLE_BATCH_AXIS = {'x': 0, 'c': 0, 'loss_target': 0}
SHARED_INPUTS = []
_WEIGHT_DTYPES = {'w_ada': _jnp.float32, 'b_ada': _jnp.float32, 'norm_w': _jnp.float32, 'w_in': _jnp.float32, 'conv_w': _jnp.float32, 'q_norm_w': _jnp.float32, 'k_norm_w': _jnp.float32, 'w_br_conv': _jnp.float32, 'w_br_attn': _jnp.float32, 'w_out': _jnp.float32}
MOMENT_SCALE = {'w_ada': 1.595101e+01, 'b_ada': 4.696185e+01, 'norm_w': 1.014671e+02, 'w_in': 2.576338e+00, 'conv_w': 1.702193e+01, 'q_norm_w': 1.537820e+00, 'k_norm_w': 1.528320e+00, 'w_br_conv': 1.293029e+00, 'w_br_attn': 1.284318e-01, 'w_out': 1.199815e+00}


def _to_microbatches(a, axis):
    t = _jnp.moveaxis(a, axis, 0)
    t = t.reshape((N_MICROBATCH, t.shape[0] // N_MICROBATCH) + t.shape[1:])
    return _jnp.moveaxis(t, 1, axis + 1)


def setup_inputs(seed: int = 0) -> dict:
    inp = _fwd_setup_inputs(seed)
    key = _jax.random.fold_in(_jax.random.key(seed), 7919)
    shape, _ = _output_shape()
    out = dict(inp)
    out["loss_target"] = _jax.random.normal(_jax.random.fold_in(key, 0), shape, _jnp.float32)
    for i, name in enumerate(TWIN_WEIGHTS):
        w = inp[name].astype(_jnp.float32)
        if MOMENT_SCALE is None:
            s = _jnp.sqrt(_jnp.mean(_jnp.square(w)) + 1e-30)
        else:
            s = MOMENT_SCALE[name]
        km, kv = _jax.random.split(_jax.random.fold_in(key, i + 1))
        out[name] = w
        out["m_" + name] = s * _jax.random.normal(km, w.shape, _jnp.float32)
        out["v_" + name] = (s * s) * _jax.random.uniform(kv, w.shape, _jnp.float32, 0.5, 1.5)
    if N_MICROBATCH > 1:
        for name, axis in PER_EXAMPLE_BATCH_AXIS.items():
            out[name] = _to_microbatches(out[name], axis)
    return {'x': out['x'], 'c': out['c'], 'w_ada': out['w_ada'], 'b_ada': out['b_ada'], 'norm_w': out['norm_w'], 'w_in': out['w_in'], 'conv_w': out['conv_w'], 'q_norm_w': out['q_norm_w'], 'k_norm_w': out['k_norm_w'], 'w_br_conv': out['w_br_conv'], 'w_br_attn': out['w_br_attn'], 'w_out': out['w_out'], 'loss_target': out['loss_target'], 'm_w_ada': out['m_w_ada'], 'm_b_ada': out['m_b_ada'], 'm_norm_w': out['m_norm_w'], 'm_w_in': out['m_w_in'], 'm_conv_w': out['m_conv_w'], 'm_q_norm_w': out['m_q_norm_w'], 'm_k_norm_w': out['m_k_norm_w'], 'm_w_br_conv': out['m_w_br_conv'], 'm_w_br_attn': out['m_w_br_attn'], 'm_w_out': out['m_w_out'], 'v_w_ada': out['v_w_ada'], 'v_b_ada': out['v_b_ada'], 'v_norm_w': out['v_norm_w'], 'v_w_in': out['v_w_in'], 'v_conv_w': out['v_conv_w'], 'v_q_norm_w': out['v_q_norm_w'], 'v_k_norm_w': out['v_k_norm_w'], 'v_w_br_conv': out['v_w_br_conv'], 'v_w_br_attn': out['v_w_br_attn'], 'v_w_out': out['v_w_out']}


def _loss(weights, diff, rest, loss_target):
    with _jax.named_scope("forward"):
        args = {**rest, TWIN_DIFF_INPUT: diff, **{k: w.astype(_WEIGHT_DTYPES[k]) for k, w in weights.items()}}
        y = _forward(args)
    with _jax.named_scope("loss_head"):
        err = _jnp.square(y.astype(_jnp.float32) - loss_target)
        return 0.5 * _jnp.sum(_jnp.mean(err, axis=-1)) if err.ndim else 0.5 * err


def _adamw(w, g, m, v):
    m = ADAM_B1 * m + (1.0 - ADAM_B1) * g
    v = ADAM_B2 * v + (1.0 - ADAM_B2) * _jnp.square(g)
    m_hat = m / (1.0 - ADAM_B1 ** ADAM_STEP)
    v_hat = v / (1.0 - ADAM_B2 ** ADAM_STEP)
    delta = -ADAM_LR * (m_hat / (_jnp.sqrt(v_hat) + ADAM_EPS) + ADAM_WD * w)
    return delta, m, v


def reference(x, c, w_ada, b_ada, norm_w, w_in, conv_w, q_norm_w, k_norm_w, w_br_conv, w_br_attn, w_out, loss_target, m_w_ada, m_b_ada, m_norm_w, m_w_in, m_conv_w, m_q_norm_w, m_k_norm_w, m_w_br_conv, m_w_br_attn, m_w_out, v_w_ada, v_b_ada, v_norm_w, v_w_in, v_conv_w, v_q_norm_w, v_k_norm_w, v_w_br_conv, v_w_br_attn, v_w_out):
    given = dict(x=x, c=c, w_ada=w_ada, b_ada=b_ada, norm_w=norm_w, w_in=w_in, conv_w=conv_w, q_norm_w=q_norm_w, k_norm_w=k_norm_w, w_br_conv=w_br_conv, w_br_attn=w_br_attn, w_out=w_out, loss_target=loss_target, m_w_ada=m_w_ada, m_b_ada=m_b_ada, m_norm_w=m_norm_w, m_w_in=m_w_in, m_conv_w=m_conv_w, m_q_norm_w=m_q_norm_w, m_k_norm_w=m_k_norm_w, m_w_br_conv=m_w_br_conv, m_w_br_attn=m_w_br_attn, m_w_out=m_w_out, v_w_ada=v_w_ada, v_b_ada=v_b_ada, v_norm_w=v_norm_w, v_w_in=v_w_in, v_conv_w=v_conv_w, v_q_norm_w=v_q_norm_w, v_k_norm_w=v_k_norm_w, v_w_br_conv=v_w_br_conv, v_w_br_attn=v_w_br_attn, v_w_out=v_w_out)
    weights = {n: given[n] for n in TWIN_WEIGHTS}
    shared = {n: given[n] for n in SHARED_INPUTS}
    per_example = {n: given[n] for n in ['x', 'c']}
    grad_fn = _jax.value_and_grad(_loss, argnums=(0, 1))

    def one_microbatch(ex, loss_target):
        ex = dict(ex)
        diff = ex.pop(TWIN_DIFF_INPUT)
        return grad_fn(weights, diff, {**shared, **ex}, loss_target)

    if N_MICROBATCH == 1:
        loss, (grad_w, grad_x) = one_microbatch(per_example, given["loss_target"])
    else:
        def body(carry, xs):
            loss_sum, grad_sum = carry
            l_k, (gw_k, gx_k) = one_microbatch(xs[0], xs[1])
            with _jax.named_scope("update"):
                return (loss_sum + l_k, _jax.tree.map(_jnp.add, grad_sum, gw_k)), gx_k

        init = (_jnp.zeros((), _jnp.float32), _jax.tree.map(_jnp.zeros_like, weights))
        (loss, grad_w), grad_x = _jax.lax.scan(body, init, (per_example, given["loss_target"]))
    with _jax.named_scope("update"):
        delta_w, new_m, new_v = {}, {}, {}
        for n in TWIN_WEIGHTS:
            delta_w[n], new_m[n], new_v[n] = _adamw(weights[n], grad_w[n], given["m_" + n], given["v_" + n])
    return (loss, grad_x, *[grad_w[n] for n in TWIN_WEIGHTS], *[delta_w[n] for n in TWIN_WEIGHTS],
            *[new_m[n] for n in TWIN_WEIGHTS], *[new_v[n] for n in TWIN_WEIGHTS])
```

```python
import functools

import jax
import jax.numpy as jnp
from jax import lax
from jax.experimental import pallas as pl
from jax.experimental.pallas import tpu as pltpu

F32 = jnp.float32
BF16 = jnp.bfloat16
MESH = pl.DeviceIdType.MESH
ANY = pl.BlockSpec(memory_space=pl.ANY)

D_MODEL = 1024
HEAD_DIM = 64
N_GROUPS = 3
DILATIONS = (1, 4, 16)
ATTN_OUT = 512
EPS = 1e-6
NEG_INF = -1e30
NAT = 6656
NQKV = 4608
BA, CA, XA, ZA, ZB, GA, GB = 0, 1024, 2048, 3072, 4096, 4608, 5632
ATT_TILE = 2048
QK_SCALE = HEAD_DIM ** -0.5
V7X_VMEM_LIMIT = 56 * 1024 * 1024

ADAM_LR = 0.001
ADAM_B1 = 0.9
ADAM_B2 = 0.999
ADAM_EPS = 1e-08
ADAM_WD = 0.01
ADAM_STEP = 10


def _params(**kw):
    return pltpu.CompilerParams(vmem_limit_bytes=V7X_VMEM_LIMIT, **kw)


def _sigmoid(z):
    return 1.0 / (1.0 + jnp.exp(-z))


def _row(v, r):
    rows = lax.broadcasted_iota(jnp.int32, v.shape, 0)
    return jnp.sum(jnp.where(rows == r, v, 0.0), axis=0, keepdims=True)


def _coords():
    return lax.axis_index("x"), lax.axis_index("y"), lax.axis_index("c")


def _flip(v, bit):
    return 1 - v if bit else v


def _allgather8(blk, name):
    r, c = blk.shape

    def body(x_ref, o_ref, ssem, rsem, lsem):
        mx, my, mc = _coords()
        me = 4 * mx + 2 * my + mc
        local = pltpu.make_async_copy(x_ref, o_ref.at[me], lsem)
        local.start()
        sends = []
        for j in range(1, 8):
            peer = (_flip(mx, j & 4), _flip(my, j & 2), _flip(mc, j & 1))
            cp = pltpu.make_async_remote_copy(
                src_ref=x_ref, dst_ref=o_ref.at[me], send_sem=ssem.at[j - 1], recv_sem=rsem.at[j - 1],
                device_id=peer, device_id_type=MESH)
            cp.start()
            sends.append(cp)
        for j in range(1, 8):
            px, py, pc = _flip(mx, j & 4), _flip(my, j & 2), _flip(mc, j & 1)
            pltpu.make_async_remote_copy(
                src_ref=x_ref, dst_ref=o_ref.at[4 * px + 2 * py + pc], send_sem=ssem.at[j - 1],
                recv_sem=rsem.at[j - 1], device_id=(px, py, pc), device_id_type=MESH).wait_recv()
        for cp in sends:
            cp.wait_send()
        local.wait()

    return pl.pallas_call(
        body, name=name,
        out_shape=jax.ShapeDtypeStruct((8, r, c), blk.dtype),
        in_specs=[pl.BlockSpec(memory_space=pltpu.VMEM)],
        out_specs=pl.BlockSpec(memory_space=pltpu.VMEM),
        scratch_shapes=[pltpu.SemaphoreType.DMA((7,)), pltpu.SemaphoreType.DMA((7,)), pltpu.SemaphoreType.DMA(())],
    )(blk)


def _allgather_chips(blk, name):
    r, c = blk.shape

    def body(x_ref, o_ref, ssem, rsem, lsem):
        mx, my, mc = _coords()
        me = 2 * mx + my
        local = pltpu.make_async_copy(x_ref, o_ref.at[me], lsem)
        local.start()
        sends = []
        for j in range(1, 4):
            peer = (_flip(mx, j & 2), _flip(my, j & 1), mc)
            cp = pltpu.make_async_remote_copy(
                src_ref=x_ref, dst_ref=o_ref.at[me], send_sem=ssem.at[j - 1], recv_sem=rsem.at[j - 1],
                device_id=peer, device_id_type=MESH)
            cp.start()
            sends.append(cp)
        for j in range(1, 4):
            px, py = _flip(mx, j & 2), _flip(my, j & 1)
            pltpu.make_async_remote_copy(
                src_ref=x_ref, dst_ref=o_ref.at[2 * px + py], send_sem=ssem.at[j - 1],
                recv_sem=rsem.at[j - 1], device_id=(px, py, mc), device_id_type=MESH).wait_recv()
        for cp in sends:
            cp.wait_send()
        local.wait()

    return pl.pallas_call(
        body, name=name,
        out_shape=jax.ShapeDtypeStruct((4, r, c), blk.dtype),
        in_specs=[pl.BlockSpec(memory_space=pltpu.VMEM)],
        out_specs=pl.BlockSpec(memory_space=pltpu.VMEM),
        scratch_shapes=[pltpu.SemaphoreType.DMA((3,)), pltpu.SemaphoreType.DMA((3,)), pltpu.SemaphoreType.DMA(())],
    )(blk)


def _weights_allgather(shards):
    n = len(shards)

    def body(*refs):
        ins, outs = refs[:n], refs[n:2 * n]
        ssem, rsem, lsem = refs[2 * n:]
        mx, my, mc = _coords()
        me = 2 * mx + my
        sib = (mx, my, 1 - mc)
        chips = [(_flip(mx, j & 2), _flip(my, j & 1)) for j in range(1, 4)]
        started = []
        for t in range(n):
            cp = pltpu.make_async_copy(ins[t], outs[t].at[me], lsem.at[t])
            cp.start()
            started.append(cp)
        sends = []
        for t in range(n):
            for j, (px, py) in enumerate(chips):
                cp = pltpu.make_async_remote_copy(
                    src_ref=ins[t].at[mc], dst_ref=outs[t].at[me, mc], send_sem=ssem.at[t, j],
                    recv_sem=rsem.at[t, j], device_id=(px, py, mc), device_id_type=MESH)
                cp.start()
                sends.append(cp)
        for t in range(n):
            for j, (px, py) in enumerate(chips):
                src = 2 * px + py
                pltpu.make_async_remote_copy(
                    src_ref=ins[t].at[mc], dst_ref=outs[t].at[src, mc], send_sem=ssem.at[t, j],
                    recv_sem=rsem.at[t, j], device_id=(px, py, mc), device_id_type=MESH).wait_recv()
                fwd = pltpu.make_async_remote_copy(
                    src_ref=outs[t].at[src, mc], dst_ref=outs[t].at[src, mc], send_sem=ssem.at[t, 3 + j],
                    recv_sem=rsem.at[t, 3 + j], device_id=sib, device_id_type=MESH)
                fwd.start()
                sends.append(fwd)
        for t in range(n):
            for j, (px, py) in enumerate(chips):
                src = 2 * px + py
                pltpu.make_async_remote_copy(
                    src_ref=outs[t].at[src, 1 - mc], dst_ref=outs[t].at[src, 1 - mc], send_sem=ssem.at[t, 3 + j],
                    recv_sem=rsem.at[t, 3 + j], device_id=sib, device_id_type=MESH).wait_recv()
        for cp in sends:
            cp.wait_send()
        for cp in started:
            cp.wait()

    return pl.pallas_call(
        body, name="weights_allgather",
        out_shape=[jax.ShapeDtypeStruct((4,) + s.shape, s.dtype) for s in shards],
        in_specs=[ANY] * n, out_specs=[ANY] * n,
        scratch_shapes=[pltpu.SemaphoreType.DMA((n, 6)), pltpu.SemaphoreType.DMA((n, 6)),
                        pltpu.SemaphoreType.DMA((n,))],
    )(*shards)


def _pair_exchange(grads):
    n = len(grads)

    def body(*refs):
        ins, outs = refs[:n], refs[n:2 * n]
        ssem, rsem = refs[2 * n:]
        mx, my, mc = _coords()
        sib = (mx, my, 1 - mc)
        sends = []
        for t in range(n):
            cp = pltpu.make_async_remote_copy(
                src_ref=ins[t].at[1 - mc], dst_ref=outs[t], send_sem=ssem.at[t], recv_sem=rsem.at[t],
                device_id=sib, device_id_type=MESH)
            cp.start()
            sends.append(cp)
        for cp in sends:
            cp.wait_recv()
        for cp in sends:
            cp.wait_send()

    return pl.pallas_call(
        body, name="grad_pair_exchange",
        out_shape=[jax.ShapeDtypeStruct(g.shape[1:], g.dtype) for g in grads],
        in_specs=[ANY] * n, out_specs=[ANY] * n,
        scratch_shapes=[pltpu.SemaphoreType.DMA((n,)), pltpu.SemaphoreType.DMA((n,))],
    )(*grads)


def _chip_scatter(parts):
    n = len(parts)

    def body(*refs):
        ins, outs = refs[:n], refs[n:2 * n]
        ssem, rsem, lsem = refs[2 * n:]
        mx, my, mc = _coords()
        me = 2 * mx + my
        chips = [(_flip(mx, j & 2), _flip(my, j & 1)) for j in range(1, 4)]
        started = []
        for t in range(n):
            cp = pltpu.make_async_copy(ins[t].at[me], outs[t].at[me], lsem.at[t])
            cp.start()
            started.append(cp)
        sends = []
        for t in range(n):
            for j, (px, py) in enumerate(chips):
                cp = pltpu.make_async_remote_copy(
                    src_ref=ins[t].at[2 * px + py], dst_ref=outs[t].at[me], send_sem=ssem.at[t, j],
                    recv_sem=rsem.at[t, j], device_id=(px, py, mc), device_id_type=MESH)
                cp.start()
                sends.append(cp)
        for t in range(n):
            for j, (px, py) in enumerate(chips):
                pltpu.make_async_remote_copy(
                    src_ref=ins[t].at[me], dst_ref=outs[t].at[2 * px + py], send_sem=ssem.at[t, j],
                    recv_sem=rsem.at[t, j], device_id=(px, py, mc), device_id_type=MESH).wait_recv()
        for cp in sends:
            cp.wait_send()
        for cp in started:
            cp.wait()

    return pl.pallas_call(
        body, name="grad_chip_scatter",
        out_shape=[jax.ShapeDtypeStruct(p.shape, p.dtype) for p in parts],
        in_specs=[ANY] * n, out_specs=[ANY] * n,
        scratch_shapes=[pltpu.SemaphoreType.DMA((n, 3)), pltpu.SemaphoreType.DMA((n, 3)),
                        pltpu.SemaphoreType.DMA((n,))],
    )(*parts)


def _half_exchange(halves):
    n = len(halves)

    def body(*refs):
        ins, outs = refs[:n], refs[n:2 * n]
        ssem, rsem, lsem = refs[2 * n:]
        mx, my, mc = _coords()
        sib = (mx, my, 1 - mc)
        started, sends = [], []
        for t in range(n):
            cp = pltpu.make_async_copy(ins[t], outs[t].at[mc], lsem.at[t])
            cp.start()
            started.append(cp)
            rc = pltpu.make_async_remote_copy(
                src_ref=ins[t], dst_ref=outs[t].at[mc], send_sem=ssem.at[t], recv_sem=rsem.at[t],
                device_id=sib, device_id_type=MESH)
            rc.start()
            sends.append(rc)
        for t in range(n):
            pltpu.make_async_remote_copy(
                src_ref=ins[t], dst_ref=outs[t].at[1 - mc], send_sem=ssem.at[t], recv_sem=rsem.at[t],
                device_id=sib, device_id_type=MESH).wait_recv()
        for cp in sends:
            cp.wait_send()
        for cp in started:
            cp.wait()

    return pl.pallas_call(
        body, name="grad_half_exchange",
        out_shape=[jax.ShapeDtypeStruct((2,) + h.shape, h.dtype) for h in halves],
        in_specs=[ANY] * n, out_specs=[ANY] * n,
        scratch_shapes=[pltpu.SemaphoreType.DMA((n,)), pltpu.SemaphoreType.DMA((n,)),
                        pltpu.SemaphoreType.DMA((n,))],
    )(*halves)


def _row_tile(rows, cols, bytes_per_row_elem=4, budget=2 * 1024 * 1024):
    t = rows
    while t % 2 == 0 and t > 16 and t * cols * bytes_per_row_elem > budget:
        t //= 2
    return t


def _pair_add(core, g, r):
    _, _, rh, cc = g.shape
    tr = _row_tile(rh, cc)

    def body(core_ref, g_ref, r_ref, o_ref):
        o_ref[...] = (g_ref[...].astype(F32) + r_ref[...].astype(F32)).astype(o_ref.dtype)

    return pl.pallas_call(
        body, name="grad_pair_add",
        grid_spec=pltpu.PrefetchScalarGridSpec(
            num_scalar_prefetch=1, grid=(4, rh // tr),
            in_specs=[pl.BlockSpec((None, None, tr, cc), lambda k, i, c: (c[0], k, i, 0)),
                      pl.BlockSpec((None, tr, cc), lambda k, i, c: (k, i, 0))],
            out_specs=pl.BlockSpec((None, tr, cc), lambda k, i, c: (k, i, 0))),
        out_shape=jax.ShapeDtypeStruct(r.shape, BF16),
        compiler_params=_params(),
    )(core, g, r)


def _sum4(q):
    _, rh, cc = q.shape
    tr = _row_tile(rh, cc)

    def body(q_ref, o_ref):
        acc = q_ref[0].astype(F32)
        for k in range(1, 4):
            acc = acc + q_ref[k].astype(F32)
        o_ref[...] = acc

    return pl.pallas_call(
        body, name="grad_sum4", grid=(rh // tr,),
        in_specs=[pl.BlockSpec((4, tr, cc), lambda i: (0, i, 0))],
        out_specs=pl.BlockSpec((tr, cc), lambda i: (i, 0)),
        out_shape=jax.ShapeDtypeStruct((rh, cc), F32),
        compiler_params=_params(),
    )(q)


def _adamw_math(w, g, m, v):
    m = ADAM_B1 * m + (1.0 - ADAM_B1) * g
    v = ADAM_B2 * v + (1.0 - ADAM_B2) * (g * g)
    m_hat = m / (1.0 - ADAM_B1 ** ADAM_STEP)
    v_hat = v / (1.0 - ADAM_B2 ** ADAM_STEP)
    delta = -ADAM_LR * (m_hat / (jnp.sqrt(v_hat) + ADAM_EPS) + ADAM_WD * w)
    return delta, m, v


def _adamw(w, g, m, v, name):
    rows, cols = w.shape
    tr = _row_tile(rows, cols, budget=1024 * 1024) if rows % 8 == 0 else rows

    def body(w_ref, g_ref, m_ref, v_ref, d_ref, mo_ref, vo_ref):
        d, mn, vn = _adamw_math(w_ref[...], g_ref[...], m_ref[...], v_ref[...])
        d_ref[...] = d
        mo_ref[...] = mn
        vo_ref[...] = vn

    spec = pl.BlockSpec((tr, cols), lambda i: (i, 0))
    sd = jax.ShapeDtypeStruct((rows, cols), F32)
    return pl.pallas_call(
        body, name=name, grid=(rows // tr,), in_specs=[spec] * 4, out_specs=[spec] * 3,
        out_shape=[sd, sd, sd], compiler_params=_params(),
    )(w, g, m, v)


def _mod_part(c_all, w_ada, b_shard):
    cols = w_ada.shape[1]

    def body(c_ref, w_ref, b_ref, o_ref, sc_ref):
        cv = c_ref[...]
        sc = cv * _sigmoid(cv)
        sc_ref[...] = sc
        o_ref[...] = jnp.dot(sc, w_ref[...], preferred_element_type=F32,
                             precision=lax.Precision.HIGHEST) + b_ref[...]

    return pl.pallas_call(
        body, name="adaln_mod",
        out_shape=[jax.ShapeDtypeStruct((8, cols), F32), jax.ShapeDtypeStruct(c_all.shape, F32)],
        in_specs=[pl.BlockSpec(memory_space=pltpu.VMEM)] * 3,
        out_specs=[pl.BlockSpec(memory_space=pltpu.VMEM)] * 2, compiler_params=_params(),
    )(c_all, w_ada, b_shard)


def _inproj_nat(x, mod, norm_w, w, *, tm=1024, tn=1664):
    S, D = x.shape
    N = w.shape[1]
    chunk = 256

    def body(x_ref, mod_ref, nw_ref, w_ref, proj_ref, h_ref, ht_ref):
        @pl.when(pl.program_id(1) == 0)
        def _():
            shift = mod_ref[:, 0:D]
            scale1 = 1.0 + mod_ref[:, D:2 * D]
            for r in range(tm // chunk):
                xv = x_ref[pl.ds(r * chunk, chunk), :]
                ms = jnp.mean(xv * xv, axis=-1, keepdims=True)
                h = (xv * lax.rsqrt(ms + EPS) * nw_ref[...]) * scale1 + shift
                h_ref[pl.ds(r * chunk, chunk), :] = h.astype(BF16)
                ht_ref[:, pl.ds(r * chunk, chunk)] = h.T.astype(BF16)
        proj_ref[...] = jnp.dot(h_ref[...], w_ref[...], preferred_element_type=F32).astype(BF16)

    return pl.pallas_call(
        body, name="inproj_nat", grid=(S // tm, N // tn),
        in_specs=[pl.BlockSpec((tm, D), lambda i, j: (i, 0)),
                  pl.BlockSpec((1, 3 * D), lambda i, j: (0, 0)),
                  pl.BlockSpec((1, D), lambda i, j: (0, 0)),
                  pl.BlockSpec((D, tn), lambda i, j: (0, j))],
        out_specs=[pl.BlockSpec((tm, tn), lambda i, j: (i, j)),
                   pl.BlockSpec((tm, D), lambda i, j: (i, 0)),
                   pl.BlockSpec((D, tm), lambda i, j: (0, i))],
        out_shape=[jax.ShapeDtypeStruct((S, N), BF16), jax.ShapeDtypeStruct((S, D), BF16),
                   jax.ShapeDtypeStruct((D, S), BF16)],
        compiler_params=_params(),
    )(x, mod, norm_w, w)


def _matmul(a, b, *, tm, tn, name):
    M, K = a.shape
    N = b.shape[1]

    def body(a_ref, b_ref, o_ref):
        o_ref[...] = jnp.dot(a_ref[...], b_ref[...], preferred_element_type=F32).astype(o_ref.dtype)

    return pl.pallas_call(
        body, name=name, grid=(M // tm, N // tn),
        in_specs=[pl.BlockSpec((tm, K), lambda i, j: (i, 0)), pl.BlockSpec((K, tn), lambda i, j: (0, j))],
        out_specs=pl.BlockSpec((tm, tn), lambda i, j: (i, j)),
        out_shape=jax.ShapeDtypeStruct((M, N), BF16), compiler_params=_params(),
    )(a, b)


def _matmul_acc(a, b, *, tn, tk, name):
    M, K = a.shape
    N = b.shape[1]
    nk = K // tk

    def body(a_ref, b_ref, o_ref, acc_ref):
        k = pl.program_id(1)

        @pl.when(k == 0)
        def _():
            acc_ref[...] = jnp.zeros_like(acc_ref)

        acc_ref[...] += jnp.dot(a_ref[...], b_ref[...], preferred_element_type=F32)

        @pl.when(k == nk - 1)
        def _():
            o_ref[...] = acc_ref[...].astype(o_ref.dtype)

    return pl.pallas_call(
        body, name=name, grid=(N // tn, nk),
        in_specs=[pl.BlockSpec((M, tk), lambda j, k: (0, k)), pl.BlockSpec((tk, tn), lambda j, k: (k, j))],
        out_specs=pl.BlockSpec((M, tn), lambda j, k: (0, j)),
        out_shape=jax.ShapeDtypeStruct((M, N), BF16),
        scratch_shapes=[pltpu.VMEM((M, tn), F32)], compiler_params=_params(),
    )(a, b)


def _head_norm(xb, w, ebd):
    x = xb.astype(F32)
    ss = jnp.dot((x * x).astype(BF16), ebd, preferred_element_type=F32)
    return x * lax.rsqrt(ss * (1.0 / HEAD_DIM) + EPS) * w


def _window_mask(no_prev):
    qi = lax.broadcasted_iota(jnp.int32, (128, 256), 0)
    kc = lax.broadcasted_iota(jnp.int32, (128, 256), 1)
    ok = (kc >= qi) & (kc <= qi + 128)
    if no_prev is not None:
        ok = ok & ((kc >= 128) | jnp.logical_not(no_prev))
    return ok


def _lane_masks():
    lane = lax.broadcasted_iota(jnp.int32, (1, 128), 1)
    return [(lane < HEAD_DIM).astype(F32), (lane >= HEAD_DIM).astype(F32)]


def _head_col(v, mh):
    return jnp.sum(v * mh, axis=1, keepdims=True) * (1.0 / HEAD_DIM)


def _attn_fwd(projq, qnw2, knw2, ebd, g):
    S = projq.shape[0]
    d = DILATIONS[g]
    T = ATT_TILE
    nt = S // T
    nb = T // (128 * d)

    def body(cur_ref, prev_ref, qw_ref, kw_ref, ebd_ref, o_ref, lse_ref, q32, k32, v32):
        t = pl.program_id(0)
        e = ebd_ref[...]
        q32[...] = _head_norm(cur_ref[:, 0:128], qw_ref[...], e) * QK_SCALE
        k32[pl.ds(0, T), :] = _head_norm(prev_ref[:, 128:256], kw_ref[...], e)
        k32[pl.ds(T, T), :] = _head_norm(cur_ref[:, 128:256], kw_ref[...], e)
        v32[pl.ds(0, T), :] = prev_ref[:, 256:384].astype(F32)
        v32[pl.ds(T, T), :] = cur_ref[:, 256:384].astype(F32)
        m0, m1 = _lane_masks()
        first = t == 0
        for r in range(d):
            for bb in range(nb):
                q0 = r + bb * 128 * d
                k0 = T + r + (bb - 1) * 128 * d
                qs = q32[pl.ds(q0, 128, stride=d), :]
                kb = k32[pl.ds(k0, 256, stride=d), :].astype(BF16)
                vb = v32[pl.ds(k0, 256, stride=d), :].astype(BF16)
                ok = _window_mask(first if bb == 0 else None)
                o_t = None
                l_t = None
                for mh in (m0, m1):
                    s = lax.dot_general((qs * mh).astype(BF16), kb, (((1,), (1,)), ((), ())),
                                        preferred_element_type=F32)
                    s = jnp.where(ok, s, NEG_INF)
                    mx = jnp.max(s, axis=-1, keepdims=True)
                    p = jnp.exp(s - mx)
                    l = jnp.sum(p, axis=-1, keepdims=True)
                    o = jnp.dot(p.astype(BF16), vb, preferred_element_type=F32) / l
                    lse = mx + jnp.log(l)
                    o_t = o * mh if o_t is None else o_t + o * mh
                    l_t = lse * mh if l_t is None else l_t + lse * mh
                o_ref[pl.ds(q0, 128, stride=d), :] = o_t
                lse_ref[pl.ds(q0, 128, stride=d), :] = l_t

    return pl.pallas_call(
        body, name=f"attn_fwd_g{g}", grid=(nt, 4),
        in_specs=[pl.BlockSpec((T, 384), lambda t, p: (t, g * 4 + p)),
                  pl.BlockSpec((T, 384), lambda t, p: (jnp.maximum(t - 1, 0), g * 4 + p)),
                  pl.BlockSpec((1, 128), lambda t, p: (0, 0)),
                  pl.BlockSpec((1, 128), lambda t, p: (0, 0)),
                  pl.BlockSpec((128, 128), lambda t, p: (0, 0))],
        out_specs=[pl.BlockSpec((T, 128), lambda t, p: (t, p)), pl.BlockSpec((T, 128), lambda t, p: (t, p))],
        out_shape=[jax.ShapeDtypeStruct((S, ATTN_OUT), F32), jax.ShapeDtypeStruct((S, ATTN_OUT), F32)],
        scratch_shapes=[pltpu.VMEM((T, 128), F32), pltpu.VMEM((2 * T, 128), F32), pltpu.VMEM((2 * T, 128), F32)],
        compiler_params=_params(),
    )(projq, projq, qnw2, knw2, ebd)


def _attn_bwd(projq, dprojq, d_o, lse, delta, qnw2, knw2, ebd, g):
    S = projq.shape[0]
    d = DILATIONS[g]
    T = ATT_TILE
    nt = S // T
    nb = T // (128 * d)

    def norm_bwd(raw_b, dy, w, e):
        x = raw_b.astype(F32)
        ss = jnp.dot((x * x).astype(BF16), e, preferred_element_type=F32)
        rstd = lax.rsqrt(ss * (1.0 / HEAD_DIM) + EPS)
        xh = x * rstd
        gw = jnp.sum(dy * xh, axis=0, keepdims=True)
        dxh = dy * w
        mean = jnp.dot((dxh * xh).astype(BF16), e, preferred_element_type=F32) * (1.0 / HEAD_DIM)
        return rstd * (dxh - xh * mean), gw

    def body(cur_ref, prev_ref, do_ref, lse_ref, dl_ref, qw_ref, kw_ref, ebd_ref, dp_in_ref,
             out_ref, gqk_ref, q32, k32, v32, do32, dq_cur, dq_prev, dk_acc, dv_acc):
        i = pl.program_id(1)
        e = ebd_ref[...]
        m0, m1 = _lane_masks()

        @pl.when(i == 0)
        def _():
            dk_acc[...] = jnp.zeros_like(dk_acc)
            dv_acc[...] = jnp.zeros_like(dv_acc)
            dq_prev[...] = jnp.zeros_like(dq_prev)
            gqk_ref[...] = jnp.zeros_like(gqk_ref)

        @pl.when(i < nt)
        def _():
            q32[...] = _head_norm(cur_ref[:, 0:128], qw_ref[...], e) * QK_SCALE
            k32[pl.ds(0, T), :] = _head_norm(prev_ref[:, 128:256], kw_ref[...], e)
            k32[pl.ds(T, T), :] = _head_norm(cur_ref[:, 128:256], kw_ref[...], e)
            v32[pl.ds(0, T), :] = prev_ref[:, 256:384].astype(F32)
            v32[pl.ds(T, T), :] = cur_ref[:, 256:384].astype(F32)
            do32[...] = do_ref[...].astype(F32)
            first = i == 0
            for r in range(d):
                for bb in range(nb):
                    q0 = r + bb * 128 * d
                    k0 = T + r + (bb - 1) * 128 * d
                    qs = q32[pl.ds(q0, 128, stride=d), :]
                    dos = do32[pl.ds(q0, 128, stride=d), :]
                    ls = lse_ref[pl.ds(q0, 128, stride=d), :]
                    dls = dl_ref[pl.ds(q0, 128, stride=d), :]
                    kb = k32[pl.ds(k0, 256, stride=d), :].astype(BF16)
                    vb = v32[pl.ds(k0, 256, stride=d), :].astype(BF16)
                    ok = _window_mask(first if bb == 0 else None)
                    dq_t = None
                    dk_t = None
                    dv_t = None
                    for mh in (m0, m1):
                        qh = (qs * mh).astype(BF16)
                        doh = (dos * mh).astype(BF16)
                        s = lax.dot_general(qh, kb, (((1,), (1,)), ((), ())), preferred_element_type=F32)
                        s = jnp.where(ok, s, NEG_INF)
                        p = jnp.exp(s - _head_col(ls, mh))
                        dp = lax.dot_general(doh, vb, (((1,), (1,)), ((), ())), preferred_element_type=F32)
                        ds = (p * (dp - _head_col(dls, mh))).astype(BF16)
                        dq = jnp.dot(ds, kb, preferred_element_type=F32) * mh
                        dk = lax.dot_general(ds, qh, (((0,), (0,)), ((), ())), preferred_element_type=F32)
                        dv = lax.dot_general(p.astype(BF16), doh, (((0,), (0,)), ((), ())),
                                             preferred_element_type=F32)
                        dq_t = dq if dq_t is None else dq_t + dq
                        dk_t = dk if dk_t is None else dk_t + dk
                        dv_t = dv if dv_t is None else dv_t + dv
                    dq_cur[pl.ds(q0, 128, stride=d), :] = dq_t
                    dk_acc[pl.ds(k0, 256, stride=d), :] = dk_acc[pl.ds(k0, 256, stride=d), :] + dk_t
                    dv_acc[pl.ds(k0, 256, stride=d), :] = dv_acc[pl.ds(k0, 256, stride=d), :] + dv_t

        @pl.when(i >= 1)
        def _():
            dq_raw, gq = norm_bwd(prev_ref[:, 0:128], dq_prev[...] * QK_SCALE, qw_ref[...], e)
            dk_raw, gk = norm_bwd(prev_ref[:, 128:256], dk_acc[pl.ds(0, T), :], kw_ref[...], e)
            out_ref[:, 0:128] = dq_raw.astype(BF16)
            out_ref[:, 128:256] = dk_raw.astype(BF16)
            out_ref[:, 256:384] = dv_acc[pl.ds(0, T), :].astype(BF16)
            gqk_ref[0:1, :] += gq
            gqk_ref[1:2, :] += gk

        dq_prev[...] = dq_cur[...]
        dk_acc[pl.ds(0, T), :] = dk_acc[pl.ds(T, T), :]
        dv_acc[pl.ds(0, T), :] = dv_acc[pl.ds(T, T), :]
        dk_acc[pl.ds(T, T), :] = jnp.zeros((T, 128), F32)
        dv_acc[pl.ds(T, T), :] = jnp.zeros((T, 128), F32)

    last = nt - 1
    qtile = lambda p, i: (jnp.minimum(i, last), p)
    return pl.pallas_call(
        body, name=f"attn_bwd_g{g}", grid=(4, nt + 1),
        in_specs=[pl.BlockSpec((T, 384), lambda p, i: (jnp.minimum(i, last), g * 4 + p)),
                  pl.BlockSpec((T, 384), lambda p, i: (jnp.maximum(i - 1, 0), g * 4 + p)),
                  pl.BlockSpec((T, 128), qtile), pl.BlockSpec((T, 128), qtile), pl.BlockSpec((T, 128), qtile),
                  pl.BlockSpec((1, 128), lambda p, i: (0, 0)),
                  pl.BlockSpec((1, 128), lambda p, i: (0, 0)),
                  pl.BlockSpec((128, 128), lambda p, i: (0, 0)),
                  ANY],
        out_specs=[pl.BlockSpec((T, 384), lambda p, i: (jnp.maximum(i - 1, 0), g * 4 + p)),
                   pl.BlockSpec((None, 8, 128), lambda p, i: (p, 0, 0))],
        out_shape=[jax.ShapeDtypeStruct(dprojq.shape, BF16), jax.ShapeDtypeStruct((4, 8, 128), F32)],
        scratch_shapes=[pltpu.VMEM((T, 128), F32), pltpu.VMEM((2 * T, 128), F32), pltpu.VMEM((2 * T, 128), F32),
                        pltpu.VMEM((T, 128), F32), pltpu.VMEM((T, 128), F32), pltpu.VMEM((T, 128), F32),
                        pltpu.VMEM((2 * T, 128), F32), pltpu.VMEM((2 * T, 128), F32)],
        input_output_aliases={8: 0},
        compiler_params=_params(),
    )(projq, projq, d_o, lse, delta, qnw2, knw2, ebd, dprojq)


def _mid(projn, o_g, lse_g, x, tgt, mod, convw8, wc, wct, wa, wat, wo, wot, ebd, *, tm=128):
    S, D = x.shape
    nt = S // tm
    hb = tm // 16

    def body(pn_ref, hc_ref, hx_ref, o0_ref, o1_ref, o2_ref, l0_ref, l1_ref, l2_ref, x_ref, t_ref, mod_ref,
             cw_ref, ebd_ref, wc_hbm, wct_hbm, wa_hbm, wat_hbm, wo_hbm, wot_hbm,
             dpn_ref, do_ref, lse_ref, dl_ref, dout_ref, vacc_ref, gwo_hbm, gwc_hbm, gwa_hbm,
             wc_s, wct_s, wa_s, wat_s, wo_s, wot_s, gwo_s, gwc_s, gwa_s, carry_s):
        i = pl.program_id(0)
        ti = nt - 1 - i

        @pl.when(i == 0)
        def _():
            for src, dst in ((wc_hbm, wc_s), (wct_hbm, wct_s), (wa_hbm, wa_s), (wat_hbm, wat_s),
                             (wo_hbm, wo_s), (wot_hbm, wot_s)):
                pltpu.sync_copy(src, dst)
            gwo_s[...] = jnp.zeros_like(gwo_s)
            gwc_s[...] = jnp.zeros_like(gwc_s)
            gwa_s[...] = jnp.zeros_like(gwa_s)
            carry_s[...] = jnp.zeros_like(carry_s)
            vacc_ref[...] = jnp.zeros_like(vacc_ref)

        rows = lax.broadcasted_iota(jnp.int32, (tm, D), 0)
        w0, w1, w2 = cw_ref[0:1, :], cw_ref[1:2, :], cw_ref[2:3, :]
        gate = mod_ref[:, 2 * D:3 * D]

        b_a = pn_ref[:, BA:BA + D].astype(F32)
        c_a = pn_ref[:, CA:CA + D].astype(F32)
        x_a = pn_ref[:, XA:XA + D].astype(F32)
        z_a = pn_ref[:, ZA:ZA + D].astype(F32)
        u = c_a * x_a
        has_prev = (ti > 0).astype(F32)
        uh = hc_ref[...].astype(F32) * hx_ref[...].astype(F32) * has_prev
        h14, h15 = _row(uh, 14), _row(uh, 15)
        u_m1 = jnp.where(rows == 0, h15, pltpu.roll(u, 1, 0))
        u_m2 = jnp.where(rows == 0, h14, jnp.where(rows == 1, h15, pltpu.roll(u, 2, 0)))
        conv = w0 * u_m2 + w1 * u_m1 + w2 * u
        sg_a = _sigmoid(z_a)
        sz_a = z_a * sg_a
        y_a = b_a * conv * sz_a
        y_ab = y_a.astype(BF16)
        pa = jnp.dot(y_ab, wc_s[...], preferred_element_type=F32)

        l0, l1, l2 = l0_ref[...], l1_ref[...], l2_ref[...]
        mxl = jnp.maximum(jnp.maximum(l0, l1), l2)
        e0, e1, e2 = jnp.exp(l0 - mxl), jnp.exp(l1 - mxl), jnp.exp(l2 - mxl)
        den = e0 + e1 + e2
        attn = (e0 * o0_ref[...] + e1 * o1_ref[...] + e2 * o2_ref[...]) / den
        lse_ref[...] = mxl + jnp.log(den)
        z_b = pn_ref[:, ZB:ZB + ATTN_OUT].astype(F32)
        sg_b = _sigmoid(z_b)
        sz_b = z_b * sg_b
        y_b = attn * sz_b
        y_bb = y_b.astype(BF16)
        pb = jnp.dot(y_bb, wa_s[...], preferred_element_type=F32)

        s_a = _sigmoid(pn_ref[:, GA:GA + D].astype(F32))
        s_b = _sigmoid(pn_ref[:, GB:GB + D].astype(F32))
        merged = s_a * pa + s_b * pb
        merged_b = merged.astype(BF16)
        mo = jnp.dot(merged_b, wo_s[...], preferred_element_type=F32)
        diff = x_ref[...] + gate * mo - t_ref[...]
        dout = diff * (1.0 / D)
        dout_ref[...] = dout
        vacc_ref[4:5, :] += jnp.sum(diff * diff, axis=0, keepdims=True)
        vacc_ref[0:1, :] += jnp.sum(dout * mo, axis=0, keepdims=True)

        d_mo = (dout * gate).astype(BF16)
        gwo_s[...] += lax.dot_general(merged_b, d_mo, (((0,), (0,)), ((), ())), preferred_element_type=F32)
        dmerged = jnp.dot(d_mo, wot_s[...], preferred_element_type=F32)
        dpa = (dmerged * s_a).astype(BF16)
        dpb = (dmerged * s_b).astype(BF16)
        dpn_ref[:, GA:GA + D] = (dmerged * pa * s_a * (1.0 - s_a)).astype(BF16)
        dpn_ref[:, GB:GB + D] = (dmerged * pb * s_b * (1.0 - s_b)).astype(BF16)
        gwc_s[...] += lax.dot_general(y_ab, dpa, (((0,), (0,)), ((), ())), preferred_element_type=F32)
        gwa_s[...] += lax.dot_general(y_bb, dpb, (((0,), (0,)), ((), ())), preferred_element_type=F32)
        dy_a = jnp.dot(dpa, wct_s[...], preferred_element_type=F32)
        dy_b = jnp.dot(dpb, wat_s[...], preferred_element_type=F32)

        dattn = dy_b * sz_b
        dpn_ref[:, ZB:ZB + ATTN_OUT] = (dy_b * attn * sg_b * (1.0 + z_b * (1.0 - sg_b))).astype(BF16)
        do_ref[...] = dattn.astype(BF16)
        dl_ref[...] = jnp.dot((dattn * attn).astype(BF16), ebd_ref[...], preferred_element_type=F32)

        dpn_ref[:, BA:BA + D] = (dy_a * conv * sz_a).astype(BF16)
        dpn_ref[:, ZA:ZA + D] = (dy_a * b_a * conv * sg_a * (1.0 + z_a * (1.0 - sg_a))).astype(BF16)
        dconv = dy_a * b_a * sz_a
        vacc_ref[1:2, :] += jnp.sum(dconv * u_m2, axis=0, keepdims=True)
        vacc_ref[2:3, :] += jnp.sum(dconv * u_m1, axis=0, keepdims=True)
        vacc_ref[3:4, :] += jnp.sum(dconv * u, axis=0, keepdims=True)
        nxt = carry_s[...]
        n0, n1 = _row(nxt, 0), _row(nxt, 1)
        dc_p1 = jnp.where(rows == tm - 1, n0, pltpu.roll(dconv, tm - 1, 0))
        dc_p2 = jnp.where(rows == tm - 1, n1, jnp.where(rows == tm - 2, n0, pltpu.roll(dconv, tm - 2, 0)))
        du = w2 * dconv + w1 * dc_p1 + w0 * dc_p2
        carry_s[...] = dconv[0:8, :]
        dpn_ref[:, CA:CA + D] = (du * x_a).astype(BF16)
        dpn_ref[:, XA:XA + D] = (du * c_a).astype(BF16)

        @pl.when(i == nt - 1)
        def _():
            pltpu.sync_copy(gwo_s, gwo_hbm)
            pltpu.sync_copy(gwc_s, gwc_hbm)
            pltpu.sync_copy(gwa_s, gwa_hbm)

    rev = lambda i: (nt - 1 - i, 0)
    halo = lambda col: pl.BlockSpec((16, D), lambda i: (jnp.maximum((nt - 1 - i) * hb - 1, 0), col))
    tile512 = pl.BlockSpec((tm, ATTN_OUT), rev)
    tileD = pl.BlockSpec((tm, D), rev)
    const = lambda shape: pl.BlockSpec(shape, lambda i: (0, 0))
    return pl.pallas_call(
        body, name="mid_fwd_bwd", grid=(nt,),
        in_specs=[pl.BlockSpec((tm, NAT), rev), halo(CA // D), halo(XA // D)] + [tile512] * 6
                 + [tileD, tileD, const((1, 3 * D)), const((8, D)), const((ATTN_OUT, ATTN_OUT))] + [ANY] * 6,
        out_specs=[pl.BlockSpec((tm, NAT), rev), tile512, tile512, tile512, tileD, const((8, D)), ANY, ANY, ANY],
        out_shape=[jax.ShapeDtypeStruct((S, NAT), BF16), jax.ShapeDtypeStruct((S, ATTN_OUT), BF16),
                   jax.ShapeDtypeStruct((S, ATTN_OUT), F32), jax.ShapeDtypeStruct((S, ATTN_OUT), F32),
                   jax.ShapeDtypeStruct((S, D), F32), jax.ShapeDtypeStruct((8, D), F32),
                   jax.ShapeDtypeStruct((D, D), F32), jax.ShapeDtypeStruct((D, D), F32),
                   jax.ShapeDtypeStruct((ATTN_OUT, D), F32)],
        scratch_shapes=[pltpu.VMEM((D, D), BF16), pltpu.VMEM((D, D), BF16), pltpu.VMEM((ATTN_OUT, D), BF16),
                        pltpu.VMEM((D, ATTN_OUT), BF16), pltpu.VMEM((D, D), BF16), pltpu.VMEM((D, D), BF16),
                        pltpu.VMEM((D, D), F32), pltpu.VMEM((D, D), F32), pltpu.VMEM((ATTN_OUT, D), F32),
                        pltpu.VMEM((8, D), F32)],
        compiler_params=_params(),
    )(projn, projn, projn, *o_g, *lse_g, x, tgt, mod, convw8, ebd, wc, wct, wa, wat, wo, wot)


def _dh_norm(dpn, dpq, wnt, wqt, x, dout, mod, norm_w, *, tm=512, tkn=1664, tkq=1536):
    S, D = x.shape
    kn = NAT // tkn
    kq = NQKV // tkq
    chunk = 128

    def body(dn_ref, wn_ref, dq_ref, wq_ref, x_ref, dout_ref, mod_ref, nw_ref, gx_ref, st_ref, acc_ref):
        i = pl.program_id(0)
        k = pl.program_id(1)

        @pl.when((i == 0) & (k == 0))
        def _():
            st_ref[...] = jnp.zeros_like(st_ref)

        @pl.when(k == 0)
        def _():
            acc_ref[...] = jnp.zeros_like(acc_ref)

        @pl.when(k < kn)
        def _():
            acc_ref[...] += jnp.dot(dn_ref[...], wn_ref[...], preferred_element_type=F32)

        @pl.when(k >= kn)
        def _():
            acc_ref[...] += jnp.dot(dq_ref[...], wq_ref[...], preferred_element_type=F32)

        @pl.when(k == kn + kq - 1)
        def _():
            scale1 = 1.0 + mod_ref[:, D:2 * D]
            nw = nw_ref[...]

            def step(r, carry):
                sl = pl.ds(pl.multiple_of(r * chunk, chunk), chunk)
                dh = acc_ref[sl, :]
                xv = x_ref[sl, :]
                rstd = lax.rsqrt(jnp.mean(xv * xv, axis=-1, keepdims=True) + EPS)
                xh = xv * rstd
                dhs = dh * scale1
                dxh = dhs * nw
                dx = rstd * (dxh - xh * jnp.mean(dxh * xh, axis=-1, keepdims=True))
                gx_ref[sl, :] = dout_ref[sl, :] + dx
                st_ref[0:1, :] += jnp.sum(dh, axis=0, keepdims=True)
                st_ref[1:2, :] += jnp.sum(dh * (xh * nw), axis=0, keepdims=True)
                st_ref[2:3, :] += jnp.sum(dhs * xh, axis=0, keepdims=True)
                return carry

            lax.fori_loop(0, tm // chunk, step, 0)

    return pl.pallas_call(
        body, name="dh_norm_bwd", grid=(S // tm, kn + kq),
        in_specs=[pl.BlockSpec((tm, tkn), lambda i, k: (i, jnp.minimum(k, kn - 1))),
                  pl.BlockSpec((tkn, D), lambda i, k: (jnp.minimum(k, kn - 1), 0)),
                  pl.BlockSpec((tm, tkq), lambda i, k: (i, jnp.maximum(k - kn, 0))),
                  pl.BlockSpec((tkq, D), lambda i, k: (jnp.maximum(k - kn, 0), 0)),
                  pl.BlockSpec((tm, D), lambda i, k: (i, 0)),
                  pl.BlockSpec((tm, D), lambda i, k: (i, 0)),
                  pl.BlockSpec((1, 3 * D), lambda i, k: (0, 0)),
                  pl.BlockSpec((1, D), lambda i, k: (0, 0))],
        out_specs=[pl.BlockSpec((tm, D), lambda i, k: (i, 0)), pl.BlockSpec((8, D), lambda i, k: (0, 0))],
        out_shape=[jax.ShapeDtypeStruct((S, D), F32), jax.ShapeDtypeStruct((8, D), F32)],
        scratch_shapes=[pltpu.VMEM((tm, D), F32)],
        compiler_params=_params(),
    )(dpn, wnt, dpq, wqt, x, dout, mod, norm_w)


def _pack_smalls(st, vacc, gqk):
    D = D_MODEL

    def body(st_ref, va_ref, gqk_ref, o_ref):
        o_ref[...] = jnp.zeros_like(o_ref)
        o_ref[0:2, :] = st_ref[0:2, :]
        o_ref[2:3, :] = va_ref[0:1, :]
        o_ref[8:9, :] = st_ref[2:3, :]
        o_ref[16:19, :] = va_ref[1:4, :]
        tot = gqk_ref[0]
        for k in range(1, gqk_ref.shape[0]):
            tot = tot + gqk_ref[k]
        tot = tot + pltpu.roll(tot, HEAD_DIM, 1)
        o_ref[24:32, 0:128] = tot
        loss = jnp.sum(va_ref[4:5, :], axis=1, keepdims=True) * (0.5 / D)
        o_ref[26:27, :] = jnp.broadcast_to(loss, (1, D))

    return pl.pallas_call(
        body, name="pack_smalls", out_shape=jax.ShapeDtypeStruct((32, D), F32),
        in_specs=[pl.BlockSpec(memory_space=pltpu.VMEM)] * 3,
        out_specs=pl.BlockSpec(memory_space=pltpu.VMEM), compiler_params=_params(),
    )(st, vacc, gqk)


def _small_update(sm_loc, dm_pad, ct_pad, w_ada, m_ada, v_ada, vec_w, vec_m, vec_v):
    D = D_MODEL
    cols = w_ada.shape[1]

    def body(sm_ref, dm_ref, ct_ref, w_ref, m_ref, v_ref, vw_ref, vm_ref, vv_ref,
             gw_ref, dw_ref, mw_ref, vw_o_ref, gv_ref, dv_ref, mv_ref, vv_o_ref):
        g = sm_ref[0]
        for k in range(1, 8):
            g = g + sm_ref[k]
        gv_ref[...] = g
        dv, mv, vv = _adamw_math(vw_ref[...], g, vm_ref[...], vv_ref[...])
        dv_ref[...] = dv
        mv_ref[...] = mv
        vv_o_ref[...] = vv
        gw = jnp.dot(ct_ref[...], dm_ref[...], preferred_element_type=F32, precision=lax.Precision.HIGHEST)
        gw_ref[...] = gw
        dw, mw, vw = _adamw_math(w_ref[...], gw, m_ref[...], v_ref[...])
        dw_ref[...] = dw
        mw_ref[...] = mw
        vw_o_ref[...] = vw

    sw = jax.ShapeDtypeStruct((D, cols), F32)
    sv = jax.ShapeDtypeStruct((32, D), F32)
    return pl.pallas_call(
        body, name="small_update", out_shape=[sw, sw, sw, sw, sv, sv, sv, sv],
        in_specs=[pl.BlockSpec(memory_space=pltpu.VMEM)] * 9,
        out_specs=[pl.BlockSpec(memory_space=pltpu.VMEM)] * 8, compiler_params=_params(),
    )(sm_loc, dm_pad, ct_pad, w_ada, m_ada, v_ada, vec_w, vec_m, vec_v)


def _qkv_cols(w):
    n = w.shape[0]
    return w.reshape(n, 3, N_GROUPS, 4, 128).transpose(0, 2, 3, 1, 4).reshape(n, NQKV)


def _qkv_cols_inv(w):
    n = w.shape[0]
    return w.reshape(n, N_GROUPS, 4, 3, 128).transpose(0, 3, 1, 2, 4).reshape(n, NQKV)


def _local_step(xs, tg, mod, norm_w, w_full, wc, wa, wo, convw8, q_norm_w, k_norm_w):
    S = xs.shape[0]
    w_nat = jnp.concatenate([w_full[:, :4096], w_full[:, 8704:]], axis=1)
    w_qkv = _qkv_cols(w_full[:, 4096:8704])
    lane = jnp.arange(128)
    ebd128 = (lane[:, None] // HEAD_DIM == lane[None, :] // HEAD_DIM).astype(BF16)
    lane5 = jnp.arange(ATTN_OUT)
    ebd512 = (lane5[:, None] // HEAD_DIM == lane5[None, :] // HEAD_DIM).astype(BF16)
    qnw2 = jnp.tile(q_norm_w, (1, 2))
    knw2 = jnp.tile(k_norm_w, (1, 2))

    projn, h, ht = _inproj_nat(xs, mod, norm_w, w_nat)
    projq = _matmul(h, w_qkv, tm=1024, tn=1536, name="inproj_qkv")
    o_g, lse_g = [], []
    for g in range(N_GROUPS):
        o, l = _attn_fwd(projq, qnw2, knw2, ebd128, g)
        o_g.append(o)
        lse_g.append(l)

    dpn, d_o, lse, delta, dout, vacc, gwo, gwc, gwa = _mid(
        projn, o_g, lse_g, xs, tg, mod, convw8, wc, wc.T, wa, wa.T, wo, wo.T, ebd512)
    dpq = lax.empty((S, NQKV), BF16)
    gqk = []
    for g in range(N_GROUPS):
        dpq, part = _attn_bwd(projq, dpq, d_o, lse, delta, qnw2, knw2, ebd128, g)
        gqk.append(part)
    grad_x, st = _dh_norm(dpn, dpq, w_nat.T, w_qkv.T, xs, dout, mod, norm_w)
    gw_nat = _matmul_acc(ht, dpn, tn=1664, tk=1024, name="grad_w_in_nat")
    gw_qkv = _qkv_cols_inv(_matmul_acc(ht, dpq, tn=1536, tk=1024, name="grad_w_in_qkv"))
    gw_full = jnp.concatenate([gw_nat[:, :4096], gw_qkv, gw_nat[:, 4096:]], axis=1)
    return grad_x, gw_full, gwc, gwa, gwo, st, vacc, gqk


def kernel(x, c, w_ada, b_ada, norm_w, w_in, conv_w, q_norm_w, k_norm_w, w_br_conv, w_br_attn, w_out, loss_target, m_w_ada, m_b_ada, m_norm_w, m_w_in, m_conv_w, m_q_norm_w, m_k_norm_w, m_w_br_conv, m_w_br_attn, m_w_out, v_w_ada, v_b_ada, v_norm_w, v_w_in, v_conv_w, v_q_norm_w, v_k_norm_w, v_w_br_conv, v_w_br_attn, v_w_out):
    D = D_MODEL
    S = x.shape[1]
    xs, tg = x[0], loss_target[0]
    mx, my, mc = lax.axis_index("x"), lax.axis_index("y"), lax.axis_index("c")
    chip = 2 * mx + my
    dev = 2 * chip + mc

    c_all = _allgather8(jnp.pad(c, ((0, 7), (0, 0))), "allgather_c")[:, 0, :]
    b_shard = lax.dynamic_slice(b_ada, (0, chip * 768), (1, 768))
    mod_part, silu_c = _mod_part(c_all, w_ada[0], b_shard)
    mod_parts = _allgather_chips(mod_part, "allgather_mod")
    mod_all = mod_parts.transpose(1, 0, 2).reshape(8, 3 * D)
    mod = lax.dynamic_slice(mod_all, (dev, 0), (1, 3 * D))

    shards = [w_in[0].astype(BF16).reshape(2, 512, 2816), w_br_conv[0].astype(BF16).reshape(2, 128, D),
              w_br_attn[0].astype(BF16).reshape(2, 256, 256), w_out[0].astype(BF16).reshape(2, 128, D)]
    g_in, g_bc, g_ba, g_wo = _weights_allgather(shards)
    w_full = g_in.reshape(4, D, 2816).transpose(1, 0, 2).reshape(D, 11264)
    wc = g_bc.reshape(D, D)
    wa = g_ba.reshape(4, ATTN_OUT, 256).transpose(1, 0, 2).reshape(ATTN_OUT, D)
    wo = g_wo.reshape(D, D)
    conv_full = _allgather_chips(jnp.pad(conv_w[0], ((0, 5), (0, 0))), "allgather_conv")
    convw8 = conv_full.transpose(1, 0, 2).reshape(8, D)

    grad_x, gw_full, gwc, gwa, gwo, st, vacc, gqk = _local_step(
        xs, tg, mod, norm_w, w_full, wc, wa, wo, convw8, q_norm_w, k_norm_w)

    def halves(a):
        k, r, cc = a.shape
        return a.reshape(k, 2, r // 2, cc).transpose(1, 0, 2, 3)

    grads = [halves(gw_full.reshape(D, 4, 2816).transpose(1, 0, 2)),
             halves(gwc.astype(BF16).reshape(4, 256, D)),
             halves(gwa.astype(BF16).reshape(ATTN_OUT, 4, 256).transpose(1, 0, 2)),
             halves(gwo.astype(BF16).reshape(4, 256, D))]
    recv = _pair_exchange(grads)
    core = jnp.reshape(mc, (1,)).astype(jnp.int32)
    parts = [_pair_add(core, gr, rc) for gr, rc in zip(grads, recv)]
    gathered = _chip_scatter(parts)
    mine = [_sum4(q) for q in gathered]
    full = _half_exchange(mine)
    g_w_in = full[0].reshape(D, 2816)
    g_w_bc = full[1].reshape(256, D)
    g_w_ba = full[2].reshape(ATTN_OUT, 256)
    g_w_wo = full[3].reshape(256, D)

    big = {}
    for nm, w, g, m, v in (("w_in", w_in, g_w_in, m_w_in, v_w_in), ("w_br_conv", w_br_conv, g_w_bc, m_w_br_conv, v_w_br_conv),
                           ("w_br_attn", w_br_attn, g_w_ba, m_w_br_attn, v_w_br_attn), ("w_out", w_out, g_w_wo, m_w_out, v_w_out)):
        big[nm] = (g,) + tuple(_adamw(w[0], g, m[0], v[0], "adamw_" + nm))

    sm_all = _allgather8(_pack_smalls(st, vacc, jnp.concatenate(gqk, axis=0)), "allgather_smalls")
    dmod_all = sm_all[:, 0:3, :].reshape(8, 3 * D)
    dm_pad = jnp.pad(lax.dynamic_slice(dmod_all, (0, chip * 768), (8, 768)), ((0, 120), (0, 0)))
    conv_dev = lax.dynamic_slice(sm_all[:, 16:24, :], (0, 0, chip * 256), (8, 8, 256))
    sm_loc = jnp.concatenate([sm_all[:, 0:16, :], jnp.pad(conv_dev, ((0, 0), (0, 0), (0, D - 256))), sm_all[:, 24:32, :]], axis=1)

    def pack_vec(b3, nw, cw, qw, kw):
        z = jnp.zeros((32, D), F32)
        z = z.at[0:3, :].set(b3.reshape(3, D)).at[8:9, :].set(nw).at[16:19, 0:256].set(cw)
        return z.at[24:25, 0:HEAD_DIM].set(qw).at[25:26, 0:HEAD_DIM].set(kw)

    vec_w = pack_vec(b_ada, norm_w, conv_w[0], q_norm_w, k_norm_w)
    vec_m = pack_vec(m_b_ada, m_norm_w, m_conv_w[0], m_q_norm_w, m_k_norm_w)
    vec_v = pack_vec(v_b_ada, v_norm_w, v_conv_w[0], v_q_norm_w, v_k_norm_w)
    ct_pad = jnp.pad(silu_c.T, ((0, 0), (0, 120)))
    outs = _small_update(sm_loc, dm_pad, ct_pad, w_ada[0], m_w_ada[0], v_w_ada[0], vec_w, vec_m, vec_v)
    ada = outs[0:4]
    vec = outs[4:8]
    loss = vec[0][26, 0]

    def unpack(t):
        return {"b_ada": t[0:3, :].reshape(1, 3 * D), "norm_w": t[8:9, :], "conv_w": t[16:19, 0:256][None],
                "q_norm_w": t[24:25, 0:HEAD_DIM], "k_norm_w": t[25:26, 0:HEAD_DIM]}

    small = [unpack(t) for t in vec]
    order = ["w_ada", "b_ada", "norm_w", "w_in", "conv_w", "q_norm_w", "k_norm_w", "w_br_conv", "w_br_attn", "w_out"]
    res = [loss, grad_x[None]]
    for kind in range(4):
        for nm in order:
            if nm == "w_ada":
                res.append(ada[kind][None])
            elif nm in big:
                res.append(big[nm][kind][None])
            else:
                res.append(small[kind][nm])
    return tuple(res)
```

```python
import functools

import jax
import jax.numpy as jnp
from jax import lax
from jax.experimental import pallas as pl
from jax.experimental.pallas import tpu as pltpu

F32 = jnp.float32
BF16 = jnp.bfloat16
MESH = pl.DeviceIdType.MESH
ANY = pl.BlockSpec(memory_space=pl.ANY)

D_MODEL = 1024
HEAD_DIM = 64
N_GROUPS = 3
DILATIONS = (1, 4, 16)
ATTN_OUT = 512
EPS = 1e-6
NEG_INF = -1e30
NAT = 6656
NQKV = 4608
BA, CA, XA, ZA, ZB, GA, GB = 0, 1024, 2048, 3072, 4096, 4608, 5632
ATT_TILE = 2048
QK_SCALE = HEAD_DIM ** -0.5
V7X_VMEM_LIMIT = 56 * 1024 * 1024

ADAM_LR = 0.001
ADAM_B1 = 0.9
ADAM_B2 = 0.999
ADAM_EPS = 1e-08
ADAM_WD = 0.01
ADAM_STEP = 10


def _params(**kw):
    return pltpu.CompilerParams(vmem_limit_bytes=V7X_VMEM_LIMIT, **kw)


def _sigmoid(z):
    return 1.0 / (1.0 + jnp.exp(-z))


def _row(v, r):
    rows = lax.broadcasted_iota(jnp.int32, v.shape, 0)
    return jnp.sum(jnp.where(rows == r, v, 0.0), axis=0, keepdims=True)


def _coords():
    return lax.axis_index("x"), lax.axis_index("y"), lax.axis_index("c")


def _flip(v, bit):
    return 1 - v if bit else v


def _allgather8(blk, name):
    r, c = blk.shape

    def body(x_ref, o_ref, ssem, rsem, lsem):
        mx, my, mc = _coords()
        me = 4 * mx + 2 * my + mc
        local = pltpu.make_async_copy(x_ref, o_ref.at[me], lsem)
        local.start()
        sends = []
        for j in range(1, 8):
            peer = (_flip(mx, j & 4), _flip(my, j & 2), _flip(mc, j & 1))
            cp = pltpu.make_async_remote_copy(
                src_ref=x_ref, dst_ref=o_ref.at[me], send_sem=ssem.at[j - 1], recv_sem=rsem.at[j - 1],
                device_id=peer, device_id_type=MESH)
            cp.start()
            sends.append(cp)
        for j in range(1, 8):
            px, py, pc = _flip(mx, j & 4), _flip(my, j & 2), _flip(mc, j & 1)
            pltpu.make_async_remote_copy(
                src_ref=x_ref, dst_ref=o_ref.at[4 * px + 2 * py + pc], send_sem=ssem.at[j - 1],
                recv_sem=rsem.at[j - 1], device_id=(px, py, pc), device_id_type=MESH).wait_recv()
        for cp in sends:
            cp.wait_send()
        local.wait()

    return pl.pallas_call(
        body, name=name,
        out_shape=jax.ShapeDtypeStruct((8, r, c), blk.dtype),
        in_specs=[pl.BlockSpec(memory_space=pltpu.VMEM)],
        out_specs=pl.BlockSpec(memory_space=pltpu.VMEM),
        scratch_shapes=[pltpu.SemaphoreType.DMA((7,)), pltpu.SemaphoreType.DMA((7,)), pltpu.SemaphoreType.DMA(())],
    )(blk)


def _allgather_chips(blk, name):
    r, c = blk.shape

    def body(x_ref, o_ref, ssem, rsem, lsem):
        mx, my, mc = _coords()
        me = 2 * mx + my
        local = pltpu.make_async_copy(x_ref, o_ref.at[me], lsem)
        local.start()
        sends = []
        for j in range(1, 4):
            peer = (_flip(mx, j & 2), _flip(my, j & 1), mc)
            cp = pltpu.make_async_remote_copy(
                src_ref=x_ref, dst_ref=o_ref.at[me], send_sem=ssem.at[j - 1], recv_sem=rsem.at[j - 1],
                device_id=peer, device_id_type=MESH)
            cp.start()
            sends.append(cp)
        for j in range(1, 4):
            px, py = _flip(mx, j & 2), _flip(my, j & 1)
            pltpu.make_async_remote_copy(
                src_ref=x_ref, dst_ref=o_ref.at[2 * px + py], send_sem=ssem.at[j - 1],
                recv_sem=rsem.at[j - 1], device_id=(px, py, mc), device_id_type=MESH).wait_recv()
        for cp in sends:
            cp.wait_send()
        local.wait()

    return pl.pallas_call(
        body, name=name,
        out_shape=jax.ShapeDtypeStruct((4, r, c), blk.dtype),
        in_specs=[pl.BlockSpec(memory_space=pltpu.VMEM)],
        out_specs=pl.BlockSpec(memory_space=pltpu.VMEM),
        scratch_shapes=[pltpu.SemaphoreType.DMA((3,)), pltpu.SemaphoreType.DMA((3,)), pltpu.SemaphoreType.DMA(())],
    )(blk)


def _weights_allgather(shards):
    n = len(shards)

    def body(*refs):
        ins, outs = refs[:n], refs[n:2 * n]
        ssem, rsem = refs[2 * n:]
        mx, my, mc = _coords()
        me = 2 * mx + my
        sib = (mx, my, 1 - mc)
        chips = [(_flip(mx, j & 2), _flip(my, j & 1)) for j in range(1, 4)]
        sends = []
        for t in range(n):
            for j, (px, py) in enumerate(chips):
                cp = pltpu.make_async_remote_copy(
                    src_ref=ins[t].at[mc], dst_ref=outs[t].at[me, mc], send_sem=ssem.at[t, j],
                    recv_sem=rsem.at[t, j], device_id=(px, py, mc), device_id_type=MESH)
                cp.start()
                sends.append(cp)
        for t in range(n):
            for j, (px, py) in enumerate(chips):
                src = 2 * px + py
                pltpu.make_async_remote_copy(
                    src_ref=ins[t].at[mc], dst_ref=outs[t].at[src, mc], send_sem=ssem.at[t, j],
                    recv_sem=rsem.at[t, j], device_id=(px, py, mc), device_id_type=MESH).wait_recv()
                fwd = pltpu.make_async_remote_copy(
                    src_ref=outs[t].at[src, mc], dst_ref=outs[t].at[src, mc], send_sem=ssem.at[t, 3 + j],
                    recv_sem=rsem.at[t, 3 + j], device_id=sib, device_id_type=MESH)
                fwd.start()
                sends.append(fwd)
        for t in range(n):
            for j, (px, py) in enumerate(chips):
                src = 2 * px + py
                pltpu.make_async_remote_copy(
                    src_ref=outs[t].at[src, 1 - mc], dst_ref=outs[t].at[src, 1 - mc], send_sem=ssem.at[t, 3 + j],
                    recv_sem=rsem.at[t, 3 + j], device_id=sib, device_id_type=MESH).wait_recv()
        for cp in sends:
            cp.wait_send()

    return pl.pallas_call(
        body, name="weights_allgather",
        out_shape=[jax.ShapeDtypeStruct((4,) + s.shape, s.dtype) for s in shards],
        in_specs=[ANY] * n, out_specs=[ANY] * n,
        scratch_shapes=[pltpu.SemaphoreType.DMA((n, 6)), pltpu.SemaphoreType.DMA((n, 6))],
    )(*shards)


def _pair_exchange(grads):
    n = len(grads)

    def body(*refs):
        ins, outs = refs[:n], refs[n:2 * n]
        ssem, rsem = refs[2 * n:]
        mx, my, mc = _coords()
        sib = (mx, my, 1 - mc)
        sends = []
        for t in range(n):
            cp = pltpu.make_async_remote_copy(
                src_ref=ins[t].at[1 - mc], dst_ref=outs[t], send_sem=ssem.at[t], recv_sem=rsem.at[t],
                device_id=sib, device_id_type=MESH)
            cp.start()
            sends.append(cp)
        for cp in sends:
            cp.wait_recv()
        for cp in sends:
            cp.wait_send()

    return pl.pallas_call(
        body, name="grad_pair_exchange",
        out_shape=[jax.ShapeDtypeStruct(g.shape[1:], g.dtype) for g in grads],
        in_specs=[ANY] * n, out_specs=[ANY] * n,
        scratch_shapes=[pltpu.SemaphoreType.DMA((n,)), pltpu.SemaphoreType.DMA((n,))],
    )(*grads)


def _chip_scatter(parts):
    n = len(parts)

    def body(*refs):
        ins, outs = refs[:n], refs[n:2 * n]
        ssem, rsem = refs[2 * n:]
        mx, my, mc = _coords()
        me = 2 * mx + my
        chips = [(_flip(mx, j & 2), _flip(my, j & 1)) for j in range(1, 4)]
        sends = []
        for t in range(n):
            for j, (px, py) in enumerate(chips):
                cp = pltpu.make_async_remote_copy(
                    src_ref=ins[t].at[2 * px + py], dst_ref=outs[t].at[me], send_sem=ssem.at[t, j],
                    recv_sem=rsem.at[t, j], device_id=(px, py, mc), device_id_type=MESH)
                cp.start()
                sends.append(cp)
        for t in range(n):
            for j, (px, py) in enumerate(chips):
                pltpu.make_async_remote_copy(
                    src_ref=ins[t].at[me], dst_ref=outs[t].at[2 * px + py], send_sem=ssem.at[t, j],
                    recv_sem=rsem.at[t, j], device_id=(px, py, mc), device_id_type=MESH).wait_recv()
        for cp in sends:
            cp.wait_send()

    return pl.pallas_call(
        body, name="grad_chip_scatter",
        out_shape=[jax.ShapeDtypeStruct(p.shape, p.dtype) for p in parts],
        in_specs=[ANY] * n, out_specs=[ANY] * n,
        scratch_shapes=[pltpu.SemaphoreType.DMA((n, 3)), pltpu.SemaphoreType.DMA((n, 3))],
    )(*parts)


def _half_exchange(halves):
    n = len(halves)

    def body(*refs):
        ins, outs = refs[:n], refs[n:2 * n]
        ssem, rsem = refs[2 * n:]
        mx, my, mc = _coords()
        sib = (mx, my, 1 - mc)
        sends = []
        for t in range(n):
            rc = pltpu.make_async_remote_copy(
                src_ref=ins[t], dst_ref=outs[t], send_sem=ssem.at[t], recv_sem=rsem.at[t],
                device_id=sib, device_id_type=MESH)
            rc.start()
            sends.append(rc)
        for cp in sends:
            cp.wait_recv()
        for cp in sends:
            cp.wait_send()

    return pl.pallas_call(
        body, name="grad_half_exchange",
        out_shape=[jax.ShapeDtypeStruct(h.shape, h.dtype) for h in halves],
        in_specs=[ANY] * n, out_specs=[ANY] * n,
        scratch_shapes=[pltpu.SemaphoreType.DMA((n,)), pltpu.SemaphoreType.DMA((n,))],
    )(*halves)


def _row_tile(rows, cols, bytes_per_row_elem=4, budget=2 * 1024 * 1024):
    t = rows
    while t % 2 == 0 and t > 16 and t * cols * bytes_per_row_elem > budget:
        t //= 2
    return t


def _pair_add(core, g, r):
    _, _, rh, cc = g.shape
    tr = _row_tile(rh, cc)

    def body(core_ref, g_ref, r_ref, o_ref):
        o_ref[...] = (g_ref[...].astype(F32) + r_ref[...].astype(F32)).astype(o_ref.dtype)

    return pl.pallas_call(
        body, name="grad_pair_add",
        grid_spec=pltpu.PrefetchScalarGridSpec(
            num_scalar_prefetch=1, grid=(4, rh // tr),
            in_specs=[pl.BlockSpec((None, None, tr, cc), lambda k, i, c: (c[0], k, i, 0)),
                      pl.BlockSpec((None, tr, cc), lambda k, i, c: (k, i, 0))],
            out_specs=pl.BlockSpec((None, tr, cc), lambda k, i, c: (k, i, 0))),
        out_shape=jax.ShapeDtypeStruct(r.shape, BF16),
        compiler_params=_params(),
    )(core, g, r)


def _sum4(chip, p, q):
    _, rh, cc = q.shape
    tr = _row_tile(rh, cc)

    def body(chip_ref, p_ref, q1_ref, q2_ref, q3_ref, o_ref):
        o_ref[...] = ((p_ref[...].astype(F32) + q1_ref[...].astype(F32)) + q2_ref[...].astype(F32)) + q3_ref[...].astype(F32)

    def other(j):
        return pl.BlockSpec((None, tr, cc), lambda i, c: (jnp.bitwise_xor(c[0], j), i, 0))

    return pl.pallas_call(
        body, name="grad_sum4",
        grid_spec=pltpu.PrefetchScalarGridSpec(
            num_scalar_prefetch=1, grid=(rh // tr,),
            in_specs=[pl.BlockSpec((None, tr, cc), lambda i, c: (c[0], i, 0)), other(1), other(2), other(3)],
            out_specs=pl.BlockSpec((tr, cc), lambda i, c: (i, 0))),
        out_shape=jax.ShapeDtypeStruct((rh, cc), F32),
        compiler_params=_params(),
    )(chip, p, q, q, q)


def _adamw_math(w, g, m, v):
    m = ADAM_B1 * m + (1.0 - ADAM_B1) * g
    v = ADAM_B2 * v + (1.0 - ADAM_B2) * (g * g)
    m_hat = m / (1.0 - ADAM_B1 ** ADAM_STEP)
    v_hat = v / (1.0 - ADAM_B2 ** ADAM_STEP)
    delta = -ADAM_LR * (m_hat / (jnp.sqrt(v_hat) + ADAM_EPS) + ADAM_WD * w)
    return delta, m, v


def _adamw_halves(core, w, mine, recv, m, v, name):
    rows, cols = w.shape
    rh = rows // 2
    tr = _row_tile(rh, cols, budget=1024 * 1024)
    nh = rh // tr

    def body(core_ref, w_ref, a_ref, b_ref, m_ref, v_ref, g_ref, d_ref, mo_ref, vo_ref):
        g = jnp.where(pl.program_id(0) == core_ref[0], a_ref[...], b_ref[...])
        d, mn, vn = _adamw_math(w_ref[...], g, m_ref[...], v_ref[...])
        g_ref[...] = g
        d_ref[...] = d
        mo_ref[...] = mn
        vo_ref[...] = vn

    full = pl.BlockSpec((tr, cols), lambda h, i, c: (h * nh + i, 0))
    half = pl.BlockSpec((tr, cols), lambda h, i, c: (i, 0))
    sd = jax.ShapeDtypeStruct((rows, cols), F32)
    return pl.pallas_call(
        body, name=name,
        grid_spec=pltpu.PrefetchScalarGridSpec(
            num_scalar_prefetch=1, grid=(2, nh),
            in_specs=[full, half, half, full, full], out_specs=[full] * 4),
        out_shape=[sd, sd, sd, sd], compiler_params=_params(),
    )(core, w, mine, recv, m, v)


def _mod_part(c_all, w_ada, b_shard):
    cols = w_ada.shape[1]

    def body(c_ref, w_ref, b_ref, o_ref, sc_ref):
        cv = c_ref[...]
        sc = cv * _sigmoid(cv)
        sc_ref[...] = sc
        o_ref[...] = jnp.dot(sc, w_ref[...], preferred_element_type=F32,
                             precision=lax.Precision.HIGHEST) + b_ref[...]

    return pl.pallas_call(
        body, name="adaln_mod",
        out_shape=[jax.ShapeDtypeStruct((8, cols), F32), jax.ShapeDtypeStruct(c_all.shape, F32)],
        in_specs=[pl.BlockSpec(memory_space=pltpu.VMEM)] * 3,
        out_specs=[pl.BlockSpec(memory_space=pltpu.VMEM)] * 2, compiler_params=_params(),
    )(c_all, w_ada, b_shard)


def _inproj_nat(x, mod, norm_w, w, *, tm=1024, tn=1664):
    S, D = x.shape
    N = w.shape[1]
    chunk = 256

    def body(x_ref, mod_ref, nw_ref, w_ref, proj_ref, h_ref, ht_ref):
        @pl.when(pl.program_id(1) == 0)
        def _():
            shift = mod_ref[:, 0:D]
            scale1 = 1.0 + mod_ref[:, D:2 * D]
            for r in range(tm // chunk):
                xv = x_ref[pl.ds(r * chunk, chunk), :]
                ms = jnp.mean(xv * xv, axis=-1, keepdims=True)
                h = (xv * lax.rsqrt(ms + EPS) * nw_ref[...]) * scale1 + shift
                h_ref[pl.ds(r * chunk, chunk), :] = h.astype(BF16)
                ht_ref[:, pl.ds(r * chunk, chunk)] = h.T.astype(BF16)
        proj_ref[...] = jnp.dot(h_ref[...], w_ref[...], preferred_element_type=F32).astype(BF16)

    return pl.pallas_call(
        body, name="inproj_nat", grid=(S // tm, N // tn),
        in_specs=[pl.BlockSpec((tm, D), lambda i, j: (i, 0)),
                  pl.BlockSpec((1, 3 * D), lambda i, j: (0, 0)),
                  pl.BlockSpec((1, D), lambda i, j: (0, 0)),
                  pl.BlockSpec((D, tn), lambda i, j: (0, j))],
        out_specs=[pl.BlockSpec((tm, tn), lambda i, j: (i, j)),
                   pl.BlockSpec((tm, D), lambda i, j: (i, 0)),
                   pl.BlockSpec((D, tm), lambda i, j: (0, i))],
        out_shape=[jax.ShapeDtypeStruct((S, N), BF16), jax.ShapeDtypeStruct((S, D), BF16),
                   jax.ShapeDtypeStruct((D, S), BF16)],
        compiler_params=_params(),
    )(x, mod, norm_w, w)


def _matmul(a, b, *, tm, tn, name):
    M, K = a.shape
    N = b.shape[1]

    def body(a_ref, b_ref, o_ref):
        o_ref[...] = jnp.dot(a_ref[...], b_ref[...], preferred_element_type=F32).astype(o_ref.dtype)

    return pl.pallas_call(
        body, name=name, grid=(M // tm, N // tn),
        in_specs=[pl.BlockSpec((tm, K), lambda i, j: (i, 0)), pl.BlockSpec((K, tn), lambda i, j: (0, j))],
        out_specs=pl.BlockSpec((tm, tn), lambda i, j: (i, j)),
        out_shape=jax.ShapeDtypeStruct((M, N), BF16), compiler_params=_params(),
    )(a, b)


def _matmul_acc(a, b, *, tn, tk, name):
    M, K = a.shape
    N = b.shape[1]
    nk = K // tk

    def body(a_ref, b_ref, o_ref, acc_ref):
        k = pl.program_id(1)

        @pl.when(k == 0)
        def _():
            acc_ref[...] = jnp.zeros_like(acc_ref)

        acc_ref[...] += jnp.dot(a_ref[...], b_ref[...], preferred_element_type=F32)

        @pl.when(k == nk - 1)
        def _():
            o_ref[...] = acc_ref[...].astype(o_ref.dtype)

    return pl.pallas_call(
        body, name=name, grid=(N // tn, nk),
        in_specs=[pl.BlockSpec((M, tk), lambda j, k: (0, k)), pl.BlockSpec((tk, tn), lambda j, k: (k, j))],
        out_specs=pl.BlockSpec((M, tn), lambda j, k: (0, j)),
        out_shape=jax.ShapeDtypeStruct((M, N), BF16),
        scratch_shapes=[pltpu.VMEM((M, tn), F32)], compiler_params=_params(),
    )(a, b)


def _head_norm(xb, w, ebd):
    x = xb.astype(F32)
    ss = jnp.dot((x * x).astype(BF16), ebd, preferred_element_type=F32)
    return x * lax.rsqrt(ss * (1.0 / HEAD_DIM) + EPS) * w


def _window_mask(no_prev):
    qi = lax.broadcasted_iota(jnp.int32, (128, 256), 0)
    kc = lax.broadcasted_iota(jnp.int32, (128, 256), 1)
    ok = (kc >= qi) & (kc <= qi + 128)
    if no_prev is not None:
        ok = ok & ((kc >= 128) | jnp.logical_not(no_prev))
    return ok


def _lane_masks():
    lane = lax.broadcasted_iota(jnp.int32, (1, 128), 1)
    return [(lane < HEAD_DIM).astype(F32), (lane >= HEAD_DIM).astype(F32)]


def _head_col(v, mh):
    return jnp.sum(v * mh, axis=1, keepdims=True) * (1.0 / HEAD_DIM)


def _attn_fwd(projq, qnw2, knw2, ebd, g):
    S = projq.shape[0]
    d = DILATIONS[g]
    T = ATT_TILE
    nt = S // T
    nb = T // (128 * d)

    def body(cur_ref, prev_ref, qw_ref, kw_ref, ebd_ref, o_ref, lse_ref, q32, k32, v32):
        t = pl.program_id(0)
        e = ebd_ref[...]
        q32[...] = _head_norm(cur_ref[:, 0:128], qw_ref[...], e) * QK_SCALE
        k32[pl.ds(0, T), :] = _head_norm(prev_ref[:, 128:256], kw_ref[...], e)
        k32[pl.ds(T, T), :] = _head_norm(cur_ref[:, 128:256], kw_ref[...], e)
        v32[pl.ds(0, T), :] = prev_ref[:, 256:384].astype(F32)
        v32[pl.ds(T, T), :] = cur_ref[:, 256:384].astype(F32)
        m0, m1 = _lane_masks()
        first = t == 0
        for r in range(d):
            for bb in range(nb):
                q0 = r + bb * 128 * d
                k0 = T + r + (bb - 1) * 128 * d
                qs = q32[pl.ds(q0, 128, stride=d), :]
                kb = k32[pl.ds(k0, 256, stride=d), :].astype(BF16)
                vb = v32[pl.ds(k0, 256, stride=d), :].astype(BF16)
                ok = _window_mask(first if bb == 0 else None)
                o_t = None
                l_t = None
                for mh in (m0, m1):
                    s = lax.dot_general((qs * mh).astype(BF16), kb, (((1,), (1,)), ((), ())),
                                        preferred_element_type=F32)
                    s = jnp.where(ok, s, NEG_INF)
                    mx = jnp.max(s, axis=-1, keepdims=True)
                    p = jnp.exp(s - mx)
                    l = jnp.sum(p, axis=-1, keepdims=True)
                    o = jnp.dot(p.astype(BF16), vb, preferred_element_type=F32) / l
                    lse = mx + jnp.log(l)
                    o_t = o * mh if o_t is None else o_t + o * mh
                    l_t = lse * mh if l_t is None else l_t + lse * mh
                o_ref[pl.ds(q0, 128, stride=d), :] = o_t
                lse_ref[pl.ds(q0, 128, stride=d), :] = l_t

    return pl.pallas_call(
        body, name=f"attn_fwd_g{g}", grid=(nt, 4),
        in_specs=[pl.BlockSpec((T, 384), lambda t, p: (t, g * 4 + p)),
                  pl.BlockSpec((T, 384), lambda t, p: (jnp.maximum(t - 1, 0), g * 4 + p)),
                  pl.BlockSpec((1, 128), lambda t, p: (0, 0)),
                  pl.BlockSpec((1, 128), lambda t, p: (0, 0)),
                  pl.BlockSpec((128, 128), lambda t, p: (0, 0))],
        out_specs=[pl.BlockSpec((T, 128), lambda t, p: (t, p)), pl.BlockSpec((T, 128), lambda t, p: (t, p))],
        out_shape=[jax.ShapeDtypeStruct((S, ATTN_OUT), F32), jax.ShapeDtypeStruct((S, ATTN_OUT), F32)],
        scratch_shapes=[pltpu.VMEM((T, 128), F32), pltpu.VMEM((2 * T, 128), F32), pltpu.VMEM((2 * T, 128), F32)],
        compiler_params=_params(),
    )(projq, projq, qnw2, knw2, ebd)


def _attn_bwd(projq, dprojq, d_o, lse, delta, qnw2, knw2, ebd, g):
    S = projq.shape[0]
    d = DILATIONS[g]
    T = ATT_TILE
    nt = S // T
    nb = T // (128 * d)

    def norm_bwd(raw_b, dy, w, e):
        x = raw_b.astype(F32)
        ss = jnp.dot((x * x).astype(BF16), e, preferred_element_type=F32)
        rstd = lax.rsqrt(ss * (1.0 / HEAD_DIM) + EPS)
        xh = x * rstd
        gw = jnp.sum(dy * xh, axis=0, keepdims=True)
        dxh = dy * w
        mean = jnp.dot((dxh * xh).astype(BF16), e, preferred_element_type=F32) * (1.0 / HEAD_DIM)
        return rstd * (dxh - xh * mean), gw

    def body(cur_ref, prev_ref, do_ref, lse_ref, dl_ref, qw_ref, kw_ref, ebd_ref, dp_in_ref,
             out_ref, gqk_ref, q32, k32, v32, do32, dq_cur, dq_prev, dk_acc, dv_acc):
        i = pl.program_id(1)
        e = ebd_ref[...]
        m0, m1 = _lane_masks()

        @pl.when(i == 0)
        def _():
            dk_acc[...] = jnp.zeros_like(dk_acc)
            dv_acc[...] = jnp.zeros_like(dv_acc)
            dq_prev[...] = jnp.zeros_like(dq_prev)
            gqk_ref[...] = jnp.zeros_like(gqk_ref)

        @pl.when(i < nt)
        def _():
            q32[...] = _head_norm(cur_ref[:, 0:128], qw_ref[...], e) * QK_SCALE
            k32[pl.ds(0, T), :] = _head_norm(prev_ref[:, 128:256], kw_ref[...], e)
            k32[pl.ds(T, T), :] = _head_norm(cur_ref[:, 128:256], kw_ref[...], e)
            v32[pl.ds(0, T), :] = prev_ref[:, 256:384].astype(F32)
            v32[pl.ds(T, T), :] = cur_ref[:, 256:384].astype(F32)
            do32[...] = do_ref[...].astype(F32)
            first = i == 0
            for r in range(d):
                for bb in range(nb):
                    q0 = r + bb * 128 * d
                    k0 = T + r + (bb - 1) * 128 * d
                    qs = q32[pl.ds(q0, 128, stride=d), :]
                    dos = do32[pl.ds(q0, 128, stride=d), :]
                    ls = lse_ref[pl.ds(q0, 128, stride=d), :]
                    dls = dl_ref[pl.ds(q0, 128, stride=d), :]
                    kb = k32[pl.ds(k0, 256, stride=d), :].astype(BF16)
                    vb = v32[pl.ds(k0, 256, stride=d), :].astype(BF16)
                    ok = _window_mask(first if bb == 0 else None)
                    dq_t = None
                    dk_t = None
                    dv_t = None
                    for mh in (m0, m1):
                        qh = (qs * mh).astype(BF16)
                        doh = (dos * mh).astype(BF16)
                        s = lax.dot_general(qh, kb, (((1,), (1,)), ((), ())), preferred_element_type=F32)
                        s = jnp.where(ok, s, NEG_INF)
                        p = jnp.exp(s - _head_col(ls, mh))
                        dp = lax.dot_general(doh, vb, (((1,), (1,)), ((), ())), preferred_element_type=F32)
                        ds = (p * (dp - _head_col(dls, mh))).astype(BF16)
                        dq = jnp.dot(ds, kb, preferred_element_type=F32) * mh
                        dk = lax.dot_general(ds, qh, (((0,), (0,)), ((), ())), preferred_element_type=F32)
                        dv = lax.dot_general(p.astype(BF16), doh, (((0,), (0,)), ((), ())),
                                             preferred_element_type=F32)
                        dq_t = dq if dq_t is None else dq_t + dq
                        dk_t = dk if dk_t is None else dk_t + dk
                        dv_t = dv if dv_t is None else dv_t + dv
                    dq_cur[pl.ds(q0, 128, stride=d), :] = dq_t
                    dk_acc[pl.ds(k0, 256, stride=d), :] = dk_acc[pl.ds(k0, 256, stride=d), :] + dk_t
                    dv_acc[pl.ds(k0, 256, stride=d), :] = dv_acc[pl.ds(k0, 256, stride=d), :] + dv_t

        @pl.when(i >= 1)
        def _():
            dq_raw, gq = norm_bwd(prev_ref[:, 0:128], dq_prev[...] * QK_SCALE, qw_ref[...], e)
            dk_raw, gk = norm_bwd(prev_ref[:, 128:256], dk_acc[pl.ds(0, T), :], kw_ref[...], e)
            out_ref[:, 0:128] = dq_raw.astype(BF16)
            out_ref[:, 128:256] = dk_raw.astype(BF16)
            out_ref[:, 256:384] = dv_acc[pl.ds(0, T), :].astype(BF16)
            gqk_ref[0:1, :] += gq
            gqk_ref[1:2, :] += gk

        dq_prev[...] = dq_cur[...]
        dk_acc[pl.ds(0, T), :] = dk_acc[pl.ds(T, T), :]
        dv_acc[pl.ds(0, T), :] = dv_acc[pl.ds(T, T), :]
        dk_acc[pl.ds(T, T), :] = jnp.zeros((T, 128), F32)
        dv_acc[pl.ds(T, T), :] = jnp.zeros((T, 128), F32)

    last = nt - 1
    qtile = lambda p, i: (jnp.minimum(i, last), p)
    return pl.pallas_call(
        body, name=f"attn_bwd_g{g}", grid=(4, nt + 1),
        in_specs=[pl.BlockSpec((T, 384), lambda p, i: (jnp.minimum(i, last), g * 4 + p)),
                  pl.BlockSpec((T, 384), lambda p, i: (jnp.maximum(i - 1, 0), g * 4 + p)),
                  pl.BlockSpec((T, 128), qtile), pl.BlockSpec((T, 128), qtile), pl.BlockSpec((T, 128), qtile),
                  pl.BlockSpec((1, 128), lambda p, i: (0, 0)),
                  pl.BlockSpec((1, 128), lambda p, i: (0, 0)),
                  pl.BlockSpec((128, 128), lambda p, i: (0, 0)),
                  ANY],
        out_specs=[pl.BlockSpec((T, 384), lambda p, i: (jnp.maximum(i - 1, 0), g * 4 + p)),
                   pl.BlockSpec((None, 8, 128), lambda p, i: (p, 0, 0))],
        out_shape=[jax.ShapeDtypeStruct(dprojq.shape, BF16), jax.ShapeDtypeStruct((4, 8, 128), F32)],
        scratch_shapes=[pltpu.VMEM((T, 128), F32), pltpu.VMEM((2 * T, 128), F32), pltpu.VMEM((2 * T, 128), F32),
                        pltpu.VMEM((T, 128), F32), pltpu.VMEM((T, 128), F32), pltpu.VMEM((T, 128), F32),
                        pltpu.VMEM((2 * T, 128), F32), pltpu.VMEM((2 * T, 128), F32)],
        input_output_aliases={8: 0},
        compiler_params=_params(),
    )(projq, projq, d_o, lse, delta, qnw2, knw2, ebd, dprojq)


def _mid(projn, o_g, lse_g, x, tgt, mod, convw8, wc, wct, wa, wat, wo, wot, ebd, *, tm=128):
    S, D = x.shape
    nt = S // tm
    hb = tm // 16

    def body(pn_ref, hc_ref, hx_ref, o0_ref, o1_ref, o2_ref, l0_ref, l1_ref, l2_ref, x_ref, t_ref, mod_ref,
             cw_ref, ebd_ref, wc_hbm, wct_hbm, wa_hbm, wat_hbm, wo_hbm, wot_hbm,
             dpn_ref, do_ref, lse_ref, dl_ref, dout_ref, vacc_ref, gwo_hbm, gwc_hbm, gwa_hbm,
             wc_s, wct_s, wa_s, wat_s, wo_s, wot_s, gwo_s, gwc_s, gwa_s, carry_s):
        i = pl.program_id(0)
        ti = nt - 1 - i

        @pl.when(i == 0)
        def _():
            for src, dst in ((wc_hbm, wc_s), (wct_hbm, wct_s), (wa_hbm, wa_s), (wat_hbm, wat_s),
                             (wo_hbm, wo_s), (wot_hbm, wot_s)):
                pltpu.sync_copy(src, dst)
            gwo_s[...] = jnp.zeros_like(gwo_s)
            gwc_s[...] = jnp.zeros_like(gwc_s)
            gwa_s[...] = jnp.zeros_like(gwa_s)
            carry_s[...] = jnp.zeros_like(carry_s)
            vacc_ref[...] = jnp.zeros_like(vacc_ref)

        rows = lax.broadcasted_iota(jnp.int32, (tm, D), 0)
        w0, w1, w2 = cw_ref[0:1, :], cw_ref[1:2, :], cw_ref[2:3, :]
        gate = mod_ref[:, 2 * D:3 * D]

        b_a = pn_ref[:, BA:BA + D].astype(F32)
        c_a = pn_ref[:, CA:CA + D].astype(F32)
        x_a = pn_ref[:, XA:XA + D].astype(F32)
        z_a = pn_ref[:, ZA:ZA + D].astype(F32)
        u = c_a * x_a
        has_prev = (ti > 0).astype(F32)
        uh = hc_ref[...].astype(F32) * hx_ref[...].astype(F32) * has_prev
        h14, h15 = _row(uh, 14), _row(uh, 15)
        u_m1 = jnp.where(rows == 0, h15, pltpu.roll(u, 1, 0))
        u_m2 = jnp.where(rows == 0, h14, jnp.where(rows == 1, h15, pltpu.roll(u, 2, 0)))
        conv = w0 * u_m2 + w1 * u_m1 + w2 * u
        sg_a = _sigmoid(z_a)
        sz_a = z_a * sg_a
        y_a = b_a * conv * sz_a
        y_ab = y_a.astype(BF16)
        pa = jnp.dot(y_ab, wc_s[...], preferred_element_type=F32)

        l0, l1, l2 = l0_ref[...], l1_ref[...], l2_ref[...]
        mxl = jnp.maximum(jnp.maximum(l0, l1), l2)
        e0, e1, e2 = jnp.exp(l0 - mxl), jnp.exp(l1 - mxl), jnp.exp(l2 - mxl)
        den = e0 + e1 + e2
        attn = (e0 * o0_ref[...] + e1 * o1_ref[...] + e2 * o2_ref[...]) / den
        lse_ref[...] = mxl + jnp.log(den)
        z_b = pn_ref[:, ZB:ZB + ATTN_OUT].astype(F32)
        sg_b = _sigmoid(z_b)
        sz_b = z_b * sg_b
        y_b = attn * sz_b
        y_bb = y_b.astype(BF16)
        pb = jnp.dot(y_bb, wa_s[...], preferred_element_type=F32)

        s_a = _sigmoid(pn_ref[:, GA:GA + D].astype(F32))
        s_b = _sigmoid(pn_ref[:, GB:GB + D].astype(F32))
        merged = s_a * pa + s_b * pb
        merged_b = merged.astype(BF16)
        mo = jnp.dot(merged_b, wo_s[...], preferred_element_type=F32)
        diff = x_ref[...] + gate * mo - t_ref[...]
        dout = diff * (1.0 / D)
        dout_ref[...] = dout
        vacc_ref[4:5, :] += jnp.sum(diff * diff, axis=0, keepdims=True)
        vacc_ref[0:1, :] += jnp.sum(dout * mo, axis=0, keepdims=True)

        d_mo = (dout * gate).astype(BF16)
        gwo_s[...] += lax.dot_general(merged_b, d_mo, (((0,), (0,)), ((), ())), preferred_element_type=F32)
        dmerged = jnp.dot(d_mo, wot_s[...], preferred_element_type=F32)
        dpa = (dmerged * s_a).astype(BF16)
        dpb = (dmerged * s_b).astype(BF16)
        dpn_ref[:, GA:GA + D] = (dmerged * pa * s_a * (1.0 - s_a)).astype(BF16)
        dpn_ref[:, GB:GB + D] = (dmerged * pb * s_b * (1.0 - s_b)).astype(BF16)
        gwc_s[...] += lax.dot_general(y_ab, dpa, (((0,), (0,)), ((), ())), preferred_element_type=F32)
        gwa_s[...] += lax.dot_general(y_bb, dpb, (((0,), (0,)), ((), ())), preferred_element_type=F32)
        dy_a = jnp.dot(dpa, wct_s[...], preferred_element_type=F32)
        dy_b = jnp.dot(dpb, wat_s[...], preferred_element_type=F32)

        dattn = dy_b * sz_b
        dpn_ref[:, ZB:ZB + ATTN_OUT] = (dy_b * attn * sg_b * (1.0 + z_b * (1.0 - sg_b))).astype(BF16)
        do_ref[...] = dattn.astype(BF16)
        dl_ref[...] = jnp.dot((dattn * attn).astype(BF16), ebd_ref[...], preferred_element_type=F32)

        dpn_ref[:, BA:BA + D] = (dy_a * conv * sz_a).astype(BF16)
        dpn_ref[:, ZA:ZA + D] = (dy_a * b_a * conv * sg_a * (1.0 + z_a * (1.0 - sg_a))).astype(BF16)
        dconv = dy_a * b_a * sz_a
        vacc_ref[1:2, :] += jnp.sum(dconv * u_m2, axis=0, keepdims=True)
        vacc_ref[2:3, :] += jnp.sum(dconv * u_m1, axis=0, keepdims=True)
        vacc_ref[3:4, :] += jnp.sum(dconv * u, axis=0, keepdims=True)
        nxt = carry_s[...]
        n0, n1 = _row(nxt, 0), _row(nxt, 1)
        dc_p1 = jnp.where(rows == tm - 1, n0, pltpu.roll(dconv, tm - 1, 0))
        dc_p2 = jnp.where(rows == tm - 1, n1, jnp.where(rows == tm - 2, n0, pltpu.roll(dconv, tm - 2, 0)))
        du = w2 * dconv + w1 * dc_p1 + w0 * dc_p2
        carry_s[...] = dconv[0:8, :]
        dpn_ref[:, CA:CA + D] = (du * x_a).astype(BF16)
        dpn_ref[:, XA:XA + D] = (du * c_a).astype(BF16)

        @pl.when(i == nt - 1)
        def _():
            pltpu.sync_copy(gwo_s, gwo_hbm)
            pltpu.sync_copy(gwc_s, gwc_hbm)
            pltpu.sync_copy(gwa_s, gwa_hbm)

    rev = lambda i: (nt - 1 - i, 0)
    halo = lambda col: pl.BlockSpec((16, D), lambda i: (jnp.maximum((nt - 1 - i) * hb - 1, 0), col))
    tile512 = pl.BlockSpec((tm, ATTN_OUT), rev)
    tileD = pl.BlockSpec((tm, D), rev)
    const = lambda shape: pl.BlockSpec(shape, lambda i: (0, 0))
    return pl.pallas_call(
        body, name="mid_fwd_bwd", grid=(nt,),
        in_specs=[pl.BlockSpec((tm, NAT), rev), halo(CA // D), halo(XA // D)] + [tile512] * 6
                 + [tileD, tileD, const((1, 3 * D)), const((8, D)), const((ATTN_OUT, ATTN_OUT))] + [ANY] * 6,
        out_specs=[pl.BlockSpec((tm, NAT), rev), tile512, tile512, tile512, tileD, const((8, D)), ANY, ANY, ANY],
        out_shape=[jax.ShapeDtypeStruct((S, NAT), BF16), jax.ShapeDtypeStruct((S, ATTN_OUT), BF16),
                   jax.ShapeDtypeStruct((S, ATTN_OUT), F32), jax.ShapeDtypeStruct((S, ATTN_OUT), F32),
                   jax.ShapeDtypeStruct((S, D), F32), jax.ShapeDtypeStruct((8, D), F32),
                   jax.ShapeDtypeStruct((D, D), F32), jax.ShapeDtypeStruct((D, D), F32),
                   jax.ShapeDtypeStruct((ATTN_OUT, D), F32)],
        scratch_shapes=[pltpu.VMEM((D, D), BF16), pltpu.VMEM((D, D), BF16), pltpu.VMEM((ATTN_OUT, D), BF16),
                        pltpu.VMEM((D, ATTN_OUT), BF16), pltpu.VMEM((D, D), BF16), pltpu.VMEM((D, D), BF16),
                        pltpu.VMEM((D, D), F32), pltpu.VMEM((D, D), F32), pltpu.VMEM((ATTN_OUT, D), F32),
                        pltpu.VMEM((8, D), F32)],
        compiler_params=_params(),
    )(projn, projn, projn, *o_g, *lse_g, x, tgt, mod, convw8, ebd, wc, wct, wa, wat, wo, wot)


def _dh_norm(dpn, dpq, wnt, wqt, x, dout, mod, norm_w, *, tm=512, tkn=1664, tkq=1536):
    S, D = x.shape
    kn = NAT // tkn
    kq = NQKV // tkq
    chunk = 128

    def body(dn_ref, wn_ref, dq_ref, wq_ref, x_ref, dout_ref, mod_ref, nw_ref, gx_ref, st_ref, acc_ref):
        i = pl.program_id(0)
        k = pl.program_id(1)

        @pl.when((i == 0) & (k == 0))
        def _():
            st_ref[...] = jnp.zeros_like(st_ref)

        @pl.when(k == 0)
        def _():
            acc_ref[...] = jnp.zeros_like(acc_ref)

        @pl.when(k < kn)
        def _():
            acc_ref[...] += jnp.dot(dn_ref[...], wn_ref[...], preferred_element_type=F32)

        @pl.when(k >= kn)
        def _():
            acc_ref[...] += jnp.dot(dq_ref[...], wq_ref[...], preferred_element_type=F32)

        @pl.when(k == kn + kq - 1)
        def _():
            scale1 = 1.0 + mod_ref[:, D:2 * D]
            nw = nw_ref[...]

            def step(r, carry):
                sl = pl.ds(pl.multiple_of(r * chunk, chunk), chunk)
                dh = acc_ref[sl, :]
                xv = x_ref[sl, :]
                rstd = lax.rsqrt(jnp.mean(xv * xv, axis=-1, keepdims=True) + EPS)
                xh = xv * rstd
                dhs = dh * scale1
                dxh = dhs * nw
                dx = rstd * (dxh - xh * jnp.mean(dxh * xh, axis=-1, keepdims=True))
                gx_ref[sl, :] = dout_ref[sl, :] + dx
                st_ref[0:1, :] += jnp.sum(dh, axis=0, keepdims=True)
                st_ref[1:2, :] += jnp.sum(dh * (xh * nw), axis=0, keepdims=True)
                st_ref[2:3, :] += jnp.sum(dhs * xh, axis=0, keepdims=True)
                return carry

            lax.fori_loop(0, tm // chunk, step, 0)

    return pl.pallas_call(
        body, name="dh_norm_bwd", grid=(S // tm, kn + kq),
        in_specs=[pl.BlockSpec((tm, tkn), lambda i, k: (i, jnp.minimum(k, kn - 1))),
                  pl.BlockSpec((tkn, D), lambda i, k: (jnp.minimum(k, kn - 1), 0)),
                  pl.BlockSpec((tm, tkq), lambda i, k: (i, jnp.maximum(k - kn, 0))),
                  pl.BlockSpec((tkq, D), lambda i, k: (jnp.maximum(k - kn, 0), 0)),
                  pl.BlockSpec((tm, D), lambda i, k: (i, 0)),
                  pl.BlockSpec((tm, D), lambda i, k: (i, 0)),
                  pl.BlockSpec((1, 3 * D), lambda i, k: (0, 0)),
                  pl.BlockSpec((1, D), lambda i, k: (0, 0))],
        out_specs=[pl.BlockSpec((tm, D), lambda i, k: (i, 0)), pl.BlockSpec((8, D), lambda i, k: (0, 0))],
        out_shape=[jax.ShapeDtypeStruct((S, D), F32), jax.ShapeDtypeStruct((8, D), F32)],
        scratch_shapes=[pltpu.VMEM((tm, D), F32)],
        compiler_params=_params(),
    )(dpn, wnt, dpq, wqt, x, dout, mod, norm_w)


def _pack_smalls(st, vacc, gqk):
    D = D_MODEL

    def body(st_ref, va_ref, gqk_ref, o_ref):
        o_ref[...] = jnp.zeros_like(o_ref)
        o_ref[0:2, :] = st_ref[0:2, :]
        o_ref[2:3, :] = va_ref[0:1, :]
        o_ref[8:9, :] = st_ref[2:3, :]
        o_ref[16:19, :] = va_ref[1:4, :]
        tot = gqk_ref[0]
        for k in range(1, gqk_ref.shape[0]):
            tot = tot + gqk_ref[k]
        tot = tot + pltpu.roll(tot, HEAD_DIM, 1)
        o_ref[24:32, 0:128] = tot
        loss = jnp.sum(va_ref[4:5, :], axis=1, keepdims=True) * (0.5 / D)
        o_ref[26:27, :] = jnp.broadcast_to(loss, (1, D))

    return pl.pallas_call(
        body, name="pack_smalls", out_shape=jax.ShapeDtypeStruct((32, D), F32),
        in_specs=[pl.BlockSpec(memory_space=pltpu.VMEM)] * 3,
        out_specs=pl.BlockSpec(memory_space=pltpu.VMEM), compiler_params=_params(),
    )(st, vacc, gqk)


def _small_update(sm_loc, dm_pad, ct_pad, w_ada, m_ada, v_ada, vec_w, vec_m, vec_v):
    D = D_MODEL
    cols = w_ada.shape[1]

    def body(sm_ref, dm_ref, ct_ref, w_ref, m_ref, v_ref, vw_ref, vm_ref, vv_ref,
             gw_ref, dw_ref, mw_ref, vw_o_ref, gv_ref, dv_ref, mv_ref, vv_o_ref):
        g = sm_ref[0]
        for k in range(1, 8):
            g = g + sm_ref[k]
        gv_ref[...] = g
        dv, mv, vv = _adamw_math(vw_ref[...], g, vm_ref[...], vv_ref[...])
        dv_ref[...] = dv
        mv_ref[...] = mv
        vv_o_ref[...] = vv
        gw = jnp.dot(ct_ref[...], dm_ref[...], preferred_element_type=F32, precision=lax.Precision.HIGHEST)
        gw_ref[...] = gw
        dw, mw, vw = _adamw_math(w_ref[...], gw, m_ref[...], v_ref[...])
        dw_ref[...] = dw
        mw_ref[...] = mw
        vw_o_ref[...] = vw

    sw = jax.ShapeDtypeStruct((D, cols), F32)
    sv = jax.ShapeDtypeStruct((32, D), F32)
    return pl.pallas_call(
        body, name="small_update", out_shape=[sw, sw, sw, sw, sv, sv, sv, sv],
        in_specs=[pl.BlockSpec(memory_space=pltpu.VMEM)] * 9,
        out_specs=[pl.BlockSpec(memory_space=pltpu.VMEM)] * 8, compiler_params=_params(),
    )(sm_loc, dm_pad, ct_pad, w_ada, m_ada, v_ada, vec_w, vec_m, vec_v)


def _qkv_cols(w):
    n = w.shape[0]
    return w.reshape(n, 3, N_GROUPS, 4, 128).transpose(0, 2, 3, 1, 4).reshape(n, NQKV)


def _qkv_cols_inv(w):
    n = w.shape[0]
    return w.reshape(n, N_GROUPS, 4, 3, 128).transpose(0, 3, 1, 2, 4).reshape(n, NQKV)


def _local_step(xs, tg, mod, norm_w, w_full, wc, wa, wo, convw8, q_norm_w, k_norm_w):
    S = xs.shape[0]
    w_nat = jnp.concatenate([w_full[:, :4096], w_full[:, 8704:]], axis=1)
    w_qkv = _qkv_cols(w_full[:, 4096:8704])
    lane = jnp.arange(128)
    ebd128 = (lane[:, None] // HEAD_DIM == lane[None, :] // HEAD_DIM).astype(BF16)
    lane5 = jnp.arange(ATTN_OUT)
    ebd512 = (lane5[:, None] // HEAD_DIM == lane5[None, :] // HEAD_DIM).astype(BF16)
    qnw2 = jnp.tile(q_norm_w, (1, 2))
    knw2 = jnp.tile(k_norm_w, (1, 2))

    projn, h, ht = _inproj_nat(xs, mod, norm_w, w_nat)
    projq = _matmul(h, w_qkv, tm=1024, tn=1536, name="inproj_qkv")
    o_g, lse_g = [], []
    for g in range(N_GROUPS):
        o, l = _attn_fwd(projq, qnw2, knw2, ebd128, g)
        o_g.append(o)
        lse_g.append(l)

    dpn, d_o, lse, delta, dout, vacc, gwo, gwc, gwa = _mid(
        projn, o_g, lse_g, xs, tg, mod, convw8, wc, wc.T, wa, wa.T, wo, wo.T, ebd512)
    dpq = lax.empty((S, NQKV), BF16)
    gqk = []
    for g in range(N_GROUPS):
        dpq, part = _attn_bwd(projq, dpq, d_o, lse, delta, qnw2, knw2, ebd128, g)
        gqk.append(part)
    grad_x, st = _dh_norm(dpn, dpq, w_nat.T, w_qkv.T, xs, dout, mod, norm_w)
    gw_nat = _matmul_acc(ht, dpn, tn=1664, tk=1024, name="grad_w_in_nat")
    gw_qkv = _qkv_cols_inv(_matmul_acc(ht, dpq, tn=1536, tk=1024, name="grad_w_in_qkv"))
    gw_full = jnp.concatenate([gw_nat[:, :4096], gw_qkv, gw_nat[:, 4096:]], axis=1)
    return grad_x, gw_full, gwc, gwa, gwo, st, vacc, gqk


def kernel(x, c, w_ada, b_ada, norm_w, w_in, conv_w, q_norm_w, k_norm_w, w_br_conv, w_br_attn, w_out, loss_target, m_w_ada, m_b_ada, m_norm_w, m_w_in, m_conv_w, m_q_norm_w, m_k_norm_w, m_w_br_conv, m_w_br_attn, m_w_out, v_w_ada, v_b_ada, v_norm_w, v_w_in, v_conv_w, v_q_norm_w, v_k_norm_w, v_w_br_conv, v_w_br_attn, v_w_out):
    D = D_MODEL
    S = x.shape[1]
    xs, tg = x[0], loss_target[0]
    mx, my, mc = lax.axis_index("x"), lax.axis_index("y"), lax.axis_index("c")
    chip = 2 * mx + my
    dev = 2 * chip + mc

    c_all = _allgather8(jnp.pad(c, ((0, 7), (0, 0))), "allgather_c")[:, 0, :]
    b_shard = lax.dynamic_slice(b_ada, (0, chip * 768), (1, 768))
    mod_part, silu_c = _mod_part(c_all, w_ada[0], b_shard)
    mod_parts = _allgather_chips(mod_part, "allgather_mod")
    mod_all = mod_parts.transpose(1, 0, 2).reshape(8, 3 * D)
    mod = lax.dynamic_slice(mod_all, (dev, 0), (1, 3 * D))

    shards = [w_in[0].astype(BF16).reshape(2, 512, 2816), w_br_conv[0].astype(BF16).reshape(2, 128, D),
              w_br_attn[0].astype(BF16).reshape(2, 256, 256), w_out[0].astype(BF16).reshape(2, 128, D)]
    g_in, g_bc, g_ba, g_wo = [lax.dynamic_update_slice(g, s[None], (chip, 0, 0, 0))
                              for g, s in zip(_weights_allgather(shards), shards)]
    w_full = g_in.reshape(4, D, 2816).transpose(1, 0, 2).reshape(D, 11264)
    wc = g_bc.reshape(D, D)
    wa = g_ba.reshape(4, ATTN_OUT, 256).transpose(1, 0, 2).reshape(ATTN_OUT, D)
    wo = g_wo.reshape(D, D)
    conv_full = _allgather_chips(jnp.pad(conv_w[0], ((0, 5), (0, 0))), "allgather_conv")
    convw8 = conv_full.transpose(1, 0, 2).reshape(8, D)

    grad_x, gw_full, gwc, gwa, gwo, st, vacc, gqk = _local_step(
        xs, tg, mod, norm_w, w_full, wc, wa, wo, convw8, q_norm_w, k_norm_w)

    def halves(a):
        k, r, cc = a.shape
        return a.reshape(k, 2, r // 2, cc).transpose(1, 0, 2, 3)

    grads = [halves(gw_full.reshape(D, 4, 2816).transpose(1, 0, 2)),
             halves(gwc.astype(BF16).reshape(4, 256, D)),
             halves(gwa.astype(BF16).reshape(ATTN_OUT, 4, 256).transpose(1, 0, 2)),
             halves(gwo.astype(BF16).reshape(4, 256, D))]
    recv = _pair_exchange(grads)
    core = jnp.reshape(mc, (1,)).astype(jnp.int32)
    parts = [_pair_add(core, gr, rc) for gr, rc in zip(grads, recv)]
    gathered = _chip_scatter(parts)
    chipv = jnp.reshape(chip, (1,)).astype(jnp.int32)
    mine = [_sum4(chipv, p, q) for p, q in zip(parts, gathered)]
    theirs = _half_exchange(mine)

    big = {}
    for t, (nm, w, m, v) in enumerate((("w_in", w_in, m_w_in, v_w_in), ("w_br_conv", w_br_conv, m_w_br_conv, v_w_br_conv),
                                       ("w_br_attn", w_br_attn, m_w_br_attn, v_w_br_attn), ("w_out", w_out, m_w_out, v_w_out))):
        big[nm] = _adamw_halves(core, w[0], mine[t], theirs[t], m[0], v[0], "adamw_" + nm)

    sm_all = _allgather8(_pack_smalls(st, vacc, jnp.concatenate(gqk, axis=0)), "allgather_smalls")
    dmod_all = sm_all[:, 0:3, :].reshape(8, 3 * D)
    dm_pad = jnp.pad(lax.dynamic_slice(dmod_all, (0, chip * 768), (8, 768)), ((0, 120), (0, 0)))
    conv_dev = lax.dynamic_slice(sm_all[:, 16:24, :], (0, 0, chip * 256), (8, 8, 256))
    sm_loc = jnp.concatenate([sm_all[:, 0:16, :], jnp.pad(conv_dev, ((0, 0), (0, 0), (0, D - 256))), sm_all[:, 24:32, :]], axis=1)

    def pack_vec(b3, nw, cw, qw, kw):
        z = jnp.zeros((32, D), F32)
        z = z.at[0:3, :].set(b3.reshape(3, D)).at[8:9, :].set(nw).at[16:19, 0:256].set(cw)
        return z.at[24:25, 0:HEAD_DIM].set(qw).at[25:26, 0:HEAD_DIM].set(kw)

    vec_w = pack_vec(b_ada, norm_w, conv_w[0], q_norm_w, k_norm_w)
    vec_m = pack_vec(m_b_ada, m_norm_w, m_conv_w[0], m_q_norm_w, m_k_norm_w)
    vec_v = pack_vec(v_b_ada, v_norm_w, v_conv_w[0], v_q_norm_w, v_k_norm_w)
    ct_pad = jnp.pad(silu_c.T, ((0, 0), (0, 120)))
    outs = _small_update(sm_loc, dm_pad, ct_pad, w_ada[0], m_w_ada[0], v_w_ada[0], vec_w, vec_m, vec_v)
    ada = outs[0:4]
    vec = outs[4:8]
    loss = vec[0][26, 0]

    def unpack(t):
        return {"b_ada": t[0:3, :].reshape(1, 3 * D), "norm_w": t[8:9, :], "conv_w": t[16:19, 0:256][None],
                "q_norm_w": t[24:25, 0:HEAD_DIM], "k_norm_w": t[25:26, 0:HEAD_DIM]}

    small = [unpack(t) for t in vec]
    order = ["w_ada", "b_ada", "norm_w", "w_in", "conv_w", "q_norm_w", "k_norm_w", "w_br_conv", "w_br_attn", "w_out"]
    res = [loss, grad_x[None]]
    for kind in range(4):
        for nm in order:
            if nm == "w_ada":
                res.append(ada[kind][None])
            elif nm in big:
                res.append(big[nm][kind][None])
            else:
                res.append(small[kind][nm])
    return tuple(res)
```

```python
import functools

import jax
import jax.numpy as jnp
from jax import lax
from jax.experimental import pallas as pl
from jax.experimental.pallas import tpu as pltpu

F32 = jnp.float32
BF16 = jnp.bfloat16
MESH = pl.DeviceIdType.MESH
ANY = pl.BlockSpec(memory_space=pl.ANY)

D_MODEL = 1024
HEAD_DIM = 64
N_GROUPS = 3
DILATIONS = (1, 4, 16)
ATTN_OUT = 512
EPS = 1e-6
NEG_INF = -1e30
NAT = 6656
NQKV = 4608
BA, CA, XA, ZA, ZB, GA, GB = 0, 1024, 2048, 3072, 4096, 4608, 5632
ATT_TILE = 2048
QK_SCALE = HEAD_DIM ** -0.5
V7X_VMEM_LIMIT = 56 * 1024 * 1024

ADAM_LR = 0.001
ADAM_B1 = 0.9
ADAM_B2 = 0.999
ADAM_EPS = 1e-08
ADAM_WD = 0.01
ADAM_STEP = 10


def _params(**kw):
    return pltpu.CompilerParams(vmem_limit_bytes=V7X_VMEM_LIMIT, **kw)


def _sigmoid(z):
    return 1.0 / (1.0 + jnp.exp(-z))


def _row(v, r):
    rows = lax.broadcasted_iota(jnp.int32, v.shape, 0)
    return jnp.sum(jnp.where(rows == r, v, 0.0), axis=0, keepdims=True)


def _coords():
    return lax.axis_index("x"), lax.axis_index("y"), lax.axis_index("c")


def _flip(v, bit):
    return 1 - v if bit else v


def _allgather8(blk, name):
    r, c = blk.shape

    def body(x_ref, o_ref, ssem, rsem, lsem):
        mx, my, mc = _coords()
        me = 4 * mx + 2 * my + mc
        local = pltpu.make_async_copy(x_ref, o_ref.at[me], lsem)
        local.start()
        sends = []
        for j in range(1, 8):
            peer = (_flip(mx, j & 4), _flip(my, j & 2), _flip(mc, j & 1))
            cp = pltpu.make_async_remote_copy(
                src_ref=x_ref, dst_ref=o_ref.at[me], send_sem=ssem.at[j - 1], recv_sem=rsem.at[j - 1],
                device_id=peer, device_id_type=MESH)
            cp.start()
            sends.append(cp)
        for j in range(1, 8):
            px, py, pc = _flip(mx, j & 4), _flip(my, j & 2), _flip(mc, j & 1)
            pltpu.make_async_remote_copy(
                src_ref=x_ref, dst_ref=o_ref.at[4 * px + 2 * py + pc], send_sem=ssem.at[j - 1],
                recv_sem=rsem.at[j - 1], device_id=(px, py, pc), device_id_type=MESH).wait_recv()
        for cp in sends:
            cp.wait_send()
        local.wait()

    return pl.pallas_call(
        body, name=name,
        out_shape=jax.ShapeDtypeStruct((8, r, c), blk.dtype),
        in_specs=[pl.BlockSpec(memory_space=pltpu.VMEM)],
        out_specs=pl.BlockSpec(memory_space=pltpu.VMEM),
        scratch_shapes=[pltpu.SemaphoreType.DMA((7,)), pltpu.SemaphoreType.DMA((7,)), pltpu.SemaphoreType.DMA(())],
    )(blk)


def _allgather_chips(blk, name):
    r, c = blk.shape

    def body(x_ref, o_ref, ssem, rsem, lsem):
        mx, my, mc = _coords()
        me = 2 * mx + my
        local = pltpu.make_async_copy(x_ref, o_ref.at[me], lsem)
        local.start()
        sends = []
        for j in range(1, 4):
            peer = (_flip(mx, j & 2), _flip(my, j & 1), mc)
            cp = pltpu.make_async_remote_copy(
                src_ref=x_ref, dst_ref=o_ref.at[me], send_sem=ssem.at[j - 1], recv_sem=rsem.at[j - 1],
                device_id=peer, device_id_type=MESH)
            cp.start()
            sends.append(cp)
        for j in range(1, 4):
            px, py = _flip(mx, j & 2), _flip(my, j & 1)
            pltpu.make_async_remote_copy(
                src_ref=x_ref, dst_ref=o_ref.at[2 * px + py], send_sem=ssem.at[j - 1],
                recv_sem=rsem.at[j - 1], device_id=(px, py, mc), device_id_type=MESH).wait_recv()
        for cp in sends:
            cp.wait_send()
        local.wait()

    return pl.pallas_call(
        body, name=name,
        out_shape=jax.ShapeDtypeStruct((4, r, c), blk.dtype),
        in_specs=[pl.BlockSpec(memory_space=pltpu.VMEM)],
        out_specs=pl.BlockSpec(memory_space=pltpu.VMEM),
        scratch_shapes=[pltpu.SemaphoreType.DMA((3,)), pltpu.SemaphoreType.DMA((3,)), pltpu.SemaphoreType.DMA(())],
    )(blk)


def _weights_allgather(shards):
    n = len(shards)

    def body(*refs):
        ins, outs = refs[:n], refs[n:2 * n]
        ssem, rsem = refs[2 * n:]
        mx, my, mc = _coords()
        me = 2 * mx + my
        sib = (mx, my, 1 - mc)
        chips = [(_flip(mx, j & 2), _flip(my, j & 1)) for j in range(1, 4)]
        sends = []
        for t in range(n):
            for j, (px, py) in enumerate(chips):
                cp = pltpu.make_async_remote_copy(
                    src_ref=ins[t].at[mc], dst_ref=outs[t].at[me, mc], send_sem=ssem.at[t, j],
                    recv_sem=rsem.at[t, j], device_id=(px, py, mc), device_id_type=MESH)
                cp.start()
                sends.append(cp)
        for t in range(n):
            for j, (px, py) in enumerate(chips):
                src = 2 * px + py
                pltpu.make_async_remote_copy(
                    src_ref=ins[t].at[mc], dst_ref=outs[t].at[src, mc], send_sem=ssem.at[t, j],
                    recv_sem=rsem.at[t, j], device_id=(px, py, mc), device_id_type=MESH).wait_recv()
                fwd = pltpu.make_async_remote_copy(
                    src_ref=outs[t].at[src, mc], dst_ref=outs[t].at[src, mc], send_sem=ssem.at[t, 3 + j],
                    recv_sem=rsem.at[t, 3 + j], device_id=sib, device_id_type=MESH)
                fwd.start()
                sends.append(fwd)
        for t in range(n):
            for j, (px, py) in enumerate(chips):
                src = 2 * px + py
                pltpu.make_async_remote_copy(
                    src_ref=outs[t].at[src, 1 - mc], dst_ref=outs[t].at[src, 1 - mc], send_sem=ssem.at[t, 3 + j],
                    recv_sem=rsem.at[t, 3 + j], device_id=sib, device_id_type=MESH).wait_recv()
        for cp in sends:
            cp.wait_send()

    return pl.pallas_call(
        body, name="weights_allgather",
        out_shape=[jax.ShapeDtypeStruct((4,) + s.shape, s.dtype) for s in shards],
        in_specs=[ANY] * n, out_specs=[ANY] * n,
        scratch_shapes=[pltpu.SemaphoreType.DMA((n, 6)), pltpu.SemaphoreType.DMA((n, 6))],
    )(*shards)


def _pair_exchange(grads):
    n = len(grads)

    def body(*refs):
        ins, outs = refs[:n], refs[n:2 * n]
        ssem, rsem = refs[2 * n:]
        mx, my, mc = _coords()
        sib = (mx, my, 1 - mc)
        sends = []
        for t in range(n):
            cp = pltpu.make_async_remote_copy(
                src_ref=ins[t].at[1 - mc], dst_ref=outs[t], send_sem=ssem.at[t], recv_sem=rsem.at[t],
                device_id=sib, device_id_type=MESH)
            cp.start()
            sends.append(cp)
        for cp in sends:
            cp.wait_recv()
        for cp in sends:
            cp.wait_send()

    return pl.pallas_call(
        body, name="grad_pair_exchange",
        out_shape=[jax.ShapeDtypeStruct(g.shape[1:], g.dtype) for g in grads],
        in_specs=[ANY] * n, out_specs=[ANY] * n,
        scratch_shapes=[pltpu.SemaphoreType.DMA((n,)), pltpu.SemaphoreType.DMA((n,))],
    )(*grads)


def _chip_scatter(parts):
    n = len(parts)

    def body(*refs):
        ins, outs = refs[:n], refs[n:2 * n]
        ssem, rsem = refs[2 * n:]
        mx, my, mc = _coords()
        me = 2 * mx + my
        chips = [(_flip(mx, j & 2), _flip(my, j & 1)) for j in range(1, 4)]
        sends = []
        for t in range(n):
            for j, (px, py) in enumerate(chips):
                cp = pltpu.make_async_remote_copy(
                    src_ref=ins[t].at[2 * px + py], dst_ref=outs[t].at[me], send_sem=ssem.at[t, j],
                    recv_sem=rsem.at[t, j], device_id=(px, py, mc), device_id_type=MESH)
                cp.start()
                sends.append(cp)
        for t in range(n):
            for j, (px, py) in enumerate(chips):
                pltpu.make_async_remote_copy(
                    src_ref=ins[t].at[me], dst_ref=outs[t].at[2 * px + py], send_sem=ssem.at[t, j],
                    recv_sem=rsem.at[t, j], device_id=(px, py, mc), device_id_type=MESH).wait_recv()
        for cp in sends:
            cp.wait_send()

    return pl.pallas_call(
        body, name="grad_chip_scatter",
        out_shape=[jax.ShapeDtypeStruct(p.shape, p.dtype) for p in parts],
        in_specs=[ANY] * n, out_specs=[ANY] * n,
        scratch_shapes=[pltpu.SemaphoreType.DMA((n, 3)), pltpu.SemaphoreType.DMA((n, 3))],
    )(*parts)


def _half_exchange(halves):
    n = len(halves)

    def body(*refs):
        ins, outs = refs[:n], refs[n:2 * n]
        ssem, rsem = refs[2 * n:]
        mx, my, mc = _coords()
        sib = (mx, my, 1 - mc)
        sends = []
        for t in range(n):
            rc = pltpu.make_async_remote_copy(
                src_ref=ins[t], dst_ref=outs[t], send_sem=ssem.at[t], recv_sem=rsem.at[t],
                device_id=sib, device_id_type=MESH)
            rc.start()
            sends.append(rc)
        for cp in sends:
            cp.wait_recv()
        for cp in sends:
            cp.wait_send()

    return pl.pallas_call(
        body, name="grad_half_exchange",
        out_shape=[jax.ShapeDtypeStruct(h.shape, h.dtype) for h in halves],
        in_specs=[ANY] * n, out_specs=[ANY] * n,
        scratch_shapes=[pltpu.SemaphoreType.DMA((n,)), pltpu.SemaphoreType.DMA((n,))],
    )(*halves)


def _row_tile(rows, cols, bytes_per_row_elem=4, budget=2 * 1024 * 1024):
    t = rows
    while t % 2 == 0 and t > 16 and t * cols * bytes_per_row_elem > budget:
        t //= 2
    return t


def _pair_add(core, g, r):
    _, _, rh, cc = g.shape
    tr = _row_tile(rh, cc)

    def body(core_ref, g_ref, r_ref, o_ref):
        o_ref[...] = (g_ref[...].astype(F32) + r_ref[...].astype(F32)).astype(o_ref.dtype)

    return pl.pallas_call(
        body, name="grad_pair_add",
        grid_spec=pltpu.PrefetchScalarGridSpec(
            num_scalar_prefetch=1, grid=(4, rh // tr),
            in_specs=[pl.BlockSpec((None, None, tr, cc), lambda k, i, c: (c[0], k, i, 0)),
                      pl.BlockSpec((None, tr, cc), lambda k, i, c: (k, i, 0))],
            out_specs=pl.BlockSpec((None, tr, cc), lambda k, i, c: (k, i, 0))),
        out_shape=jax.ShapeDtypeStruct(r.shape, BF16),
        compiler_params=_params(),
    )(core, g, r)


def _sum4(chip, p, q):
    _, rh, cc = q.shape
    tr = _row_tile(rh, cc)

    def body(chip_ref, p_ref, q1_ref, q2_ref, q3_ref, o_ref):
        o_ref[...] = ((p_ref[...].astype(F32) + q1_ref[...].astype(F32)) + q2_ref[...].astype(F32)) + q3_ref[...].astype(F32)

    def other(j):
        return pl.BlockSpec((None, tr, cc), lambda i, c: (jnp.bitwise_xor(c[0], j), i, 0))

    return pl.pallas_call(
        body, name="grad_sum4",
        grid_spec=pltpu.PrefetchScalarGridSpec(
            num_scalar_prefetch=1, grid=(rh // tr,),
            in_specs=[pl.BlockSpec((None, tr, cc), lambda i, c: (c[0], i, 0)), other(1), other(2), other(3)],
            out_specs=pl.BlockSpec((tr, cc), lambda i, c: (i, 0))),
        out_shape=jax.ShapeDtypeStruct((rh, cc), F32),
        compiler_params=_params(),
    )(chip, p, q, q, q)


def _adamw_math(w, g, m, v):
    m = ADAM_B1 * m + (1.0 - ADAM_B1) * g
    v = ADAM_B2 * v + (1.0 - ADAM_B2) * (g * g)
    m_hat = m / (1.0 - ADAM_B1 ** ADAM_STEP)
    v_hat = v / (1.0 - ADAM_B2 ** ADAM_STEP)
    delta = -ADAM_LR * (m_hat / (jnp.sqrt(v_hat) + ADAM_EPS) + ADAM_WD * w)
    return delta, m, v


def _adamw_halves(core, w, mine, recv, m, v, name):
    rows, cols = w.shape
    rh = rows // 2
    tr = _row_tile(rh, cols, budget=1024 * 1024)
    nh = rh // tr

    def body(core_ref, w_ref, a_ref, b_ref, m_ref, v_ref, g_ref, d_ref, mo_ref, vo_ref):
        g = jnp.where(pl.program_id(0) == core_ref[0], a_ref[...], b_ref[...])
        d, mn, vn = _adamw_math(w_ref[...], g, m_ref[...], v_ref[...])
        g_ref[...] = g
        d_ref[...] = d
        mo_ref[...] = mn
        vo_ref[...] = vn

    full = pl.BlockSpec((tr, cols), lambda h, i, c: (h * nh + i, 0))
    half = pl.BlockSpec((tr, cols), lambda h, i, c: (i, 0))
    sd = jax.ShapeDtypeStruct((rows, cols), F32)
    return pl.pallas_call(
        body, name=name,
        grid_spec=pltpu.PrefetchScalarGridSpec(
            num_scalar_prefetch=1, grid=(2, nh),
            in_specs=[full, half, half, full, full], out_specs=[full] * 4),
        out_shape=[sd, sd, sd, sd], compiler_params=_params(),
    )(core, w, mine, recv, m, v)


def _mod_part(c_all, w_ada, b_shard):
    cols = w_ada.shape[1]

    def body(c_ref, w_ref, b_ref, o_ref, sc_ref):
        cv = c_ref[...]
        sc = cv * _sigmoid(cv)
        sc_ref[...] = sc
        o_ref[...] = jnp.dot(sc, w_ref[...], preferred_element_type=F32,
                             precision=lax.Precision.HIGHEST) + b_ref[...]

    return pl.pallas_call(
        body, name="adaln_mod",
        out_shape=[jax.ShapeDtypeStruct((8, cols), F32), jax.ShapeDtypeStruct(c_all.shape, F32)],
        in_specs=[pl.BlockSpec(memory_space=pltpu.VMEM)] * 3,
        out_specs=[pl.BlockSpec(memory_space=pltpu.VMEM)] * 2, compiler_params=_params(),
    )(c_all, w_ada, b_shard)


def _inproj_nat(x, mod, norm_w, w, *, tm=1024, tn=1664):
    S, D = x.shape
    N = w.shape[1]
    chunk = 256

    def body(x_ref, mod_ref, nw_ref, w_ref, proj_ref, h_ref, ht_ref):
        @pl.when(pl.program_id(1) == 0)
        def _():
            shift = mod_ref[:, 0:D]
            scale1 = 1.0 + mod_ref[:, D:2 * D]
            for r in range(tm // chunk):
                xv = x_ref[pl.ds(r * chunk, chunk), :]
                ms = jnp.mean(xv * xv, axis=-1, keepdims=True)
                h = (xv * lax.rsqrt(ms + EPS) * nw_ref[...]) * scale1 + shift
                h_ref[pl.ds(r * chunk, chunk), :] = h.astype(BF16)
                ht_ref[:, pl.ds(r * chunk, chunk)] = h.T.astype(BF16)
        proj_ref[...] = jnp.dot(h_ref[...], w_ref[...], preferred_element_type=F32).astype(BF16)

    return pl.pallas_call(
        body, name="inproj_nat", grid=(S // tm, N // tn),
        in_specs=[pl.BlockSpec((tm, D), lambda i, j: (i, 0)),
                  pl.BlockSpec((1, 3 * D), lambda i, j: (0, 0)),
                  pl.BlockSpec((1, D), lambda i, j: (0, 0)),
                  pl.BlockSpec((D, tn), lambda i, j: (0, j))],
        out_specs=[pl.BlockSpec((tm, tn), lambda i, j: (i, j)),
                   pl.BlockSpec((tm, D), lambda i, j: (i, 0)),
                   pl.BlockSpec((D, tm), lambda i, j: (0, i))],
        out_shape=[jax.ShapeDtypeStruct((S, N), BF16), jax.ShapeDtypeStruct((S, D), BF16),
                   jax.ShapeDtypeStruct((D, S), BF16)],
        compiler_params=_params(),
    )(x, mod, norm_w, w)


def _matmul(a, b, *, tm, tn, name):
    M, K = a.shape
    N = b.shape[1]

    def body(a_ref, b_ref, o_ref):
        o_ref[...] = jnp.dot(a_ref[...], b_ref[...], preferred_element_type=F32).astype(o_ref.dtype)

    return pl.pallas_call(
        body, name=name, grid=(M // tm, N // tn),
        in_specs=[pl.BlockSpec((tm, K), lambda i, j: (i, 0)), pl.BlockSpec((K, tn), lambda i, j: (0, j))],
        out_specs=pl.BlockSpec((tm, tn), lambda i, j: (i, j)),
        out_shape=jax.ShapeDtypeStruct((M, N), BF16), compiler_params=_params(),
    )(a, b)


def _matmul_acc(a, b, *, tn, tk, name):
    M, K = a.shape
    N = b.shape[1]
    nk = K // tk

    def body(a_ref, b_ref, o_ref, acc_ref):
        k = pl.program_id(1)

        @pl.when(k == 0)
        def _():
            acc_ref[...] = jnp.zeros_like(acc_ref)

        acc_ref[...] += jnp.dot(a_ref[...], b_ref[...], preferred_element_type=F32)

        @pl.when(k == nk - 1)
        def _():
            o_ref[...] = acc_ref[...].astype(o_ref.dtype)

    return pl.pallas_call(
        body, name=name, grid=(N // tn, nk),
        in_specs=[pl.BlockSpec((M, tk), lambda j, k: (0, k)), pl.BlockSpec((tk, tn), lambda j, k: (k, j))],
        out_specs=pl.BlockSpec((M, tn), lambda j, k: (0, j)),
        out_shape=jax.ShapeDtypeStruct((M, N), BF16),
        scratch_shapes=[pltpu.VMEM((M, tn), F32)], compiler_params=_params(),
    )(a, b)


SHARD_COLS = 2816


def _column_tables(part):
    if part == "nat":
        bw = 256
        orig = [j * bw if j < 16 else 8704 + (j - 16) * bw for j in range(NAT // bw)]
    else:
        bw = 128
        orig = []
        for jj in range(NQKV // bw):
            g, p, t = jj // 12, (jj % 12) // 3, jj % 3
            orig.append(4096 + t * 1536 + g * 512 + p * 128)
    tk = jnp.array([o // SHARD_COLS for o in orig], jnp.int32)
    tc = jnp.array([(o % SHARD_COLS) // bw for o in orig], jnp.int32)
    return tk, tc, bw


def _weights_prep(gath, own, chipv, part):
    D = D_MODEL
    tk, tc, bw = _column_tables(part)
    n = tk.shape[0]

    def body(tk_ref, tc_ref, chip_ref, g_ref, o_ref, w_ref, wt_ref):
        blk = jnp.where(tk_ref[pl.program_id(0)] == chip_ref[0], o_ref[...], g_ref[...])
        w_ref[...] = blk
        wt_ref[...] = blk.astype(F32).T.astype(BF16)

    def gath_idx(j, tk, tc, ch):
        k = tk[j]
        return (jnp.where(k == ch[0], (k + 1) % 4, k), 0, tc[j])

    return pl.pallas_call(
        body, name="weights_prep_" + part,
        grid_spec=pltpu.PrefetchScalarGridSpec(
            num_scalar_prefetch=3, grid=(n,),
            in_specs=[pl.BlockSpec((None, D, bw), gath_idx),
                      pl.BlockSpec((D, bw), lambda j, tk, tc, ch: (0, tc[j]))],
            out_specs=[pl.BlockSpec((D, bw), lambda j, tk, tc, ch: (0, j)),
                       pl.BlockSpec((bw, D), lambda j, tk, tc, ch: (j, 0))]),
        out_shape=[jax.ShapeDtypeStruct((D, n * bw), BF16), jax.ShapeDtypeStruct((n * bw, D), BF16)],
        compiler_params=_params(),
    )(tk, tc, chipv, gath, own)


def _grad_relayout(gw, into, part):
    D = D_MODEL
    tk, tc, bw = _column_tables(part)
    n = tk.shape[0]

    def body(tk_ref, tc_ref, g_ref, into_ref, o_ref):
        o_ref[0] = g_ref[0:D // 2, :]
        o_ref[1] = g_ref[D // 2:D, :]

    return pl.pallas_call(
        body, name="grad_relayout_" + part,
        grid_spec=pltpu.PrefetchScalarGridSpec(
            num_scalar_prefetch=2, grid=(n,),
            in_specs=[pl.BlockSpec((D, bw), lambda j, tk, tc: (0, j)), ANY],
            out_specs=pl.BlockSpec((2, None, D // 2, bw), lambda j, tk, tc: (0, tk[j], 0, tc[j]))),
        out_shape=jax.ShapeDtypeStruct(into.shape, BF16),
        input_output_aliases={3: 0},
        compiler_params=_params(),
    )(tk, tc, gw, into)


def _head_norm(xb, w, ebd):
    x = xb.astype(F32)
    ss = jnp.dot((x * x).astype(BF16), ebd, preferred_element_type=F32)
    return x * lax.rsqrt(ss * (1.0 / HEAD_DIM) + EPS) * w


def _window_mask(no_prev):
    qi = lax.broadcasted_iota(jnp.int32, (128, 256), 0)
    kc = lax.broadcasted_iota(jnp.int32, (128, 256), 1)
    ok = (kc >= qi) & (kc <= qi + 128)
    if no_prev is not None:
        ok = ok & ((kc >= 128) | jnp.logical_not(no_prev))
    return ok


def _lane_masks():
    lane = lax.broadcasted_iota(jnp.int32, (1, 128), 1)
    return [(lane < HEAD_DIM).astype(F32), (lane >= HEAD_DIM).astype(F32)]


def _head_col(v, mh):
    return jnp.sum(v * mh, axis=1, keepdims=True) * (1.0 / HEAD_DIM)


def _attn_fwd(projq, qnw2, knw2, ebd, g):
    S = projq.shape[0]
    d = DILATIONS[g]
    T = ATT_TILE
    nt = S // T
    nb = T // (128 * d)

    def body(cur_ref, prev_ref, qw_ref, kw_ref, ebd_ref, o_ref, lse_ref, q32, k32, v32):
        t = pl.program_id(0)
        e = ebd_ref[...]
        q32[...] = _head_norm(cur_ref[:, 0:128], qw_ref[...], e) * QK_SCALE
        k32[pl.ds(0, T), :] = _head_norm(prev_ref[:, 128:256], kw_ref[...], e)
        k32[pl.ds(T, T), :] = _head_norm(cur_ref[:, 128:256], kw_ref[...], e)
        v32[pl.ds(0, T), :] = prev_ref[:, 256:384].astype(F32)
        v32[pl.ds(T, T), :] = cur_ref[:, 256:384].astype(F32)
        m0, m1 = _lane_masks()
        first = t == 0
        for r in range(d):
            for bb in range(nb):
                q0 = r + bb * 128 * d
                k0 = T + r + (bb - 1) * 128 * d
                qs = q32[pl.ds(q0, 128, stride=d), :]
                kb = k32[pl.ds(k0, 256, stride=d), :].astype(BF16)
                vb = v32[pl.ds(k0, 256, stride=d), :].astype(BF16)
                ok = _window_mask(first if bb == 0 else None)
                o_t = None
                l_t = None
                for mh in (m0, m1):
                    s = lax.dot_general((qs * mh).astype(BF16), kb, (((1,), (1,)), ((), ())),
                                        preferred_element_type=F32)
                    s = jnp.where(ok, s, NEG_INF)
                    mx = jnp.max(s, axis=-1, keepdims=True)
                    p = jnp.exp(s - mx)
                    l = jnp.sum(p, axis=-1, keepdims=True)
                    o = jnp.dot(p.astype(BF16), vb, preferred_element_type=F32) / l
                    lse = mx + jnp.log(l)
                    o_t = o * mh if o_t is None else o_t + o * mh
                    l_t = lse * mh if l_t is None else l_t + lse * mh
                o_ref[pl.ds(q0, 128, stride=d), :] = o_t
                lse_ref[pl.ds(q0, 128, stride=d), :] = l_t

    return pl.pallas_call(
        body, name=f"attn_fwd_g{g}", grid=(nt, 4),
        in_specs=[pl.BlockSpec((T, 384), lambda t, p: (t, g * 4 + p)),
                  pl.BlockSpec((T, 384), lambda t, p: (jnp.maximum(t - 1, 0), g * 4 + p)),
                  pl.BlockSpec((1, 128), lambda t, p: (0, 0)),
                  pl.BlockSpec((1, 128), lambda t, p: (0, 0)),
                  pl.BlockSpec((128, 128), lambda t, p: (0, 0))],
        out_specs=[pl.BlockSpec((T, 128), lambda t, p: (t, p)), pl.BlockSpec((T, 128), lambda t, p: (t, p))],
        out_shape=[jax.ShapeDtypeStruct((S, ATTN_OUT), F32), jax.ShapeDtypeStruct((S, ATTN_OUT), F32)],
        scratch_shapes=[pltpu.VMEM((T, 128), F32), pltpu.VMEM((2 * T, 128), F32), pltpu.VMEM((2 * T, 128), F32)],
        compiler_params=_params(),
    )(projq, projq, qnw2, knw2, ebd)


def _attn_bwd(projq, dprojq, d_o, lse, delta, qnw2, knw2, ebd, g):
    S = projq.shape[0]
    d = DILATIONS[g]
    T = ATT_TILE
    nt = S // T
    nb = T // (128 * d)

    def norm_bwd(raw_b, dy, w, e):
        x = raw_b.astype(F32)
        ss = jnp.dot((x * x).astype(BF16), e, preferred_element_type=F32)
        rstd = lax.rsqrt(ss * (1.0 / HEAD_DIM) + EPS)
        xh = x * rstd
        gw = jnp.sum(dy * xh, axis=0, keepdims=True)
        dxh = dy * w
        mean = jnp.dot((dxh * xh).astype(BF16), e, preferred_element_type=F32) * (1.0 / HEAD_DIM)
        return rstd * (dxh - xh * mean), gw

    def body(cur_ref, prev_ref, do_ref, lse_ref, dl_ref, qw_ref, kw_ref, ebd_ref, dp_in_ref,
             out_ref, gqk_ref, q32, k32, v32, do32, dq_cur, dq_prev, dk_acc, dv_acc):
        i = pl.program_id(1)
        e = ebd_ref[...]
        m0, m1 = _lane_masks()

        @pl.when(i == 0)
        def _():
            dk_acc[...] = jnp.zeros_like(dk_acc)
            dv_acc[...] = jnp.zeros_like(dv_acc)
            dq_prev[...] = jnp.zeros_like(dq_prev)
            gqk_ref[...] = jnp.zeros_like(gqk_ref)

        @pl.when(i < nt)
        def _():
            q32[...] = _head_norm(cur_ref[:, 0:128], qw_ref[...], e) * QK_SCALE
            k32[pl.ds(0, T), :] = _head_norm(prev_ref[:, 128:256], kw_ref[...], e)
            k32[pl.ds(T, T), :] = _head_norm(cur_ref[:, 128:256], kw_ref[...], e)
            v32[pl.ds(0, T), :] = prev_ref[:, 256:384].astype(F32)
            v32[pl.ds(T, T), :] = cur_ref[:, 256:384].astype(F32)
            do32[...] = do_ref[...].astype(F32)
            first = i == 0
            for r in range(d):
                for bb in range(nb):
                    q0 = r + bb * 128 * d
                    k0 = T + r + (bb - 1) * 128 * d
                    qs = q32[pl.ds(q0, 128, stride=d), :]
                    dos = do32[pl.ds(q0, 128, stride=d), :]
                    ls = lse_ref[pl.ds(q0, 128, stride=d), :]
                    dls = dl_ref[pl.ds(q0, 128, stride=d), :]
                    kb = k32[pl.ds(k0, 256, stride=d), :].astype(BF16)
                    vb = v32[pl.ds(k0, 256, stride=d), :].astype(BF16)
                    ok = _window_mask(first if bb == 0 else None)
                    dq_t = None
                    dk_t = None
                    dv_t = None
                    for mh in (m0, m1):
                        qh = (qs * mh).astype(BF16)
                        doh = (dos * mh).astype(BF16)
                        s = lax.dot_general(qh, kb, (((1,), (1,)), ((), ())), preferred_element_type=F32)
                        s = jnp.where(ok, s, NEG_INF)
                        p = jnp.exp(s - _head_col(ls, mh))
                        dp = lax.dot_general(doh, vb, (((1,), (1,)), ((), ())), preferred_element_type=F32)
                        ds = (p * (dp - _head_col(dls, mh))).astype(BF16)
                        dq = jnp.dot(ds, kb, preferred_element_type=F32) * mh
                        dk = lax.dot_general(ds, qh, (((0,), (0,)), ((), ())), preferred_element_type=F32)
                        dv = lax.dot_general(p.astype(BF16), doh, (((0,), (0,)), ((), ())),
                                             preferred_element_type=F32)
                        dq_t = dq if dq_t is None else dq_t + dq
                        dk_t = dk if dk_t is None else dk_t + dk
                        dv_t = dv if dv_t is None else dv_t + dv
                    dq_cur[pl.ds(q0, 128, stride=d), :] = dq_t
                    dk_acc[pl.ds(k0, 256, stride=d), :] = dk_acc[pl.ds(k0, 256, stride=d), :] + dk_t
                    dv_acc[pl.ds(k0, 256, stride=d), :] = dv_acc[pl.ds(k0, 256, stride=d), :] + dv_t

        @pl.when(i >= 1)
        def _():
            dq_raw, gq = norm_bwd(prev_ref[:, 0:128], dq_prev[...] * QK_SCALE, qw_ref[...], e)
            dk_raw, gk = norm_bwd(prev_ref[:, 128:256], dk_acc[pl.ds(0, T), :], kw_ref[...], e)
            out_ref[:, 0:128] = dq_raw.astype(BF16)
            out_ref[:, 128:256] = dk_raw.astype(BF16)
            out_ref[:, 256:384] = dv_acc[pl.ds(0, T), :].astype(BF16)
            gqk_ref[0:1, :] += gq
            gqk_ref[1:2, :] += gk

        dq_prev[...] = dq_cur[...]
        dk_acc[pl.ds(0, T), :] = dk_acc[pl.ds(T, T), :]
        dv_acc[pl.ds(0, T), :] = dv_acc[pl.ds(T, T), :]
        dk_acc[pl.ds(T, T), :] = jnp.zeros((T, 128), F32)
        dv_acc[pl.ds(T, T), :] = jnp.zeros((T, 128), F32)

    last = nt - 1
    qtile = lambda p, i: (jnp.minimum(i, last), p)
    return pl.pallas_call(
        body, name=f"attn_bwd_g{g}", grid=(4, nt + 1),
        in_specs=[pl.BlockSpec((T, 384), lambda p, i: (jnp.minimum(i, last), g * 4 + p)),
                  pl.BlockSpec((T, 384), lambda p, i: (jnp.maximum(i - 1, 0), g * 4 + p)),
                  pl.BlockSpec((T, 128), qtile), pl.BlockSpec((T, 128), qtile), pl.BlockSpec((T, 128), qtile),
                  pl.BlockSpec((1, 128), lambda p, i: (0, 0)),
                  pl.BlockSpec((1, 128), lambda p, i: (0, 0)),
                  pl.BlockSpec((128, 128), lambda p, i: (0, 0)),
                  ANY],
        out_specs=[pl.BlockSpec((T, 384), lambda p, i: (jnp.maximum(i - 1, 0), g * 4 + p)),
                   pl.BlockSpec((None, 8, 128), lambda p, i: (p, 0, 0))],
        out_shape=[jax.ShapeDtypeStruct(dprojq.shape, BF16), jax.ShapeDtypeStruct((4, 8, 128), F32)],
        scratch_shapes=[pltpu.VMEM((T, 128), F32), pltpu.VMEM((2 * T, 128), F32), pltpu.VMEM((2 * T, 128), F32),
                        pltpu.VMEM((T, 128), F32), pltpu.VMEM((T, 128), F32), pltpu.VMEM((T, 128), F32),
                        pltpu.VMEM((2 * T, 128), F32), pltpu.VMEM((2 * T, 128), F32)],
        input_output_aliases={8: 0},
        compiler_params=_params(),
    )(projq, projq, d_o, lse, delta, qnw2, knw2, ebd, dprojq)


def _mid(projn, o_g, lse_g, x, tgt, mod, convw8, wc, wct, wa, wat, wo, wot, ebd, *, tm=128):
    S, D = x.shape
    nt = S // tm
    hb = tm // 16

    def body(pn_ref, hc_ref, hx_ref, o0_ref, o1_ref, o2_ref, l0_ref, l1_ref, l2_ref, x_ref, t_ref, mod_ref,
             cw_ref, ebd_ref, wc_hbm, wct_hbm, wa_hbm, wat_hbm, wo_hbm, wot_hbm,
             dpn_ref, do_ref, lse_ref, dl_ref, dout_ref, vacc_ref, gwo_hbm, gwc_hbm, gwa_hbm,
             wc_s, wct_s, wa_s, wat_s, wo_s, wot_s, gwo_s, gwc_s, gwa_s, carry_s):
        i = pl.program_id(0)
        ti = nt - 1 - i

        @pl.when(i == 0)
        def _():
            for src, dst in ((wc_hbm, wc_s), (wct_hbm, wct_s), (wa_hbm, wa_s), (wat_hbm, wat_s),
                             (wo_hbm, wo_s), (wot_hbm, wot_s)):
                pltpu.sync_copy(src, dst)
            gwo_s[...] = jnp.zeros_like(gwo_s)
            gwc_s[...] = jnp.zeros_like(gwc_s)
            gwa_s[...] = jnp.zeros_like(gwa_s)
            carry_s[...] = jnp.zeros_like(carry_s)
            vacc_ref[...] = jnp.zeros_like(vacc_ref)

        rows = lax.broadcasted_iota(jnp.int32, (tm, D), 0)
        w0, w1, w2 = cw_ref[0:1, :], cw_ref[1:2, :], cw_ref[2:3, :]
        gate = mod_ref[:, 2 * D:3 * D]

        b_a = pn_ref[:, BA:BA + D].astype(F32)
        c_a = pn_ref[:, CA:CA + D].astype(F32)
        x_a = pn_ref[:, XA:XA + D].astype(F32)
        z_a = pn_ref[:, ZA:ZA + D].astype(F32)
        u = c_a * x_a
        has_prev = (ti > 0).astype(F32)
        uh = hc_ref[...].astype(F32) * hx_ref[...].astype(F32) * has_prev
        h14, h15 = _row(uh, 14), _row(uh, 15)
        u_m1 = jnp.where(rows == 0, h15, pltpu.roll(u, 1, 0))
        u_m2 = jnp.where(rows == 0, h14, jnp.where(rows == 1, h15, pltpu.roll(u, 2, 0)))
        conv = w0 * u_m2 + w1 * u_m1 + w2 * u
        sg_a = _sigmoid(z_a)
        sz_a = z_a * sg_a
        y_a = b_a * conv * sz_a
        y_ab = y_a.astype(BF16)
        pa = jnp.dot(y_ab, wc_s[...], preferred_element_type=F32)

        l0, l1, l2 = l0_ref[...], l1_ref[...], l2_ref[...]
        mxl = jnp.maximum(jnp.maximum(l0, l1), l2)
        e0, e1, e2 = jnp.exp(l0 - mxl), jnp.exp(l1 - mxl), jnp.exp(l2 - mxl)
        den = e0 + e1 + e2
        attn = (e0 * o0_ref[...] + e1 * o1_ref[...] + e2 * o2_ref[...]) / den
        lse_ref[...] = mxl + jnp.log(den)
        z_b = pn_ref[:, ZB:ZB + ATTN_OUT].astype(F32)
        sg_b = _sigmoid(z_b)
        sz_b = z_b * sg_b
        y_b = attn * sz_b
        y_bb = y_b.astype(BF16)
        pb = jnp.dot(y_bb, wa_s[...], preferred_element_type=F32)

        s_a = _sigmoid(pn_ref[:, GA:GA + D].astype(F32))
        s_b = _sigmoid(pn_ref[:, GB:GB + D].astype(F32))
        merged = s_a * pa + s_b * pb
        merged_b = merged.astype(BF16)
        mo = jnp.dot(merged_b, wo_s[...], preferred_element_type=F32)
        diff = x_ref[...] + gate * mo - t_ref[...]
        dout = diff * (1.0 / D)
        dout_ref[...] = dout
        vacc_ref[4:5, :] += jnp.sum(diff * diff, axis=0, keepdims=True)
        vacc_ref[0:1, :] += jnp.sum(dout * mo, axis=0, keepdims=True)

        d_mo = (dout * gate).astype(BF16)
        gwo_s[...] += lax.dot_general(merged_b, d_mo, (((0,), (0,)), ((), ())), preferred_element_type=F32)
        dmerged = jnp.dot(d_mo, wot_s[...], preferred_element_type=F32)
        dpa = (dmerged * s_a).astype(BF16)
        dpb = (dmerged * s_b).astype(BF16)
        dpn_ref[:, GA:GA + D] = (dmerged * pa * s_a * (1.0 - s_a)).astype(BF16)
        dpn_ref[:, GB:GB + D] = (dmerged * pb * s_b * (1.0 - s_b)).astype(BF16)
        gwc_s[...] += lax.dot_general(y_ab, dpa, (((0,), (0,)), ((), ())), preferred_element_type=F32)
        gwa_s[...] += lax.dot_general(y_bb, dpb, (((0,), (0,)), ((), ())), preferred_element_type=F32)
        dy_a = jnp.dot(dpa, wct_s[...], preferred_element_type=F32)
        dy_b = jnp.dot(dpb, wat_s[...], preferred_element_type=F32)

        dattn = dy_b * sz_b
        dpn_ref[:, ZB:ZB + ATTN_OUT] = (dy_b * attn * sg_b * (1.0 + z_b * (1.0 - sg_b))).astype(BF16)
        do_ref[...] = dattn.astype(BF16)
        dl_ref[...] = jnp.dot((dattn * attn).astype(BF16), ebd_ref[...], preferred_element_type=F32)

        dpn_ref[:, BA:BA + D] = (dy_a * conv * sz_a).astype(BF16)
        dpn_ref[:, ZA:ZA + D] = (dy_a * b_a * conv * sg_a * (1.0 + z_a * (1.0 - sg_a))).astype(BF16)
        dconv = dy_a * b_a * sz_a
        vacc_ref[1:2, :] += jnp.sum(dconv * u_m2, axis=0, keepdims=True)
        vacc_ref[2:3, :] += jnp.sum(dconv * u_m1, axis=0, keepdims=True)
        vacc_ref[3:4, :] += jnp.sum(dconv * u, axis=0, keepdims=True)
        nxt = carry_s[...]
        n0, n1 = _row(nxt, 0), _row(nxt, 1)
        dc_p1 = jnp.where(rows == tm - 1, n0, pltpu.roll(dconv, tm - 1, 0))
        dc_p2 = jnp.where(rows == tm - 1, n1, jnp.where(rows == tm - 2, n0, pltpu.roll(dconv, tm - 2, 0)))
        du = w2 * dconv + w1 * dc_p1 + w0 * dc_p2
        carry_s[...] = dconv[0:8, :]
        dpn_ref[:, CA:CA + D] = (du * x_a).astype(BF16)
        dpn_ref[:, XA:XA + D] = (du * c_a).astype(BF16)

        @pl.when(i == nt - 1)
        def _():
            pltpu.sync_copy(gwo_s, gwo_hbm)
            pltpu.sync_copy(gwc_s, gwc_hbm)
            pltpu.sync_copy(gwa_s, gwa_hbm)

    rev = lambda i: (nt - 1 - i, 0)
    halo = lambda col: pl.BlockSpec((16, D), lambda i: (jnp.maximum((nt - 1 - i) * hb - 1, 0), col))
    tile512 = pl.BlockSpec((tm, ATTN_OUT), rev)
    tileD = pl.BlockSpec((tm, D), rev)
    const = lambda shape: pl.BlockSpec(shape, lambda i: (0, 0))
    return pl.pallas_call(
        body, name="mid_fwd_bwd", grid=(nt,),
        in_specs=[pl.BlockSpec((tm, NAT), rev), halo(CA // D), halo(XA // D)] + [tile512] * 6
                 + [tileD, tileD, const((1, 3 * D)), const((8, D)), const((ATTN_OUT, ATTN_OUT))] + [ANY] * 6,
        out_specs=[pl.BlockSpec((tm, NAT), rev), tile512, tile512, tile512, tileD, const((8, D)), ANY, ANY, ANY],
        out_shape=[jax.ShapeDtypeStruct((S, NAT), BF16), jax.ShapeDtypeStruct((S, ATTN_OUT), BF16),
                   jax.ShapeDtypeStruct((S, ATTN_OUT), F32), jax.ShapeDtypeStruct((S, ATTN_OUT), F32),
                   jax.ShapeDtypeStruct((S, D), F32), jax.ShapeDtypeStruct((8, D), F32),
                   jax.ShapeDtypeStruct((D, D), F32), jax.ShapeDtypeStruct((D, D), F32),
                   jax.ShapeDtypeStruct((ATTN_OUT, D), F32)],
        scratch_shapes=[pltpu.VMEM((D, D), BF16), pltpu.VMEM((D, D), BF16), pltpu.VMEM((ATTN_OUT, D), BF16),
                        pltpu.VMEM((D, ATTN_OUT), BF16), pltpu.VMEM((D, D), BF16), pltpu.VMEM((D, D), BF16),
                        pltpu.VMEM((D, D), F32), pltpu.VMEM((D, D), F32), pltpu.VMEM((ATTN_OUT, D), F32),
                        pltpu.VMEM((8, D), F32)],
        compiler_params=_params(),
    )(projn, projn, projn, *o_g, *lse_g, x, tgt, mod, convw8, ebd, wc, wct, wa, wat, wo, wot)


def _dh_norm(dpn, dpq, wnt, wqt, x, dout, mod, norm_w, *, tm=256):
    S, D = x.shape

    def body(dn_ref, dq_ref, x_ref, dout_ref, mod_ref, nw_ref, wn_hbm, wq_hbm, gx_ref, st_ref, wn_s, wq_s):
        @pl.when(pl.program_id(0) == 0)
        def _():
            pltpu.sync_copy(wn_hbm, wn_s)
            pltpu.sync_copy(wq_hbm, wq_s)
            st_ref[...] = jnp.zeros_like(st_ref)

        dh = (jnp.dot(dn_ref[...], wn_s[...], preferred_element_type=F32)
              + jnp.dot(dq_ref[...], wq_s[...], preferred_element_type=F32))
        scale1 = 1.0 + mod_ref[:, D:2 * D]
        nw = nw_ref[...]
        xv = x_ref[...]
        rstd = lax.rsqrt(jnp.mean(xv * xv, axis=-1, keepdims=True) + EPS)
        xh = xv * rstd
        dhs = dh * scale1
        dxh = dhs * nw
        dx = rstd * (dxh - xh * jnp.mean(dxh * xh, axis=-1, keepdims=True))
        gx_ref[...] = dout_ref[...] + dx
        st_ref[0:1, :] += jnp.sum(dh, axis=0, keepdims=True)
        st_ref[1:2, :] += jnp.sum(dh * (xh * nw), axis=0, keepdims=True)
        st_ref[2:3, :] += jnp.sum(dhs * xh, axis=0, keepdims=True)

    row = lambda n: pl.BlockSpec((tm, n), lambda i: (i, 0))
    return pl.pallas_call(
        body, name="dh_norm_bwd", grid=(S // tm,),
        in_specs=[row(NAT), row(NQKV), row(D), row(D),
                  pl.BlockSpec((1, 3 * D), lambda i: (0, 0)), pl.BlockSpec((1, D), lambda i: (0, 0)), ANY, ANY],
        out_specs=[row(D), pl.BlockSpec((8, D), lambda i: (0, 0))],
        out_shape=[jax.ShapeDtypeStruct((S, D), F32), jax.ShapeDtypeStruct((8, D), F32)],
        scratch_shapes=[pltpu.VMEM((NAT, D), BF16), pltpu.VMEM((NQKV, D), BF16)],
        compiler_params=_params(),
    )(dpn, dpq, x, dout, mod, norm_w, wnt, wqt)


def _pack_smalls(st, vacc, gqk):
    D = D_MODEL

    def body(st_ref, va_ref, gqk_ref, o_ref):
        o_ref[...] = jnp.zeros_like(o_ref)
        o_ref[0:2, :] = st_ref[0:2, :]
        o_ref[2:3, :] = va_ref[0:1, :]
        o_ref[8:9, :] = st_ref[2:3, :]
        o_ref[16:19, :] = va_ref[1:4, :]
        tot = gqk_ref[0]
        for k in range(1, gqk_ref.shape[0]):
            tot = tot + gqk_ref[k]
        tot = tot + pltpu.roll(tot, HEAD_DIM, 1)
        o_ref[24:32, 0:128] = tot
        loss = jnp.sum(va_ref[4:5, :], axis=1, keepdims=True) * (0.5 / D)
        o_ref[26:27, :] = jnp.broadcast_to(loss, (1, D))

    return pl.pallas_call(
        body, name="pack_smalls", out_shape=jax.ShapeDtypeStruct((32, D), F32),
        in_specs=[pl.BlockSpec(memory_space=pltpu.VMEM)] * 3,
        out_specs=pl.BlockSpec(memory_space=pltpu.VMEM), compiler_params=_params(),
    )(st, vacc, gqk)


def _small_update(sm_loc, dm_pad, ct_pad, w_ada, m_ada, v_ada, vec_w, vec_m, vec_v):
    D = D_MODEL
    cols = w_ada.shape[1]

    def body(sm_ref, dm_ref, ct_ref, w_ref, m_ref, v_ref, vw_ref, vm_ref, vv_ref,
             gw_ref, dw_ref, mw_ref, vw_o_ref, gv_ref, dv_ref, mv_ref, vv_o_ref):
        g = sm_ref[0]
        for k in range(1, 8):
            g = g + sm_ref[k]
        gv_ref[...] = g
        dv, mv, vv = _adamw_math(vw_ref[...], g, vm_ref[...], vv_ref[...])
        dv_ref[...] = dv
        mv_ref[...] = mv
        vv_o_ref[...] = vv
        gw = jnp.dot(ct_ref[...], dm_ref[...], preferred_element_type=F32, precision=lax.Precision.HIGHEST)
        gw_ref[...] = gw
        dw, mw, vw = _adamw_math(w_ref[...], gw, m_ref[...], v_ref[...])
        dw_ref[...] = dw
        mw_ref[...] = mw
        vw_o_ref[...] = vw

    sw = jax.ShapeDtypeStruct((D, cols), F32)
    sv = jax.ShapeDtypeStruct((32, D), F32)
    return pl.pallas_call(
        body, name="small_update", out_shape=[sw, sw, sw, sw, sv, sv, sv, sv],
        in_specs=[pl.BlockSpec(memory_space=pltpu.VMEM)] * 9,
        out_specs=[pl.BlockSpec(memory_space=pltpu.VMEM)] * 8, compiler_params=_params(),
    )(sm_loc, dm_pad, ct_pad, w_ada, m_ada, v_ada, vec_w, vec_m, vec_v)


def _local_step(xs, tg, mod, norm_w, w_nat, w_nat_t, w_qkv, w_qkv_t, wc, wa, wo, convw8, q_norm_w, k_norm_w):
    S = xs.shape[0]
    lane = jnp.arange(128)
    ebd128 = (lane[:, None] // HEAD_DIM == lane[None, :] // HEAD_DIM).astype(BF16)
    lane5 = jnp.arange(ATTN_OUT)
    ebd512 = (lane5[:, None] // HEAD_DIM == lane5[None, :] // HEAD_DIM).astype(BF16)
    qnw2 = jnp.tile(q_norm_w, (1, 2))
    knw2 = jnp.tile(k_norm_w, (1, 2))

    projn, h, ht = _inproj_nat(xs, mod, norm_w, w_nat)
    projq = _matmul(h, w_qkv, tm=1024, tn=1536, name="inproj_qkv")
    o_g, lse_g = [], []
    for g in range(N_GROUPS):
        o, l = _attn_fwd(projq, qnw2, knw2, ebd128, g)
        o_g.append(o)
        lse_g.append(l)

    dpn, d_o, lse, delta, dout, vacc, gwo, gwc, gwa = _mid(
        projn, o_g, lse_g, xs, tg, mod, convw8, wc, wc.T, wa, wa.T, wo, wo.T, ebd512)
    dpq = lax.empty((S, NQKV), BF16)
    gqk = []
    for g in range(N_GROUPS):
        dpq, part = _attn_bwd(projq, dpq, d_o, lse, delta, qnw2, knw2, ebd128, g)
        gqk.append(part)
    grad_x, st = _dh_norm(dpn, dpq, w_nat_t, w_qkv_t, xs, dout, mod, norm_w)
    gw_nat = _matmul_acc(ht, dpn, tn=1664, tk=1024, name="grad_w_in_nat")
    gw_qkv = _matmul_acc(ht, dpq, tn=1536, tk=1024, name="grad_w_in_qkv")
    return grad_x, gw_nat, gw_qkv, gwc, gwa, gwo, st, vacc, gqk


def kernel(x, c, w_ada, b_ada, norm_w, w_in, conv_w, q_norm_w, k_norm_w, w_br_conv, w_br_attn, w_out, loss_target, m_w_ada, m_b_ada, m_norm_w, m_w_in, m_conv_w, m_q_norm_w, m_k_norm_w, m_w_br_conv, m_w_br_attn, m_w_out, v_w_ada, v_b_ada, v_norm_w, v_w_in, v_conv_w, v_q_norm_w, v_k_norm_w, v_w_br_conv, v_w_br_attn, v_w_out):
    D = D_MODEL
    S = x.shape[1]
    xs, tg = x[0], loss_target[0]
    mx, my, mc = lax.axis_index("x"), lax.axis_index("y"), lax.axis_index("c")
    chip = 2 * mx + my
    dev = 2 * chip + mc

    c_all = _allgather8(jnp.pad(c, ((0, 7), (0, 0))), "allgather_c")[:, 0, :]
    b_shard = lax.dynamic_slice(b_ada, (0, chip * 768), (1, 768))
    mod_part, silu_c = _mod_part(c_all, w_ada[0], b_shard)
    mod_parts = _allgather_chips(mod_part, "allgather_mod")
    mod_all = mod_parts.transpose(1, 0, 2).reshape(8, 3 * D)
    mod = lax.dynamic_slice(mod_all, (dev, 0), (1, 3 * D))

    shards = [w_in[0].astype(BF16).reshape(2, 512, 2816), w_br_conv[0].astype(BF16).reshape(2, 128, D),
              w_br_attn[0].astype(BF16).reshape(2, 256, 256), w_out[0].astype(BF16).reshape(2, 128, D)]
    gathered_w = _weights_allgather(shards)
    g_bc, g_ba, g_wo = [lax.dynamic_update_slice(g, s[None], (chip, 0, 0, 0))
                        for g, s in zip(gathered_w[1:], shards[1:])]
    chipv = jnp.reshape(chip, (1,)).astype(jnp.int32)
    g_in = gathered_w[0].reshape(4, D, SHARD_COLS)
    own_in = shards[0].reshape(D, SHARD_COLS)
    w_nat, w_nat_t = _weights_prep(g_in, own_in, chipv, "nat")
    w_qkv, w_qkv_t = _weights_prep(g_in, own_in, chipv, "qkv")
    wc = g_bc.reshape(D, D)
    wa = g_ba.reshape(4, ATTN_OUT, 256).transpose(1, 0, 2).reshape(ATTN_OUT, D)
    wo = g_wo.reshape(D, D)
    conv_full = _allgather_chips(jnp.pad(conv_w[0], ((0, 5), (0, 0))), "allgather_conv")
    convw8 = conv_full.transpose(1, 0, 2).reshape(8, D)

    grad_x, gw_nat, gw_qkv, gwc, gwa, gwo, st, vacc, gqk = _local_step(
        xs, tg, mod, norm_w, w_nat, w_nat_t, w_qkv, w_qkv_t, wc, wa, wo, convw8, q_norm_w, k_norm_w)
    gw_in = _grad_relayout(gw_nat, lax.empty((2, 4, D // 2, SHARD_COLS), BF16), "nat")
    gw_in = _grad_relayout(gw_qkv, gw_in, "qkv")

    def halves(a):
        k, r, cc = a.shape
        return a.reshape(k, 2, r // 2, cc).transpose(1, 0, 2, 3)

    grads = [gw_in,
             halves(gwc.astype(BF16).reshape(4, 256, D)),
             halves(gwa.astype(BF16).reshape(ATTN_OUT, 4, 256).transpose(1, 0, 2)),
             halves(gwo.astype(BF16).reshape(4, 256, D))]
    recv = _pair_exchange(grads)
    core = jnp.reshape(mc, (1,)).astype(jnp.int32)
    parts = [_pair_add(core, gr, rc) for gr, rc in zip(grads, recv)]
    gathered = _chip_scatter(parts)
    mine = [_sum4(chipv, p, q) for p, q in zip(parts, gathered)]
    theirs = _half_exchange(mine)

    big = {}
    for t, (nm, w, m, v) in enumerate((("w_in", w_in, m_w_in, v_w_in), ("w_br_conv", w_br_conv, m_w_br_conv, v_w_br_conv),
                                       ("w_br_attn", w_br_attn, m_w_br_attn, v_w_br_attn), ("w_out", w_out, m_w_out, v_w_out))):
        big[nm] = _adamw_halves(core, w[0], mine[t], theirs[t], m[0], v[0], "adamw_" + nm)

    sm_all = _allgather8(_pack_smalls(st, vacc, jnp.concatenate(gqk, axis=0)), "allgather_smalls")
    dmod_all = sm_all[:, 0:3, :].reshape(8, 3 * D)
    dm_pad = jnp.pad(lax.dynamic_slice(dmod_all, (0, chip * 768), (8, 768)), ((0, 120), (0, 0)))
    conv_dev = lax.dynamic_slice(sm_all[:, 16:24, :], (0, 0, chip * 256), (8, 8, 256))
    sm_loc = jnp.concatenate([sm_all[:, 0:16, :], jnp.pad(conv_dev, ((0, 0), (0, 0), (0, D - 256))), sm_all[:, 24:32, :]], axis=1)

    def pack_vec(b3, nw, cw, qw, kw):
        z = jnp.zeros((32, D), F32)
        z = z.at[0:3, :].set(b3.reshape(3, D)).at[8:9, :].set(nw).at[16:19, 0:256].set(cw)
        return z.at[24:25, 0:HEAD_DIM].set(qw).at[25:26, 0:HEAD_DIM].set(kw)

    vec_w = pack_vec(b_ada, norm_w, conv_w[0], q_norm_w, k_norm_w)
    vec_m = pack_vec(m_b_ada, m_norm_w, m_conv_w[0], m_q_norm_w, m_k_norm_w)
    vec_v = pack_vec(v_b_ada, v_norm_w, v_conv_w[0], v_q_norm_w, v_k_norm_w)
    ct_pad = jnp.pad(silu_c.T, ((0, 0), (0, 120)))
    outs = _small_update(sm_loc, dm_pad, ct_pad, w_ada[0], m_w_ada[0], v_w_ada[0], vec_w, vec_m, vec_v)
    ada = outs[0:4]
    vec = outs[4:8]
    loss = vec[0][26, 0]

    def unpack(t):
        return {"b_ada": t[0:3, :].reshape(1, 3 * D), "norm_w": t[8:9, :], "conv_w": t[16:19, 0:256][None],
                "q_norm_w": t[24:25, 0:HEAD_DIM], "k_norm_w": t[25:26, 0:HEAD_DIM]}

    small = [unpack(t) for t in vec]
    order = ["w_ada", "b_ada", "norm_w", "w_in", "conv_w", "q_norm_w", "k_norm_w", "w_br_conv", "w_br_attn", "w_out"]
    res = [loss, grad_x[None]]
    for kind in range(4):
        for nm in order:
            if nm == "w_ada":
                res.append(ada[kind][None])
            elif nm in big:
                res.append(big[nm][kind][None])
            else:
                res.append(small[kind][nm])
    return tuple(res)
```

```python
import functools

import jax
import jax.numpy as jnp
from jax import lax
from jax.experimental import pallas as pl
from jax.experimental.pallas import tpu as pltpu

F32 = jnp.float32
BF16 = jnp.bfloat16
MESH = pl.DeviceIdType.MESH
ANY = pl.BlockSpec(memory_space=pl.ANY)

D_MODEL = 1024
HEAD_DIM = 64
N_GROUPS = 3
DILATIONS = (1, 4, 16)
ATTN_OUT = 512
EPS = 1e-6
NEG_INF = -1e30
NAT = 6656
NQKV = 4608
BA, CA, XA, ZA, ZB, GA, GB = 0, 1024, 2048, 3072, 4096, 4608, 5632
ATT_TILE = 2048
QK_SCALE = HEAD_DIM ** -0.5
V7X_VMEM_LIMIT = 56 * 1024 * 1024

ADAM_LR = 0.001
ADAM_B1 = 0.9
ADAM_B2 = 0.999
ADAM_EPS = 1e-08
ADAM_WD = 0.01
ADAM_STEP = 10


def _params(**kw):
    return pltpu.CompilerParams(vmem_limit_bytes=V7X_VMEM_LIMIT, **kw)


def _sigmoid(z):
    return 1.0 / (1.0 + jnp.exp(-z))


def _row(v, r):
    rows = lax.broadcasted_iota(jnp.int32, v.shape, 0)
    return jnp.sum(jnp.where(rows == r, v, 0.0), axis=0, keepdims=True)


def _coords():
    return lax.axis_index("x"), lax.axis_index("y"), lax.axis_index("c")


def _flip(v, bit):
    return 1 - v if bit else v


def _allgather8(blk, name):
    r, c = blk.shape

    def body(x_ref, o_ref, ssem, rsem, lsem):
        mx, my, mc = _coords()
        me = 4 * mx + 2 * my + mc
        local = pltpu.make_async_copy(x_ref, o_ref.at[me], lsem)
        local.start()
        sends = []
        for j in range(1, 8):
            peer = (_flip(mx, j & 4), _flip(my, j & 2), _flip(mc, j & 1))
            cp = pltpu.make_async_remote_copy(
                src_ref=x_ref, dst_ref=o_ref.at[me], send_sem=ssem.at[j - 1], recv_sem=rsem.at[j - 1],
                device_id=peer, device_id_type=MESH)
            cp.start()
            sends.append(cp)
        for j in range(1, 8):
            px, py, pc = _flip(mx, j & 4), _flip(my, j & 2), _flip(mc, j & 1)
            pltpu.make_async_remote_copy(
                src_ref=x_ref, dst_ref=o_ref.at[4 * px + 2 * py + pc], send_sem=ssem.at[j - 1],
                recv_sem=rsem.at[j - 1], device_id=(px, py, pc), device_id_type=MESH).wait_recv()
        for cp in sends:
            cp.wait_send()
        local.wait()

    return pl.pallas_call(
        body, name=name,
        out_shape=jax.ShapeDtypeStruct((8, r, c), blk.dtype),
        in_specs=[pl.BlockSpec(memory_space=pltpu.VMEM)],
        out_specs=pl.BlockSpec(memory_space=pltpu.VMEM),
        scratch_shapes=[pltpu.SemaphoreType.DMA((7,)), pltpu.SemaphoreType.DMA((7,)), pltpu.SemaphoreType.DMA(())],
    )(blk)


def _allgather_chips(blk, name):
    r, c = blk.shape

    def body(x_ref, o_ref, ssem, rsem, lsem):
        mx, my, mc = _coords()
        me = 2 * mx + my
        local = pltpu.make_async_copy(x_ref, o_ref.at[me], lsem)
        local.start()
        sends = []
        for j in range(1, 4):
            peer = (_flip(mx, j & 2), _flip(my, j & 1), mc)
            cp = pltpu.make_async_remote_copy(
                src_ref=x_ref, dst_ref=o_ref.at[me], send_sem=ssem.at[j - 1], recv_sem=rsem.at[j - 1],
                device_id=peer, device_id_type=MESH)
            cp.start()
            sends.append(cp)
        for j in range(1, 4):
            px, py = _flip(mx, j & 2), _flip(my, j & 1)
            pltpu.make_async_remote_copy(
                src_ref=x_ref, dst_ref=o_ref.at[2 * px + py], send_sem=ssem.at[j - 1],
                recv_sem=rsem.at[j - 1], device_id=(px, py, mc), device_id_type=MESH).wait_recv()
        for cp in sends:
            cp.wait_send()
        local.wait()

    return pl.pallas_call(
        body, name=name,
        out_shape=jax.ShapeDtypeStruct((4, r, c), blk.dtype),
        in_specs=[pl.BlockSpec(memory_space=pltpu.VMEM)],
        out_specs=pl.BlockSpec(memory_space=pltpu.VMEM),
        scratch_shapes=[pltpu.SemaphoreType.DMA((3,)), pltpu.SemaphoreType.DMA((3,)), pltpu.SemaphoreType.DMA(())],
    )(blk)


def _weights_allgather(shards):
    n = len(shards)

    def body(*refs):
        ins, outs = refs[:n], refs[n:2 * n]
        ssem, rsem = refs[2 * n:]
        mx, my, mc = _coords()
        me = 2 * mx + my
        sib = (mx, my, 1 - mc)
        chips = [(_flip(mx, j & 2), _flip(my, j & 1)) for j in range(1, 4)]
        sends = []
        for t in range(n):
            for j, (px, py) in enumerate(chips):
                cp = pltpu.make_async_remote_copy(
                    src_ref=ins[t].at[mc], dst_ref=outs[t].at[me, mc], send_sem=ssem.at[t, j],
                    recv_sem=rsem.at[t, j], device_id=(px, py, mc), device_id_type=MESH)
                cp.start()
                sends.append(cp)
        for t in range(n):
            for j, (px, py) in enumerate(chips):
                src = 2 * px + py
                pltpu.make_async_remote_copy(
                    src_ref=ins[t].at[mc], dst_ref=outs[t].at[src, mc], send_sem=ssem.at[t, j],
                    recv_sem=rsem.at[t, j], device_id=(px, py, mc), device_id_type=MESH).wait_recv()
                fwd = pltpu.make_async_remote_copy(
                    src_ref=outs[t].at[src, mc], dst_ref=outs[t].at[src, mc], send_sem=ssem.at[t, 3 + j],
                    recv_sem=rsem.at[t, 3 + j], device_id=sib, device_id_type=MESH)
                fwd.start()
                sends.append(fwd)
        for t in range(n):
            for j, (px, py) in enumerate(chips):
                src = 2 * px + py
                pltpu.make_async_remote_copy(
                    src_ref=outs[t].at[src, 1 - mc], dst_ref=outs[t].at[src, 1 - mc], send_sem=ssem.at[t, 3 + j],
                    recv_sem=rsem.at[t, 3 + j], device_id=sib, device_id_type=MESH).wait_recv()
        for cp in sends:
            cp.wait_send()

    return pl.pallas_call(
        body, name="weights_allgather",
        out_shape=[jax.ShapeDtypeStruct((4,) + s.shape, s.dtype) for s in shards],
        in_specs=[ANY] * n, out_specs=[ANY] * n,
        scratch_shapes=[pltpu.SemaphoreType.DMA((n, 6)), pltpu.SemaphoreType.DMA((n, 6))],
    )(*shards)


def _pair_exchange(grads):
    n = len(grads)

    def body(*refs):
        ins, outs = refs[:n], refs[n:2 * n]
        ssem, rsem = refs[2 * n:]
        mx, my, mc = _coords()
        sib = (mx, my, 1 - mc)
        sends = []
        for t in range(n):
            cp = pltpu.make_async_remote_copy(
                src_ref=ins[t].at[1 - mc], dst_ref=outs[t], send_sem=ssem.at[t], recv_sem=rsem.at[t],
                device_id=sib, device_id_type=MESH)
            cp.start()
            sends.append(cp)
        for cp in sends:
            cp.wait_recv()
        for cp in sends:
            cp.wait_send()

    return pl.pallas_call(
        body, name="grad_pair_exchange",
        out_shape=[jax.ShapeDtypeStruct(g.shape[1:], g.dtype) for g in grads],
        in_specs=[ANY] * n, out_specs=[ANY] * n,
        scratch_shapes=[pltpu.SemaphoreType.DMA((n,)), pltpu.SemaphoreType.DMA((n,))],
    )(*grads)


def _scatter_copies(ins, outs, ssem, rsem):
    mx, my, mc = _coords()
    me = 2 * mx + my
    mine, theirs = [], []
    for t in range(len(ins)):
        for j in range(1, 4):
            px, py = _flip(mx, j & 2), _flip(my, j & 1)
            mine.append(pltpu.make_async_remote_copy(
                src_ref=ins[t].at[2 * px + py], dst_ref=outs[t].at[me], send_sem=ssem.at[t, j - 1],
                recv_sem=rsem.at[t, j - 1], device_id=(px, py, mc), device_id_type=MESH))
            theirs.append(pltpu.make_async_remote_copy(
                src_ref=ins[t].at[me], dst_ref=outs[t].at[2 * px + py], send_sem=ssem.at[t, j - 1],
                recv_sem=rsem.at[t, j - 1], device_id=(px, py, mc), device_id_type=MESH))
    return mine, theirs


def _half_exchange(halves):
    n = len(halves)

    def body(*refs):
        ins, outs = refs[:n], refs[n:2 * n]
        ssem, rsem = refs[2 * n:]
        mx, my, mc = _coords()
        sib = (mx, my, 1 - mc)
        sends = []
        for t in range(n):
            rc = pltpu.make_async_remote_copy(
                src_ref=ins[t], dst_ref=outs[t], send_sem=ssem.at[t], recv_sem=rsem.at[t],
                device_id=sib, device_id_type=MESH)
            rc.start()
            sends.append(rc)
        for cp in sends:
            cp.wait_recv()
        for cp in sends:
            cp.wait_send()

    return pl.pallas_call(
        body, name="grad_half_exchange",
        out_shape=[jax.ShapeDtypeStruct(h.shape, h.dtype) for h in halves],
        in_specs=[ANY] * n, out_specs=[ANY] * n,
        scratch_shapes=[pltpu.SemaphoreType.DMA((n,)), pltpu.SemaphoreType.DMA((n,))],
    )(*halves)


def _row_tile(rows, cols, bytes_per_row_elem=4, budget=2 * 1024 * 1024):
    t = rows
    while t % 2 == 0 and t > 16 and t * cols * bytes_per_row_elem > budget:
        t //= 2
    return t


def _pair_add(core, g, r):
    _, _, rh, cc = g.shape
    tr = _row_tile(rh, cc)

    def body(core_ref, g_ref, r_ref, o_ref):
        o_ref[...] = (g_ref[...].astype(F32) + r_ref[...].astype(F32)).astype(o_ref.dtype)

    return pl.pallas_call(
        body, name="grad_pair_add",
        grid_spec=pltpu.PrefetchScalarGridSpec(
            num_scalar_prefetch=1, grid=(4, rh // tr),
            in_specs=[pl.BlockSpec((None, None, tr, cc), lambda k, i, c: (c[0], k, i, 0)),
                      pl.BlockSpec((None, tr, cc), lambda k, i, c: (k, i, 0))],
            out_specs=pl.BlockSpec((None, tr, cc), lambda k, i, c: (k, i, 0))),
        out_shape=jax.ShapeDtypeStruct(r.shape, BF16),
        compiler_params=_params(),
    )(core, g, r)


def _sum4(chip, p, q):
    _, rh, cc = q.shape
    tr = _row_tile(rh, cc)

    def body(chip_ref, p_ref, q1_ref, q2_ref, q3_ref, o_ref):
        o_ref[...] = ((p_ref[...].astype(F32) + q1_ref[...].astype(F32)) + q2_ref[...].astype(F32)) + q3_ref[...].astype(F32)

    def other(j):
        return pl.BlockSpec((None, tr, cc), lambda i, c: (jnp.bitwise_xor(c[0], j), i, 0))

    return pl.pallas_call(
        body, name="grad_sum4",
        grid_spec=pltpu.PrefetchScalarGridSpec(
            num_scalar_prefetch=1, grid=(rh // tr,),
            in_specs=[pl.BlockSpec((None, tr, cc), lambda i, c: (c[0], i, 0)), other(1), other(2), other(3)],
            out_specs=pl.BlockSpec((tr, cc), lambda i, c: (i, 0))),
        out_shape=jax.ShapeDtypeStruct((rh, cc), F32),
        compiler_params=_params(),
    )(chip, p, q, q, q)


def _adamw_math(w, g, m, v):
    m = ADAM_B1 * m + (1.0 - ADAM_B1) * g
    v = ADAM_B2 * v + (1.0 - ADAM_B2) * (g * g)
    m_hat = m / (1.0 - ADAM_B1 ** ADAM_STEP)
    v_hat = v / (1.0 - ADAM_B2 ** ADAM_STEP)
    delta = -ADAM_LR * (m_hat / (jnp.sqrt(v_hat) + ADAM_EPS) + ADAM_WD * w)
    return delta, m, v


def _adamw_halves(core, w, mine, recv, m, v, name):
    rows, cols = w.shape
    rh = rows // 2
    tr = _row_tile(rh, cols, budget=1024 * 1024)
    nh = rh // tr

    def body(core_ref, w_ref, a_ref, b_ref, m_ref, v_ref, g_ref, d_ref, mo_ref, vo_ref):
        g = jnp.where(pl.program_id(0) == core_ref[0], a_ref[...], b_ref[...])
        d, mn, vn = _adamw_math(w_ref[...], g, m_ref[...], v_ref[...])
        g_ref[...] = g
        d_ref[...] = d
        mo_ref[...] = mn
        vo_ref[...] = vn

    full = pl.BlockSpec((tr, cols), lambda h, i, c: (h * nh + i, 0))
    half = pl.BlockSpec((tr, cols), lambda h, i, c: (i, 0))
    sd = jax.ShapeDtypeStruct((rows, cols), F32)
    return pl.pallas_call(
        body, name=name,
        grid_spec=pltpu.PrefetchScalarGridSpec(
            num_scalar_prefetch=1, grid=(2, nh),
            in_specs=[full, half, half, full, full], out_specs=[full] * 4),
        out_shape=[sd, sd, sd, sd], compiler_params=_params(),
    )(core, w, mine, recv, m, v)


def _mod_part(c_all, w_ada, b_shard):
    cols = w_ada.shape[1]

    def body(c_ref, w_ref, b_ref, o_ref, sc_ref):
        cv = c_ref[...]
        sc = cv * _sigmoid(cv)
        sc_ref[...] = sc
        o_ref[...] = jnp.dot(sc, w_ref[...], preferred_element_type=F32,
                             precision=lax.Precision.HIGHEST) + b_ref[...]

    return pl.pallas_call(
        body, name="adaln_mod",
        out_shape=[jax.ShapeDtypeStruct((8, cols), F32), jax.ShapeDtypeStruct(c_all.shape, F32)],
        in_specs=[pl.BlockSpec(memory_space=pltpu.VMEM)] * 3,
        out_specs=[pl.BlockSpec(memory_space=pltpu.VMEM)] * 2, compiler_params=_params(),
    )(c_all, w_ada, b_shard)


def _inproj_nat(x, mod, norm_w, w, *, tm=1024, tn=1664):
    S, D = x.shape
    N = w.shape[1]
    chunk = 256

    def body(x_ref, mod_ref, nw_ref, w_ref, proj_ref, h_ref, ht_ref):
        @pl.when(pl.program_id(1) == 0)
        def _():
            shift = mod_ref[:, 0:D]
            scale1 = 1.0 + mod_ref[:, D:2 * D]
            for r in range(tm // chunk):
                xv = x_ref[pl.ds(r * chunk, chunk), :]
                ms = jnp.mean(xv * xv, axis=-1, keepdims=True)
                h = (xv * lax.rsqrt(ms + EPS) * nw_ref[...]) * scale1 + shift
                h_ref[pl.ds(r * chunk, chunk), :] = h.astype(BF16)
                ht_ref[:, pl.ds(r * chunk, chunk)] = h.T.astype(BF16)
        proj_ref[...] = jnp.dot(h_ref[...], w_ref[...], preferred_element_type=F32).astype(BF16)

    return pl.pallas_call(
        body, name="inproj_nat", grid=(S // tm, N // tn),
        in_specs=[pl.BlockSpec((tm, D), lambda i, j: (i, 0)),
                  pl.BlockSpec((1, 3 * D), lambda i, j: (0, 0)),
                  pl.BlockSpec((1, D), lambda i, j: (0, 0)),
                  pl.BlockSpec((D, tn), lambda i, j: (0, j))],
        out_specs=[pl.BlockSpec((tm, tn), lambda i, j: (i, j)),
                   pl.BlockSpec((tm, D), lambda i, j: (i, 0)),
                   pl.BlockSpec((D, tm), lambda i, j: (0, i))],
        out_shape=[jax.ShapeDtypeStruct((S, N), BF16), jax.ShapeDtypeStruct((S, D), BF16),
                   jax.ShapeDtypeStruct((D, S), BF16)],
        compiler_params=_params(),
    )(x, mod, norm_w, w)


def _matmul(a, b, *, tm, tn, name):
    M, K = a.shape
    N = b.shape[1]

    def body(a_ref, b_ref, o_ref):
        o_ref[...] = jnp.dot(a_ref[...], b_ref[...], preferred_element_type=F32).astype(o_ref.dtype)

    return pl.pallas_call(
        body, name=name, grid=(M // tm, N // tn),
        in_specs=[pl.BlockSpec((tm, K), lambda i, j: (i, 0)), pl.BlockSpec((K, tn), lambda i, j: (0, j))],
        out_specs=pl.BlockSpec((tm, tn), lambda i, j: (i, j)),
        out_shape=jax.ShapeDtypeStruct((M, N), BF16), compiler_params=_params(),
    )(a, b)


def _matmul_acc(a, b, *, tn, tk, name):
    M, K = a.shape
    N = b.shape[1]
    nk = K // tk

    def body(a_ref, b_ref, o_ref, acc_ref):
        k = pl.program_id(1)

        @pl.when(k == 0)
        def _():
            acc_ref[...] = jnp.zeros_like(acc_ref)

        acc_ref[...] += jnp.dot(a_ref[...], b_ref[...], preferred_element_type=F32)

        @pl.when(k == nk - 1)
        def _():
            o_ref[...] = acc_ref[...].astype(o_ref.dtype)

    return pl.pallas_call(
        body, name=name, grid=(N // tn, nk),
        in_specs=[pl.BlockSpec((M, tk), lambda j, k: (0, k)), pl.BlockSpec((tk, tn), lambda j, k: (k, j))],
        out_specs=pl.BlockSpec((M, tn), lambda j, k: (0, j)),
        out_shape=jax.ShapeDtypeStruct((M, N), BF16),
        scratch_shapes=[pltpu.VMEM((M, tn), F32)], compiler_params=_params(),
    )(a, b)


SHARD_COLS = 2816


def _column_tables(part):
    if part == "nat":
        bw = 256
        orig = [j * bw if j < 16 else 8704 + (j - 16) * bw for j in range(NAT // bw)]
    else:
        bw = 128
        orig = []
        for jj in range(NQKV // bw):
            g, p, t = jj // 12, (jj % 12) // 3, jj % 3
            orig.append(4096 + t * 1536 + g * 512 + p * 128)
    tk = jnp.array([o // SHARD_COLS for o in orig], jnp.int32)
    tc = jnp.array([(o % SHARD_COLS) // bw for o in orig], jnp.int32)
    return tk, tc, bw


def _weights_prep(gath, own, chipv, part):
    D = D_MODEL
    tk, tc, bw = _column_tables(part)
    n = tk.shape[0]

    def body(tk_ref, tc_ref, chip_ref, g_ref, o_ref, w_ref, wt_ref):
        blk = jnp.where(tk_ref[pl.program_id(0)] == chip_ref[0], o_ref[...], g_ref[...])
        w_ref[...] = blk
        wt_ref[...] = blk.astype(F32).T.astype(BF16)

    def gath_idx(j, tk, tc, ch):
        k = tk[j]
        return (jnp.where(k == ch[0], (k + 1) % 4, k), 0, tc[j])

    return pl.pallas_call(
        body, name="weights_prep_" + part,
        grid_spec=pltpu.PrefetchScalarGridSpec(
            num_scalar_prefetch=3, grid=(n,),
            in_specs=[pl.BlockSpec((None, D, bw), gath_idx),
                      pl.BlockSpec((D, bw), lambda j, tk, tc, ch: (0, tc[j]))],
            out_specs=[pl.BlockSpec((D, bw), lambda j, tk, tc, ch: (0, j)),
                       pl.BlockSpec((bw, D), lambda j, tk, tc, ch: (j, 0))]),
        out_shape=[jax.ShapeDtypeStruct((D, n * bw), BF16), jax.ShapeDtypeStruct((n * bw, D), BF16)],
        compiler_params=_params(),
    )(tk, tc, chipv, gath, own)


def _grad_relayout(gw, into, part):
    D = D_MODEL
    tk, tc, bw = _column_tables(part)
    n = tk.shape[0]

    def body(tk_ref, tc_ref, g_ref, into_ref, o_ref):
        o_ref[0] = g_ref[0:D // 2, :]
        o_ref[1] = g_ref[D // 2:D, :]

    return pl.pallas_call(
        body, name="grad_relayout_" + part,
        grid_spec=pltpu.PrefetchScalarGridSpec(
            num_scalar_prefetch=2, grid=(n,),
            in_specs=[pl.BlockSpec((D, bw), lambda j, tk, tc: (0, j)), ANY],
            out_specs=pl.BlockSpec((2, None, D // 2, bw), lambda j, tk, tc: (0, tk[j], 0, tc[j]))),
        out_shape=jax.ShapeDtypeStruct(into.shape, BF16),
        input_output_aliases={3: 0},
        compiler_params=_params(),
    )(tk, tc, gw, into)


def _head_norm(xb, w, ebd):
    x = xb.astype(F32)
    ss = jnp.dot((x * x).astype(BF16), ebd, preferred_element_type=F32)
    return x * lax.rsqrt(ss * (1.0 / HEAD_DIM) + EPS) * w


def _window_mask(no_prev):
    qi = lax.broadcasted_iota(jnp.int32, (128, 256), 0)
    kc = lax.broadcasted_iota(jnp.int32, (128, 256), 1)
    ok = (kc >= qi) & (kc <= qi + 128)
    if no_prev is not None:
        ok = ok & ((kc >= 128) | jnp.logical_not(no_prev))
    return ok


def _lane_masks():
    lane = lax.broadcasted_iota(jnp.int32, (1, 128), 1)
    return [(lane < HEAD_DIM).astype(F32), (lane >= HEAD_DIM).astype(F32)]


def _head_col(v, mh):
    return jnp.sum(v * mh, axis=1, keepdims=True) * (1.0 / HEAD_DIM)


def _attn_fwd(projq, qnw2, knw2, ebd, g):
    S = projq.shape[0]
    d = DILATIONS[g]
    T = ATT_TILE
    nt = S // T
    nb = T // (128 * d)

    def body(cur_ref, prev_ref, qw_ref, kw_ref, ebd_ref, o_ref, lse_ref, q32, k32, v32):
        t = pl.program_id(0)
        e = ebd_ref[...]
        q32[...] = _head_norm(cur_ref[:, 0:128], qw_ref[...], e) * QK_SCALE
        k32[pl.ds(0, T), :] = _head_norm(prev_ref[:, 128:256], kw_ref[...], e)
        k32[pl.ds(T, T), :] = _head_norm(cur_ref[:, 128:256], kw_ref[...], e)
        v32[pl.ds(0, T), :] = prev_ref[:, 256:384].astype(F32)
        v32[pl.ds(T, T), :] = cur_ref[:, 256:384].astype(F32)
        m0, m1 = _lane_masks()
        first = t == 0
        for r in range(d):
            for bb in range(nb):
                q0 = r + bb * 128 * d
                k0 = T + r + (bb - 1) * 128 * d
                qs = q32[pl.ds(q0, 128, stride=d), :]
                kb = k32[pl.ds(k0, 256, stride=d), :].astype(BF16)
                vb = v32[pl.ds(k0, 256, stride=d), :].astype(BF16)
                ok = _window_mask(first if bb == 0 else None)
                o_t = None
                l_t = None
                for mh in (m0, m1):
                    s = lax.dot_general((qs * mh).astype(BF16), kb, (((1,), (1,)), ((), ())),
                                        preferred_element_type=F32)
                    s = jnp.where(ok, s, NEG_INF)
                    mx = jnp.max(s, axis=-1, keepdims=True)
                    p = jnp.exp(s - mx)
                    l = jnp.sum(p, axis=-1, keepdims=True)
                    o = jnp.dot(p.astype(BF16), vb, preferred_element_type=F32) / l
                    lse = mx + jnp.log(l)
                    o_t = o * mh if o_t is None else o_t + o * mh
                    l_t = lse * mh if l_t is None else l_t + lse * mh
                o_ref[pl.ds(q0, 128, stride=d), :] = o_t
                lse_ref[pl.ds(q0, 128, stride=d), :] = l_t

    return pl.pallas_call(
        body, name=f"attn_fwd_g{g}", grid=(nt, 4),
        in_specs=[pl.BlockSpec((T, 384), lambda t, p: (t, g * 4 + p)),
                  pl.BlockSpec((T, 384), lambda t, p: (jnp.maximum(t - 1, 0), g * 4 + p)),
                  pl.BlockSpec((1, 128), lambda t, p: (0, 0)),
                  pl.BlockSpec((1, 128), lambda t, p: (0, 0)),
                  pl.BlockSpec((128, 128), lambda t, p: (0, 0))],
        out_specs=[pl.BlockSpec((T, 128), lambda t, p: (t, p)), pl.BlockSpec((T, 128), lambda t, p: (t, p))],
        out_shape=[jax.ShapeDtypeStruct((S, ATTN_OUT), F32), jax.ShapeDtypeStruct((S, ATTN_OUT), F32)],
        scratch_shapes=[pltpu.VMEM((T, 128), F32), pltpu.VMEM((2 * T, 128), F32), pltpu.VMEM((2 * T, 128), F32)],
        compiler_params=_params(),
    )(projq, projq, qnw2, knw2, ebd)


def _attn_bwd(projq, dprojq, d_o, lse, delta, qnw2, knw2, ebd, g):
    S = projq.shape[0]
    d = DILATIONS[g]
    T = ATT_TILE
    nt = S // T
    nb = T // (128 * d)

    def norm_bwd(raw_b, dy, w, e):
        x = raw_b.astype(F32)
        ss = jnp.dot((x * x).astype(BF16), e, preferred_element_type=F32)
        rstd = lax.rsqrt(ss * (1.0 / HEAD_DIM) + EPS)
        xh = x * rstd
        gw = jnp.sum(dy * xh, axis=0, keepdims=True)
        dxh = dy * w
        mean = jnp.dot((dxh * xh).astype(BF16), e, preferred_element_type=F32) * (1.0 / HEAD_DIM)
        return rstd * (dxh - xh * mean), gw

    def body(cur_ref, prev_ref, do_ref, lse_ref, dl_ref, qw_ref, kw_ref, ebd_ref, dp_in_ref,
             out_ref, gqk_ref, q32, k32, v32, do32, dq_cur, dq_prev, dk_acc, dv_acc):
        i = pl.program_id(1)
        e = ebd_ref[...]
        m0, m1 = _lane_masks()

        @pl.when(i == 0)
        def _():
            dk_acc[...] = jnp.zeros_like(dk_acc)
            dv_acc[...] = jnp.zeros_like(dv_acc)
            dq_prev[...] = jnp.zeros_like(dq_prev)
            gqk_ref[...] = jnp.zeros_like(gqk_ref)

        @pl.when(i < nt)
        def _():
            q32[...] = _head_norm(cur_ref[:, 0:128], qw_ref[...], e) * QK_SCALE
            k32[pl.ds(0, T), :] = _head_norm(prev_ref[:, 128:256], kw_ref[...], e)
            k32[pl.ds(T, T), :] = _head_norm(cur_ref[:, 128:256], kw_ref[...], e)
            v32[pl.ds(0, T), :] = prev_ref[:, 256:384].astype(F32)
            v32[pl.ds(T, T), :] = cur_ref[:, 256:384].astype(F32)
            do32[...] = do_ref[...].astype(F32)
            first = i == 0
            for r in range(d):
                for bb in range(nb):
                    q0 = r + bb * 128 * d
                    k0 = T + r + (bb - 1) * 128 * d
                    qs = q32[pl.ds(q0, 128, stride=d), :]
                    dos = do32[pl.ds(q0, 128, stride=d), :]
                    ls = lse_ref[pl.ds(q0, 128, stride=d), :]
                    dls = dl_ref[pl.ds(q0, 128, stride=d), :]
                    kb = k32[pl.ds(k0, 256, stride=d), :].astype(BF16)
                    vb = v32[pl.ds(k0, 256, stride=d), :].astype(BF16)
                    ok = _window_mask(first if bb == 0 else None)
                    dq_t = None
                    dk_t = None
                    dv_t = None
                    for mh in (m0, m1):
                        qh = (qs * mh).astype(BF16)
                        doh = (dos * mh).astype(BF16)
                        s = lax.dot_general(qh, kb, (((1,), (1,)), ((), ())), preferred_element_type=F32)
                        s = jnp.where(ok, s, NEG_INF)
                        p = jnp.exp(s - _head_col(ls, mh))
                        dp = lax.dot_general(doh, vb, (((1,), (1,)), ((), ())), preferred_element_type=F32)
                        ds = (p * (dp - _head_col(dls, mh))).astype(BF16)
                        dq = jnp.dot(ds, kb, preferred_element_type=F32) * mh
                        dk = lax.dot_general(ds, qh, (((0,), (0,)), ((), ())), preferred_element_type=F32)
                        dv = lax.dot_general(p.astype(BF16), doh, (((0,), (0,)), ((), ())),
                                             preferred_element_type=F32)
                        dq_t = dq if dq_t is None else dq_t + dq
                        dk_t = dk if dk_t is None else dk_t + dk
                        dv_t = dv if dv_t is None else dv_t + dv
                    dq_cur[pl.ds(q0, 128, stride=d), :] = dq_t
                    dk_acc[pl.ds(k0, 256, stride=d), :] = dk_acc[pl.ds(k0, 256, stride=d), :] + dk_t
                    dv_acc[pl.ds(k0, 256, stride=d), :] = dv_acc[pl.ds(k0, 256, stride=d), :] + dv_t

        @pl.when(i >= 1)
        def _():
            dq_raw, gq = norm_bwd(prev_ref[:, 0:128], dq_prev[...] * QK_SCALE, qw_ref[...], e)
            dk_raw, gk = norm_bwd(prev_ref[:, 128:256], dk_acc[pl.ds(0, T), :], kw_ref[...], e)
            out_ref[:, 0:128] = dq_raw.astype(BF16)
            out_ref[:, 128:256] = dk_raw.astype(BF16)
            out_ref[:, 256:384] = dv_acc[pl.ds(0, T), :].astype(BF16)
            gqk_ref[0:1, :] += gq
            gqk_ref[1:2, :] += gk

        dq_prev[...] = dq_cur[...]
        dk_acc[pl.ds(0, T), :] = dk_acc[pl.ds(T, T), :]
        dv_acc[pl.ds(0, T), :] = dv_acc[pl.ds(T, T), :]
        dk_acc[pl.ds(T, T), :] = jnp.zeros((T, 128), F32)
        dv_acc[pl.ds(T, T), :] = jnp.zeros((T, 128), F32)

    last = nt - 1
    qtile = lambda p, i: (jnp.minimum(i, last), p)
    return pl.pallas_call(
        body, name=f"attn_bwd_g{g}", grid=(4, nt + 1),
        in_specs=[pl.BlockSpec((T, 384), lambda p, i: (jnp.minimum(i, last), g * 4 + p)),
                  pl.BlockSpec((T, 384), lambda p, i: (jnp.maximum(i - 1, 0), g * 4 + p)),
                  pl.BlockSpec((T, 128), qtile), pl.BlockSpec((T, 128), qtile), pl.BlockSpec((T, 128), qtile),
                  pl.BlockSpec((1, 128), lambda p, i: (0, 0)),
                  pl.BlockSpec((1, 128), lambda p, i: (0, 0)),
                  pl.BlockSpec((128, 128), lambda p, i: (0, 0)),
                  ANY],
        out_specs=[pl.BlockSpec((T, 384), lambda p, i: (jnp.maximum(i - 1, 0), g * 4 + p)),
                   pl.BlockSpec((None, 8, 128), lambda p, i: (p, 0, 0))],
        out_shape=[jax.ShapeDtypeStruct(dprojq.shape, BF16), jax.ShapeDtypeStruct((4, 8, 128), F32)],
        scratch_shapes=[pltpu.VMEM((T, 128), F32), pltpu.VMEM((2 * T, 128), F32), pltpu.VMEM((2 * T, 128), F32),
                        pltpu.VMEM((T, 128), F32), pltpu.VMEM((T, 128), F32), pltpu.VMEM((T, 128), F32),
                        pltpu.VMEM((2 * T, 128), F32), pltpu.VMEM((2 * T, 128), F32)],
        input_output_aliases={8: 0},
        compiler_params=_params(),
    )(projq, projq, d_o, lse, delta, qnw2, knw2, ebd, dprojq)


def _mid(projn, o_g, lse_g, x, tgt, mod, convw8, wc, wct, wa, wat, wo, wot, ebd, *, tm=128):
    S, D = x.shape
    nt = S // tm
    hb = tm // 16

    def body(pn_ref, hc_ref, hx_ref, o0_ref, o1_ref, o2_ref, l0_ref, l1_ref, l2_ref, x_ref, t_ref, mod_ref,
             cw_ref, ebd_ref, wc_hbm, wct_hbm, wa_hbm, wat_hbm, wo_hbm, wot_hbm,
             dpn_ref, do_ref, lse_ref, dl_ref, dout_ref, vacc_ref, gwo_hbm, gwc_hbm, gwa_hbm,
             wc_s, wct_s, wa_s, wat_s, wo_s, wot_s, gwo_s, gwc_s, gwa_s, carry_s):
        i = pl.program_id(0)
        ti = nt - 1 - i

        @pl.when(i == 0)
        def _():
            for src, dst in ((wc_hbm, wc_s), (wct_hbm, wct_s), (wa_hbm, wa_s), (wat_hbm, wat_s),
                             (wo_hbm, wo_s), (wot_hbm, wot_s)):
                pltpu.sync_copy(src, dst)
            gwo_s[...] = jnp.zeros_like(gwo_s)
            gwc_s[...] = jnp.zeros_like(gwc_s)
            gwa_s[...] = jnp.zeros_like(gwa_s)
            carry_s[...] = jnp.zeros_like(carry_s)
            vacc_ref[...] = jnp.zeros_like(vacc_ref)

        rows = lax.broadcasted_iota(jnp.int32, (tm, D), 0)
        w0, w1, w2 = cw_ref[0:1, :], cw_ref[1:2, :], cw_ref[2:3, :]
        gate = mod_ref[:, 2 * D:3 * D]

        b_a = pn_ref[:, BA:BA + D].astype(F32)
        c_a = pn_ref[:, CA:CA + D].astype(F32)
        x_a = pn_ref[:, XA:XA + D].astype(F32)
        z_a = pn_ref[:, ZA:ZA + D].astype(F32)
        u = c_a * x_a
        has_prev = (ti > 0).astype(F32)
        uh = hc_ref[...].astype(F32) * hx_ref[...].astype(F32) * has_prev
        h14, h15 = _row(uh, 14), _row(uh, 15)
        u_m1 = jnp.where(rows == 0, h15, pltpu.roll(u, 1, 0))
        u_m2 = jnp.where(rows == 0, h14, jnp.where(rows == 1, h15, pltpu.roll(u, 2, 0)))
        conv = w0 * u_m2 + w1 * u_m1 + w2 * u
        sg_a = _sigmoid(z_a)
        sz_a = z_a * sg_a
        y_a = b_a * conv * sz_a
        y_ab = y_a.astype(BF16)
        pa = jnp.dot(y_ab, wc_s[...], preferred_element_type=F32)

        l0, l1, l2 = l0_ref[...], l1_ref[...], l2_ref[...]
        mxl = jnp.maximum(jnp.maximum(l0, l1), l2)
        e0, e1, e2 = jnp.exp(l0 - mxl), jnp.exp(l1 - mxl), jnp.exp(l2 - mxl)
        den = e0 + e1 + e2
        attn = (e0 * o0_ref[...] + e1 * o1_ref[...] + e2 * o2_ref[...]) / den
        lse_ref[...] = mxl + jnp.log(den)
        z_b = pn_ref[:, ZB:ZB + ATTN_OUT].astype(F32)
        sg_b = _sigmoid(z_b)
        sz_b = z_b * sg_b
        y_b = attn * sz_b
        y_bb = y_b.astype(BF16)
        pb = jnp.dot(y_bb, wa_s[...], preferred_element_type=F32)

        s_a = _sigmoid(pn_ref[:, GA:GA + D].astype(F32))
        s_b = _sigmoid(pn_ref[:, GB:GB + D].astype(F32))
        merged = s_a * pa + s_b * pb
        merged_b = merged.astype(BF16)
        mo = jnp.dot(merged_b, wo_s[...], preferred_element_type=F32)
        diff = x_ref[...] + gate * mo - t_ref[...]
        dout = diff * (1.0 / D)
        dout_ref[...] = dout
        vacc_ref[4:5, :] += jnp.sum(diff * diff, axis=0, keepdims=True)
        vacc_ref[0:1, :] += jnp.sum(dout * mo, axis=0, keepdims=True)

        d_mo = (dout * gate).astype(BF16)
        gwo_s[...] += lax.dot_general(merged_b, d_mo, (((0,), (0,)), ((), ())), preferred_element_type=F32)
        dmerged = jnp.dot(d_mo, wot_s[...], preferred_element_type=F32)
        dpa = (dmerged * s_a).astype(BF16)
        dpb = (dmerged * s_b).astype(BF16)
        dpn_ref[:, GA:GA + D] = (dmerged * pa * s_a * (1.0 - s_a)).astype(BF16)
        dpn_ref[:, GB:GB + D] = (dmerged * pb * s_b * (1.0 - s_b)).astype(BF16)
        gwc_s[...] += lax.dot_general(y_ab, dpa, (((0,), (0,)), ((), ())), preferred_element_type=F32)
        gwa_s[...] += lax.dot_general(y_bb, dpb, (((0,), (0,)), ((), ())), preferred_element_type=F32)
        dy_a = jnp.dot(dpa, wct_s[...], preferred_element_type=F32)
        dy_b = jnp.dot(dpb, wat_s[...], preferred_element_type=F32)

        dattn = dy_b * sz_b
        dpn_ref[:, ZB:ZB + ATTN_OUT] = (dy_b * attn * sg_b * (1.0 + z_b * (1.0 - sg_b))).astype(BF16)
        do_ref[...] = dattn.astype(BF16)
        dl_ref[...] = jnp.dot((dattn * attn).astype(BF16), ebd_ref[...], preferred_element_type=F32)

        dpn_ref[:, BA:BA + D] = (dy_a * conv * sz_a).astype(BF16)
        dpn_ref[:, ZA:ZA + D] = (dy_a * b_a * conv * sg_a * (1.0 + z_a * (1.0 - sg_a))).astype(BF16)
        dconv = dy_a * b_a * sz_a
        vacc_ref[1:2, :] += jnp.sum(dconv * u_m2, axis=0, keepdims=True)
        vacc_ref[2:3, :] += jnp.sum(dconv * u_m1, axis=0, keepdims=True)
        vacc_ref[3:4, :] += jnp.sum(dconv * u, axis=0, keepdims=True)
        nxt = carry_s[...]
        n0, n1 = _row(nxt, 0), _row(nxt, 1)
        dc_p1 = jnp.where(rows == tm - 1, n0, pltpu.roll(dconv, tm - 1, 0))
        dc_p2 = jnp.where(rows == tm - 1, n1, jnp.where(rows == tm - 2, n0, pltpu.roll(dconv, tm - 2, 0)))
        du = w2 * dconv + w1 * dc_p1 + w0 * dc_p2
        carry_s[...] = dconv[0:8, :]
        dpn_ref[:, CA:CA + D] = (du * x_a).astype(BF16)
        dpn_ref[:, XA:XA + D] = (du * c_a).astype(BF16)

        @pl.when(i == nt - 1)
        def _():
            pltpu.sync_copy(gwo_s, gwo_hbm)
            pltpu.sync_copy(gwc_s, gwc_hbm)
            pltpu.sync_copy(gwa_s, gwa_hbm)

    rev = lambda i: (nt - 1 - i, 0)
    halo = lambda col: pl.BlockSpec((16, D), lambda i: (jnp.maximum((nt - 1 - i) * hb - 1, 0), col))
    tile512 = pl.BlockSpec((tm, ATTN_OUT), rev)
    tileD = pl.BlockSpec((tm, D), rev)
    const = lambda shape: pl.BlockSpec(shape, lambda i: (0, 0))
    return pl.pallas_call(
        body, name="mid_fwd_bwd", grid=(nt,),
        in_specs=[pl.BlockSpec((tm, NAT), rev), halo(CA // D), halo(XA // D)] + [tile512] * 6
                 + [tileD, tileD, const((1, 3 * D)), const((8, D)), const((ATTN_OUT, ATTN_OUT))] + [ANY] * 6,
        out_specs=[pl.BlockSpec((tm, NAT), rev), tile512, tile512, tile512, tileD, const((8, D)), ANY, ANY, ANY],
        out_shape=[jax.ShapeDtypeStruct((S, NAT), BF16), jax.ShapeDtypeStruct((S, ATTN_OUT), BF16),
                   jax.ShapeDtypeStruct((S, ATTN_OUT), F32), jax.ShapeDtypeStruct((S, ATTN_OUT), F32),
                   jax.ShapeDtypeStruct((S, D), F32), jax.ShapeDtypeStruct((8, D), F32),
                   jax.ShapeDtypeStruct((D, D), F32), jax.ShapeDtypeStruct((D, D), F32),
                   jax.ShapeDtypeStruct((ATTN_OUT, D), F32)],
        scratch_shapes=[pltpu.VMEM((D, D), BF16), pltpu.VMEM((D, D), BF16), pltpu.VMEM((ATTN_OUT, D), BF16),
                        pltpu.VMEM((D, ATTN_OUT), BF16), pltpu.VMEM((D, D), BF16), pltpu.VMEM((D, D), BF16),
                        pltpu.VMEM((D, D), F32), pltpu.VMEM((D, D), F32), pltpu.VMEM((ATTN_OUT, D), F32),
                        pltpu.VMEM((8, D), F32)],
        compiler_params=_params(),
    )(projn, projn, projn, *o_g, *lse_g, x, tgt, mod, convw8, ebd, wc, wct, wa, wat, wo, wot)


def _dh_norm(dpn, dpq, wnt, wqt, x, dout, mod, norm_w, parts, *, tm=256):
    S, D = x.shape
    n = len(parts)
    nt = S // tm

    def body(dn_ref, dq_ref, x_ref, dout_ref, mod_ref, nw_ref, wn_hbm, wq_hbm, *rest):
        p_hbm, rest = rest[:n], rest[n:]
        gx_ref, st_ref = rest[0], rest[1]
        q_hbm = rest[2:2 + n]
        wn_s, wq_s = rest[2 + n:4 + n]
        sems = rest[4 + n:]

        @pl.when(pl.program_id(0) == 0)
        def _():
            if n:
                mine, _ = _scatter_copies(p_hbm, q_hbm, *sems)
                for cp in mine:
                    cp.start()
            pltpu.sync_copy(wn_hbm, wn_s)
            pltpu.sync_copy(wq_hbm, wq_s)
            st_ref[...] = jnp.zeros_like(st_ref)

        if n:
            @pl.when(pl.program_id(0) == nt - 1)
            def _():
                mine, theirs = _scatter_copies(p_hbm, q_hbm, *sems)
                for cp in theirs:
                    cp.wait_recv()
                for cp in mine:
                    cp.wait_send()

        dh = (jnp.dot(dn_ref[...], wn_s[...], preferred_element_type=F32)
              + jnp.dot(dq_ref[...], wq_s[...], preferred_element_type=F32))
        scale1 = 1.0 + mod_ref[:, D:2 * D]
        nw = nw_ref[...]
        xv = x_ref[...]
        rstd = lax.rsqrt(jnp.mean(xv * xv, axis=-1, keepdims=True) + EPS)
        xh = xv * rstd
        dhs = dh * scale1
        dxh = dhs * nw
        dx = rstd * (dxh - xh * jnp.mean(dxh * xh, axis=-1, keepdims=True))
        gx_ref[...] = dout_ref[...] + dx
        st_ref[0:1, :] += jnp.sum(dh, axis=0, keepdims=True)
        st_ref[1:2, :] += jnp.sum(dh * (xh * nw), axis=0, keepdims=True)
        st_ref[2:3, :] += jnp.sum(dhs * xh, axis=0, keepdims=True)

    row = lambda w: pl.BlockSpec((tm, w), lambda i: (i, 0))
    outs = pl.pallas_call(
        body, name="dh_norm_bwd", grid=(nt,),
        in_specs=[row(NAT), row(NQKV), row(D), row(D),
                  pl.BlockSpec((1, 3 * D), lambda i: (0, 0)), pl.BlockSpec((1, D), lambda i: (0, 0)), ANY, ANY]
                 + [ANY] * n,
        out_specs=[row(D), pl.BlockSpec((8, D), lambda i: (0, 0))] + [ANY] * n,
        out_shape=[jax.ShapeDtypeStruct((S, D), F32), jax.ShapeDtypeStruct((8, D), F32)]
                  + [jax.ShapeDtypeStruct(p.shape, p.dtype) for p in parts],
        scratch_shapes=[pltpu.VMEM((NAT, D), BF16), pltpu.VMEM((NQKV, D), BF16)]
                       + ([pltpu.SemaphoreType.DMA((n, 3)), pltpu.SemaphoreType.DMA((n, 3))] if n else []),
        compiler_params=_params(),
    )(dpn, dpq, x, dout, mod, norm_w, wnt, wqt, *parts)
    return outs[0], outs[1], outs[2:]


def _pack_smalls(st, vacc, gqk):
    D = D_MODEL

    def body(st_ref, va_ref, gqk_ref, o_ref):
        o_ref[...] = jnp.zeros_like(o_ref)
        o_ref[0:2, :] = st_ref[0:2, :]
        o_ref[2:3, :] = va_ref[0:1, :]
        o_ref[8:9, :] = st_ref[2:3, :]
        o_ref[16:19, :] = va_ref[1:4, :]
        tot = gqk_ref[0]
        for k in range(1, gqk_ref.shape[0]):
            tot = tot + gqk_ref[k]
        tot = tot + pltpu.roll(tot, HEAD_DIM, 1)
        o_ref[24:32, 0:128] = tot
        loss = jnp.sum(va_ref[4:5, :], axis=1, keepdims=True) * (0.5 / D)
        o_ref[26:27, :] = jnp.broadcast_to(loss, (1, D))

    return pl.pallas_call(
        body, name="pack_smalls", out_shape=jax.ShapeDtypeStruct((32, D), F32),
        in_specs=[pl.BlockSpec(memory_space=pltpu.VMEM)] * 3,
        out_specs=pl.BlockSpec(memory_space=pltpu.VMEM), compiler_params=_params(),
    )(st, vacc, gqk)


def _small_update(sm_loc, dm_pad, ct_pad, w_ada, m_ada, v_ada, vec_w, vec_m, vec_v):
    D = D_MODEL
    cols = w_ada.shape[1]

    def body(sm_ref, dm_ref, ct_ref, w_ref, m_ref, v_ref, vw_ref, vm_ref, vv_ref,
             gw_ref, dw_ref, mw_ref, vw_o_ref, gv_ref, dv_ref, mv_ref, vv_o_ref):
        g = sm_ref[0]
        for k in range(1, 8):
            g = g + sm_ref[k]
        gv_ref[...] = g
        dv, mv, vv = _adamw_math(vw_ref[...], g, vm_ref[...], vv_ref[...])
        dv_ref[...] = dv
        mv_ref[...] = mv
        vv_o_ref[...] = vv
        gw = jnp.dot(ct_ref[...], dm_ref[...], preferred_element_type=F32, precision=lax.Precision.HIGHEST)
        gw_ref[...] = gw
        dw, mw, vw = _adamw_math(w_ref[...], gw, m_ref[...], v_ref[...])
        dw_ref[...] = dw
        mw_ref[...] = mw
        vw_o_ref[...] = vw

    sw = jax.ShapeDtypeStruct((D, cols), F32)
    sv = jax.ShapeDtypeStruct((32, D), F32)
    return pl.pallas_call(
        body, name="small_update", out_shape=[sw, sw, sw, sw, sv, sv, sv, sv],
        in_specs=[pl.BlockSpec(memory_space=pltpu.VMEM)] * 9,
        out_specs=[pl.BlockSpec(memory_space=pltpu.VMEM)] * 8, compiler_params=_params(),
    )(sm_loc, dm_pad, ct_pad, w_ada, m_ada, v_ada, vec_w, vec_m, vec_v)


def _local_step(xs, tg, mod, norm_w, w_nat, w_qkv, wc, wa, wo, convw8, q_norm_w, k_norm_w):
    S = xs.shape[0]
    lane = jnp.arange(128)
    ebd128 = (lane[:, None] // HEAD_DIM == lane[None, :] // HEAD_DIM).astype(BF16)
    lane5 = jnp.arange(ATTN_OUT)
    ebd512 = (lane5[:, None] // HEAD_DIM == lane5[None, :] // HEAD_DIM).astype(BF16)
    qnw2 = jnp.tile(q_norm_w, (1, 2))
    knw2 = jnp.tile(k_norm_w, (1, 2))

    projn, h, ht = _inproj_nat(xs, mod, norm_w, w_nat)
    projq = _matmul(h, w_qkv, tm=1024, tn=1536, name="inproj_qkv")
    o_g, lse_g = [], []
    for g in range(N_GROUPS):
        o, l = _attn_fwd(projq, qnw2, knw2, ebd128, g)
        o_g.append(o)
        lse_g.append(l)

    dpn, d_o, lse, delta, dout, vacc, gwo, gwc, gwa = _mid(
        projn, o_g, lse_g, xs, tg, mod, convw8, wc, wc.T, wa, wa.T, wo, wo.T, ebd512)
    dpq = lax.empty((S, NQKV), BF16)
    gqk = []
    for g in range(N_GROUPS):
        dpq, part = _attn_bwd(projq, dpq, d_o, lse, delta, qnw2, knw2, ebd128, g)
        gqk.append(part)
    gw_nat = _matmul_acc(ht, dpn, tn=1664, tk=1024, name="grad_w_in_nat")
    gw_qkv = _matmul_acc(ht, dpq, tn=1536, tk=1024, name="grad_w_in_qkv")
    return dpn, dpq, dout, gw_nat, gw_qkv, gwc, gwa, gwo, vacc, gqk


def kernel(x, c, w_ada, b_ada, norm_w, w_in, conv_w, q_norm_w, k_norm_w, w_br_conv, w_br_attn, w_out, loss_target, m_w_ada, m_b_ada, m_norm_w, m_w_in, m_conv_w, m_q_norm_w, m_k_norm_w, m_w_br_conv, m_w_br_attn, m_w_out, v_w_ada, v_b_ada, v_norm_w, v_w_in, v_conv_w, v_q_norm_w, v_k_norm_w, v_w_br_conv, v_w_br_attn, v_w_out):
    D = D_MODEL
    S = x.shape[1]
    xs, tg = x[0], loss_target[0]
    mx, my, mc = lax.axis_index("x"), lax.axis_index("y"), lax.axis_index("c")
    chip = 2 * mx + my
    dev = 2 * chip + mc

    c_all = _allgather8(jnp.pad(c, ((0, 7), (0, 0))), "allgather_c")[:, 0, :]
    b_shard = lax.dynamic_slice(b_ada, (0, chip * 768), (1, 768))
    mod_part, silu_c = _mod_part(c_all, w_ada[0], b_shard)
    mod_parts = _allgather_chips(mod_part, "allgather_mod")
    mod_all = mod_parts.transpose(1, 0, 2).reshape(8, 3 * D)
    mod = lax.dynamic_slice(mod_all, (dev, 0), (1, 3 * D))

    shards = [w_in[0].astype(BF16).reshape(2, 512, 2816), w_br_conv[0].astype(BF16).reshape(2, 128, D),
              w_br_attn[0].astype(BF16).reshape(2, 256, 256), w_out[0].astype(BF16).reshape(2, 128, D)]
    gathered_w = _weights_allgather(shards)
    g_bc, g_ba, g_wo = [lax.dynamic_update_slice(g, s[None], (chip, 0, 0, 0))
                        for g, s in zip(gathered_w[1:], shards[1:])]
    chipv = jnp.reshape(chip, (1,)).astype(jnp.int32)
    g_in = gathered_w[0].reshape(4, D, SHARD_COLS)
    own_in = shards[0].reshape(D, SHARD_COLS)
    w_nat, w_nat_t = _weights_prep(g_in, own_in, chipv, "nat")
    w_qkv, w_qkv_t = _weights_prep(g_in, own_in, chipv, "qkv")
    wc = g_bc.reshape(D, D)
    wa = g_ba.reshape(4, ATTN_OUT, 256).transpose(1, 0, 2).reshape(ATTN_OUT, D)
    wo = g_wo.reshape(D, D)
    conv_full = _allgather_chips(jnp.pad(conv_w[0], ((0, 5), (0, 0))), "allgather_conv")
    convw8 = conv_full.transpose(1, 0, 2).reshape(8, D)

    dpn, dpq, dout, gw_nat, gw_qkv, gwc, gwa, gwo, vacc, gqk = _local_step(
        xs, tg, mod, norm_w, w_nat, w_qkv, wc, wa, wo, convw8, q_norm_w, k_norm_w)
    gw_in = _grad_relayout(gw_nat, lax.empty((2, 4, D // 2, SHARD_COLS), BF16), "nat")
    gw_in = _grad_relayout(gw_qkv, gw_in, "qkv")

    def halves(a):
        k, r, cc = a.shape
        return a.reshape(k, 2, r // 2, cc).transpose(1, 0, 2, 3)

    grads = [gw_in,
             halves(gwc.astype(BF16).reshape(4, 256, D)),
             halves(gwa.astype(BF16).reshape(ATTN_OUT, 4, 256).transpose(1, 0, 2)),
             halves(gwo.astype(BF16).reshape(4, 256, D))]
    recv = _pair_exchange(grads)
    core = jnp.reshape(mc, (1,)).astype(jnp.int32)
    parts = [_pair_add(core, gr, rc) for gr, rc in zip(grads, recv)]
    grad_x, st, gathered = _dh_norm(dpn, dpq, w_nat_t, w_qkv_t, xs, dout, mod, norm_w, parts)
    mine = [_sum4(chipv, p, q) for p, q in zip(parts, gathered)]
    theirs = _half_exchange(mine)

    big = {}
    for t, (nm, w, m, v) in enumerate((("w_in", w_in, m_w_in, v_w_in), ("w_br_conv", w_br_conv, m_w_br_conv, v_w_br_conv),
                                       ("w_br_attn", w_br_attn, m_w_br_attn, v_w_br_attn), ("w_out", w_out, m_w_out, v_w_out))):
        big[nm] = _adamw_halves(core, w[0], mine[t], theirs[t], m[0], v[0], "adamw_" + nm)

    sm_all = _allgather8(_pack_smalls(st, vacc, jnp.concatenate(gqk, axis=0)), "allgather_smalls")
    dmod_all = sm_all[:, 0:3, :].reshape(8, 3 * D)
    dm_pad = jnp.pad(lax.dynamic_slice(dmod_all, (0, chip * 768), (8, 768)), ((0, 120), (0, 0)))
    conv_dev = lax.dynamic_slice(sm_all[:, 16:24, :], (0, 0, chip * 256), (8, 8, 256))
    sm_loc = jnp.concatenate([sm_all[:, 0:16, :], jnp.pad(conv_dev, ((0, 0), (0, 0), (0, D - 256))), sm_all[:, 24:32, :]], axis=1)

    def pack_vec(b3, nw, cw, qw, kw):
        z = jnp.zeros((32, D), F32)
        z = z.at[0:3, :].set(b3.reshape(3, D)).at[8:9, :].set(nw).at[16:19, 0:256].set(cw)
        return z.at[24:25, 0:HEAD_DIM].set(qw).at[25:26, 0:HEAD_DIM].set(kw)

    vec_w = pack_vec(b_ada, norm_w, conv_w[0], q_norm_w, k_norm_w)
    vec_m = pack_vec(m_b_ada, m_norm_w, m_conv_w[0], m_q_norm_w, m_k_norm_w)
    vec_v = pack_vec(v_b_ada, v_norm_w, v_conv_w[0], v_q_norm_w, v_k_norm_w)
    ct_pad = jnp.pad(silu_c.T, ((0, 0), (0, 120)))
    outs = _small_update(sm_loc, dm_pad, ct_pad, w_ada[0], m_w_ada[0], v_w_ada[0], vec_w, vec_m, vec_v)
    ada = outs[0:4]
    vec = outs[4:8]
    loss = vec[0][26, 0]

    def unpack(t):
        return {"b_ada": t[0:3, :].reshape(1, 3 * D), "norm_w": t[8:9, :], "conv_w": t[16:19, 0:256][None],
                "q_norm_w": t[24:25, 0:HEAD_DIM], "k_norm_w": t[25:26, 0:HEAD_DIM]}

    small = [unpack(t) for t in vec]
    order = ["w_ada", "b_ada", "norm_w", "w_in", "conv_w", "q_norm_w", "k_norm_w", "w_br_conv", "w_br_attn", "w_out"]
    res = [loss, grad_x[None]]
    for kind in range(4):
        for nm in order:
            if nm == "w_ada":
                res.append(ada[kind][None])
            elif nm in big:
                res.append(big[nm][kind][None])
            else:
                res.append(small[kind][nm])
    return tuple(res)
```

```python
import functools

import jax
import jax.numpy as jnp
from jax import lax
from jax.experimental import pallas as pl
from jax.experimental.pallas import tpu as pltpu

F32 = jnp.float32
BF16 = jnp.bfloat16
MESH = pl.DeviceIdType.MESH
ANY = pl.BlockSpec(memory_space=pl.ANY)

D_MODEL = 1024
HEAD_DIM = 64
N_GROUPS = 3
DILATIONS = (1, 4, 16)
ATTN_OUT = 512
EPS = 1e-6
NEG_INF = -1e30
NAT = 6656
NQKV = 4608
BA, CA, XA, ZA, ZB, GA, GB = 0, 1024, 2048, 3072, 4096, 4608, 5632
ATT_TILE = 2048
QK_SCALE = HEAD_DIM ** -0.5
LOG2E = 1.4426950408889634
LN2 = 0.6931471805599453
V7X_VMEM_LIMIT = 56 * 1024 * 1024

ADAM_LR = 0.001
ADAM_B1 = 0.9
ADAM_B2 = 0.999
ADAM_EPS = 1e-08
ADAM_WD = 0.01
ADAM_STEP = 10


def _params(**kw):
    return pltpu.CompilerParams(vmem_limit_bytes=V7X_VMEM_LIMIT, **kw)


def _sigmoid(z):
    return 1.0 / (1.0 + jnp.exp(-z))


def _row(v, r):
    rows = lax.broadcasted_iota(jnp.int32, v.shape, 0)
    return jnp.sum(jnp.where(rows == r, v, 0.0), axis=0, keepdims=True)


def _coords():
    return lax.axis_index("x"), lax.axis_index("y"), lax.axis_index("c")


def _flip(v, bit):
    return 1 - v if bit else v


def _allgather8(blk, name):
    r, c = blk.shape

    def body(x_ref, o_ref, ssem, rsem, lsem):
        mx, my, mc = _coords()
        me = 4 * mx + 2 * my + mc
        local = pltpu.make_async_copy(x_ref, o_ref.at[me], lsem)
        local.start()
        sends = []
        for j in range(1, 8):
            peer = (_flip(mx, j & 4), _flip(my, j & 2), _flip(mc, j & 1))
            cp = pltpu.make_async_remote_copy(
                src_ref=x_ref, dst_ref=o_ref.at[me], send_sem=ssem.at[j - 1], recv_sem=rsem.at[j - 1],
                device_id=peer, device_id_type=MESH)
            cp.start()
            sends.append(cp)
        for j in range(1, 8):
            px, py, pc = _flip(mx, j & 4), _flip(my, j & 2), _flip(mc, j & 1)
            pltpu.make_async_remote_copy(
                src_ref=x_ref, dst_ref=o_ref.at[4 * px + 2 * py + pc], send_sem=ssem.at[j - 1],
                recv_sem=rsem.at[j - 1], device_id=(px, py, pc), device_id_type=MESH).wait_recv()
        for cp in sends:
            cp.wait_send()
        local.wait()

    return pl.pallas_call(
        body, name=name,
        out_shape=jax.ShapeDtypeStruct((8, r, c), blk.dtype),
        in_specs=[pl.BlockSpec(memory_space=pltpu.VMEM)],
        out_specs=pl.BlockSpec(memory_space=pltpu.VMEM),
        scratch_shapes=[pltpu.SemaphoreType.DMA((7,)), pltpu.SemaphoreType.DMA((7,)), pltpu.SemaphoreType.DMA(())],
    )(blk)


def _allgather_chips(blk, name):
    r, c = blk.shape

    def body(x_ref, o_ref, ssem, rsem, lsem):
        mx, my, mc = _coords()
        me = 2 * mx + my
        local = pltpu.make_async_copy(x_ref, o_ref.at[me], lsem)
        local.start()
        sends = []
        for j in range(1, 4):
            peer = (_flip(mx, j & 2), _flip(my, j & 1), mc)
            cp = pltpu.make_async_remote_copy(
                src_ref=x_ref, dst_ref=o_ref.at[me], send_sem=ssem.at[j - 1], recv_sem=rsem.at[j - 1],
                device_id=peer, device_id_type=MESH)
            cp.start()
            sends.append(cp)
        for j in range(1, 4):
            px, py = _flip(mx, j & 2), _flip(my, j & 1)
            pltpu.make_async_remote_copy(
                src_ref=x_ref, dst_ref=o_ref.at[2 * px + py], send_sem=ssem.at[j - 1],
                recv_sem=rsem.at[j - 1], device_id=(px, py, mc), device_id_type=MESH).wait_recv()
        for cp in sends:
            cp.wait_send()
        local.wait()

    return pl.pallas_call(
        body, name=name,
        out_shape=jax.ShapeDtypeStruct((4, r, c), blk.dtype),
        in_specs=[pl.BlockSpec(memory_space=pltpu.VMEM)],
        out_specs=pl.BlockSpec(memory_space=pltpu.VMEM),
        scratch_shapes=[pltpu.SemaphoreType.DMA((3,)), pltpu.SemaphoreType.DMA((3,)), pltpu.SemaphoreType.DMA(())],
    )(blk)


def _weights_allgather(shards):
    n = len(shards)

    def body(*refs):
        ins, outs = refs[:n], refs[n:2 * n]
        ssem, rsem = refs[2 * n:]
        mx, my, mc = _coords()
        me = 2 * mx + my
        sib = (mx, my, 1 - mc)
        chips = [(_flip(mx, j & 2), _flip(my, j & 1)) for j in range(1, 4)]
        sends = []
        for t in range(n):
            for j, (px, py) in enumerate(chips):
                cp = pltpu.make_async_remote_copy(
                    src_ref=ins[t].at[mc], dst_ref=outs[t].at[me, mc], send_sem=ssem.at[t, j],
                    recv_sem=rsem.at[t, j], device_id=(px, py, mc), device_id_type=MESH)
                cp.start()
                sends.append(cp)
        for t in range(n):
            for j, (px, py) in enumerate(chips):
                src = 2 * px + py
                pltpu.make_async_remote_copy(
                    src_ref=ins[t].at[mc], dst_ref=outs[t].at[src, mc], send_sem=ssem.at[t, j],
                    recv_sem=rsem.at[t, j], device_id=(px, py, mc), device_id_type=MESH).wait_recv()
                fwd = pltpu.make_async_remote_copy(
                    src_ref=outs[t].at[src, mc], dst_ref=outs[t].at[src, mc], send_sem=ssem.at[t, 3 + j],
                    recv_sem=rsem.at[t, 3 + j], device_id=sib, device_id_type=MESH)
                fwd.start()
                sends.append(fwd)
        for t in range(n):
            for j, (px, py) in enumerate(chips):
                src = 2 * px + py
                pltpu.make_async_remote_copy(
                    src_ref=outs[t].at[src, 1 - mc], dst_ref=outs[t].at[src, 1 - mc], send_sem=ssem.at[t, 3 + j],
                    recv_sem=rsem.at[t, 3 + j], device_id=sib, device_id_type=MESH).wait_recv()
        for cp in sends:
            cp.wait_send()

    return pl.pallas_call(
        body, name="weights_allgather",
        out_shape=[jax.ShapeDtypeStruct((4,) + s.shape, s.dtype) for s in shards],
        in_specs=[ANY] * n, out_specs=[ANY] * n,
        scratch_shapes=[pltpu.SemaphoreType.DMA((n, 6)), pltpu.SemaphoreType.DMA((n, 6))],
    )(*shards)


def _pair_exchange(grads):
    n = len(grads)

    def body(*refs):
        ins, outs = refs[:n], refs[n:2 * n]
        ssem, rsem = refs[2 * n:]
        mx, my, mc = _coords()
        sib = (mx, my, 1 - mc)
        sends = []
        for t in range(n):
            cp = pltpu.make_async_remote_copy(
                src_ref=ins[t].at[1 - mc], dst_ref=outs[t], send_sem=ssem.at[t], recv_sem=rsem.at[t],
                device_id=sib, device_id_type=MESH)
            cp.start()
            sends.append(cp)
        for cp in sends:
            cp.wait_recv()
        for cp in sends:
            cp.wait_send()

    return pl.pallas_call(
        body, name="grad_pair_exchange",
        out_shape=[jax.ShapeDtypeStruct(g.shape[1:], g.dtype) for g in grads],
        in_specs=[ANY] * n, out_specs=[ANY] * n,
        scratch_shapes=[pltpu.SemaphoreType.DMA((n,)), pltpu.SemaphoreType.DMA((n,))],
    )(*grads)


def _scatter_copies(ins, outs, ssem, rsem):
    mx, my, mc = _coords()
    me = 2 * mx + my
    mine, theirs = [], []
    for t in range(len(ins)):
        for j in range(1, 4):
            px, py = _flip(mx, j & 2), _flip(my, j & 1)
            mine.append(pltpu.make_async_remote_copy(
                src_ref=ins[t].at[2 * px + py], dst_ref=outs[t].at[me], send_sem=ssem.at[t, j - 1],
                recv_sem=rsem.at[t, j - 1], device_id=(px, py, mc), device_id_type=MESH))
            theirs.append(pltpu.make_async_remote_copy(
                src_ref=ins[t].at[me], dst_ref=outs[t].at[2 * px + py], send_sem=ssem.at[t, j - 1],
                recv_sem=rsem.at[t, j - 1], device_id=(px, py, mc), device_id_type=MESH))
    return mine, theirs


def _half_exchange(halves):
    n = len(halves)

    def body(*refs):
        ins, outs = refs[:n], refs[n:2 * n]
        ssem, rsem = refs[2 * n:]
        mx, my, mc = _coords()
        sib = (mx, my, 1 - mc)
        sends = []
        for t in range(n):
            rc = pltpu.make_async_remote_copy(
                src_ref=ins[t], dst_ref=outs[t], send_sem=ssem.at[t], recv_sem=rsem.at[t],
                device_id=sib, device_id_type=MESH)
            rc.start()
            sends.append(rc)
        for cp in sends:
            cp.wait_recv()
        for cp in sends:
            cp.wait_send()

    return pl.pallas_call(
        body, name="grad_half_exchange",
        out_shape=[jax.ShapeDtypeStruct(h.shape, h.dtype) for h in halves],
        in_specs=[ANY] * n, out_specs=[ANY] * n,
        scratch_shapes=[pltpu.SemaphoreType.DMA((n,)), pltpu.SemaphoreType.DMA((n,))],
    )(*halves)


def _row_tile(rows, cols, bytes_per_row_elem=4, budget=2 * 1024 * 1024):
    t = rows
    while t % 2 == 0 and t > 16 and t * cols * bytes_per_row_elem > budget:
        t //= 2
    return t


def _pair_add(core, g, r):
    _, _, rh, cc = g.shape
    tr = _row_tile(rh, cc)

    def body(core_ref, g_ref, r_ref, o_ref):
        o_ref[...] = (g_ref[...].astype(F32) + r_ref[...].astype(F32)).astype(o_ref.dtype)

    return pl.pallas_call(
        body, name="grad_pair_add",
        grid_spec=pltpu.PrefetchScalarGridSpec(
            num_scalar_prefetch=1, grid=(4, rh // tr),
            in_specs=[pl.BlockSpec((None, None, tr, cc), lambda k, i, c: (c[0], k, i, 0)),
                      pl.BlockSpec((None, tr, cc), lambda k, i, c: (k, i, 0))],
            out_specs=pl.BlockSpec((None, tr, cc), lambda k, i, c: (k, i, 0))),
        out_shape=jax.ShapeDtypeStruct(r.shape, BF16),
        compiler_params=_params(),
    )(core, g, r)


def _sum4(chip, p, q):
    _, rh, cc = q.shape
    tr = _row_tile(rh, cc)

    def body(chip_ref, p_ref, q1_ref, q2_ref, q3_ref, o_ref):
        o_ref[...] = ((p_ref[...].astype(F32) + q1_ref[...].astype(F32)) + q2_ref[...].astype(F32)) + q3_ref[...].astype(F32)

    def other(j):
        return pl.BlockSpec((None, tr, cc), lambda i, c: (jnp.bitwise_xor(c[0], j), i, 0))

    return pl.pallas_call(
        body, name="grad_sum4",
        grid_spec=pltpu.PrefetchScalarGridSpec(
            num_scalar_prefetch=1, grid=(rh // tr,),
            in_specs=[pl.BlockSpec((None, tr, cc), lambda i, c: (c[0], i, 0)), other(1), other(2), other(3)],
            out_specs=pl.BlockSpec((tr, cc), lambda i, c: (i, 0))),
        out_shape=jax.ShapeDtypeStruct((rh, cc), F32),
        compiler_params=_params(),
    )(chip, p, q, q, q)


def _adamw_math(w, g, m, v):
    m = ADAM_B1 * m + (1.0 - ADAM_B1) * g
    v = ADAM_B2 * v + (1.0 - ADAM_B2) * (g * g)
    m_hat = m / (1.0 - ADAM_B1 ** ADAM_STEP)
    v_hat = v / (1.0 - ADAM_B2 ** ADAM_STEP)
    delta = -ADAM_LR * (m_hat / (jnp.sqrt(v_hat) + ADAM_EPS) + ADAM_WD * w)
    return delta, m, v


def _adamw_halves(core, w, mine, recv, m, v, name):
    rows, cols = w.shape
    rh = rows // 2
    tr = _row_tile(rh, cols, budget=1024 * 1024)
    nh = rh // tr

    def body(core_ref, w_ref, a_ref, b_ref, m_ref, v_ref, g_ref, d_ref, mo_ref, vo_ref):
        g = jnp.where(pl.program_id(0) == core_ref[0], a_ref[...], b_ref[...])
        d, mn, vn = _adamw_math(w_ref[...], g, m_ref[...], v_ref[...])
        g_ref[...] = g
        d_ref[...] = d
        mo_ref[...] = mn
        vo_ref[...] = vn

    full = pl.BlockSpec((tr, cols), lambda h, i, c: (h * nh + i, 0))
    half = pl.BlockSpec((tr, cols), lambda h, i, c: (i, 0))
    sd = jax.ShapeDtypeStruct((rows, cols), F32)
    return pl.pallas_call(
        body, name=name,
        grid_spec=pltpu.PrefetchScalarGridSpec(
            num_scalar_prefetch=1, grid=(2, nh),
            in_specs=[full, half, half, full, full], out_specs=[full] * 4),
        out_shape=[sd, sd, sd, sd], compiler_params=_params(),
    )(core, w, mine, recv, m, v)


def _mod_part(c_all, w_ada, b_shard):
    cols = w_ada.shape[1]

    def body(c_ref, w_ref, b_ref, o_ref, sc_ref):
        cv = c_ref[...]
        sc = cv * _sigmoid(cv)
        sc_ref[...] = sc
        o_ref[...] = jnp.dot(sc, w_ref[...], preferred_element_type=F32,
                             precision=lax.Precision.HIGHEST) + b_ref[...]

    return pl.pallas_call(
        body, name="adaln_mod",
        out_shape=[jax.ShapeDtypeStruct((8, cols), F32), jax.ShapeDtypeStruct(c_all.shape, F32)],
        in_specs=[pl.BlockSpec(memory_space=pltpu.VMEM)] * 3,
        out_specs=[pl.BlockSpec(memory_space=pltpu.VMEM)] * 2, compiler_params=_params(),
    )(c_all, w_ada, b_shard)


def _inproj_nat(x, mod, norm_w, w, *, tm=1024, tn=1664):
    S, D = x.shape
    N = w.shape[1]
    chunk = 256

    def body(x_ref, mod_ref, nw_ref, w_ref, proj_ref, h_ref, ht_ref):
        @pl.when(pl.program_id(1) == 0)
        def _():
            shift = mod_ref[:, 0:D]
            scale1 = 1.0 + mod_ref[:, D:2 * D]
            for r in range(tm // chunk):
                xv = x_ref[pl.ds(r * chunk, chunk), :]
                ms = jnp.mean(xv * xv, axis=-1, keepdims=True)
                h = (xv * lax.rsqrt(ms + EPS) * nw_ref[...]) * scale1 + shift
                h_ref[pl.ds(r * chunk, chunk), :] = h.astype(BF16)
                ht_ref[:, pl.ds(r * chunk, chunk)] = h.T.astype(BF16)
        proj_ref[...] = jnp.dot(h_ref[...], w_ref[...], preferred_element_type=F32).astype(BF16)

    return pl.pallas_call(
        body, name="inproj_nat", grid=(S // tm, N // tn),
        in_specs=[pl.BlockSpec((tm, D), lambda i, j: (i, 0)),
                  pl.BlockSpec((1, 3 * D), lambda i, j: (0, 0)),
                  pl.BlockSpec((1, D), lambda i, j: (0, 0)),
                  pl.BlockSpec((D, tn), lambda i, j: (0, j))],
        out_specs=[pl.BlockSpec((tm, tn), lambda i, j: (i, j)),
                   pl.BlockSpec((tm, D), lambda i, j: (i, 0)),
                   pl.BlockSpec((D, tm), lambda i, j: (0, i))],
        out_shape=[jax.ShapeDtypeStruct((S, N), BF16), jax.ShapeDtypeStruct((S, D), BF16),
                   jax.ShapeDtypeStruct((D, S), BF16)],
        compiler_params=_params(),
    )(x, mod, norm_w, w)


def _matmul(a, b, *, tm, tn, name):
    M, K = a.shape
    N = b.shape[1]

    def body(a_ref, b_ref, o_ref):
        o_ref[...] = jnp.dot(a_ref[...], b_ref[...], preferred_element_type=F32).astype(o_ref.dtype)

    return pl.pallas_call(
        body, name=name, grid=(M // tm, N // tn),
        in_specs=[pl.BlockSpec((tm, K), lambda i, j: (i, 0)), pl.BlockSpec((K, tn), lambda i, j: (0, j))],
        out_specs=pl.BlockSpec((tm, tn), lambda i, j: (i, j)),
        out_shape=jax.ShapeDtypeStruct((M, N), BF16), compiler_params=_params(),
    )(a, b)


def _matmul_acc(a, b, *, tn, tk, name):
    M, K = a.shape
    N = b.shape[1]
    nk = K // tk

    def body(a_ref, b_ref, o_ref, acc_ref):
        k = pl.program_id(1)

        @pl.when(k == 0)
        def _():
            acc_ref[...] = jnp.zeros_like(acc_ref)

        acc_ref[...] += jnp.dot(a_ref[...], b_ref[...], preferred_element_type=F32)

        @pl.when(k == nk - 1)
        def _():
            o_ref[...] = acc_ref[...].astype(o_ref.dtype)

    return pl.pallas_call(
        body, name=name, grid=(N // tn, nk),
        in_specs=[pl.BlockSpec((M, tk), lambda j, k: (0, k)), pl.BlockSpec((tk, tn), lambda j, k: (k, j))],
        out_specs=pl.BlockSpec((M, tn), lambda j, k: (0, j)),
        out_shape=jax.ShapeDtypeStruct((M, N), BF16),
        scratch_shapes=[pltpu.VMEM((M, tn), F32)], compiler_params=_params(),
    )(a, b)


SHARD_COLS = 2816


def _column_tables(part):
    if part == "nat":
        bw = 256
        orig = [j * bw if j < 16 else 8704 + (j - 16) * bw for j in range(NAT // bw)]
    else:
        bw = 128
        orig = []
        for jj in range(NQKV // bw):
            g, p, t = jj // 12, (jj % 12) // 3, jj % 3
            orig.append(4096 + t * 1536 + g * 512 + p * 128)
    tk = jnp.array([o // SHARD_COLS for o in orig], jnp.int32)
    tc = jnp.array([(o % SHARD_COLS) // bw for o in orig], jnp.int32)
    return tk, tc, bw


def _weights_prep(gath, own, chipv, part):
    D = D_MODEL
    tk, tc, bw = _column_tables(part)
    n = tk.shape[0]

    def body(tk_ref, tc_ref, chip_ref, g_ref, o_ref, w_ref, wt_ref):
        blk = jnp.where(tk_ref[pl.program_id(0)] == chip_ref[0], o_ref[...], g_ref[...])
        w_ref[...] = blk
        wt_ref[...] = blk.astype(F32).T.astype(BF16)

    def gath_idx(j, tk, tc, ch):
        k = tk[j]
        return (jnp.where(k == ch[0], (k + 1) % 4, k), 0, tc[j])

    return pl.pallas_call(
        body, name="weights_prep_" + part,
        grid_spec=pltpu.PrefetchScalarGridSpec(
            num_scalar_prefetch=3, grid=(n,),
            in_specs=[pl.BlockSpec((None, D, bw), gath_idx),
                      pl.BlockSpec((D, bw), lambda j, tk, tc, ch: (0, tc[j]))],
            out_specs=[pl.BlockSpec((D, bw), lambda j, tk, tc, ch: (0, j)),
                       pl.BlockSpec((bw, D), lambda j, tk, tc, ch: (j, 0))]),
        out_shape=[jax.ShapeDtypeStruct((D, n * bw), BF16), jax.ShapeDtypeStruct((n * bw, D), BF16)],
        compiler_params=_params(),
    )(tk, tc, chipv, gath, own)


def _grad_relayout(gw, into, part):
    D = D_MODEL
    tk, tc, bw = _column_tables(part)
    n = tk.shape[0]

    def body(tk_ref, tc_ref, g_ref, into_ref, o_ref):
        o_ref[0] = g_ref[0:D // 2, :]
        o_ref[1] = g_ref[D // 2:D, :]

    return pl.pallas_call(
        body, name="grad_relayout_" + part,
        grid_spec=pltpu.PrefetchScalarGridSpec(
            num_scalar_prefetch=2, grid=(n,),
            in_specs=[pl.BlockSpec((D, bw), lambda j, tk, tc: (0, j)), ANY],
            out_specs=pl.BlockSpec((2, None, D // 2, bw), lambda j, tk, tc: (0, tk[j], 0, tc[j]))),
        out_shape=jax.ShapeDtypeStruct(into.shape, BF16),
        input_output_aliases={3: 0},
        compiler_params=_params(),
    )(tk, tc, gw, into)


def _head_norm(xb, w, ebd):
    x = xb.astype(F32)
    ss = jnp.dot((x * x).astype(BF16), ebd, preferred_element_type=F32)
    return x * lax.rsqrt(ss * (1.0 / HEAD_DIM) + EPS) * w


def _window_mask(no_prev):
    qi = lax.broadcasted_iota(jnp.int32, (128, 256), 0)
    kc = lax.broadcasted_iota(jnp.int32, (128, 256), 1)
    ok = (kc >= qi) & (kc <= qi + 128)
    if no_prev is not None:
        ok = ok & ((kc >= 128) | jnp.logical_not(no_prev))
    return ok


def _lane_masks():
    lane = lax.broadcasted_iota(jnp.int32, (1, 128), 1)
    return [(lane < HEAD_DIM).astype(F32), (lane >= HEAD_DIM).astype(F32)]


def _attn_fwd(projq, qnw2, knw2, ebd, g):
    S = projq.shape[0]
    d = DILATIONS[g]
    T = ATT_TILE
    nt = S // T
    nb = T // (128 * d)
    sdt = BF16 if d == 1 else F32

    def body(cur_ref, qw_ref, kw_ref, ebd_ref, o_ref, lse_ref, qs_s, ks_s, vs_s):
        t = pl.program_id(1)
        e = ebd_ref[...]

        @pl.when(t == 0)
        def _():
            ks_s[pl.ds(0, T), :] = jnp.zeros((T, 128), sdt)
            vs_s[pl.ds(0, T), :] = jnp.zeros((T, 128), sdt)

        @pl.when(t > 0)
        def _():
            ks_s[pl.ds(0, T), :] = ks_s[pl.ds(T, T), :]
            vs_s[pl.ds(0, T), :] = vs_s[pl.ds(T, T), :]

        qs_s[...] = (_head_norm(cur_ref[:, 0:128], qw_ref[...], e) * (QK_SCALE * LOG2E)).astype(sdt)
        ks_s[pl.ds(T, T), :] = _head_norm(cur_ref[:, 128:256], kw_ref[...], e).astype(sdt)
        vs_s[pl.ds(T, T), :] = cur_ref[:, 256:384].astype(sdt)
        m0, m1 = _lane_masks()
        first = t == 0
        for r in range(d):
            for bb in range(nb):
                q0 = r + bb * 128 * d
                k0 = T + r + (bb - 1) * 128 * d
                qs = qs_s[pl.ds(q0, 128, stride=d), :].astype(F32)
                kb = ks_s[pl.ds(k0, 256, stride=d), :].astype(BF16)
                vb = vs_s[pl.ds(k0, 256, stride=d), :].astype(BF16)
                ok = _window_mask(first if bb == 0 else None)
                o_t = None
                l_t = None
                for mh in (m0, m1):
                    s = lax.dot_general((qs * mh).astype(BF16), kb, (((1,), (1,)), ((), ())),
                                        preferred_element_type=F32)
                    s = jnp.where(ok, s, NEG_INF)
                    mx = jnp.max(s, axis=-1, keepdims=True)
                    p = jnp.exp2(s - mx)
                    l = jnp.sum(p, axis=-1, keepdims=True)
                    o = jnp.dot(p.astype(BF16), vb, preferred_element_type=F32) / l
                    lse = mx * LN2 + jnp.log(l)
                    o_t = o * mh if o_t is None else o_t + o * mh
                    l_t = lse * mh if l_t is None else l_t + lse * mh
                o_ref[pl.ds(q0, 128, stride=d), :] = o_t
                lse_ref[pl.ds(q0, 128, stride=d), :] = l_t

    return pl.pallas_call(
        body, name=f"attn_fwd_g{g}", grid=(4, nt),
        in_specs=[pl.BlockSpec((T, 384), lambda p, t: (t, g * 4 + p)),
                  pl.BlockSpec((1, 128), lambda p, t: (0, 0)),
                  pl.BlockSpec((1, 128), lambda p, t: (0, 0)),
                  pl.BlockSpec((128, 128), lambda p, t: (0, 0))],
        out_specs=[pl.BlockSpec((T, 128), lambda p, t: (t, p)), pl.BlockSpec((T, 128), lambda p, t: (t, p))],
        out_shape=[jax.ShapeDtypeStruct((S, ATTN_OUT), F32), jax.ShapeDtypeStruct((S, ATTN_OUT), F32)],
        scratch_shapes=[pltpu.VMEM((T, 128), sdt), pltpu.VMEM((2 * T, 128), sdt), pltpu.VMEM((2 * T, 128), sdt)],
        compiler_params=_params(),
    )(projq, qnw2, knw2, ebd)


def _attn_bwd(projq, dprojq, d_o, lse, delta, qnw2, knw2, ebd, g):
    S = projq.shape[0]
    d = DILATIONS[g]
    T = ATT_TILE
    nt = S // T
    nb = T // (128 * d)
    sdt = BF16 if d == 1 else F32

    def norm_bwd(raw_b, dy, w, e):
        x = raw_b.astype(F32)
        ss = jnp.dot((x * x).astype(BF16), e, preferred_element_type=F32)
        rstd = lax.rsqrt(ss * (1.0 / HEAD_DIM) + EPS)
        xh = x * rstd
        gw = jnp.sum(dy * xh, axis=0, keepdims=True)
        dxh = dy * w
        mean = jnp.dot((dxh * xh).astype(BF16), e, preferred_element_type=F32) * (1.0 / HEAD_DIM)
        return rstd * (dxh - xh * mean), gw

    def body(cur_ref, prev_ref, do_ref, lse_ref, dl_ref, qw_ref, kw_ref, ebd_ref, dp_in_ref,
             out_ref, gqk_ref, qs_s, ks_s, vs_s, do_s, dq_cur, dq_prev, dk_acc, dv_acc):
        i = pl.program_id(1)
        e = ebd_ref[...]
        m0, m1 = _lane_masks()

        @pl.when(i == 0)
        def _():
            dk_acc[...] = jnp.zeros_like(dk_acc)
            dv_acc[...] = jnp.zeros_like(dv_acc)
            dq_prev[...] = jnp.zeros_like(dq_prev)
            gqk_ref[...] = jnp.zeros_like(gqk_ref)
            ks_s[pl.ds(0, T), :] = jnp.zeros((T, 128), sdt)
            vs_s[pl.ds(0, T), :] = jnp.zeros((T, 128), sdt)

        @pl.when((i > 0) & (i < nt))
        def _():
            ks_s[pl.ds(0, T), :] = ks_s[pl.ds(T, T), :]
            vs_s[pl.ds(0, T), :] = vs_s[pl.ds(T, T), :]

        @pl.when(i < nt)
        def _():
            qs_s[...] = (_head_norm(cur_ref[:, 0:128], qw_ref[...], e) * (QK_SCALE * LOG2E)).astype(sdt)
            ks_s[pl.ds(T, T), :] = _head_norm(cur_ref[:, 128:256], kw_ref[...], e).astype(sdt)
            vs_s[pl.ds(T, T), :] = cur_ref[:, 256:384].astype(sdt)
            do_s[...] = do_ref[...].astype(sdt)
            first = i == 0
            for r in range(d):
                for bb in range(nb):
                    q0 = r + bb * 128 * d
                    k0 = T + r + (bb - 1) * 128 * d
                    qs = qs_s[pl.ds(q0, 128, stride=d), :].astype(F32)
                    dos = do_s[pl.ds(q0, 128, stride=d), :].astype(F32)
                    ls = lse_ref[pl.ds(q0, 128, stride=d), :]
                    dls = dl_ref[pl.ds(q0, 128, stride=d), :]
                    kb = ks_s[pl.ds(k0, 256, stride=d), :].astype(BF16)
                    vb = vs_s[pl.ds(k0, 256, stride=d), :].astype(BF16)
                    ok = _window_mask(first if bb == 0 else None)
                    dq_t = None
                    dk_t = None
                    dv_t = None
                    for h, mh in enumerate((m0, m1)):
                        lane0 = h * HEAD_DIM
                        qh = (qs * mh).astype(BF16)
                        doh = (dos * mh).astype(BF16)
                        s = lax.dot_general(qh, kb, (((1,), (1,)), ((), ())), preferred_element_type=F32)
                        s = jnp.where(ok, s, NEG_INF)
                        p = jnp.exp2(s - ls[:, lane0:lane0 + 1] * LOG2E)
                        dp = lax.dot_general(doh, vb, (((1,), (1,)), ((), ())), preferred_element_type=F32)
                        ds = (p * (dp - dls[:, lane0:lane0 + 1])).astype(BF16)
                        dq = jnp.dot(ds, kb, preferred_element_type=F32) * mh
                        dk = lax.dot_general(ds, qh, (((0,), (0,)), ((), ())), preferred_element_type=F32)
                        dv = lax.dot_general(p.astype(BF16), doh, (((0,), (0,)), ((), ())),
                                             preferred_element_type=F32)
                        dq_t = dq if dq_t is None else dq_t + dq
                        dk_t = dk if dk_t is None else dk_t + dk
                        dv_t = dv if dv_t is None else dv_t + dv
                    dq_cur[pl.ds(q0, 128, stride=d), :] = dq_t
                    dk_acc[pl.ds(k0, 256, stride=d), :] = dk_acc[pl.ds(k0, 256, stride=d), :] + dk_t
                    dv_acc[pl.ds(k0, 256, stride=d), :] = dv_acc[pl.ds(k0, 256, stride=d), :] + dv_t

        @pl.when(i >= 1)
        def _():
            dq_raw, gq = norm_bwd(prev_ref[:, 0:128], dq_prev[...] * QK_SCALE, qw_ref[...], e)
            dk_raw, gk = norm_bwd(prev_ref[:, 128:256], dk_acc[pl.ds(0, T), :] * LN2, kw_ref[...], e)
            out_ref[:, 0:128] = dq_raw.astype(BF16)
            out_ref[:, 128:256] = dk_raw.astype(BF16)
            out_ref[:, 256:384] = dv_acc[pl.ds(0, T), :].astype(BF16)
            gqk_ref[0:1, :] += gq
            gqk_ref[1:2, :] += gk

        dq_prev[...] = dq_cur[...]
        dk_acc[pl.ds(0, T), :] = dk_acc[pl.ds(T, T), :]
        dv_acc[pl.ds(0, T), :] = dv_acc[pl.ds(T, T), :]
        dk_acc[pl.ds(T, T), :] = jnp.zeros((T, 128), F32)
        dv_acc[pl.ds(T, T), :] = jnp.zeros((T, 128), F32)

    last = nt - 1
    qtile = lambda p, i: (jnp.minimum(i, last), p)
    return pl.pallas_call(
        body, name=f"attn_bwd_g{g}", grid=(4, nt + 1),
        in_specs=[pl.BlockSpec((T, 384), lambda p, i: (jnp.minimum(i, last), g * 4 + p)),
                  pl.BlockSpec((T, 384), lambda p, i: (jnp.maximum(i - 1, 0), g * 4 + p)),
                  pl.BlockSpec((T, 128), qtile), pl.BlockSpec((T, 128), qtile), pl.BlockSpec((T, 128), qtile),
                  pl.BlockSpec((1, 128), lambda p, i: (0, 0)),
                  pl.BlockSpec((1, 128), lambda p, i: (0, 0)),
                  pl.BlockSpec((128, 128), lambda p, i: (0, 0)),
                  ANY],
        out_specs=[pl.BlockSpec((T, 384), lambda p, i: (jnp.maximum(i - 1, 0), g * 4 + p)),
                   pl.BlockSpec((None, 8, 128), lambda p, i: (p, 0, 0))],
        out_shape=[jax.ShapeDtypeStruct(dprojq.shape, BF16), jax.ShapeDtypeStruct((4, 8, 128), F32)],
        scratch_shapes=[pltpu.VMEM((T, 128), sdt), pltpu.VMEM((2 * T, 128), sdt), pltpu.VMEM((2 * T, 128), sdt),
                        pltpu.VMEM((T, 128), sdt), pltpu.VMEM((T, 128), F32), pltpu.VMEM((T, 128), F32),
                        pltpu.VMEM((2 * T, 128), F32), pltpu.VMEM((2 * T, 128), F32)],
        input_output_aliases={8: 0},
        compiler_params=_params(),
    )(projq, projq, d_o, lse, delta, qnw2, knw2, ebd, dprojq)


def _mid(projn, o_g, lse_g, x, tgt, mod, convw8, wc, wct, wa, wat, wo, wot, ebd, *, tm=128):
    S, D = x.shape
    nt = S // tm
    assert nt % 2 == 0
    hb = tm // 16

    def body(pn_ref, hc_ref, hx_ref, o0_ref, o1_ref, o2_ref, l0_ref, l1_ref, l2_ref, x_ref, t_ref, mod_ref,
             cw_ref, ebd_ref, wc_hbm, wct_hbm, wa_hbm, wat_hbm, wo_hbm, wot_hbm,
             dpn_ref, do_ref, lse_ref, dl_ref, dout_ref, vacc_ref, gwo_hbm, gwc_hbm, gwa_hbm,
             wc_s, wct_s, wa_s, wat_s, wo_s, wot_s, gwo_s, gwc_s, gwa_s, carry_s,
             st_m, st_d, st_ya, st_pa, st_yb, st_pb):
        i = pl.program_id(0)
        ti = nt - 1 - i
        par = i % 2
        stash = pl.ds(pl.multiple_of(par * tm, tm), tm)

        @pl.when(i == 0)
        def _():
            for src, dst in ((wc_hbm, wc_s), (wct_hbm, wct_s), (wa_hbm, wa_s), (wat_hbm, wat_s),
                             (wo_hbm, wo_s), (wot_hbm, wot_s)):
                pltpu.sync_copy(src, dst)
            gwo_s[...] = jnp.zeros_like(gwo_s)
            gwc_s[...] = jnp.zeros_like(gwc_s)
            gwa_s[...] = jnp.zeros_like(gwa_s)
            carry_s[...] = jnp.zeros_like(carry_s)
            vacc_ref[...] = jnp.zeros_like(vacc_ref)

        rows = lax.broadcasted_iota(jnp.int32, (tm, D), 0)
        w0, w1, w2 = cw_ref[0:1, :], cw_ref[1:2, :], cw_ref[2:3, :]
        gate = mod_ref[:, 2 * D:3 * D]

        b_a = pn_ref[:, BA:BA + D].astype(F32)
        c_a = pn_ref[:, CA:CA + D].astype(F32)
        x_a = pn_ref[:, XA:XA + D].astype(F32)
        z_a = pn_ref[:, ZA:ZA + D].astype(F32)
        u = c_a * x_a
        has_prev = (ti > 0).astype(F32)
        uh = hc_ref[...].astype(F32) * hx_ref[...].astype(F32) * has_prev
        h14, h15 = _row(uh, 14), _row(uh, 15)
        u_m1 = jnp.where(rows == 0, h15, pltpu.roll(u, 1, 0))
        u_m2 = jnp.where(rows == 0, h14, jnp.where(rows == 1, h15, pltpu.roll(u, 2, 0)))
        conv = w0 * u_m2 + w1 * u_m1 + w2 * u
        sg_a = _sigmoid(z_a)
        sz_a = z_a * sg_a
        y_a = b_a * conv * sz_a
        y_ab = y_a.astype(BF16)
        pa = jnp.dot(y_ab, wc_s[...], preferred_element_type=F32)

        l0, l1, l2 = l0_ref[...], l1_ref[...], l2_ref[...]
        mxl = jnp.maximum(jnp.maximum(l0, l1), l2)
        e0, e1, e2 = jnp.exp(l0 - mxl), jnp.exp(l1 - mxl), jnp.exp(l2 - mxl)
        den = e0 + e1 + e2
        attn = (e0 * o0_ref[...] + e1 * o1_ref[...] + e2 * o2_ref[...]) / den
        lse_ref[...] = mxl + jnp.log(den)
        z_b = pn_ref[:, ZB:ZB + ATTN_OUT].astype(F32)
        sg_b = _sigmoid(z_b)
        sz_b = z_b * sg_b
        y_b = attn * sz_b
        y_bb = y_b.astype(BF16)
        pb = jnp.dot(y_bb, wa_s[...], preferred_element_type=F32)

        s_a = _sigmoid(pn_ref[:, GA:GA + D].astype(F32))
        s_b = _sigmoid(pn_ref[:, GB:GB + D].astype(F32))
        merged = s_a * pa + s_b * pb
        merged_b = merged.astype(BF16)
        mo = jnp.dot(merged_b, wo_s[...], preferred_element_type=F32)
        diff = x_ref[...] + gate * mo - t_ref[...]
        dout = diff * (1.0 / D)
        dout_ref[...] = dout
        vacc_ref[4:5, :] += jnp.sum(diff * diff, axis=0, keepdims=True)
        vacc_ref[0:1, :] += jnp.sum(dout * mo, axis=0, keepdims=True)

        d_mo = (dout * gate).astype(BF16)
        dmerged = jnp.dot(d_mo, wot_s[...], preferred_element_type=F32)
        dpa = (dmerged * s_a).astype(BF16)
        dpb = (dmerged * s_b).astype(BF16)
        dpn_ref[:, GA:GA + D] = (dmerged * pa * s_a * (1.0 - s_a)).astype(BF16)
        dpn_ref[:, GB:GB + D] = (dmerged * pb * s_b * (1.0 - s_b)).astype(BF16)
        st_m[stash, :] = merged_b
        st_d[stash, :] = d_mo
        st_ya[stash, :] = y_ab
        st_pa[stash, :] = dpa
        st_yb[stash, :] = y_bb
        st_pb[stash, :] = dpb

        @pl.when(par == 1)
        def _():
            tn = (((0,), (0,)), ((), ()))
            gwo_s[...] += lax.dot_general(st_m[...], st_d[...], tn, preferred_element_type=F32)
            gwc_s[...] += lax.dot_general(st_ya[...], st_pa[...], tn, preferred_element_type=F32)
            gwa_s[...] += lax.dot_general(st_yb[...], st_pb[...], tn, preferred_element_type=F32)

        dy_a = jnp.dot(dpa, wct_s[...], preferred_element_type=F32)
        dy_b = jnp.dot(dpb, wat_s[...], preferred_element_type=F32)

        dattn = dy_b * sz_b
        dpn_ref[:, ZB:ZB + ATTN_OUT] = (dy_b * attn * sg_b * (1.0 + z_b * (1.0 - sg_b))).astype(BF16)
        do_ref[...] = dattn.astype(BF16)
        dl_ref[...] = jnp.dot((dattn * attn).astype(BF16), ebd_ref[...], preferred_element_type=F32)

        dpn_ref[:, BA:BA + D] = (dy_a * conv * sz_a).astype(BF16)
        dpn_ref[:, ZA:ZA + D] = (dy_a * b_a * conv * sg_a * (1.0 + z_a * (1.0 - sg_a))).astype(BF16)
        dconv = dy_a * b_a * sz_a
        vacc_ref[1:2, :] += jnp.sum(dconv * u_m2, axis=0, keepdims=True)
        vacc_ref[2:3, :] += jnp.sum(dconv * u_m1, axis=0, keepdims=True)
        vacc_ref[3:4, :] += jnp.sum(dconv * u, axis=0, keepdims=True)
        nxt = carry_s[...]
        n0, n1 = _row(nxt, 0), _row(nxt, 1)
        dc_p1 = jnp.where(rows == tm - 1, n0, pltpu.roll(dconv, tm - 1, 0))
        dc_p2 = jnp.where(rows == tm - 1, n1, jnp.where(rows == tm - 2, n0, pltpu.roll(dconv, tm - 2, 0)))
        du = w2 * dconv + w1 * dc_p1 + w0 * dc_p2
        carry_s[...] = dconv[0:8, :]
        dpn_ref[:, CA:CA + D] = (du * x_a).astype(BF16)
        dpn_ref[:, XA:XA + D] = (du * c_a).astype(BF16)

        @pl.when(i == nt - 1)
        def _():
            pltpu.sync_copy(gwo_s, gwo_hbm)
            pltpu.sync_copy(gwc_s, gwc_hbm)
            pltpu.sync_copy(gwa_s, gwa_hbm)

    rev = lambda i: (nt - 1 - i, 0)
    halo = lambda col: pl.BlockSpec((16, D), lambda i: (jnp.maximum((nt - 1 - i) * hb - 1, 0), col))
    tile512 = pl.BlockSpec((tm, ATTN_OUT), rev)
    tileD = pl.BlockSpec((tm, D), rev)
    const = lambda shape: pl.BlockSpec(shape, lambda i: (0, 0))
    return pl.pallas_call(
        body, name="mid_fwd_bwd", grid=(nt,),
        in_specs=[pl.BlockSpec((tm, NAT), rev), halo(CA // D), halo(XA // D)] + [tile512] * 6
                 + [tileD, tileD, const((1, 3 * D)), const((8, D)), const((ATTN_OUT, ATTN_OUT))] + [ANY] * 6,
        out_specs=[pl.BlockSpec((tm, NAT), rev), tile512, tile512, tile512, tileD, const((8, D)), ANY, ANY, ANY],
        out_shape=[jax.ShapeDtypeStruct((S, NAT), BF16), jax.ShapeDtypeStruct((S, ATTN_OUT), BF16),
                   jax.ShapeDtypeStruct((S, ATTN_OUT), F32), jax.ShapeDtypeStruct((S, ATTN_OUT), F32),
                   jax.ShapeDtypeStruct((S, D), F32), jax.ShapeDtypeStruct((8, D), F32),
                   jax.ShapeDtypeStruct((D, D), F32), jax.ShapeDtypeStruct((D, D), F32),
                   jax.ShapeDtypeStruct((ATTN_OUT, D), F32)],
        scratch_shapes=[pltpu.VMEM((D, D), BF16), pltpu.VMEM((D, D), BF16), pltpu.VMEM((ATTN_OUT, D), BF16),
                        pltpu.VMEM((D, ATTN_OUT), BF16), pltpu.VMEM((D, D), BF16), pltpu.VMEM((D, D), BF16),
                        pltpu.VMEM((D, D), F32), pltpu.VMEM((D, D), F32), pltpu.VMEM((ATTN_OUT, D), F32),
                        pltpu.VMEM((8, D), F32)]
                       + [pltpu.VMEM((2 * tm, D), BF16)] * 4 + [pltpu.VMEM((2 * tm, ATTN_OUT), BF16), pltpu.VMEM((2 * tm, D), BF16)],
        compiler_params=_params(),
    )(projn, projn, projn, *o_g, *lse_g, x, tgt, mod, convw8, ebd, wc, wct, wa, wat, wo, wot)


def _dh_norm(dpn, dpq, wnt, wqt, x, dout, mod, norm_w, parts, *, tm=256):
    S, D = x.shape
    n = len(parts)
    nt = S // tm

    def body(dn_ref, dq_ref, x_ref, dout_ref, mod_ref, nw_ref, wn_hbm, wq_hbm, *rest):
        p_hbm, rest = rest[:n], rest[n:]
        gx_ref, st_ref = rest[0], rest[1]
        q_hbm = rest[2:2 + n]
        wn_s, wq_s = rest[2 + n:4 + n]
        sems = rest[4 + n:]

        @pl.when(pl.program_id(0) == 0)
        def _():
            if n:
                mine, _ = _scatter_copies(p_hbm, q_hbm, *sems)
                for cp in mine:
                    cp.start()
            pltpu.sync_copy(wn_hbm, wn_s)
            pltpu.sync_copy(wq_hbm, wq_s)
            st_ref[...] = jnp.zeros_like(st_ref)

        if n:
            @pl.when(pl.program_id(0) == nt - 1)
            def _():
                mine, theirs = _scatter_copies(p_hbm, q_hbm, *sems)
                for cp in theirs:
                    cp.wait_recv()
                for cp in mine:
                    cp.wait_send()

        dh = (jnp.dot(dn_ref[...], wn_s[...], preferred_element_type=F32)
              + jnp.dot(dq_ref[...], wq_s[...], preferred_element_type=F32))
        scale1 = 1.0 + mod_ref[:, D:2 * D]
        nw = nw_ref[...]
        xv = x_ref[...]
        rstd = lax.rsqrt(jnp.mean(xv * xv, axis=-1, keepdims=True) + EPS)
        xh = xv * rstd
        dhs = dh * scale1
        dxh = dhs * nw
        dx = rstd * (dxh - xh * jnp.mean(dxh * xh, axis=-1, keepdims=True))
        gx_ref[...] = dout_ref[...] + dx
        st_ref[0:1, :] += jnp.sum(dh, axis=0, keepdims=True)
        st_ref[1:2, :] += jnp.sum(dh * (xh * nw), axis=0, keepdims=True)
        st_ref[2:3, :] += jnp.sum(dhs * xh, axis=0, keepdims=True)

    row = lambda w: pl.BlockSpec((tm, w), lambda i: (i, 0))
    outs = pl.pallas_call(
        body, name="dh_norm_bwd", grid=(nt,),
        in_specs=[row(NAT), row(NQKV), row(D), row(D),
                  pl.BlockSpec((1, 3 * D), lambda i: (0, 0)), pl.BlockSpec((1, D), lambda i: (0, 0)), ANY, ANY]
                 + [ANY] * n,
        out_specs=[row(D), pl.BlockSpec((8, D), lambda i: (0, 0))] + [ANY] * n,
        out_shape=[jax.ShapeDtypeStruct((S, D), F32), jax.ShapeDtypeStruct((8, D), F32)]
                  + [jax.ShapeDtypeStruct(p.shape, p.dtype) for p in parts],
        scratch_shapes=[pltpu.VMEM((NAT, D), BF16), pltpu.VMEM((NQKV, D), BF16)]
                       + ([pltpu.SemaphoreType.DMA((n, 3)), pltpu.SemaphoreType.DMA((n, 3))] if n else []),
        compiler_params=_params(),
    )(dpn, dpq, x, dout, mod, norm_w, wnt, wqt, *parts)
    return outs[0], outs[1], outs[2:]


def _pack_smalls(st, vacc, gqk):
    D = D_MODEL

    def body(st_ref, va_ref, gqk_ref, o_ref):
        o_ref[...] = jnp.zeros_like(o_ref)
        o_ref[0:2, :] = st_ref[0:2, :]
        o_ref[2:3, :] = va_ref[0:1, :]
        o_ref[8:9, :] = st_ref[2:3, :]
        o_ref[16:19, :] = va_ref[1:4, :]
        tot = gqk_ref[0]
        for k in range(1, gqk_ref.shape[0]):
            tot = tot + gqk_ref[k]
        tot = tot + pltpu.roll(tot, HEAD_DIM, 1)
        o_ref[24:32, 0:128] = tot
        loss = jnp.sum(va_ref[4:5, :], axis=1, keepdims=True) * (0.5 / D)
        o_ref[26:27, :] = jnp.broadcast_to(loss, (1, D))

    return pl.pallas_call(
        body, name="pack_smalls", out_shape=jax.ShapeDtypeStruct((32, D), F32),
        in_specs=[pl.BlockSpec(memory_space=pltpu.VMEM)] * 3,
        out_specs=pl.BlockSpec(memory_space=pltpu.VMEM), compiler_params=_params(),
    )(st, vacc, gqk)


def _small_update(sm_loc, dm_pad, ct_pad, w_ada, m_ada, v_ada, vec_w, vec_m, vec_v):
    D = D_MODEL
    cols = w_ada.shape[1]

    def body(sm_ref, dm_ref, ct_ref, w_ref, m_ref, v_ref, vw_ref, vm_ref, vv_ref,
             gw_ref, dw_ref, mw_ref, vw_o_ref, gv_ref, dv_ref, mv_ref, vv_o_ref):
        g = sm_ref[0]
        for k in range(1, 8):
            g = g + sm_ref[k]
        gv_ref[...] = g
        dv, mv, vv = _adamw_math(vw_ref[...], g, vm_ref[...], vv_ref[...])
        dv_ref[...] = dv
        mv_ref[...] = mv
        vv_o_ref[...] = vv
        gw = jnp.dot(ct_ref[...], dm_ref[...], preferred_element_type=F32, precision=lax.Precision.HIGHEST)
        gw_ref[...] = gw
        dw, mw, vw = _adamw_math(w_ref[...], gw, m_ref[...], v_ref[...])
        dw_ref[...] = dw
        mw_ref[...] = mw
        vw_o_ref[...] = vw

    sw = jax.ShapeDtypeStruct((D, cols), F32)
    sv = jax.ShapeDtypeStruct((32, D), F32)
    return pl.pallas_call(
        body, name="small_update", out_shape=[sw, sw, sw, sw, sv, sv, sv, sv],
        in_specs=[pl.BlockSpec(memory_space=pltpu.VMEM)] * 9,
        out_specs=[pl.BlockSpec(memory_space=pltpu.VMEM)] * 8, compiler_params=_params(),
    )(sm_loc, dm_pad, ct_pad, w_ada, m_ada, v_ada, vec_w, vec_m, vec_v)


def _local_step(xs, tg, mod, norm_w, w_nat, w_qkv, wc, wa, wo, convw8, q_norm_w, k_norm_w):
    S = xs.shape[0]
    lane = jnp.arange(128)
    ebd128 = (lane[:, None] // HEAD_DIM == lane[None, :] // HEAD_DIM).astype(BF16)
    lane5 = jnp.arange(ATTN_OUT)
    ebd512 = (lane5[:, None] // HEAD_DIM == lane5[None, :] // HEAD_DIM).astype(BF16)
    qnw2 = jnp.tile(q_norm_w, (1, 2))
    knw2 = jnp.tile(k_norm_w, (1, 2))

    projn, h, ht = _inproj_nat(xs, mod, norm_w, w_nat)
    projq = _matmul(h, w_qkv, tm=1024, tn=1536, name="inproj_qkv")
    o_g, lse_g = [], []
    for g in range(N_GROUPS):
        o, l = _attn_fwd(projq, qnw2, knw2, ebd128, g)
        o_g.append(o)
        lse_g.append(l)

    dpn, d_o, lse, delta, dout, vacc, gwo, gwc, gwa = _mid(
        projn, o_g, lse_g, xs, tg, mod, convw8, wc, wc.T, wa, wa.T, wo, wo.T, ebd512)
    dpq = lax.empty((S, NQKV), BF16)
    gqk = []
    for g in range(N_GROUPS):
        dpq, part = _attn_bwd(projq, dpq, d_o, lse, delta, qnw2, knw2, ebd128, g)
        gqk.append(part)
    gw_nat = _matmul_acc(ht, dpn, tn=1664, tk=1024, name="grad_w_in_nat")
    gw_qkv = _matmul_acc(ht, dpq, tn=1536, tk=1024, name="grad_w_in_qkv")
    return dpn, dpq, dout, gw_nat, gw_qkv, gwc, gwa, gwo, vacc, gqk


def kernel(x, c, w_ada, b_ada, norm_w, w_in, conv_w, q_norm_w, k_norm_w, w_br_conv, w_br_attn, w_out, loss_target, m_w_ada, m_b_ada, m_norm_w, m_w_in, m_conv_w, m_q_norm_w, m_k_norm_w, m_w_br_conv, m_w_br_attn, m_w_out, v_w_ada, v_b_ada, v_norm_w, v_w_in, v_conv_w, v_q_norm_w, v_k_norm_w, v_w_br_conv, v_w_br_attn, v_w_out):
    D = D_MODEL
    S = x.shape[1]
    xs, tg = x[0], loss_target[0]
    mx, my, mc = lax.axis_index("x"), lax.axis_index("y"), lax.axis_index("c")
    chip = 2 * mx + my
    dev = 2 * chip + mc

    c_all = _allgather8(jnp.pad(c, ((0, 7), (0, 0))), "allgather_c")[:, 0, :]
    b_shard = lax.dynamic_slice(b_ada, (0, chip * 768), (1, 768))
    mod_part, silu_c = _mod_part(c_all, w_ada[0], b_shard)
    mod_parts = _allgather_chips(mod_part, "allgather_mod")
    mod_all = mod_parts.transpose(1, 0, 2).reshape(8, 3 * D)
    mod = lax.dynamic_slice(mod_all, (dev, 0), (1, 3 * D))

    shards = [w_in[0].astype(BF16).reshape(2, 512, 2816), w_br_conv[0].astype(BF16).reshape(2, 128, D),
              w_br_attn[0].astype(BF16).reshape(2, 256, 256), w_out[0].astype(BF16).reshape(2, 128, D)]
    gathered_w = _weights_allgather(shards)
    g_bc, g_ba, g_wo = [lax.dynamic_update_slice(g, s[None], (chip, 0, 0, 0))
                        for g, s in zip(gathered_w[1:], shards[1:])]
    chipv = jnp.reshape(chip, (1,)).astype(jnp.int32)
    g_in = gathered_w[0].reshape(4, D, SHARD_COLS)
    own_in = shards[0].reshape(D, SHARD_COLS)
    w_nat, w_nat_t = _weights_prep(g_in, own_in, chipv, "nat")
    w_qkv, w_qkv_t = _weights_prep(g_in, own_in, chipv, "qkv")
    wc = g_bc.reshape(D, D)
    wa = g_ba.reshape(4, ATTN_OUT, 256).transpose(1, 0, 2).reshape(ATTN_OUT, D)
    wo = g_wo.reshape(D, D)
    conv_full = _allgather_chips(jnp.pad(conv_w[0], ((0, 5), (0, 0))), "allgather_conv")
    convw8 = conv_full.transpose(1, 0, 2).reshape(8, D)

    dpn, dpq, dout, gw_nat, gw_qkv, gwc, gwa, gwo, vacc, gqk = _local_step(
        xs, tg, mod, norm_w, w_nat, w_qkv, wc, wa, wo, convw8, q_norm_w, k_norm_w)
    gw_in = _grad_relayout(gw_nat, lax.empty((2, 4, D // 2, SHARD_COLS), BF16), "nat")
    gw_in = _grad_relayout(gw_qkv, gw_in, "qkv")

    def halves(a):
        k, r, cc = a.shape
        return a.reshape(k, 2, r // 2, cc).transpose(1, 0, 2, 3)

    grads = [gw_in,
             halves(gwc.astype(BF16).reshape(4, 256, D)),
             halves(gwa.astype(BF16).reshape(ATTN_OUT, 4, 256).transpose(1, 0, 2)),
             halves(gwo.astype(BF16).reshape(4, 256, D))]
    recv = _pair_exchange(grads)
    core = jnp.reshape(mc, (1,)).astype(jnp.int32)
    parts = [_pair_add(core, gr, rc) for gr, rc in zip(grads, recv)]
    grad_x, st, gathered = _dh_norm(dpn, dpq, w_nat_t, w_qkv_t, xs, dout, mod, norm_w, parts)
    mine = [_sum4(chipv, p, q) for p, q in zip(parts, gathered)]
    theirs = _half_exchange(mine)

    big = {}
    for t, (nm, w, m, v) in enumerate((("w_in", w_in, m_w_in, v_w_in), ("w_br_conv", w_br_conv, m_w_br_conv, v_w_br_conv),
                                       ("w_br_attn", w_br_attn, m_w_br_attn, v_w_br_attn), ("w_out", w_out, m_w_out, v_w_out))):
        big[nm] = _adamw_halves(core, w[0], mine[t], theirs[t], m[0], v[0], "adamw_" + nm)

    sm_all = _allgather8(_pack_smalls(st, vacc, jnp.concatenate(gqk, axis=0)), "allgather_smalls")
    dmod_all = sm_all[:, 0:3, :].reshape(8, 3 * D)
    dm_pad = jnp.pad(lax.dynamic_slice(dmod_all, (0, chip * 768), (8, 768)), ((0, 120), (0, 0)))
    conv_dev = lax.dynamic_slice(sm_all[:, 16:24, :], (0, 0, chip * 256), (8, 8, 256))
    sm_loc = jnp.concatenate([sm_all[:, 0:16, :], jnp.pad(conv_dev, ((0, 0), (0, 0), (0, D - 256))), sm_all[:, 24:32, :]], axis=1)

    def pack_vec(b3, nw, cw, qw, kw):
        z = jnp.zeros((32, D), F32)
        z = z.at[0:3, :].set(b3.reshape(3, D)).at[8:9, :].set(nw).at[16:19, 0:256].set(cw)
        return z.at[24:25, 0:HEAD_DIM].set(qw).at[25:26, 0:HEAD_DIM].set(kw)

    vec_w = pack_vec(b_ada, norm_w, conv_w[0], q_norm_w, k_norm_w)
    vec_m = pack_vec(m_b_ada, m_norm_w, m_conv_w[0], m_q_norm_w, m_k_norm_w)
    vec_v = pack_vec(v_b_ada, v_norm_w, v_conv_w[0], v_q_norm_w, v_k_norm_w)
    ct_pad = jnp.pad(silu_c.T, ((0, 0), (0, 120)))
    outs = _small_update(sm_loc, dm_pad, ct_pad, w_ada[0], m_w_ada[0], v_w_ada[0], vec_w, vec_m, vec_v)
    ada = outs[0:4]
    vec = outs[4:8]
    loss = vec[0][26, 0]

    def unpack(t):
        return {"b_ada": t[0:3, :].reshape(1, 3 * D), "norm_w": t[8:9, :], "conv_w": t[16:19, 0:256][None],
                "q_norm_w": t[24:25, 0:HEAD_DIM], "k_norm_w": t[25:26, 0:HEAD_DIM]}

    small = [unpack(t) for t in vec]
    order = ["w_ada", "b_ada", "norm_w", "w_in", "conv_w", "q_norm_w", "k_norm_w", "w_br_conv", "w_br_attn", "w_out"]
    res = [loss, grad_x[None]]
    for kind in range(4):
        for nm in order:
            if nm == "w_ada":
                res.append(ada[kind][None])
            elif nm in big:
                res.append(big[nm][kind][None])
            else:
                res.append(small[kind][nm])
    return tuple(res)
```

```python
import functools

import jax
import jax.numpy as jnp
from jax import lax
from jax.experimental import pallas as pl
from jax.experimental.pallas import tpu as pltpu

F32 = jnp.float32
BF16 = jnp.bfloat16
MESH = pl.DeviceIdType.MESH
ANY = pl.BlockSpec(memory_space=pl.ANY)

D_MODEL = 1024
HEAD_DIM = 64
N_GROUPS = 3
DILATIONS = (1, 4, 16)
ATTN_OUT = 512
EPS = 1e-6
NEG_INF = -1e30
NAT = 6656
NQKV = 4608
BA, CA, XA, ZA, ZB, GA, GB = 0, 1024, 2048, 3072, 4096, 4608, 5632
ATT_TILE = 2048
QK_SCALE = HEAD_DIM ** -0.5
LOG2E = 1.4426950408889634
LN2 = 0.6931471805599453
V7X_VMEM_BYTES = 64 * 1024 * 1024
V7X_VMEM_LIMIT = 56 * 1024 * 1024

ADAM_LR = 0.001
ADAM_B1 = 0.9
ADAM_B2 = 0.999
ADAM_EPS = 1e-08
ADAM_WD = 0.01
ADAM_STEP = 10


def _params(vmem=V7X_VMEM_LIMIT, **kw):
    return pltpu.CompilerParams(vmem_limit_bytes=vmem, **kw)


def _sigmoid(z):
    return 1.0 / (1.0 + jnp.exp(-z))


def _row(v, r):
    rows = lax.broadcasted_iota(jnp.int32, v.shape, 0)
    return jnp.sum(jnp.where(rows == r, v, 0.0), axis=0, keepdims=True)


def _coords():
    return lax.axis_index("x"), lax.axis_index("y"), lax.axis_index("c")


def _flip(v, bit):
    return 1 - v if bit else v


def _allgather8(blk, name):
    r, c = blk.shape

    def body(x_ref, o_ref, ssem, rsem, lsem):
        mx, my, mc = _coords()
        me = 4 * mx + 2 * my + mc
        local = pltpu.make_async_copy(x_ref, o_ref.at[me], lsem)
        local.start()
        sends = []
        for j in range(1, 8):
            peer = (_flip(mx, j & 4), _flip(my, j & 2), _flip(mc, j & 1))
            cp = pltpu.make_async_remote_copy(
                src_ref=x_ref, dst_ref=o_ref.at[me], send_sem=ssem.at[j - 1], recv_sem=rsem.at[j - 1],
                device_id=peer, device_id_type=MESH)
            cp.start()
            sends.append(cp)
        for j in range(1, 8):
            px, py, pc = _flip(mx, j & 4), _flip(my, j & 2), _flip(mc, j & 1)
            pltpu.make_async_remote_copy(
                src_ref=x_ref, dst_ref=o_ref.at[4 * px + 2 * py + pc], send_sem=ssem.at[j - 1],
                recv_sem=rsem.at[j - 1], device_id=(px, py, pc), device_id_type=MESH).wait_recv()
        for cp in sends:
            cp.wait_send()
        local.wait()

    return pl.pallas_call(
        body, name=name,
        out_shape=jax.ShapeDtypeStruct((8, r, c), blk.dtype),
        in_specs=[pl.BlockSpec(memory_space=pltpu.VMEM)],
        out_specs=pl.BlockSpec(memory_space=pltpu.VMEM),
        scratch_shapes=[pltpu.SemaphoreType.DMA((7,)), pltpu.SemaphoreType.DMA((7,)), pltpu.SemaphoreType.DMA(())],
    )(blk)


def _allgather_chips(blk, name):
    r, c = blk.shape

    def body(x_ref, o_ref, ssem, rsem, lsem):
        mx, my, mc = _coords()
        me = 2 * mx + my
        local = pltpu.make_async_copy(x_ref, o_ref.at[me], lsem)
        local.start()
        sends = []
        for j in range(1, 4):
            peer = (_flip(mx, j & 2), _flip(my, j & 1), mc)
            cp = pltpu.make_async_remote_copy(
                src_ref=x_ref, dst_ref=o_ref.at[me], send_sem=ssem.at[j - 1], recv_sem=rsem.at[j - 1],
                device_id=peer, device_id_type=MESH)
            cp.start()
            sends.append(cp)
        for j in range(1, 4):
            px, py = _flip(mx, j & 2), _flip(my, j & 1)
            pltpu.make_async_remote_copy(
                src_ref=x_ref, dst_ref=o_ref.at[2 * px + py], send_sem=ssem.at[j - 1],
                recv_sem=rsem.at[j - 1], device_id=(px, py, mc), device_id_type=MESH).wait_recv()
        for cp in sends:
            cp.wait_send()
        local.wait()

    return pl.pallas_call(
        body, name=name,
        out_shape=jax.ShapeDtypeStruct((4, r, c), blk.dtype),
        in_specs=[pl.BlockSpec(memory_space=pltpu.VMEM)],
        out_specs=pl.BlockSpec(memory_space=pltpu.VMEM),
        scratch_shapes=[pltpu.SemaphoreType.DMA((3,)), pltpu.SemaphoreType.DMA((3,)), pltpu.SemaphoreType.DMA(())],
    )(blk)


def _weights_allgather(shards):
    n = len(shards)

    def body(*refs):
        ins, outs = refs[:n], refs[n:2 * n]
        ssem, rsem = refs[2 * n:]
        mx, my, mc = _coords()
        me = 2 * mx + my
        sib = (mx, my, 1 - mc)
        chips = [(_flip(mx, j & 2), _flip(my, j & 1)) for j in range(1, 4)]
        sends = []
        for t in range(n):
            for j, (px, py) in enumerate(chips):
                cp = pltpu.make_async_remote_copy(
                    src_ref=ins[t].at[mc], dst_ref=outs[t].at[me, mc], send_sem=ssem.at[t, j],
                    recv_sem=rsem.at[t, j], device_id=(px, py, mc), device_id_type=MESH)
                cp.start()
                sends.append(cp)
        for t in range(n):
            for j, (px, py) in enumerate(chips):
                src = 2 * px + py
                pltpu.make_async_remote_copy(
                    src_ref=ins[t].at[mc], dst_ref=outs[t].at[src, mc], send_sem=ssem.at[t, j],
                    recv_sem=rsem.at[t, j], device_id=(px, py, mc), device_id_type=MESH).wait_recv()
                fwd = pltpu.make_async_remote_copy(
                    src_ref=outs[t].at[src, mc], dst_ref=outs[t].at[src, mc], send_sem=ssem.at[t, 3 + j],
                    recv_sem=rsem.at[t, 3 + j], device_id=sib, device_id_type=MESH)
                fwd.start()
                sends.append(fwd)
        for t in range(n):
            for j, (px, py) in enumerate(chips):
                src = 2 * px + py
                pltpu.make_async_remote_copy(
                    src_ref=outs[t].at[src, 1 - mc], dst_ref=outs[t].at[src, 1 - mc], send_sem=ssem.at[t, 3 + j],
                    recv_sem=rsem.at[t, 3 + j], device_id=sib, device_id_type=MESH).wait_recv()
        for cp in sends:
            cp.wait_send()

    return pl.pallas_call(
        body, name="weights_allgather",
        out_shape=[jax.ShapeDtypeStruct((4,) + s.shape, s.dtype) for s in shards],
        in_specs=[ANY] * n, out_specs=[ANY] * n,
        scratch_shapes=[pltpu.SemaphoreType.DMA((n, 6)), pltpu.SemaphoreType.DMA((n, 6))],
    )(*shards)


def _pair_exchange(grads):
    n = len(grads)

    def body(*refs):
        ins, outs = refs[:n], refs[n:2 * n]
        ssem, rsem = refs[2 * n:]
        mx, my, mc = _coords()
        sib = (mx, my, 1 - mc)
        sends = []
        for t in range(n):
            cp = pltpu.make_async_remote_copy(
                src_ref=ins[t].at[1 - mc], dst_ref=outs[t], send_sem=ssem.at[t], recv_sem=rsem.at[t],
                device_id=sib, device_id_type=MESH)
            cp.start()
            sends.append(cp)
        for cp in sends:
            cp.wait_recv()
        for cp in sends:
            cp.wait_send()

    return pl.pallas_call(
        body, name="grad_pair_exchange",
        out_shape=[jax.ShapeDtypeStruct(g.shape[1:], g.dtype) for g in grads],
        in_specs=[ANY] * n, out_specs=[ANY] * n,
        scratch_shapes=[pltpu.SemaphoreType.DMA((n,)), pltpu.SemaphoreType.DMA((n,))],
    )(*grads)


def _scatter_copies(ins, outs, ssem, rsem):
    mx, my, mc = _coords()
    me = 2 * mx + my
    mine, theirs = [], []
    for t in range(len(ins)):
        for j in range(1, 4):
            px, py = _flip(mx, j & 2), _flip(my, j & 1)
            mine.append(pltpu.make_async_remote_copy(
                src_ref=ins[t].at[2 * px + py], dst_ref=outs[t].at[me], send_sem=ssem.at[t, j - 1],
                recv_sem=rsem.at[t, j - 1], device_id=(px, py, mc), device_id_type=MESH))
            theirs.append(pltpu.make_async_remote_copy(
                src_ref=ins[t].at[me], dst_ref=outs[t].at[2 * px + py], send_sem=ssem.at[t, j - 1],
                recv_sem=rsem.at[t, j - 1], device_id=(px, py, mc), device_id_type=MESH))
    return mine, theirs


def _half_exchange(halves):
    n = len(halves)

    def body(*refs):
        ins, outs = refs[:n], refs[n:2 * n]
        ssem, rsem = refs[2 * n:]
        mx, my, mc = _coords()
        sib = (mx, my, 1 - mc)
        sends = []
        for t in range(n):
            rc = pltpu.make_async_remote_copy(
                src_ref=ins[t], dst_ref=outs[t], send_sem=ssem.at[t], recv_sem=rsem.at[t],
                device_id=sib, device_id_type=MESH)
            rc.start()
            sends.append(rc)
        for cp in sends:
            cp.wait_recv()
        for cp in sends:
            cp.wait_send()

    return pl.pallas_call(
        body, name="grad_half_exchange",
        out_shape=[jax.ShapeDtypeStruct(h.shape, h.dtype) for h in halves],
        in_specs=[ANY] * n, out_specs=[ANY] * n,
        scratch_shapes=[pltpu.SemaphoreType.DMA((n,)), pltpu.SemaphoreType.DMA((n,))],
    )(*halves)


def _row_tile(rows, cols, bytes_per_row_elem=4, budget=2 * 1024 * 1024):
    t = rows
    while t % 2 == 0 and t > 16 and t * cols * bytes_per_row_elem > budget:
        t //= 2
    return t


def _pair_add(core, g, r):
    _, _, rh, cc = g.shape
    tr = _row_tile(rh, cc)

    def body(core_ref, g_ref, r_ref, o_ref):
        o_ref[...] = (g_ref[...].astype(F32) + r_ref[...].astype(F32)).astype(o_ref.dtype)

    return pl.pallas_call(
        body, name="grad_pair_add",
        grid_spec=pltpu.PrefetchScalarGridSpec(
            num_scalar_prefetch=1, grid=(4, rh // tr),
            in_specs=[pl.BlockSpec((None, None, tr, cc), lambda k, i, c: (c[0], k, i, 0)),
                      pl.BlockSpec((None, tr, cc), lambda k, i, c: (k, i, 0))],
            out_specs=pl.BlockSpec((None, tr, cc), lambda k, i, c: (k, i, 0))),
        out_shape=jax.ShapeDtypeStruct(r.shape, BF16),
        compiler_params=_params(),
    )(core, g, r)


def _sum4(chip, p, q):
    _, rh, cc = q.shape
    tr = _row_tile(rh, cc)

    def body(chip_ref, p_ref, q1_ref, q2_ref, q3_ref, o_ref):
        o_ref[...] = ((p_ref[...].astype(F32) + q1_ref[...].astype(F32)) + q2_ref[...].astype(F32)) + q3_ref[...].astype(F32)

    def other(j):
        return pl.BlockSpec((None, tr, cc), lambda i, c: (jnp.bitwise_xor(c[0], j), i, 0))

    return pl.pallas_call(
        body, name="grad_sum4",
        grid_spec=pltpu.PrefetchScalarGridSpec(
            num_scalar_prefetch=1, grid=(rh // tr,),
            in_specs=[pl.BlockSpec((None, tr, cc), lambda i, c: (c[0], i, 0)), other(1), other(2), other(3)],
            out_specs=pl.BlockSpec((tr, cc), lambda i, c: (i, 0))),
        out_shape=jax.ShapeDtypeStruct((rh, cc), F32),
        compiler_params=_params(),
    )(chip, p, q, q, q)


def _adamw_math(w, g, m, v):
    m = ADAM_B1 * m + (1.0 - ADAM_B1) * g
    v = ADAM_B2 * v + (1.0 - ADAM_B2) * (g * g)
    m_hat = m / (1.0 - ADAM_B1 ** ADAM_STEP)
    v_hat = v / (1.0 - ADAM_B2 ** ADAM_STEP)
    delta = -ADAM_LR * (m_hat / (jnp.sqrt(v_hat) + ADAM_EPS) + ADAM_WD * w)
    return delta, m, v


def _adamw_halves(core, w, mine, recv, m, v, name):
    rows, cols = w.shape
    rh = rows // 2
    tr = _row_tile(rh, cols, budget=1024 * 1024)
    nh = rh // tr

    def body(core_ref, w_ref, a_ref, b_ref, m_ref, v_ref, g_ref, d_ref, mo_ref, vo_ref):
        g = jnp.where(pl.program_id(0) == core_ref[0], a_ref[...], b_ref[...])
        d, mn, vn = _adamw_math(w_ref[...], g, m_ref[...], v_ref[...])
        g_ref[...] = g
        d_ref[...] = d
        mo_ref[...] = mn
        vo_ref[...] = vn

    full = pl.BlockSpec((tr, cols), lambda h, i, c: (h * nh + i, 0))
    half = pl.BlockSpec((tr, cols), lambda h, i, c: (i, 0))
    sd = jax.ShapeDtypeStruct((rows, cols), F32)
    return pl.pallas_call(
        body, name=name,
        grid_spec=pltpu.PrefetchScalarGridSpec(
            num_scalar_prefetch=1, grid=(2, nh),
            in_specs=[full, half, half, full, full], out_specs=[full] * 4),
        out_shape=[sd, sd, sd, sd], compiler_params=_params(),
    )(core, w, mine, recv, m, v)


def _mod_part(c_all, w_ada, b_shard):
    cols = w_ada.shape[1]

    def body(c_ref, w_ref, b_ref, o_ref, sc_ref):
        cv = c_ref[...]
        sc = cv * _sigmoid(cv)
        sc_ref[...] = sc
        o_ref[...] = jnp.dot(sc, w_ref[...], preferred_element_type=F32,
                             precision=lax.Precision.HIGHEST) + b_ref[...]

    return pl.pallas_call(
        body, name="adaln_mod",
        out_shape=[jax.ShapeDtypeStruct((8, cols), F32), jax.ShapeDtypeStruct(c_all.shape, F32)],
        in_specs=[pl.BlockSpec(memory_space=pltpu.VMEM)] * 3,
        out_specs=[pl.BlockSpec(memory_space=pltpu.VMEM)] * 2, compiler_params=_params(),
    )(c_all, w_ada, b_shard)


def _inproj_nat(x, mod, norm_w, w, *, tm=1024, tn=1664):
    S, D = x.shape
    N = w.shape[1]
    chunk = 256

    def body(x_ref, mod_ref, nw_ref, w_ref, proj_ref, h_ref, ht_ref):
        @pl.when(pl.program_id(1) == 0)
        def _():
            shift = mod_ref[:, 0:D]
            scale1 = 1.0 + mod_ref[:, D:2 * D]
            for r in range(tm // chunk):
                xv = x_ref[pl.ds(r * chunk, chunk), :]
                ms = jnp.mean(xv * xv, axis=-1, keepdims=True)
                h = (xv * lax.rsqrt(ms + EPS) * nw_ref[...]) * scale1 + shift
                h_ref[pl.ds(r * chunk, chunk), :] = h.astype(BF16)
                ht_ref[:, pl.ds(r * chunk, chunk)] = h.T.astype(BF16)
        proj_ref[...] = jnp.dot(h_ref[...], w_ref[...], preferred_element_type=F32).astype(BF16)

    return pl.pallas_call(
        body, name="inproj_nat", grid=(S // tm, N // tn),
        in_specs=[pl.BlockSpec((tm, D), lambda i, j: (i, 0)),
                  pl.BlockSpec((1, 3 * D), lambda i, j: (0, 0)),
                  pl.BlockSpec((1, D), lambda i, j: (0, 0)),
                  pl.BlockSpec((D, tn), lambda i, j: (0, j))],
        out_specs=[pl.BlockSpec((tm, tn), lambda i, j: (i, j)),
                   pl.BlockSpec((tm, D), lambda i, j: (i, 0)),
                   pl.BlockSpec((D, tm), lambda i, j: (0, i))],
        out_shape=[jax.ShapeDtypeStruct((S, N), BF16), jax.ShapeDtypeStruct((S, D), BF16),
                   jax.ShapeDtypeStruct((D, S), BF16)],
        compiler_params=_params(),
    )(x, mod, norm_w, w)


def _matmul(a, b, *, tm, tn, name):
    M, K = a.shape
    N = b.shape[1]

    def body(a_ref, b_ref, o_ref):
        o_ref[...] = jnp.dot(a_ref[...], b_ref[...], preferred_element_type=F32).astype(o_ref.dtype)

    return pl.pallas_call(
        body, name=name, grid=(M // tm, N // tn),
        in_specs=[pl.BlockSpec((tm, K), lambda i, j: (i, 0)), pl.BlockSpec((K, tn), lambda i, j: (0, j))],
        out_specs=pl.BlockSpec((tm, tn), lambda i, j: (i, j)),
        out_shape=jax.ShapeDtypeStruct((M, N), BF16), compiler_params=_params(),
    )(a, b)


def _matmul_acc(a, b, *, tn, tk, name):
    M, K = a.shape
    N = b.shape[1]
    nk = K // tk

    def body(a_ref, b_ref, o_ref, acc_ref):
        k = pl.program_id(1)

        @pl.when(k == 0)
        def _():
            acc_ref[...] = jnp.zeros_like(acc_ref)

        acc_ref[...] += jnp.dot(a_ref[...], b_ref[...], preferred_element_type=F32)

        @pl.when(k == nk - 1)
        def _():
            o_ref[...] = acc_ref[...].astype(o_ref.dtype)

    return pl.pallas_call(
        body, name=name, grid=(N // tn, nk),
        in_specs=[pl.BlockSpec((M, tk), lambda j, k: (0, k)), pl.BlockSpec((tk, tn), lambda j, k: (k, j))],
        out_specs=pl.BlockSpec((M, tn), lambda j, k: (0, j)),
        out_shape=jax.ShapeDtypeStruct((M, N), BF16),
        scratch_shapes=[pltpu.VMEM((M, tn), F32)], compiler_params=_params(),
    )(a, b)


SHARD_COLS = 2816


def _column_tables(part):
    if part == "nat":
        bw = 256
        orig = [j * bw if j < 16 else 8704 + (j - 16) * bw for j in range(NAT // bw)]
    else:
        bw = 128
        orig = []
        for jj in range(NQKV // bw):
            g, p, t = jj // 12, (jj % 12) // 3, jj % 3
            orig.append(4096 + t * 1536 + g * 512 + p * 128)
    tk = jnp.array([o // SHARD_COLS for o in orig], jnp.int32)
    tc = jnp.array([(o % SHARD_COLS) // bw for o in orig], jnp.int32)
    return tk, tc, bw


def _weights_prep(gath, own, chipv, part):
    D = D_MODEL
    tk, tc, bw = _column_tables(part)
    n = tk.shape[0]

    def body(tk_ref, tc_ref, chip_ref, g_ref, o_ref, w_ref, wt_ref):
        blk = jnp.where(tk_ref[pl.program_id(0)] == chip_ref[0], o_ref[...], g_ref[...])
        w_ref[...] = blk
        wt_ref[...] = blk.astype(F32).T.astype(BF16)

    def gath_idx(j, tk, tc, ch):
        k = tk[j]
        return (jnp.where(k == ch[0], (k + 1) % 4, k), 0, tc[j])

    return pl.pallas_call(
        body, name="weights_prep_" + part,
        grid_spec=pltpu.PrefetchScalarGridSpec(
            num_scalar_prefetch=3, grid=(n,),
            in_specs=[pl.BlockSpec((None, D, bw), gath_idx),
                      pl.BlockSpec((D, bw), lambda j, tk, tc, ch: (0, tc[j]))],
            out_specs=[pl.BlockSpec((D, bw), lambda j, tk, tc, ch: (0, j)),
                       pl.BlockSpec((bw, D), lambda j, tk, tc, ch: (j, 0))]),
        out_shape=[jax.ShapeDtypeStruct((D, n * bw), BF16), jax.ShapeDtypeStruct((n * bw, D), BF16)],
        compiler_params=_params(),
    )(tk, tc, chipv, gath, own)


def _grad_relayout(gw, into, part):
    D = D_MODEL
    tk, tc, bw = _column_tables(part)
    n = tk.shape[0]

    def body(tk_ref, tc_ref, g_ref, into_ref, o_ref):
        o_ref[0] = g_ref[0:D // 2, :]
        o_ref[1] = g_ref[D // 2:D, :]

    return pl.pallas_call(
        body, name="grad_relayout_" + part,
        grid_spec=pltpu.PrefetchScalarGridSpec(
            num_scalar_prefetch=2, grid=(n,),
            in_specs=[pl.BlockSpec((D, bw), lambda j, tk, tc: (0, j)), ANY],
            out_specs=pl.BlockSpec((2, None, D // 2, bw), lambda j, tk, tc: (0, tk[j], 0, tc[j]))),
        out_shape=jax.ShapeDtypeStruct(into.shape, BF16),
        input_output_aliases={3: 0},
        compiler_params=_params(),
    )(tk, tc, gw, into)


def _head_norm(xb, w, ebd):
    x = xb.astype(F32)
    ss = jnp.dot((x * x).astype(BF16), ebd, preferred_element_type=F32)
    return x * lax.rsqrt(ss * (1.0 / HEAD_DIM) + EPS) * w


def _window_mask(no_prev):
    qi = lax.broadcasted_iota(jnp.int32, (128, 256), 0)
    kc = lax.broadcasted_iota(jnp.int32, (128, 256), 1)
    ok = (kc >= qi) & (kc <= qi + 128)
    if no_prev is not None:
        ok = ok & ((kc >= 128) | jnp.logical_not(no_prev))
    return ok


def _lane_masks():
    lane = lax.broadcasted_iota(jnp.int32, (1, 128), 1)
    return [(lane < HEAD_DIM).astype(F32), (lane >= HEAD_DIM).astype(F32)]


def _attn_fwd(projq, qnw2, knw2, ebd, g):
    S = projq.shape[0]
    d = DILATIONS[g]
    T = ATT_TILE
    nt = S // T
    nb = T // (128 * d)
    sdt = BF16 if d == 1 else F32

    def body(cur_ref, qw_ref, kw_ref, ebd_ref, o_ref, lse_ref, qs_s, ks_s, vs_s):
        t = pl.program_id(1)
        e = ebd_ref[...]

        @pl.when(t == 0)
        def _():
            ks_s[pl.ds(0, T), :] = jnp.zeros((T, 128), sdt)
            vs_s[pl.ds(0, T), :] = jnp.zeros((T, 128), sdt)

        @pl.when(t > 0)
        def _():
            ks_s[pl.ds(0, T), :] = ks_s[pl.ds(T, T), :]
            vs_s[pl.ds(0, T), :] = vs_s[pl.ds(T, T), :]

        qs_s[...] = (_head_norm(cur_ref[:, 0:128], qw_ref[...], e) * (QK_SCALE * LOG2E)).astype(sdt)
        ks_s[pl.ds(T, T), :] = _head_norm(cur_ref[:, 128:256], kw_ref[...], e).astype(sdt)
        vs_s[pl.ds(T, T), :] = cur_ref[:, 256:384].astype(sdt)
        m0, m1 = _lane_masks()
        first = t == 0
        for r in range(d):
            for bb in range(nb):
                q0 = r + bb * 128 * d
                k0 = T + r + (bb - 1) * 128 * d
                qs = qs_s[pl.ds(q0, 128, stride=d), :].astype(F32)
                kb = ks_s[pl.ds(k0, 256, stride=d), :].astype(BF16)
                vb = vs_s[pl.ds(k0, 256, stride=d), :].astype(BF16)
                ok = _window_mask(first if bb == 0 else None)
                o_t = None
                l_t = None
                for mh in (m0, m1):
                    s = lax.dot_general((qs * mh).astype(BF16), kb, (((1,), (1,)), ((), ())),
                                        preferred_element_type=F32)
                    s = jnp.where(ok, s, NEG_INF)
                    mx = jnp.max(s, axis=-1, keepdims=True)
                    p = jnp.exp2(s - mx)
                    l = jnp.sum(p, axis=-1, keepdims=True)
                    o = jnp.dot(p.astype(BF16), vb, preferred_element_type=F32) / l
                    lse = mx * LN2 + jnp.log(l)
                    o_t = o * mh if o_t is None else o_t + o * mh
                    l_t = lse * mh if l_t is None else l_t + lse * mh
                o_ref[pl.ds(q0, 128, stride=d), :] = o_t
                lse_ref[pl.ds(q0, 128, stride=d), :] = l_t

    return pl.pallas_call(
        body, name=f"attn_fwd_g{g}", grid=(4, nt),
        in_specs=[pl.BlockSpec((T, 384), lambda p, t: (t, g * 4 + p)),
                  pl.BlockSpec((1, 128), lambda p, t: (0, 0)),
                  pl.BlockSpec((1, 128), lambda p, t: (0, 0)),
                  pl.BlockSpec((128, 128), lambda p, t: (0, 0))],
        out_specs=[pl.BlockSpec((T, 128), lambda p, t: (t, p)), pl.BlockSpec((T, 128), lambda p, t: (t, p))],
        out_shape=[jax.ShapeDtypeStruct((S, ATTN_OUT), F32), jax.ShapeDtypeStruct((S, ATTN_OUT), F32)],
        scratch_shapes=[pltpu.VMEM((T, 128), sdt), pltpu.VMEM((2 * T, 128), sdt), pltpu.VMEM((2 * T, 128), sdt)],
        compiler_params=_params(),
    )(projq, qnw2, knw2, ebd)


def _attn_bwd(projq, dprojq, d_o, lse, delta, qnw2, knw2, ebd, g):
    S = projq.shape[0]
    d = DILATIONS[g]
    T = ATT_TILE
    nt = S // T
    nb = T // (128 * d)
    sdt = F32

    def norm_bwd(raw_b, dy, w, e):
        x = raw_b.astype(F32)
        ss = jnp.dot((x * x).astype(BF16), e, preferred_element_type=F32)
        rstd = lax.rsqrt(ss * (1.0 / HEAD_DIM) + EPS)
        xh = x * rstd
        gw = jnp.sum(dy * xh, axis=0, keepdims=True)
        dxh = dy * w
        mean = jnp.dot((dxh * xh).astype(BF16), e, preferred_element_type=F32) * (1.0 / HEAD_DIM)
        return rstd * (dxh - xh * mean), gw

    def body(cur_ref, prev_ref, do_ref, lse_ref, dl_ref, qw_ref, kw_ref, ebd_ref, dp_in_ref,
             out_ref, gqk_ref, qs_s, ks_s, vs_s, do_s, dq_cur, dq_prev, dk_acc, dv_acc):
        i = pl.program_id(1)
        e = ebd_ref[...]
        m0, m1 = _lane_masks()

        @pl.when(i == 0)
        def _():
            dk_acc[...] = jnp.zeros_like(dk_acc)
            dv_acc[...] = jnp.zeros_like(dv_acc)
            dq_prev[...] = jnp.zeros_like(dq_prev)
            gqk_ref[...] = jnp.zeros_like(gqk_ref)
            ks_s[pl.ds(0, T), :] = jnp.zeros((T, 128), sdt)
            vs_s[pl.ds(0, T), :] = jnp.zeros((T, 128), sdt)

        @pl.when((i > 0) & (i < nt))
        def _():
            ks_s[pl.ds(0, T), :] = ks_s[pl.ds(T, T), :]
            vs_s[pl.ds(0, T), :] = vs_s[pl.ds(T, T), :]

        @pl.when(i < nt)
        def _():
            qs_s[...] = (_head_norm(cur_ref[:, 0:128], qw_ref[...], e) * (QK_SCALE * LOG2E)).astype(sdt)
            ks_s[pl.ds(T, T), :] = _head_norm(cur_ref[:, 128:256], kw_ref[...], e).astype(sdt)
            vs_s[pl.ds(T, T), :] = cur_ref[:, 256:384].astype(sdt)
            do_s[...] = do_ref[...].astype(sdt)
            first = i == 0
            for r in range(d):
                for bb in range(nb):
                    q0 = r + bb * 128 * d
                    k0 = T + r + (bb - 1) * 128 * d
                    qs = qs_s[pl.ds(q0, 128, stride=d), :].astype(F32)
                    dos = do_s[pl.ds(q0, 128, stride=d), :].astype(F32)
                    ls = lse_ref[pl.ds(q0, 128, stride=d), :]
                    dls = dl_ref[pl.ds(q0, 128, stride=d), :]
                    kb = ks_s[pl.ds(k0, 256, stride=d), :].astype(BF16)
                    vb = vs_s[pl.ds(k0, 256, stride=d), :].astype(BF16)
                    ok = _window_mask(first if bb == 0 else None)
                    dq_t = None
                    dk_t = None
                    dv_t = None
                    for h, mh in enumerate((m0, m1)):
                        lane0 = h * HEAD_DIM
                        qh = (qs * mh).astype(BF16)
                        doh = (dos * mh).astype(BF16)
                        s = lax.dot_general(qh, kb, (((1,), (1,)), ((), ())), preferred_element_type=F32)
                        s = jnp.where(ok, s, NEG_INF)
                        p = jnp.exp2(s - ls[:, lane0:lane0 + 1] * LOG2E)
                        dp = lax.dot_general(doh, vb, (((1,), (1,)), ((), ())), preferred_element_type=F32)
                        ds = (p * (dp - dls[:, lane0:lane0 + 1])).astype(BF16)
                        dq = jnp.dot(ds, kb, preferred_element_type=F32) * mh
                        dk = lax.dot_general(ds, qh, (((0,), (0,)), ((), ())), preferred_element_type=F32)
                        dv = lax.dot_general(p.astype(BF16), doh, (((0,), (0,)), ((), ())),
                                             preferred_element_type=F32)
                        dq_t = dq if dq_t is None else dq_t + dq
                        dk_t = dk if dk_t is None else dk_t + dk
                        dv_t = dv if dv_t is None else dv_t + dv
                    dq_cur[pl.ds(q0, 128, stride=d), :] = dq_t
                    dk_acc[pl.ds(k0, 256, stride=d), :] = dk_acc[pl.ds(k0, 256, stride=d), :] + dk_t
                    dv_acc[pl.ds(k0, 256, stride=d), :] = dv_acc[pl.ds(k0, 256, stride=d), :] + dv_t

        @pl.when(i >= 1)
        def _():
            dq_raw, gq = norm_bwd(prev_ref[:, 0:128], dq_prev[...] * QK_SCALE, qw_ref[...], e)
            dk_raw, gk = norm_bwd(prev_ref[:, 128:256], dk_acc[pl.ds(0, T), :] * LN2, kw_ref[...], e)
            out_ref[:, 0:128] = dq_raw.astype(BF16)
            out_ref[:, 128:256] = dk_raw.astype(BF16)
            out_ref[:, 256:384] = dv_acc[pl.ds(0, T), :].astype(BF16)
            gqk_ref[0:1, :] += gq
            gqk_ref[1:2, :] += gk

        dq_prev[...] = dq_cur[...]
        dk_acc[pl.ds(0, T), :] = dk_acc[pl.ds(T, T), :]
        dv_acc[pl.ds(0, T), :] = dv_acc[pl.ds(T, T), :]
        dk_acc[pl.ds(T, T), :] = jnp.zeros((T, 128), F32)
        dv_acc[pl.ds(T, T), :] = jnp.zeros((T, 128), F32)

    last = nt - 1
    qtile = lambda p, i: (jnp.minimum(i, last), p)
    return pl.pallas_call(
        body, name=f"attn_bwd_g{g}", grid=(4, nt + 1),
        in_specs=[pl.BlockSpec((T, 384), lambda p, i: (jnp.minimum(i, last), g * 4 + p)),
                  pl.BlockSpec((T, 384), lambda p, i: (jnp.maximum(i - 1, 0), g * 4 + p)),
                  pl.BlockSpec((T, 128), qtile), pl.BlockSpec((T, 128), qtile), pl.BlockSpec((T, 128), qtile),
                  pl.BlockSpec((1, 128), lambda p, i: (0, 0)),
                  pl.BlockSpec((1, 128), lambda p, i: (0, 0)),
                  pl.BlockSpec((128, 128), lambda p, i: (0, 0)),
                  ANY],
        out_specs=[pl.BlockSpec((T, 384), lambda p, i: (jnp.maximum(i - 1, 0), g * 4 + p)),
                   pl.BlockSpec((None, 8, 128), lambda p, i: (p, 0, 0))],
        out_shape=[jax.ShapeDtypeStruct(dprojq.shape, BF16), jax.ShapeDtypeStruct((4, 8, 128), F32)],
        scratch_shapes=[pltpu.VMEM((T, 128), sdt), pltpu.VMEM((2 * T, 128), sdt), pltpu.VMEM((2 * T, 128), sdt),
                        pltpu.VMEM((T, 128), sdt), pltpu.VMEM((T, 128), F32), pltpu.VMEM((T, 128), F32),
                        pltpu.VMEM((2 * T, 128), F32), pltpu.VMEM((2 * T, 128), F32)],
        input_output_aliases={8: 0},
        compiler_params=_params(),
    )(projq, projq, d_o, lse, delta, qnw2, knw2, ebd, dprojq)


def _mid(projn, o_g, lse_g, x, tgt, mod, convw8, wc, wct, wa, wo, wot, ebd, *, tm=256):
    S, D = x.shape
    nt = S // tm
    nst = max(1, 256 // tm)
    assert nt % nst == 0
    hb = tm // 16

    def body(pn_ref, hc_ref, hx_ref, o0_ref, o1_ref, o2_ref, l0_ref, l1_ref, l2_ref, x_ref, t_ref, mod_ref,
             cw_ref, ebd_ref, wc_hbm, wct_hbm, wa_hbm, wo_hbm, wot_hbm,
             dpn_ref, do_ref, lse_ref, dl_ref, dout_ref, vacc_ref, gwo_hbm, gwc_hbm, gwa_hbm,
             wc_s, wct_s, wa_s, wo_s, wot_s, gwo_s, gwc_s, gwa_s, carry_s, *stashes):
        i = pl.program_id(0)
        ti = nt - 1 - i
        par = i % nst
        stash = pl.ds(pl.multiple_of(par * tm, tm), tm)

        @pl.when(i == 0)
        def _():
            for src, dst in ((wc_hbm, wc_s), (wct_hbm, wct_s), (wa_hbm, wa_s), (wo_hbm, wo_s), (wot_hbm, wot_s)):
                pltpu.sync_copy(src, dst)
            gwo_s[...] = jnp.zeros_like(gwo_s)
            gwc_s[...] = jnp.zeros_like(gwc_s)
            gwa_s[...] = jnp.zeros_like(gwa_s)
            carry_s[...] = jnp.zeros_like(carry_s)
            vacc_ref[...] = jnp.zeros_like(vacc_ref)

        rows = lax.broadcasted_iota(jnp.int32, (tm, D), 0)
        w0, w1, w2 = cw_ref[0:1, :], cw_ref[1:2, :], cw_ref[2:3, :]
        gate = mod_ref[:, 2 * D:3 * D]

        b_a = pn_ref[:, BA:BA + D].astype(F32)
        c_a = pn_ref[:, CA:CA + D].astype(F32)
        x_a = pn_ref[:, XA:XA + D].astype(F32)
        z_a = pn_ref[:, ZA:ZA + D].astype(F32)
        u = c_a * x_a
        has_prev = (ti > 0).astype(F32)
        uh = hc_ref[...].astype(F32) * hx_ref[...].astype(F32) * has_prev
        h14, h15 = _row(uh, 14), _row(uh, 15)
        u_m1 = jnp.where(rows == 0, h15, pltpu.roll(u, 1, 0))
        u_m2 = jnp.where(rows == 0, h14, jnp.where(rows == 1, h15, pltpu.roll(u, 2, 0)))
        conv = w0 * u_m2 + w1 * u_m1 + w2 * u
        sg_a = _sigmoid(z_a)
        sz_a = z_a * sg_a
        y_a = b_a * conv * sz_a
        y_ab = y_a.astype(BF16)
        pa = jnp.dot(y_ab, wc_s[...], preferred_element_type=F32)

        l0, l1, l2 = l0_ref[...], l1_ref[...], l2_ref[...]
        mxl = jnp.maximum(jnp.maximum(l0, l1), l2)
        e0, e1, e2 = jnp.exp(l0 - mxl), jnp.exp(l1 - mxl), jnp.exp(l2 - mxl)
        den = e0 + e1 + e2
        attn = (e0 * o0_ref[...] + e1 * o1_ref[...] + e2 * o2_ref[...]) / den
        lse_ref[...] = mxl + jnp.log(den)
        z_b = pn_ref[:, ZB:ZB + ATTN_OUT].astype(F32)
        sg_b = _sigmoid(z_b)
        sz_b = z_b * sg_b
        y_b = attn * sz_b
        y_bb = y_b.astype(BF16)
        pb = jnp.dot(y_bb, wa_s[...], preferred_element_type=F32)

        s_a = _sigmoid(pn_ref[:, GA:GA + D].astype(F32))
        s_b = _sigmoid(pn_ref[:, GB:GB + D].astype(F32))
        merged = s_a * pa + s_b * pb
        merged_b = merged.astype(BF16)
        mo = jnp.dot(merged_b, wo_s[...], preferred_element_type=F32)
        diff = x_ref[...] + gate * mo - t_ref[...]
        dout = diff * (1.0 / D)
        dout_ref[...] = dout
        vacc_ref[4:5, :] += jnp.sum(diff * diff, axis=0, keepdims=True)
        vacc_ref[0:1, :] += jnp.sum(dout * mo, axis=0, keepdims=True)

        d_mo = (dout * gate).astype(BF16)
        dmerged = jnp.dot(d_mo, wot_s[...], preferred_element_type=F32)
        dpa = (dmerged * s_a).astype(BF16)
        dpb = (dmerged * s_b).astype(BF16)
        dpn_ref[:, GA:GA + D] = (dmerged * pa * s_a * (1.0 - s_a)).astype(BF16)
        dpn_ref[:, GB:GB + D] = (dmerged * pb * s_b * (1.0 - s_b)).astype(BF16)
        tn = (((0,), (0,)), ((), ()))
        if nst == 1:
            gwo_s[...] += lax.dot_general(merged_b, d_mo, tn, preferred_element_type=F32)
            gwc_s[...] += lax.dot_general(y_ab, dpa, tn, preferred_element_type=F32)
            gwa_s[...] += lax.dot_general(y_bb, dpb, tn, preferred_element_type=F32)
        else:
            st_m, st_d, st_ya, st_pa, st_yb, st_pb = stashes
            st_m[stash, :] = merged_b
            st_d[stash, :] = d_mo
            st_ya[stash, :] = y_ab
            st_pa[stash, :] = dpa
            st_yb[stash, :] = y_bb
            st_pb[stash, :] = dpb

            @pl.when(par == nst - 1)
            def _():
                gwo_s[...] += lax.dot_general(st_m[...], st_d[...], tn, preferred_element_type=F32)
                gwc_s[...] += lax.dot_general(st_ya[...], st_pa[...], tn, preferred_element_type=F32)
                gwa_s[...] += lax.dot_general(st_yb[...], st_pb[...], tn, preferred_element_type=F32)

        dy_a = jnp.dot(dpa, wct_s[...], preferred_element_type=F32)
        dy_b = lax.dot_general(dpb, wa_s[...], (((1,), (1,)), ((), ())), preferred_element_type=F32)

        dattn = dy_b * sz_b
        dpn_ref[:, ZB:ZB + ATTN_OUT] = (dy_b * attn * sg_b * (1.0 + z_b * (1.0 - sg_b))).astype(BF16)
        do_ref[...] = dattn.astype(BF16)
        dl_ref[...] = jnp.dot((dattn * attn).astype(BF16), ebd_ref[...], preferred_element_type=F32)

        dpn_ref[:, BA:BA + D] = (dy_a * conv * sz_a).astype(BF16)
        dpn_ref[:, ZA:ZA + D] = (dy_a * b_a * conv * sg_a * (1.0 + z_a * (1.0 - sg_a))).astype(BF16)
        dconv = dy_a * b_a * sz_a
        vacc_ref[1:2, :] += jnp.sum(dconv * u_m2, axis=0, keepdims=True)
        vacc_ref[2:3, :] += jnp.sum(dconv * u_m1, axis=0, keepdims=True)
        vacc_ref[3:4, :] += jnp.sum(dconv * u, axis=0, keepdims=True)
        nxt = carry_s[...]
        n0, n1 = _row(nxt, 0), _row(nxt, 1)
        dc_p1 = jnp.where(rows == tm - 1, n0, pltpu.roll(dconv, tm - 1, 0))
        dc_p2 = jnp.where(rows == tm - 1, n1, jnp.where(rows == tm - 2, n0, pltpu.roll(dconv, tm - 2, 0)))
        du = w2 * dconv + w1 * dc_p1 + w0 * dc_p2
        carry_s[...] = dconv[0:8, :]
        dpn_ref[:, CA:CA + D] = (du * x_a).astype(BF16)
        dpn_ref[:, XA:XA + D] = (du * c_a).astype(BF16)

        @pl.when(i == nt - 1)
        def _():
            pltpu.sync_copy(gwo_s, gwo_hbm)
            pltpu.sync_copy(gwc_s, gwc_hbm)
            pltpu.sync_copy(gwa_s, gwa_hbm)

    rev = lambda i: (nt - 1 - i, 0)
    halo = lambda col: pl.BlockSpec((16, D), lambda i: (jnp.maximum((nt - 1 - i) * hb - 1, 0), col))
    tile512 = pl.BlockSpec((tm, ATTN_OUT), rev)
    tileD = pl.BlockSpec((tm, D), rev)
    const = lambda shape: pl.BlockSpec(shape, lambda i: (0, 0))
    return pl.pallas_call(
        body, name="mid_fwd_bwd", grid=(nt,),
        in_specs=[pl.BlockSpec((tm, NAT), rev), halo(CA // D), halo(XA // D)] + [tile512] * 6
                 + [tileD, tileD, const((1, 3 * D)), const((8, D)), const((ATTN_OUT, ATTN_OUT))] + [ANY] * 5,
        out_specs=[pl.BlockSpec((tm, NAT), rev), tile512, tile512, tile512, tileD, const((8, D)), ANY, ANY, ANY],
        out_shape=[jax.ShapeDtypeStruct((S, NAT), BF16), jax.ShapeDtypeStruct((S, ATTN_OUT), BF16),
                   jax.ShapeDtypeStruct((S, ATTN_OUT), F32), jax.ShapeDtypeStruct((S, ATTN_OUT), F32),
                   jax.ShapeDtypeStruct((S, D), F32), jax.ShapeDtypeStruct((8, D), F32),
                   jax.ShapeDtypeStruct((D, D), F32), jax.ShapeDtypeStruct((D, D), F32),
                   jax.ShapeDtypeStruct((ATTN_OUT, D), F32)],
        scratch_shapes=[pltpu.VMEM((D, D), BF16), pltpu.VMEM((D, D), BF16), pltpu.VMEM((ATTN_OUT, D), BF16),
                        pltpu.VMEM((D, D), BF16), pltpu.VMEM((D, D), BF16),
                        pltpu.VMEM((D, D), F32), pltpu.VMEM((D, D), F32), pltpu.VMEM((ATTN_OUT, D), F32),
                        pltpu.VMEM((8, D), F32)]
                       + ([pltpu.VMEM((nst * tm, D), BF16)] * 4
                          + [pltpu.VMEM((nst * tm, ATTN_OUT), BF16), pltpu.VMEM((nst * tm, D), BF16)] if nst > 1 else []),
        compiler_params=_params(vmem=V7X_VMEM_BYTES - 64 * 1024),
    )(projn, projn, projn, *o_g, *lse_g, x, tgt, mod, convw8, ebd, wc, wct, wa, wo, wot)


def _dh_norm(dpn, dpq, wnt, wqt, x, dout, mod, norm_w, parts, *, tm=256):
    S, D = x.shape
    n = len(parts)
    nt = S // tm

    def body(dn_ref, dq_ref, x_ref, dout_ref, mod_ref, nw_ref, wn_hbm, wq_hbm, *rest):
        p_hbm, rest = rest[:n], rest[n:]
        gx_ref, st_ref = rest[0], rest[1]
        q_hbm = rest[2:2 + n]
        wn_s, wq_s = rest[2 + n:4 + n]
        sems = rest[4 + n:]

        @pl.when(pl.program_id(0) == 0)
        def _():
            if n:
                mine, _ = _scatter_copies(p_hbm, q_hbm, *sems)
                for cp in mine:
                    cp.start()
            pltpu.sync_copy(wn_hbm, wn_s)
            pltpu.sync_copy(wq_hbm, wq_s)
            st_ref[...] = jnp.zeros_like(st_ref)

        if n:
            @pl.when(pl.program_id(0) == nt - 1)
            def _():
                mine, theirs = _scatter_copies(p_hbm, q_hbm, *sems)
                for cp in theirs:
                    cp.wait_recv()
                for cp in mine:
                    cp.wait_send()

        dh = (jnp.dot(dn_ref[...], wn_s[...], preferred_element_type=F32)
              + jnp.dot(dq_ref[...], wq_s[...], preferred_element_type=F32))
        scale1 = 1.0 + mod_ref[:, D:2 * D]
        nw = nw_ref[...]
        xv = x_ref[...]
        rstd = lax.rsqrt(jnp.mean(xv * xv, axis=-1, keepdims=True) + EPS)
        xh = xv * rstd
        dhs = dh * scale1
        dxh = dhs * nw
        dx = rstd * (dxh - xh * jnp.mean(dxh * xh, axis=-1, keepdims=True))
        gx_ref[...] = dout_ref[...] + dx
        st_ref[0:1, :] += jnp.sum(dh, axis=0, keepdims=True)
        st_ref[1:2, :] += jnp.sum(dh * (xh * nw), axis=0, keepdims=True)
        st_ref[2:3, :] += jnp.sum(dhs * xh, axis=0, keepdims=True)

    row = lambda w: pl.BlockSpec((tm, w), lambda i: (i, 0))
    outs = pl.pallas_call(
        body, name="dh_norm_bwd", grid=(nt,),
        in_specs=[row(NAT), row(NQKV), row(D), row(D),
                  pl.BlockSpec((1, 3 * D), lambda i: (0, 0)), pl.BlockSpec((1, D), lambda i: (0, 0)), ANY, ANY]
                 + [ANY] * n,
        out_specs=[row(D), pl.BlockSpec((8, D), lambda i: (0, 0))] + [ANY] * n,
        out_shape=[jax.ShapeDtypeStruct((S, D), F32), jax.ShapeDtypeStruct((8, D), F32)]
                  + [jax.ShapeDtypeStruct(p.shape, p.dtype) for p in parts],
        scratch_shapes=[pltpu.VMEM((NAT, D), BF16), pltpu.VMEM((NQKV, D), BF16)]
                       + ([pltpu.SemaphoreType.DMA((n, 3)), pltpu.SemaphoreType.DMA((n, 3))] if n else []),
        compiler_params=_params(),
    )(dpn, dpq, x, dout, mod, norm_w, wnt, wqt, *parts)
    return outs[0], outs[1], outs[2:]


def _pack_smalls(st, vacc, gqk):
    D = D_MODEL

    def body(st_ref, va_ref, gqk_ref, o_ref):
        o_ref[...] = jnp.zeros_like(o_ref)
        o_ref[0:2, :] = st_ref[0:2, :]
        o_ref[2:3, :] = va_ref[0:1, :]
        o_ref[8:9, :] = st_ref[2:3, :]
        o_ref[16:19, :] = va_ref[1:4, :]
        tot = gqk_ref[0]
        for k in range(1, gqk_ref.shape[0]):
            tot = tot + gqk_ref[k]
        tot = tot + pltpu.roll(tot, HEAD_DIM, 1)
        o_ref[24:32, 0:128] = tot
        loss = jnp.sum(va_ref[4:5, :], axis=1, keepdims=True) * (0.5 / D)
        o_ref[26:27, :] = jnp.broadcast_to(loss, (1, D))

    return pl.pallas_call(
        body, name="pack_smalls", out_shape=jax.ShapeDtypeStruct((32, D), F32),
        in_specs=[pl.BlockSpec(memory_space=pltpu.VMEM)] * 3,
        out_specs=pl.BlockSpec(memory_space=pltpu.VMEM), compiler_params=_params(),
    )(st, vacc, gqk)


def _small_update(sm_loc, dm_pad, ct_pad, w_ada, m_ada, v_ada, vec_w, vec_m, vec_v):
    D = D_MODEL
    cols = w_ada.shape[1]

    def body(sm_ref, dm_ref, ct_ref, w_ref, m_ref, v_ref, vw_ref, vm_ref, vv_ref,
             gw_ref, dw_ref, mw_ref, vw_o_ref, gv_ref, dv_ref, mv_ref, vv_o_ref):
        g = sm_ref[0]
        for k in range(1, 8):
            g = g + sm_ref[k]
        gv_ref[...] = g
        dv, mv, vv = _adamw_math(vw_ref[...], g, vm_ref[...], vv_ref[...])
        dv_ref[...] = dv
        mv_ref[...] = mv
        vv_o_ref[...] = vv
        gw = jnp.dot(ct_ref[...], dm_ref[...], preferred_element_type=F32, precision=lax.Precision.HIGHEST)
        gw_ref[...] = gw
        dw, mw, vw = _adamw_math(w_ref[...], gw, m_ref[...], v_ref[...])
        dw_ref[...] = dw
        mw_ref[...] = mw
        vw_o_ref[...] = vw

    sw = jax.ShapeDtypeStruct((D, cols), F32)
    sv = jax.ShapeDtypeStruct((32, D), F32)
    return pl.pallas_call(
        body, name="small_update", out_shape=[sw, sw, sw, sw, sv, sv, sv, sv],
        in_specs=[pl.BlockSpec(memory_space=pltpu.VMEM)] * 9,
        out_specs=[pl.BlockSpec(memory_space=pltpu.VMEM)] * 8, compiler_params=_params(),
    )(sm_loc, dm_pad, ct_pad, w_ada, m_ada, v_ada, vec_w, vec_m, vec_v)


def _local_step(xs, tg, mod, norm_w, w_nat, w_qkv, wc, wa, wo, convw8, q_norm_w, k_norm_w):
    S = xs.shape[0]
    lane = jnp.arange(128)
    ebd128 = (lane[:, None] // HEAD_DIM == lane[None, :] // HEAD_DIM).astype(BF16)
    lane5 = jnp.arange(ATTN_OUT)
    ebd512 = (lane5[:, None] // HEAD_DIM == lane5[None, :] // HEAD_DIM).astype(BF16)
    qnw2 = jnp.tile(q_norm_w, (1, 2))
    knw2 = jnp.tile(k_norm_w, (1, 2))

    projn, h, ht = _inproj_nat(xs, mod, norm_w, w_nat)
    projq = _matmul(h, w_qkv, tm=1024, tn=1536, name="inproj_qkv")
    o_g, lse_g = [], []
    for g in range(N_GROUPS):
        o, l = _attn_fwd(projq, qnw2, knw2, ebd128, g)
        o_g.append(o)
        lse_g.append(l)

    dpn, d_o, lse, delta, dout, vacc, gwo, gwc, gwa = _mid(
        projn, o_g, lse_g, xs, tg, mod, convw8, wc, wc.T, wa, wo, wo.T, ebd512)
    dpq = lax.empty((S, NQKV), BF16)
    gqk = []
    for g in range(N_GROUPS):
        dpq, part = _attn_bwd(projq, dpq, d_o, lse, delta, qnw2, knw2, ebd128, g)
        gqk.append(part)
    gw_nat = _matmul_acc(ht, dpn, tn=1664, tk=1024, name="grad_w_in_nat")
    gw_qkv = _matmul_acc(ht, dpq, tn=1536, tk=1024, name="grad_w_in_qkv")
    return dpn, dpq, dout, gw_nat, gw_qkv, gwc, gwa, gwo, vacc, gqk


def kernel(x, c, w_ada, b_ada, norm_w, w_in, conv_w, q_norm_w, k_norm_w, w_br_conv, w_br_attn, w_out, loss_target, m_w_ada, m_b_ada, m_norm_w, m_w_in, m_conv_w, m_q_norm_w, m_k_norm_w, m_w_br_conv, m_w_br_attn, m_w_out, v_w_ada, v_b_ada, v_norm_w, v_w_in, v_conv_w, v_q_norm_w, v_k_norm_w, v_w_br_conv, v_w_br_attn, v_w_out):
    D = D_MODEL
    S = x.shape[1]
    xs, tg = x[0], loss_target[0]
    mx, my, mc = lax.axis_index("x"), lax.axis_index("y"), lax.axis_index("c")
    chip = 2 * mx + my
    dev = 2 * chip + mc

    c_all = _allgather8(jnp.pad(c, ((0, 7), (0, 0))), "allgather_c")[:, 0, :]
    b_shard = lax.dynamic_slice(b_ada, (0, chip * 768), (1, 768))
    mod_part, silu_c = _mod_part(c_all, w_ada[0], b_shard)
    mod_parts = _allgather_chips(mod_part, "allgather_mod")
    mod_all = mod_parts.transpose(1, 0, 2).reshape(8, 3 * D)
    mod = lax.dynamic_slice(mod_all, (dev, 0), (1, 3 * D))

    shards = [w_in[0].astype(BF16).reshape(2, 512, 2816), w_br_conv[0].astype(BF16).reshape(2, 128, D),
              w_br_attn[0].astype(BF16).reshape(2, 256, 256), w_out[0].astype(BF16).reshape(2, 128, D)]
    gathered_w = _weights_allgather(shards)
    g_bc, g_ba, g_wo = [lax.dynamic_update_slice(g, s[None], (chip, 0, 0, 0))
                        for g, s in zip(gathered_w[1:], shards[1:])]
    chipv = jnp.reshape(chip, (1,)).astype(jnp.int32)
    g_in = gathered_w[0].reshape(4, D, SHARD_COLS)
    own_in = shards[0].reshape(D, SHARD_COLS)
    w_nat, w_nat_t = _weights_prep(g_in, own_in, chipv, "nat")
    w_qkv, w_qkv_t = _weights_prep(g_in, own_in, chipv, "qkv")
    wc = g_bc.reshape(D, D)
    wa = g_ba.reshape(4, ATTN_OUT, 256).transpose(1, 0, 2).reshape(ATTN_OUT, D)
    wo = g_wo.reshape(D, D)
    conv_full = _allgather_chips(jnp.pad(conv_w[0], ((0, 5), (0, 0))), "allgather_conv")
    convw8 = conv_full.transpose(1, 0, 2).reshape(8, D)

    dpn, dpq, dout, gw_nat, gw_qkv, gwc, gwa, gwo, vacc, gqk = _local_step(
        xs, tg, mod, norm_w, w_nat, w_qkv, wc, wa, wo, convw8, q_norm_w, k_norm_w)
    gw_in = _grad_relayout(gw_nat, lax.empty((2, 4, D // 2, SHARD_COLS), BF16), "nat")
    gw_in = _grad_relayout(gw_qkv, gw_in, "qkv")

    def halves(a):
        k, r, cc = a.shape
        return a.reshape(k, 2, r // 2, cc).transpose(1, 0, 2, 3)

    grads = [gw_in,
             halves(gwc.astype(BF16).reshape(4, 256, D)),
             halves(gwa.astype(BF16).reshape(ATTN_OUT, 4, 256).transpose(1, 0, 2)),
             halves(gwo.astype(BF16).reshape(4, 256, D))]
    recv = _pair_exchange(grads)
    core = jnp.reshape(mc, (1,)).astype(jnp.int32)
    parts = [_pair_add(core, gr, rc) for gr, rc in zip(grads, recv)]
    grad_x, st, gathered = _dh_norm(dpn, dpq, w_nat_t, w_qkv_t, xs, dout, mod, norm_w, parts)
    mine = [_sum4(chipv, p, q) for p, q in zip(parts, gathered)]
    theirs = _half_exchange(mine)

    big = {}
    for t, (nm, w, m, v) in enumerate((("w_in", w_in, m_w_in, v_w_in), ("w_br_conv", w_br_conv, m_w_br_conv, v_w_br_conv),
                                       ("w_br_attn", w_br_attn, m_w_br_attn, v_w_br_attn), ("w_out", w_out, m_w_out, v_w_out))):
        big[nm] = _adamw_halves(core, w[0], mine[t], theirs[t], m[0], v[0], "adamw_" + nm)

    sm_all = _allgather8(_pack_smalls(st, vacc, jnp.concatenate(gqk, axis=0)), "allgather_smalls")
    dmod_all = sm_all[:, 0:3, :].reshape(8, 3 * D)
    dm_pad = jnp.pad(lax.dynamic_slice(dmod_all, (0, chip * 768), (8, 768)), ((0, 120), (0, 0)))
    conv_dev = lax.dynamic_slice(sm_all[:, 16:24, :], (0, 0, chip * 256), (8, 8, 256))
    sm_loc = jnp.concatenate([sm_all[:, 0:16, :], jnp.pad(conv_dev, ((0, 0), (0, 0), (0, D - 256))), sm_all[:, 24:32, :]], axis=1)

    def pack_vec(b3, nw, cw, qw, kw):
        z = jnp.zeros((32, D), F32)
        z = z.at[0:3, :].set(b3.reshape(3, D)).at[8:9, :].set(nw).at[16:19, 0:256].set(cw)
        return z.at[24:25, 0:HEAD_DIM].set(qw).at[25:26, 0:HEAD_DIM].set(kw)

    vec_w = pack_vec(b_ada, norm_w, conv_w[0], q_norm_w, k_norm_w)
    vec_m = pack_vec(m_b_ada, m_norm_w, m_conv_w[0], m_q_norm_w, m_k_norm_w)
    vec_v = pack_vec(v_b_ada, v_norm_w, v_conv_w[0], v_q_norm_w, v_k_norm_w)
    ct_pad = jnp.pad(silu_c.T, ((0, 0), (0, 120)))
    outs = _small_update(sm_loc, dm_pad, ct_pad, w_ada[0], m_w_ada[0], v_w_ada[0], vec_w, vec_m, vec_v)
    ada = outs[0:4]
    vec = outs[4:8]
    loss = vec[0][26, 0]

    def unpack(t):
        return {"b_ada": t[0:3, :].reshape(1, 3 * D), "norm_w": t[8:9, :], "conv_w": t[16:19, 0:256][None],
                "q_norm_w": t[24:25, 0:HEAD_DIM], "k_norm_w": t[25:26, 0:HEAD_DIM]}

    small = [unpack(t) for t in vec]
    order = ["w_ada", "b_ada", "norm_w", "w_in", "conv_w", "q_norm_w", "k_norm_w", "w_br_conv", "w_br_attn", "w_out"]
    res = [loss, grad_x[None]]
    for kind in range(4):
        for nm in order:
            if nm == "w_ada":
                res.append(ada[kind][None])
            elif nm in big:
                res.append(big[nm][kind][None])
            else:
                res.append(small[kind][nm])
    return tuple(res)
```

```python
import functools

import jax
import jax.numpy as jnp
from jax import lax
from jax.experimental import pallas as pl
from jax.experimental.pallas import tpu as pltpu

F32 = jnp.float32
BF16 = jnp.bfloat16
MESH = pl.DeviceIdType.MESH
ANY = pl.BlockSpec(memory_space=pl.ANY)

D_MODEL = 1024
HEAD_DIM = 64
N_GROUPS = 3
DILATIONS = (1, 4, 16)
ATTN_OUT = 512
EPS = 1e-6
NEG_INF = -1e30
NAT = 6656
NQKV = 4608
BA, CA, XA, ZA, ZB, GA, GB = 0, 1024, 2048, 3072, 4096, 4608, 5632
ATT_TILE = 2048
QK_SCALE = HEAD_DIM ** -0.5
LOG2E = 1.4426950408889634
LN2 = 0.6931471805599453
V7X_VMEM_BYTES = 64 * 1024 * 1024
V7X_VMEM_LIMIT = 56 * 1024 * 1024

ADAM_LR = 0.001
ADAM_B1 = 0.9
ADAM_B2 = 0.999
ADAM_EPS = 1e-08
ADAM_WD = 0.01
ADAM_STEP = 10


def _params(vmem=V7X_VMEM_LIMIT, **kw):
    return pltpu.CompilerParams(vmem_limit_bytes=vmem, **kw)


def _sigmoid(z):
    return 1.0 / (1.0 + jnp.exp(-z))


def _row(v, r):
    rows = lax.broadcasted_iota(jnp.int32, v.shape, 0)
    return jnp.sum(jnp.where(rows == r, v, 0.0), axis=0, keepdims=True)


def _coords():
    return lax.axis_index("x"), lax.axis_index("y"), lax.axis_index("c")


def _flip(v, bit):
    return 1 - v if bit else v


def _allgather8(blk, name):
    r, c = blk.shape

    def body(x_ref, o_ref, ssem, rsem, lsem):
        mx, my, mc = _coords()
        me = 4 * mx + 2 * my + mc
        local = pltpu.make_async_copy(x_ref, o_ref.at[me], lsem)
        local.start()
        sends = []
        for j in range(1, 8):
            peer = (_flip(mx, j & 4), _flip(my, j & 2), _flip(mc, j & 1))
            cp = pltpu.make_async_remote_copy(
                src_ref=x_ref, dst_ref=o_ref.at[me], send_sem=ssem.at[j - 1], recv_sem=rsem.at[j - 1],
                device_id=peer, device_id_type=MESH)
            cp.start()
            sends.append(cp)
        for j in range(1, 8):
            px, py, pc = _flip(mx, j & 4), _flip(my, j & 2), _flip(mc, j & 1)
            pltpu.make_async_remote_copy(
                src_ref=x_ref, dst_ref=o_ref.at[4 * px + 2 * py + pc], send_sem=ssem.at[j - 1],
                recv_sem=rsem.at[j - 1], device_id=(px, py, pc), device_id_type=MESH).wait_recv()
        for cp in sends:
            cp.wait_send()
        local.wait()

    return pl.pallas_call(
        body, name=name,
        out_shape=jax.ShapeDtypeStruct((8, r, c), blk.dtype),
        in_specs=[pl.BlockSpec(memory_space=pltpu.VMEM)],
        out_specs=pl.BlockSpec(memory_space=pltpu.VMEM),
        scratch_shapes=[pltpu.SemaphoreType.DMA((7,)), pltpu.SemaphoreType.DMA((7,)), pltpu.SemaphoreType.DMA(())],
    )(blk)


def _allgather_chips(blk, name):
    r, c = blk.shape

    def body(x_ref, o_ref, ssem, rsem, lsem):
        mx, my, mc = _coords()
        me = 2 * mx + my
        local = pltpu.make_async_copy(x_ref, o_ref.at[me], lsem)
        local.start()
        sends = []
        for j in range(1, 4):
            peer = (_flip(mx, j & 2), _flip(my, j & 1), mc)
            cp = pltpu.make_async_remote_copy(
                src_ref=x_ref, dst_ref=o_ref.at[me], send_sem=ssem.at[j - 1], recv_sem=rsem.at[j - 1],
                device_id=peer, device_id_type=MESH)
            cp.start()
            sends.append(cp)
        for j in range(1, 4):
            px, py = _flip(mx, j & 2), _flip(my, j & 1)
            pltpu.make_async_remote_copy(
                src_ref=x_ref, dst_ref=o_ref.at[2 * px + py], send_sem=ssem.at[j - 1],
                recv_sem=rsem.at[j - 1], device_id=(px, py, mc), device_id_type=MESH).wait_recv()
        for cp in sends:
            cp.wait_send()
        local.wait()

    return pl.pallas_call(
        body, name=name,
        out_shape=jax.ShapeDtypeStruct((4, r, c), blk.dtype),
        in_specs=[pl.BlockSpec(memory_space=pltpu.VMEM)],
        out_specs=pl.BlockSpec(memory_space=pltpu.VMEM),
        scratch_shapes=[pltpu.SemaphoreType.DMA((3,)), pltpu.SemaphoreType.DMA((3,)), pltpu.SemaphoreType.DMA(())],
    )(blk)


def _weights_allgather(shards):
    n = len(shards)

    def body(*refs):
        ins, outs = refs[:n], refs[n:2 * n]
        ssem, rsem = refs[2 * n:]
        mx, my, mc = _coords()
        me = 2 * mx + my
        sib = (mx, my, 1 - mc)
        chips = [(_flip(mx, j & 2), _flip(my, j & 1)) for j in range(1, 4)]
        sends = []
        for t in range(n):
            for j, (px, py) in enumerate(chips):
                cp = pltpu.make_async_remote_copy(
                    src_ref=ins[t].at[mc], dst_ref=outs[t].at[me, mc], send_sem=ssem.at[t, j],
                    recv_sem=rsem.at[t, j], device_id=(px, py, mc), device_id_type=MESH)
                cp.start()
                sends.append(cp)
        for t in range(n):
            for j, (px, py) in enumerate(chips):
                src = 2 * px + py
                pltpu.make_async_remote_copy(
                    src_ref=ins[t].at[mc], dst_ref=outs[t].at[src, mc], send_sem=ssem.at[t, j],
                    recv_sem=rsem.at[t, j], device_id=(px, py, mc), device_id_type=MESH).wait_recv()
                fwd = pltpu.make_async_remote_copy(
                    src_ref=outs[t].at[src, mc], dst_ref=outs[t].at[src, mc], send_sem=ssem.at[t, 3 + j],
                    recv_sem=rsem.at[t, 3 + j], device_id=sib, device_id_type=MESH)
                fwd.start()
                sends.append(fwd)
        for t in range(n):
            for j, (px, py) in enumerate(chips):
                src = 2 * px + py
                pltpu.make_async_remote_copy(
                    src_ref=outs[t].at[src, 1 - mc], dst_ref=outs[t].at[src, 1 - mc], send_sem=ssem.at[t, 3 + j],
                    recv_sem=rsem.at[t, 3 + j], device_id=sib, device_id_type=MESH).wait_recv()
        for cp in sends:
            cp.wait_send()

    return pl.pallas_call(
        body, name="weights_allgather",
        out_shape=[jax.ShapeDtypeStruct((4,) + s.shape, s.dtype) for s in shards],
        in_specs=[ANY] * n, out_specs=[ANY] * n,
        scratch_shapes=[pltpu.SemaphoreType.DMA((n, 6)), pltpu.SemaphoreType.DMA((n, 6))],
    )(*shards)


def _pair_exchange(grads):
    n = len(grads)

    def body(*refs):
        ins, outs = refs[:n], refs[n:2 * n]
        ssem, rsem = refs[2 * n:]
        mx, my, mc = _coords()
        sib = (mx, my, 1 - mc)
        sends = []
        for t in range(n):
            cp = pltpu.make_async_remote_copy(
                src_ref=ins[t].at[1 - mc], dst_ref=outs[t], send_sem=ssem.at[t], recv_sem=rsem.at[t],
                device_id=sib, device_id_type=MESH)
            cp.start()
            sends.append(cp)
        for cp in sends:
            cp.wait_recv()
        for cp in sends:
            cp.wait_send()

    return pl.pallas_call(
        body, name="grad_pair_exchange",
        out_shape=[jax.ShapeDtypeStruct(g.shape[1:], g.dtype) for g in grads],
        in_specs=[ANY] * n, out_specs=[ANY] * n,
        scratch_shapes=[pltpu.SemaphoreType.DMA((n,)), pltpu.SemaphoreType.DMA((n,))],
    )(*grads)


def _scatter_copies(ins, outs, ssem, rsem):
    mx, my, mc = _coords()
    me = 2 * mx + my
    mine, theirs = [], []
    for t in range(len(ins)):
        for j in range(1, 4):
            px, py = _flip(mx, j & 2), _flip(my, j & 1)
            mine.append(pltpu.make_async_remote_copy(
                src_ref=ins[t].at[2 * px + py], dst_ref=outs[t].at[me], send_sem=ssem.at[t, j - 1],
                recv_sem=rsem.at[t, j - 1], device_id=(px, py, mc), device_id_type=MESH))
            theirs.append(pltpu.make_async_remote_copy(
                src_ref=ins[t].at[me], dst_ref=outs[t].at[2 * px + py], send_sem=ssem.at[t, j - 1],
                recv_sem=rsem.at[t, j - 1], device_id=(px, py, mc), device_id_type=MESH))
    return mine, theirs


def _half_exchange(halves):
    n = len(halves)

    def body(*refs):
        ins, outs = refs[:n], refs[n:2 * n]
        ssem, rsem = refs[2 * n:]
        mx, my, mc = _coords()
        sib = (mx, my, 1 - mc)
        sends = []
        for t in range(n):
            rc = pltpu.make_async_remote_copy(
                src_ref=ins[t], dst_ref=outs[t], send_sem=ssem.at[t], recv_sem=rsem.at[t],
                device_id=sib, device_id_type=MESH)
            rc.start()
            sends.append(rc)
        for cp in sends:
            cp.wait_recv()
        for cp in sends:
            cp.wait_send()

    return pl.pallas_call(
        body, name="grad_half_exchange",
        out_shape=[jax.ShapeDtypeStruct(h.shape, h.dtype) for h in halves],
        in_specs=[ANY] * n, out_specs=[ANY] * n,
        scratch_shapes=[pltpu.SemaphoreType.DMA((n,)), pltpu.SemaphoreType.DMA((n,))],
    )(*halves)


def _row_tile(rows, cols, bytes_per_row_elem=4, budget=2 * 1024 * 1024):
    t = rows
    while t % 2 == 0 and t > 16 and t * cols * bytes_per_row_elem > budget:
        t //= 2
    return t


def _pair_add(core, g, r):
    _, _, rh, cc = g.shape
    tr = _row_tile(rh, cc)

    def body(core_ref, g_ref, r_ref, o_ref):
        o_ref[...] = (g_ref[...].astype(F32) + r_ref[...].astype(F32)).astype(o_ref.dtype)

    return pl.pallas_call(
        body, name="grad_pair_add",
        grid_spec=pltpu.PrefetchScalarGridSpec(
            num_scalar_prefetch=1, grid=(4, rh // tr),
            in_specs=[pl.BlockSpec((None, None, tr, cc), lambda k, i, c: (c[0], k, i, 0)),
                      pl.BlockSpec((None, tr, cc), lambda k, i, c: (k, i, 0))],
            out_specs=pl.BlockSpec((None, tr, cc), lambda k, i, c: (k, i, 0))),
        out_shape=jax.ShapeDtypeStruct(r.shape, BF16),
        compiler_params=_params(),
    )(core, g, r)


def _sum4(chip, p, q):
    _, rh, cc = q.shape
    tr = _row_tile(rh, cc)

    def body(chip_ref, p_ref, q1_ref, q2_ref, q3_ref, o_ref):
        o_ref[...] = ((p_ref[...].astype(F32) + q1_ref[...].astype(F32)) + q2_ref[...].astype(F32)) + q3_ref[...].astype(F32)

    def other(j):
        return pl.BlockSpec((None, tr, cc), lambda i, c: (jnp.bitwise_xor(c[0], j), i, 0))

    return pl.pallas_call(
        body, name="grad_sum4",
        grid_spec=pltpu.PrefetchScalarGridSpec(
            num_scalar_prefetch=1, grid=(rh // tr,),
            in_specs=[pl.BlockSpec((None, tr, cc), lambda i, c: (c[0], i, 0)), other(1), other(2), other(3)],
            out_specs=pl.BlockSpec((tr, cc), lambda i, c: (i, 0))),
        out_shape=jax.ShapeDtypeStruct((rh, cc), F32),
        compiler_params=_params(),
    )(chip, p, q, q, q)


def _adamw_math(w, g, m, v):
    m = ADAM_B1 * m + (1.0 - ADAM_B1) * g
    v = ADAM_B2 * v + (1.0 - ADAM_B2) * (g * g)
    m_hat = m / (1.0 - ADAM_B1 ** ADAM_STEP)
    v_hat = v / (1.0 - ADAM_B2 ** ADAM_STEP)
    delta = -ADAM_LR * (m_hat / (jnp.sqrt(v_hat) + ADAM_EPS) + ADAM_WD * w)
    return delta, m, v


def _adamw_halves(core, w, mine, recv, m, v, name):
    rows, cols = w.shape
    rh = rows // 2
    tr = _row_tile(rh, cols, budget=1024 * 1024)
    nh = rh // tr

    def body(core_ref, w_ref, a_ref, b_ref, m_ref, v_ref, g_ref, d_ref, mo_ref, vo_ref):
        g = jnp.where(pl.program_id(0) == core_ref[0], a_ref[...], b_ref[...])
        d, mn, vn = _adamw_math(w_ref[...], g, m_ref[...], v_ref[...])
        g_ref[...] = g
        d_ref[...] = d
        mo_ref[...] = mn
        vo_ref[...] = vn

    full = pl.BlockSpec((tr, cols), lambda h, i, c: (h * nh + i, 0))
    half = pl.BlockSpec((tr, cols), lambda h, i, c: (i, 0))
    sd = jax.ShapeDtypeStruct((rows, cols), F32)
    return pl.pallas_call(
        body, name=name,
        grid_spec=pltpu.PrefetchScalarGridSpec(
            num_scalar_prefetch=1, grid=(2, nh),
            in_specs=[full, half, half, full, full], out_specs=[full] * 4),
        out_shape=[sd, sd, sd, sd], compiler_params=_params(),
    )(core, w, mine, recv, m, v)


def _mod_part(c_all, w_ada, b_shard):
    cols = w_ada.shape[1]

    def body(c_ref, w_ref, b_ref, o_ref, sc_ref):
        cv = c_ref[...]
        sc = cv * _sigmoid(cv)
        sc_ref[...] = sc
        o_ref[...] = jnp.dot(sc, w_ref[...], preferred_element_type=F32,
                             precision=lax.Precision.HIGHEST) + b_ref[...]

    return pl.pallas_call(
        body, name="adaln_mod",
        out_shape=[jax.ShapeDtypeStruct((8, cols), F32), jax.ShapeDtypeStruct(c_all.shape, F32)],
        in_specs=[pl.BlockSpec(memory_space=pltpu.VMEM)] * 3,
        out_specs=[pl.BlockSpec(memory_space=pltpu.VMEM)] * 2, compiler_params=_params(),
    )(c_all, w_ada, b_shard)


def _inproj_nat(x, mod, norm_w, w, *, tm=1024, tn=1664):
    S, D = x.shape
    N = w.shape[1]
    chunk = 256

    def body(x_ref, mod_ref, nw_ref, w_ref, proj_ref, h_ref, ht_ref):
        @pl.when(pl.program_id(1) == 0)
        def _():
            shift = mod_ref[:, 0:D]
            scale1 = 1.0 + mod_ref[:, D:2 * D]
            for r in range(tm // chunk):
                xv = x_ref[pl.ds(r * chunk, chunk), :]
                ms = jnp.mean(xv * xv, axis=-1, keepdims=True)
                h = (xv * lax.rsqrt(ms + EPS) * nw_ref[...]) * scale1 + shift
                h_ref[pl.ds(r * chunk, chunk), :] = h.astype(BF16)
                ht_ref[:, pl.ds(r * chunk, chunk)] = h.T.astype(BF16)
        proj_ref[...] = jnp.dot(h_ref[...], w_ref[...], preferred_element_type=F32).astype(BF16)

    return pl.pallas_call(
        body, name="inproj_nat", grid=(S // tm, N // tn),
        in_specs=[pl.BlockSpec((tm, D), lambda i, j: (i, 0)),
                  pl.BlockSpec((1, 3 * D), lambda i, j: (0, 0)),
                  pl.BlockSpec((1, D), lambda i, j: (0, 0)),
                  pl.BlockSpec((D, tn), lambda i, j: (0, j))],
        out_specs=[pl.BlockSpec((tm, tn), lambda i, j: (i, j)),
                   pl.BlockSpec((tm, D), lambda i, j: (i, 0)),
                   pl.BlockSpec((D, tm), lambda i, j: (0, i))],
        out_shape=[jax.ShapeDtypeStruct((S, N), BF16), jax.ShapeDtypeStruct((S, D), BF16),
                   jax.ShapeDtypeStruct((D, S), BF16)],
        compiler_params=_params(),
    )(x, mod, norm_w, w)


def _matmul(a, b, *, tm, tn, name):
    M, K = a.shape
    N = b.shape[1]

    def body(a_ref, b_ref, o_ref):
        o_ref[...] = jnp.dot(a_ref[...], b_ref[...], preferred_element_type=F32).astype(o_ref.dtype)

    return pl.pallas_call(
        body, name=name, grid=(M // tm, N // tn),
        in_specs=[pl.BlockSpec((tm, K), lambda i, j: (i, 0)), pl.BlockSpec((K, tn), lambda i, j: (0, j))],
        out_specs=pl.BlockSpec((tm, tn), lambda i, j: (i, j)),
        out_shape=jax.ShapeDtypeStruct((M, N), BF16), compiler_params=_params(),
    )(a, b)


def _matmul_acc(a, b, *, tn, tk, name):
    M, K = a.shape
    N = b.shape[1]
    nk = K // tk

    def body(a_ref, b_ref, o_ref, acc_ref):
        k = pl.program_id(1)

        @pl.when(k == 0)
        def _():
            acc_ref[...] = jnp.zeros_like(acc_ref)

        acc_ref[...] += jnp.dot(a_ref[...], b_ref[...], preferred_element_type=F32)

        @pl.when(k == nk - 1)
        def _():
            o_ref[...] = acc_ref[...].astype(o_ref.dtype)

    return pl.pallas_call(
        body, name=name, grid=(N // tn, nk),
        in_specs=[pl.BlockSpec((M, tk), lambda j, k: (0, k)), pl.BlockSpec((tk, tn), lambda j, k: (k, j))],
        out_specs=pl.BlockSpec((M, tn), lambda j, k: (0, j)),
        out_shape=jax.ShapeDtypeStruct((M, N), BF16),
        scratch_shapes=[pltpu.VMEM((M, tn), F32)], compiler_params=_params(),
    )(a, b)


SHARD_COLS = 2816


def _column_tables(part):
    if part == "nat":
        bw = 256
        orig = [j * bw if j < 16 else 8704 + (j - 16) * bw for j in range(NAT // bw)]
    else:
        bw = 128
        orig = []
        for jj in range(NQKV // bw):
            g, p, t = jj // 12, (jj % 12) // 3, jj % 3
            orig.append(4096 + t * 1536 + g * 512 + p * 128)
    tk = jnp.array([o // SHARD_COLS for o in orig], jnp.int32)
    tc = jnp.array([(o % SHARD_COLS) // bw for o in orig], jnp.int32)
    return tk, tc, bw


def _weights_prep(gath, own, chipv, part):
    D = D_MODEL
    tk, tc, bw = _column_tables(part)
    n = tk.shape[0]

    def body(tk_ref, tc_ref, chip_ref, g_ref, o_ref, w_ref, wt_ref):
        blk = jnp.where(tk_ref[pl.program_id(0)] == chip_ref[0], o_ref[...], g_ref[...])
        w_ref[...] = blk
        wt_ref[...] = blk.astype(F32).T.astype(BF16)

    def gath_idx(j, tk, tc, ch):
        k = tk[j]
        return (jnp.where(k == ch[0], (k + 1) % 4, k), 0, tc[j])

    return pl.pallas_call(
        body, name="weights_prep_" + part,
        grid_spec=pltpu.PrefetchScalarGridSpec(
            num_scalar_prefetch=3, grid=(n,),
            in_specs=[pl.BlockSpec((None, D, bw), gath_idx),
                      pl.BlockSpec((D, bw), lambda j, tk, tc, ch: (0, tc[j]))],
            out_specs=[pl.BlockSpec((D, bw), lambda j, tk, tc, ch: (0, j)),
                       pl.BlockSpec((bw, D), lambda j, tk, tc, ch: (j, 0))]),
        out_shape=[jax.ShapeDtypeStruct((D, n * bw), BF16), jax.ShapeDtypeStruct((n * bw, D), BF16)],
        compiler_params=_params(),
    )(tk, tc, chipv, gath, own)


def _grad_relayout(gw, into, part):
    D = D_MODEL
    tk, tc, bw = _column_tables(part)
    n = tk.shape[0]

    def body(tk_ref, tc_ref, g_ref, into_ref, o_ref):
        o_ref[0] = g_ref[0:D // 2, :]
        o_ref[1] = g_ref[D // 2:D, :]

    return pl.pallas_call(
        body, name="grad_relayout_" + part,
        grid_spec=pltpu.PrefetchScalarGridSpec(
            num_scalar_prefetch=2, grid=(n,),
            in_specs=[pl.BlockSpec((D, bw), lambda j, tk, tc: (0, j)), ANY],
            out_specs=pl.BlockSpec((2, None, D // 2, bw), lambda j, tk, tc: (0, tk[j], 0, tc[j]))),
        out_shape=jax.ShapeDtypeStruct(into.shape, BF16),
        input_output_aliases={3: 0},
        compiler_params=_params(),
    )(tk, tc, gw, into)


def _head_norm(xb, w, ebd):
    x = xb.astype(F32)
    ss = jnp.dot((x * x).astype(BF16), ebd, preferred_element_type=F32)
    return x * lax.rsqrt(ss * (1.0 / HEAD_DIM) + EPS) * w


def _window_mask(no_prev):
    qi = lax.broadcasted_iota(jnp.int32, (128, 256), 0)
    kc = lax.broadcasted_iota(jnp.int32, (128, 256), 1)
    ok = (kc >= qi) & (kc <= qi + 128)
    if no_prev is not None:
        ok = ok & ((kc >= 128) | jnp.logical_not(no_prev))
    return ok


def _lane_masks():
    lane = lax.broadcasted_iota(jnp.int32, (1, 128), 1)
    return [(lane < HEAD_DIM).astype(F32), (lane >= HEAD_DIM).astype(F32)]


def _head_col(v, mh):
    return jnp.sum(v * mh, axis=1, keepdims=True) * (1.0 / HEAD_DIM)


def _attn_fwd(projq, qnw2, knw2, ebd, g):
    S = projq.shape[0]
    d = DILATIONS[g]
    T = ATT_TILE
    nt = S // T
    nb = T // (128 * d)
    sdt = BF16 if d == 1 else F32

    def body(cur_ref, qw_ref, kw_ref, ebd_ref, o_ref, lse_ref, qs_s, ks_s, vs_s):
        t = pl.program_id(1)
        e = ebd_ref[...]

        @pl.when(t == 0)
        def _():
            ks_s[pl.ds(0, T), :] = jnp.zeros((T, 128), sdt)
            vs_s[pl.ds(0, T), :] = jnp.zeros((T, 128), sdt)

        @pl.when(t > 0)
        def _():
            ks_s[pl.ds(0, T), :] = ks_s[pl.ds(T, T), :]
            vs_s[pl.ds(0, T), :] = vs_s[pl.ds(T, T), :]

        qs_s[...] = (_head_norm(cur_ref[:, 0:128], qw_ref[...], e) * (QK_SCALE * LOG2E)).astype(sdt)
        ks_s[pl.ds(T, T), :] = _head_norm(cur_ref[:, 128:256], kw_ref[...], e).astype(sdt)
        vs_s[pl.ds(T, T), :] = cur_ref[:, 256:384].astype(sdt)
        m0, m1 = _lane_masks()
        first = t == 0
        for r in range(d):
            for bb in range(nb):
                q0 = r + bb * 128 * d
                k0 = T + r + (bb - 1) * 128 * d
                qs = qs_s[pl.ds(q0, 128, stride=d), :].astype(F32)
                kb = ks_s[pl.ds(k0, 256, stride=d), :].astype(BF16)
                vb = vs_s[pl.ds(k0, 256, stride=d), :].astype(BF16)
                ok = _window_mask(first if bb == 0 else None)
                ok2 = jnp.concatenate([ok, ok], axis=0)
                q2 = jnp.concatenate([qs * m0, qs * m1], axis=0).astype(BF16)
                s = lax.dot_general(q2, kb, (((1,), (1,)), ((), ())), preferred_element_type=F32)
                s = jnp.where(ok2, s, NEG_INF)
                mx = jnp.max(s, axis=-1, keepdims=True)
                p = jnp.exp2(s - mx)
                l = jnp.sum(p, axis=-1, keepdims=True)
                o = jnp.dot(p.astype(BF16), vb, preferred_element_type=F32) / l
                lse = mx * LN2 + jnp.log(l)
                o_ref[pl.ds(q0, 128, stride=d), :] = o[0:128] * m0 + o[128:256] * m1
                lse_ref[pl.ds(q0, 128, stride=d), :] = lse[0:128] * m0 + lse[128:256] * m1

    return pl.pallas_call(
        body, name=f"attn_fwd_g{g}", grid=(4, nt),
        in_specs=[pl.BlockSpec((T, 384), lambda p, t: (t, g * 4 + p)),
                  pl.BlockSpec((1, 128), lambda p, t: (0, 0)),
                  pl.BlockSpec((1, 128), lambda p, t: (0, 0)),
                  pl.BlockSpec((128, 128), lambda p, t: (0, 0))],
        out_specs=[pl.BlockSpec((T, 128), lambda p, t: (t, p)), pl.BlockSpec((T, 128), lambda p, t: (t, p))],
        out_shape=[jax.ShapeDtypeStruct((S, ATTN_OUT), F32), jax.ShapeDtypeStruct((S, ATTN_OUT), F32)],
        scratch_shapes=[pltpu.VMEM((T, 128), sdt), pltpu.VMEM((2 * T, 128), sdt), pltpu.VMEM((2 * T, 128), sdt)],
        compiler_params=_params(),
    )(projq, qnw2, knw2, ebd)


def _attn_bwd(projq, dprojq, d_o, lse, delta, qnw2, knw2, ebd, g):
    S = projq.shape[0]
    d = DILATIONS[g]
    T = ATT_TILE
    nt = S // T
    nb = T // (128 * d)
    sdt = F32

    def norm_bwd(raw_b, dy, w, e):
        x = raw_b.astype(F32)
        ss = jnp.dot((x * x).astype(BF16), e, preferred_element_type=F32)
        rstd = lax.rsqrt(ss * (1.0 / HEAD_DIM) + EPS)
        xh = x * rstd
        gw = jnp.sum(dy * xh, axis=0, keepdims=True)
        dxh = dy * w
        mean = jnp.dot((dxh * xh).astype(BF16), e, preferred_element_type=F32) * (1.0 / HEAD_DIM)
        return rstd * (dxh - xh * mean), gw

    nt_dims = (((1,), (1,)), ((), ()))
    tn_dims = (((0,), (0,)), ((), ()))

    def unit_stacked(qs, dos, ls, dls, kb, vb, ok, m0, m1):
        ok2 = jnp.concatenate([ok, ok], axis=0)
        q2 = jnp.concatenate([qs * m0, qs * m1], axis=0).astype(BF16)
        do2 = jnp.concatenate([dos * m0, dos * m1], axis=0).astype(BF16)
        lcol = jnp.concatenate([_head_col(ls, m0), _head_col(ls, m1)], axis=0)
        dcol = jnp.concatenate([_head_col(dls, m0), _head_col(dls, m1)], axis=0)
        s = jnp.where(ok2, lax.dot_general(q2, kb, nt_dims, preferred_element_type=F32), NEG_INF)
        p = jnp.exp2(s - lcol * LOG2E)
        dp = lax.dot_general(do2, vb, nt_dims, preferred_element_type=F32)
        ds = (p * (dp - dcol)).astype(BF16)
        dq2 = jnp.dot(ds, kb, preferred_element_type=F32)
        dk = lax.dot_general(ds, q2, tn_dims, preferred_element_type=F32)
        dv = lax.dot_general(p.astype(BF16), do2, tn_dims, preferred_element_type=F32)
        return dq2[0:128] * m0 + dq2[128:256] * m1, dk, dv

    def unit_per_head(qs, dos, ls, dls, kb, vb, ok, m0, m1):
        dq_t = dk_t = dv_t = None
        for mh in (m0, m1):
            qh = (qs * mh).astype(BF16)
            doh = (dos * mh).astype(BF16)
            s = jnp.where(ok, lax.dot_general(qh, kb, nt_dims, preferred_element_type=F32), NEG_INF)
            p = jnp.exp2(s - _head_col(ls, mh) * LOG2E)
            dp = lax.dot_general(doh, vb, nt_dims, preferred_element_type=F32)
            ds = (p * (dp - _head_col(dls, mh))).astype(BF16)
            dq = jnp.dot(ds, kb, preferred_element_type=F32) * mh
            dk = lax.dot_general(ds, qh, tn_dims, preferred_element_type=F32)
            dv = lax.dot_general(p.astype(BF16), doh, tn_dims, preferred_element_type=F32)
            dq_t = dq if dq_t is None else dq_t + dq
            dk_t = dk if dk_t is None else dk_t + dk
            dv_t = dv if dv_t is None else dv_t + dv
        return dq_t, dk_t, dv_t

    unit = unit_per_head if d == 1 else unit_stacked

    def body(cur_ref, prev_ref, do_ref, lse_ref, dl_ref, qw_ref, kw_ref, ebd_ref, dp_in_ref,
             out_ref, gqk_ref, qs_s, ks_s, vs_s, do_s, dq_cur, dq_prev, dk_acc, dv_acc):
        i = pl.program_id(1)
        e = ebd_ref[...]
        m0, m1 = _lane_masks()

        @pl.when(i == 0)
        def _():
            dk_acc[...] = jnp.zeros_like(dk_acc)
            dv_acc[...] = jnp.zeros_like(dv_acc)
            dq_prev[...] = jnp.zeros_like(dq_prev)
            gqk_ref[...] = jnp.zeros_like(gqk_ref)

        @pl.when(i < nt)
        def _():
            qs_s[...] = _head_norm(cur_ref[:, 0:128], qw_ref[...], e) * (QK_SCALE * LOG2E)
            ks_s[pl.ds(0, T), :] = _head_norm(prev_ref[:, 128:256], kw_ref[...], e)
            ks_s[pl.ds(T, T), :] = _head_norm(cur_ref[:, 128:256], kw_ref[...], e)
            vs_s[pl.ds(0, T), :] = prev_ref[:, 256:384].astype(F32)
            vs_s[pl.ds(T, T), :] = cur_ref[:, 256:384].astype(F32)
            do_s[...] = do_ref[...].astype(F32)
            first = i == 0
            for r in range(d):
                for bb in range(nb):
                    q0 = r + bb * 128 * d
                    k0 = T + r + (bb - 1) * 128 * d
                    qs = qs_s[pl.ds(q0, 128, stride=d), :]
                    dos = do_s[pl.ds(q0, 128, stride=d), :]
                    ls = lse_ref[pl.ds(q0, 128, stride=d), :]
                    dls = dl_ref[pl.ds(q0, 128, stride=d), :]
                    kb = ks_s[pl.ds(k0, 256, stride=d), :].astype(BF16)
                    vb = vs_s[pl.ds(k0, 256, stride=d), :].astype(BF16)
                    ok = _window_mask(first if bb == 0 else None)
                    dq_t, dk_t, dv_t = unit(qs, dos, ls, dls, kb, vb, ok, m0, m1)
                    dq_cur[pl.ds(q0, 128, stride=d), :] = dq_t
                    dk_acc[pl.ds(k0, 256, stride=d), :] = dk_acc[pl.ds(k0, 256, stride=d), :] + dk_t
                    dv_acc[pl.ds(k0, 256, stride=d), :] = dv_acc[pl.ds(k0, 256, stride=d), :] + dv_t

        @pl.when(i >= 1)
        def _():
            dq_raw, gq = norm_bwd(prev_ref[:, 0:128], dq_prev[...] * QK_SCALE, qw_ref[...], e)
            dk_raw, gk = norm_bwd(prev_ref[:, 128:256], dk_acc[pl.ds(0, T), :] * LN2, kw_ref[...], e)
            out_ref[:, 0:128] = dq_raw.astype(BF16)
            out_ref[:, 128:256] = dk_raw.astype(BF16)
            out_ref[:, 256:384] = dv_acc[pl.ds(0, T), :].astype(BF16)
            gqk_ref[0:1, :] += gq
            gqk_ref[1:2, :] += gk

        dq_prev[...] = dq_cur[...]
        dk_acc[pl.ds(0, T), :] = dk_acc[pl.ds(T, T), :]
        dv_acc[pl.ds(0, T), :] = dv_acc[pl.ds(T, T), :]
        dk_acc[pl.ds(T, T), :] = jnp.zeros((T, 128), F32)
        dv_acc[pl.ds(T, T), :] = jnp.zeros((T, 128), F32)

    last = nt - 1
    qtile = lambda p, i: (jnp.minimum(i, last), p)
    return pl.pallas_call(
        body, name=f"attn_bwd_g{g}", grid=(4, nt + 1),
        in_specs=[pl.BlockSpec((T, 384), lambda p, i: (jnp.minimum(i, last), g * 4 + p)),
                  pl.BlockSpec((T, 384), lambda p, i: (jnp.maximum(i - 1, 0), g * 4 + p)),
                  pl.BlockSpec((T, 128), qtile), pl.BlockSpec((T, 128), qtile), pl.BlockSpec((T, 128), qtile),
                  pl.BlockSpec((1, 128), lambda p, i: (0, 0)),
                  pl.BlockSpec((1, 128), lambda p, i: (0, 0)),
                  pl.BlockSpec((128, 128), lambda p, i: (0, 0)),
                  ANY],
        out_specs=[pl.BlockSpec((T, 384), lambda p, i: (jnp.maximum(i - 1, 0), g * 4 + p)),
                   pl.BlockSpec((None, 8, 128), lambda p, i: (p, 0, 0))],
        out_shape=[jax.ShapeDtypeStruct(dprojq.shape, BF16), jax.ShapeDtypeStruct((4, 8, 128), F32)],
        scratch_shapes=[pltpu.VMEM((T, 128), sdt), pltpu.VMEM((2 * T, 128), sdt), pltpu.VMEM((2 * T, 128), sdt),
                        pltpu.VMEM((T, 128), sdt), pltpu.VMEM((T, 128), F32), pltpu.VMEM((T, 128), F32),
                        pltpu.VMEM((2 * T, 128), F32), pltpu.VMEM((2 * T, 128), F32)],
        input_output_aliases={8: 0},
        compiler_params=_params(),
    )(projq, projq, d_o, lse, delta, qnw2, knw2, ebd, dprojq)


def _mid(projn, o_g, lse_g, x, tgt, mod, convw8, wc, wct, wa, wo, wot, ebd, *, tm=256):
    S, D = x.shape
    nt = S // tm
    nst = max(1, 256 // tm)
    assert nt % nst == 0
    hb = tm // 16

    def body(pn_ref, hc_ref, hx_ref, o0_ref, o1_ref, o2_ref, l0_ref, l1_ref, l2_ref, x_ref, t_ref, mod_ref,
             cw_ref, ebd_ref, wc_hbm, wct_hbm, wa_hbm, wo_hbm, wot_hbm,
             dpn_ref, do_ref, lse_ref, dl_ref, dout_ref, vacc_ref, gwo_hbm, gwc_hbm, gwa_hbm,
             wc_s, wct_s, wa_s, wo_s, wot_s, gwo_s, gwc_s, gwa_s, carry_s, *stashes):
        i = pl.program_id(0)
        ti = nt - 1 - i
        par = i % nst
        stash = pl.ds(pl.multiple_of(par * tm, tm), tm)

        @pl.when(i == 0)
        def _():
            for src, dst in ((wc_hbm, wc_s), (wct_hbm, wct_s), (wa_hbm, wa_s), (wo_hbm, wo_s), (wot_hbm, wot_s)):
                pltpu.sync_copy(src, dst)
            gwo_s[...] = jnp.zeros_like(gwo_s)
            gwc_s[...] = jnp.zeros_like(gwc_s)
            gwa_s[...] = jnp.zeros_like(gwa_s)
            carry_s[...] = jnp.zeros_like(carry_s)
            vacc_ref[...] = jnp.zeros_like(vacc_ref)

        rows = lax.broadcasted_iota(jnp.int32, (tm, D), 0)
        w0, w1, w2 = cw_ref[0:1, :], cw_ref[1:2, :], cw_ref[2:3, :]
        gate = mod_ref[:, 2 * D:3 * D]

        b_a = pn_ref[:, BA:BA + D].astype(F32)
        c_a = pn_ref[:, CA:CA + D].astype(F32)
        x_a = pn_ref[:, XA:XA + D].astype(F32)
        z_a = pn_ref[:, ZA:ZA + D].astype(F32)
        u = c_a * x_a
        has_prev = (ti > 0).astype(F32)
        uh = hc_ref[...].astype(F32) * hx_ref[...].astype(F32) * has_prev
        h14, h15 = _row(uh, 14), _row(uh, 15)
        u_m1 = jnp.where(rows == 0, h15, pltpu.roll(u, 1, 0))
        u_m2 = jnp.where(rows == 0, h14, jnp.where(rows == 1, h15, pltpu.roll(u, 2, 0)))
        conv = w0 * u_m2 + w1 * u_m1 + w2 * u
        sg_a = _sigmoid(z_a)
        sz_a = z_a * sg_a
        y_a = b_a * conv * sz_a
        y_ab = y_a.astype(BF16)
        pa = jnp.dot(y_ab, wc_s[...], preferred_element_type=F32)

        l0, l1, l2 = l0_ref[...], l1_ref[...], l2_ref[...]
        mxl = jnp.maximum(jnp.maximum(l0, l1), l2)
        e0, e1, e2 = jnp.exp(l0 - mxl), jnp.exp(l1 - mxl), jnp.exp(l2 - mxl)
        den = e0 + e1 + e2
        attn = (e0 * o0_ref[...] + e1 * o1_ref[...] + e2 * o2_ref[...]) / den
        lse_ref[...] = mxl + jnp.log(den)
        z_b = pn_ref[:, ZB:ZB + ATTN_OUT].astype(F32)
        sg_b = _sigmoid(z_b)
        sz_b = z_b * sg_b
        y_b = attn * sz_b
        y_bb = y_b.astype(BF16)
        pb = jnp.dot(y_bb, wa_s[...], preferred_element_type=F32)

        s_a = _sigmoid(pn_ref[:, GA:GA + D].astype(F32))
        s_b = _sigmoid(pn_ref[:, GB:GB + D].astype(F32))
        merged = s_a * pa + s_b * pb
        merged_b = merged.astype(BF16)
        mo = jnp.dot(merged_b, wo_s[...], preferred_element_type=F32)
        diff = x_ref[...] + gate * mo - t_ref[...]
        dout = diff * (1.0 / D)
        dout_ref[...] = dout
        vacc_ref[4:5, :] += jnp.sum(diff * diff, axis=0, keepdims=True)
        vacc_ref[0:1, :] += jnp.sum(dout * mo, axis=0, keepdims=True)

        d_mo = (dout * gate).astype(BF16)
        dmerged = jnp.dot(d_mo, wot_s[...], preferred_element_type=F32)
        dpa = (dmerged * s_a).astype(BF16)
        dpb = (dmerged * s_b).astype(BF16)
        dpn_ref[:, GA:GA + D] = (dmerged * pa * s_a * (1.0 - s_a)).astype(BF16)
        dpn_ref[:, GB:GB + D] = (dmerged * pb * s_b * (1.0 - s_b)).astype(BF16)
        tn = (((0,), (0,)), ((), ()))
        if nst == 1:
            gwo_s[...] += lax.dot_general(merged_b, d_mo, tn, preferred_element_type=F32)
            gwc_s[...] += lax.dot_general(y_ab, dpa, tn, preferred_element_type=F32)
            gwa_s[...] += lax.dot_general(y_bb, dpb, tn, preferred_element_type=F32)
        else:
            st_m, st_d, st_ya, st_pa, st_yb, st_pb = stashes
            st_m[stash, :] = merged_b
            st_d[stash, :] = d_mo
            st_ya[stash, :] = y_ab
            st_pa[stash, :] = dpa
            st_yb[stash, :] = y_bb
            st_pb[stash, :] = dpb

            @pl.when(par == nst - 1)
            def _():
                gwo_s[...] += lax.dot_general(st_m[...], st_d[...], tn, preferred_element_type=F32)
                gwc_s[...] += lax.dot_general(st_ya[...], st_pa[...], tn, preferred_element_type=F32)
                gwa_s[...] += lax.dot_general(st_yb[...], st_pb[...], tn, preferred_element_type=F32)

        dy_a = jnp.dot(dpa, wct_s[...], preferred_element_type=F32)
        dy_b = lax.dot_general(dpb, wa_s[...], (((1,), (1,)), ((), ())), preferred_element_type=F32)

        dattn = dy_b * sz_b
        dpn_ref[:, ZB:ZB + ATTN_OUT] = (dy_b * attn * sg_b * (1.0 + z_b * (1.0 - sg_b))).astype(BF16)
        do_ref[...] = dattn.astype(BF16)
        dl_ref[...] = jnp.dot((dattn * attn).astype(BF16), ebd_ref[...], preferred_element_type=F32)

        dpn_ref[:, BA:BA + D] = (dy_a * conv * sz_a).astype(BF16)
        dpn_ref[:, ZA:ZA + D] = (dy_a * b_a * conv * sg_a * (1.0 + z_a * (1.0 - sg_a))).astype(BF16)
        dconv = dy_a * b_a * sz_a
        vacc_ref[1:2, :] += jnp.sum(dconv * u_m2, axis=0, keepdims=True)
        vacc_ref[2:3, :] += jnp.sum(dconv * u_m1, axis=0, keepdims=True)
        vacc_ref[3:4, :] += jnp.sum(dconv * u, axis=0, keepdims=True)
        nxt = carry_s[...]
        n0, n1 = _row(nxt, 0), _row(nxt, 1)
        dc_p1 = jnp.where(rows == tm - 1, n0, pltpu.roll(dconv, tm - 1, 0))
        dc_p2 = jnp.where(rows == tm - 1, n1, jnp.where(rows == tm - 2, n0, pltpu.roll(dconv, tm - 2, 0)))
        du = w2 * dconv + w1 * dc_p1 + w0 * dc_p2
        carry_s[...] = dconv[0:8, :]
        dpn_ref[:, CA:CA + D] = (du * x_a).astype(BF16)
        dpn_ref[:, XA:XA + D] = (du * c_a).astype(BF16)

        @pl.when(i == nt - 1)
        def _():
            pltpu.sync_copy(gwo_s, gwo_hbm)
            pltpu.sync_copy(gwc_s, gwc_hbm)
            pltpu.sync_copy(gwa_s, gwa_hbm)

    rev = lambda i: (nt - 1 - i, 0)
    halo = lambda col: pl.BlockSpec((16, D), lambda i: (jnp.maximum((nt - 1 - i) * hb - 1, 0), col))
    tile512 = pl.BlockSpec((tm, ATTN_OUT), rev)
    tileD = pl.BlockSpec((tm, D), rev)
    const = lambda shape: pl.BlockSpec(shape, lambda i: (0, 0))
    return pl.pallas_call(
        body, name="mid_fwd_bwd", grid=(nt,),
        in_specs=[pl.BlockSpec((tm, NAT), rev), halo(CA // D), halo(XA // D)] + [tile512] * 6
                 + [tileD, tileD, const((1, 3 * D)), const((8, D)), const((ATTN_OUT, ATTN_OUT))] + [ANY] * 5,
        out_specs=[pl.BlockSpec((tm, NAT), rev), tile512, tile512, tile512, tileD, const((8, D)), ANY, ANY, ANY],
        out_shape=[jax.ShapeDtypeStruct((S, NAT), BF16), jax.ShapeDtypeStruct((S, ATTN_OUT), BF16),
                   jax.ShapeDtypeStruct((S, ATTN_OUT), F32), jax.ShapeDtypeStruct((S, ATTN_OUT), F32),
                   jax.ShapeDtypeStruct((S, D), F32), jax.ShapeDtypeStruct((8, D), F32),
                   jax.ShapeDtypeStruct((D, D), F32), jax.ShapeDtypeStruct((D, D), F32),
                   jax.ShapeDtypeStruct((ATTN_OUT, D), F32)],
        scratch_shapes=[pltpu.VMEM((D, D), BF16), pltpu.VMEM((D, D), BF16), pltpu.VMEM((ATTN_OUT, D), BF16),
                        pltpu.VMEM((D, D), BF16), pltpu.VMEM((D, D), BF16),
                        pltpu.VMEM((D, D), F32), pltpu.VMEM((D, D), F32), pltpu.VMEM((ATTN_OUT, D), F32),
                        pltpu.VMEM((8, D), F32)]
                       + ([pltpu.VMEM((nst * tm, D), BF16)] * 4
                          + [pltpu.VMEM((nst * tm, ATTN_OUT), BF16), pltpu.VMEM((nst * tm, D), BF16)] if nst > 1 else []),
        compiler_params=_params(vmem=V7X_VMEM_BYTES - 64 * 1024),
    )(projn, projn, projn, *o_g, *lse_g, x, tgt, mod, convw8, ebd, wc, wct, wa, wo, wot)


def _dh_norm(dpn, dpq, wnt, wqt, x, dout, mod, norm_w, parts, *, tm=256):
    S, D = x.shape
    n = len(parts)
    nt = S // tm

    def body(dn_ref, dq_ref, x_ref, dout_ref, mod_ref, nw_ref, wn_hbm, wq_hbm, *rest):
        p_hbm, rest = rest[:n], rest[n:]
        gx_ref, st_ref = rest[0], rest[1]
        q_hbm = rest[2:2 + n]
        wn_s, wq_s = rest[2 + n:4 + n]
        sems = rest[4 + n:]

        @pl.when(pl.program_id(0) == 0)
        def _():
            if n:
                mine, _ = _scatter_copies(p_hbm, q_hbm, *sems)
                for cp in mine:
                    cp.start()
            pltpu.sync_copy(wn_hbm, wn_s)
            pltpu.sync_copy(wq_hbm, wq_s)
            st_ref[...] = jnp.zeros_like(st_ref)

        if n:
            @pl.when(pl.program_id(0) == nt - 1)
            def _():
                mine, theirs = _scatter_copies(p_hbm, q_hbm, *sems)
                for cp in theirs:
                    cp.wait_recv()
                for cp in mine:
                    cp.wait_send()

        dh = (jnp.dot(dn_ref[...], wn_s[...], preferred_element_type=F32)
              + jnp.dot(dq_ref[...], wq_s[...], preferred_element_type=F32))
        scale1 = 1.0 + mod_ref[:, D:2 * D]
        nw = nw_ref[...]
        xv = x_ref[...]
        rstd = lax.rsqrt(jnp.mean(xv * xv, axis=-1, keepdims=True) + EPS)
        xh = xv * rstd
        dhs = dh * scale1
        dxh = dhs * nw
        dx = rstd * (dxh - xh * jnp.mean(dxh * xh, axis=-1, keepdims=True))
        gx_ref[...] = dout_ref[...] + dx
        st_ref[0:1, :] += jnp.sum(dh, axis=0, keepdims=True)
        st_ref[1:2, :] += jnp.sum(dh * (xh * nw), axis=0, keepdims=True)
        st_ref[2:3, :] += jnp.sum(dhs * xh, axis=0, keepdims=True)

    row = lambda w: pl.BlockSpec((tm, w), lambda i: (i, 0))
    outs = pl.pallas_call(
        body, name="dh_norm_bwd", grid=(nt,),
        in_specs=[row(NAT), row(NQKV), row(D), row(D),
                  pl.BlockSpec((1, 3 * D), lambda i: (0, 0)), pl.BlockSpec((1, D), lambda i: (0, 0)), ANY, ANY]
                 + [ANY] * n,
        out_specs=[row(D), pl.BlockSpec((8, D), lambda i: (0, 0))] + [ANY] * n,
        out_shape=[jax.ShapeDtypeStruct((S, D), F32), jax.ShapeDtypeStruct((8, D), F32)]
                  + [jax.ShapeDtypeStruct(p.shape, p.dtype) for p in parts],
        scratch_shapes=[pltpu.VMEM((NAT, D), BF16), pltpu.VMEM((NQKV, D), BF16)]
                       + ([pltpu.SemaphoreType.DMA((n, 3)), pltpu.SemaphoreType.DMA((n, 3))] if n else []),
        compiler_params=_params(),
    )(dpn, dpq, x, dout, mod, norm_w, wnt, wqt, *parts)
    return outs[0], outs[1], outs[2:]


def _pack_smalls(st, vacc, gqk):
    D = D_MODEL

    def body(st_ref, va_ref, gqk_ref, o_ref):
        o_ref[...] = jnp.zeros_like(o_ref)
        o_ref[0:2, :] = st_ref[0:2, :]
        o_ref[2:3, :] = va_ref[0:1, :]
        o_ref[8:9, :] = st_ref[2:3, :]
        o_ref[16:19, :] = va_ref[1:4, :]
        tot = gqk_ref[0]
        for k in range(1, gqk_ref.shape[0]):
            tot = tot + gqk_ref[k]
        tot = tot + pltpu.roll(tot, HEAD_DIM, 1)
        o_ref[24:32, 0:128] = tot
        loss = jnp.sum(va_ref[4:5, :], axis=1, keepdims=True) * (0.5 / D)
        o_ref[26:27, :] = jnp.broadcast_to(loss, (1, D))

    return pl.pallas_call(
        body, name="pack_smalls", out_shape=jax.ShapeDtypeStruct((32, D), F32),
        in_specs=[pl.BlockSpec(memory_space=pltpu.VMEM)] * 3,
        out_specs=pl.BlockSpec(memory_space=pltpu.VMEM), compiler_params=_params(),
    )(st, vacc, gqk)


def _small_update(sm_loc, dm_pad, ct_pad, w_ada, m_ada, v_ada, vec_w, vec_m, vec_v):
    D = D_MODEL
    cols = w_ada.shape[1]

    def body(sm_ref, dm_ref, ct_ref, w_ref, m_ref, v_ref, vw_ref, vm_ref, vv_ref,
             gw_ref, dw_ref, mw_ref, vw_o_ref, gv_ref, dv_ref, mv_ref, vv_o_ref):
        g = sm_ref[0]
        for k in range(1, 8):
            g = g + sm_ref[k]
        gv_ref[...] = g
        dv, mv, vv = _adamw_math(vw_ref[...], g, vm_ref[...], vv_ref[...])
        dv_ref[...] = dv
        mv_ref[...] = mv
        vv_o_ref[...] = vv
        gw = jnp.dot(ct_ref[...], dm_ref[...], preferred_element_type=F32, precision=lax.Precision.HIGHEST)
        gw_ref[...] = gw
        dw, mw, vw = _adamw_math(w_ref[...], gw, m_ref[...], v_ref[...])
        dw_ref[...] = dw
        mw_ref[...] = mw
        vw_o_ref[...] = vw

    sw = jax.ShapeDtypeStruct((D, cols), F32)
    sv = jax.ShapeDtypeStruct((32, D), F32)
    return pl.pallas_call(
        body, name="small_update", out_shape=[sw, sw, sw, sw, sv, sv, sv, sv],
        in_specs=[pl.BlockSpec(memory_space=pltpu.VMEM)] * 9,
        out_specs=[pl.BlockSpec(memory_space=pltpu.VMEM)] * 8, compiler_params=_params(),
    )(sm_loc, dm_pad, ct_pad, w_ada, m_ada, v_ada, vec_w, vec_m, vec_v)


def _local_step(xs, tg, mod, norm_w, w_nat, w_qkv, wc, wa, wo, convw8, q_norm_w, k_norm_w):
    S = xs.shape[0]
    lane = jnp.arange(128)
    ebd128 = (lane[:, None] // HEAD_DIM == lane[None, :] // HEAD_DIM).astype(BF16)
    lane5 = jnp.arange(ATTN_OUT)
    ebd512 = (lane5[:, None] // HEAD_DIM == lane5[None, :] // HEAD_DIM).astype(BF16)
    qnw2 = jnp.tile(q_norm_w, (1, 2))
    knw2 = jnp.tile(k_norm_w, (1, 2))

    projn, h, ht = _inproj_nat(xs, mod, norm_w, w_nat)
    projq = _matmul(h, w_qkv, tm=1024, tn=1536, name="inproj_qkv")
    o_g, lse_g = [], []
    for g in range(N_GROUPS):
        o, l = _attn_fwd(projq, qnw2, knw2, ebd128, g)
        o_g.append(o)
        lse_g.append(l)

    dpn, d_o, lse, delta, dout, vacc, gwo, gwc, gwa = _mid(
        projn, o_g, lse_g, xs, tg, mod, convw8, wc, wc.T, wa, wo, wo.T, ebd512)
    dpq = lax.empty((S, NQKV), BF16)
    gqk = []
    for g in range(N_GROUPS):
        dpq, part = _attn_bwd(projq, dpq, d_o, lse, delta, qnw2, knw2, ebd128, g)
        gqk.append(part)
    gw_nat = _matmul_acc(ht, dpn, tn=1664, tk=2048, name="grad_w_in_nat")
    gw_qkv = _matmul_acc(ht, dpq, tn=1536, tk=2048, name="grad_w_in_qkv")
    return dpn, dpq, dout, gw_nat, gw_qkv, gwc, gwa, gwo, vacc, gqk


def kernel(x, c, w_ada, b_ada, norm_w, w_in, conv_w, q_norm_w, k_norm_w, w_br_conv, w_br_attn, w_out, loss_target, m_w_ada, m_b_ada, m_norm_w, m_w_in, m_conv_w, m_q_norm_w, m_k_norm_w, m_w_br_conv, m_w_br_attn, m_w_out, v_w_ada, v_b_ada, v_norm_w, v_w_in, v_conv_w, v_q_norm_w, v_k_norm_w, v_w_br_conv, v_w_br_attn, v_w_out):
    D = D_MODEL
    S = x.shape[1]
    xs, tg = x[0], loss_target[0]
    mx, my, mc = lax.axis_index("x"), lax.axis_index("y"), lax.axis_index("c")
    chip = 2 * mx + my
    dev = 2 * chip + mc

    c_all = _allgather8(jnp.pad(c, ((0, 7), (0, 0))), "allgather_c")[:, 0, :]
    b_shard = lax.dynamic_slice(b_ada, (0, chip * 768), (1, 768))
    mod_part, silu_c = _mod_part(c_all, w_ada[0], b_shard)
    mod_parts = _allgather_chips(mod_part, "allgather_mod")
    mod_all = mod_parts.transpose(1, 0, 2).reshape(8, 3 * D)
    mod = lax.dynamic_slice(mod_all, (dev, 0), (1, 3 * D))

    shards = [w_in[0].astype(BF16).reshape(2, 512, 2816), w_br_conv[0].astype(BF16).reshape(2, 128, D),
              w_br_attn[0].astype(BF16).reshape(2, 256, 256), w_out[0].astype(BF16).reshape(2, 128, D)]
    gathered_w = _weights_allgather(shards)
    g_bc, g_ba, g_wo = [lax.dynamic_update_slice(g, s[None], (chip, 0, 0, 0))
                        for g, s in zip(gathered_w[1:], shards[1:])]
    chipv = jnp.reshape(chip, (1,)).astype(jnp.int32)
    g_in = gathered_w[0].reshape(4, D, SHARD_COLS)
    own_in = shards[0].reshape(D, SHARD_COLS)
    w_nat, w_nat_t = _weights_prep(g_in, own_in, chipv, "nat")
    w_qkv, w_qkv_t = _weights_prep(g_in, own_in, chipv, "qkv")
    wc = g_bc.reshape(D, D)
    wa = g_ba.reshape(4, ATTN_OUT, 256).transpose(1, 0, 2).reshape(ATTN_OUT, D)
    wo = g_wo.reshape(D, D)
    conv_full = _allgather_chips(jnp.pad(conv_w[0], ((0, 5), (0, 0))), "allgather_conv")
    convw8 = conv_full.transpose(1, 0, 2).reshape(8, D)

    dpn, dpq, dout, gw_nat, gw_qkv, gwc, gwa, gwo, vacc, gqk = _local_step(
        xs, tg, mod, norm_w, w_nat, w_qkv, wc, wa, wo, convw8, q_norm_w, k_norm_w)
    gw_in = _grad_relayout(gw_nat, lax.empty((2, 4, D // 2, SHARD_COLS), BF16), "nat")
    gw_in = _grad_relayout(gw_qkv, gw_in, "qkv")

    def halves(a):
        k, r, cc = a.shape
        return a.reshape(k, 2, r // 2, cc).transpose(1, 0, 2, 3)

    grads = [gw_in,
             halves(gwc.astype(BF16).reshape(4, 256, D)),
             halves(gwa.astype(BF16).reshape(ATTN_OUT, 4, 256).transpose(1, 0, 2)),
             halves(gwo.astype(BF16).reshape(4, 256, D))]
    recv = _pair_exchange(grads)
    core = jnp.reshape(mc, (1,)).astype(jnp.int32)
    parts = [_pair_add(core, gr, rc) for gr, rc in zip(grads, recv)]
    grad_x, st, gathered = _dh_norm(dpn, dpq, w_nat_t, w_qkv_t, xs, dout, mod, norm_w, parts)
    mine = [_sum4(chipv, p, q) for p, q in zip(parts, gathered)]
    theirs = _half_exchange(mine)

    big = {}
    for t, (nm, w, m, v) in enumerate((("w_in", w_in, m_w_in, v_w_in), ("w_br_conv", w_br_conv, m_w_br_conv, v_w_br_conv),
                                       ("w_br_attn", w_br_attn, m_w_br_attn, v_w_br_attn), ("w_out", w_out, m_w_out, v_w_out))):
        big[nm] = _adamw_halves(core, w[0], mine[t], theirs[t], m[0], v[0], "adamw_" + nm)

    sm_all = _allgather8(_pack_smalls(st, vacc, jnp.concatenate(gqk, axis=0)), "allgather_smalls")
    dmod_all = sm_all[:, 0:3, :].reshape(8, 3 * D)
    dm_pad = jnp.pad(lax.dynamic_slice(dmod_all, (0, chip * 768), (8, 768)), ((0, 120), (0, 0)))
    conv_dev = lax.dynamic_slice(sm_all[:, 16:24, :], (0, 0, chip * 256), (8, 8, 256))
    sm_loc = jnp.concatenate([sm_all[:, 0:16, :], jnp.pad(conv_dev, ((0, 0), (0, 0), (0, D - 256))), sm_all[:, 24:32, :]], axis=1)

    def pack_vec(b3, nw, cw, qw, kw):
        z = jnp.zeros((32, D), F32)
        z = z.at[0:3, :].set(b3.reshape(3, D)).at[8:9, :].set(nw).at[16:19, 0:256].set(cw)
        return z.at[24:25, 0:HEAD_DIM].set(qw).at[25:26, 0:HEAD_DIM].set(kw)

    vec_w = pack_vec(b_ada, norm_w, conv_w[0], q_norm_w, k_norm_w)
    vec_m = pack_vec(m_b_ada, m_norm_w, m_conv_w[0], m_q_norm_w, m_k_norm_w)
    vec_v = pack_vec(v_b_ada, v_norm_w, v_conv_w[0], v_q_norm_w, v_k_norm_w)
    ct_pad = jnp.pad(silu_c.T, ((0, 0), (0, 120)))
    outs = _small_update(sm_loc, dm_pad, ct_pad, w_ada[0], m_w_ada[0], v_w_ada[0], vec_w, vec_m, vec_v)
    ada = outs[0:4]
    vec = outs[4:8]
    loss = vec[0][26, 0]

    def unpack(t):
        return {"b_ada": t[0:3, :].reshape(1, 3 * D), "norm_w": t[8:9, :], "conv_w": t[16:19, 0:256][None],
                "q_norm_w": t[24:25, 0:HEAD_DIM], "k_norm_w": t[25:26, 0:HEAD_DIM]}

    small = [unpack(t) for t in vec]
    order = ["w_ada", "b_ada", "norm_w", "w_in", "conv_w", "q_norm_w", "k_norm_w", "w_br_conv", "w_br_attn", "w_out"]
    res = [loss, grad_x[None]]
    for kind in range(4):
        for nm in order:
            if nm == "w_ada":
                res.append(ada[kind][None])
            elif nm in big:
                res.append(big[nm][kind][None])
            else:
                res.append(small[kind][nm])
    return tuple(res)
```

```python
import functools

import jax
import jax.numpy as jnp
from jax import lax
from jax.experimental import pallas as pl
from jax.experimental.pallas import tpu as pltpu

F32 = jnp.float32
BF16 = jnp.bfloat16
MESH = pl.DeviceIdType.MESH
ANY = pl.BlockSpec(memory_space=pl.ANY)

D_MODEL = 1024
HEAD_DIM = 64
N_GROUPS = 3
DILATIONS = (1, 4, 16)
ATTN_OUT = 512
EPS = 1e-6
NEG_INF = -1e30
NAT = 6656
NQKV = 4608
BA, CA, XA, ZA, ZB, GA, GB = 0, 1024, 2048, 3072, 4096, 4608, 5632
ATT_TILE = 2048
QK_SCALE = HEAD_DIM ** -0.5
LOG2E = 1.4426950408889634
LN2 = 0.6931471805599453
V7X_VMEM_BYTES = 64 * 1024 * 1024
V7X_VMEM_LIMIT = 56 * 1024 * 1024

ADAM_LR = 0.001
ADAM_B1 = 0.9
ADAM_B2 = 0.999
ADAM_EPS = 1e-08
ADAM_WD = 0.01
ADAM_STEP = 10


def _params(vmem=V7X_VMEM_LIMIT, **kw):
    return pltpu.CompilerParams(vmem_limit_bytes=vmem, **kw)


def _sigmoid(z):
    return 1.0 / (1.0 + jnp.exp(-z))


def _row(v, r):
    rows = lax.broadcasted_iota(jnp.int32, v.shape, 0)
    return jnp.sum(jnp.where(rows == r, v, 0.0), axis=0, keepdims=True)


def _coords():
    return lax.axis_index("x"), lax.axis_index("y"), lax.axis_index("c")


def _flip(v, bit):
    return 1 - v if bit else v


def _allgather8(blk, name):
    r, c = blk.shape

    def body(x_ref, o_ref, ssem, rsem, lsem):
        mx, my, mc = _coords()
        me = 4 * mx + 2 * my + mc
        local = pltpu.make_async_copy(x_ref, o_ref.at[me], lsem)
        local.start()
        sends = []
        for j in range(1, 8):
            peer = (_flip(mx, j & 4), _flip(my, j & 2), _flip(mc, j & 1))
            cp = pltpu.make_async_remote_copy(
                src_ref=x_ref, dst_ref=o_ref.at[me], send_sem=ssem.at[j - 1], recv_sem=rsem.at[j - 1],
                device_id=peer, device_id_type=MESH)
            cp.start()
            sends.append(cp)
        for j in range(1, 8):
            px, py, pc = _flip(mx, j & 4), _flip(my, j & 2), _flip(mc, j & 1)
            pltpu.make_async_remote_copy(
                src_ref=x_ref, dst_ref=o_ref.at[4 * px + 2 * py + pc], send_sem=ssem.at[j - 1],
                recv_sem=rsem.at[j - 1], device_id=(px, py, pc), device_id_type=MESH).wait_recv()
        for cp in sends:
            cp.wait_send()
        local.wait()

    return pl.pallas_call(
        body, name=name,
        out_shape=jax.ShapeDtypeStruct((8, r, c), blk.dtype),
        in_specs=[pl.BlockSpec(memory_space=pltpu.VMEM)],
        out_specs=pl.BlockSpec(memory_space=pltpu.VMEM),
        scratch_shapes=[pltpu.SemaphoreType.DMA((7,)), pltpu.SemaphoreType.DMA((7,)), pltpu.SemaphoreType.DMA(())],
    )(blk)


def _allgather_chips(blk, name):
    r, c = blk.shape

    def body(x_ref, o_ref, ssem, rsem, lsem):
        mx, my, mc = _coords()
        me = 2 * mx + my
        local = pltpu.make_async_copy(x_ref, o_ref.at[me], lsem)
        local.start()
        sends = []
        for j in range(1, 4):
            peer = (_flip(mx, j & 2), _flip(my, j & 1), mc)
            cp = pltpu.make_async_remote_copy(
                src_ref=x_ref, dst_ref=o_ref.at[me], send_sem=ssem.at[j - 1], recv_sem=rsem.at[j - 1],
                device_id=peer, device_id_type=MESH)
            cp.start()
            sends.append(cp)
        for j in range(1, 4):
            px, py = _flip(mx, j & 2), _flip(my, j & 1)
            pltpu.make_async_remote_copy(
                src_ref=x_ref, dst_ref=o_ref.at[2 * px + py], send_sem=ssem.at[j - 1],
                recv_sem=rsem.at[j - 1], device_id=(px, py, mc), device_id_type=MESH).wait_recv()
        for cp in sends:
            cp.wait_send()
        local.wait()

    return pl.pallas_call(
        body, name=name,
        out_shape=jax.ShapeDtypeStruct((4, r, c), blk.dtype),
        in_specs=[pl.BlockSpec(memory_space=pltpu.VMEM)],
        out_specs=pl.BlockSpec(memory_space=pltpu.VMEM),
        scratch_shapes=[pltpu.SemaphoreType.DMA((3,)), pltpu.SemaphoreType.DMA((3,)), pltpu.SemaphoreType.DMA(())],
    )(blk)


def _weights_allgather(shards):
    n = len(shards)

    def body(*refs):
        ins, outs = refs[:n], refs[n:2 * n]
        ssem, rsem = refs[2 * n:]
        mx, my, mc = _coords()
        me = 2 * mx + my
        sib = (mx, my, 1 - mc)
        chips = [(_flip(mx, j & 2), _flip(my, j & 1)) for j in range(1, 4)]
        sends = []
        for t in range(n):
            for j, (px, py) in enumerate(chips):
                cp = pltpu.make_async_remote_copy(
                    src_ref=ins[t].at[mc], dst_ref=outs[t].at[me, mc], send_sem=ssem.at[t, j],
                    recv_sem=rsem.at[t, j], device_id=(px, py, mc), device_id_type=MESH)
                cp.start()
                sends.append(cp)
        for t in range(n):
            for j, (px, py) in enumerate(chips):
                src = 2 * px + py
                pltpu.make_async_remote_copy(
                    src_ref=ins[t].at[mc], dst_ref=outs[t].at[src, mc], send_sem=ssem.at[t, j],
                    recv_sem=rsem.at[t, j], device_id=(px, py, mc), device_id_type=MESH).wait_recv()
                fwd = pltpu.make_async_remote_copy(
                    src_ref=outs[t].at[src, mc], dst_ref=outs[t].at[src, mc], send_sem=ssem.at[t, 3 + j],
                    recv_sem=rsem.at[t, 3 + j], device_id=sib, device_id_type=MESH)
                fwd.start()
                sends.append(fwd)
        for t in range(n):
            for j, (px, py) in enumerate(chips):
                src = 2 * px + py
                pltpu.make_async_remote_copy(
                    src_ref=outs[t].at[src, 1 - mc], dst_ref=outs[t].at[src, 1 - mc], send_sem=ssem.at[t, 3 + j],
                    recv_sem=rsem.at[t, 3 + j], device_id=sib, device_id_type=MESH).wait_recv()
        for cp in sends:
            cp.wait_send()

    return pl.pallas_call(
        body, name="weights_allgather",
        out_shape=[jax.ShapeDtypeStruct((4,) + s.shape, s.dtype) for s in shards],
        in_specs=[ANY] * n, out_specs=[ANY] * n,
        scratch_shapes=[pltpu.SemaphoreType.DMA((n, 6)), pltpu.SemaphoreType.DMA((n, 6))],
    )(*shards)


def _pair_exchange(grads):
    n = len(grads)

    def body(*refs):
        ins, outs = refs[:n], refs[n:2 * n]
        ssem, rsem = refs[2 * n:]
        mx, my, mc = _coords()
        sib = (mx, my, 1 - mc)
        sends = []
        for t in range(n):
            cp = pltpu.make_async_remote_copy(
                src_ref=ins[t].at[1 - mc], dst_ref=outs[t], send_sem=ssem.at[t], recv_sem=rsem.at[t],
                device_id=sib, device_id_type=MESH)
            cp.start()
            sends.append(cp)
        for cp in sends:
            cp.wait_recv()
        for cp in sends:
            cp.wait_send()

    return pl.pallas_call(
        body, name="grad_pair_exchange",
        out_shape=[jax.ShapeDtypeStruct(g.shape[1:], g.dtype) for g in grads],
        in_specs=[ANY] * n, out_specs=[ANY] * n,
        scratch_shapes=[pltpu.SemaphoreType.DMA((n,)), pltpu.SemaphoreType.DMA((n,))],
    )(*grads)


def _scatter_copies(ins, outs, ssem, rsem):
    mx, my, mc = _coords()
    me = 2 * mx + my
    mine, theirs = [], []
    for t in range(len(ins)):
        for j in range(1, 4):
            px, py = _flip(mx, j & 2), _flip(my, j & 1)
            mine.append(pltpu.make_async_remote_copy(
                src_ref=ins[t].at[2 * px + py], dst_ref=outs[t].at[me], send_sem=ssem.at[t, j - 1],
                recv_sem=rsem.at[t, j - 1], device_id=(px, py, mc), device_id_type=MESH))
            theirs.append(pltpu.make_async_remote_copy(
                src_ref=ins[t].at[me], dst_ref=outs[t].at[2 * px + py], send_sem=ssem.at[t, j - 1],
                recv_sem=rsem.at[t, j - 1], device_id=(px, py, mc), device_id_type=MESH))
    return mine, theirs


def _half_exchange(halves, smalls):
    n = len(halves)

    def body(*refs):
        ins, sm_ref = refs[:n], refs[n]
        outs, all_ref = refs[n + 1:2 * n + 1], refs[2 * n + 1]
        ssem, rsem, gsem, hsem, lsem = refs[2 * n + 2:]
        mx, my, mc = _coords()
        me = 4 * mx + 2 * my + mc
        sib = (mx, my, 1 - mc)
        local = pltpu.make_async_copy(sm_ref, all_ref.at[me], lsem)
        local.start()
        sends = []
        for t in range(n):
            rc = pltpu.make_async_remote_copy(
                src_ref=ins[t], dst_ref=outs[t], send_sem=ssem.at[t], recv_sem=rsem.at[t],
                device_id=sib, device_id_type=MESH)
            rc.start()
            sends.append(rc)
        gathers = []
        for j in range(1, 8):
            peer = (_flip(mx, j & 4), _flip(my, j & 2), _flip(mc, j & 1))
            cp = pltpu.make_async_remote_copy(
                src_ref=sm_ref, dst_ref=all_ref.at[me], send_sem=gsem.at[j - 1], recv_sem=hsem.at[j - 1],
                device_id=peer, device_id_type=MESH)
            cp.start()
            gathers.append(cp)
        for cp in sends:
            cp.wait_recv()
        for j in range(1, 8):
            px, py, pc = _flip(mx, j & 4), _flip(my, j & 2), _flip(mc, j & 1)
            pltpu.make_async_remote_copy(
                src_ref=sm_ref, dst_ref=all_ref.at[4 * px + 2 * py + pc], send_sem=gsem.at[j - 1],
                recv_sem=hsem.at[j - 1], device_id=(px, py, pc), device_id_type=MESH).wait_recv()
        for cp in sends + gathers:
            cp.wait_send()
        local.wait()

    vm = pl.BlockSpec(memory_space=pltpu.VMEM)
    outs = pl.pallas_call(
        body, name="grad_half_exchange",
        out_shape=[jax.ShapeDtypeStruct(h.shape, h.dtype) for h in halves]
                  + [jax.ShapeDtypeStruct((8,) + smalls.shape, smalls.dtype)],
        in_specs=[ANY] * n + [vm], out_specs=[ANY] * n + [vm],
        scratch_shapes=[pltpu.SemaphoreType.DMA((n,)), pltpu.SemaphoreType.DMA((n,)),
                        pltpu.SemaphoreType.DMA((7,)), pltpu.SemaphoreType.DMA((7,)), pltpu.SemaphoreType.DMA(())],
    )(*halves, smalls)
    return outs[:n], outs[n]


def _row_tile(rows, cols, bytes_per_row_elem=4, budget=2 * 1024 * 1024):
    t = rows
    while t % 2 == 0 and t > 16 and t * cols * bytes_per_row_elem > budget:
        t //= 2
    return t


def _pair_add(core, g, r):
    _, _, rh, cc = g.shape
    tr = _row_tile(rh, cc)

    def body(core_ref, g_ref, r_ref, o_ref):
        o_ref[...] = (g_ref[...].astype(F32) + r_ref[...].astype(F32)).astype(o_ref.dtype)

    return pl.pallas_call(
        body, name="grad_pair_add",
        grid_spec=pltpu.PrefetchScalarGridSpec(
            num_scalar_prefetch=1, grid=(4, rh // tr),
            in_specs=[pl.BlockSpec((None, None, tr, cc), lambda k, i, c: (c[0], k, i, 0)),
                      pl.BlockSpec((None, tr, cc), lambda k, i, c: (k, i, 0))],
            out_specs=pl.BlockSpec((None, tr, cc), lambda k, i, c: (k, i, 0))),
        out_shape=jax.ShapeDtypeStruct(r.shape, BF16),
        compiler_params=_params(),
    )(core, g, r)


def _sum4(chip, p, q):
    _, rh, cc = q.shape
    tr = _row_tile(rh, cc)

    def body(chip_ref, p_ref, q1_ref, q2_ref, q3_ref, o_ref):
        o_ref[...] = ((p_ref[...].astype(F32) + q1_ref[...].astype(F32)) + q2_ref[...].astype(F32)) + q3_ref[...].astype(F32)

    def other(j):
        return pl.BlockSpec((None, tr, cc), lambda i, c: (jnp.bitwise_xor(c[0], j), i, 0))

    return pl.pallas_call(
        body, name="grad_sum4",
        grid_spec=pltpu.PrefetchScalarGridSpec(
            num_scalar_prefetch=1, grid=(rh // tr,),
            in_specs=[pl.BlockSpec((None, tr, cc), lambda i, c: (c[0], i, 0)), other(1), other(2), other(3)],
            out_specs=pl.BlockSpec((tr, cc), lambda i, c: (i, 0))),
        out_shape=jax.ShapeDtypeStruct((rh, cc), F32),
        compiler_params=_params(),
    )(chip, p, q, q, q)


def _adamw_math(w, g, m, v):
    m = ADAM_B1 * m + (1.0 - ADAM_B1) * g
    v = ADAM_B2 * v + (1.0 - ADAM_B2) * (g * g)
    m_hat = m / (1.0 - ADAM_B1 ** ADAM_STEP)
    v_hat = v / (1.0 - ADAM_B2 ** ADAM_STEP)
    delta = -ADAM_LR * (m_hat / (jnp.sqrt(v_hat) + ADAM_EPS) + ADAM_WD * w)
    return delta, m, v


def _adamw_halves(core, w, mine, recv, m, v, name):
    rows, cols = w.shape
    rh = rows // 2
    tr = _row_tile(rh, cols, budget=1024 * 1024)
    nh = rh // tr

    def body(core_ref, w_ref, a_ref, b_ref, m_ref, v_ref, g_ref, d_ref, mo_ref, vo_ref):
        g = jnp.where(pl.program_id(0) == core_ref[0], a_ref[...], b_ref[...])
        d, mn, vn = _adamw_math(w_ref[...], g, m_ref[...], v_ref[...])
        g_ref[...] = g
        d_ref[...] = d
        mo_ref[...] = mn
        vo_ref[...] = vn

    full = pl.BlockSpec((tr, cols), lambda h, i, c: (h * nh + i, 0))
    half = pl.BlockSpec((tr, cols), lambda h, i, c: (i, 0))
    sd = jax.ShapeDtypeStruct((rows, cols), F32)
    return pl.pallas_call(
        body, name=name,
        grid_spec=pltpu.PrefetchScalarGridSpec(
            num_scalar_prefetch=1, grid=(2, nh),
            in_specs=[full, half, half, full, full], out_specs=[full] * 4),
        out_shape=[sd, sd, sd, sd], compiler_params=_params(),
    )(core, w, mine, recv, m, v)


def _mod_part(c_all, w_ada, b_shard):
    cols = w_ada.shape[1]

    def body(c_ref, w_ref, b_ref, o_ref, sc_ref):
        cv = c_ref[...]
        sc = cv * _sigmoid(cv)
        sc_ref[...] = sc
        o_ref[...] = jnp.dot(sc, w_ref[...], preferred_element_type=F32,
                             precision=lax.Precision.HIGHEST) + b_ref[...]

    return pl.pallas_call(
        body, name="adaln_mod",
        out_shape=[jax.ShapeDtypeStruct((8, cols), F32), jax.ShapeDtypeStruct(c_all.shape, F32)],
        in_specs=[pl.BlockSpec(memory_space=pltpu.VMEM)] * 3,
        out_specs=[pl.BlockSpec(memory_space=pltpu.VMEM)] * 2, compiler_params=_params(),
    )(c_all, w_ada, b_shard)


def _inproj_nat(x, mod, norm_w, w, *, tm=1024, tn=1664):
    S, D = x.shape
    N = w.shape[1]
    chunk = 256

    def body(x_ref, mod_ref, nw_ref, w_ref, proj_ref, h_ref, ht_ref):
        @pl.when(pl.program_id(1) == 0)
        def _():
            shift = mod_ref[:, 0:D]
            scale1 = 1.0 + mod_ref[:, D:2 * D]
            for r in range(tm // chunk):
                xv = x_ref[pl.ds(r * chunk, chunk), :]
                ms = jnp.mean(xv * xv, axis=-1, keepdims=True)
                h = (xv * lax.rsqrt(ms + EPS) * nw_ref[...]) * scale1 + shift
                h_ref[pl.ds(r * chunk, chunk), :] = h.astype(BF16)
                ht_ref[:, pl.ds(r * chunk, chunk)] = h.T.astype(BF16)
        proj_ref[...] = jnp.dot(h_ref[...], w_ref[...], preferred_element_type=F32).astype(BF16)

    return pl.pallas_call(
        body, name="inproj_nat", grid=(S // tm, N // tn),
        in_specs=[pl.BlockSpec((tm, D), lambda i, j: (i, 0)),
                  pl.BlockSpec((1, 3 * D), lambda i, j: (0, 0)),
                  pl.BlockSpec((1, D), lambda i, j: (0, 0)),
                  pl.BlockSpec((D, tn), lambda i, j: (0, j))],
        out_specs=[pl.BlockSpec((tm, tn), lambda i, j: (i, j)),
                   pl.BlockSpec((tm, D), lambda i, j: (i, 0)),
                   pl.BlockSpec((D, tm), lambda i, j: (0, i))],
        out_shape=[jax.ShapeDtypeStruct((S, N), BF16), jax.ShapeDtypeStruct((S, D), BF16),
                   jax.ShapeDtypeStruct((D, S), BF16)],
        compiler_params=_params(),
    )(x, mod, norm_w, w)


def _matmul(a, b, *, tm, tn, name):
    M, K = a.shape
    N = b.shape[1]

    def body(a_ref, b_ref, o_ref):
        o_ref[...] = jnp.dot(a_ref[...], b_ref[...], preferred_element_type=F32).astype(o_ref.dtype)

    return pl.pallas_call(
        body, name=name, grid=(M // tm, N // tn),
        in_specs=[pl.BlockSpec((tm, K), lambda i, j: (i, 0)), pl.BlockSpec((K, tn), lambda i, j: (0, j))],
        out_specs=pl.BlockSpec((tm, tn), lambda i, j: (i, j)),
        out_shape=jax.ShapeDtypeStruct((M, N), BF16), compiler_params=_params(),
    )(a, b)


def _matmul_acc(a, b, *, tn, tk, name):
    M, K = a.shape
    N = b.shape[1]
    nk = K // tk

    def body(a_ref, b_ref, o_ref, acc_ref):
        k = pl.program_id(1)

        @pl.when(k == 0)
        def _():
            acc_ref[...] = jnp.zeros_like(acc_ref)

        acc_ref[...] += jnp.dot(a_ref[...], b_ref[...], preferred_element_type=F32)

        @pl.when(k == nk - 1)
        def _():
            o_ref[...] = acc_ref[...].astype(o_ref.dtype)

    return pl.pallas_call(
        body, name=name, grid=(N // tn, nk),
        in_specs=[pl.BlockSpec((M, tk), lambda j, k: (0, k)), pl.BlockSpec((tk, tn), lambda j, k: (k, j))],
        out_specs=pl.BlockSpec((M, tn), lambda j, k: (0, j)),
        out_shape=jax.ShapeDtypeStruct((M, N), BF16),
        scratch_shapes=[pltpu.VMEM((M, tn), F32)], compiler_params=_params(),
    )(a, b)


SHARD_COLS = 2816


def _column_tables(part):
    if part == "nat":
        bw = 256
        orig = [j * bw if j < 16 else 8704 + (j - 16) * bw for j in range(NAT // bw)]
    else:
        bw = 128
        orig = []
        for jj in range(NQKV // bw):
            g, p, t = jj // 12, (jj % 12) // 3, jj % 3
            orig.append(4096 + t * 1536 + g * 512 + p * 128)
    tk = jnp.array([o // SHARD_COLS for o in orig], jnp.int32)
    tc = jnp.array([(o % SHARD_COLS) // bw for o in orig], jnp.int32)
    return tk, tc, bw


def _weights_prep(gath, own, chipv, part):
    D = D_MODEL
    tk, tc, bw = _column_tables(part)
    n = tk.shape[0]

    def body(tk_ref, tc_ref, chip_ref, g_ref, o_ref, w_ref):
        w_ref[...] = jnp.where(tk_ref[pl.program_id(0)] == chip_ref[0], o_ref[...], g_ref[...])

    def gath_idx(j, tk, tc, ch):
        k = tk[j]
        return (jnp.where(k == ch[0], (k + 1) % 4, k), 0, tc[j])

    return pl.pallas_call(
        body, name="weights_prep_" + part,
        grid_spec=pltpu.PrefetchScalarGridSpec(
            num_scalar_prefetch=3, grid=(n,),
            in_specs=[pl.BlockSpec((None, D, bw), gath_idx),
                      pl.BlockSpec((D, bw), lambda j, tk, tc, ch: (0, tc[j]))],
            out_specs=pl.BlockSpec((D, bw), lambda j, tk, tc, ch: (0, j))),
        out_shape=jax.ShapeDtypeStruct((D, n * bw), BF16),
        compiler_params=_params(),
    )(tk, tc, chipv, gath, own)


def _grad_relayout(gw, into, part):
    D = D_MODEL
    tk, tc, bw = _column_tables(part)
    n = tk.shape[0]

    def body(tk_ref, tc_ref, g_ref, into_ref, o_ref):
        o_ref[0] = g_ref[0:D // 2, :]
        o_ref[1] = g_ref[D // 2:D, :]

    return pl.pallas_call(
        body, name="grad_relayout_" + part,
        grid_spec=pltpu.PrefetchScalarGridSpec(
            num_scalar_prefetch=2, grid=(n,),
            in_specs=[pl.BlockSpec((D, bw), lambda j, tk, tc: (0, j)), ANY],
            out_specs=pl.BlockSpec((2, None, D // 2, bw), lambda j, tk, tc: (0, tk[j], 0, tc[j]))),
        out_shape=jax.ShapeDtypeStruct(into.shape, BF16),
        input_output_aliases={3: 0},
        compiler_params=_params(),
    )(tk, tc, gw, into)


def _head_norm(xb, w, ebd):
    x = xb.astype(F32)
    ss = jnp.dot((x * x).astype(BF16), ebd, preferred_element_type=F32)
    return x * lax.rsqrt(ss * (1.0 / HEAD_DIM) + EPS) * w


def _window_mask(no_prev):
    qi = lax.broadcasted_iota(jnp.int32, (128, 256), 0)
    kc = lax.broadcasted_iota(jnp.int32, (128, 256), 1)
    ok = (kc >= qi) & (kc <= qi + 128)
    if no_prev is not None:
        ok = ok & ((kc >= 128) | jnp.logical_not(no_prev))
    return ok


def _lane_masks():
    lane = lax.broadcasted_iota(jnp.int32, (1, 128), 1)
    return [(lane < HEAD_DIM).astype(F32), (lane >= HEAD_DIM).astype(F32)]


def _head_col(v, mh):
    return jnp.sum(v * mh, axis=1, keepdims=True) * (1.0 / HEAD_DIM)


def _attn_fwd(projq, qnw2, knw2, ebd, g):
    S = projq.shape[0]
    d = DILATIONS[g]
    T = ATT_TILE
    nt = S // T
    nb = T // (128 * d)
    sdt = BF16 if d == 1 else F32

    def body(cur_ref, qw_ref, kw_ref, ebd_ref, o_ref, lse_ref, qs_s, ks_s, vs_s):
        t = pl.program_id(1)
        e = ebd_ref[...]

        @pl.when(t == 0)
        def _():
            ks_s[pl.ds(0, T), :] = jnp.zeros((T, 128), sdt)
            vs_s[pl.ds(0, T), :] = jnp.zeros((T, 128), sdt)

        @pl.when(t > 0)
        def _():
            ks_s[pl.ds(0, T), :] = ks_s[pl.ds(T, T), :]
            vs_s[pl.ds(0, T), :] = vs_s[pl.ds(T, T), :]

        qs_s[...] = (_head_norm(cur_ref[:, 0:128], qw_ref[...], e) * (QK_SCALE * LOG2E)).astype(sdt)
        ks_s[pl.ds(T, T), :] = _head_norm(cur_ref[:, 128:256], kw_ref[...], e).astype(sdt)
        vs_s[pl.ds(T, T), :] = cur_ref[:, 256:384].astype(sdt)
        m0, m1 = _lane_masks()
        first = t == 0
        for r in range(d):
            for bb in range(nb):
                q0 = r + bb * 128 * d
                k0 = T + r + (bb - 1) * 128 * d
                qs = qs_s[pl.ds(q0, 128, stride=d), :].astype(F32)
                kb = ks_s[pl.ds(k0, 256, stride=d), :].astype(BF16)
                vb = vs_s[pl.ds(k0, 256, stride=d), :].astype(BF16)
                ok = _window_mask(first if bb == 0 else None)
                ok2 = jnp.concatenate([ok, ok], axis=0)
                q2 = jnp.concatenate([qs * m0, qs * m1], axis=0).astype(BF16)
                s = lax.dot_general(q2, kb, (((1,), (1,)), ((), ())), preferred_element_type=F32)
                s = jnp.where(ok2, s, NEG_INF)
                mx = jnp.max(s, axis=-1, keepdims=True)
                p = jnp.exp2(s - mx)
                l = jnp.sum(p, axis=-1, keepdims=True)
                o = jnp.dot(p.astype(BF16), vb, preferred_element_type=F32) / l
                lse = mx * LN2 + jnp.log(l)
                o_ref[pl.ds(q0, 128, stride=d), :] = o[0:128] * m0 + o[128:256] * m1
                lse_ref[pl.ds(q0, 128, stride=d), :] = lse[0:128] * m0 + lse[128:256] * m1

    return pl.pallas_call(
        body, name=f"attn_fwd_g{g}", grid=(4, nt),
        in_specs=[pl.BlockSpec((T, 384), lambda p, t: (t, g * 4 + p)),
                  pl.BlockSpec((1, 128), lambda p, t: (0, 0)),
                  pl.BlockSpec((1, 128), lambda p, t: (0, 0)),
                  pl.BlockSpec((128, 128), lambda p, t: (0, 0))],
        out_specs=[pl.BlockSpec((T, 128), lambda p, t: (t, p)), pl.BlockSpec((T, 128), lambda p, t: (t, p))],
        out_shape=[jax.ShapeDtypeStruct((S, ATTN_OUT), F32), jax.ShapeDtypeStruct((S, ATTN_OUT), F32)],
        scratch_shapes=[pltpu.VMEM((T, 128), sdt), pltpu.VMEM((2 * T, 128), sdt), pltpu.VMEM((2 * T, 128), sdt)],
        compiler_params=_params(),
    )(projq, qnw2, knw2, ebd)


def _attn_bwd(projq, dprojq, d_o, lse, delta, qnw2, knw2, ebd, g):
    S = projq.shape[0]
    d = DILATIONS[g]
    T = ATT_TILE
    nt = S // T
    nb = T // (128 * d)
    sdt = F32

    def norm_bwd(raw_b, dy, w, e):
        x = raw_b.astype(F32)
        ss = jnp.dot((x * x).astype(BF16), e, preferred_element_type=F32)
        rstd = lax.rsqrt(ss * (1.0 / HEAD_DIM) + EPS)
        xh = x * rstd
        gw = jnp.sum(dy * xh, axis=0, keepdims=True)
        dxh = dy * w
        mean = jnp.dot((dxh * xh).astype(BF16), e, preferred_element_type=F32) * (1.0 / HEAD_DIM)
        return rstd * (dxh - xh * mean), gw

    nt_dims = (((1,), (1,)), ((), ()))
    tn_dims = (((0,), (0,)), ((), ()))

    def unit_stacked(qs, dos, ls, dls, kb, vb, ok, m0, m1):
        ok2 = jnp.concatenate([ok, ok], axis=0)
        q2 = jnp.concatenate([qs * m0, qs * m1], axis=0).astype(BF16)
        do2 = jnp.concatenate([dos * m0, dos * m1], axis=0).astype(BF16)
        lcol = jnp.concatenate([_head_col(ls, m0), _head_col(ls, m1)], axis=0)
        dcol = jnp.concatenate([_head_col(dls, m0), _head_col(dls, m1)], axis=0)
        s = jnp.where(ok2, lax.dot_general(q2, kb, nt_dims, preferred_element_type=F32), NEG_INF)
        p = jnp.exp2(s - lcol * LOG2E)
        dp = lax.dot_general(do2, vb, nt_dims, preferred_element_type=F32)
        ds = (p * (dp - dcol)).astype(BF16)
        dq2 = jnp.dot(ds, kb, preferred_element_type=F32)
        dk = lax.dot_general(ds, q2, tn_dims, preferred_element_type=F32)
        dv = lax.dot_general(p.astype(BF16), do2, tn_dims, preferred_element_type=F32)
        return dq2[0:128] * m0 + dq2[128:256] * m1, dk, dv

    def unit_per_head(qs, dos, ls, dls, kb, vb, ok, m0, m1):
        dq_t = dk_t = dv_t = None
        for mh in (m0, m1):
            qh = (qs * mh).astype(BF16)
            doh = (dos * mh).astype(BF16)
            s = jnp.where(ok, lax.dot_general(qh, kb, nt_dims, preferred_element_type=F32), NEG_INF)
            p = jnp.exp2(s - _head_col(ls, mh) * LOG2E)
            dp = lax.dot_general(doh, vb, nt_dims, preferred_element_type=F32)
            ds = (p * (dp - _head_col(dls, mh))).astype(BF16)
            dq = jnp.dot(ds, kb, preferred_element_type=F32) * mh
            dk = lax.dot_general(ds, qh, tn_dims, preferred_element_type=F32)
            dv = lax.dot_general(p.astype(BF16), doh, tn_dims, preferred_element_type=F32)
            dq_t = dq if dq_t is None else dq_t + dq
            dk_t = dk if dk_t is None else dk_t + dk
            dv_t = dv if dv_t is None else dv_t + dv
        return dq_t, dk_t, dv_t

    unit = unit_per_head if d == 1 else unit_stacked

    def body(cur_ref, prev_ref, do_ref, lse_ref, dl_ref, qw_ref, kw_ref, ebd_ref, dp_in_ref,
             out_ref, gqk_ref, qs_s, ks_s, vs_s, do_s, dq_cur, dq_prev, dk_acc, dv_acc):
        i = pl.program_id(1)
        e = ebd_ref[...]
        m0, m1 = _lane_masks()

        @pl.when(i == 0)
        def _():
            dk_acc[...] = jnp.zeros_like(dk_acc)
            dv_acc[...] = jnp.zeros_like(dv_acc)
            dq_prev[...] = jnp.zeros_like(dq_prev)
            gqk_ref[...] = jnp.zeros_like(gqk_ref)

        @pl.when(i < nt)
        def _():
            qs_s[...] = _head_norm(cur_ref[:, 0:128], qw_ref[...], e) * (QK_SCALE * LOG2E)
            ks_s[pl.ds(0, T), :] = _head_norm(prev_ref[:, 128:256], kw_ref[...], e)
            ks_s[pl.ds(T, T), :] = _head_norm(cur_ref[:, 128:256], kw_ref[...], e)
            vs_s[pl.ds(0, T), :] = prev_ref[:, 256:384].astype(F32)
            vs_s[pl.ds(T, T), :] = cur_ref[:, 256:384].astype(F32)
            do_s[...] = do_ref[...].astype(F32)
            first = i == 0
            for r in range(d):
                for bb in range(nb):
                    q0 = r + bb * 128 * d
                    k0 = T + r + (bb - 1) * 128 * d
                    qs = qs_s[pl.ds(q0, 128, stride=d), :]
                    dos = do_s[pl.ds(q0, 128, stride=d), :]
                    ls = lse_ref[pl.ds(q0, 128, stride=d), :]
                    dls = dl_ref[pl.ds(q0, 128, stride=d), :]
                    kb = ks_s[pl.ds(k0, 256, stride=d), :].astype(BF16)
                    vb = vs_s[pl.ds(k0, 256, stride=d), :].astype(BF16)
                    ok = _window_mask(first if bb == 0 else None)
                    dq_t, dk_t, dv_t = unit(qs, dos, ls, dls, kb, vb, ok, m0, m1)
                    dq_cur[pl.ds(q0, 128, stride=d), :] = dq_t
                    dk_acc[pl.ds(k0, 256, stride=d), :] = dk_acc[pl.ds(k0, 256, stride=d), :] + dk_t
                    dv_acc[pl.ds(k0, 256, stride=d), :] = dv_acc[pl.ds(k0, 256, stride=d), :] + dv_t

        @pl.when(i >= 1)
        def _():
            dq_raw, gq = norm_bwd(prev_ref[:, 0:128], dq_prev[...] * QK_SCALE, qw_ref[...], e)
            dk_raw, gk = norm_bwd(prev_ref[:, 128:256], dk_acc[pl.ds(0, T), :] * LN2, kw_ref[...], e)
            out_ref[:, 0:128] = dq_raw.astype(BF16)
            out_ref[:, 128:256] = dk_raw.astype(BF16)
            out_ref[:, 256:384] = dv_acc[pl.ds(0, T), :].astype(BF16)
            gqk_ref[0:1, :] += gq
            gqk_ref[1:2, :] += gk

        dq_prev[...] = dq_cur[...]
        dk_acc[pl.ds(0, T), :] = dk_acc[pl.ds(T, T), :]
        dv_acc[pl.ds(0, T), :] = dv_acc[pl.ds(T, T), :]
        dk_acc[pl.ds(T, T), :] = jnp.zeros((T, 128), F32)
        dv_acc[pl.ds(T, T), :] = jnp.zeros((T, 128), F32)

    last = nt - 1
    qtile = lambda p, i: (jnp.minimum(i, last), p)
    return pl.pallas_call(
        body, name=f"attn_bwd_g{g}", grid=(4, nt + 1),
        in_specs=[pl.BlockSpec((T, 384), lambda p, i: (jnp.minimum(i, last), g * 4 + p)),
                  pl.BlockSpec((T, 384), lambda p, i: (jnp.maximum(i - 1, 0), g * 4 + p)),
                  pl.BlockSpec((T, 128), qtile), pl.BlockSpec((T, 128), qtile), pl.BlockSpec((T, 128), qtile),
                  pl.BlockSpec((1, 128), lambda p, i: (0, 0)),
                  pl.BlockSpec((1, 128), lambda p, i: (0, 0)),
                  pl.BlockSpec((128, 128), lambda p, i: (0, 0)),
                  ANY],
        out_specs=[pl.BlockSpec((T, 384), lambda p, i: (jnp.maximum(i - 1, 0), g * 4 + p)),
                   pl.BlockSpec((None, 8, 128), lambda p, i: (p, 0, 0))],
        out_shape=[jax.ShapeDtypeStruct(dprojq.shape, BF16), jax.ShapeDtypeStruct((4, 8, 128), F32)],
        scratch_shapes=[pltpu.VMEM((T, 128), sdt), pltpu.VMEM((2 * T, 128), sdt), pltpu.VMEM((2 * T, 128), sdt),
                        pltpu.VMEM((T, 128), sdt), pltpu.VMEM((T, 128), F32), pltpu.VMEM((T, 128), F32),
                        pltpu.VMEM((2 * T, 128), F32), pltpu.VMEM((2 * T, 128), F32)],
        input_output_aliases={8: 0},
        compiler_params=_params(),
    )(projq, projq, d_o, lse, delta, qnw2, knw2, ebd, dprojq)


def _mid(projn, o_g, lse_g, x, tgt, mod, convw8, wc, wa, wo, ebd, *, tm=256):
    S, D = x.shape
    nt = S // tm
    nst = max(1, 256 // tm)
    assert nt % nst == 0
    hb = tm // 16

    def body(pn_ref, hc_ref, hx_ref, o0_ref, o1_ref, o2_ref, l0_ref, l1_ref, l2_ref, x_ref, t_ref, mod_ref,
             cw_ref, ebd_ref, wc_hbm, wa_hbm, wo_hbm,
             dpn_ref, do_ref, lse_ref, dl_ref, dout_ref, vacc_ref, gwo_hbm, gwc_hbm, gwa_hbm,
             wc_s, wa_s, wo_s, gwo_s, gwc_s, gwa_s, carry_s, *stashes):
        i = pl.program_id(0)
        ti = nt - 1 - i
        par = i % nst
        stash = pl.ds(pl.multiple_of(par * tm, tm), tm)

        @pl.when(i == 0)
        def _():
            for src, dst in ((wc_hbm, wc_s), (wa_hbm, wa_s), (wo_hbm, wo_s)):
                pltpu.sync_copy(src, dst)
            gwo_s[...] = jnp.zeros_like(gwo_s)
            gwc_s[...] = jnp.zeros_like(gwc_s)
            gwa_s[...] = jnp.zeros_like(gwa_s)
            carry_s[...] = jnp.zeros_like(carry_s)
            vacc_ref[...] = jnp.zeros_like(vacc_ref)

        rows = lax.broadcasted_iota(jnp.int32, (tm, D), 0)
        w0, w1, w2 = cw_ref[0:1, :], cw_ref[1:2, :], cw_ref[2:3, :]
        gate = mod_ref[:, 2 * D:3 * D]

        b_a = pn_ref[:, BA:BA + D].astype(F32)
        c_a = pn_ref[:, CA:CA + D].astype(F32)
        x_a = pn_ref[:, XA:XA + D].astype(F32)
        z_a = pn_ref[:, ZA:ZA + D].astype(F32)
        u = c_a * x_a
        has_prev = (ti > 0).astype(F32)
        uh = hc_ref[...].astype(F32) * hx_ref[...].astype(F32) * has_prev
        h14, h15 = _row(uh, 14), _row(uh, 15)
        u_m1 = jnp.where(rows == 0, h15, pltpu.roll(u, 1, 0))
        u_m2 = jnp.where(rows == 0, h14, jnp.where(rows == 1, h15, pltpu.roll(u, 2, 0)))
        conv = w0 * u_m2 + w1 * u_m1 + w2 * u
        sg_a = _sigmoid(z_a)
        sz_a = z_a * sg_a
        y_a = b_a * conv * sz_a
        y_ab = y_a.astype(BF16)
        pa = jnp.dot(y_ab, wc_s[...], preferred_element_type=F32)

        l0, l1, l2 = l0_ref[...], l1_ref[...], l2_ref[...]
        mxl = jnp.maximum(jnp.maximum(l0, l1), l2)
        e0, e1, e2 = jnp.exp(l0 - mxl), jnp.exp(l1 - mxl), jnp.exp(l2 - mxl)
        den = e0 + e1 + e2
        attn = (e0 * o0_ref[...] + e1 * o1_ref[...] + e2 * o2_ref[...]) / den
        lse_ref[...] = mxl + jnp.log(den)
        z_b = pn_ref[:, ZB:ZB + ATTN_OUT].astype(F32)
        sg_b = _sigmoid(z_b)
        sz_b = z_b * sg_b
        y_b = attn * sz_b
        y_bb = y_b.astype(BF16)
        pb = jnp.dot(y_bb, wa_s[...], preferred_element_type=F32)

        s_a = _sigmoid(pn_ref[:, GA:GA + D].astype(F32))
        s_b = _sigmoid(pn_ref[:, GB:GB + D].astype(F32))
        merged = s_a * pa + s_b * pb
        merged_b = merged.astype(BF16)
        mo = jnp.dot(merged_b, wo_s[...], preferred_element_type=F32)
        diff = x_ref[...] + gate * mo - t_ref[...]
        dout = diff * (1.0 / D)
        dout_ref[...] = dout
        vacc_ref[4:5, :] += jnp.sum(diff * diff, axis=0, keepdims=True)
        vacc_ref[0:1, :] += jnp.sum(dout * mo, axis=0, keepdims=True)

        d_mo = (dout * gate).astype(BF16)
        nt_dims = (((1,), (1,)), ((), ()))
        dmerged = lax.dot_general(d_mo, wo_s[...], nt_dims, preferred_element_type=F32)
        dpa = (dmerged * s_a).astype(BF16)
        dpb = (dmerged * s_b).astype(BF16)
        dpn_ref[:, GA:GA + D] = (dmerged * pa * s_a * (1.0 - s_a)).astype(BF16)
        dpn_ref[:, GB:GB + D] = (dmerged * pb * s_b * (1.0 - s_b)).astype(BF16)
        tn = (((0,), (0,)), ((), ()))
        if nst == 1:
            gwo_s[...] += lax.dot_general(merged_b, d_mo, tn, preferred_element_type=F32)
            gwc_s[...] += lax.dot_general(y_ab, dpa, tn, preferred_element_type=F32)
            gwa_s[...] += lax.dot_general(y_bb, dpb, tn, preferred_element_type=F32)
        else:
            st_m, st_d, st_ya, st_pa, st_yb, st_pb = stashes
            st_m[stash, :] = merged_b
            st_d[stash, :] = d_mo
            st_ya[stash, :] = y_ab
            st_pa[stash, :] = dpa
            st_yb[stash, :] = y_bb
            st_pb[stash, :] = dpb

            @pl.when(par == nst - 1)
            def _():
                gwo_s[...] += lax.dot_general(st_m[...], st_d[...], tn, preferred_element_type=F32)
                gwc_s[...] += lax.dot_general(st_ya[...], st_pa[...], tn, preferred_element_type=F32)
                gwa_s[...] += lax.dot_general(st_yb[...], st_pb[...], tn, preferred_element_type=F32)

        dy_a = lax.dot_general(dpa, wc_s[...], nt_dims, preferred_element_type=F32)
        dy_b = lax.dot_general(dpb, wa_s[...], nt_dims, preferred_element_type=F32)

        dattn = dy_b * sz_b
        dpn_ref[:, ZB:ZB + ATTN_OUT] = (dy_b * attn * sg_b * (1.0 + z_b * (1.0 - sg_b))).astype(BF16)
        do_ref[...] = dattn.astype(BF16)
        dl_ref[...] = jnp.dot((dattn * attn).astype(BF16), ebd_ref[...], preferred_element_type=F32)

        dpn_ref[:, BA:BA + D] = (dy_a * conv * sz_a).astype(BF16)
        dpn_ref[:, ZA:ZA + D] = (dy_a * b_a * conv * sg_a * (1.0 + z_a * (1.0 - sg_a))).astype(BF16)
        dconv = dy_a * b_a * sz_a
        vacc_ref[1:2, :] += jnp.sum(dconv * u_m2, axis=0, keepdims=True)
        vacc_ref[2:3, :] += jnp.sum(dconv * u_m1, axis=0, keepdims=True)
        vacc_ref[3:4, :] += jnp.sum(dconv * u, axis=0, keepdims=True)
        nxt = carry_s[...]
        n0, n1 = _row(nxt, 0), _row(nxt, 1)
        dc_p1 = jnp.where(rows == tm - 1, n0, pltpu.roll(dconv, tm - 1, 0))
        dc_p2 = jnp.where(rows == tm - 1, n1, jnp.where(rows == tm - 2, n0, pltpu.roll(dconv, tm - 2, 0)))
        du = w2 * dconv + w1 * dc_p1 + w0 * dc_p2
        carry_s[...] = dconv[0:8, :]
        dpn_ref[:, CA:CA + D] = (du * x_a).astype(BF16)
        dpn_ref[:, XA:XA + D] = (du * c_a).astype(BF16)

        @pl.when(i == nt - 1)
        def _():
            pltpu.sync_copy(gwo_s, gwo_hbm)
            pltpu.sync_copy(gwc_s, gwc_hbm)
            pltpu.sync_copy(gwa_s, gwa_hbm)

    rev = lambda i: (nt - 1 - i, 0)
    halo = lambda col: pl.BlockSpec((16, D), lambda i: (jnp.maximum((nt - 1 - i) * hb - 1, 0), col))
    tile512 = pl.BlockSpec((tm, ATTN_OUT), rev)
    tileD = pl.BlockSpec((tm, D), rev)
    const = lambda shape: pl.BlockSpec(shape, lambda i: (0, 0))
    return pl.pallas_call(
        body, name="mid_fwd_bwd", grid=(nt,),
        in_specs=[pl.BlockSpec((tm, NAT), rev), halo(CA // D), halo(XA // D)] + [tile512] * 6
                 + [tileD, tileD, const((1, 3 * D)), const((8, D)), const((ATTN_OUT, ATTN_OUT))] + [ANY] * 3,
        out_specs=[pl.BlockSpec((tm, NAT), rev), tile512, tile512, tile512, tileD, const((8, D)), ANY, ANY, ANY],
        out_shape=[jax.ShapeDtypeStruct((S, NAT), BF16), jax.ShapeDtypeStruct((S, ATTN_OUT), BF16),
                   jax.ShapeDtypeStruct((S, ATTN_OUT), F32), jax.ShapeDtypeStruct((S, ATTN_OUT), F32),
                   jax.ShapeDtypeStruct((S, D), F32), jax.ShapeDtypeStruct((8, D), F32),
                   jax.ShapeDtypeStruct((D, D), F32), jax.ShapeDtypeStruct((D, D), F32),
                   jax.ShapeDtypeStruct((ATTN_OUT, D), F32)],
        scratch_shapes=[pltpu.VMEM((D, D), BF16), pltpu.VMEM((ATTN_OUT, D), BF16), pltpu.VMEM((D, D), BF16),
                        pltpu.VMEM((D, D), F32), pltpu.VMEM((D, D), F32), pltpu.VMEM((ATTN_OUT, D), F32),
                        pltpu.VMEM((8, D), F32)]
                       + ([pltpu.VMEM((nst * tm, D), BF16)] * 4
                          + [pltpu.VMEM((nst * tm, ATTN_OUT), BF16), pltpu.VMEM((nst * tm, D), BF16)] if nst > 1 else []),
        compiler_params=_params(vmem=60 * 1024 * 1024),
    )(projn, projn, projn, *o_g, *lse_g, x, tgt, mod, convw8, ebd, wc, wa, wo)


def _dh_matmul(dpn, dpq, w_nat, w_qkv, parts, *, tm=512):
    S = dpn.shape[0]
    D = D_MODEL
    n = len(parts)
    nt = S // tm

    def body(dn_ref, dq_ref, wn_hbm, wq_hbm, *rest):
        p_hbm, rest = rest[:n], rest[n:]
        dh_ref = rest[0]
        q_hbm = rest[1:1 + n]
        wn_s, wq_s = rest[1 + n:3 + n]
        sems = rest[3 + n:]

        @pl.when(pl.program_id(0) == 0)
        def _():
            if n:
                mine, _ = _scatter_copies(p_hbm, q_hbm, *sems)
                for cp in mine:
                    cp.start()
            pltpu.sync_copy(wn_hbm, wn_s)
            pltpu.sync_copy(wq_hbm, wq_s)

        if n:
            @pl.when(pl.program_id(0) == nt - 1)
            def _():
                mine, theirs = _scatter_copies(p_hbm, q_hbm, *sems)
                for cp in theirs:
                    cp.wait_recv()
                for cp in mine:
                    cp.wait_send()

        nt_dims = (((1,), (1,)), ((), ()))
        dh = (lax.dot_general(dn_ref[...], wn_s[...], nt_dims, preferred_element_type=F32)
              + lax.dot_general(dq_ref[...], wq_s[...], nt_dims, preferred_element_type=F32))
        dh_ref[...] = dh.astype(BF16)

    row = lambda w: pl.BlockSpec((tm, w), lambda i: (i, 0))
    outs = pl.pallas_call(
        body, name="dh_matmul", grid=(nt,),
        in_specs=[row(NAT), row(NQKV), ANY, ANY] + [ANY] * n,
        out_specs=[row(D)] + [ANY] * n,
        out_shape=[jax.ShapeDtypeStruct((S, D), BF16)] + [jax.ShapeDtypeStruct(p.shape, p.dtype) for p in parts],
        scratch_shapes=[pltpu.VMEM((D, NAT), BF16), pltpu.VMEM((D, NQKV), BF16)]
                       + ([pltpu.SemaphoreType.DMA((n, 3)), pltpu.SemaphoreType.DMA((n, 3))] if n else []),
        compiler_params=_params(vmem=60 * 1024 * 1024),
    )(dpn, dpq, w_nat, w_qkv, *parts)
    return outs[0], outs[1:]


def _norm_bwd(dh, x, dout, mod, norm_w, *, tm=512):
    S, D = x.shape

    def body(dh_ref, x_ref, dout_ref, mod_ref, nw_ref, gx_ref, st_ref):
        @pl.when(pl.program_id(0) == 0)
        def _():
            st_ref[...] = jnp.zeros_like(st_ref)

        dh = dh_ref[...].astype(F32)
        scale1 = 1.0 + mod_ref[:, D:2 * D]
        nw = nw_ref[...]
        xv = x_ref[...]
        rstd = lax.rsqrt(jnp.mean(xv * xv, axis=-1, keepdims=True) + EPS)
        xh = xv * rstd
        dhs = dh * scale1
        dxh = dhs * nw
        dx = rstd * (dxh - xh * jnp.mean(dxh * xh, axis=-1, keepdims=True))
        gx_ref[...] = dout_ref[...] + dx
        st_ref[0:1, :] += jnp.sum(dh, axis=0, keepdims=True)
        st_ref[1:2, :] += jnp.sum(dh * (xh * nw), axis=0, keepdims=True)
        st_ref[2:3, :] += jnp.sum(dhs * xh, axis=0, keepdims=True)

    row = pl.BlockSpec((tm, D), lambda i: (i, 0))
    return pl.pallas_call(
        body, name="norm_bwd", grid=(S // tm,),
        in_specs=[row, row, row, pl.BlockSpec((1, 3 * D), lambda i: (0, 0)), pl.BlockSpec((1, D), lambda i: (0, 0))],
        out_specs=[row, pl.BlockSpec((8, D), lambda i: (0, 0))],
        out_shape=[jax.ShapeDtypeStruct((S, D), F32), jax.ShapeDtypeStruct((8, D), F32)],
        compiler_params=_params(),
    )(dh, x, dout, mod, norm_w)


def _pack_smalls(st, vacc, gqk):
    D = D_MODEL

    def body(st_ref, va_ref, gqk_ref, o_ref):
        o_ref[...] = jnp.zeros_like(o_ref)
        o_ref[0:2, :] = st_ref[0:2, :]
        o_ref[2:3, :] = va_ref[0:1, :]
        o_ref[8:9, :] = st_ref[2:3, :]
        o_ref[16:19, :] = va_ref[1:4, :]
        tot = gqk_ref[0]
        for k in range(1, gqk_ref.shape[0]):
            tot = tot + gqk_ref[k]
        tot = tot + pltpu.roll(tot, HEAD_DIM, 1)
        o_ref[24:32, 0:128] = tot
        loss = jnp.sum(va_ref[4:5, :], axis=1, keepdims=True) * (0.5 / D)
        o_ref[26:27, :] = jnp.broadcast_to(loss, (1, D))

    return pl.pallas_call(
        body, name="pack_smalls", out_shape=jax.ShapeDtypeStruct((32, D), F32),
        in_specs=[pl.BlockSpec(memory_space=pltpu.VMEM)] * 3,
        out_specs=pl.BlockSpec(memory_space=pltpu.VMEM), compiler_params=_params(),
    )(st, vacc, gqk)


def _small_update(sm_loc, dm_pad, ct_pad, w_ada, m_ada, v_ada, vec_w, vec_m, vec_v):
    D = D_MODEL
    cols = w_ada.shape[1]

    def body(sm_ref, dm_ref, ct_ref, w_ref, m_ref, v_ref, vw_ref, vm_ref, vv_ref,
             gw_ref, dw_ref, mw_ref, vw_o_ref, gv_ref, dv_ref, mv_ref, vv_o_ref):
        g = sm_ref[0]
        for k in range(1, 8):
            g = g + sm_ref[k]
        gv_ref[...] = g
        dv, mv, vv = _adamw_math(vw_ref[...], g, vm_ref[...], vv_ref[...])
        dv_ref[...] = dv
        mv_ref[...] = mv
        vv_o_ref[...] = vv
        gw = jnp.dot(ct_ref[...], dm_ref[...], preferred_element_type=F32, precision=lax.Precision.HIGHEST)
        gw_ref[...] = gw
        dw, mw, vw = _adamw_math(w_ref[...], gw, m_ref[...], v_ref[...])
        dw_ref[...] = dw
        mw_ref[...] = mw
        vw_o_ref[...] = vw

    sw = jax.ShapeDtypeStruct((D, cols), F32)
    sv = jax.ShapeDtypeStruct((32, D), F32)
    return pl.pallas_call(
        body, name="small_update", out_shape=[sw, sw, sw, sw, sv, sv, sv, sv],
        in_specs=[pl.BlockSpec(memory_space=pltpu.VMEM)] * 9,
        out_specs=[pl.BlockSpec(memory_space=pltpu.VMEM)] * 8, compiler_params=_params(),
    )(sm_loc, dm_pad, ct_pad, w_ada, m_ada, v_ada, vec_w, vec_m, vec_v)


def _local_step(xs, tg, mod, norm_w, w_nat, w_qkv, wc, wa, wo, convw8, q_norm_w, k_norm_w):
    S = xs.shape[0]
    lane = jnp.arange(128)
    ebd128 = (lane[:, None] // HEAD_DIM == lane[None, :] // HEAD_DIM).astype(BF16)
    lane5 = jnp.arange(ATTN_OUT)
    ebd512 = (lane5[:, None] // HEAD_DIM == lane5[None, :] // HEAD_DIM).astype(BF16)
    qnw2 = jnp.tile(q_norm_w, (1, 2))
    knw2 = jnp.tile(k_norm_w, (1, 2))

    projn, h, ht = _inproj_nat(xs, mod, norm_w, w_nat)
    projq = _matmul(h, w_qkv, tm=1024, tn=1536, name="inproj_qkv")
    o_g, lse_g = [], []
    for g in range(N_GROUPS):
        o, l = _attn_fwd(projq, qnw2, knw2, ebd128, g)
        o_g.append(o)
        lse_g.append(l)

    dpn, d_o, lse, delta, dout, vacc, gwo, gwc, gwa = _mid(
        projn, o_g, lse_g, xs, tg, mod, convw8, wc, wa, wo, ebd512)
    dpq = lax.empty((S, NQKV), BF16)
    gqk = []
    for g in range(N_GROUPS):
        dpq, part = _attn_bwd(projq, dpq, d_o, lse, delta, qnw2, knw2, ebd128, g)
        gqk.append(part)
    gw_nat = _matmul_acc(ht, dpn, tn=1664, tk=2048, name="grad_w_in_nat")
    gw_qkv = _matmul_acc(ht, dpq, tn=1536, tk=2048, name="grad_w_in_qkv")
    return dpn, dpq, dout, gw_nat, gw_qkv, gwc, gwa, gwo, vacc, gqk


def kernel(x, c, w_ada, b_ada, norm_w, w_in, conv_w, q_norm_w, k_norm_w, w_br_conv, w_br_attn, w_out, loss_target, m_w_ada, m_b_ada, m_norm_w, m_w_in, m_conv_w, m_q_norm_w, m_k_norm_w, m_w_br_conv, m_w_br_attn, m_w_out, v_w_ada, v_b_ada, v_norm_w, v_w_in, v_conv_w, v_q_norm_w, v_k_norm_w, v_w_br_conv, v_w_br_attn, v_w_out):
    D = D_MODEL
    S = x.shape[1]
    xs, tg = x[0], loss_target[0]
    mx, my, mc = lax.axis_index("x"), lax.axis_index("y"), lax.axis_index("c")
    chip = 2 * mx + my
    dev = 2 * chip + mc

    c_blk = jnp.concatenate([c, jnp.pad(conv_w[0], ((0, 0), (0, D - 256))), jnp.zeros((4, D), F32)], axis=0)
    cg = _allgather8(c_blk, "allgather_c")
    c_all = cg[:, 0, :]
    convw8 = jnp.pad(cg[::2, 1:4, 0:256].transpose(1, 0, 2).reshape(3, D), ((0, 5), (0, 0)))
    b_shard = lax.dynamic_slice(b_ada, (0, chip * 768), (1, 768))
    mod_part, silu_c = _mod_part(c_all, w_ada[0], b_shard)
    mod_parts = _allgather_chips(mod_part, "allgather_mod")
    mod_all = mod_parts.transpose(1, 0, 2).reshape(8, 3 * D)
    mod = lax.dynamic_slice(mod_all, (dev, 0), (1, 3 * D))

    shards = [w_in[0].astype(BF16).reshape(2, 512, 2816), w_br_conv[0].astype(BF16).reshape(2, 128, D),
              w_br_attn[0].astype(BF16).reshape(2, 256, 256), w_out[0].astype(BF16).reshape(2, 128, D)]
    gathered_w = _weights_allgather(shards)
    g_bc, g_ba, g_wo = [lax.dynamic_update_slice(g, s[None], (chip, 0, 0, 0))
                        for g, s in zip(gathered_w[1:], shards[1:])]
    chipv = jnp.reshape(chip, (1,)).astype(jnp.int32)
    g_in = gathered_w[0].reshape(4, D, SHARD_COLS)
    own_in = shards[0].reshape(D, SHARD_COLS)
    w_nat = _weights_prep(g_in, own_in, chipv, "nat")
    w_qkv = _weights_prep(g_in, own_in, chipv, "qkv")
    wc = g_bc.reshape(D, D)
    wa = g_ba.reshape(4, ATTN_OUT, 256).transpose(1, 0, 2).reshape(ATTN_OUT, D)
    wo = g_wo.reshape(D, D)

    dpn, dpq, dout, gw_nat, gw_qkv, gwc, gwa, gwo, vacc, gqk = _local_step(
        xs, tg, mod, norm_w, w_nat, w_qkv, wc, wa, wo, convw8, q_norm_w, k_norm_w)
    gw_in = _grad_relayout(gw_nat, lax.empty((2, 4, D // 2, SHARD_COLS), BF16), "nat")
    gw_in = _grad_relayout(gw_qkv, gw_in, "qkv")

    def halves(a):
        k, r, cc = a.shape
        return a.reshape(k, 2, r // 2, cc).transpose(1, 0, 2, 3)

    grads = [gw_in,
             halves(gwc.astype(BF16).reshape(4, 256, D)),
             halves(gwa.astype(BF16).reshape(ATTN_OUT, 4, 256).transpose(1, 0, 2)),
             halves(gwo.astype(BF16).reshape(4, 256, D))]
    recv = _pair_exchange(grads)
    core = jnp.reshape(mc, (1,)).astype(jnp.int32)
    parts = [_pair_add(core, gr, rc) for gr, rc in zip(grads, recv)]
    dh, gathered = _dh_matmul(dpn, dpq, w_nat, w_qkv, parts)
    grad_x, st = _norm_bwd(dh, xs, dout, mod, norm_w)
    mine = [_sum4(chipv, p, q) for p, q in zip(parts, gathered)]
    theirs, sm_all = _half_exchange(mine, _pack_smalls(st, vacc, jnp.concatenate(gqk, axis=0)))

    big = {}
    for t, (nm, w, m, v) in enumerate((("w_in", w_in, m_w_in, v_w_in), ("w_br_conv", w_br_conv, m_w_br_conv, v_w_br_conv),
                                       ("w_br_attn", w_br_attn, m_w_br_attn, v_w_br_attn), ("w_out", w_out, m_w_out, v_w_out))):
        big[nm] = _adamw_halves(core, w[0], mine[t], theirs[t], m[0], v[0], "adamw_" + nm)

    dmod_all = sm_all[:, 0:3, :].reshape(8, 3 * D)
    dm_pad = jnp.pad(lax.dynamic_slice(dmod_all, (0, chip * 768), (8, 768)), ((0, 120), (0, 0)))
    conv_dev = lax.dynamic_slice(sm_all[:, 16:24, :], (0, 0, chip * 256), (8, 8, 256))
    sm_loc = jnp.concatenate([sm_all[:, 0:16, :], jnp.pad(conv_dev, ((0, 0), (0, 0), (0, D - 256))), sm_all[:, 24:32, :]], axis=1)

    def pack_vec(b3, nw, cw, qw, kw):
        z = jnp.zeros((32, D), F32)
        z = z.at[0:3, :].set(b3.reshape(3, D)).at[8:9, :].set(nw).at[16:19, 0:256].set(cw)
        return z.at[24:25, 0:HEAD_DIM].set(qw).at[25:26, 0:HEAD_DIM].set(kw)

    vec_w = pack_vec(b_ada, norm_w, conv_w[0], q_norm_w, k_norm_w)
    vec_m = pack_vec(m_b_ada, m_norm_w, m_conv_w[0], m_q_norm_w, m_k_norm_w)
    vec_v = pack_vec(v_b_ada, v_norm_w, v_conv_w[0], v_q_norm_w, v_k_norm_w)
    ct_pad = jnp.pad(silu_c.T, ((0, 0), (0, 120)))
    outs = _small_update(sm_loc, dm_pad, ct_pad, w_ada[0], m_w_ada[0], v_w_ada[0], vec_w, vec_m, vec_v)
    ada = outs[0:4]
    vec = outs[4:8]
    loss = vec[0][26, 0]

    def unpack(t):
        return {"b_ada": t[0:3, :].reshape(1, 3 * D), "norm_w": t[8:9, :], "conv_w": t[16:19, 0:256][None],
                "q_norm_w": t[24:25, 0:HEAD_DIM], "k_norm_w": t[25:26, 0:HEAD_DIM]}

    small = [unpack(t) for t in vec]
    order = ["w_ada", "b_ada", "norm_w", "w_in", "conv_w", "q_norm_w", "k_norm_w", "w_br_conv", "w_br_attn", "w_out"]
    res = [loss, grad_x[None]]
    for kind in range(4):
        for nm in order:
            if nm == "w_ada":
                res.append(ada[kind][None])
            elif nm in big:
                res.append(big[nm][kind][None])
            else:
                res.append(small[kind][nm])
    return tuple(res)
```

```python
import functools

import jax
import jax.numpy as jnp
from jax import lax
from jax.experimental import pallas as pl
from jax.experimental.pallas import tpu as pltpu

F32 = jnp.float32
BF16 = jnp.bfloat16
MESH = pl.DeviceIdType.MESH
ANY = pl.BlockSpec(memory_space=pl.ANY)

D_MODEL = 1024
HEAD_DIM = 64
N_GROUPS = 3
DILATIONS = (1, 4, 16)
ATTN_OUT = 512
EPS = 1e-6
NEG_INF = -1e30
NAT = 6656
NQKV = 4608
BA, CA, XA, ZA, ZB, GA, GB = 0, 1024, 2048, 3072, 4096, 4608, 5632
ATT_TILE = 2048
QK_SCALE = HEAD_DIM ** -0.5
LOG2E = 1.4426950408889634
LN2 = 0.6931471805599453
V7X_VMEM_BYTES = 64 * 1024 * 1024
V7X_VMEM_LIMIT = 56 * 1024 * 1024

ADAM_LR = 0.001
ADAM_B1 = 0.9
ADAM_B2 = 0.999
ADAM_EPS = 1e-08
ADAM_WD = 0.01
ADAM_STEP = 10


def _params(vmem=V7X_VMEM_LIMIT, **kw):
    return pltpu.CompilerParams(vmem_limit_bytes=vmem, **kw)


def _sigmoid(z):
    return 1.0 / (1.0 + jnp.exp(-z))


def _row(v, r):
    rows = lax.broadcasted_iota(jnp.int32, v.shape, 0)
    return jnp.sum(jnp.where(rows == r, v, 0.0), axis=0, keepdims=True)


def _coords():
    return lax.axis_index("x"), lax.axis_index("y"), lax.axis_index("c")


def _flip(v, bit):
    return 1 - v if bit else v


def _allgather8(blk, name):
    r, c = blk.shape

    def body(x_ref, o_ref, ssem, rsem, lsem):
        mx, my, mc = _coords()
        me = 4 * mx + 2 * my + mc
        local = pltpu.make_async_copy(x_ref, o_ref.at[me], lsem)
        local.start()
        sends = []
        for j in range(1, 8):
            peer = (_flip(mx, j & 4), _flip(my, j & 2), _flip(mc, j & 1))
            cp = pltpu.make_async_remote_copy(
                src_ref=x_ref, dst_ref=o_ref.at[me], send_sem=ssem.at[j - 1], recv_sem=rsem.at[j - 1],
                device_id=peer, device_id_type=MESH)
            cp.start()
            sends.append(cp)
        for j in range(1, 8):
            px, py, pc = _flip(mx, j & 4), _flip(my, j & 2), _flip(mc, j & 1)
            pltpu.make_async_remote_copy(
                src_ref=x_ref, dst_ref=o_ref.at[4 * px + 2 * py + pc], send_sem=ssem.at[j - 1],
                recv_sem=rsem.at[j - 1], device_id=(px, py, pc), device_id_type=MESH).wait_recv()
        for cp in sends:
            cp.wait_send()
        local.wait()

    return pl.pallas_call(
        body, name=name,
        out_shape=jax.ShapeDtypeStruct((8, r, c), blk.dtype),
        in_specs=[pl.BlockSpec(memory_space=pltpu.VMEM)],
        out_specs=pl.BlockSpec(memory_space=pltpu.VMEM),
        scratch_shapes=[pltpu.SemaphoreType.DMA((7,)), pltpu.SemaphoreType.DMA((7,)), pltpu.SemaphoreType.DMA(())],
    )(blk)


def _allgather_chips(blk, name):
    r, c = blk.shape

    def body(x_ref, o_ref, ssem, rsem, lsem):
        mx, my, mc = _coords()
        me = 2 * mx + my
        local = pltpu.make_async_copy(x_ref, o_ref.at[me], lsem)
        local.start()
        sends = []
        for j in range(1, 4):
            peer = (_flip(mx, j & 2), _flip(my, j & 1), mc)
            cp = pltpu.make_async_remote_copy(
                src_ref=x_ref, dst_ref=o_ref.at[me], send_sem=ssem.at[j - 1], recv_sem=rsem.at[j - 1],
                device_id=peer, device_id_type=MESH)
            cp.start()
            sends.append(cp)
        for j in range(1, 4):
            px, py = _flip(mx, j & 2), _flip(my, j & 1)
            pltpu.make_async_remote_copy(
                src_ref=x_ref, dst_ref=o_ref.at[2 * px + py], send_sem=ssem.at[j - 1],
                recv_sem=rsem.at[j - 1], device_id=(px, py, mc), device_id_type=MESH).wait_recv()
        for cp in sends:
            cp.wait_send()
        local.wait()

    return pl.pallas_call(
        body, name=name,
        out_shape=jax.ShapeDtypeStruct((4, r, c), blk.dtype),
        in_specs=[pl.BlockSpec(memory_space=pltpu.VMEM)],
        out_specs=pl.BlockSpec(memory_space=pltpu.VMEM),
        scratch_shapes=[pltpu.SemaphoreType.DMA((3,)), pltpu.SemaphoreType.DMA((3,)), pltpu.SemaphoreType.DMA(())],
    )(blk)


def _weights_allgather(shards):
    n = len(shards)

    def body(*refs):
        ins, outs = refs[:n], refs[n:2 * n]
        ssem, rsem = refs[2 * n:]
        mx, my, mc = _coords()
        me = 2 * mx + my
        sib = (mx, my, 1 - mc)
        chips = [(_flip(mx, j & 2), _flip(my, j & 1)) for j in range(1, 4)]
        sends = []
        for t in range(n):
            for j, (px, py) in enumerate(chips):
                cp = pltpu.make_async_remote_copy(
                    src_ref=ins[t].at[mc], dst_ref=outs[t].at[me, mc], send_sem=ssem.at[t, j],
                    recv_sem=rsem.at[t, j], device_id=(px, py, mc), device_id_type=MESH)
                cp.start()
                sends.append(cp)
        for t in range(n):
            for j, (px, py) in enumerate(chips):
                src = 2 * px + py
                pltpu.make_async_remote_copy(
                    src_ref=ins[t].at[mc], dst_ref=outs[t].at[src, mc], send_sem=ssem.at[t, j],
                    recv_sem=rsem.at[t, j], device_id=(px, py, mc), device_id_type=MESH).wait_recv()
                fwd = pltpu.make_async_remote_copy(
                    src_ref=outs[t].at[src, mc], dst_ref=outs[t].at[src, mc], send_sem=ssem.at[t, 3 + j],
                    recv_sem=rsem.at[t, 3 + j], device_id=sib, device_id_type=MESH)
                fwd.start()
                sends.append(fwd)
        for t in range(n):
            for j, (px, py) in enumerate(chips):
                src = 2 * px + py
                pltpu.make_async_remote_copy(
                    src_ref=outs[t].at[src, 1 - mc], dst_ref=outs[t].at[src, 1 - mc], send_sem=ssem.at[t, 3 + j],
                    recv_sem=rsem.at[t, 3 + j], device_id=sib, device_id_type=MESH).wait_recv()
        for cp in sends:
            cp.wait_send()

    return pl.pallas_call(
        body, name="weights_allgather",
        out_shape=[jax.ShapeDtypeStruct((4,) + s.shape, s.dtype) for s in shards],
        in_specs=[ANY] * n, out_specs=[ANY] * n,
        scratch_shapes=[pltpu.SemaphoreType.DMA((n, 6)), pltpu.SemaphoreType.DMA((n, 6))],
    )(*shards)


def _pair_exchange(grads):
    n = len(grads)

    def body(*refs):
        ins, outs = refs[:n], refs[n:2 * n]
        ssem, rsem = refs[2 * n:]
        mx, my, mc = _coords()
        sib = (mx, my, 1 - mc)
        sends = []
        for t in range(n):
            cp = pltpu.make_async_remote_copy(
                src_ref=ins[t].at[1 - mc], dst_ref=outs[t], send_sem=ssem.at[t], recv_sem=rsem.at[t],
                device_id=sib, device_id_type=MESH)
            cp.start()
            sends.append(cp)
        for cp in sends:
            cp.wait_recv()
        for cp in sends:
            cp.wait_send()

    return pl.pallas_call(
        body, name="grad_pair_exchange",
        out_shape=[jax.ShapeDtypeStruct(g.shape[1:], g.dtype) for g in grads],
        in_specs=[ANY] * n, out_specs=[ANY] * n,
        scratch_shapes=[pltpu.SemaphoreType.DMA((n,)), pltpu.SemaphoreType.DMA((n,))],
    )(*grads)


def _scatter_copies(ins, outs, ssem, rsem):
    mx, my, mc = _coords()
    me = 2 * mx + my
    mine, theirs = [], []
    for t in range(len(ins)):
        for j in range(1, 4):
            px, py = _flip(mx, j & 2), _flip(my, j & 1)
            mine.append(pltpu.make_async_remote_copy(
                src_ref=ins[t].at[2 * px + py], dst_ref=outs[t].at[me], send_sem=ssem.at[t, j - 1],
                recv_sem=rsem.at[t, j - 1], device_id=(px, py, mc), device_id_type=MESH))
            theirs.append(pltpu.make_async_remote_copy(
                src_ref=ins[t].at[me], dst_ref=outs[t].at[2 * px + py], send_sem=ssem.at[t, j - 1],
                recv_sem=rsem.at[t, j - 1], device_id=(px, py, mc), device_id_type=MESH))
    return mine, theirs


def _half_exchange(halves, smalls):
    n = len(halves)

    def body(*refs):
        ins, sm_ref = refs[:n], refs[n]
        outs, all_ref = refs[n + 1:2 * n + 1], refs[2 * n + 1]
        ssem, rsem, gsem, hsem, lsem = refs[2 * n + 2:]
        mx, my, mc = _coords()
        me = 4 * mx + 2 * my + mc
        sib = (mx, my, 1 - mc)
        local = pltpu.make_async_copy(sm_ref, all_ref.at[me], lsem)
        local.start()
        sends = []
        for t in range(n):
            rc = pltpu.make_async_remote_copy(
                src_ref=ins[t], dst_ref=outs[t], send_sem=ssem.at[t], recv_sem=rsem.at[t],
                device_id=sib, device_id_type=MESH)
            rc.start()
            sends.append(rc)
        gathers = []
        for j in range(1, 8):
            peer = (_flip(mx, j & 4), _flip(my, j & 2), _flip(mc, j & 1))
            cp = pltpu.make_async_remote_copy(
                src_ref=sm_ref, dst_ref=all_ref.at[me], send_sem=gsem.at[j - 1], recv_sem=hsem.at[j - 1],
                device_id=peer, device_id_type=MESH)
            cp.start()
            gathers.append(cp)
        for cp in sends:
            cp.wait_recv()
        for j in range(1, 8):
            px, py, pc = _flip(mx, j & 4), _flip(my, j & 2), _flip(mc, j & 1)
            pltpu.make_async_remote_copy(
                src_ref=sm_ref, dst_ref=all_ref.at[4 * px + 2 * py + pc], send_sem=gsem.at[j - 1],
                recv_sem=hsem.at[j - 1], device_id=(px, py, pc), device_id_type=MESH).wait_recv()
        for cp in sends + gathers:
            cp.wait_send()
        local.wait()

    vm = pl.BlockSpec(memory_space=pltpu.VMEM)
    outs = pl.pallas_call(
        body, name="grad_half_exchange",
        out_shape=[jax.ShapeDtypeStruct(h.shape, h.dtype) for h in halves]
                  + [jax.ShapeDtypeStruct((8,) + smalls.shape, smalls.dtype)],
        in_specs=[ANY] * n + [vm], out_specs=[ANY] * n + [vm],
        scratch_shapes=[pltpu.SemaphoreType.DMA((n,)), pltpu.SemaphoreType.DMA((n,)),
                        pltpu.SemaphoreType.DMA((7,)), pltpu.SemaphoreType.DMA((7,)), pltpu.SemaphoreType.DMA(())],
    )(*halves, smalls)
    return outs[:n], outs[n]


def _row_tile(rows, cols, bytes_per_row_elem=4, budget=2 * 1024 * 1024):
    t = rows
    while t % 2 == 0 and t > 16 and t * cols * bytes_per_row_elem > budget:
        t //= 2
    return t


def _pair_add(core, g, r):
    _, _, rh, cc = g.shape
    tr = _row_tile(rh, cc)

    def body(core_ref, g_ref, r_ref, o_ref):
        o_ref[...] = (g_ref[...].astype(F32) + r_ref[...].astype(F32)).astype(o_ref.dtype)

    return pl.pallas_call(
        body, name="grad_pair_add",
        grid_spec=pltpu.PrefetchScalarGridSpec(
            num_scalar_prefetch=1, grid=(4, rh // tr),
            in_specs=[pl.BlockSpec((None, None, tr, cc), lambda k, i, c: (c[0], k, i, 0)),
                      pl.BlockSpec((None, tr, cc), lambda k, i, c: (k, i, 0))],
            out_specs=pl.BlockSpec((None, tr, cc), lambda k, i, c: (k, i, 0))),
        out_shape=jax.ShapeDtypeStruct(r.shape, BF16),
        compiler_params=_params(),
    )(core, g, r)


def _sum4(chip, p, q):
    _, rh, cc = q.shape
    tr = _row_tile(rh, cc)

    def body(chip_ref, p_ref, q1_ref, q2_ref, q3_ref, o_ref):
        o_ref[...] = ((p_ref[...].astype(F32) + q1_ref[...].astype(F32)) + q2_ref[...].astype(F32)) + q3_ref[...].astype(F32)

    def other(j):
        return pl.BlockSpec((None, tr, cc), lambda i, c: (jnp.bitwise_xor(c[0], j), i, 0))

    return pl.pallas_call(
        body, name="grad_sum4",
        grid_spec=pltpu.PrefetchScalarGridSpec(
            num_scalar_prefetch=1, grid=(rh // tr,),
            in_specs=[pl.BlockSpec((None, tr, cc), lambda i, c: (c[0], i, 0)), other(1), other(2), other(3)],
            out_specs=pl.BlockSpec((tr, cc), lambda i, c: (i, 0))),
        out_shape=jax.ShapeDtypeStruct((rh, cc), F32),
        compiler_params=_params(),
    )(chip, p, q, q, q)


def _adamw_math(w, g, m, v):
    m = ADAM_B1 * m + (1.0 - ADAM_B1) * g
    v = ADAM_B2 * v + (1.0 - ADAM_B2) * (g * g)
    m_hat = m / (1.0 - ADAM_B1 ** ADAM_STEP)
    v_hat = v / (1.0 - ADAM_B2 ** ADAM_STEP)
    delta = -ADAM_LR * (m_hat / (jnp.sqrt(v_hat) + ADAM_EPS) + ADAM_WD * w)
    return delta, m, v


def _adamw_halves(core, w, mine, recv, m, v, name):
    rows, cols = w.shape
    rh = rows // 2
    tr = _row_tile(rh, cols, budget=1024 * 1024)
    nh = rh // tr

    def body(core_ref, w_ref, a_ref, b_ref, m_ref, v_ref, g_ref, d_ref, mo_ref, vo_ref):
        g = jnp.where(pl.program_id(0) == core_ref[0], a_ref[...], b_ref[...])
        d, mn, vn = _adamw_math(w_ref[...], g, m_ref[...], v_ref[...])
        g_ref[...] = g
        d_ref[...] = d
        mo_ref[...] = mn
        vo_ref[...] = vn

    full = pl.BlockSpec((tr, cols), lambda h, i, c: (h * nh + i, 0))
    half = pl.BlockSpec((tr, cols), lambda h, i, c: (i, 0))
    sd = jax.ShapeDtypeStruct((rows, cols), F32)
    return pl.pallas_call(
        body, name=name,
        grid_spec=pltpu.PrefetchScalarGridSpec(
            num_scalar_prefetch=1, grid=(2, nh),
            in_specs=[full, half, half, full, full], out_specs=[full] * 4),
        out_shape=[sd, sd, sd, sd], compiler_params=_params(),
    )(core, w, mine, recv, m, v)


def _mod_part(c_all, w_ada, b_shard):
    cols = w_ada.shape[1]

    def body(c_ref, w_ref, b_ref, o_ref, sc_ref):
        cv = c_ref[...]
        sc = cv * _sigmoid(cv)
        sc_ref[...] = sc
        o_ref[...] = jnp.dot(sc, w_ref[...], preferred_element_type=F32,
                             precision=lax.Precision.HIGHEST) + b_ref[...]

    return pl.pallas_call(
        body, name="adaln_mod",
        out_shape=[jax.ShapeDtypeStruct((8, cols), F32), jax.ShapeDtypeStruct(c_all.shape, F32)],
        in_specs=[pl.BlockSpec(memory_space=pltpu.VMEM)] * 3,
        out_specs=[pl.BlockSpec(memory_space=pltpu.VMEM)] * 2, compiler_params=_params(),
    )(c_all, w_ada, b_shard)


def _inproj_nat(x, mod, norm_w, w, *, tm=1024, tn=1664):
    S, D = x.shape
    N = w.shape[1]
    chunk = 256

    def body(x_ref, mod_ref, nw_ref, w_ref, proj_ref, h_ref, ht_ref):
        @pl.when(pl.program_id(1) == 0)
        def _():
            shift = mod_ref[:, 0:D]
            scale1 = 1.0 + mod_ref[:, D:2 * D]
            for r in range(tm // chunk):
                xv = x_ref[pl.ds(r * chunk, chunk), :]
                ms = jnp.mean(xv * xv, axis=-1, keepdims=True)
                h = (xv * lax.rsqrt(ms + EPS) * nw_ref[...]) * scale1 + shift
                h_ref[pl.ds(r * chunk, chunk), :] = h.astype(BF16)
                ht_ref[:, pl.ds(r * chunk, chunk)] = h.T.astype(BF16)
        proj_ref[...] = jnp.dot(h_ref[...], w_ref[...], preferred_element_type=F32).astype(BF16)

    return pl.pallas_call(
        body, name="inproj_nat", grid=(S // tm, N // tn),
        in_specs=[pl.BlockSpec((tm, D), lambda i, j: (i, 0)),
                  pl.BlockSpec((1, 3 * D), lambda i, j: (0, 0)),
                  pl.BlockSpec((1, D), lambda i, j: (0, 0)),
                  pl.BlockSpec((D, tn), lambda i, j: (0, j))],
        out_specs=[pl.BlockSpec((tm, tn), lambda i, j: (i, j)),
                   pl.BlockSpec((tm, D), lambda i, j: (i, 0)),
                   pl.BlockSpec((D, tm), lambda i, j: (0, i))],
        out_shape=[jax.ShapeDtypeStruct((S, N), BF16), jax.ShapeDtypeStruct((S, D), BF16),
                   jax.ShapeDtypeStruct((D, S), BF16)],
        compiler_params=_params(),
    )(x, mod, norm_w, w)


def _matmul(a, b, *, tm, tn, name):
    M, K = a.shape
    N = b.shape[1]

    def body(a_ref, b_ref, o_ref):
        o_ref[...] = jnp.dot(a_ref[...], b_ref[...], preferred_element_type=F32).astype(o_ref.dtype)

    return pl.pallas_call(
        body, name=name, grid=(M // tm, N // tn),
        in_specs=[pl.BlockSpec((tm, K), lambda i, j: (i, 0)), pl.BlockSpec((K, tn), lambda i, j: (0, j))],
        out_specs=pl.BlockSpec((tm, tn), lambda i, j: (i, j)),
        out_shape=jax.ShapeDtypeStruct((M, N), BF16), compiler_params=_params(),
    )(a, b)


def _matmul_acc(a, b, *, tn, tk, name):
    M, K = a.shape
    N = b.shape[1]
    nk = K // tk

    def body(a_ref, b_ref, o_ref, acc_ref):
        k = pl.program_id(1)

        @pl.when(k == 0)
        def _():
            acc_ref[...] = jnp.zeros_like(acc_ref)

        acc_ref[...] += jnp.dot(a_ref[...], b_ref[...], preferred_element_type=F32)

        @pl.when(k == nk - 1)
        def _():
            o_ref[...] = acc_ref[...].astype(o_ref.dtype)

    return pl.pallas_call(
        body, name=name, grid=(N // tn, nk),
        in_specs=[pl.BlockSpec((M, tk), lambda j, k: (0, k)), pl.BlockSpec((tk, tn), lambda j, k: (k, j))],
        out_specs=pl.BlockSpec((M, tn), lambda j, k: (0, j)),
        out_shape=jax.ShapeDtypeStruct((M, N), BF16),
        scratch_shapes=[pltpu.VMEM((M, tn), F32)], compiler_params=_params(),
    )(a, b)


SHARD_COLS = 2816


def _column_tables(part):
    if part == "nat":
        bw = 256
        orig = [j * bw if j < 16 else 8704 + (j - 16) * bw for j in range(NAT // bw)]
    else:
        bw = 128
        orig = []
        for jj in range(NQKV // bw):
            g, p, t = jj // 12, (jj % 12) // 3, jj % 3
            orig.append(4096 + t * 1536 + g * 512 + p * 128)
    tk = jnp.array([o // SHARD_COLS for o in orig], jnp.int32)
    tc = jnp.array([(o % SHARD_COLS) // bw for o in orig], jnp.int32)
    return tk, tc, bw


def _weights_prep(gath, own, chipv, part):
    D = D_MODEL
    tk, tc, bw = _column_tables(part)
    n = tk.shape[0]

    def body(tk_ref, tc_ref, chip_ref, g_ref, o_ref, w_ref):
        w_ref[...] = jnp.where(tk_ref[pl.program_id(0)] == chip_ref[0], o_ref[...], g_ref[...])

    def gath_idx(j, tk, tc, ch):
        k = tk[j]
        return (jnp.where(k == ch[0], (k + 1) % 4, k), 0, tc[j])

    return pl.pallas_call(
        body, name="weights_prep_" + part,
        grid_spec=pltpu.PrefetchScalarGridSpec(
            num_scalar_prefetch=3, grid=(n,),
            in_specs=[pl.BlockSpec((None, D, bw), gath_idx),
                      pl.BlockSpec((D, bw), lambda j, tk, tc, ch: (0, tc[j]))],
            out_specs=pl.BlockSpec((D, bw), lambda j, tk, tc, ch: (0, j))),
        out_shape=jax.ShapeDtypeStruct((D, n * bw), BF16),
        compiler_params=_params(),
    )(tk, tc, chipv, gath, own)


def _pair_add_to_shards(core, gw, r, into, part):
    D = D_MODEL
    tk, tc, bw = _column_tables(part)
    n = tk.shape[0]

    def body(tk_ref, tc_ref, core_ref, g_ref, r_ref, into_ref, o_ref):
        o_ref[...] = (g_ref[...].astype(F32) + r_ref[...].astype(F32)).astype(BF16)

    return pl.pallas_call(
        body, name="grad_pair_add_" + part,
        grid_spec=pltpu.PrefetchScalarGridSpec(
            num_scalar_prefetch=3, grid=(n,),
            in_specs=[pl.BlockSpec((None, D // 2, bw), lambda j, tk, tc, c: (c[0], 0, j)),
                      pl.BlockSpec((D // 2, bw), lambda j, tk, tc, c: (0, j)), ANY],
            out_specs=pl.BlockSpec((None, D // 2, bw), lambda j, tk, tc, c: (tk[j], 0, tc[j]))),
        out_shape=jax.ShapeDtypeStruct(into.shape, BF16),
        input_output_aliases={5: 0},
        compiler_params=_params(),
    )(tk, tc, core, gw, r, into)


def _head_norm(xb, w, ebd):
    x = xb.astype(F32)
    ss = jnp.dot((x * x).astype(BF16), ebd, preferred_element_type=F32)
    return x * lax.rsqrt(ss * (1.0 / HEAD_DIM) + EPS) * w


def _window_mask(no_prev):
    qi = lax.broadcasted_iota(jnp.int32, (128, 256), 0)
    kc = lax.broadcasted_iota(jnp.int32, (128, 256), 1)
    ok = (kc >= qi) & (kc <= qi + 128)
    if no_prev is not None:
        ok = ok & ((kc >= 128) | jnp.logical_not(no_prev))
    return ok


def _lane_masks():
    lane = lax.broadcasted_iota(jnp.int32, (1, 128), 1)
    return [(lane < HEAD_DIM).astype(F32), (lane >= HEAD_DIM).astype(F32)]


def _head_col(v, mh):
    return jnp.sum(v * mh, axis=1, keepdims=True) * (1.0 / HEAD_DIM)


def _attn_fwd(projq, qnw2, knw2, ebd, g):
    S = projq.shape[0]
    d = DILATIONS[g]
    T = ATT_TILE
    nt = S // T
    nb = T // (128 * d)
    sdt = BF16 if d == 1 else F32

    def body(cur_ref, qw_ref, kw_ref, ebd_ref, o_ref, lse_ref, qs_s, ks_s, vs_s):
        t = pl.program_id(1)
        e = ebd_ref[...]

        @pl.when(t == 0)
        def _():
            ks_s[pl.ds(0, T), :] = jnp.zeros((T, 128), sdt)
            vs_s[pl.ds(0, T), :] = jnp.zeros((T, 128), sdt)

        @pl.when(t > 0)
        def _():
            ks_s[pl.ds(0, T), :] = ks_s[pl.ds(T, T), :]
            vs_s[pl.ds(0, T), :] = vs_s[pl.ds(T, T), :]

        qs_s[...] = (_head_norm(cur_ref[:, 0:128], qw_ref[...], e) * (QK_SCALE * LOG2E)).astype(sdt)
        ks_s[pl.ds(T, T), :] = _head_norm(cur_ref[:, 128:256], kw_ref[...], e).astype(sdt)
        vs_s[pl.ds(T, T), :] = cur_ref[:, 256:384].astype(sdt)
        m0, m1 = _lane_masks()
        first = t == 0
        for r in range(d):
            for bb in range(nb):
                q0 = r + bb * 128 * d
                k0 = T + r + (bb - 1) * 128 * d
                qs = qs_s[pl.ds(q0, 128, stride=d), :].astype(F32)
                kb = ks_s[pl.ds(k0, 256, stride=d), :].astype(BF16)
                vb = vs_s[pl.ds(k0, 256, stride=d), :].astype(BF16)
                ok = _window_mask(first if bb == 0 else None)
                ok2 = jnp.concatenate([ok, ok], axis=0)
                q2 = jnp.concatenate([qs * m0, qs * m1], axis=0).astype(BF16)
                s = lax.dot_general(q2, kb, (((1,), (1,)), ((), ())), preferred_element_type=F32)
                s = jnp.where(ok2, s, NEG_INF)
                mx = jnp.max(s, axis=-1, keepdims=True)
                p = jnp.exp2(s - mx)
                l = jnp.sum(p, axis=-1, keepdims=True)
                o = jnp.dot(p.astype(BF16), vb, preferred_element_type=F32) / l
                lse = mx * LN2 + jnp.log(l)
                o_ref[pl.ds(q0, 128, stride=d), :] = o[0:128] * m0 + o[128:256] * m1
                lse_ref[pl.ds(q0, 128, stride=d), :] = lse[0:128] * m0 + lse[128:256] * m1

    return pl.pallas_call(
        body, name=f"attn_fwd_g{g}", grid=(4, nt),
        in_specs=[pl.BlockSpec((T, 384), lambda p, t: (t, g * 4 + p)),
                  pl.BlockSpec((1, 128), lambda p, t: (0, 0)),
                  pl.BlockSpec((1, 128), lambda p, t: (0, 0)),
                  pl.BlockSpec((128, 128), lambda p, t: (0, 0))],
        out_specs=[pl.BlockSpec((T, 128), lambda p, t: (t, p)), pl.BlockSpec((T, 128), lambda p, t: (t, p))],
        out_shape=[jax.ShapeDtypeStruct((S, ATTN_OUT), F32), jax.ShapeDtypeStruct((S, ATTN_OUT), F32)],
        scratch_shapes=[pltpu.VMEM((T, 128), sdt), pltpu.VMEM((2 * T, 128), sdt), pltpu.VMEM((2 * T, 128), sdt)],
        compiler_params=_params(),
    )(projq, qnw2, knw2, ebd)


def _attn_bwd(projq, dprojq, d_o, lse, delta, qnw2, knw2, ebd, g):
    S = projq.shape[0]
    d = DILATIONS[g]
    T = ATT_TILE
    nt = S // T
    nb = T // (128 * d)
    sdt = F32

    def norm_bwd(raw_b, dy, w, e):
        x = raw_b.astype(F32)
        ss = jnp.dot((x * x).astype(BF16), e, preferred_element_type=F32)
        rstd = lax.rsqrt(ss * (1.0 / HEAD_DIM) + EPS)
        xh = x * rstd
        gw = jnp.sum(dy * xh, axis=0, keepdims=True)
        dxh = dy * w
        mean = jnp.dot((dxh * xh).astype(BF16), e, preferred_element_type=F32) * (1.0 / HEAD_DIM)
        return rstd * (dxh - xh * mean), gw

    nt_dims = (((1,), (1,)), ((), ()))
    tn_dims = (((0,), (0,)), ((), ()))

    def unit_stacked(qs, dos, ls, dls, kb, vb, ok, m0, m1):
        ok2 = jnp.concatenate([ok, ok], axis=0)
        q2 = jnp.concatenate([qs * m0, qs * m1], axis=0).astype(BF16)
        do2 = jnp.concatenate([dos * m0, dos * m1], axis=0).astype(BF16)
        lcol = jnp.concatenate([_head_col(ls, m0), _head_col(ls, m1)], axis=0)
        dcol = jnp.concatenate([_head_col(dls, m0), _head_col(dls, m1)], axis=0)
        s = jnp.where(ok2, lax.dot_general(q2, kb, nt_dims, preferred_element_type=F32), NEG_INF)
        p = jnp.exp2(s - lcol * LOG2E)
        dp = lax.dot_general(do2, vb, nt_dims, preferred_element_type=F32)
        ds = (p * (dp - dcol)).astype(BF16)
        dq2 = jnp.dot(ds, kb, preferred_element_type=F32)
        dk = lax.dot_general(ds, q2, tn_dims, preferred_element_type=F32)
        dv = lax.dot_general(p.astype(BF16), do2, tn_dims, preferred_element_type=F32)
        return dq2[0:128] * m0 + dq2[128:256] * m1, dk, dv

    def unit_per_head(qs, dos, ls, dls, kb, vb, ok, m0, m1):
        dq_t = dk_t = dv_t = None
        for mh in (m0, m1):
            qh = (qs * mh).astype(BF16)
            doh = (dos * mh).astype(BF16)
            s = jnp.where(ok, lax.dot_general(qh, kb, nt_dims, preferred_element_type=F32), NEG_INF)
            p = jnp.exp2(s - _head_col(ls, mh) * LOG2E)
            dp = lax.dot_general(doh, vb, nt_dims, preferred_element_type=F32)
            ds = (p * (dp - _head_col(dls, mh))).astype(BF16)
            dq = jnp.dot(ds, kb, preferred_element_type=F32) * mh
            dk = lax.dot_general(ds, qh, tn_dims, preferred_element_type=F32)
            dv = lax.dot_general(p.astype(BF16), doh, tn_dims, preferred_element_type=F32)
            dq_t = dq if dq_t is None else dq_t + dq
            dk_t = dk if dk_t is None else dk_t + dk
            dv_t = dv if dv_t is None else dv_t + dv
        return dq_t, dk_t, dv_t

    unit = unit_per_head if d == 1 else unit_stacked

    def body(cur_ref, prev_ref, do_ref, lse_ref, dl_ref, qw_ref, kw_ref, ebd_ref, dp_in_ref,
             out_ref, gqk_ref, qs_s, ks_s, vs_s, do_s, dq_cur, dq_prev, dk_acc, dv_acc):
        i = pl.program_id(1)
        e = ebd_ref[...]
        m0, m1 = _lane_masks()

        @pl.when(i == 0)
        def _():
            dk_acc[...] = jnp.zeros_like(dk_acc)
            dv_acc[...] = jnp.zeros_like(dv_acc)
            dq_prev[...] = jnp.zeros_like(dq_prev)
            gqk_ref[...] = jnp.zeros_like(gqk_ref)

        @pl.when(i < nt)
        def _():
            qs_s[...] = _head_norm(cur_ref[:, 0:128], qw_ref[...], e) * (QK_SCALE * LOG2E)
            ks_s[pl.ds(0, T), :] = _head_norm(prev_ref[:, 128:256], kw_ref[...], e)
            ks_s[pl.ds(T, T), :] = _head_norm(cur_ref[:, 128:256], kw_ref[...], e)
            vs_s[pl.ds(0, T), :] = prev_ref[:, 256:384].astype(F32)
            vs_s[pl.ds(T, T), :] = cur_ref[:, 256:384].astype(F32)
            do_s[...] = do_ref[...].astype(F32)
            first = i == 0
            for r in range(d):
                for bb in range(nb):
                    q0 = r + bb * 128 * d
                    k0 = T + r + (bb - 1) * 128 * d
                    qs = qs_s[pl.ds(q0, 128, stride=d), :]
                    dos = do_s[pl.ds(q0, 128, stride=d), :]
                    ls = lse_ref[pl.ds(q0, 128, stride=d), :]
                    dls = dl_ref[pl.ds(q0, 128, stride=d), :]
                    kb = ks_s[pl.ds(k0, 256, stride=d), :].astype(BF16)
                    vb = vs_s[pl.ds(k0, 256, stride=d), :].astype(BF16)
                    ok = _window_mask(first if bb == 0 else None)
                    dq_t, dk_t, dv_t = unit(qs, dos, ls, dls, kb, vb, ok, m0, m1)
                    dq_cur[pl.ds(q0, 128, stride=d), :] = dq_t
                    dk_acc[pl.ds(k0, 256, stride=d), :] = dk_acc[pl.ds(k0, 256, stride=d), :] + dk_t
                    dv_acc[pl.ds(k0, 256, stride=d), :] = dv_acc[pl.ds(k0, 256, stride=d), :] + dv_t

        @pl.when(i >= 1)
        def _():
            dq_raw, gq = norm_bwd(prev_ref[:, 0:128], dq_prev[...] * QK_SCALE, qw_ref[...], e)
            dk_raw, gk = norm_bwd(prev_ref[:, 128:256], dk_acc[pl.ds(0, T), :] * LN2, kw_ref[...], e)
            out_ref[:, 0:128] = dq_raw.astype(BF16)
            out_ref[:, 128:256] = dk_raw.astype(BF16)
            out_ref[:, 256:384] = dv_acc[pl.ds(0, T), :].astype(BF16)
            gqk_ref[0:1, :] += gq
            gqk_ref[1:2, :] += gk

        dq_prev[...] = dq_cur[...]
        dk_acc[pl.ds(0, T), :] = dk_acc[pl.ds(T, T), :]
        dv_acc[pl.ds(0, T), :] = dv_acc[pl.ds(T, T), :]
        dk_acc[pl.ds(T, T), :] = jnp.zeros((T, 128), F32)
        dv_acc[pl.ds(T, T), :] = jnp.zeros((T, 128), F32)

    last = nt - 1
    qtile = lambda p, i: (jnp.minimum(i, last), p)
    return pl.pallas_call(
        body, name=f"attn_bwd_g{g}", grid=(4, nt + 1),
        in_specs=[pl.BlockSpec((T, 384), lambda p, i: (jnp.minimum(i, last), g * 4 + p)),
                  pl.BlockSpec((T, 384), lambda p, i: (jnp.maximum(i - 1, 0), g * 4 + p)),
                  pl.BlockSpec((T, 128), qtile), pl.BlockSpec((T, 128), qtile), pl.BlockSpec((T, 128), qtile),
                  pl.BlockSpec((1, 128), lambda p, i: (0, 0)),
                  pl.BlockSpec((1, 128), lambda p, i: (0, 0)),
                  pl.BlockSpec((128, 128), lambda p, i: (0, 0)),
                  ANY],
        out_specs=[pl.BlockSpec((T, 384), lambda p, i: (jnp.maximum(i - 1, 0), g * 4 + p)),
                   pl.BlockSpec((None, 8, 128), lambda p, i: (p, 0, 0))],
        out_shape=[jax.ShapeDtypeStruct(dprojq.shape, BF16), jax.ShapeDtypeStruct((4, 8, 128), F32)],
        scratch_shapes=[pltpu.VMEM((T, 128), sdt), pltpu.VMEM((2 * T, 128), sdt), pltpu.VMEM((2 * T, 128), sdt),
                        pltpu.VMEM((T, 128), sdt), pltpu.VMEM((T, 128), F32), pltpu.VMEM((T, 128), F32),
                        pltpu.VMEM((2 * T, 128), F32), pltpu.VMEM((2 * T, 128), F32)],
        input_output_aliases={8: 0},
        compiler_params=_params(),
    )(projq, projq, d_o, lse, delta, qnw2, knw2, ebd, dprojq)


def _mid(projn, o_g, lse_g, x, tgt, mod, convw8, wc, wa, wo, ebd, *, tm=256):
    S, D = x.shape
    nt = S // tm
    nst = max(1, 256 // tm)
    assert nt % nst == 0
    hb = tm // 16

    def body(pn_ref, hc_ref, hx_ref, o0_ref, o1_ref, o2_ref, l0_ref, l1_ref, l2_ref, x_ref, t_ref, mod_ref,
             cw_ref, ebd_ref, wc_hbm, wa_hbm, wo_hbm,
             dpn_ref, do_ref, lse_ref, dl_ref, dout_ref, vacc_ref, gwo_hbm, gwc_hbm, gwa_hbm,
             wc_s, wa_s, wo_s, gwo_s, gwc_s, gwa_s, carry_s, *stashes):
        i = pl.program_id(0)
        ti = nt - 1 - i
        par = i % nst
        stash = pl.ds(pl.multiple_of(par * tm, tm), tm)

        @pl.when(i == 0)
        def _():
            for src, dst in ((wc_hbm, wc_s), (wa_hbm, wa_s), (wo_hbm, wo_s)):
                pltpu.sync_copy(src, dst)
            gwo_s[...] = jnp.zeros_like(gwo_s)
            gwc_s[...] = jnp.zeros_like(gwc_s)
            gwa_s[...] = jnp.zeros_like(gwa_s)
            carry_s[...] = jnp.zeros_like(carry_s)
            vacc_ref[...] = jnp.zeros_like(vacc_ref)

        rows = lax.broadcasted_iota(jnp.int32, (tm, D), 0)
        w0, w1, w2 = cw_ref[0:1, :], cw_ref[1:2, :], cw_ref[2:3, :]
        gate = mod_ref[:, 2 * D:3 * D]

        b_a = pn_ref[:, BA:BA + D].astype(F32)
        c_a = pn_ref[:, CA:CA + D].astype(F32)
        x_a = pn_ref[:, XA:XA + D].astype(F32)
        z_a = pn_ref[:, ZA:ZA + D].astype(F32)
        u = c_a * x_a
        has_prev = (ti > 0).astype(F32)
        uh = hc_ref[...].astype(F32) * hx_ref[...].astype(F32) * has_prev
        h14, h15 = _row(uh, 14), _row(uh, 15)
        u_m1 = jnp.where(rows == 0, h15, pltpu.roll(u, 1, 0))
        u_m2 = jnp.where(rows == 0, h14, jnp.where(rows == 1, h15, pltpu.roll(u, 2, 0)))
        conv = w0 * u_m2 + w1 * u_m1 + w2 * u
        sg_a = _sigmoid(z_a)
        sz_a = z_a * sg_a
        y_a = b_a * conv * sz_a
        y_ab = y_a.astype(BF16)
        pa = jnp.dot(y_ab, wc_s[...], preferred_element_type=F32)

        l0, l1, l2 = l0_ref[...], l1_ref[...], l2_ref[...]
        mxl = jnp.maximum(jnp.maximum(l0, l1), l2)
        e0, e1, e2 = jnp.exp(l0 - mxl), jnp.exp(l1 - mxl), jnp.exp(l2 - mxl)
        den = e0 + e1 + e2
        attn = (e0 * o0_ref[...] + e1 * o1_ref[...] + e2 * o2_ref[...]) / den
        lse_ref[...] = mxl + jnp.log(den)
        z_b = pn_ref[:, ZB:ZB + ATTN_OUT].astype(F32)
        sg_b = _sigmoid(z_b)
        sz_b = z_b * sg_b
        y_b = attn * sz_b
        y_bb = y_b.astype(BF16)
        pb = jnp.dot(y_bb, wa_s[...], preferred_element_type=F32)

        s_a = _sigmoid(pn_ref[:, GA:GA + D].astype(F32))
        s_b = _sigmoid(pn_ref[:, GB:GB + D].astype(F32))
        merged = s_a * pa + s_b * pb
        merged_b = merged.astype(BF16)
        mo = jnp.dot(merged_b, wo_s[...], preferred_element_type=F32)
        diff = x_ref[...] + gate * mo - t_ref[...]
        dout = diff * (1.0 / D)
        dout_ref[...] = dout
        vacc_ref[4:5, :] += jnp.sum(diff * diff, axis=0, keepdims=True)
        vacc_ref[0:1, :] += jnp.sum(dout * mo, axis=0, keepdims=True)

        d_mo = (dout * gate).astype(BF16)
        nt_dims = (((1,), (1,)), ((), ()))
        dmerged = lax.dot_general(d_mo, wo_s[...], nt_dims, preferred_element_type=F32)
        dpa = (dmerged * s_a).astype(BF16)
        dpb = (dmerged * s_b).astype(BF16)
        dpn_ref[:, GA:GA + D] = (dmerged * pa * s_a * (1.0 - s_a)).astype(BF16)
        dpn_ref[:, GB:GB + D] = (dmerged * pb * s_b * (1.0 - s_b)).astype(BF16)
        tn = (((0,), (0,)), ((), ()))
        if nst == 1:
            gwo_s[...] += lax.dot_general(merged_b, d_mo, tn, preferred_element_type=F32)
            gwc_s[...] += lax.dot_general(y_ab, dpa, tn, preferred_element_type=F32)
            gwa_s[...] += lax.dot_general(y_bb, dpb, tn, preferred_element_type=F32)
        else:
            st_m, st_d, st_ya, st_pa, st_yb, st_pb = stashes
            st_m[stash, :] = merged_b
            st_d[stash, :] = d_mo
            st_ya[stash, :] = y_ab
            st_pa[stash, :] = dpa
            st_yb[stash, :] = y_bb
            st_pb[stash, :] = dpb

            @pl.when(par == nst - 1)
            def _():
                gwo_s[...] += lax.dot_general(st_m[...], st_d[...], tn, preferred_element_type=F32)
                gwc_s[...] += lax.dot_general(st_ya[...], st_pa[...], tn, preferred_element_type=F32)
                gwa_s[...] += lax.dot_general(st_yb[...], st_pb[...], tn, preferred_element_type=F32)

        dy_a = lax.dot_general(dpa, wc_s[...], nt_dims, preferred_element_type=F32)
        dy_b = lax.dot_general(dpb, wa_s[...], nt_dims, preferred_element_type=F32)

        dattn = dy_b * sz_b
        dpn_ref[:, ZB:ZB + ATTN_OUT] = (dy_b * attn * sg_b * (1.0 + z_b * (1.0 - sg_b))).astype(BF16)
        do_ref[...] = dattn.astype(BF16)
        dl_ref[...] = jnp.dot((dattn * attn).astype(BF16), ebd_ref[...], preferred_element_type=F32)

        dpn_ref[:, BA:BA + D] = (dy_a * conv * sz_a).astype(BF16)
        dpn_ref[:, ZA:ZA + D] = (dy_a * b_a * conv * sg_a * (1.0 + z_a * (1.0 - sg_a))).astype(BF16)
        dconv = dy_a * b_a * sz_a
        vacc_ref[1:2, :] += jnp.sum(dconv * u_m2, axis=0, keepdims=True)
        vacc_ref[2:3, :] += jnp.sum(dconv * u_m1, axis=0, keepdims=True)
        vacc_ref[3:4, :] += jnp.sum(dconv * u, axis=0, keepdims=True)
        nxt = carry_s[...]
        n0, n1 = _row(nxt, 0), _row(nxt, 1)
        dc_p1 = jnp.where(rows == tm - 1, n0, pltpu.roll(dconv, tm - 1, 0))
        dc_p2 = jnp.where(rows == tm - 1, n1, jnp.where(rows == tm - 2, n0, pltpu.roll(dconv, tm - 2, 0)))
        du = w2 * dconv + w1 * dc_p1 + w0 * dc_p2
        carry_s[...] = dconv[0:8, :]
        dpn_ref[:, CA:CA + D] = (du * x_a).astype(BF16)
        dpn_ref[:, XA:XA + D] = (du * c_a).astype(BF16)

        @pl.when(i == nt - 1)
        def _():
            pltpu.sync_copy(gwo_s, gwo_hbm)
            pltpu.sync_copy(gwc_s, gwc_hbm)
            pltpu.sync_copy(gwa_s, gwa_hbm)

    rev = lambda i: (nt - 1 - i, 0)
    halo = lambda col: pl.BlockSpec((16, D), lambda i: (jnp.maximum((nt - 1 - i) * hb - 1, 0), col))
    tile512 = pl.BlockSpec((tm, ATTN_OUT), rev)
    tileD = pl.BlockSpec((tm, D), rev)
    const = lambda shape: pl.BlockSpec(shape, lambda i: (0, 0))
    return pl.pallas_call(
        body, name="mid_fwd_bwd", grid=(nt,),
        in_specs=[pl.BlockSpec((tm, NAT), rev), halo(CA // D), halo(XA // D)] + [tile512] * 6
                 + [tileD, tileD, const((1, 3 * D)), const((8, D)), const((ATTN_OUT, ATTN_OUT))] + [ANY] * 3,
        out_specs=[pl.BlockSpec((tm, NAT), rev), tile512, tile512, tile512, tileD, const((8, D)), ANY, ANY, ANY],
        out_shape=[jax.ShapeDtypeStruct((S, NAT), BF16), jax.ShapeDtypeStruct((S, ATTN_OUT), BF16),
                   jax.ShapeDtypeStruct((S, ATTN_OUT), F32), jax.ShapeDtypeStruct((S, ATTN_OUT), F32),
                   jax.ShapeDtypeStruct((S, D), F32), jax.ShapeDtypeStruct((8, D), F32),
                   jax.ShapeDtypeStruct((D, D), F32), jax.ShapeDtypeStruct((D, D), F32),
                   jax.ShapeDtypeStruct((ATTN_OUT, D), F32)],
        scratch_shapes=[pltpu.VMEM((D, D), BF16), pltpu.VMEM((ATTN_OUT, D), BF16), pltpu.VMEM((D, D), BF16),
                        pltpu.VMEM((D, D), F32), pltpu.VMEM((D, D), F32), pltpu.VMEM((ATTN_OUT, D), F32),
                        pltpu.VMEM((8, D), F32)]
                       + ([pltpu.VMEM((nst * tm, D), BF16)] * 4
                          + [pltpu.VMEM((nst * tm, ATTN_OUT), BF16), pltpu.VMEM((nst * tm, D), BF16)] if nst > 1 else []),
        compiler_params=_params(vmem=60 * 1024 * 1024),
    )(projn, projn, projn, *o_g, *lse_g, x, tgt, mod, convw8, ebd, wc, wa, wo)


def _dh_matmul(dpn, dpq, w_nat, w_qkv, parts, *, tm=512):
    S = dpn.shape[0]
    D = D_MODEL
    n = len(parts)
    nt = S // tm

    def body(dn_ref, dq_ref, wn_hbm, wq_hbm, *rest):
        p_hbm, rest = rest[:n], rest[n:]
        dh_ref = rest[0]
        q_hbm = rest[1:1 + n]
        wn_s, wq_s = rest[1 + n:3 + n]
        sems = rest[3 + n:]

        @pl.when(pl.program_id(0) == 0)
        def _():
            if n:
                mine, _ = _scatter_copies(p_hbm, q_hbm, *sems)
                for cp in mine:
                    cp.start()
            pltpu.sync_copy(wn_hbm, wn_s)
            pltpu.sync_copy(wq_hbm, wq_s)

        if n:
            @pl.when(pl.program_id(0) == nt - 1)
            def _():
                mine, theirs = _scatter_copies(p_hbm, q_hbm, *sems)
                for cp in theirs:
                    cp.wait_recv()
                for cp in mine:
                    cp.wait_send()

        nt_dims = (((1,), (1,)), ((), ()))
        dh = (lax.dot_general(dn_ref[...], wn_s[...], nt_dims, preferred_element_type=F32)
              + lax.dot_general(dq_ref[...], wq_s[...], nt_dims, preferred_element_type=F32))
        dh_ref[...] = dh.astype(BF16)

    row = lambda w: pl.BlockSpec((tm, w), lambda i: (i, 0))
    outs = pl.pallas_call(
        body, name="dh_matmul", grid=(nt,),
        in_specs=[row(NAT), row(NQKV), ANY, ANY] + [ANY] * n,
        out_specs=[row(D)] + [ANY] * n,
        out_shape=[jax.ShapeDtypeStruct((S, D), BF16)] + [jax.ShapeDtypeStruct(p.shape, p.dtype) for p in parts],
        scratch_shapes=[pltpu.VMEM((D, NAT), BF16), pltpu.VMEM((D, NQKV), BF16)]
                       + ([pltpu.SemaphoreType.DMA((n, 3)), pltpu.SemaphoreType.DMA((n, 3))] if n else []),
        compiler_params=_params(vmem=60 * 1024 * 1024),
    )(dpn, dpq, w_nat, w_qkv, *parts)
    return outs[0], outs[1:]


def _norm_bwd(dh, x, dout, mod, norm_w, *, tm=512):
    S, D = x.shape

    def body(dh_ref, x_ref, dout_ref, mod_ref, nw_ref, gx_ref, st_ref):
        @pl.when(pl.program_id(0) == 0)
        def _():
            st_ref[...] = jnp.zeros_like(st_ref)

        dh = dh_ref[...].astype(F32)
        scale1 = 1.0 + mod_ref[:, D:2 * D]
        nw = nw_ref[...]
        xv = x_ref[...]
        rstd = lax.rsqrt(jnp.mean(xv * xv, axis=-1, keepdims=True) + EPS)
        xh = xv * rstd
        dhs = dh * scale1
        dxh = dhs * nw
        dx = rstd * (dxh - xh * jnp.mean(dxh * xh, axis=-1, keepdims=True))
        gx_ref[...] = dout_ref[...] + dx
        st_ref[0:1, :] += jnp.sum(dh, axis=0, keepdims=True)
        st_ref[1:2, :] += jnp.sum(dh * (xh * nw), axis=0, keepdims=True)
        st_ref[2:3, :] += jnp.sum(dhs * xh, axis=0, keepdims=True)

    row = pl.BlockSpec((tm, D), lambda i: (i, 0))
    return pl.pallas_call(
        body, name="norm_bwd", grid=(S // tm,),
        in_specs=[row, row, row, pl.BlockSpec((1, 3 * D), lambda i: (0, 0)), pl.BlockSpec((1, D), lambda i: (0, 0))],
        out_specs=[row, pl.BlockSpec((8, D), lambda i: (0, 0))],
        out_shape=[jax.ShapeDtypeStruct((S, D), F32), jax.ShapeDtypeStruct((8, D), F32)],
        compiler_params=_params(),
    )(dh, x, dout, mod, norm_w)


def _pack_smalls(st, vacc, gqk):
    D = D_MODEL

    def body(st_ref, va_ref, gqk_ref, o_ref):
        o_ref[...] = jnp.zeros_like(o_ref)
        o_ref[0:2, :] = st_ref[0:2, :]
        o_ref[2:3, :] = va_ref[0:1, :]
        o_ref[8:9, :] = st_ref[2:3, :]
        o_ref[16:19, :] = va_ref[1:4, :]
        tot = gqk_ref[0]
        for k in range(1, gqk_ref.shape[0]):
            tot = tot + gqk_ref[k]
        tot = tot + pltpu.roll(tot, HEAD_DIM, 1)
        o_ref[24:32, 0:128] = tot
        loss = jnp.sum(va_ref[4:5, :], axis=1, keepdims=True) * (0.5 / D)
        o_ref[26:27, :] = jnp.broadcast_to(loss, (1, D))

    return pl.pallas_call(
        body, name="pack_smalls", out_shape=jax.ShapeDtypeStruct((32, D), F32),
        in_specs=[pl.BlockSpec(memory_space=pltpu.VMEM)] * 3,
        out_specs=pl.BlockSpec(memory_space=pltpu.VMEM), compiler_params=_params(),
    )(st, vacc, gqk)


def _small_update(sm_loc, dm_pad, ct_pad, w_ada, m_ada, v_ada, vec_w, vec_m, vec_v):
    D = D_MODEL
    cols = w_ada.shape[1]

    def body(sm_ref, dm_ref, ct_ref, w_ref, m_ref, v_ref, vw_ref, vm_ref, vv_ref,
             gw_ref, dw_ref, mw_ref, vw_o_ref, gv_ref, dv_ref, mv_ref, vv_o_ref):
        g = sm_ref[0]
        for k in range(1, 8):
            g = g + sm_ref[k]
        gv_ref[...] = g
        dv, mv, vv = _adamw_math(vw_ref[...], g, vm_ref[...], vv_ref[...])
        dv_ref[...] = dv
        mv_ref[...] = mv
        vv_o_ref[...] = vv
        gw = jnp.dot(ct_ref[...], dm_ref[...], preferred_element_type=F32, precision=lax.Precision.HIGHEST)
        gw_ref[...] = gw
        dw, mw, vw = _adamw_math(w_ref[...], gw, m_ref[...], v_ref[...])
        dw_ref[...] = dw
        mw_ref[...] = mw
        vw_o_ref[...] = vw

    sw = jax.ShapeDtypeStruct((D, cols), F32)
    sv = jax.ShapeDtypeStruct((32, D), F32)
    return pl.pallas_call(
        body, name="small_update", out_shape=[sw, sw, sw, sw, sv, sv, sv, sv],
        in_specs=[pl.BlockSpec(memory_space=pltpu.VMEM)] * 9,
        out_specs=[pl.BlockSpec(memory_space=pltpu.VMEM)] * 8, compiler_params=_params(),
    )(sm_loc, dm_pad, ct_pad, w_ada, m_ada, v_ada, vec_w, vec_m, vec_v)


def _local_step(xs, tg, mod, norm_w, w_nat, w_qkv, wc, wa, wo, convw8, q_norm_w, k_norm_w):
    S = xs.shape[0]
    lane = jnp.arange(128)
    ebd128 = (lane[:, None] // HEAD_DIM == lane[None, :] // HEAD_DIM).astype(BF16)
    lane5 = jnp.arange(ATTN_OUT)
    ebd512 = (lane5[:, None] // HEAD_DIM == lane5[None, :] // HEAD_DIM).astype(BF16)
    qnw2 = jnp.tile(q_norm_w, (1, 2))
    knw2 = jnp.tile(k_norm_w, (1, 2))

    projn, h, ht = _inproj_nat(xs, mod, norm_w, w_nat)
    projq = _matmul(h, w_qkv, tm=1024, tn=1536, name="inproj_qkv")
    o_g, lse_g = [], []
    for g in range(N_GROUPS):
        o, l = _attn_fwd(projq, qnw2, knw2, ebd128, g)
        o_g.append(o)
        lse_g.append(l)

    dpn, d_o, lse, delta, dout, vacc, gwo, gwc, gwa = _mid(
        projn, o_g, lse_g, xs, tg, mod, convw8, wc, wa, wo, ebd512)
    dpq = lax.empty((S, NQKV), BF16)
    gqk = []
    for g in range(N_GROUPS):
        dpq, part = _attn_bwd(projq, dpq, d_o, lse, delta, qnw2, knw2, ebd128, g)
        gqk.append(part)
    gw_nat = _matmul_acc(ht, dpn, tn=1664, tk=2048, name="grad_w_in_nat")
    gw_qkv = _matmul_acc(ht, dpq, tn=1536, tk=2048, name="grad_w_in_qkv")
    return dpn, dpq, dout, gw_nat, gw_qkv, gwc, gwa, gwo, vacc, gqk


def kernel(x, c, w_ada, b_ada, norm_w, w_in, conv_w, q_norm_w, k_norm_w, w_br_conv, w_br_attn, w_out, loss_target, m_w_ada, m_b_ada, m_norm_w, m_w_in, m_conv_w, m_q_norm_w, m_k_norm_w, m_w_br_conv, m_w_br_attn, m_w_out, v_w_ada, v_b_ada, v_norm_w, v_w_in, v_conv_w, v_q_norm_w, v_k_norm_w, v_w_br_conv, v_w_br_attn, v_w_out):
    D = D_MODEL
    S = x.shape[1]
    xs, tg = x[0], loss_target[0]
    mx, my, mc = lax.axis_index("x"), lax.axis_index("y"), lax.axis_index("c")
    chip = 2 * mx + my
    dev = 2 * chip + mc

    c_blk = jnp.concatenate([c, jnp.pad(conv_w[0], ((0, 0), (0, D - 256))), jnp.zeros((4, D), F32)], axis=0)
    cg = _allgather8(c_blk, "allgather_c")
    c_all = cg[:, 0, :]
    convw8 = jnp.pad(cg[::2, 1:4, 0:256].transpose(1, 0, 2).reshape(3, D), ((0, 5), (0, 0)))
    b_shard = lax.dynamic_slice(b_ada, (0, chip * 768), (1, 768))
    mod_part, silu_c = _mod_part(c_all, w_ada[0], b_shard)
    mod_parts = _allgather_chips(mod_part, "allgather_mod")
    mod_all = mod_parts.transpose(1, 0, 2).reshape(8, 3 * D)
    mod = lax.dynamic_slice(mod_all, (dev, 0), (1, 3 * D))

    shards = [w_in[0].astype(BF16).reshape(2, 512, 2816), w_br_conv[0].astype(BF16).reshape(2, 128, D),
              w_br_attn[0].astype(BF16).reshape(2, 256, 256), w_out[0].astype(BF16).reshape(2, 128, D)]
    gathered_w = _weights_allgather(shards)
    g_bc, g_ba, g_wo = [lax.dynamic_update_slice(g, s[None], (chip, 0, 0, 0))
                        for g, s in zip(gathered_w[1:], shards[1:])]
    chipv = jnp.reshape(chip, (1,)).astype(jnp.int32)
    g_in = gathered_w[0].reshape(4, D, SHARD_COLS)
    own_in = shards[0].reshape(D, SHARD_COLS)
    w_nat = _weights_prep(g_in, own_in, chipv, "nat")
    w_qkv = _weights_prep(g_in, own_in, chipv, "qkv")
    wc = g_bc.reshape(D, D)
    wa = g_ba.reshape(4, ATTN_OUT, 256).transpose(1, 0, 2).reshape(ATTN_OUT, D)
    wo = g_wo.reshape(D, D)

    dpn, dpq, dout, gw_nat, gw_qkv, gwc, gwa, gwo, vacc, gqk = _local_step(
        xs, tg, mod, norm_w, w_nat, w_qkv, wc, wa, wo, convw8, q_norm_w, k_norm_w)

    def halves(a):
        k, r, cc = a.shape
        return a.reshape(k, 2, r // 2, cc).transpose(1, 0, 2, 3)

    grads = [gw_nat.reshape(2, D // 2, NAT), gw_qkv.reshape(2, D // 2, NQKV),
             halves(gwc.astype(BF16).reshape(4, 256, D)),
             halves(gwa.astype(BF16).reshape(ATTN_OUT, 4, 256).transpose(1, 0, 2)),
             halves(gwo.astype(BF16).reshape(4, 256, D))]
    recv = _pair_exchange(grads)
    core = jnp.reshape(mc, (1,)).astype(jnp.int32)
    p_in = _pair_add_to_shards(core, grads[0], recv[0], lax.empty((4, D // 2, SHARD_COLS), BF16), "nat")
    p_in = _pair_add_to_shards(core, grads[1], recv[1], p_in, "qkv")
    parts = [p_in] + [_pair_add(core, gr, rc) for gr, rc in zip(grads[2:], recv[2:])]
    dh, gathered = _dh_matmul(dpn, dpq, w_nat, w_qkv, parts)
    grad_x, st = _norm_bwd(dh, xs, dout, mod, norm_w)
    mine = [_sum4(chipv, p, q) for p, q in zip(parts, gathered)]
    theirs, sm_all = _half_exchange(mine, _pack_smalls(st, vacc, jnp.concatenate(gqk, axis=0)))

    big = {}
    for t, (nm, w, m, v) in enumerate((("w_in", w_in, m_w_in, v_w_in), ("w_br_conv", w_br_conv, m_w_br_conv, v_w_br_conv),
                                       ("w_br_attn", w_br_attn, m_w_br_attn, v_w_br_attn), ("w_out", w_out, m_w_out, v_w_out))):
        big[nm] = _adamw_halves(core, w[0], mine[t], theirs[t], m[0], v[0], "adamw_" + nm)

    dmod_all = sm_all[:, 0:3, :].reshape(8, 3 * D)
    dm_pad = jnp.pad(lax.dynamic_slice(dmod_all, (0, chip * 768), (8, 768)), ((0, 120), (0, 0)))
    conv_dev = lax.dynamic_slice(sm_all[:, 16:24, :], (0, 0, chip * 256), (8, 8, 256))
    sm_loc = jnp.concatenate([sm_all[:, 0:16, :], jnp.pad(conv_dev, ((0, 0), (0, 0), (0, D - 256))), sm_all[:, 24:32, :]], axis=1)

    def pack_vec(b3, nw, cw, qw, kw):
        z = jnp.zeros((32, D), F32)
        z = z.at[0:3, :].set(b3.reshape(3, D)).at[8:9, :].set(nw).at[16:19, 0:256].set(cw)
        return z.at[24:25, 0:HEAD_DIM].set(qw).at[25:26, 0:HEAD_DIM].set(kw)

    vec_w = pack_vec(b_ada, norm_w, conv_w[0], q_norm_w, k_norm_w)
    vec_m = pack_vec(m_b_ada, m_norm_w, m_conv_w[0], m_q_norm_w, m_k_norm_w)
    vec_v = pack_vec(v_b_ada, v_norm_w, v_conv_w[0], v_q_norm_w, v_k_norm_w)
    ct_pad = jnp.pad(silu_c.T, ((0, 0), (0, 120)))
    outs = _small_update(sm_loc, dm_pad, ct_pad, w_ada[0], m_w_ada[0], v_w_ada[0], vec_w, vec_m, vec_v)
    ada = outs[0:4]
    vec = outs[4:8]
    loss = vec[0][26, 0]

    def unpack(t):
        return {"b_ada": t[0:3, :].reshape(1, 3 * D), "norm_w": t[8:9, :], "conv_w": t[16:19, 0:256][None],
                "q_norm_w": t[24:25, 0:HEAD_DIM], "k_norm_w": t[25:26, 0:HEAD_DIM]}

    small = [unpack(t) for t in vec]
    order = ["w_ada", "b_ada", "norm_w", "w_in", "conv_w", "q_norm_w", "k_norm_w", "w_br_conv", "w_br_attn", "w_out"]
    res = [loss, grad_x[None]]
    for kind in range(4):
        for nm in order:
            if nm == "w_ada":
                res.append(ada[kind][None])
            elif nm in big:
                res.append(big[nm][kind][None])
            else:
                res.append(small[kind][nm])
    return tuple(res)
```

```python
import functools

import jax
import jax.numpy as jnp
from jax import lax
from jax.experimental import pallas as pl
from jax.experimental.pallas import tpu as pltpu

F32 = jnp.float32
BF16 = jnp.bfloat16
MESH = pl.DeviceIdType.MESH
ANY = pl.BlockSpec(memory_space=pl.ANY)

D_MODEL = 1024
HEAD_DIM = 64
N_GROUPS = 3
DILATIONS = (1, 4, 16)
ATTN_OUT = 512
EPS = 1e-6
NEG_INF = -1e30
NAT = 6656
NQKV = 4608
BA, CA, XA, ZA, ZB, GA, GB = 0, 1024, 2048, 3072, 4096, 4608, 5632
ATT_TILE = 2048
QK_SCALE = HEAD_DIM ** -0.5
LOG2E = 1.4426950408889634
LN2 = 0.6931471805599453
V7X_VMEM_BYTES = 64 * 1024 * 1024
V7X_VMEM_LIMIT = 56 * 1024 * 1024

ADAM_LR = 0.001
ADAM_B1 = 0.9
ADAM_B2 = 0.999
ADAM_EPS = 1e-08
ADAM_WD = 0.01
ADAM_STEP = 10


def _params(vmem=V7X_VMEM_LIMIT, **kw):
    return pltpu.CompilerParams(vmem_limit_bytes=vmem, **kw)


def _sigmoid(z):
    return 1.0 / (1.0 + jnp.exp(-z))


def _row(v, r):
    rows = lax.broadcasted_iota(jnp.int32, v.shape, 0)
    return jnp.sum(jnp.where(rows == r, v, 0.0), axis=0, keepdims=True)


def _coords():
    return lax.axis_index("x"), lax.axis_index("y"), lax.axis_index("c")


def _flip(v, bit):
    return 1 - v if bit else v


def _allgather8(blk, name):
    r, c = blk.shape

    def body(x_ref, o_ref, ssem, rsem, lsem):
        mx, my, mc = _coords()
        me = 4 * mx + 2 * my + mc
        local = pltpu.make_async_copy(x_ref, o_ref.at[me], lsem)
        local.start()
        sends = []
        for j in range(1, 8):
            peer = (_flip(mx, j & 4), _flip(my, j & 2), _flip(mc, j & 1))
            cp = pltpu.make_async_remote_copy(
                src_ref=x_ref, dst_ref=o_ref.at[me], send_sem=ssem.at[j - 1], recv_sem=rsem.at[j - 1],
                device_id=peer, device_id_type=MESH)
            cp.start()
            sends.append(cp)
        for j in range(1, 8):
            px, py, pc = _flip(mx, j & 4), _flip(my, j & 2), _flip(mc, j & 1)
            pltpu.make_async_remote_copy(
                src_ref=x_ref, dst_ref=o_ref.at[4 * px + 2 * py + pc], send_sem=ssem.at[j - 1],
                recv_sem=rsem.at[j - 1], device_id=(px, py, pc), device_id_type=MESH).wait_recv()
        for cp in sends:
            cp.wait_send()
        local.wait()

    return pl.pallas_call(
        body, name=name,
        out_shape=jax.ShapeDtypeStruct((8, r, c), blk.dtype),
        in_specs=[pl.BlockSpec(memory_space=pltpu.VMEM)],
        out_specs=pl.BlockSpec(memory_space=pltpu.VMEM),
        scratch_shapes=[pltpu.SemaphoreType.DMA((7,)), pltpu.SemaphoreType.DMA((7,)), pltpu.SemaphoreType.DMA(())],
    )(blk)


def _allgather_chips(blk, name):
    r, c = blk.shape

    def body(x_ref, o_ref, ssem, rsem, lsem):
        mx, my, mc = _coords()
        me = 2 * mx + my
        local = pltpu.make_async_copy(x_ref, o_ref.at[me], lsem)
        local.start()
        sends = []
        for j in range(1, 4):
            peer = (_flip(mx, j & 2), _flip(my, j & 1), mc)
            cp = pltpu.make_async_remote_copy(
                src_ref=x_ref, dst_ref=o_ref.at[me], send_sem=ssem.at[j - 1], recv_sem=rsem.at[j - 1],
                device_id=peer, device_id_type=MESH)
            cp.start()
            sends.append(cp)
        for j in range(1, 4):
            px, py = _flip(mx, j & 2), _flip(my, j & 1)
            pltpu.make_async_remote_copy(
                src_ref=x_ref, dst_ref=o_ref.at[2 * px + py], send_sem=ssem.at[j - 1],
                recv_sem=rsem.at[j - 1], device_id=(px, py, mc), device_id_type=MESH).wait_recv()
        for cp in sends:
            cp.wait_send()
        local.wait()

    return pl.pallas_call(
        body, name=name,
        out_shape=jax.ShapeDtypeStruct((4, r, c), blk.dtype),
        in_specs=[pl.BlockSpec(memory_space=pltpu.VMEM)],
        out_specs=pl.BlockSpec(memory_space=pltpu.VMEM),
        scratch_shapes=[pltpu.SemaphoreType.DMA((3,)), pltpu.SemaphoreType.DMA((3,)), pltpu.SemaphoreType.DMA(())],
    )(blk)


def _weights_allgather(big_shard, shards):
    n = len(shards)

    def body(*refs):
        big_in, ins = refs[0], refs[1:1 + n]
        big_out, outs = refs[1 + n], refs[2 + n:2 + 2 * n]
        ssem, rsem, bs, br = refs[2 + 2 * n:]
        mx, my, mc = _coords()
        me = 2 * mx + my
        sib = (mx, my, 1 - mc)
        chips = [(_flip(mx, j & 2), _flip(my, j & 1)) for j in range(1, 4)]
        sends = []

        kx, ky, kd = 2 * (1 - mx) + my, 2 * mx + (1 - my), 2 * (1 - mx) + (1 - my)
        xn, yn = (1 - mx, my, mc), (mx, 1 - my, mc)

        def big(src_chip, q, sem, to, half=None):
            h = mc if half is None else half
            return pltpu.make_async_remote_copy(
                src_ref=big_out.at[src_chip, h, q], dst_ref=big_out.at[src_chip, h, q],
                send_sem=bs.at[sem], recv_sem=br.at[sem], device_id=to, device_id_type=MESH)

        def own(q, sem, to):
            return pltpu.make_async_remote_copy(
                src_ref=big_in.at[mc, q], dst_ref=big_out.at[me, mc, q],
                send_sem=bs.at[sem], recv_sem=br.at[sem], device_id=to, device_id_type=MESH)

        for cp in (own(0, 0, xn), own(1, 3, yn), own(1, 1, xn), own(0, 2, yn)):
            cp.start()
            sends.append(cp)
        for t in range(n):
            for j, (px, py) in enumerate(chips):
                cp = pltpu.make_async_remote_copy(
                    src_ref=ins[t].at[mc], dst_ref=outs[t].at[me, mc], send_sem=ssem.at[t, j],
                    recv_sem=rsem.at[t, j], device_id=(px, py, mc), device_id_type=MESH)
                cp.start()
                sends.append(cp)

        pieces = [(kx, 0), (kx, 1), (ky, 0), (ky, 1), (kd, 0), (kd, 1)]

        def landed(piece, sem, frm, pass_on=None):
            src_chip, q = pieces[piece]
            big(src_chip, q, sem, frm).wait_recv()
            nxt = ([] if pass_on is None else [big(src_chip, q, pass_on[0], pass_on[1])])
            nxt.append(big(src_chip, q, 6 + piece, sib))
            for cp in nxt:
                cp.start()
                sends.append(cp)

        landed(0, 0, xn, pass_on=(4, yn))
        landed(3, 3, yn, pass_on=(5, xn))
        landed(1, 1, xn)
        landed(2, 2, yn)
        landed(4, 4, yn)
        landed(5, 5, xn)
        for t in range(n):
            for j, (px, py) in enumerate(chips):
                src = 2 * px + py
                pltpu.make_async_remote_copy(
                    src_ref=ins[t].at[mc], dst_ref=outs[t].at[src, mc], send_sem=ssem.at[t, j],
                    recv_sem=rsem.at[t, j], device_id=(px, py, mc), device_id_type=MESH).wait_recv()
                fwd = pltpu.make_async_remote_copy(
                    src_ref=outs[t].at[src, mc], dst_ref=outs[t].at[src, mc], send_sem=ssem.at[t, 3 + j],
                    recv_sem=rsem.at[t, 3 + j], device_id=sib, device_id_type=MESH)
                fwd.start()
                sends.append(fwd)
        for t in range(n):
            for j, (px, py) in enumerate(chips):
                src = 2 * px + py
                pltpu.make_async_remote_copy(
                    src_ref=outs[t].at[src, 1 - mc], dst_ref=outs[t].at[src, 1 - mc], send_sem=ssem.at[t, 3 + j],
                    recv_sem=rsem.at[t, 3 + j], device_id=sib, device_id_type=MESH).wait_recv()
        for i, (src_chip, q) in enumerate(pieces):
            big(src_chip, q, 6 + i, sib, half=1 - mc).wait_recv()
        for cp in sends:
            cp.wait_send()

    outs = pl.pallas_call(
        body, name="weights_allgather",
        out_shape=[jax.ShapeDtypeStruct((4,) + big_shard.shape, big_shard.dtype)]
                  + [jax.ShapeDtypeStruct((4,) + s.shape, s.dtype) for s in shards],
        in_specs=[ANY] * (n + 1), out_specs=[ANY] * (n + 1),
        scratch_shapes=[pltpu.SemaphoreType.DMA((n, 6)), pltpu.SemaphoreType.DMA((n, 6)),
                        pltpu.SemaphoreType.DMA((12,)), pltpu.SemaphoreType.DMA((12,))],
    )(big_shard, *shards)
    return outs[0], outs[1:]


def _pair_exchange(grads):
    n = len(grads)

    def body(*refs):
        ins, outs = refs[:n], refs[n:2 * n]
        ssem, rsem = refs[2 * n:]
        mx, my, mc = _coords()
        sib = (mx, my, 1 - mc)
        sends = []
        for t in range(n):
            cp = pltpu.make_async_remote_copy(
                src_ref=ins[t].at[1 - mc], dst_ref=outs[t], send_sem=ssem.at[t], recv_sem=rsem.at[t],
                device_id=sib, device_id_type=MESH)
            cp.start()
            sends.append(cp)
        for cp in sends:
            cp.wait_recv()
        for cp in sends:
            cp.wait_send()

    return pl.pallas_call(
        body, name="grad_pair_exchange",
        out_shape=[jax.ShapeDtypeStruct(g.shape[1:], g.dtype) for g in grads],
        in_specs=[ANY] * n, out_specs=[ANY] * n,
        scratch_shapes=[pltpu.SemaphoreType.DMA((n,)), pltpu.SemaphoreType.DMA((n,))],
    )(*grads)


def _scatter_copies(ins, outs, ssem, rsem):
    mx, my, mc = _coords()
    me = 2 * mx + my
    mine, theirs = [], []
    for t in range(len(ins)):
        for j in range(1, 4):
            px, py = _flip(mx, j & 2), _flip(my, j & 1)
            mine.append(pltpu.make_async_remote_copy(
                src_ref=ins[t].at[2 * px + py], dst_ref=outs[t].at[me], send_sem=ssem.at[t, j - 1],
                recv_sem=rsem.at[t, j - 1], device_id=(px, py, mc), device_id_type=MESH))
            theirs.append(pltpu.make_async_remote_copy(
                src_ref=ins[t].at[me], dst_ref=outs[t].at[2 * px + py], send_sem=ssem.at[t, j - 1],
                recv_sem=rsem.at[t, j - 1], device_id=(px, py, mc), device_id_type=MESH))
    return mine, theirs


def _half_exchange(halves, smalls):
    n = len(halves)

    def body(*refs):
        ins, sm_ref = refs[:n], refs[n]
        outs, all_ref = refs[n + 1:2 * n + 1], refs[2 * n + 1]
        ssem, rsem, gsem, hsem, lsem = refs[2 * n + 2:]
        mx, my, mc = _coords()
        me = 4 * mx + 2 * my + mc
        sib = (mx, my, 1 - mc)
        local = pltpu.make_async_copy(sm_ref, all_ref.at[me], lsem)
        local.start()
        sends = []
        for t in range(n):
            rc = pltpu.make_async_remote_copy(
                src_ref=ins[t], dst_ref=outs[t], send_sem=ssem.at[t], recv_sem=rsem.at[t],
                device_id=sib, device_id_type=MESH)
            rc.start()
            sends.append(rc)
        gathers = []
        for j in range(1, 8):
            peer = (_flip(mx, j & 4), _flip(my, j & 2), _flip(mc, j & 1))
            cp = pltpu.make_async_remote_copy(
                src_ref=sm_ref, dst_ref=all_ref.at[me], send_sem=gsem.at[j - 1], recv_sem=hsem.at[j - 1],
                device_id=peer, device_id_type=MESH)
            cp.start()
            gathers.append(cp)
        for cp in sends:
            cp.wait_recv()
        for j in range(1, 8):
            px, py, pc = _flip(mx, j & 4), _flip(my, j & 2), _flip(mc, j & 1)
            pltpu.make_async_remote_copy(
                src_ref=sm_ref, dst_ref=all_ref.at[4 * px + 2 * py + pc], send_sem=gsem.at[j - 1],
                recv_sem=hsem.at[j - 1], device_id=(px, py, pc), device_id_type=MESH).wait_recv()
        for cp in sends + gathers:
            cp.wait_send()
        local.wait()

    vm = pl.BlockSpec(memory_space=pltpu.VMEM)
    outs = pl.pallas_call(
        body, name="grad_half_exchange",
        out_shape=[jax.ShapeDtypeStruct(h.shape, h.dtype) for h in halves]
                  + [jax.ShapeDtypeStruct((8,) + smalls.shape, smalls.dtype)],
        in_specs=[ANY] * n + [vm], out_specs=[ANY] * n + [vm],
        scratch_shapes=[pltpu.SemaphoreType.DMA((n,)), pltpu.SemaphoreType.DMA((n,)),
                        pltpu.SemaphoreType.DMA((7,)), pltpu.SemaphoreType.DMA((7,)), pltpu.SemaphoreType.DMA(())],
    )(*halves, smalls)
    return outs[:n], outs[n]


def _row_tile(rows, cols, bytes_per_row_elem=4, budget=2 * 1024 * 1024):
    t = rows
    while t % 2 == 0 and t > 16 and t * cols * bytes_per_row_elem > budget:
        t //= 2
    return t


def _pair_add(core, g, r):
    _, _, rh, cc = g.shape
    tr = _row_tile(rh, cc)

    def body(core_ref, g_ref, r_ref, o_ref):
        o_ref[...] = (g_ref[...].astype(F32) + r_ref[...].astype(F32)).astype(o_ref.dtype)

    return pl.pallas_call(
        body, name="grad_pair_add",
        grid_spec=pltpu.PrefetchScalarGridSpec(
            num_scalar_prefetch=1, grid=(4, rh // tr),
            in_specs=[pl.BlockSpec((None, None, tr, cc), lambda k, i, c: (c[0], k, i, 0)),
                      pl.BlockSpec((None, tr, cc), lambda k, i, c: (k, i, 0))],
            out_specs=pl.BlockSpec((None, tr, cc), lambda k, i, c: (k, i, 0))),
        out_shape=jax.ShapeDtypeStruct(r.shape, BF16),
        compiler_params=_params(),
    )(core, g, r)


def _sum4(chip, p, q):
    _, rh, cc = q.shape
    tr = _row_tile(rh, cc)

    def body(chip_ref, p_ref, q1_ref, q2_ref, q3_ref, o_ref):
        o_ref[...] = ((p_ref[...].astype(F32) + q1_ref[...].astype(F32)) + q2_ref[...].astype(F32)) + q3_ref[...].astype(F32)

    def other(j):
        return pl.BlockSpec((None, tr, cc), lambda i, c: (jnp.bitwise_xor(c[0], j), i, 0))

    return pl.pallas_call(
        body, name="grad_sum4",
        grid_spec=pltpu.PrefetchScalarGridSpec(
            num_scalar_prefetch=1, grid=(rh // tr,),
            in_specs=[pl.BlockSpec((None, tr, cc), lambda i, c: (c[0], i, 0)), other(1), other(2), other(3)],
            out_specs=pl.BlockSpec((tr, cc), lambda i, c: (i, 0))),
        out_shape=jax.ShapeDtypeStruct((rh, cc), F32),
        compiler_params=_params(),
    )(chip, p, q, q, q)


def _adamw_math(w, g, m, v):
    m = ADAM_B1 * m + (1.0 - ADAM_B1) * g
    v = ADAM_B2 * v + (1.0 - ADAM_B2) * (g * g)
    m_hat = m / (1.0 - ADAM_B1 ** ADAM_STEP)
    v_hat = v / (1.0 - ADAM_B2 ** ADAM_STEP)
    delta = -ADAM_LR * (m_hat / (jnp.sqrt(v_hat) + ADAM_EPS) + ADAM_WD * w)
    return delta, m, v


def _adamw_halves(core, w, mine, recv, m, v, name):
    rows, cols = w.shape
    rh = rows // 2
    tr = _row_tile(rh, cols, budget=1024 * 1024)
    nh = rh // tr

    def body(core_ref, w_ref, a_ref, b_ref, m_ref, v_ref, g_ref, d_ref, mo_ref, vo_ref):
        g = jnp.where(pl.program_id(0) == core_ref[0], a_ref[...], b_ref[...])
        d, mn, vn = _adamw_math(w_ref[...], g, m_ref[...], v_ref[...])
        g_ref[...] = g
        d_ref[...] = d
        mo_ref[...] = mn
        vo_ref[...] = vn

    full = pl.BlockSpec((tr, cols), lambda h, i, c: (h * nh + i, 0))
    half = pl.BlockSpec((tr, cols), lambda h, i, c: (i, 0))
    sd = jax.ShapeDtypeStruct((rows, cols), F32)
    return pl.pallas_call(
        body, name=name,
        grid_spec=pltpu.PrefetchScalarGridSpec(
            num_scalar_prefetch=1, grid=(2, nh),
            in_specs=[full, half, half, full, full], out_specs=[full] * 4),
        out_shape=[sd, sd, sd, sd], compiler_params=_params(),
    )(core, w, mine, recv, m, v)


def _mod_part(c_all, w_ada, b_shard):
    cols = w_ada.shape[1]

    def body(c_ref, w_ref, b_ref, o_ref, sc_ref):
        cv = c_ref[...]
        sc = cv * _sigmoid(cv)
        sc_ref[...] = sc
        o_ref[...] = jnp.dot(sc, w_ref[...], preferred_element_type=F32,
                             precision=lax.Precision.HIGHEST) + b_ref[...]

    return pl.pallas_call(
        body, name="adaln_mod",
        out_shape=[jax.ShapeDtypeStruct((8, cols), F32), jax.ShapeDtypeStruct(c_all.shape, F32)],
        in_specs=[pl.BlockSpec(memory_space=pltpu.VMEM)] * 3,
        out_specs=[pl.BlockSpec(memory_space=pltpu.VMEM)] * 2, compiler_params=_params(),
    )(c_all, w_ada, b_shard)


def _inproj_nat(x, mod, norm_w, w, *, tm=1024, tn=1664):
    S, D = x.shape
    N = w.shape[1]
    chunk = 256

    def body(x_ref, mod_ref, nw_ref, w_ref, proj_ref, h_ref, ht_ref):
        @pl.when(pl.program_id(1) == 0)
        def _():
            shift = mod_ref[:, 0:D]
            scale1 = 1.0 + mod_ref[:, D:2 * D]
            for r in range(tm // chunk):
                xv = x_ref[pl.ds(r * chunk, chunk), :]
                ms = jnp.mean(xv * xv, axis=-1, keepdims=True)
                h = (xv * lax.rsqrt(ms + EPS) * nw_ref[...]) * scale1 + shift
                h_ref[pl.ds(r * chunk, chunk), :] = h.astype(BF16)
                ht_ref[:, pl.ds(r * chunk, chunk)] = h.T.astype(BF16)
        proj_ref[...] = jnp.dot(h_ref[...], w_ref[...], preferred_element_type=F32).astype(BF16)

    return pl.pallas_call(
        body, name="inproj_nat", grid=(S // tm, N // tn),
        in_specs=[pl.BlockSpec((tm, D), lambda i, j: (i, 0)),
                  pl.BlockSpec((1, 3 * D), lambda i, j: (0, 0)),
                  pl.BlockSpec((1, D), lambda i, j: (0, 0)),
                  pl.BlockSpec((D, tn), lambda i, j: (0, j))],
        out_specs=[pl.BlockSpec((tm, tn), lambda i, j: (i, j)),
                   pl.BlockSpec((tm, D), lambda i, j: (i, 0)),
                   pl.BlockSpec((D, tm), lambda i, j: (0, i))],
        out_shape=[jax.ShapeDtypeStruct((S, N), BF16), jax.ShapeDtypeStruct((S, D), BF16),
                   jax.ShapeDtypeStruct((D, S), BF16)],
        compiler_params=_params(),
    )(x, mod, norm_w, w)


def _matmul(a, b, *, tm, tn, name):
    M, K = a.shape
    N = b.shape[1]

    def body(a_ref, b_ref, o_ref):
        o_ref[...] = jnp.dot(a_ref[...], b_ref[...], preferred_element_type=F32).astype(o_ref.dtype)

    return pl.pallas_call(
        body, name=name, grid=(M // tm, N // tn),
        in_specs=[pl.BlockSpec((tm, K), lambda i, j: (i, 0)), pl.BlockSpec((K, tn), lambda i, j: (0, j))],
        out_specs=pl.BlockSpec((tm, tn), lambda i, j: (i, j)),
        out_shape=jax.ShapeDtypeStruct((M, N), BF16), compiler_params=_params(),
    )(a, b)


def _matmul_acc(a, b, *, tn, tk, name):
    M, K = a.shape
    N = b.shape[1]
    nk = K // tk

    def body(a_ref, b_ref, o_ref, acc_ref):
        k = pl.program_id(1)

        @pl.when(k == 0)
        def _():
            acc_ref[...] = jnp.zeros_like(acc_ref)

        acc_ref[...] += jnp.dot(a_ref[...], b_ref[...], preferred_element_type=F32)

        @pl.when(k == nk - 1)
        def _():
            o_ref[...] = acc_ref[...].astype(o_ref.dtype)

    return pl.pallas_call(
        body, name=name, grid=(N // tn, nk),
        in_specs=[pl.BlockSpec((M, tk), lambda j, k: (0, k)), pl.BlockSpec((tk, tn), lambda j, k: (k, j))],
        out_specs=pl.BlockSpec((M, tn), lambda j, k: (0, j)),
        out_shape=jax.ShapeDtypeStruct((M, N), BF16),
        scratch_shapes=[pltpu.VMEM((M, tn), F32)], compiler_params=_params(),
    )(a, b)


SHARD_COLS = 2816


def _column_tables(part):
    if part == "nat":
        bw = 256
        orig = [j * bw if j < 16 else 8704 + (j - 16) * bw for j in range(NAT // bw)]
    else:
        bw = 128
        orig = []
        for jj in range(NQKV // bw):
            g, p, t = jj // 12, (jj % 12) // 3, jj % 3
            orig.append(4096 + t * 1536 + g * 512 + p * 128)
    tk = jnp.array([o // SHARD_COLS for o in orig], jnp.int32)
    tc = jnp.array([(o % SHARD_COLS) // bw for o in orig], jnp.int32)
    return tk, tc, bw


def _weights_prep(gath, own, chipv, part):
    D = D_MODEL
    tk, tc, bw = _column_tables(part)
    n = tk.shape[0]
    per_step = 2
    assert n % per_step == 0

    def body(tk_ref, tc_ref, chip_ref, *refs):
        w_ref = refs[-1]
        for u in range(per_step):
            g_ref, o_ref = refs[2 * u], refs[2 * u + 1]
            mine = tk_ref[pl.program_id(0) * per_step + u] == chip_ref[0]
            w_ref[:, u * bw:(u + 1) * bw] = jnp.where(mine, o_ref[...], g_ref[...])

    def gath_spec(u):
        def idx(j, tk, tc, ch):
            k = tk[j * per_step + u]
            return (jnp.where(k == ch[0], (k + 1) % 4, k), 0, tc[j * per_step + u])
        return pl.BlockSpec((None, D, bw), idx)

    def own_spec(u):
        return pl.BlockSpec((D, bw), lambda j, tk, tc, ch: (0, tc[j * per_step + u]))

    specs = []
    for u in range(per_step):
        specs += [gath_spec(u), own_spec(u)]
    return pl.pallas_call(
        body, name="weights_prep_" + part,
        grid_spec=pltpu.PrefetchScalarGridSpec(
            num_scalar_prefetch=3, grid=(n // per_step,),
            in_specs=specs,
            out_specs=pl.BlockSpec((D, per_step * bw), lambda j, tk, tc, ch: (0, j))),
        out_shape=jax.ShapeDtypeStruct((D, n * bw), BF16),
        compiler_params=_params(),
    )(tk, tc, chipv, *([gath, own] * per_step))


def _pair_add_to_shards(core, gw, r, into, part):
    D = D_MODEL
    tk, tc, bw = _column_tables(part)
    n = tk.shape[0]

    def body(tk_ref, tc_ref, core_ref, g_ref, r_ref, into_ref, o_ref):
        o_ref[...] = (g_ref[...].astype(F32) + r_ref[...].astype(F32)).astype(BF16)

    return pl.pallas_call(
        body, name="grad_pair_add_" + part,
        grid_spec=pltpu.PrefetchScalarGridSpec(
            num_scalar_prefetch=3, grid=(n,),
            in_specs=[pl.BlockSpec((None, D // 2, bw), lambda j, tk, tc, c: (c[0], 0, j)),
                      pl.BlockSpec((D // 2, bw), lambda j, tk, tc, c: (0, j)), ANY],
            out_specs=pl.BlockSpec((None, D // 2, bw), lambda j, tk, tc, c: (tk[j], 0, tc[j]))),
        out_shape=jax.ShapeDtypeStruct(into.shape, BF16),
        input_output_aliases={5: 0},
        compiler_params=_params(),
    )(tk, tc, core, gw, r, into)


def _head_norm(xb, w, ebd):
    x = xb.astype(F32)
    ss = jnp.dot((x * x).astype(BF16), ebd, preferred_element_type=F32)
    return x * lax.rsqrt(ss * (1.0 / HEAD_DIM) + EPS) * w


def _window_mask(no_prev):
    qi = lax.broadcasted_iota(jnp.int32, (128, 256), 0)
    kc = lax.broadcasted_iota(jnp.int32, (128, 256), 1)
    ok = (kc >= qi) & (kc <= qi + 128)
    if no_prev is not None:
        ok = ok & ((kc >= 128) | jnp.logical_not(no_prev))
    return ok


def _lane_masks():
    lane = lax.broadcasted_iota(jnp.int32, (1, 128), 1)
    return [(lane < HEAD_DIM).astype(F32), (lane >= HEAD_DIM).astype(F32)]


def _head_col(v, mh):
    return jnp.sum(v * mh, axis=1, keepdims=True) * (1.0 / HEAD_DIM)


def _attn_fwd(projq, qnw2, knw2, ebd, g):
    S = projq.shape[0]
    d = DILATIONS[g]
    T = ATT_TILE
    nt = S // T
    nb = T // (128 * d)
    sdt = BF16 if d == 1 else F32

    def body(cur_ref, qw_ref, kw_ref, ebd_ref, o_ref, lse_ref, qs_s, ks_s, vs_s):
        t = pl.program_id(1)
        e = ebd_ref[...]

        @pl.when(t == 0)
        def _():
            ks_s[pl.ds(0, T), :] = jnp.zeros((T, 128), sdt)
            vs_s[pl.ds(0, T), :] = jnp.zeros((T, 128), sdt)

        @pl.when(t > 0)
        def _():
            ks_s[pl.ds(0, T), :] = ks_s[pl.ds(T, T), :]
            vs_s[pl.ds(0, T), :] = vs_s[pl.ds(T, T), :]

        qs_s[...] = (_head_norm(cur_ref[:, 0:128], qw_ref[...], e) * (QK_SCALE * LOG2E)).astype(sdt)
        ks_s[pl.ds(T, T), :] = _head_norm(cur_ref[:, 128:256], kw_ref[...], e).astype(sdt)
        vs_s[pl.ds(T, T), :] = cur_ref[:, 256:384].astype(sdt)
        m0, m1 = _lane_masks()
        first = t == 0
        for r in range(d):
            for bb in range(nb):
                q0 = r + bb * 128 * d
                k0 = T + r + (bb - 1) * 128 * d
                qs = qs_s[pl.ds(q0, 128, stride=d), :].astype(F32)
                kb = ks_s[pl.ds(k0, 256, stride=d), :].astype(BF16)
                vb = vs_s[pl.ds(k0, 256, stride=d), :].astype(BF16)
                ok = _window_mask(first if bb == 0 else None)
                ok2 = jnp.concatenate([ok, ok], axis=0)
                q2 = jnp.concatenate([qs * m0, qs * m1], axis=0).astype(BF16)
                s = lax.dot_general(q2, kb, (((1,), (1,)), ((), ())), preferred_element_type=F32)
                s = jnp.where(ok2, s, NEG_INF)
                mx = jnp.max(s, axis=-1, keepdims=True)
                p = jnp.exp2(s - mx)
                l = jnp.sum(p, axis=-1, keepdims=True)
                o = jnp.dot(p.astype(BF16), vb, preferred_element_type=F32) / l
                lse = mx * LN2 + jnp.log(l)
                o_ref[pl.ds(q0, 128, stride=d), :] = o[0:128] * m0 + o[128:256] * m1
                lse_ref[pl.ds(q0, 128, stride=d), :] = lse[0:128] * m0 + lse[128:256] * m1

    return pl.pallas_call(
        body, name=f"attn_fwd_g{g}", grid=(4, nt),
        in_specs=[pl.BlockSpec((T, 384), lambda p, t: (t, g * 4 + p)),
                  pl.BlockSpec((1, 128), lambda p, t: (0, 0)),
                  pl.BlockSpec((1, 128), lambda p, t: (0, 0)),
                  pl.BlockSpec((128, 128), lambda p, t: (0, 0))],
        out_specs=[pl.BlockSpec((T, 128), lambda p, t: (t, p)), pl.BlockSpec((T, 128), lambda p, t: (t, p))],
        out_shape=[jax.ShapeDtypeStruct((S, ATTN_OUT), F32), jax.ShapeDtypeStruct((S, ATTN_OUT), F32)],
        scratch_shapes=[pltpu.VMEM((T, 128), sdt), pltpu.VMEM((2 * T, 128), sdt), pltpu.VMEM((2 * T, 128), sdt)],
        compiler_params=_params(),
    )(projq, qnw2, knw2, ebd)


def _attn_bwd(projq, dprojq, d_o, lse, delta, qnw2, knw2, ebd, g):
    S = projq.shape[0]
    d = DILATIONS[g]
    T = ATT_TILE
    nt = S // T
    nb = T // (128 * d)
    sdt = F32

    def norm_bwd(raw_b, dy, w, e):
        x = raw_b.astype(F32)
        ss = jnp.dot((x * x).astype(BF16), e, preferred_element_type=F32)
        rstd = lax.rsqrt(ss * (1.0 / HEAD_DIM) + EPS)
        xh = x * rstd
        gw = jnp.sum(dy * xh, axis=0, keepdims=True)
        dxh = dy * w
        mean = jnp.dot((dxh * xh).astype(BF16), e, preferred_element_type=F32) * (1.0 / HEAD_DIM)
        return rstd * (dxh - xh * mean), gw

    nt_dims = (((1,), (1,)), ((), ()))
    tn_dims = (((0,), (0,)), ((), ()))

    def unit_stacked(qs, dos, ls, dls, kb, vb, ok, m0, m1):
        ok2 = jnp.concatenate([ok, ok], axis=0)
        q2 = jnp.concatenate([qs * m0, qs * m1], axis=0).astype(BF16)
        do2 = jnp.concatenate([dos * m0, dos * m1], axis=0).astype(BF16)
        lcol = jnp.concatenate([_head_col(ls, m0), _head_col(ls, m1)], axis=0)
        dcol = jnp.concatenate([_head_col(dls, m0), _head_col(dls, m1)], axis=0)
        s = jnp.where(ok2, lax.dot_general(q2, kb, nt_dims, preferred_element_type=F32), NEG_INF)
        p = jnp.exp2(s - lcol * LOG2E)
        dp = lax.dot_general(do2, vb, nt_dims, preferred_element_type=F32)
        ds = (p * (dp - dcol)).astype(BF16)
        dq2 = jnp.dot(ds, kb, preferred_element_type=F32)
        dk = lax.dot_general(ds, q2, tn_dims, preferred_element_type=F32)
        dv = lax.dot_general(p.astype(BF16), do2, tn_dims, preferred_element_type=F32)
        return dq2[0:128] * m0 + dq2[128:256] * m1, dk, dv

    def unit_per_head(qs, dos, ls, dls, kb, vb, ok, m0, m1):
        dq_t = dk_t = dv_t = None
        for mh in (m0, m1):
            qh = (qs * mh).astype(BF16)
            doh = (dos * mh).astype(BF16)
            s = jnp.where(ok, lax.dot_general(qh, kb, nt_dims, preferred_element_type=F32), NEG_INF)
            p = jnp.exp2(s - _head_col(ls, mh) * LOG2E)
            dp = lax.dot_general(doh, vb, nt_dims, preferred_element_type=F32)
            ds = (p * (dp - _head_col(dls, mh))).astype(BF16)
            dq = jnp.dot(ds, kb, preferred_element_type=F32) * mh
            dk = lax.dot_general(ds, qh, tn_dims, preferred_element_type=F32)
            dv = lax.dot_general(p.astype(BF16), doh, tn_dims, preferred_element_type=F32)
            dq_t = dq if dq_t is None else dq_t + dq
            dk_t = dk if dk_t is None else dk_t + dk
            dv_t = dv if dv_t is None else dv_t + dv
        return dq_t, dk_t, dv_t

    unit = unit_per_head if d == 1 else unit_stacked

    def body(cur_ref, prev_ref, do_ref, lse_ref, dl_ref, qw_ref, kw_ref, ebd_ref, dp_in_ref,
             out_ref, gqk_ref, qs_s, ks_s, vs_s, do_s, dq_cur, dq_prev, dk_acc, dv_acc):
        i = pl.program_id(1)
        e = ebd_ref[...]
        m0, m1 = _lane_masks()

        @pl.when(i == 0)
        def _():
            dk_acc[...] = jnp.zeros_like(dk_acc)
            dv_acc[...] = jnp.zeros_like(dv_acc)
            dq_prev[...] = jnp.zeros_like(dq_prev)
            gqk_ref[...] = jnp.zeros_like(gqk_ref)

        @pl.when(i < nt)
        def _():
            qs_s[...] = _head_norm(cur_ref[:, 0:128], qw_ref[...], e) * (QK_SCALE * LOG2E)
            ks_s[pl.ds(0, T), :] = _head_norm(prev_ref[:, 128:256], kw_ref[...], e)
            ks_s[pl.ds(T, T), :] = _head_norm(cur_ref[:, 128:256], kw_ref[...], e)
            vs_s[pl.ds(0, T), :] = prev_ref[:, 256:384].astype(F32)
            vs_s[pl.ds(T, T), :] = cur_ref[:, 256:384].astype(F32)
            do_s[...] = do_ref[...].astype(F32)
            first = i == 0
            for r in range(d):
                for bb in range(nb):
                    q0 = r + bb * 128 * d
                    k0 = T + r + (bb - 1) * 128 * d
                    qs = qs_s[pl.ds(q0, 128, stride=d), :]
                    dos = do_s[pl.ds(q0, 128, stride=d), :]
                    ls = lse_ref[pl.ds(q0, 128, stride=d), :]
                    dls = dl_ref[pl.ds(q0, 128, stride=d), :]
                    kb = ks_s[pl.ds(k0, 256, stride=d), :].astype(BF16)
                    vb = vs_s[pl.ds(k0, 256, stride=d), :].astype(BF16)
                    ok = _window_mask(first if bb == 0 else None)
                    dq_t, dk_t, dv_t = unit(qs, dos, ls, dls, kb, vb, ok, m0, m1)
                    dq_cur[pl.ds(q0, 128, stride=d), :] = dq_t
                    dk_acc[pl.ds(k0, 256, stride=d), :] = dk_acc[pl.ds(k0, 256, stride=d), :] + dk_t
                    dv_acc[pl.ds(k0, 256, stride=d), :] = dv_acc[pl.ds(k0, 256, stride=d), :] + dv_t

        @pl.when(i >= 1)
        def _():
            dq_raw, gq = norm_bwd(prev_ref[:, 0:128], dq_prev[...] * QK_SCALE, qw_ref[...], e)
            dk_raw, gk = norm_bwd(prev_ref[:, 128:256], dk_acc[pl.ds(0, T), :] * LN2, kw_ref[...], e)
            out_ref[:, 0:128] = dq_raw.astype(BF16)
            out_ref[:, 128:256] = dk_raw.astype(BF16)
            out_ref[:, 256:384] = dv_acc[pl.ds(0, T), :].astype(BF16)
            gqk_ref[0:1, :] += gq
            gqk_ref[1:2, :] += gk

        dq_prev[...] = dq_cur[...]
        dk_acc[pl.ds(0, T), :] = dk_acc[pl.ds(T, T), :]
        dv_acc[pl.ds(0, T), :] = dv_acc[pl.ds(T, T), :]
        dk_acc[pl.ds(T, T), :] = jnp.zeros((T, 128), F32)
        dv_acc[pl.ds(T, T), :] = jnp.zeros((T, 128), F32)

    last = nt - 1
    qtile = lambda p, i: (jnp.minimum(i, last), p)
    return pl.pallas_call(
        body, name=f"attn_bwd_g{g}", grid=(4, nt + 1),
        in_specs=[pl.BlockSpec((T, 384), lambda p, i: (jnp.minimum(i, last), g * 4 + p)),
                  pl.BlockSpec((T, 384), lambda p, i: (jnp.maximum(i - 1, 0), g * 4 + p)),
                  pl.BlockSpec((T, 128), qtile), pl.BlockSpec((T, 128), qtile), pl.BlockSpec((T, 128), qtile),
                  pl.BlockSpec((1, 128), lambda p, i: (0, 0)),
                  pl.BlockSpec((1, 128), lambda p, i: (0, 0)),
                  pl.BlockSpec((128, 128), lambda p, i: (0, 0)),
                  ANY],
        out_specs=[pl.BlockSpec((T, 384), lambda p, i: (jnp.maximum(i - 1, 0), g * 4 + p)),
                   pl.BlockSpec((None, 8, 128), lambda p, i: (p, 0, 0))],
        out_shape=[jax.ShapeDtypeStruct(dprojq.shape, BF16), jax.ShapeDtypeStruct((4, 8, 128), F32)],
        scratch_shapes=[pltpu.VMEM((T, 128), sdt), pltpu.VMEM((2 * T, 128), sdt), pltpu.VMEM((2 * T, 128), sdt),
                        pltpu.VMEM((T, 128), sdt), pltpu.VMEM((T, 128), F32), pltpu.VMEM((T, 128), F32),
                        pltpu.VMEM((2 * T, 128), F32), pltpu.VMEM((2 * T, 128), F32)],
        input_output_aliases={8: 0},
        compiler_params=_params(),
    )(projq, projq, d_o, lse, delta, qnw2, knw2, ebd, dprojq)


def _mid(projn, o_g, lse_g, x, tgt, mod, convw8, wc, wa, wo, ebd, *, tm=256):
    S, D = x.shape
    nt = S // tm
    nst = max(1, 256 // tm)
    assert nt % nst == 0
    hb = tm // 16

    def body(pn_ref, hc_ref, hx_ref, o0_ref, o1_ref, o2_ref, l0_ref, l1_ref, l2_ref, x_ref, t_ref, mod_ref,
             cw_ref, ebd_ref, wc_hbm, wa_hbm, wo_hbm,
             dpn_ref, do_ref, lse_ref, dl_ref, dout_ref, vacc_ref, gwo_hbm, gwc_hbm, gwa_hbm,
             wc_s, wa_s, wo_s, gwo_s, gwc_s, gwa_s, carry_s, *stashes):
        i = pl.program_id(0)
        ti = nt - 1 - i
        par = i % nst
        stash = pl.ds(pl.multiple_of(par * tm, tm), tm)

        @pl.when(i == 0)
        def _():
            for src, dst in ((wc_hbm, wc_s), (wa_hbm, wa_s), (wo_hbm, wo_s)):
                pltpu.sync_copy(src, dst)
            gwo_s[...] = jnp.zeros_like(gwo_s)
            gwc_s[...] = jnp.zeros_like(gwc_s)
            gwa_s[...] = jnp.zeros_like(gwa_s)
            carry_s[...] = jnp.zeros_like(carry_s)
            vacc_ref[...] = jnp.zeros_like(vacc_ref)

        rows = lax.broadcasted_iota(jnp.int32, (tm, D), 0)
        w0, w1, w2 = cw_ref[0:1, :], cw_ref[1:2, :], cw_ref[2:3, :]
        gate = mod_ref[:, 2 * D:3 * D]

        b_a = pn_ref[:, BA:BA + D].astype(F32)
        c_a = pn_ref[:, CA:CA + D].astype(F32)
        x_a = pn_ref[:, XA:XA + D].astype(F32)
        z_a = pn_ref[:, ZA:ZA + D].astype(F32)
        u = c_a * x_a
        has_prev = (ti > 0).astype(F32)
        uh = hc_ref[...].astype(F32) * hx_ref[...].astype(F32) * has_prev
        h14, h15 = _row(uh, 14), _row(uh, 15)
        u_m1 = jnp.where(rows == 0, h15, pltpu.roll(u, 1, 0))
        u_m2 = jnp.where(rows == 0, h14, jnp.where(rows == 1, h15, pltpu.roll(u, 2, 0)))
        conv = w0 * u_m2 + w1 * u_m1 + w2 * u
        sg_a = _sigmoid(z_a)
        sz_a = z_a * sg_a
        y_a = b_a * conv * sz_a
        y_ab = y_a.astype(BF16)
        pa = jnp.dot(y_ab, wc_s[...], preferred_element_type=F32)

        l0, l1, l2 = l0_ref[...], l1_ref[...], l2_ref[...]
        mxl = jnp.maximum(jnp.maximum(l0, l1), l2)
        e0, e1, e2 = jnp.exp(l0 - mxl), jnp.exp(l1 - mxl), jnp.exp(l2 - mxl)
        den = e0 + e1 + e2
        attn = (e0 * o0_ref[...] + e1 * o1_ref[...] + e2 * o2_ref[...]) / den
        lse_ref[...] = mxl + jnp.log(den)
        z_b = pn_ref[:, ZB:ZB + ATTN_OUT].astype(F32)
        sg_b = _sigmoid(z_b)
        sz_b = z_b * sg_b
        y_b = attn * sz_b
        y_bb = y_b.astype(BF16)
        pb = jnp.dot(y_bb, wa_s[...], preferred_element_type=F32)

        s_a = _sigmoid(pn_ref[:, GA:GA + D].astype(F32))
        s_b = _sigmoid(pn_ref[:, GB:GB + D].astype(F32))
        merged = s_a * pa + s_b * pb
        merged_b = merged.astype(BF16)
        mo = jnp.dot(merged_b, wo_s[...], preferred_element_type=F32)
        diff = x_ref[...] + gate * mo - t_ref[...]
        dout = diff * (1.0 / D)
        dout_ref[...] = dout
        vacc_ref[4:5, :] += jnp.sum(diff * diff, axis=0, keepdims=True)
        vacc_ref[0:1, :] += jnp.sum(dout * mo, axis=0, keepdims=True)

        d_mo = (dout * gate).astype(BF16)
        nt_dims = (((1,), (1,)), ((), ()))
        dmerged = lax.dot_general(d_mo, wo_s[...], nt_dims, preferred_element_type=F32)
        dpa = (dmerged * s_a).astype(BF16)
        dpb = (dmerged * s_b).astype(BF16)
        dpn_ref[:, GA:GA + D] = (dmerged * pa * s_a * (1.0 - s_a)).astype(BF16)
        dpn_ref[:, GB:GB + D] = (dmerged * pb * s_b * (1.0 - s_b)).astype(BF16)
        tn = (((0,), (0,)), ((), ()))
        if nst == 1:
            gwo_s[...] += lax.dot_general(merged_b, d_mo, tn, preferred_element_type=F32)
            gwc_s[...] += lax.dot_general(y_ab, dpa, tn, preferred_element_type=F32)
            gwa_s[...] += lax.dot_general(y_bb, dpb, tn, preferred_element_type=F32)
        else:
            st_m, st_d, st_ya, st_pa, st_yb, st_pb = stashes
            st_m[stash, :] = merged_b
            st_d[stash, :] = d_mo
            st_ya[stash, :] = y_ab
            st_pa[stash, :] = dpa
            st_yb[stash, :] = y_bb
            st_pb[stash, :] = dpb

            @pl.when(par == nst - 1)
            def _():
                gwo_s[...] += lax.dot_general(st_m[...], st_d[...], tn, preferred_element_type=F32)
                gwc_s[...] += lax.dot_general(st_ya[...], st_pa[...], tn, preferred_element_type=F32)
                gwa_s[...] += lax.dot_general(st_yb[...], st_pb[...], tn, preferred_element_type=F32)

        dy_a = lax.dot_general(dpa, wc_s[...], nt_dims, preferred_element_type=F32)
        dy_b = lax.dot_general(dpb, wa_s[...], nt_dims, preferred_element_type=F32)

        dattn = dy_b * sz_b
        dpn_ref[:, ZB:ZB + ATTN_OUT] = (dy_b * attn * sg_b * (1.0 + z_b * (1.0 - sg_b))).astype(BF16)
        do_ref[...] = dattn.astype(BF16)
        dl_ref[...] = jnp.dot((dattn * attn).astype(BF16), ebd_ref[...], preferred_element_type=F32)

        dpn_ref[:, BA:BA + D] = (dy_a * conv * sz_a).astype(BF16)
        dpn_ref[:, ZA:ZA + D] = (dy_a * b_a * conv * sg_a * (1.0 + z_a * (1.0 - sg_a))).astype(BF16)
        dconv = dy_a * b_a * sz_a
        vacc_ref[1:2, :] += jnp.sum(dconv * u_m2, axis=0, keepdims=True)
        vacc_ref[2:3, :] += jnp.sum(dconv * u_m1, axis=0, keepdims=True)
        vacc_ref[3:4, :] += jnp.sum(dconv * u, axis=0, keepdims=True)
        nxt = carry_s[...]
        n0, n1 = _row(nxt, 0), _row(nxt, 1)
        dc_p1 = jnp.where(rows == tm - 1, n0, pltpu.roll(dconv, tm - 1, 0))
        dc_p2 = jnp.where(rows == tm - 1, n1, jnp.where(rows == tm - 2, n0, pltpu.roll(dconv, tm - 2, 0)))
        du = w2 * dconv + w1 * dc_p1 + w0 * dc_p2
        carry_s[...] = dconv[0:8, :]
        dpn_ref[:, CA:CA + D] = (du * x_a).astype(BF16)
        dpn_ref[:, XA:XA + D] = (du * c_a).astype(BF16)

        @pl.when(i == nt - 1)
        def _():
            pltpu.sync_copy(gwo_s, gwo_hbm)
            pltpu.sync_copy(gwc_s, gwc_hbm)
            pltpu.sync_copy(gwa_s, gwa_hbm)

    rev = lambda i: (nt - 1 - i, 0)
    halo = lambda col: pl.BlockSpec((16, D), lambda i: (jnp.maximum((nt - 1 - i) * hb - 1, 0), col))
    tile512 = pl.BlockSpec((tm, ATTN_OUT), rev)
    tileD = pl.BlockSpec((tm, D), rev)
    const = lambda shape: pl.BlockSpec(shape, lambda i: (0, 0))
    return pl.pallas_call(
        body, name="mid_fwd_bwd", grid=(nt,),
        in_specs=[pl.BlockSpec((tm, NAT), rev), halo(CA // D), halo(XA // D)] + [tile512] * 6
                 + [tileD, tileD, const((1, 3 * D)), const((8, D)), const((ATTN_OUT, ATTN_OUT))] + [ANY] * 3,
        out_specs=[pl.BlockSpec((tm, NAT), rev), tile512, tile512, tile512, tileD, const((8, D)), ANY, ANY, ANY],
        out_shape=[jax.ShapeDtypeStruct((S, NAT), BF16), jax.ShapeDtypeStruct((S, ATTN_OUT), BF16),
                   jax.ShapeDtypeStruct((S, ATTN_OUT), F32), jax.ShapeDtypeStruct((S, ATTN_OUT), F32),
                   jax.ShapeDtypeStruct((S, D), F32), jax.ShapeDtypeStruct((8, D), F32),
                   jax.ShapeDtypeStruct((D, D), F32), jax.ShapeDtypeStruct((D, D), F32),
                   jax.ShapeDtypeStruct((ATTN_OUT, D), F32)],
        scratch_shapes=[pltpu.VMEM((D, D), BF16), pltpu.VMEM((ATTN_OUT, D), BF16), pltpu.VMEM((D, D), BF16),
                        pltpu.VMEM((D, D), F32), pltpu.VMEM((D, D), F32), pltpu.VMEM((ATTN_OUT, D), F32),
                        pltpu.VMEM((8, D), F32)]
                       + ([pltpu.VMEM((nst * tm, D), BF16)] * 4
                          + [pltpu.VMEM((nst * tm, ATTN_OUT), BF16), pltpu.VMEM((nst * tm, D), BF16)] if nst > 1 else []),
        compiler_params=_params(vmem=60 * 1024 * 1024),
    )(projn, projn, projn, *o_g, *lse_g, x, tgt, mod, convw8, ebd, wc, wa, wo)


def _dh_matmul(dpn, dpq, w_nat, w_qkv, parts, *, tm=512):
    S = dpn.shape[0]
    D = D_MODEL
    n = len(parts)
    nt = S // tm

    def body(dn_ref, dq_ref, wn_hbm, wq_hbm, *rest):
        p_hbm, rest = rest[:n], rest[n:]
        dh_ref = rest[0]
        q_hbm = rest[1:1 + n]
        wn_s, wq_s = rest[1 + n:3 + n]
        sems = rest[3 + n:]

        @pl.when(pl.program_id(0) == 0)
        def _():
            if n:
                mine, _ = _scatter_copies(p_hbm, q_hbm, *sems)
                for cp in mine:
                    cp.start()
            pltpu.sync_copy(wn_hbm, wn_s)
            pltpu.sync_copy(wq_hbm, wq_s)

        if n:
            @pl.when(pl.program_id(0) == nt - 1)
            def _():
                mine, theirs = _scatter_copies(p_hbm, q_hbm, *sems)
                for cp in theirs:
                    cp.wait_recv()
                for cp in mine:
                    cp.wait_send()

        nt_dims = (((1,), (1,)), ((), ()))
        dh = (lax.dot_general(dn_ref[...], wn_s[...], nt_dims, preferred_element_type=F32)
              + lax.dot_general(dq_ref[...], wq_s[...], nt_dims, preferred_element_type=F32))
        dh_ref[...] = dh.astype(BF16)

    row = lambda w: pl.BlockSpec((tm, w), lambda i: (i, 0))
    outs = pl.pallas_call(
        body, name="dh_matmul", grid=(nt,),
        in_specs=[row(NAT), row(NQKV), ANY, ANY] + [ANY] * n,
        out_specs=[row(D)] + [ANY] * n,
        out_shape=[jax.ShapeDtypeStruct((S, D), BF16)] + [jax.ShapeDtypeStruct(p.shape, p.dtype) for p in parts],
        scratch_shapes=[pltpu.VMEM((D, NAT), BF16), pltpu.VMEM((D, NQKV), BF16)]
                       + ([pltpu.SemaphoreType.DMA((n, 3)), pltpu.SemaphoreType.DMA((n, 3))] if n else []),
        compiler_params=_params(vmem=60 * 1024 * 1024),
    )(dpn, dpq, w_nat, w_qkv, *parts)
    return outs[0], outs[1:]


def _norm_bwd(dh, x, dout, mod, norm_w, *, tm=512):
    S, D = x.shape

    def body(dh_ref, x_ref, dout_ref, mod_ref, nw_ref, gx_ref, st_ref):
        @pl.when(pl.program_id(0) == 0)
        def _():
            st_ref[...] = jnp.zeros_like(st_ref)

        dh = dh_ref[...].astype(F32)
        scale1 = 1.0 + mod_ref[:, D:2 * D]
        nw = nw_ref[...]
        xv = x_ref[...]
        rstd = lax.rsqrt(jnp.mean(xv * xv, axis=-1, keepdims=True) + EPS)
        xh = xv * rstd
        dhs = dh * scale1
        dxh = dhs * nw
        dx = rstd * (dxh - xh * jnp.mean(dxh * xh, axis=-1, keepdims=True))
        gx_ref[...] = dout_ref[...] + dx
        st_ref[0:1, :] += jnp.sum(dh, axis=0, keepdims=True)
        st_ref[1:2, :] += jnp.sum(dh * (xh * nw), axis=0, keepdims=True)
        st_ref[2:3, :] += jnp.sum(dhs * xh, axis=0, keepdims=True)

    row = pl.BlockSpec((tm, D), lambda i: (i, 0))
    return pl.pallas_call(
        body, name="norm_bwd", grid=(S // tm,),
        in_specs=[row, row, row, pl.BlockSpec((1, 3 * D), lambda i: (0, 0)), pl.BlockSpec((1, D), lambda i: (0, 0))],
        out_specs=[row, pl.BlockSpec((8, D), lambda i: (0, 0))],
        out_shape=[jax.ShapeDtypeStruct((S, D), F32), jax.ShapeDtypeStruct((8, D), F32)],
        compiler_params=_params(),
    )(dh, x, dout, mod, norm_w)


def _pack_smalls(st, vacc, gqk):
    D = D_MODEL

    def body(st_ref, va_ref, gqk_ref, o_ref):
        o_ref[...] = jnp.zeros_like(o_ref)
        o_ref[0:2, :] = st_ref[0:2, :]
        o_ref[2:3, :] = va_ref[0:1, :]
        o_ref[8:9, :] = st_ref[2:3, :]
        o_ref[16:19, :] = va_ref[1:4, :]
        tot = gqk_ref[0]
        for k in range(1, gqk_ref.shape[0]):
            tot = tot + gqk_ref[k]
        tot = tot + pltpu.roll(tot, HEAD_DIM, 1)
        o_ref[24:32, 0:128] = tot
        loss = jnp.sum(va_ref[4:5, :], axis=1, keepdims=True) * (0.5 / D)
        o_ref[26:27, :] = jnp.broadcast_to(loss, (1, D))

    return pl.pallas_call(
        body, name="pack_smalls", out_shape=jax.ShapeDtypeStruct((32, D), F32),
        in_specs=[pl.BlockSpec(memory_space=pltpu.VMEM)] * 3,
        out_specs=pl.BlockSpec(memory_space=pltpu.VMEM), compiler_params=_params(),
    )(st, vacc, gqk)


def _small_update(sm_loc, dm_pad, ct_pad, w_ada, m_ada, v_ada, vec_w, vec_m, vec_v):
    D = D_MODEL
    cols = w_ada.shape[1]

    def body(sm_ref, dm_ref, ct_ref, w_ref, m_ref, v_ref, vw_ref, vm_ref, vv_ref,
             gw_ref, dw_ref, mw_ref, vw_o_ref, gv_ref, dv_ref, mv_ref, vv_o_ref):
        g = sm_ref[0]
        for k in range(1, 8):
            g = g + sm_ref[k]
        gv_ref[...] = g
        dv, mv, vv = _adamw_math(vw_ref[...], g, vm_ref[...], vv_ref[...])
        dv_ref[...] = dv
        mv_ref[...] = mv
        vv_o_ref[...] = vv
        gw = jnp.dot(ct_ref[...], dm_ref[...], preferred_element_type=F32, precision=lax.Precision.HIGHEST)
        gw_ref[...] = gw
        dw, mw, vw = _adamw_math(w_ref[...], gw, m_ref[...], v_ref[...])
        dw_ref[...] = dw
        mw_ref[...] = mw
        vw_o_ref[...] = vw

    sw = jax.ShapeDtypeStruct((D, cols), F32)
    sv = jax.ShapeDtypeStruct((32, D), F32)
    return pl.pallas_call(
        body, name="small_update", out_shape=[sw, sw, sw, sw, sv, sv, sv, sv],
        in_specs=[pl.BlockSpec(memory_space=pltpu.VMEM)] * 9,
        out_specs=[pl.BlockSpec(memory_space=pltpu.VMEM)] * 8, compiler_params=_params(),
    )(sm_loc, dm_pad, ct_pad, w_ada, m_ada, v_ada, vec_w, vec_m, vec_v)


def _local_step(xs, tg, mod, norm_w, w_nat, w_qkv, wc, wa, wo, convw8, q_norm_w, k_norm_w):
    S = xs.shape[0]
    lane = jnp.arange(128)
    ebd128 = (lane[:, None] // HEAD_DIM == lane[None, :] // HEAD_DIM).astype(BF16)
    lane5 = jnp.arange(ATTN_OUT)
    ebd512 = (lane5[:, None] // HEAD_DIM == lane5[None, :] // HEAD_DIM).astype(BF16)
    qnw2 = jnp.tile(q_norm_w, (1, 2))
    knw2 = jnp.tile(k_norm_w, (1, 2))

    projn, h, ht = _inproj_nat(xs, mod, norm_w, w_nat)
    projq = _matmul(h, w_qkv, tm=1024, tn=1536, name="inproj_qkv")
    o_g, lse_g = [], []
    for g in range(N_GROUPS):
        o, l = _attn_fwd(projq, qnw2, knw2, ebd128, g)
        o_g.append(o)
        lse_g.append(l)

    dpn, d_o, lse, delta, dout, vacc, gwo, gwc, gwa = _mid(
        projn, o_g, lse_g, xs, tg, mod, convw8, wc, wa, wo, ebd512)
    dpq = lax.empty((S, NQKV), BF16)
    gqk = []
    for g in range(N_GROUPS):
        dpq, part = _attn_bwd(projq, dpq, d_o, lse, delta, qnw2, knw2, ebd128, g)
        gqk.append(part)
    gw_nat = _matmul_acc(ht, dpn, tn=1664, tk=2048, name="grad_w_in_nat")
    gw_qkv = _matmul_acc(ht, dpq, tn=1536, tk=2048, name="grad_w_in_qkv")
    return dpn, dpq, dout, gw_nat, gw_qkv, gwc, gwa, gwo, vacc, gqk


def kernel(x, c, w_ada, b_ada, norm_w, w_in, conv_w, q_norm_w, k_norm_w, w_br_conv, w_br_attn, w_out, loss_target, m_w_ada, m_b_ada, m_norm_w, m_w_in, m_conv_w, m_q_norm_w, m_k_norm_w, m_w_br_conv, m_w_br_attn, m_w_out, v_w_ada, v_b_ada, v_norm_w, v_w_in, v_conv_w, v_q_norm_w, v_k_norm_w, v_w_br_conv, v_w_br_attn, v_w_out):
    D = D_MODEL
    S = x.shape[1]
    xs, tg = x[0], loss_target[0]
    mx, my, mc = lax.axis_index("x"), lax.axis_index("y"), lax.axis_index("c")
    chip = 2 * mx + my
    dev = 2 * chip + mc

    c_blk = jnp.concatenate([c, jnp.pad(conv_w[0], ((0, 0), (0, D - 256))), jnp.zeros((4, D), F32)], axis=0)
    cg = _allgather8(c_blk, "allgather_c")
    c_all = cg[:, 0, :]
    convw8 = jnp.pad(cg[::2, 1:4, 0:256].transpose(1, 0, 2).reshape(3, D), ((0, 5), (0, 0)))
    b_shard = lax.dynamic_slice(b_ada, (0, chip * 768), (1, 768))
    mod_part, silu_c = _mod_part(c_all, w_ada[0], b_shard)
    mod_parts = _allgather_chips(mod_part, "allgather_mod")
    mod_all = mod_parts.transpose(1, 0, 2).reshape(8, 3 * D)
    mod = lax.dynamic_slice(mod_all, (dev, 0), (1, 3 * D))

    shards = [w_in[0].astype(BF16).reshape(2, 512, 2816), w_br_conv[0].astype(BF16).reshape(2, 128, D),
              w_br_attn[0].astype(BF16).reshape(2, 256, 256), w_out[0].astype(BF16).reshape(2, 128, D)]
    gathered_in, gathered_w = _weights_allgather(shards[0].reshape(2, 2, 256, SHARD_COLS), shards[1:])
    g_bc, g_ba, g_wo = [lax.dynamic_update_slice(g, s[None], (chip, 0, 0, 0))
                        for g, s in zip(gathered_w, shards[1:])]
    chipv = jnp.reshape(chip, (1,)).astype(jnp.int32)
    g_in = gathered_in.reshape(4, D, SHARD_COLS)
    own_in = shards[0].reshape(D, SHARD_COLS)
    w_nat = _weights_prep(g_in, own_in, chipv, "nat")
    w_qkv = _weights_prep(g_in, own_in, chipv, "qkv")
    wc = g_bc.reshape(D, D)
    wa = g_ba.reshape(4, ATTN_OUT, 256).transpose(1, 0, 2).reshape(ATTN_OUT, D)
    wo = g_wo.reshape(D, D)

    dpn, dpq, dout, gw_nat, gw_qkv, gwc, gwa, gwo, vacc, gqk = _local_step(
        xs, tg, mod, norm_w, w_nat, w_qkv, wc, wa, wo, convw8, q_norm_w, k_norm_w)

    def halves(a):
        k, r, cc = a.shape
        return a.reshape(k, 2, r // 2, cc).transpose(1, 0, 2, 3)

    grads = [gw_nat.reshape(2, D // 2, NAT), gw_qkv.reshape(2, D // 2, NQKV),
             halves(gwc.astype(BF16).reshape(4, 256, D)),
             halves(gwa.astype(BF16).reshape(ATTN_OUT, 4, 256).transpose(1, 0, 2)),
             halves(gwo.astype(BF16).reshape(4, 256, D))]
    recv = _pair_exchange(grads)
    core = jnp.reshape(mc, (1,)).astype(jnp.int32)
    p_in = _pair_add_to_shards(core, grads[0], recv[0], lax.empty((4, D // 2, SHARD_COLS), BF16), "nat")
    p_in = _pair_add_to_shards(core, grads[1], recv[1], p_in, "qkv")
    parts = [p_in] + [_pair_add(core, gr, rc) for gr, rc in zip(grads[2:], recv[2:])]
    dh, gathered = _dh_matmul(dpn, dpq, w_nat, w_qkv, parts)
    grad_x, st = _norm_bwd(dh, xs, dout, mod, norm_w)
    mine = [_sum4(chipv, p, q) for p, q in zip(parts, gathered)]
    theirs, sm_all = _half_exchange(mine, _pack_smalls(st, vacc, jnp.concatenate(gqk, axis=0)))

    big = {}
    for t, (nm, w, m, v) in enumerate((("w_in", w_in, m_w_in, v_w_in), ("w_br_conv", w_br_conv, m_w_br_conv, v_w_br_conv),
                                       ("w_br_attn", w_br_attn, m_w_br_attn, v_w_br_attn), ("w_out", w_out, m_w_out, v_w_out))):
        big[nm] = _adamw_halves(core, w[0], mine[t], theirs[t], m[0], v[0], "adamw_" + nm)

    dmod_all = sm_all[:, 0:3, :].reshape(8, 3 * D)
    dm_pad = jnp.pad(lax.dynamic_slice(dmod_all, (0, chip * 768), (8, 768)), ((0, 120), (0, 0)))
    conv_dev = lax.dynamic_slice(sm_all[:, 16:24, :], (0, 0, chip * 256), (8, 8, 256))
    sm_loc = jnp.concatenate([sm_all[:, 0:16, :], jnp.pad(conv_dev, ((0, 0), (0, 0), (0, D - 256))), sm_all[:, 24:32, :]], axis=1)

    def pack_vec(b3, nw, cw, qw, kw):
        z = jnp.zeros((32, D), F32)
        z = z.at[0:3, :].set(b3.reshape(3, D)).at[8:9, :].set(nw).at[16:19, 0:256].set(cw)
        return z.at[24:25, 0:HEAD_DIM].set(qw).at[25:26, 0:HEAD_DIM].set(kw)

    vec_w = pack_vec(b_ada, norm_w, conv_w[0], q_norm_w, k_norm_w)
    vec_m = pack_vec(m_b_ada, m_norm_w, m_conv_w[0], m_q_norm_w, m_k_norm_w)
    vec_v = pack_vec(v_b_ada, v_norm_w, v_conv_w[0], v_q_norm_w, v_k_norm_w)
    ct_pad = jnp.pad(silu_c.T, ((0, 0), (0, 120)))
    outs = _small_update(sm_loc, dm_pad, ct_pad, w_ada[0], m_w_ada[0], v_w_ada[0], vec_w, vec_m, vec_v)
    ada = outs[0:4]
    vec = outs[4:8]
    loss = vec[0][26, 0]

    def unpack(t):
        return {"b_ada": t[0:3, :].reshape(1, 3 * D), "norm_w": t[8:9, :], "conv_w": t[16:19, 0:256][None],
                "q_norm_w": t[24:25, 0:HEAD_DIM], "k_norm_w": t[25:26, 0:HEAD_DIM]}

    small = [unpack(t) for t in vec]
    order = ["w_ada", "b_ada", "norm_w", "w_in", "conv_w", "q_norm_w", "k_norm_w", "w_br_conv", "w_br_attn", "w_out"]
    res = [loss, grad_x[None]]
    for kind in range(4):
        for nm in order:
            if nm == "w_ada":
                res.append(ada[kind][None])
            elif nm in big:
                res.append(big[nm][kind][None])
            else:
                res.append(small[kind][nm])
    return tuple(res)
```

```python
import functools

import jax
import jax.numpy as jnp
from jax import lax
from jax.experimental import pallas as pl
from jax.experimental.pallas import tpu as pltpu

F32 = jnp.float32
BF16 = jnp.bfloat16
MESH = pl.DeviceIdType.MESH
ANY = pl.BlockSpec(memory_space=pl.ANY)

D_MODEL = 1024
HEAD_DIM = 64
N_GROUPS = 3
DILATIONS = (1, 4, 16)
ATTN_OUT = 512
EPS = 1e-6
NEG_INF = -1e30
NAT = 6656
NQKV = 4608
BA, CA, XA, ZA, ZB, GA, GB = 0, 1024, 2048, 3072, 4096, 4608, 5632
ATT_TILE = 2048
QK_SCALE = HEAD_DIM ** -0.5
LOG2E = 1.4426950408889634
LN2 = 0.6931471805599453
V7X_VMEM_BYTES = 64 * 1024 * 1024
V7X_VMEM_LIMIT = 56 * 1024 * 1024

ADAM_LR = 0.001
ADAM_B1 = 0.9
ADAM_B2 = 0.999
ADAM_EPS = 1e-08
ADAM_WD = 0.01
ADAM_STEP = 10


def _params(vmem=V7X_VMEM_LIMIT, **kw):
    return pltpu.CompilerParams(vmem_limit_bytes=vmem, **kw)


def _sigmoid(z):
    return 1.0 / (1.0 + jnp.exp(-z))


def _row(v, r):
    rows = lax.broadcasted_iota(jnp.int32, v.shape, 0)
    return jnp.sum(jnp.where(rows == r, v, 0.0), axis=0, keepdims=True)


def _coords():
    return lax.axis_index("x"), lax.axis_index("y"), lax.axis_index("c")


def _flip(v, bit):
    return 1 - v if bit else v


def _weights_allgather(big_shard, shards, c_blk, w_ada, b_shard):
    n = len(shards)
    D = D_MODEL
    cols = w_ada.shape[1]

    def body(*refs):
        big_in, ins = refs[0], refs[1:1 + n]
        c_ref, wada_ref, b_ref = refs[1 + n:4 + n]
        big_out, outs = refs[4 + n], refs[5 + n:5 + 2 * n]
        cg_ref, mp_ref, sc_ref = refs[5 + 2 * n:8 + 2 * n]
        ssem, rsem, bs, br, cs, cr, ms, mr, lsem = refs[8 + 2 * n:]
        mx, my, mc = _coords()
        me = 2 * mx + my
        me8 = 2 * me + mc
        sib = (mx, my, 1 - mc)
        chips = [(_flip(mx, j & 2), _flip(my, j & 1)) for j in range(1, 4)]
        sends = []

        local = pltpu.make_async_copy(c_ref, cg_ref.at[me8], lsem)
        local.start()
        for j in range(1, 8):
            peer = (_flip(mx, j & 4), _flip(my, j & 2), _flip(mc, j & 1))
            cp = pltpu.make_async_remote_copy(
                src_ref=c_ref, dst_ref=cg_ref.at[me8], send_sem=cs.at[j - 1], recv_sem=cr.at[j - 1],
                device_id=peer, device_id_type=MESH)
            cp.start()
            sends.append(cp)

        kx, ky, kd = 2 * (1 - mx) + my, 2 * mx + (1 - my), 2 * (1 - mx) + (1 - my)
        xn, yn = (1 - mx, my, mc), (mx, 1 - my, mc)

        def big(src_chip, q, sem, to, half=None):
            h = mc if half is None else half
            return pltpu.make_async_remote_copy(
                src_ref=big_out.at[src_chip, h, q], dst_ref=big_out.at[src_chip, h, q],
                send_sem=bs.at[sem], recv_sem=br.at[sem], device_id=to, device_id_type=MESH)

        def own(q, sem, to):
            return pltpu.make_async_remote_copy(
                src_ref=big_in.at[mc, q], dst_ref=big_out.at[me, mc, q],
                send_sem=bs.at[sem], recv_sem=br.at[sem], device_id=to, device_id_type=MESH)

        for cp in (own(0, 0, xn), own(1, 3, yn), own(1, 1, xn), own(0, 2, yn)):
            cp.start()
            sends.append(cp)
        for t in range(n):
            for j, (px, py) in enumerate(chips):
                cp = pltpu.make_async_remote_copy(
                    src_ref=ins[t].at[mc], dst_ref=outs[t].at[me, mc], send_sem=ssem.at[t, j],
                    recv_sem=rsem.at[t, j], device_id=(px, py, mc), device_id_type=MESH)
                cp.start()
                sends.append(cp)

        pieces = [(kx, 0), (kx, 1), (ky, 0), (ky, 1), (kd, 0), (kd, 1)]

        def landed(piece, sem, frm, pass_on=None):
            src_chip, q = pieces[piece]
            big(src_chip, q, sem, frm).wait_recv()
            nxt = ([] if pass_on is None else [big(src_chip, q, pass_on[0], pass_on[1])])
            nxt.append(big(src_chip, q, 6 + piece, sib))
            for cp in nxt:
                cp.start()
                sends.append(cp)

        for j in range(1, 8):
            px, py, pc = _flip(mx, j & 4), _flip(my, j & 2), _flip(mc, j & 1)
            pltpu.make_async_remote_copy(
                src_ref=c_ref, dst_ref=cg_ref.at[4 * px + 2 * py + pc], send_sem=cs.at[j - 1],
                recv_sem=cr.at[j - 1], device_id=(px, py, pc), device_id_type=MESH).wait_recv()
        local.wait()
        blocks = cg_ref[...].reshape(64, D)
        sc = blocks * _sigmoid(blocks)
        sc_ref[...] = sc
        mp_ref[me] = jnp.dot(sc, wada_ref[...], preferred_element_type=F32,
                             precision=lax.Precision.HIGHEST) + b_ref[...]
        for j, (px, py) in enumerate(chips):
            cp = pltpu.make_async_remote_copy(
                src_ref=mp_ref.at[me], dst_ref=mp_ref.at[me], send_sem=ms.at[j], recv_sem=mr.at[j],
                device_id=(px, py, mc), device_id_type=MESH)
            cp.start()
            sends.append(cp)

        landed(0, 0, xn, pass_on=(4, yn))
        landed(3, 3, yn, pass_on=(5, xn))
        landed(1, 1, xn)
        landed(2, 2, yn)
        for j, (px, py) in enumerate(chips):
            pltpu.make_async_remote_copy(
                src_ref=mp_ref.at[me], dst_ref=mp_ref.at[2 * px + py], send_sem=ms.at[j], recv_sem=mr.at[j],
                device_id=(px, py, mc), device_id_type=MESH).wait_recv()
        landed(4, 4, yn)
        landed(5, 5, xn)
        for t in range(n):
            for j, (px, py) in enumerate(chips):
                src = 2 * px + py
                pltpu.make_async_remote_copy(
                    src_ref=ins[t].at[mc], dst_ref=outs[t].at[src, mc], send_sem=ssem.at[t, j],
                    recv_sem=rsem.at[t, j], device_id=(px, py, mc), device_id_type=MESH).wait_recv()
                fwd = pltpu.make_async_remote_copy(
                    src_ref=outs[t].at[src, mc], dst_ref=outs[t].at[src, mc], send_sem=ssem.at[t, 3 + j],
                    recv_sem=rsem.at[t, 3 + j], device_id=sib, device_id_type=MESH)
                fwd.start()
                sends.append(fwd)
        for t in range(n):
            for j, (px, py) in enumerate(chips):
                src = 2 * px + py
                pltpu.make_async_remote_copy(
                    src_ref=outs[t].at[src, 1 - mc], dst_ref=outs[t].at[src, 1 - mc], send_sem=ssem.at[t, 3 + j],
                    recv_sem=rsem.at[t, 3 + j], device_id=sib, device_id_type=MESH).wait_recv()
        for i, (src_chip, q) in enumerate(pieces):
            big(src_chip, q, 6 + i, sib, half=1 - mc).wait_recv()
        for cp in sends:
            cp.wait_send()

    vm = pl.BlockSpec(memory_space=pltpu.VMEM)
    outs = pl.pallas_call(
        body, name="weights_allgather",
        out_shape=[jax.ShapeDtypeStruct((4,) + big_shard.shape, big_shard.dtype)]
                  + [jax.ShapeDtypeStruct((4,) + s.shape, s.dtype) for s in shards]
                  + [jax.ShapeDtypeStruct((8,) + c_blk.shape, F32), jax.ShapeDtypeStruct((4, 64, cols), F32),
                     jax.ShapeDtypeStruct((64, D), F32)],
        in_specs=[ANY] * (n + 1) + [vm] * 3, out_specs=[ANY] * (n + 1) + [vm] * 3,
        scratch_shapes=[pltpu.SemaphoreType.DMA((n, 6)), pltpu.SemaphoreType.DMA((n, 6)),
                        pltpu.SemaphoreType.DMA((12,)), pltpu.SemaphoreType.DMA((12,)),
                        pltpu.SemaphoreType.DMA((7,)), pltpu.SemaphoreType.DMA((7,)),
                        pltpu.SemaphoreType.DMA((3,)), pltpu.SemaphoreType.DMA((3,)), pltpu.SemaphoreType.DMA(())],
        compiler_params=_params(),
    )(big_shard, *shards, c_blk, w_ada, b_shard)
    return outs[0], outs[1:1 + n], outs[1 + n], outs[2 + n], outs[3 + n]


def _pair_exchange(grads):
    n = len(grads)

    def body(*refs):
        ins, outs = refs[:n], refs[n:2 * n]
        ssem, rsem = refs[2 * n:]
        mx, my, mc = _coords()
        sib = (mx, my, 1 - mc)
        sends = []
        for t in range(n):
            cp = pltpu.make_async_remote_copy(
                src_ref=ins[t].at[1 - mc], dst_ref=outs[t], send_sem=ssem.at[t], recv_sem=rsem.at[t],
                device_id=sib, device_id_type=MESH)
            cp.start()
            sends.append(cp)
        for cp in sends:
            cp.wait_recv()
        for cp in sends:
            cp.wait_send()

    return pl.pallas_call(
        body, name="grad_pair_exchange",
        out_shape=[jax.ShapeDtypeStruct(g.shape[1:], g.dtype) for g in grads],
        in_specs=[ANY] * n, out_specs=[ANY] * n,
        scratch_shapes=[pltpu.SemaphoreType.DMA((n,)), pltpu.SemaphoreType.DMA((n,))],
    )(*grads)


def _scatter_copies(ins, outs, ssem, rsem):
    mx, my, mc = _coords()
    me = 2 * mx + my
    mine, theirs = [], []
    for t in range(len(ins)):
        for j in range(1, 4):
            px, py = _flip(mx, j & 2), _flip(my, j & 1)
            mine.append(pltpu.make_async_remote_copy(
                src_ref=ins[t].at[2 * px + py], dst_ref=outs[t].at[me], send_sem=ssem.at[t, j - 1],
                recv_sem=rsem.at[t, j - 1], device_id=(px, py, mc), device_id_type=MESH))
            theirs.append(pltpu.make_async_remote_copy(
                src_ref=ins[t].at[me], dst_ref=outs[t].at[2 * px + py], send_sem=ssem.at[t, j - 1],
                recv_sem=rsem.at[t, j - 1], device_id=(px, py, mc), device_id_type=MESH))
    return mine, theirs


def _half_exchange(halves, smalls):
    n = len(halves)

    def body(*refs):
        ins, sm_ref = refs[:n], refs[n]
        outs, all_ref = refs[n + 1:2 * n + 1], refs[2 * n + 1]
        ssem, rsem, gsem, hsem, lsem = refs[2 * n + 2:]
        mx, my, mc = _coords()
        me = 4 * mx + 2 * my + mc
        sib = (mx, my, 1 - mc)
        local = pltpu.make_async_copy(sm_ref, all_ref.at[me], lsem)
        local.start()
        sends = []
        for t in range(n):
            rc = pltpu.make_async_remote_copy(
                src_ref=ins[t], dst_ref=outs[t], send_sem=ssem.at[t], recv_sem=rsem.at[t],
                device_id=sib, device_id_type=MESH)
            rc.start()
            sends.append(rc)
        gathers = []
        for j in range(1, 8):
            peer = (_flip(mx, j & 4), _flip(my, j & 2), _flip(mc, j & 1))
            cp = pltpu.make_async_remote_copy(
                src_ref=sm_ref, dst_ref=all_ref.at[me], send_sem=gsem.at[j - 1], recv_sem=hsem.at[j - 1],
                device_id=peer, device_id_type=MESH)
            cp.start()
            gathers.append(cp)
        for cp in sends:
            cp.wait_recv()
        for j in range(1, 8):
            px, py, pc = _flip(mx, j & 4), _flip(my, j & 2), _flip(mc, j & 1)
            pltpu.make_async_remote_copy(
                src_ref=sm_ref, dst_ref=all_ref.at[4 * px + 2 * py + pc], send_sem=gsem.at[j - 1],
                recv_sem=hsem.at[j - 1], device_id=(px, py, pc), device_id_type=MESH).wait_recv()
        for cp in sends + gathers:
            cp.wait_send()
        local.wait()

    vm = pl.BlockSpec(memory_space=pltpu.VMEM)
    outs = pl.pallas_call(
        body, name="grad_half_exchange",
        out_shape=[jax.ShapeDtypeStruct(h.shape, h.dtype) for h in halves]
                  + [jax.ShapeDtypeStruct((8,) + smalls.shape, smalls.dtype)],
        in_specs=[ANY] * n + [vm], out_specs=[ANY] * n + [vm],
        scratch_shapes=[pltpu.SemaphoreType.DMA((n,)), pltpu.SemaphoreType.DMA((n,)),
                        pltpu.SemaphoreType.DMA((7,)), pltpu.SemaphoreType.DMA((7,)), pltpu.SemaphoreType.DMA(())],
    )(*halves, smalls)
    return outs[:n], outs[n]


def _row_tile(rows, cols, bytes_per_row_elem=4, budget=2 * 1024 * 1024):
    t = rows
    while t % 2 == 0 and t > 16 and t * cols * bytes_per_row_elem > budget:
        t //= 2
    return t


def _pair_add(core, g, r):
    _, _, rh, cc = g.shape
    tr = _row_tile(rh, cc)

    def body(core_ref, g_ref, r_ref, o_ref):
        o_ref[...] = (g_ref[...].astype(F32) + r_ref[...].astype(F32)).astype(o_ref.dtype)

    return pl.pallas_call(
        body, name="grad_pair_add",
        grid_spec=pltpu.PrefetchScalarGridSpec(
            num_scalar_prefetch=1, grid=(4, rh // tr),
            in_specs=[pl.BlockSpec((None, None, tr, cc), lambda k, i, c: (c[0], k, i, 0)),
                      pl.BlockSpec((None, tr, cc), lambda k, i, c: (k, i, 0))],
            out_specs=pl.BlockSpec((None, tr, cc), lambda k, i, c: (k, i, 0))),
        out_shape=jax.ShapeDtypeStruct(r.shape, BF16),
        compiler_params=_params(),
    )(core, g, r)


def _sum4(chip, p, q):
    _, rh, cc = q.shape
    tr = _row_tile(rh, cc)

    def body(chip_ref, p_ref, q1_ref, q2_ref, q3_ref, o_ref):
        o_ref[...] = ((p_ref[...].astype(F32) + q1_ref[...].astype(F32)) + q2_ref[...].astype(F32)) + q3_ref[...].astype(F32)

    def other(j):
        return pl.BlockSpec((None, tr, cc), lambda i, c: (jnp.bitwise_xor(c[0], j), i, 0))

    return pl.pallas_call(
        body, name="grad_sum4",
        grid_spec=pltpu.PrefetchScalarGridSpec(
            num_scalar_prefetch=1, grid=(rh // tr,),
            in_specs=[pl.BlockSpec((None, tr, cc), lambda i, c: (c[0], i, 0)), other(1), other(2), other(3)],
            out_specs=pl.BlockSpec((tr, cc), lambda i, c: (i, 0))),
        out_shape=jax.ShapeDtypeStruct((rh, cc), F32),
        compiler_params=_params(),
    )(chip, p, q, q, q)


def _adamw_math(w, g, m, v):
    m = ADAM_B1 * m + (1.0 - ADAM_B1) * g
    v = ADAM_B2 * v + (1.0 - ADAM_B2) * (g * g)
    m_hat = m / (1.0 - ADAM_B1 ** ADAM_STEP)
    v_hat = v / (1.0 - ADAM_B2 ** ADAM_STEP)
    delta = -ADAM_LR * (m_hat / (jnp.sqrt(v_hat) + ADAM_EPS) + ADAM_WD * w)
    return delta, m, v


def _adamw_halves(core, w, mine, recv, m, v, name):
    rows, cols = w.shape
    rh = rows // 2
    tr = _row_tile(rh, cols, budget=1024 * 1024)
    nh = rh // tr

    def body(core_ref, w_ref, a_ref, b_ref, m_ref, v_ref, g_ref, d_ref, mo_ref, vo_ref):
        g = jnp.where(pl.program_id(0) == core_ref[0], a_ref[...], b_ref[...])
        d, mn, vn = _adamw_math(w_ref[...], g, m_ref[...], v_ref[...])
        g_ref[...] = g
        d_ref[...] = d
        mo_ref[...] = mn
        vo_ref[...] = vn

    full = pl.BlockSpec((tr, cols), lambda h, i, c: (h * nh + i, 0))
    half = pl.BlockSpec((tr, cols), lambda h, i, c: (i, 0))
    sd = jax.ShapeDtypeStruct((rows, cols), F32)
    return pl.pallas_call(
        body, name=name,
        grid_spec=pltpu.PrefetchScalarGridSpec(
            num_scalar_prefetch=1, grid=(2, nh),
            in_specs=[full, half, half, full, full], out_specs=[full] * 4),
        out_shape=[sd, sd, sd, sd], compiler_params=_params(),
    )(core, w, mine, recv, m, v)


def _inproj_nat(x, mod, norm_w, w, *, tm=1024, tn=1664):
    S, D = x.shape
    N = w.shape[1]
    chunk = 256

    def body(x_ref, mod_ref, nw_ref, w_ref, proj_ref, h_ref, ht_ref):
        @pl.when(pl.program_id(1) == 0)
        def _():
            shift = mod_ref[:, 0:D]
            scale1 = 1.0 + mod_ref[:, D:2 * D]
            for r in range(tm // chunk):
                xv = x_ref[pl.ds(r * chunk, chunk), :]
                ms = jnp.mean(xv * xv, axis=-1, keepdims=True)
                h = (xv * lax.rsqrt(ms + EPS) * nw_ref[...]) * scale1 + shift
                h_ref[pl.ds(r * chunk, chunk), :] = h.astype(BF16)
                ht_ref[:, pl.ds(r * chunk, chunk)] = h.T.astype(BF16)
        proj_ref[...] = jnp.dot(h_ref[...], w_ref[...], preferred_element_type=F32).astype(BF16)

    return pl.pallas_call(
        body, name="inproj_nat", grid=(S // tm, N // tn),
        in_specs=[pl.BlockSpec((tm, D), lambda i, j: (i, 0)),
                  pl.BlockSpec((1, 3 * D), lambda i, j: (0, 0)),
                  pl.BlockSpec((1, D), lambda i, j: (0, 0)),
                  pl.BlockSpec((D, tn), lambda i, j: (0, j))],
        out_specs=[pl.BlockSpec((tm, tn), lambda i, j: (i, j)),
                   pl.BlockSpec((tm, D), lambda i, j: (i, 0)),
                   pl.BlockSpec((D, tm), lambda i, j: (0, i))],
        out_shape=[jax.ShapeDtypeStruct((S, N), BF16), jax.ShapeDtypeStruct((S, D), BF16),
                   jax.ShapeDtypeStruct((D, S), BF16)],
        compiler_params=_params(),
    )(x, mod, norm_w, w)


def _matmul(a, b, *, tm, tn, name):
    M, K = a.shape
    N = b.shape[1]

    def body(a_ref, b_ref, o_ref):
        o_ref[...] = jnp.dot(a_ref[...], b_ref[...], preferred_element_type=F32).astype(o_ref.dtype)

    return pl.pallas_call(
        body, name=name, grid=(M // tm, N // tn),
        in_specs=[pl.BlockSpec((tm, K), lambda i, j: (i, 0)), pl.BlockSpec((K, tn), lambda i, j: (0, j))],
        out_specs=pl.BlockSpec((tm, tn), lambda i, j: (i, j)),
        out_shape=jax.ShapeDtypeStruct((M, N), BF16), compiler_params=_params(),
    )(a, b)


def _matmul_acc(a, b, *, tn, tk, name):
    M, K = a.shape
    N = b.shape[1]
    nk = K // tk

    def body(a_ref, b_ref, o_ref, acc_ref):
        k = pl.program_id(1)

        @pl.when(k == 0)
        def _():
            acc_ref[...] = jnp.zeros_like(acc_ref)

        acc_ref[...] += jnp.dot(a_ref[...], b_ref[...], preferred_element_type=F32)

        @pl.when(k == nk - 1)
        def _():
            o_ref[...] = acc_ref[...].astype(o_ref.dtype)

    return pl.pallas_call(
        body, name=name, grid=(N // tn, nk),
        in_specs=[pl.BlockSpec((M, tk), lambda j, k: (0, k)), pl.BlockSpec((tk, tn), lambda j, k: (k, j))],
        out_specs=pl.BlockSpec((M, tn), lambda j, k: (0, j)),
        out_shape=jax.ShapeDtypeStruct((M, N), BF16),
        scratch_shapes=[pltpu.VMEM((M, tn), F32)], compiler_params=_params(),
    )(a, b)


SHARD_COLS = 2816


def _column_tables(part):
    if part == "nat":
        bw = 256
        orig = [j * bw if j < 16 else 8704 + (j - 16) * bw for j in range(NAT // bw)]
    else:
        bw = 128
        orig = []
        for jj in range(NQKV // bw):
            g, p, t = jj // 12, (jj % 12) // 3, jj % 3
            orig.append(4096 + t * 1536 + g * 512 + p * 128)
    tk = jnp.array([o // SHARD_COLS for o in orig], jnp.int32)
    tc = jnp.array([(o % SHARD_COLS) // bw for o in orig], jnp.int32)
    return tk, tc, bw


def _weights_prep(gath, own, chipv, part):
    D = D_MODEL
    tk, tc, bw = _column_tables(part)
    n = tk.shape[0]
    per_step = 2
    assert n % per_step == 0

    def body(tk_ref, tc_ref, chip_ref, *refs):
        w_ref = refs[-1]
        for u in range(per_step):
            g_ref, o_ref = refs[2 * u], refs[2 * u + 1]
            mine = tk_ref[pl.program_id(0) * per_step + u] == chip_ref[0]
            w_ref[:, u * bw:(u + 1) * bw] = jnp.where(mine, o_ref[...], g_ref[...])

    def gath_spec(u):
        def idx(j, tk, tc, ch):
            k = tk[j * per_step + u]
            return (jnp.where(k == ch[0], (k + 1) % 4, k), 0, tc[j * per_step + u])
        return pl.BlockSpec((None, D, bw), idx)

    def own_spec(u):
        return pl.BlockSpec((D, bw), lambda j, tk, tc, ch: (0, tc[j * per_step + u]))

    specs = []
    for u in range(per_step):
        specs += [gath_spec(u), own_spec(u)]
    return pl.pallas_call(
        body, name="weights_prep_" + part,
        grid_spec=pltpu.PrefetchScalarGridSpec(
            num_scalar_prefetch=3, grid=(n // per_step,),
            in_specs=specs,
            out_specs=pl.BlockSpec((D, per_step * bw), lambda j, tk, tc, ch: (0, j))),
        out_shape=jax.ShapeDtypeStruct((D, n * bw), BF16),
        compiler_params=_params(),
    )(tk, tc, chipv, *([gath, own] * per_step))


def _pair_add_to_shards(core, gw, r, into, part):
    D = D_MODEL
    tk, tc, bw = _column_tables(part)
    n = tk.shape[0]

    def body(tk_ref, tc_ref, core_ref, g_ref, r_ref, into_ref, o_ref):
        o_ref[...] = (g_ref[...].astype(F32) + r_ref[...].astype(F32)).astype(BF16)

    return pl.pallas_call(
        body, name="grad_pair_add_" + part,
        grid_spec=pltpu.PrefetchScalarGridSpec(
            num_scalar_prefetch=3, grid=(n,),
            in_specs=[pl.BlockSpec((None, D // 2, bw), lambda j, tk, tc, c: (c[0], 0, j)),
                      pl.BlockSpec((D // 2, bw), lambda j, tk, tc, c: (0, j)), ANY],
            out_specs=pl.BlockSpec((None, D // 2, bw), lambda j, tk, tc, c: (tk[j], 0, tc[j]))),
        out_shape=jax.ShapeDtypeStruct(into.shape, BF16),
        input_output_aliases={5: 0},
        compiler_params=_params(),
    )(tk, tc, core, gw, r, into)


def _head_norm(xb, w, ebd):
    x = xb.astype(F32)
    ss = jnp.dot((x * x).astype(BF16), ebd, preferred_element_type=F32)
    return x * lax.rsqrt(ss * (1.0 / HEAD_DIM) + EPS) * w


def _window_mask(no_prev):
    qi = lax.broadcasted_iota(jnp.int32, (128, 256), 0)
    kc = lax.broadcasted_iota(jnp.int32, (128, 256), 1)
    ok = (kc >= qi) & (kc <= qi + 128)
    if no_prev is not None:
        ok = ok & ((kc >= 128) | jnp.logical_not(no_prev))
    return ok


def _lane_masks():
    lane = lax.broadcasted_iota(jnp.int32, (1, 128), 1)
    return [(lane < HEAD_DIM).astype(F32), (lane >= HEAD_DIM).astype(F32)]


def _head_col(v, mh):
    return jnp.sum(v * mh, axis=1, keepdims=True) * (1.0 / HEAD_DIM)


def _attn_fwd(projq, qnw2, knw2, ebd, g):
    S = projq.shape[0]
    d = DILATIONS[g]
    T = ATT_TILE
    nt = S // T
    nb = T // (128 * d)
    sdt = BF16 if d == 1 else F32

    def body(cur_ref, qw_ref, kw_ref, ebd_ref, o_ref, lse_ref, qs_s, ks_s, vs_s):
        t = pl.program_id(1)
        e = ebd_ref[...]

        @pl.when(t == 0)
        def _():
            ks_s[pl.ds(0, T), :] = jnp.zeros((T, 128), sdt)
            vs_s[pl.ds(0, T), :] = jnp.zeros((T, 128), sdt)

        @pl.when(t > 0)
        def _():
            ks_s[pl.ds(0, T), :] = ks_s[pl.ds(T, T), :]
            vs_s[pl.ds(0, T), :] = vs_s[pl.ds(T, T), :]

        qs_s[...] = (_head_norm(cur_ref[:, 0:128], qw_ref[...], e) * (QK_SCALE * LOG2E)).astype(sdt)
        ks_s[pl.ds(T, T), :] = _head_norm(cur_ref[:, 128:256], kw_ref[...], e).astype(sdt)
        vs_s[pl.ds(T, T), :] = cur_ref[:, 256:384].astype(sdt)
        m0, m1 = _lane_masks()
        first = t == 0
        for r in range(d):
            for bb in range(nb):
                q0 = r + bb * 128 * d
                k0 = T + r + (bb - 1) * 128 * d
                qs = qs_s[pl.ds(q0, 128, stride=d), :].astype(F32)
                kb = ks_s[pl.ds(k0, 256, stride=d), :].astype(BF16)
                vb = vs_s[pl.ds(k0, 256, stride=d), :].astype(BF16)
                ok = _window_mask(first if bb == 0 else None)
                ok2 = jnp.concatenate([ok, ok], axis=0)
                q2 = jnp.concatenate([qs * m0, qs * m1], axis=0).astype(BF16)
                s = lax.dot_general(q2, kb, (((1,), (1,)), ((), ())), preferred_element_type=F32)
                s = jnp.where(ok2, s, NEG_INF)
                mx = jnp.max(s, axis=-1, keepdims=True)
                p = jnp.exp2(s - mx)
                l = jnp.sum(p, axis=-1, keepdims=True)
                o = jnp.dot(p.astype(BF16), vb, preferred_element_type=F32) / l
                lse = mx * LN2 + jnp.log(l)
                o_ref[pl.ds(q0, 128, stride=d), :] = o[0:128] * m0 + o[128:256] * m1
                lse_ref[pl.ds(q0, 128, stride=d), :] = lse[0:128] * m0 + lse[128:256] * m1

    return pl.pallas_call(
        body, name=f"attn_fwd_g{g}", grid=(4, nt),
        in_specs=[pl.BlockSpec((T, 384), lambda p, t: (t, g * 4 + p)),
                  pl.BlockSpec((1, 128), lambda p, t: (0, 0)),
                  pl.BlockSpec((1, 128), lambda p, t: (0, 0)),
                  pl.BlockSpec((128, 128), lambda p, t: (0, 0))],
        out_specs=[pl.BlockSpec((T, 128), lambda p, t: (t, p)), pl.BlockSpec((T, 128), lambda p, t: (t, p))],
        out_shape=[jax.ShapeDtypeStruct((S, ATTN_OUT), F32), jax.ShapeDtypeStruct((S, ATTN_OUT), F32)],
        scratch_shapes=[pltpu.VMEM((T, 128), sdt), pltpu.VMEM((2 * T, 128), sdt), pltpu.VMEM((2 * T, 128), sdt)],
        compiler_params=_params(),
    )(projq, qnw2, knw2, ebd)


def _attn_bwd(projq, dprojq, d_o, lse, delta, qnw2, knw2, ebd, g):
    S = projq.shape[0]
    d = DILATIONS[g]
    T = ATT_TILE
    nt = S // T
    nb = T // (128 * d)
    sdt = F32

    def norm_bwd(raw_b, dy, w, e):
        x = raw_b.astype(F32)
        ss = jnp.dot((x * x).astype(BF16), e, preferred_element_type=F32)
        rstd = lax.rsqrt(ss * (1.0 / HEAD_DIM) + EPS)
        xh = x * rstd
        gw = jnp.sum(dy * xh, axis=0, keepdims=True)
        dxh = dy * w
        mean = jnp.dot((dxh * xh).astype(BF16), e, preferred_element_type=F32) * (1.0 / HEAD_DIM)
        return rstd * (dxh - xh * mean), gw

    nt_dims = (((1,), (1,)), ((), ()))
    tn_dims = (((0,), (0,)), ((), ()))

    def unit_stacked(qs, dos, ls, dls, kb, vb, ok, m0, m1):
        ok2 = jnp.concatenate([ok, ok], axis=0)
        q2 = jnp.concatenate([qs * m0, qs * m1], axis=0).astype(BF16)
        do2 = jnp.concatenate([dos * m0, dos * m1], axis=0).astype(BF16)
        lcol = jnp.concatenate([_head_col(ls, m0), _head_col(ls, m1)], axis=0)
        dcol = jnp.concatenate([_head_col(dls, m0), _head_col(dls, m1)], axis=0)
        s = jnp.where(ok2, lax.dot_general(q2, kb, nt_dims, preferred_element_type=F32), NEG_INF)
        p = jnp.exp2(s - lcol * LOG2E)
        dp = lax.dot_general(do2, vb, nt_dims, preferred_element_type=F32)
        ds = (p * (dp - dcol)).astype(BF16)
        dq2 = jnp.dot(ds, kb, preferred_element_type=F32)
        dk = lax.dot_general(ds, q2, tn_dims, preferred_element_type=F32)
        dv = lax.dot_general(p.astype(BF16), do2, tn_dims, preferred_element_type=F32)
        return dq2[0:128] * m0 + dq2[128:256] * m1, dk, dv

    def unit_per_head(qs, dos, ls, dls, kb, vb, ok, m0, m1):
        dq_t = dk_t = dv_t = None
        for mh in (m0, m1):
            qh = (qs * mh).astype(BF16)
            doh = (dos * mh).astype(BF16)
            s = jnp.where(ok, lax.dot_general(qh, kb, nt_dims, preferred_element_type=F32), NEG_INF)
            p = jnp.exp2(s - _head_col(ls, mh) * LOG2E)
            dp = lax.dot_general(doh, vb, nt_dims, preferred_element_type=F32)
            ds = (p * (dp - _head_col(dls, mh))).astype(BF16)
            dq = jnp.dot(ds, kb, preferred_element_type=F32) * mh
            dk = lax.dot_general(ds, qh, tn_dims, preferred_element_type=F32)
            dv = lax.dot_general(p.astype(BF16), doh, tn_dims, preferred_element_type=F32)
            dq_t = dq if dq_t is None else dq_t + dq
            dk_t = dk if dk_t is None else dk_t + dk
            dv_t = dv if dv_t is None else dv_t + dv
        return dq_t, dk_t, dv_t

    unit = unit_per_head if d == 1 else unit_stacked

    def body(cur_ref, prev_ref, do_ref, lse_ref, dl_ref, qw_ref, kw_ref, ebd_ref, dp_in_ref,
             out_ref, gqk_ref, qs_s, ks_s, vs_s, do_s, dq_cur, dq_prev, dk_acc, dv_acc):
        i = pl.program_id(1)
        e = ebd_ref[...]
        m0, m1 = _lane_masks()

        @pl.when(i == 0)
        def _():
            dk_acc[...] = jnp.zeros_like(dk_acc)
            dv_acc[...] = jnp.zeros_like(dv_acc)
            dq_prev[...] = jnp.zeros_like(dq_prev)
            gqk_ref[...] = jnp.zeros_like(gqk_ref)

        @pl.when(i < nt)
        def _():
            qs_s[...] = _head_norm(cur_ref[:, 0:128], qw_ref[...], e) * (QK_SCALE * LOG2E)
            ks_s[pl.ds(0, T), :] = _head_norm(prev_ref[:, 128:256], kw_ref[...], e)
            ks_s[pl.ds(T, T), :] = _head_norm(cur_ref[:, 128:256], kw_ref[...], e)
            vs_s[pl.ds(0, T), :] = prev_ref[:, 256:384].astype(F32)
            vs_s[pl.ds(T, T), :] = cur_ref[:, 256:384].astype(F32)
            do_s[...] = do_ref[...].astype(F32)
            first = i == 0
            for r in range(d):
                for bb in range(nb):
                    q0 = r + bb * 128 * d
                    k0 = T + r + (bb - 1) * 128 * d
                    qs = qs_s[pl.ds(q0, 128, stride=d), :]
                    dos = do_s[pl.ds(q0, 128, stride=d), :]
                    ls = lse_ref[pl.ds(q0, 128, stride=d), :]
                    dls = dl_ref[pl.ds(q0, 128, stride=d), :]
                    kb = ks_s[pl.ds(k0, 256, stride=d), :].astype(BF16)
                    vb = vs_s[pl.ds(k0, 256, stride=d), :].astype(BF16)
                    ok = _window_mask(first if bb == 0 else None)
                    dq_t, dk_t, dv_t = unit(qs, dos, ls, dls, kb, vb, ok, m0, m1)
                    dq_cur[pl.ds(q0, 128, stride=d), :] = dq_t
                    dk_acc[pl.ds(k0, 256, stride=d), :] = dk_acc[pl.ds(k0, 256, stride=d), :] + dk_t
                    dv_acc[pl.ds(k0, 256, stride=d), :] = dv_acc[pl.ds(k0, 256, stride=d), :] + dv_t

        @pl.when(i >= 1)
        def _():
            dq_raw, gq = norm_bwd(prev_ref[:, 0:128], dq_prev[...] * QK_SCALE, qw_ref[...], e)
            dk_raw, gk = norm_bwd(prev_ref[:, 128:256], dk_acc[pl.ds(0, T), :] * LN2, kw_ref[...], e)
            out_ref[:, 0:128] = dq_raw.astype(BF16)
            out_ref[:, 128:256] = dk_raw.astype(BF16)
            out_ref[:, 256:384] = dv_acc[pl.ds(0, T), :].astype(BF16)
            gqk_ref[0:1, :] += gq
            gqk_ref[1:2, :] += gk

        dq_prev[...] = dq_cur[...]
        dk_acc[pl.ds(0, T), :] = dk_acc[pl.ds(T, T), :]
        dv_acc[pl.ds(0, T), :] = dv_acc[pl.ds(T, T), :]
        dk_acc[pl.ds(T, T), :] = jnp.zeros((T, 128), F32)
        dv_acc[pl.ds(T, T), :] = jnp.zeros((T, 128), F32)

    last = nt - 1
    qtile = lambda p, i: (jnp.minimum(i, last), p)
    return pl.pallas_call(
        body, name=f"attn_bwd_g{g}", grid=(4, nt + 1),
        in_specs=[pl.BlockSpec((T, 384), lambda p, i: (jnp.minimum(i, last), g * 4 + p)),
                  pl.BlockSpec((T, 384), lambda p, i: (jnp.maximum(i - 1, 0), g * 4 + p)),
                  pl.BlockSpec((T, 128), qtile), pl.BlockSpec((T, 128), qtile), pl.BlockSpec((T, 128), qtile),
                  pl.BlockSpec((1, 128), lambda p, i: (0, 0)),
                  pl.BlockSpec((1, 128), lambda p, i: (0, 0)),
                  pl.BlockSpec((128, 128), lambda p, i: (0, 0)),
                  ANY],
        out_specs=[pl.BlockSpec((T, 384), lambda p, i: (jnp.maximum(i - 1, 0), g * 4 + p)),
                   pl.BlockSpec((None, 8, 128), lambda p, i: (p, 0, 0))],
        out_shape=[jax.ShapeDtypeStruct(dprojq.shape, BF16), jax.ShapeDtypeStruct((4, 8, 128), F32)],
        scratch_shapes=[pltpu.VMEM((T, 128), sdt), pltpu.VMEM((2 * T, 128), sdt), pltpu.VMEM((2 * T, 128), sdt),
                        pltpu.VMEM((T, 128), sdt), pltpu.VMEM((T, 128), F32), pltpu.VMEM((T, 128), F32),
                        pltpu.VMEM((2 * T, 128), F32), pltpu.VMEM((2 * T, 128), F32)],
        input_output_aliases={8: 0},
        compiler_params=_params(),
    )(projq, projq, d_o, lse, delta, qnw2, knw2, ebd, dprojq)


def _mid(projn, o_g, lse_g, x, tgt, mod, convw8, wc, wa, wo, ebd, *, tm=256):
    S, D = x.shape
    nt = S // tm
    nst = max(1, 256 // tm)
    assert nt % nst == 0
    hb = tm // 16

    def body(pn_ref, hc_ref, hx_ref, o0_ref, o1_ref, o2_ref, l0_ref, l1_ref, l2_ref, x_ref, t_ref, mod_ref,
             cw_ref, ebd_ref, wc_hbm, wa_hbm, wo_hbm,
             dpn_ref, do_ref, lse_ref, dl_ref, dout_ref, vacc_ref, gwo_hbm, gwc_hbm, gwa_hbm,
             wc_s, wa_s, wo_s, gwo_s, gwc_s, gwa_s, carry_s, *stashes):
        i = pl.program_id(0)
        ti = nt - 1 - i
        par = i % nst
        stash = pl.ds(pl.multiple_of(par * tm, tm), tm)

        @pl.when(i == 0)
        def _():
            for src, dst in ((wc_hbm, wc_s), (wa_hbm, wa_s), (wo_hbm, wo_s)):
                pltpu.sync_copy(src, dst)
            gwo_s[...] = jnp.zeros_like(gwo_s)
            gwc_s[...] = jnp.zeros_like(gwc_s)
            gwa_s[...] = jnp.zeros_like(gwa_s)
            carry_s[...] = jnp.zeros_like(carry_s)
            vacc_ref[...] = jnp.zeros_like(vacc_ref)

        rows = lax.broadcasted_iota(jnp.int32, (tm, D), 0)
        w0, w1, w2 = cw_ref[0:1, :], cw_ref[1:2, :], cw_ref[2:3, :]
        gate = mod_ref[:, 2 * D:3 * D]

        b_a = pn_ref[:, BA:BA + D].astype(F32)
        c_a = pn_ref[:, CA:CA + D].astype(F32)
        x_a = pn_ref[:, XA:XA + D].astype(F32)
        z_a = pn_ref[:, ZA:ZA + D].astype(F32)
        u = c_a * x_a
        has_prev = (ti > 0).astype(F32)
        uh = hc_ref[...].astype(F32) * hx_ref[...].astype(F32) * has_prev
        h14, h15 = _row(uh, 14), _row(uh, 15)
        u_m1 = jnp.where(rows == 0, h15, pltpu.roll(u, 1, 0))
        u_m2 = jnp.where(rows == 0, h14, jnp.where(rows == 1, h15, pltpu.roll(u, 2, 0)))
        conv = w0 * u_m2 + w1 * u_m1 + w2 * u
        sg_a = _sigmoid(z_a)
        sz_a = z_a * sg_a
        y_a = b_a * conv * sz_a
        y_ab = y_a.astype(BF16)
        pa = jnp.dot(y_ab, wc_s[...], preferred_element_type=F32)

        l0, l1, l2 = l0_ref[...], l1_ref[...], l2_ref[...]
        mxl = jnp.maximum(jnp.maximum(l0, l1), l2)
        e0, e1, e2 = jnp.exp(l0 - mxl), jnp.exp(l1 - mxl), jnp.exp(l2 - mxl)
        den = e0 + e1 + e2
        attn = (e0 * o0_ref[...] + e1 * o1_ref[...] + e2 * o2_ref[...]) / den
        lse_ref[...] = mxl + jnp.log(den)
        z_b = pn_ref[:, ZB:ZB + ATTN_OUT].astype(F32)
        sg_b = _sigmoid(z_b)
        sz_b = z_b * sg_b
        y_b = attn * sz_b
        y_bb = y_b.astype(BF16)
        pb = jnp.dot(y_bb, wa_s[...], preferred_element_type=F32)

        s_a = _sigmoid(pn_ref[:, GA:GA + D].astype(F32))
        s_b = _sigmoid(pn_ref[:, GB:GB + D].astype(F32))
        merged = s_a * pa + s_b * pb
        merged_b = merged.astype(BF16)
        mo = jnp.dot(merged_b, wo_s[...], preferred_element_type=F32)
        diff = x_ref[...] + gate * mo - t_ref[...]
        dout = diff * (1.0 / D)
        dout_ref[...] = dout
        vacc_ref[4:5, :] += jnp.sum(diff * diff, axis=0, keepdims=True)
        vacc_ref[0:1, :] += jnp.sum(dout * mo, axis=0, keepdims=True)

        d_mo = (dout * gate).astype(BF16)
        nt_dims = (((1,), (1,)), ((), ()))
        dmerged = lax.dot_general(d_mo, wo_s[...], nt_dims, preferred_element_type=F32)
        dpa = (dmerged * s_a).astype(BF16)
        dpb = (dmerged * s_b).astype(BF16)
        dpn_ref[:, GA:GA + D] = (dmerged * pa * s_a * (1.0 - s_a)).astype(BF16)
        dpn_ref[:, GB:GB + D] = (dmerged * pb * s_b * (1.0 - s_b)).astype(BF16)
        tn = (((0,), (0,)), ((), ()))
        if nst == 1:
            gwo_s[...] += lax.dot_general(merged_b, d_mo, tn, preferred_element_type=F32)
            gwc_s[...] += lax.dot_general(y_ab, dpa, tn, preferred_element_type=F32)
            gwa_s[...] += lax.dot_general(y_bb, dpb, tn, preferred_element_type=F32)
        else:
            st_m, st_d, st_ya, st_pa, st_yb, st_pb = stashes
            st_m[stash, :] = merged_b
            st_d[stash, :] = d_mo
            st_ya[stash, :] = y_ab
            st_pa[stash, :] = dpa
            st_yb[stash, :] = y_bb
            st_pb[stash, :] = dpb

            @pl.when(par == nst - 1)
            def _():
                gwo_s[...] += lax.dot_general(st_m[...], st_d[...], tn, preferred_element_type=F32)
                gwc_s[...] += lax.dot_general(st_ya[...], st_pa[...], tn, preferred_element_type=F32)
                gwa_s[...] += lax.dot_general(st_yb[...], st_pb[...], tn, preferred_element_type=F32)

        dy_a = lax.dot_general(dpa, wc_s[...], nt_dims, preferred_element_type=F32)
        dy_b = lax.dot_general(dpb, wa_s[...], nt_dims, preferred_element_type=F32)

        dattn = dy_b * sz_b
        dpn_ref[:, ZB:ZB + ATTN_OUT] = (dy_b * attn * sg_b * (1.0 + z_b * (1.0 - sg_b))).astype(BF16)
        do_ref[...] = dattn.astype(BF16)
        dl_ref[...] = jnp.dot((dattn * attn).astype(BF16), ebd_ref[...], preferred_element_type=F32)

        dpn_ref[:, BA:BA + D] = (dy_a * conv * sz_a).astype(BF16)
        dpn_ref[:, ZA:ZA + D] = (dy_a * b_a * conv * sg_a * (1.0 + z_a * (1.0 - sg_a))).astype(BF16)
        dconv = dy_a * b_a * sz_a
        vacc_ref[1:2, :] += jnp.sum(dconv * u_m2, axis=0, keepdims=True)
        vacc_ref[2:3, :] += jnp.sum(dconv * u_m1, axis=0, keepdims=True)
        vacc_ref[3:4, :] += jnp.sum(dconv * u, axis=0, keepdims=True)
        nxt = carry_s[...]
        n0, n1 = _row(nxt, 0), _row(nxt, 1)
        dc_p1 = jnp.where(rows == tm - 1, n0, pltpu.roll(dconv, tm - 1, 0))
        dc_p2 = jnp.where(rows == tm - 1, n1, jnp.where(rows == tm - 2, n0, pltpu.roll(dconv, tm - 2, 0)))
        du = w2 * dconv + w1 * dc_p1 + w0 * dc_p2
        carry_s[...] = dconv[0:8, :]
        dpn_ref[:, CA:CA + D] = (du * x_a).astype(BF16)
        dpn_ref[:, XA:XA + D] = (du * c_a).astype(BF16)

        @pl.when(i == nt - 1)
        def _():
            pltpu.sync_copy(gwo_s, gwo_hbm)
            pltpu.sync_copy(gwc_s, gwc_hbm)
            pltpu.sync_copy(gwa_s, gwa_hbm)

    rev = lambda i: (nt - 1 - i, 0)
    halo = lambda col: pl.BlockSpec((16, D), lambda i: (jnp.maximum((nt - 1 - i) * hb - 1, 0), col))
    tile512 = pl.BlockSpec((tm, ATTN_OUT), rev)
    tileD = pl.BlockSpec((tm, D), rev)
    const = lambda shape: pl.BlockSpec(shape, lambda i: (0, 0))
    return pl.pallas_call(
        body, name="mid_fwd_bwd", grid=(nt,),
        in_specs=[pl.BlockSpec((tm, NAT), rev), halo(CA // D), halo(XA // D)] + [tile512] * 6
                 + [tileD, tileD, const((1, 3 * D)), const((8, D)), const((ATTN_OUT, ATTN_OUT))] + [ANY] * 3,
        out_specs=[pl.BlockSpec((tm, NAT), rev), tile512, tile512, tile512, tileD, const((8, D)), ANY, ANY, ANY],
        out_shape=[jax.ShapeDtypeStruct((S, NAT), BF16), jax.ShapeDtypeStruct((S, ATTN_OUT), BF16),
                   jax.ShapeDtypeStruct((S, ATTN_OUT), F32), jax.ShapeDtypeStruct((S, ATTN_OUT), F32),
                   jax.ShapeDtypeStruct((S, D), F32), jax.ShapeDtypeStruct((8, D), F32),
                   jax.ShapeDtypeStruct((D, D), F32), jax.ShapeDtypeStruct((D, D), F32),
                   jax.ShapeDtypeStruct((ATTN_OUT, D), F32)],
        scratch_shapes=[pltpu.VMEM((D, D), BF16), pltpu.VMEM((ATTN_OUT, D), BF16), pltpu.VMEM((D, D), BF16),
                        pltpu.VMEM((D, D), F32), pltpu.VMEM((D, D), F32), pltpu.VMEM((ATTN_OUT, D), F32),
                        pltpu.VMEM((8, D), F32)]
                       + ([pltpu.VMEM((nst * tm, D), BF16)] * 4
                          + [pltpu.VMEM((nst * tm, ATTN_OUT), BF16), pltpu.VMEM((nst * tm, D), BF16)] if nst > 1 else []),
        compiler_params=_params(vmem=60 * 1024 * 1024),
    )(projn, projn, projn, *o_g, *lse_g, x, tgt, mod, convw8, ebd, wc, wa, wo)


def _dh_matmul(dpn, dpq, w_nat, w_qkv, parts, *, tm=512):
    S = dpn.shape[0]
    D = D_MODEL
    n = len(parts)
    nt = S // tm

    def body(dn_ref, dq_ref, wn_hbm, wq_hbm, *rest):
        p_hbm, rest = rest[:n], rest[n:]
        dh_ref = rest[0]
        q_hbm = rest[1:1 + n]
        wn_s, wq_s = rest[1 + n:3 + n]
        sems = rest[3 + n:]

        @pl.when(pl.program_id(0) == 0)
        def _():
            if n:
                mine, _ = _scatter_copies(p_hbm, q_hbm, *sems)
                for cp in mine:
                    cp.start()
            pltpu.sync_copy(wn_hbm, wn_s)
            pltpu.sync_copy(wq_hbm, wq_s)

        if n:
            @pl.when(pl.program_id(0) == nt - 1)
            def _():
                mine, theirs = _scatter_copies(p_hbm, q_hbm, *sems)
                for cp in theirs:
                    cp.wait_recv()
                for cp in mine:
                    cp.wait_send()

        nt_dims = (((1,), (1,)), ((), ()))
        dh = (lax.dot_general(dn_ref[...], wn_s[...], nt_dims, preferred_element_type=F32)
              + lax.dot_general(dq_ref[...], wq_s[...], nt_dims, preferred_element_type=F32))
        dh_ref[...] = dh.astype(BF16)

    row = lambda w: pl.BlockSpec((tm, w), lambda i: (i, 0))
    outs = pl.pallas_call(
        body, name="dh_matmul", grid=(nt,),
        in_specs=[row(NAT), row(NQKV), ANY, ANY] + [ANY] * n,
        out_specs=[row(D)] + [ANY] * n,
        out_shape=[jax.ShapeDtypeStruct((S, D), BF16)] + [jax.ShapeDtypeStruct(p.shape, p.dtype) for p in parts],
        scratch_shapes=[pltpu.VMEM((D, NAT), BF16), pltpu.VMEM((D, NQKV), BF16)]
                       + ([pltpu.SemaphoreType.DMA((n, 3)), pltpu.SemaphoreType.DMA((n, 3))] if n else []),
        compiler_params=_params(vmem=60 * 1024 * 1024),
    )(dpn, dpq, w_nat, w_qkv, *parts)
    return outs[0], outs[1:]


def _norm_bwd(dh, x, dout, mod, norm_w, *, tm=512):
    S, D = x.shape

    def body(dh_ref, x_ref, dout_ref, mod_ref, nw_ref, gx_ref, st_ref):
        @pl.when(pl.program_id(0) == 0)
        def _():
            st_ref[...] = jnp.zeros_like(st_ref)

        dh = dh_ref[...].astype(F32)
        scale1 = 1.0 + mod_ref[:, D:2 * D]
        nw = nw_ref[...]
        xv = x_ref[...]
        rstd = lax.rsqrt(jnp.mean(xv * xv, axis=-1, keepdims=True) + EPS)
        xh = xv * rstd
        dhs = dh * scale1
        dxh = dhs * nw
        dx = rstd * (dxh - xh * jnp.mean(dxh * xh, axis=-1, keepdims=True))
        gx_ref[...] = dout_ref[...] + dx
        st_ref[0:1, :] += jnp.sum(dh, axis=0, keepdims=True)
        st_ref[1:2, :] += jnp.sum(dh * (xh * nw), axis=0, keepdims=True)
        st_ref[2:3, :] += jnp.sum(dhs * xh, axis=0, keepdims=True)

    row = pl.BlockSpec((tm, D), lambda i: (i, 0))
    return pl.pallas_call(
        body, name="norm_bwd", grid=(S // tm,),
        in_specs=[row, row, row, pl.BlockSpec((1, 3 * D), lambda i: (0, 0)), pl.BlockSpec((1, D), lambda i: (0, 0))],
        out_specs=[row, pl.BlockSpec((8, D), lambda i: (0, 0))],
        out_shape=[jax.ShapeDtypeStruct((S, D), F32), jax.ShapeDtypeStruct((8, D), F32)],
        compiler_params=_params(),
    )(dh, x, dout, mod, norm_w)


def _pack_smalls(st, vacc, gqk):
    D = D_MODEL

    def body(st_ref, va_ref, gqk_ref, o_ref):
        o_ref[...] = jnp.zeros_like(o_ref)
        o_ref[0:2, :] = st_ref[0:2, :]
        o_ref[2:3, :] = va_ref[0:1, :]
        o_ref[8:9, :] = st_ref[2:3, :]
        o_ref[16:19, :] = va_ref[1:4, :]
        tot = gqk_ref[0]
        for k in range(1, gqk_ref.shape[0]):
            tot = tot + gqk_ref[k]
        tot = tot + pltpu.roll(tot, HEAD_DIM, 1)
        o_ref[24:32, 0:128] = tot
        loss = jnp.sum(va_ref[4:5, :], axis=1, keepdims=True) * (0.5 / D)
        o_ref[26:27, :] = jnp.broadcast_to(loss, (1, D))

    return pl.pallas_call(
        body, name="pack_smalls", out_shape=jax.ShapeDtypeStruct((32, D), F32),
        in_specs=[pl.BlockSpec(memory_space=pltpu.VMEM)] * 3,
        out_specs=pl.BlockSpec(memory_space=pltpu.VMEM), compiler_params=_params(),
    )(st, vacc, gqk)


def _small_update(sm_loc, dm_pad, ct_pad, w_ada, m_ada, v_ada, vec_w, vec_m, vec_v):
    D = D_MODEL
    cols = w_ada.shape[1]

    def body(sm_ref, dm_ref, ct_ref, w_ref, m_ref, v_ref, vw_ref, vm_ref, vv_ref,
             gw_ref, dw_ref, mw_ref, vw_o_ref, gv_ref, dv_ref, mv_ref, vv_o_ref):
        g = sm_ref[0]
        for k in range(1, 8):
            g = g + sm_ref[k]
        gv_ref[...] = g
        dv, mv, vv = _adamw_math(vw_ref[...], g, vm_ref[...], vv_ref[...])
        dv_ref[...] = dv
        mv_ref[...] = mv
        vv_o_ref[...] = vv
        gw = jnp.dot(ct_ref[...], dm_ref[...], preferred_element_type=F32, precision=lax.Precision.HIGHEST)
        gw_ref[...] = gw
        dw, mw, vw = _adamw_math(w_ref[...], gw, m_ref[...], v_ref[...])
        dw_ref[...] = dw
        mw_ref[...] = mw
        vw_o_ref[...] = vw

    sw = jax.ShapeDtypeStruct((D, cols), F32)
    sv = jax.ShapeDtypeStruct((32, D), F32)
    return pl.pallas_call(
        body, name="small_update", out_shape=[sw, sw, sw, sw, sv, sv, sv, sv],
        in_specs=[pl.BlockSpec(memory_space=pltpu.VMEM)] * 9,
        out_specs=[pl.BlockSpec(memory_space=pltpu.VMEM)] * 8, compiler_params=_params(),
    )(sm_loc, dm_pad, ct_pad, w_ada, m_ada, v_ada, vec_w, vec_m, vec_v)


def _local_step(xs, tg, mod, norm_w, w_nat, w_qkv, wc, wa, wo, convw8, q_norm_w, k_norm_w):
    S = xs.shape[0]
    lane = jnp.arange(128)
    ebd128 = (lane[:, None] // HEAD_DIM == lane[None, :] // HEAD_DIM).astype(BF16)
    lane5 = jnp.arange(ATTN_OUT)
    ebd512 = (lane5[:, None] // HEAD_DIM == lane5[None, :] // HEAD_DIM).astype(BF16)
    qnw2 = jnp.tile(q_norm_w, (1, 2))
    knw2 = jnp.tile(k_norm_w, (1, 2))

    projn, h, ht = _inproj_nat(xs, mod, norm_w, w_nat)
    projq = _matmul(h, w_qkv, tm=1024, tn=1536, name="inproj_qkv")
    o_g, lse_g = [], []
    for g in range(N_GROUPS):
        o, l = _attn_fwd(projq, qnw2, knw2, ebd128, g)
        o_g.append(o)
        lse_g.append(l)

    dpn, d_o, lse, delta, dout, vacc, gwo, gwc, gwa = _mid(
        projn, o_g, lse_g, xs, tg, mod, convw8, wc, wa, wo, ebd512)
    dpq = lax.empty((S, NQKV), BF16)
    gqk = []
    for g in range(N_GROUPS):
        dpq, part = _attn_bwd(projq, dpq, d_o, lse, delta, qnw2, knw2, ebd128, g)
        gqk.append(part)
    gw_nat = _matmul_acc(ht, dpn, tn=1664, tk=2048, name="grad_w_in_nat")
    gw_qkv = _matmul_acc(ht, dpq, tn=1536, tk=2048, name="grad_w_in_qkv")
    return dpn, dpq, dout, gw_nat, gw_qkv, gwc, gwa, gwo, vacc, gqk


def kernel(x, c, w_ada, b_ada, norm_w, w_in, conv_w, q_norm_w, k_norm_w, w_br_conv, w_br_attn, w_out, loss_target, m_w_ada, m_b_ada, m_norm_w, m_w_in, m_conv_w, m_q_norm_w, m_k_norm_w, m_w_br_conv, m_w_br_attn, m_w_out, v_w_ada, v_b_ada, v_norm_w, v_w_in, v_conv_w, v_q_norm_w, v_k_norm_w, v_w_br_conv, v_w_br_attn, v_w_out):
    D = D_MODEL
    S = x.shape[1]
    xs, tg = x[0], loss_target[0]
    mx, my, mc = lax.axis_index("x"), lax.axis_index("y"), lax.axis_index("c")
    chip = 2 * mx + my
    dev = 2 * chip + mc

    c_blk = jnp.concatenate([c, jnp.pad(conv_w[0], ((0, 0), (0, D - 256))), jnp.zeros((4, D), F32)], axis=0)
    b_shard = lax.dynamic_slice(b_ada, (0, chip * 768), (1, 768))
    shards = [w_in[0].astype(BF16).reshape(2, 512, 2816), w_br_conv[0].astype(BF16).reshape(2, 128, D),
              w_br_attn[0].astype(BF16).reshape(2, 256, 256), w_out[0].astype(BF16).reshape(2, 128, D)]
    gathered_in, gathered_w, cg, mod_parts, silu_blocks = _weights_allgather(
        shards[0].reshape(2, 2, 256, SHARD_COLS), shards[1:], c_blk, w_ada[0], b_shard)
    convw8 = jnp.pad(cg[::2, 1:4, 0:256].transpose(1, 0, 2).reshape(3, D), ((0, 5), (0, 0)))
    silu_c = silu_blocks[::8]
    mod_all = mod_parts[:, ::8, :].transpose(1, 0, 2).reshape(8, 3 * D)
    mod = lax.dynamic_slice(mod_all, (dev, 0), (1, 3 * D))
    g_bc, g_ba, g_wo = [lax.dynamic_update_slice(g, s[None], (chip, 0, 0, 0))
                        for g, s in zip(gathered_w, shards[1:])]
    chipv = jnp.reshape(chip, (1,)).astype(jnp.int32)
    g_in = gathered_in.reshape(4, D, SHARD_COLS)
    own_in = shards[0].reshape(D, SHARD_COLS)
    w_nat = _weights_prep(g_in, own_in, chipv, "nat")
    w_qkv = _weights_prep(g_in, own_in, chipv, "qkv")
    wc = g_bc.reshape(D, D)
    wa = g_ba.reshape(4, ATTN_OUT, 256).transpose(1, 0, 2).reshape(ATTN_OUT, D)
    wo = g_wo.reshape(D, D)

    dpn, dpq, dout, gw_nat, gw_qkv, gwc, gwa, gwo, vacc, gqk = _local_step(
        xs, tg, mod, norm_w, w_nat, w_qkv, wc, wa, wo, convw8, q_norm_w, k_norm_w)

    def halves(a):
        k, r, cc = a.shape
        return a.reshape(k, 2, r // 2, cc).transpose(1, 0, 2, 3)

    grads = [gw_nat.reshape(2, D // 2, NAT), gw_qkv.reshape(2, D // 2, NQKV),
             halves(gwc.astype(BF16).reshape(4, 256, D)),
             halves(gwa.astype(BF16).reshape(ATTN_OUT, 4, 256).transpose(1, 0, 2)),
             halves(gwo.astype(BF16).reshape(4, 256, D))]
    recv = _pair_exchange(grads)
    core = jnp.reshape(mc, (1,)).astype(jnp.int32)
    p_in = _pair_add_to_shards(core, grads[0], recv[0], lax.empty((4, D // 2, SHARD_COLS), BF16), "nat")
    p_in = _pair_add_to_shards(core, grads[1], recv[1], p_in, "qkv")
    parts = [p_in] + [_pair_add(core, gr, rc) for gr, rc in zip(grads[2:], recv[2:])]
    dh, gathered = _dh_matmul(dpn, dpq, w_nat, w_qkv, parts)
    grad_x, st = _norm_bwd(dh, xs, dout, mod, norm_w)
    mine = [_sum4(chipv, p, q) for p, q in zip(parts, gathered)]
    theirs, sm_all = _half_exchange(mine, _pack_smalls(st, vacc, jnp.concatenate(gqk, axis=0)))

    big = {}
    for t, (nm, w, m, v) in enumerate((("w_in", w_in, m_w_in, v_w_in), ("w_br_conv", w_br_conv, m_w_br_conv, v_w_br_conv),
                                       ("w_br_attn", w_br_attn, m_w_br_attn, v_w_br_attn), ("w_out", w_out, m_w_out, v_w_out))):
        big[nm] = _adamw_halves(core, w[0], mine[t], theirs[t], m[0], v[0], "adamw_" + nm)

    dmod_all = sm_all[:, 0:3, :].reshape(8, 3 * D)
    dm_pad = jnp.pad(lax.dynamic_slice(dmod_all, (0, chip * 768), (8, 768)), ((0, 120), (0, 0)))
    conv_dev = lax.dynamic_slice(sm_all[:, 16:24, :], (0, 0, chip * 256), (8, 8, 256))
    sm_loc = jnp.concatenate([sm_all[:, 0:16, :], jnp.pad(conv_dev, ((0, 0), (0, 0), (0, D - 256))), sm_all[:, 24:32, :]], axis=1)

    def pack_vec(b3, nw, cw, qw, kw):
        z = jnp.zeros((32, D), F32)
        z = z.at[0:3, :].set(b3.reshape(3, D)).at[8:9, :].set(nw).at[16:19, 0:256].set(cw)
        return z.at[24:25, 0:HEAD_DIM].set(qw).at[25:26, 0:HEAD_DIM].set(kw)

    vec_w = pack_vec(b_ada, norm_w, conv_w[0], q_norm_w, k_norm_w)
    vec_m = pack_vec(m_b_ada, m_norm_w, m_conv_w[0], m_q_norm_w, m_k_norm_w)
    vec_v = pack_vec(v_b_ada, v_norm_w, v_conv_w[0], v_q_norm_w, v_k_norm_w)
    ct_pad = jnp.pad(silu_c.T, ((0, 0), (0, 120)))
    outs = _small_update(sm_loc, dm_pad, ct_pad, w_ada[0], m_w_ada[0], v_w_ada[0], vec_w, vec_m, vec_v)
    ada = outs[0:4]
    vec = outs[4:8]
    loss = vec[0][26, 0]

    def unpack(t):
        return {"b_ada": t[0:3, :].reshape(1, 3 * D), "norm_w": t[8:9, :], "conv_w": t[16:19, 0:256][None],
                "q_norm_w": t[24:25, 0:HEAD_DIM], "k_norm_w": t[25:26, 0:HEAD_DIM]}

    small = [unpack(t) for t in vec]
    order = ["w_ada", "b_ada", "norm_w", "w_in", "conv_w", "q_norm_w", "k_norm_w", "w_br_conv", "w_br_attn", "w_out"]
    res = [loss, grad_x[None]]
    for kind in range(4):
        for nm in order:
            if nm == "w_ada":
                res.append(ada[kind][None])
            elif nm in big:
                res.append(big[nm][kind][None])
            else:
                res.append(small[kind][nm])
    return tuple(res)
```

```python
import functools

import jax
import jax.numpy as jnp
from jax import lax
from jax.experimental import pallas as pl
from jax.experimental.pallas import tpu as pltpu

F32 = jnp.float32
BF16 = jnp.bfloat16
MESH = pl.DeviceIdType.MESH
ANY = pl.BlockSpec(memory_space=pl.ANY)

D_MODEL = 1024
HEAD_DIM = 64
N_GROUPS = 3
DILATIONS = (1, 4, 16)
ATTN_OUT = 512
EPS = 1e-6
NEG_INF = -1e30
NAT = 6656
NQKV = 4608
BA, CA, XA, ZA, ZB, GA, GB = 0, 1024, 2048, 3072, 4096, 4608, 5632
ATT_TILE = 2048
QK_SCALE = HEAD_DIM ** -0.5
LOG2E = 1.4426950408889634
LN2 = 0.6931471805599453
V7X_VMEM_BYTES = 64 * 1024 * 1024
V7X_VMEM_LIMIT = 56 * 1024 * 1024

ADAM_LR = 0.001
ADAM_B1 = 0.9
ADAM_B2 = 0.999
ADAM_EPS = 1e-08
ADAM_WD = 0.01
ADAM_STEP = 10


def _params(vmem=V7X_VMEM_LIMIT, **kw):
    return pltpu.CompilerParams(vmem_limit_bytes=vmem, **kw)


def _sigmoid(z):
    return 1.0 / (1.0 + jnp.exp(-z))


def _row(v, r):
    rows = lax.broadcasted_iota(jnp.int32, v.shape, 0)
    return jnp.sum(jnp.where(rows == r, v, 0.0), axis=0, keepdims=True)


def _coords():
    return lax.axis_index("x"), lax.axis_index("y"), lax.axis_index("c")


def _flip(v, bit):
    return 1 - v if bit else v


def _weights_allgather(big_shard, shards, c_blk, w_ada, b_shard):
    n = len(shards)
    D = D_MODEL
    cols = w_ada.shape[1]

    def body(*refs):
        big_in, ins = refs[0], refs[1:1 + n]
        c_ref, wada_ref, b_ref = refs[1 + n:4 + n]
        big_out, outs = refs[4 + n], refs[5 + n:5 + 2 * n]
        cg_ref, mp_ref, sc_ref = refs[5 + 2 * n:8 + 2 * n]
        ssem, rsem, bs, br, cs, cr, ms, mr, lsem = refs[8 + 2 * n:]
        mx, my, mc = _coords()
        me = 2 * mx + my
        me8 = 2 * me + mc
        sib = (mx, my, 1 - mc)
        chips = [(_flip(mx, j & 2), _flip(my, j & 1)) for j in range(1, 4)]
        sends = []

        local = pltpu.make_async_copy(c_ref, cg_ref.at[me8], lsem)
        local.start()
        for j in range(1, 8):
            peer = (_flip(mx, j & 4), _flip(my, j & 2), _flip(mc, j & 1))
            cp = pltpu.make_async_remote_copy(
                src_ref=c_ref, dst_ref=cg_ref.at[me8], send_sem=cs.at[j - 1], recv_sem=cr.at[j - 1],
                device_id=peer, device_id_type=MESH)
            cp.start()
            sends.append(cp)

        kx, ky, kd = 2 * (1 - mx) + my, 2 * mx + (1 - my), 2 * (1 - mx) + (1 - my)
        xn, yn = (1 - mx, my, mc), (mx, 1 - my, mc)

        def big(src_chip, q, sem, to, half=None):
            h = mc if half is None else half
            return pltpu.make_async_remote_copy(
                src_ref=big_out.at[src_chip, h, q], dst_ref=big_out.at[src_chip, h, q],
                send_sem=bs.at[sem], recv_sem=br.at[sem], device_id=to, device_id_type=MESH)

        def own(q, sem, to):
            return pltpu.make_async_remote_copy(
                src_ref=big_in.at[mc, q], dst_ref=big_out.at[me, mc, q],
                send_sem=bs.at[sem], recv_sem=br.at[sem], device_id=to, device_id_type=MESH)

        for cp in (own(0, 0, xn), own(1, 3, yn), own(1, 1, xn), own(0, 2, yn)):
            cp.start()
            sends.append(cp)
        for t in range(n):
            for j, (px, py) in enumerate(chips):
                cp = pltpu.make_async_remote_copy(
                    src_ref=ins[t].at[mc], dst_ref=outs[t].at[me, mc], send_sem=ssem.at[t, j],
                    recv_sem=rsem.at[t, j], device_id=(px, py, mc), device_id_type=MESH)
                cp.start()
                sends.append(cp)

        pieces = [(kx, 0), (kx, 1), (ky, 0), (ky, 1), (kd, 0), (kd, 1)]

        def landed(piece, sem, frm, pass_on=None):
            src_chip, q = pieces[piece]
            big(src_chip, q, sem, frm).wait_recv()
            nxt = ([] if pass_on is None else [big(src_chip, q, pass_on[0], pass_on[1])])
            nxt.append(big(src_chip, q, 6 + piece, sib))
            for cp in nxt:
                cp.start()
                sends.append(cp)

        for j in range(1, 8):
            px, py, pc = _flip(mx, j & 4), _flip(my, j & 2), _flip(mc, j & 1)
            pltpu.make_async_remote_copy(
                src_ref=c_ref, dst_ref=cg_ref.at[4 * px + 2 * py + pc], send_sem=cs.at[j - 1],
                recv_sem=cr.at[j - 1], device_id=(px, py, pc), device_id_type=MESH).wait_recv()
        local.wait()
        pick = (lax.broadcasted_iota(jnp.int32, (8, 64), 1) == 8 * lax.broadcasted_iota(jnp.int32, (8, 64), 0))
        cv = jnp.dot(pick.astype(F32), cg_ref[...].reshape(64, D), preferred_element_type=F32,
                     precision=lax.Precision.HIGHEST)
        sc = cv * _sigmoid(cv)
        sc_ref[...] = sc
        mp_ref[me] = jnp.dot(sc, wada_ref[...], preferred_element_type=F32,
                             precision=lax.Precision.HIGHEST) + b_ref[...]
        for j, (px, py) in enumerate(chips):
            cp = pltpu.make_async_remote_copy(
                src_ref=mp_ref.at[me], dst_ref=mp_ref.at[me], send_sem=ms.at[j], recv_sem=mr.at[j],
                device_id=(px, py, mc), device_id_type=MESH)
            cp.start()
            sends.append(cp)

        landed(0, 0, xn, pass_on=(4, yn))
        landed(3, 3, yn, pass_on=(5, xn))
        landed(1, 1, xn)
        landed(2, 2, yn)
        for j, (px, py) in enumerate(chips):
            pltpu.make_async_remote_copy(
                src_ref=mp_ref.at[me], dst_ref=mp_ref.at[2 * px + py], send_sem=ms.at[j], recv_sem=mr.at[j],
                device_id=(px, py, mc), device_id_type=MESH).wait_recv()
        landed(4, 4, yn)
        landed(5, 5, xn)
        for t in range(n):
            for j, (px, py) in enumerate(chips):
                src = 2 * px + py
                pltpu.make_async_remote_copy(
                    src_ref=ins[t].at[mc], dst_ref=outs[t].at[src, mc], send_sem=ssem.at[t, j],
                    recv_sem=rsem.at[t, j], device_id=(px, py, mc), device_id_type=MESH).wait_recv()
                fwd = pltpu.make_async_remote_copy(
                    src_ref=outs[t].at[src, mc], dst_ref=outs[t].at[src, mc], send_sem=ssem.at[t, 3 + j],
                    recv_sem=rsem.at[t, 3 + j], device_id=sib, device_id_type=MESH)
                fwd.start()
                sends.append(fwd)
        for t in range(n):
            for j, (px, py) in enumerate(chips):
                src = 2 * px + py
                pltpu.make_async_remote_copy(
                    src_ref=outs[t].at[src, 1 - mc], dst_ref=outs[t].at[src, 1 - mc], send_sem=ssem.at[t, 3 + j],
                    recv_sem=rsem.at[t, 3 + j], device_id=sib, device_id_type=MESH).wait_recv()
        for i, (src_chip, q) in enumerate(pieces):
            big(src_chip, q, 6 + i, sib, half=1 - mc).wait_recv()
        for cp in sends:
            cp.wait_send()

    vm = pl.BlockSpec(memory_space=pltpu.VMEM)
    outs = pl.pallas_call(
        body, name="weights_allgather",
        out_shape=[jax.ShapeDtypeStruct((4,) + big_shard.shape, big_shard.dtype)]
                  + [jax.ShapeDtypeStruct((4,) + s.shape, s.dtype) for s in shards]
                  + [jax.ShapeDtypeStruct((8,) + c_blk.shape, F32), jax.ShapeDtypeStruct((4, 8, cols), F32),
                     jax.ShapeDtypeStruct((8, D), F32)],
        in_specs=[ANY] * (n + 1) + [vm] * 3, out_specs=[ANY] * (n + 1) + [vm] * 3,
        scratch_shapes=[pltpu.SemaphoreType.DMA((n, 6)), pltpu.SemaphoreType.DMA((n, 6)),
                        pltpu.SemaphoreType.DMA((12,)), pltpu.SemaphoreType.DMA((12,)),
                        pltpu.SemaphoreType.DMA((7,)), pltpu.SemaphoreType.DMA((7,)),
                        pltpu.SemaphoreType.DMA((3,)), pltpu.SemaphoreType.DMA((3,)), pltpu.SemaphoreType.DMA(())],
        compiler_params=_params(),
    )(big_shard, *shards, c_blk, w_ada, b_shard)
    return outs[0], outs[1:1 + n], outs[1 + n], outs[2 + n], outs[3 + n]


def _pair_exchange(grads):
    n = len(grads)

    def body(*refs):
        ins, outs = refs[:n], refs[n:2 * n]
        ssem, rsem = refs[2 * n:]
        mx, my, mc = _coords()
        sib = (mx, my, 1 - mc)
        sends = []
        for t in range(n):
            cp = pltpu.make_async_remote_copy(
                src_ref=ins[t].at[1 - mc], dst_ref=outs[t], send_sem=ssem.at[t], recv_sem=rsem.at[t],
                device_id=sib, device_id_type=MESH)
            cp.start()
            sends.append(cp)
        for cp in sends:
            cp.wait_recv()
        for cp in sends:
            cp.wait_send()

    return pl.pallas_call(
        body, name="grad_pair_exchange",
        out_shape=[jax.ShapeDtypeStruct(g.shape[1:], g.dtype) for g in grads],
        in_specs=[ANY] * n, out_specs=[ANY] * n,
        scratch_shapes=[pltpu.SemaphoreType.DMA((n,)), pltpu.SemaphoreType.DMA((n,))],
    )(*grads)


def _scatter_copies(ins, outs, ssem, rsem):
    mx, my, mc = _coords()
    me = 2 * mx + my
    mine, theirs = [], []
    for t in range(len(ins)):
        for j in range(1, 4):
            px, py = _flip(mx, j & 2), _flip(my, j & 1)
            mine.append(pltpu.make_async_remote_copy(
                src_ref=ins[t].at[2 * px + py], dst_ref=outs[t].at[me], send_sem=ssem.at[t, j - 1],
                recv_sem=rsem.at[t, j - 1], device_id=(px, py, mc), device_id_type=MESH))
            theirs.append(pltpu.make_async_remote_copy(
                src_ref=ins[t].at[me], dst_ref=outs[t].at[2 * px + py], send_sem=ssem.at[t, j - 1],
                recv_sem=rsem.at[t, j - 1], device_id=(px, py, mc), device_id_type=MESH))
    return mine, theirs


def _half_exchange(halves, smalls):
    n = len(halves)

    def body(*refs):
        ins, sm_ref = refs[:n], refs[n]
        outs, all_ref = refs[n + 1:2 * n + 1], refs[2 * n + 1]
        ssem, rsem, gsem, hsem, lsem = refs[2 * n + 2:]
        mx, my, mc = _coords()
        me = 4 * mx + 2 * my + mc
        sib = (mx, my, 1 - mc)
        local = pltpu.make_async_copy(sm_ref, all_ref.at[me], lsem)
        local.start()
        sends = []
        for t in range(n):
            rc = pltpu.make_async_remote_copy(
                src_ref=ins[t], dst_ref=outs[t], send_sem=ssem.at[t], recv_sem=rsem.at[t],
                device_id=sib, device_id_type=MESH)
            rc.start()
            sends.append(rc)
        gathers = []
        for j in range(1, 8):
            peer = (_flip(mx, j & 4), _flip(my, j & 2), _flip(mc, j & 1))
            cp = pltpu.make_async_remote_copy(
                src_ref=sm_ref, dst_ref=all_ref.at[me], send_sem=gsem.at[j - 1], recv_sem=hsem.at[j - 1],
                device_id=peer, device_id_type=MESH)
            cp.start()
            gathers.append(cp)
        for cp in sends:
            cp.wait_recv()
        for j in range(1, 8):
            px, py, pc = _flip(mx, j & 4), _flip(my, j & 2), _flip(mc, j & 1)
            pltpu.make_async_remote_copy(
                src_ref=sm_ref, dst_ref=all_ref.at[4 * px + 2 * py + pc], send_sem=gsem.at[j - 1],
                recv_sem=hsem.at[j - 1], device_id=(px, py, pc), device_id_type=MESH).wait_recv()
        for cp in sends + gathers:
            cp.wait_send()
        local.wait()

    vm = pl.BlockSpec(memory_space=pltpu.VMEM)
    outs = pl.pallas_call(
        body, name="grad_half_exchange",
        out_shape=[jax.ShapeDtypeStruct(h.shape, h.dtype) for h in halves]
                  + [jax.ShapeDtypeStruct((8,) + smalls.shape, smalls.dtype)],
        in_specs=[ANY] * n + [vm], out_specs=[ANY] * n + [vm],
        scratch_shapes=[pltpu.SemaphoreType.DMA((n,)), pltpu.SemaphoreType.DMA((n,)),
                        pltpu.SemaphoreType.DMA((7,)), pltpu.SemaphoreType.DMA((7,)), pltpu.SemaphoreType.DMA(())],
    )(*halves, smalls)
    return outs[:n], outs[n]


def _row_tile(rows, cols, bytes_per_row_elem=4, budget=2 * 1024 * 1024):
    t = rows
    while t % 2 == 0 and t > 16 and t * cols * bytes_per_row_elem > budget:
        t //= 2
    return t


def _pair_add(core, g, r):
    _, _, rh, cc = g.shape
    tr = _row_tile(rh, cc)

    def body(core_ref, g_ref, r_ref, o_ref):
        o_ref[...] = (g_ref[...].astype(F32) + r_ref[...].astype(F32)).astype(o_ref.dtype)

    return pl.pallas_call(
        body, name="grad_pair_add",
        grid_spec=pltpu.PrefetchScalarGridSpec(
            num_scalar_prefetch=1, grid=(4, rh // tr),
            in_specs=[pl.BlockSpec((None, None, tr, cc), lambda k, i, c: (c[0], k, i, 0)),
                      pl.BlockSpec((None, tr, cc), lambda k, i, c: (k, i, 0))],
            out_specs=pl.BlockSpec((None, tr, cc), lambda k, i, c: (k, i, 0))),
        out_shape=jax.ShapeDtypeStruct(r.shape, BF16),
        compiler_params=_params(),
    )(core, g, r)


def _sum4(chip, p, q):
    _, rh, cc = q.shape
    tr = _row_tile(rh, cc)

    def body(chip_ref, p_ref, q1_ref, q2_ref, q3_ref, o_ref):
        o_ref[...] = ((p_ref[...].astype(F32) + q1_ref[...].astype(F32)) + q2_ref[...].astype(F32)) + q3_ref[...].astype(F32)

    def other(j):
        return pl.BlockSpec((None, tr, cc), lambda i, c: (jnp.bitwise_xor(c[0], j), i, 0))

    return pl.pallas_call(
        body, name="grad_sum4",
        grid_spec=pltpu.PrefetchScalarGridSpec(
            num_scalar_prefetch=1, grid=(rh // tr,),
            in_specs=[pl.BlockSpec((None, tr, cc), lambda i, c: (c[0], i, 0)), other(1), other(2), other(3)],
            out_specs=pl.BlockSpec((tr, cc), lambda i, c: (i, 0))),
        out_shape=jax.ShapeDtypeStruct((rh, cc), F32),
        compiler_params=_params(),
    )(chip, p, q, q, q)


def _adamw_math(w, g, m, v):
    m = ADAM_B1 * m + (1.0 - ADAM_B1) * g
    v = ADAM_B2 * v + (1.0 - ADAM_B2) * (g * g)
    m_hat = m / (1.0 - ADAM_B1 ** ADAM_STEP)
    v_hat = v / (1.0 - ADAM_B2 ** ADAM_STEP)
    delta = -ADAM_LR * (m_hat / (jnp.sqrt(v_hat) + ADAM_EPS) + ADAM_WD * w)
    return delta, m, v


def _adamw_halves(core, w, mine, recv, m, v, name):
    rows, cols = w.shape
    rh = rows // 2
    tr = _row_tile(rh, cols, budget=1024 * 1024)
    nh = rh // tr

    def body(core_ref, w_ref, a_ref, b_ref, m_ref, v_ref, g_ref, d_ref, mo_ref, vo_ref):
        g = jnp.where(pl.program_id(0) == core_ref[0], a_ref[...], b_ref[...])
        d, mn, vn = _adamw_math(w_ref[...], g, m_ref[...], v_ref[...])
        g_ref[...] = g
        d_ref[...] = d
        mo_ref[...] = mn
        vo_ref[...] = vn

    full = pl.BlockSpec((tr, cols), lambda h, i, c: (h * nh + i, 0))
    half = pl.BlockSpec((tr, cols), lambda h, i, c: (i, 0))
    sd = jax.ShapeDtypeStruct((rows, cols), F32)
    return pl.pallas_call(
        body, name=name,
        grid_spec=pltpu.PrefetchScalarGridSpec(
            num_scalar_prefetch=1, grid=(2, nh),
            in_specs=[full, half, half, full, full], out_specs=[full] * 4),
        out_shape=[sd, sd, sd, sd], compiler_params=_params(),
    )(core, w, mine, recv, m, v)


def _inproj_nat(x, mod, norm_w, w, *, tm=1024, tn=1664):
    S, D = x.shape
    N = w.shape[1]
    nt, nj = S // tm, N // tn
    chunk = 256

    def body(x_ref, mod_ref, nw_ref, w_ref, proj_ref, h_ref, ht_ref, h_a, ht_a, h_b, ht_b):
        i, j = pl.program_id(0), pl.program_id(1)

        def make_h(h_s, ht_s):
            shift = mod_ref[:, 0:D]
            scale1 = 1.0 + mod_ref[:, D:2 * D]
            for r in range(tm // chunk):
                xv = x_ref[pl.ds(r * chunk, chunk), :]
                ms = jnp.mean(xv * xv, axis=-1, keepdims=True)
                h = (xv * lax.rsqrt(ms + EPS) * nw_ref[...]) * scale1 + shift
                h_s[pl.ds(r * chunk, chunk), :] = h.astype(BF16)
                ht_s[:, pl.ds(r * chunk, chunk)] = h.T.astype(BF16)

        @pl.when((i == 0) & (j == 0))
        def _():
            make_h(h_a, ht_a)

        def step(cur, nxt):
            @pl.when(j == 0)
            def _():
                h_ref[...] = cur[0][...]
                ht_ref[...] = cur[1][...]

            def project():
                proj_ref[...] = jnp.dot(cur[0][...], w_ref[...], preferred_element_type=F32).astype(BF16)

            ahead = (j == nj - 1) & (i < nt - 1)

            @pl.when(ahead)
            def _():
                project()
                make_h(*nxt)

            @pl.when(jnp.logical_not(ahead))
            def _():
                project()

        @pl.when(i % 2 == 0)
        def _():
            step((h_a, ht_a), (h_b, ht_b))

        @pl.when(i % 2 == 1)
        def _():
            step((h_b, ht_b), (h_a, ht_a))

    def x_tile(i, j):
        return (jnp.where((i == 0) & (j == 0), 0, jnp.minimum(i + 1, nt - 1)), 0)

    return pl.pallas_call(
        body, name="inproj_nat", grid=(nt, nj),
        in_specs=[pl.BlockSpec((tm, D), x_tile),
                  pl.BlockSpec((1, 3 * D), lambda i, j: (0, 0)),
                  pl.BlockSpec((1, D), lambda i, j: (0, 0)),
                  pl.BlockSpec((D, tn), lambda i, j: (0, j))],
        out_specs=[pl.BlockSpec((tm, tn), lambda i, j: (i, j)),
                   pl.BlockSpec((tm, D), lambda i, j: (i, 0)),
                   pl.BlockSpec((D, tm), lambda i, j: (0, i))],
        out_shape=[jax.ShapeDtypeStruct((S, N), BF16), jax.ShapeDtypeStruct((S, D), BF16),
                   jax.ShapeDtypeStruct((D, S), BF16)],
        scratch_shapes=[pltpu.VMEM((tm, D), BF16), pltpu.VMEM((D, tm), BF16),
                        pltpu.VMEM((tm, D), BF16), pltpu.VMEM((D, tm), BF16)],
        compiler_params=_params(),
    )(x, mod, norm_w, w)


def _matmul(a, b, *, tm, tn, name):
    M, K = a.shape
    N = b.shape[1]

    def body(a_ref, b_ref, o_ref):
        o_ref[...] = jnp.dot(a_ref[...], b_ref[...], preferred_element_type=F32).astype(o_ref.dtype)

    return pl.pallas_call(
        body, name=name, grid=(M // tm, N // tn),
        in_specs=[pl.BlockSpec((tm, K), lambda i, j: (i, 0)), pl.BlockSpec((K, tn), lambda i, j: (0, j))],
        out_specs=pl.BlockSpec((tm, tn), lambda i, j: (i, j)),
        out_shape=jax.ShapeDtypeStruct((M, N), BF16), compiler_params=_params(),
    )(a, b)


def _matmul_acc(a, b, *, tn, tk, name):
    M, K = a.shape
    N = b.shape[1]
    nk = K // tk

    def body(a_ref, b_ref, o_ref, acc_ref):
        k = pl.program_id(1)

        @pl.when(k == 0)
        def _():
            acc_ref[...] = jnp.zeros_like(acc_ref)

        acc_ref[...] += jnp.dot(a_ref[...], b_ref[...], preferred_element_type=F32)

        @pl.when(k == nk - 1)
        def _():
            o_ref[...] = acc_ref[...].astype(o_ref.dtype)

    return pl.pallas_call(
        body, name=name, grid=(N // tn, nk),
        in_specs=[pl.BlockSpec((M, tk), lambda j, k: (0, k)), pl.BlockSpec((tk, tn), lambda j, k: (k, j))],
        out_specs=pl.BlockSpec((M, tn), lambda j, k: (0, j)),
        out_shape=jax.ShapeDtypeStruct((M, N), BF16),
        scratch_shapes=[pltpu.VMEM((M, tn), F32)], compiler_params=_params(),
    )(a, b)


SHARD_COLS = 2816


def _column_tables(part):
    if part == "nat":
        bw = 256
        orig = [j * bw if j < 16 else 8704 + (j - 16) * bw for j in range(NAT // bw)]
    else:
        bw = 128
        orig = []
        for jj in range(NQKV // bw):
            g, p, t = jj // 12, (jj % 12) // 3, jj % 3
            orig.append(4096 + t * 1536 + g * 512 + p * 128)
    tk = jnp.array([o // SHARD_COLS for o in orig], jnp.int32)
    tc = jnp.array([(o % SHARD_COLS) // bw for o in orig], jnp.int32)
    return tk, tc, bw


def _weights_prep(gath, own, chipv, part):
    D = D_MODEL
    tk, tc, bw = _column_tables(part)
    n = tk.shape[0]
    per_step = 2
    assert n % per_step == 0

    def body(tk_ref, tc_ref, chip_ref, *refs):
        w_ref = refs[-1]
        for u in range(per_step):
            g_ref, o_ref = refs[2 * u], refs[2 * u + 1]
            mine = tk_ref[pl.program_id(0) * per_step + u] == chip_ref[0]
            w_ref[:, u * bw:(u + 1) * bw] = jnp.where(mine, o_ref[...], g_ref[...])

    def gath_spec(u):
        def idx(j, tk, tc, ch):
            k = tk[j * per_step + u]
            return (jnp.where(k == ch[0], (k + 1) % 4, k), 0, tc[j * per_step + u])
        return pl.BlockSpec((None, D, bw), idx)

    def own_spec(u):
        return pl.BlockSpec((D, bw), lambda j, tk, tc, ch: (0, tc[j * per_step + u]))

    specs = []
    for u in range(per_step):
        specs += [gath_spec(u), own_spec(u)]
    return pl.pallas_call(
        body, name="weights_prep_" + part,
        grid_spec=pltpu.PrefetchScalarGridSpec(
            num_scalar_prefetch=3, grid=(n // per_step,),
            in_specs=specs,
            out_specs=pl.BlockSpec((D, per_step * bw), lambda j, tk, tc, ch: (0, j))),
        out_shape=jax.ShapeDtypeStruct((D, n * bw), BF16),
        compiler_params=_params(),
    )(tk, tc, chipv, *([gath, own] * per_step))


def _pair_add_to_shards(core, gw, r, into, part):
    D = D_MODEL
    tk, tc, bw = _column_tables(part)
    n = tk.shape[0]

    def body(tk_ref, tc_ref, core_ref, g_ref, r_ref, into_ref, o_ref):
        o_ref[...] = (g_ref[...].astype(F32) + r_ref[...].astype(F32)).astype(BF16)

    return pl.pallas_call(
        body, name="grad_pair_add_" + part,
        grid_spec=pltpu.PrefetchScalarGridSpec(
            num_scalar_prefetch=3, grid=(n,),
            in_specs=[pl.BlockSpec((None, D // 2, bw), lambda j, tk, tc, c: (c[0], 0, j)),
                      pl.BlockSpec((D // 2, bw), lambda j, tk, tc, c: (0, j)), ANY],
            out_specs=pl.BlockSpec((None, D // 2, bw), lambda j, tk, tc, c: (tk[j], 0, tc[j]))),
        out_shape=jax.ShapeDtypeStruct(into.shape, BF16),
        input_output_aliases={5: 0},
        compiler_params=_params(),
    )(tk, tc, core, gw, r, into)


def _head_norm(xb, w, ebd):
    x = xb.astype(F32)
    ss = jnp.dot((x * x).astype(BF16), ebd, preferred_element_type=F32)
    return x * lax.rsqrt(ss * (1.0 / HEAD_DIM) + EPS) * w


def _window_mask(no_prev):
    qi = lax.broadcasted_iota(jnp.int32, (128, 256), 0)
    kc = lax.broadcasted_iota(jnp.int32, (128, 256), 1)
    ok = (kc >= qi) & (kc <= qi + 128)
    if no_prev is not None:
        ok = ok & ((kc >= 128) | jnp.logical_not(no_prev))
    return ok


def _lane_masks():
    lane = lax.broadcasted_iota(jnp.int32, (1, 128), 1)
    return [(lane < HEAD_DIM).astype(F32), (lane >= HEAD_DIM).astype(F32)]


def _head_col(v, mh):
    return jnp.sum(v * mh, axis=1, keepdims=True) * (1.0 / HEAD_DIM)


def _attn_fwd(projq, qnw2, knw2, ebd, g):
    S = projq.shape[0]
    d = DILATIONS[g]
    T = ATT_TILE
    nt = S // T
    nb = T // (128 * d)
    sdt = BF16 if d == 1 else F32

    def body(cur_ref, qw_ref, kw_ref, ebd_ref, o_ref, lse_ref, qs_s, ks_s, vs_s):
        t = pl.program_id(1)
        e = ebd_ref[...]

        @pl.when(t == 0)
        def _():
            ks_s[pl.ds(0, T), :] = jnp.zeros((T, 128), sdt)
            vs_s[pl.ds(0, T), :] = jnp.zeros((T, 128), sdt)

        @pl.when(t > 0)
        def _():
            ks_s[pl.ds(0, T), :] = ks_s[pl.ds(T, T), :]
            vs_s[pl.ds(0, T), :] = vs_s[pl.ds(T, T), :]

        qs_s[...] = (_head_norm(cur_ref[:, 0:128], qw_ref[...], e) * (QK_SCALE * LOG2E)).astype(sdt)
        ks_s[pl.ds(T, T), :] = _head_norm(cur_ref[:, 128:256], kw_ref[...], e).astype(sdt)
        vs_s[pl.ds(T, T), :] = cur_ref[:, 256:384].astype(sdt)
        m0, m1 = _lane_masks()
        first = t == 0
        for r in range(d):
            for bb in range(nb):
                q0 = r + bb * 128 * d
                k0 = T + r + (bb - 1) * 128 * d
                qs = qs_s[pl.ds(q0, 128, stride=d), :].astype(F32)
                kb = ks_s[pl.ds(k0, 256, stride=d), :].astype(BF16)
                vb = vs_s[pl.ds(k0, 256, stride=d), :].astype(BF16)
                ok = _window_mask(first if bb == 0 else None)
                ok2 = jnp.concatenate([ok, ok], axis=0)
                q2 = jnp.concatenate([qs * m0, qs * m1], axis=0).astype(BF16)
                s = lax.dot_general(q2, kb, (((1,), (1,)), ((), ())), preferred_element_type=F32)
                s = jnp.where(ok2, s, NEG_INF)
                mx = jnp.max(s, axis=-1, keepdims=True)
                p = jnp.exp2(s - mx)
                l = jnp.sum(p, axis=-1, keepdims=True)
                o = jnp.dot(p.astype(BF16), vb, preferred_element_type=F32) / l
                lse = mx * LN2 + jnp.log(l)
                o_ref[pl.ds(q0, 128, stride=d), :] = o[0:128] * m0 + o[128:256] * m1
                lse_ref[pl.ds(q0, 128, stride=d), :] = lse[0:128] * m0 + lse[128:256] * m1

    return pl.pallas_call(
        body, name=f"attn_fwd_g{g}", grid=(4, nt),
        in_specs=[pl.BlockSpec((T, 384), lambda p, t: (t, g * 4 + p)),
                  pl.BlockSpec((1, 128), lambda p, t: (0, 0)),
                  pl.BlockSpec((1, 128), lambda p, t: (0, 0)),
                  pl.BlockSpec((128, 128), lambda p, t: (0, 0))],
        out_specs=[pl.BlockSpec((T, 128), lambda p, t: (t, p)), pl.BlockSpec((T, 128), lambda p, t: (t, p))],
        out_shape=[jax.ShapeDtypeStruct((S, ATTN_OUT), F32), jax.ShapeDtypeStruct((S, ATTN_OUT), F32)],
        scratch_shapes=[pltpu.VMEM((T, 128), sdt), pltpu.VMEM((2 * T, 128), sdt), pltpu.VMEM((2 * T, 128), sdt)],
        compiler_params=_params(),
    )(projq, qnw2, knw2, ebd)


def _attn_bwd(projq, dprojq, d_o, lse, delta, qnw2, knw2, ebd, g):
    S = projq.shape[0]
    d = DILATIONS[g]
    T = ATT_TILE
    nt = S // T
    nb = T // (128 * d)
    sdt = F32

    def norm_bwd(raw_b, dy, w, e):
        x = raw_b.astype(F32)
        ss = jnp.dot((x * x).astype(BF16), e, preferred_element_type=F32)
        rstd = lax.rsqrt(ss * (1.0 / HEAD_DIM) + EPS)
        xh = x * rstd
        gw = jnp.sum(dy * xh, axis=0, keepdims=True)
        dxh = dy * w
        mean = jnp.dot((dxh * xh).astype(BF16), e, preferred_element_type=F32) * (1.0 / HEAD_DIM)
        return rstd * (dxh - xh * mean), gw

    nt_dims = (((1,), (1,)), ((), ()))
    tn_dims = (((0,), (0,)), ((), ()))

    def unit_stacked(qs, dos, ls, dls, kb, vb, ok, m0, m1):
        ok2 = jnp.concatenate([ok, ok], axis=0)
        q2 = jnp.concatenate([qs * m0, qs * m1], axis=0).astype(BF16)
        do2 = jnp.concatenate([dos * m0, dos * m1], axis=0).astype(BF16)
        lcol = jnp.concatenate([_head_col(ls, m0), _head_col(ls, m1)], axis=0)
        dcol = jnp.concatenate([_head_col(dls, m0), _head_col(dls, m1)], axis=0)
        s = jnp.where(ok2, lax.dot_general(q2, kb, nt_dims, preferred_element_type=F32), NEG_INF)
        p = jnp.exp2(s - lcol * LOG2E)
        dp = lax.dot_general(do2, vb, nt_dims, preferred_element_type=F32)
        ds = (p * (dp - dcol)).astype(BF16)
        dq2 = jnp.dot(ds, kb, preferred_element_type=F32)
        dk = lax.dot_general(ds, q2, tn_dims, preferred_element_type=F32)
        dv = lax.dot_general(p.astype(BF16), do2, tn_dims, preferred_element_type=F32)
        return dq2[0:128] * m0 + dq2[128:256] * m1, dk, dv

    def unit_per_head(qs, dos, ls, dls, kb, vb, ok, m0, m1):
        dq_t = dk_t = dv_t = None
        for mh in (m0, m1):
            qh = (qs * mh).astype(BF16)
            doh = (dos * mh).astype(BF16)
            s = jnp.where(ok, lax.dot_general(qh, kb, nt_dims, preferred_element_type=F32), NEG_INF)
            p = jnp.exp2(s - _head_col(ls, mh) * LOG2E)
            dp = lax.dot_general(doh, vb, nt_dims, preferred_element_type=F32)
            ds = (p * (dp - _head_col(dls, mh))).astype(BF16)
            dq = jnp.dot(ds, kb, preferred_element_type=F32) * mh
            dk = lax.dot_general(ds, qh, tn_dims, preferred_element_type=F32)
            dv = lax.dot_general(p.astype(BF16), doh, tn_dims, preferred_element_type=F32)
            dq_t = dq if dq_t is None else dq_t + dq
            dk_t = dk if dk_t is None else dk_t + dk
            dv_t = dv if dv_t is None else dv_t + dv
        return dq_t, dk_t, dv_t

    unit = unit_per_head if d == 1 else unit_stacked

    def body(cur_ref, prev_ref, do_ref, lse_ref, dl_ref, qw_ref, kw_ref, ebd_ref, dp_in_ref,
             out_ref, gqk_ref, qs_s, ks_s, vs_s, do_s, dq_cur, dq_prev, dk_acc, dv_acc):
        i = pl.program_id(1)
        e = ebd_ref[...]
        m0, m1 = _lane_masks()

        @pl.when(i == 0)
        def _():
            dk_acc[...] = jnp.zeros_like(dk_acc)
            dv_acc[...] = jnp.zeros_like(dv_acc)
            dq_prev[...] = jnp.zeros_like(dq_prev)
            gqk_ref[...] = jnp.zeros_like(gqk_ref)

        @pl.when(i < nt)
        def _():
            qs_s[...] = _head_norm(cur_ref[:, 0:128], qw_ref[...], e) * (QK_SCALE * LOG2E)
            ks_s[pl.ds(0, T), :] = _head_norm(prev_ref[:, 128:256], kw_ref[...], e)
            ks_s[pl.ds(T, T), :] = _head_norm(cur_ref[:, 128:256], kw_ref[...], e)
            vs_s[pl.ds(0, T), :] = prev_ref[:, 256:384].astype(F32)
            vs_s[pl.ds(T, T), :] = cur_ref[:, 256:384].astype(F32)
            do_s[...] = do_ref[...].astype(F32)
            first = i == 0
            for r in range(d):
                for bb in range(nb):
                    q0 = r + bb * 128 * d
                    k0 = T + r + (bb - 1) * 128 * d
                    qs = qs_s[pl.ds(q0, 128, stride=d), :]
                    dos = do_s[pl.ds(q0, 128, stride=d), :]
                    ls = lse_ref[pl.ds(q0, 128, stride=d), :]
                    dls = dl_ref[pl.ds(q0, 128, stride=d), :]
                    kb = ks_s[pl.ds(k0, 256, stride=d), :].astype(BF16)
                    vb = vs_s[pl.ds(k0, 256, stride=d), :].astype(BF16)
                    ok = _window_mask(first if bb == 0 else None)
                    dq_t, dk_t, dv_t = unit(qs, dos, ls, dls, kb, vb, ok, m0, m1)
                    dq_cur[pl.ds(q0, 128, stride=d), :] = dq_t
                    dk_acc[pl.ds(k0, 256, stride=d), :] = dk_acc[pl.ds(k0, 256, stride=d), :] + dk_t
                    dv_acc[pl.ds(k0, 256, stride=d), :] = dv_acc[pl.ds(k0, 256, stride=d), :] + dv_t

        @pl.when(i >= 1)
        def _():
            dq_raw, gq = norm_bwd(prev_ref[:, 0:128], dq_prev[...] * QK_SCALE, qw_ref[...], e)
            dk_raw, gk = norm_bwd(prev_ref[:, 128:256], dk_acc[pl.ds(0, T), :] * LN2, kw_ref[...], e)
            out_ref[:, 0:128] = dq_raw.astype(BF16)
            out_ref[:, 128:256] = dk_raw.astype(BF16)
            out_ref[:, 256:384] = dv_acc[pl.ds(0, T), :].astype(BF16)
            gqk_ref[0:1, :] += gq
            gqk_ref[1:2, :] += gk

        dq_prev[...] = dq_cur[...]
        dk_acc[pl.ds(0, T), :] = dk_acc[pl.ds(T, T), :]
        dv_acc[pl.ds(0, T), :] = dv_acc[pl.ds(T, T), :]
        dk_acc[pl.ds(T, T), :] = jnp.zeros((T, 128), F32)
        dv_acc[pl.ds(T, T), :] = jnp.zeros((T, 128), F32)

    last = nt - 1
    qtile = lambda p, i: (jnp.minimum(i, last), p)
    return pl.pallas_call(
        body, name=f"attn_bwd_g{g}", grid=(4, nt + 1),
        in_specs=[pl.BlockSpec((T, 384), lambda p, i: (jnp.minimum(i, last), g * 4 + p)),
                  pl.BlockSpec((T, 384), lambda p, i: (jnp.maximum(i - 1, 0), g * 4 + p)),
                  pl.BlockSpec((T, 128), qtile), pl.BlockSpec((T, 128), qtile), pl.BlockSpec((T, 128), qtile),
                  pl.BlockSpec((1, 128), lambda p, i: (0, 0)),
                  pl.BlockSpec((1, 128), lambda p, i: (0, 0)),
                  pl.BlockSpec((128, 128), lambda p, i: (0, 0)),
                  ANY],
        out_specs=[pl.BlockSpec((T, 384), lambda p, i: (jnp.maximum(i - 1, 0), g * 4 + p)),
                   pl.BlockSpec((None, 8, 128), lambda p, i: (p, 0, 0))],
        out_shape=[jax.ShapeDtypeStruct(dprojq.shape, BF16), jax.ShapeDtypeStruct((4, 8, 128), F32)],
        scratch_shapes=[pltpu.VMEM((T, 128), sdt), pltpu.VMEM((2 * T, 128), sdt), pltpu.VMEM((2 * T, 128), sdt),
                        pltpu.VMEM((T, 128), sdt), pltpu.VMEM((T, 128), F32), pltpu.VMEM((T, 128), F32),
                        pltpu.VMEM((2 * T, 128), F32), pltpu.VMEM((2 * T, 128), F32)],
        input_output_aliases={8: 0},
        compiler_params=_params(),
    )(projq, projq, d_o, lse, delta, qnw2, knw2, ebd, dprojq)


def _mid(projn, o_g, lse_g, x, tgt, mod, convw8, wc, wa, wo, ebd, *, tm=256):
    S, D = x.shape
    nt = S // tm
    nst = max(1, 256 // tm)
    assert nt % nst == 0
    hb = tm // 16

    def body(pn_ref, hc_ref, hx_ref, o0_ref, o1_ref, o2_ref, l0_ref, l1_ref, l2_ref, x_ref, t_ref, mod_ref,
             cw_ref, ebd_ref, wc_hbm, wa_hbm, wo_hbm,
             dpn_ref, do_ref, lse_ref, dl_ref, dout_ref, vacc_ref, gwo_hbm, gwc_hbm, gwa_hbm,
             wc_s, wa_s, wo_s, gwo_s, gwc_s, gwa_s, carry_s, *stashes):
        i = pl.program_id(0)
        ti = nt - 1 - i
        par = i % nst
        stash = pl.ds(pl.multiple_of(par * tm, tm), tm)

        @pl.when(i == 0)
        def _():
            for src, dst in ((wc_hbm, wc_s), (wa_hbm, wa_s), (wo_hbm, wo_s)):
                pltpu.sync_copy(src, dst)
            gwo_s[...] = jnp.zeros_like(gwo_s)
            gwc_s[...] = jnp.zeros_like(gwc_s)
            gwa_s[...] = jnp.zeros_like(gwa_s)
            carry_s[...] = jnp.zeros_like(carry_s)
            vacc_ref[...] = jnp.zeros_like(vacc_ref)

        rows = lax.broadcasted_iota(jnp.int32, (tm, D), 0)
        w0, w1, w2 = cw_ref[0:1, :], cw_ref[1:2, :], cw_ref[2:3, :]
        gate = mod_ref[:, 2 * D:3 * D]

        b_a = pn_ref[:, BA:BA + D].astype(F32)
        c_a = pn_ref[:, CA:CA + D].astype(F32)
        x_a = pn_ref[:, XA:XA + D].astype(F32)
        z_a = pn_ref[:, ZA:ZA + D].astype(F32)
        u = c_a * x_a
        has_prev = (ti > 0).astype(F32)
        uh = hc_ref[...].astype(F32) * hx_ref[...].astype(F32) * has_prev
        h14, h15 = _row(uh, 14), _row(uh, 15)
        u_m1 = jnp.where(rows == 0, h15, pltpu.roll(u, 1, 0))
        u_m2 = jnp.where(rows == 0, h14, jnp.where(rows == 1, h15, pltpu.roll(u, 2, 0)))
        conv = w0 * u_m2 + w1 * u_m1 + w2 * u
        sg_a = _sigmoid(z_a)
        sz_a = z_a * sg_a
        y_a = b_a * conv * sz_a
        y_ab = y_a.astype(BF16)
        pa = jnp.dot(y_ab, wc_s[...], preferred_element_type=F32)

        l0, l1, l2 = l0_ref[...], l1_ref[...], l2_ref[...]
        mxl = jnp.maximum(jnp.maximum(l0, l1), l2)
        e0, e1, e2 = jnp.exp(l0 - mxl), jnp.exp(l1 - mxl), jnp.exp(l2 - mxl)
        den = e0 + e1 + e2
        attn = (e0 * o0_ref[...] + e1 * o1_ref[...] + e2 * o2_ref[...]) / den
        lse_ref[...] = mxl + jnp.log(den)
        z_b = pn_ref[:, ZB:ZB + ATTN_OUT].astype(F32)
        sg_b = _sigmoid(z_b)
        sz_b = z_b * sg_b
        y_b = attn * sz_b
        y_bb = y_b.astype(BF16)
        pb = jnp.dot(y_bb, wa_s[...], preferred_element_type=F32)

        s_a = _sigmoid(pn_ref[:, GA:GA + D].astype(F32))
        s_b = _sigmoid(pn_ref[:, GB:GB + D].astype(F32))
        merged = s_a * pa + s_b * pb
        merged_b = merged.astype(BF16)
        mo = jnp.dot(merged_b, wo_s[...], preferred_element_type=F32)
        diff = x_ref[...] + gate * mo - t_ref[...]
        dout = diff * (1.0 / D)
        dout_ref[...] = dout
        vacc_ref[4:5, :] += jnp.sum(diff * diff, axis=0, keepdims=True)
        vacc_ref[0:1, :] += jnp.sum(dout * mo, axis=0, keepdims=True)

        d_mo = (dout * gate).astype(BF16)
        nt_dims = (((1,), (1,)), ((), ()))
        dmerged = lax.dot_general(d_mo, wo_s[...], nt_dims, preferred_element_type=F32)
        dpa = (dmerged * s_a).astype(BF16)
        dpb = (dmerged * s_b).astype(BF16)
        dpn_ref[:, GA:GA + D] = (dmerged * pa * s_a * (1.0 - s_a)).astype(BF16)
        dpn_ref[:, GB:GB + D] = (dmerged * pb * s_b * (1.0 - s_b)).astype(BF16)
        tn = (((0,), (0,)), ((), ()))
        if nst == 1:
            gwo_s[...] += lax.dot_general(merged_b, d_mo, tn, preferred_element_type=F32)
            gwc_s[...] += lax.dot_general(y_ab, dpa, tn, preferred_element_type=F32)
            gwa_s[...] += lax.dot_general(y_bb, dpb, tn, preferred_element_type=F32)
        else:
            st_m, st_d, st_ya, st_pa, st_yb, st_pb = stashes
            st_m[stash, :] = merged_b
            st_d[stash, :] = d_mo
            st_ya[stash, :] = y_ab
            st_pa[stash, :] = dpa
            st_yb[stash, :] = y_bb
            st_pb[stash, :] = dpb

            @pl.when(par == nst - 1)
            def _():
                gwo_s[...] += lax.dot_general(st_m[...], st_d[...], tn, preferred_element_type=F32)
                gwc_s[...] += lax.dot_general(st_ya[...], st_pa[...], tn, preferred_element_type=F32)
                gwa_s[...] += lax.dot_general(st_yb[...], st_pb[...], tn, preferred_element_type=F32)

        dy_a = lax.dot_general(dpa, wc_s[...], nt_dims, preferred_element_type=F32)
        dy_b = lax.dot_general(dpb, wa_s[...], nt_dims, preferred_element_type=F32)

        dattn = dy_b * sz_b
        dpn_ref[:, ZB:ZB + ATTN_OUT] = (dy_b * attn * sg_b * (1.0 + z_b * (1.0 - sg_b))).astype(BF16)
        do_ref[...] = dattn.astype(BF16)
        dl_ref[...] = jnp.dot((dattn * attn).astype(BF16), ebd_ref[...], preferred_element_type=F32)

        dpn_ref[:, BA:BA + D] = (dy_a * conv * sz_a).astype(BF16)
        dpn_ref[:, ZA:ZA + D] = (dy_a * b_a * conv * sg_a * (1.0 + z_a * (1.0 - sg_a))).astype(BF16)
        dconv = dy_a * b_a * sz_a
        vacc_ref[1:2, :] += jnp.sum(dconv * u_m2, axis=0, keepdims=True)
        vacc_ref[2:3, :] += jnp.sum(dconv * u_m1, axis=0, keepdims=True)
        vacc_ref[3:4, :] += jnp.sum(dconv * u, axis=0, keepdims=True)
        nxt = carry_s[...]
        n0, n1 = _row(nxt, 0), _row(nxt, 1)
        dc_p1 = jnp.where(rows == tm - 1, n0, pltpu.roll(dconv, tm - 1, 0))
        dc_p2 = jnp.where(rows == tm - 1, n1, jnp.where(rows == tm - 2, n0, pltpu.roll(dconv, tm - 2, 0)))
        du = w2 * dconv + w1 * dc_p1 + w0 * dc_p2
        carry_s[...] = dconv[0:8, :]
        dpn_ref[:, CA:CA + D] = (du * x_a).astype(BF16)
        dpn_ref[:, XA:XA + D] = (du * c_a).astype(BF16)

        @pl.when(i == nt - 1)
        def _():
            pltpu.sync_copy(gwo_s, gwo_hbm)
            pltpu.sync_copy(gwc_s, gwc_hbm)
            pltpu.sync_copy(gwa_s, gwa_hbm)

    rev = lambda i: (nt - 1 - i, 0)
    halo = lambda col: pl.BlockSpec((16, D), lambda i: (jnp.maximum((nt - 1 - i) * hb - 1, 0), col))
    tile512 = pl.BlockSpec((tm, ATTN_OUT), rev)
    tileD = pl.BlockSpec((tm, D), rev)
    const = lambda shape: pl.BlockSpec(shape, lambda i: (0, 0))
    return pl.pallas_call(
        body, name="mid_fwd_bwd", grid=(nt,),
        in_specs=[pl.BlockSpec((tm, NAT), rev), halo(CA // D), halo(XA // D)] + [tile512] * 6
                 + [tileD, tileD, const((1, 3 * D)), const((8, D)), const((ATTN_OUT, ATTN_OUT))] + [ANY] * 3,
        out_specs=[pl.BlockSpec((tm, NAT), rev), tile512, tile512, tile512, tileD, const((8, D)), ANY, ANY, ANY],
        out_shape=[jax.ShapeDtypeStruct((S, NAT), BF16), jax.ShapeDtypeStruct((S, ATTN_OUT), BF16),
                   jax.ShapeDtypeStruct((S, ATTN_OUT), F32), jax.ShapeDtypeStruct((S, ATTN_OUT), F32),
                   jax.ShapeDtypeStruct((S, D), F32), jax.ShapeDtypeStruct((8, D), F32),
                   jax.ShapeDtypeStruct((D, D), F32), jax.ShapeDtypeStruct((D, D), F32),
                   jax.ShapeDtypeStruct((ATTN_OUT, D), F32)],
        scratch_shapes=[pltpu.VMEM((D, D), BF16), pltpu.VMEM((ATTN_OUT, D), BF16), pltpu.VMEM((D, D), BF16),
                        pltpu.VMEM((D, D), F32), pltpu.VMEM((D, D), F32), pltpu.VMEM((ATTN_OUT, D), F32),
                        pltpu.VMEM((8, D), F32)]
                       + ([pltpu.VMEM((nst * tm, D), BF16)] * 4
                          + [pltpu.VMEM((nst * tm, ATTN_OUT), BF16), pltpu.VMEM((nst * tm, D), BF16)] if nst > 1 else []),
        compiler_params=_params(vmem=60 * 1024 * 1024),
    )(projn, projn, projn, *o_g, *lse_g, x, tgt, mod, convw8, ebd, wc, wa, wo)


def _dh_matmul(dpn, dpq, w_nat, w_qkv, parts, *, tm=512):
    S = dpn.shape[0]
    D = D_MODEL
    n = len(parts)
    nt = S // tm

    def body(dn_ref, dq_ref, wn_hbm, wq_hbm, *rest):
        p_hbm, rest = rest[:n], rest[n:]
        dh_ref = rest[0]
        q_hbm = rest[1:1 + n]
        wn_s, wq_s = rest[1 + n:3 + n]
        sems = rest[3 + n:]

        @pl.when(pl.program_id(0) == 0)
        def _():
            if n:
                mine, _ = _scatter_copies(p_hbm, q_hbm, *sems)
                for cp in mine:
                    cp.start()
            pltpu.sync_copy(wn_hbm, wn_s)
            pltpu.sync_copy(wq_hbm, wq_s)

        if n:
            @pl.when(pl.program_id(0) == nt - 1)
            def _():
                mine, theirs = _scatter_copies(p_hbm, q_hbm, *sems)
                for cp in theirs:
                    cp.wait_recv()
                for cp in mine:
                    cp.wait_send()

        nt_dims = (((1,), (1,)), ((), ()))
        dh = (lax.dot_general(dn_ref[...], wn_s[...], nt_dims, preferred_element_type=F32)
              + lax.dot_general(dq_ref[...], wq_s[...], nt_dims, preferred_element_type=F32))
        dh_ref[...] = dh.astype(BF16)

    row = lambda w: pl.BlockSpec((tm, w), lambda i: (i, 0))
    outs = pl.pallas_call(
        body, name="dh_matmul", grid=(nt,),
        in_specs=[row(NAT), row(NQKV), ANY, ANY] + [ANY] * n,
        out_specs=[row(D)] + [ANY] * n,
        out_shape=[jax.ShapeDtypeStruct((S, D), BF16)] + [jax.ShapeDtypeStruct(p.shape, p.dtype) for p in parts],
        scratch_shapes=[pltpu.VMEM((D, NAT), BF16), pltpu.VMEM((D, NQKV), BF16)]
                       + ([pltpu.SemaphoreType.DMA((n, 3)), pltpu.SemaphoreType.DMA((n, 3))] if n else []),
        compiler_params=_params(vmem=60 * 1024 * 1024),
    )(dpn, dpq, w_nat, w_qkv, *parts)
    return outs[0], outs[1:]


def _norm_bwd(dh, x, dout, mod, norm_w, *, tm=512):
    S, D = x.shape

    def body(dh_ref, x_ref, dout_ref, mod_ref, nw_ref, gx_ref, st_ref):
        @pl.when(pl.program_id(0) == 0)
        def _():
            st_ref[...] = jnp.zeros_like(st_ref)

        dh = dh_ref[...].astype(F32)
        scale1 = 1.0 + mod_ref[:, D:2 * D]
        nw = nw_ref[...]
        xv = x_ref[...]
        rstd = lax.rsqrt(jnp.mean(xv * xv, axis=-1, keepdims=True) + EPS)
        xh = xv * rstd
        dhs = dh * scale1
        dxh = dhs * nw
        dx = rstd * (dxh - xh * jnp.mean(dxh * xh, axis=-1, keepdims=True))
        gx_ref[...] = dout_ref[...] + dx
        st_ref[0:1, :] += jnp.sum(dh, axis=0, keepdims=True)
        st_ref[1:2, :] += jnp.sum(dh * (xh * nw), axis=0, keepdims=True)
        st_ref[2:3, :] += jnp.sum(dhs * xh, axis=0, keepdims=True)

    row = pl.BlockSpec((tm, D), lambda i: (i, 0))
    return pl.pallas_call(
        body, name="norm_bwd", grid=(S // tm,),
        in_specs=[row, row, row, pl.BlockSpec((1, 3 * D), lambda i: (0, 0)), pl.BlockSpec((1, D), lambda i: (0, 0))],
        out_specs=[row, pl.BlockSpec((8, D), lambda i: (0, 0))],
        out_shape=[jax.ShapeDtypeStruct((S, D), F32), jax.ShapeDtypeStruct((8, D), F32)],
        compiler_params=_params(),
    )(dh, x, dout, mod, norm_w)


def _pack_smalls(st, vacc, gqk):
    D = D_MODEL

    def body(st_ref, va_ref, gqk_ref, o_ref):
        o_ref[...] = jnp.zeros_like(o_ref)
        o_ref[0:2, :] = st_ref[0:2, :]
        o_ref[2:3, :] = va_ref[0:1, :]
        o_ref[8:9, :] = st_ref[2:3, :]
        o_ref[16:19, :] = va_ref[1:4, :]
        tot = gqk_ref[0]
        for k in range(1, gqk_ref.shape[0]):
            tot = tot + gqk_ref[k]
        tot = tot + pltpu.roll(tot, HEAD_DIM, 1)
        o_ref[24:32, 0:128] = tot
        loss = jnp.sum(va_ref[4:5, :], axis=1, keepdims=True) * (0.5 / D)
        o_ref[26:27, :] = jnp.broadcast_to(loss, (1, D))

    return pl.pallas_call(
        body, name="pack_smalls", out_shape=jax.ShapeDtypeStruct((32, D), F32),
        in_specs=[pl.BlockSpec(memory_space=pltpu.VMEM)] * 3,
        out_specs=pl.BlockSpec(memory_space=pltpu.VMEM), compiler_params=_params(),
    )(st, vacc, gqk)


def _small_update(sm_loc, dm_pad, ct_pad, w_ada, m_ada, v_ada, vec_w, vec_m, vec_v):
    D = D_MODEL
    cols = w_ada.shape[1]

    def body(sm_ref, dm_ref, ct_ref, w_ref, m_ref, v_ref, vw_ref, vm_ref, vv_ref,
             gw_ref, dw_ref, mw_ref, vw_o_ref, gv_ref, dv_ref, mv_ref, vv_o_ref):
        g = sm_ref[0]
        for k in range(1, 8):
            g = g + sm_ref[k]
        gv_ref[...] = g
        dv, mv, vv = _adamw_math(vw_ref[...], g, vm_ref[...], vv_ref[...])
        dv_ref[...] = dv
        mv_ref[...] = mv
        vv_o_ref[...] = vv
        gw = jnp.dot(ct_ref[...], dm_ref[...], preferred_element_type=F32, precision=lax.Precision.HIGHEST)
        gw_ref[...] = gw
        dw, mw, vw = _adamw_math(w_ref[...], gw, m_ref[...], v_ref[...])
        dw_ref[...] = dw
        mw_ref[...] = mw
        vw_o_ref[...] = vw

    sw = jax.ShapeDtypeStruct((D, cols), F32)
    sv = jax.ShapeDtypeStruct((32, D), F32)
    return pl.pallas_call(
        body, name="small_update", out_shape=[sw, sw, sw, sw, sv, sv, sv, sv],
        in_specs=[pl.BlockSpec(memory_space=pltpu.VMEM)] * 9,
        out_specs=[pl.BlockSpec(memory_space=pltpu.VMEM)] * 8, compiler_params=_params(),
    )(sm_loc, dm_pad, ct_pad, w_ada, m_ada, v_ada, vec_w, vec_m, vec_v)


def _local_step(xs, tg, mod, norm_w, w_nat, w_qkv, wc, wa, wo, convw8, q_norm_w, k_norm_w):
    S = xs.shape[0]
    lane = jnp.arange(128)
    ebd128 = (lane[:, None] // HEAD_DIM == lane[None, :] // HEAD_DIM).astype(BF16)
    lane5 = jnp.arange(ATTN_OUT)
    ebd512 = (lane5[:, None] // HEAD_DIM == lane5[None, :] // HEAD_DIM).astype(BF16)
    qnw2 = jnp.tile(q_norm_w, (1, 2))
    knw2 = jnp.tile(k_norm_w, (1, 2))

    projn, h, ht = _inproj_nat(xs, mod, norm_w, w_nat)
    projq = _matmul(h, w_qkv, tm=1024, tn=1536, name="inproj_qkv")
    o_g, lse_g = [], []
    for g in range(N_GROUPS):
        o, l = _attn_fwd(projq, qnw2, knw2, ebd128, g)
        o_g.append(o)
        lse_g.append(l)

    dpn, d_o, lse, delta, dout, vacc, gwo, gwc, gwa = _mid(
        projn, o_g, lse_g, xs, tg, mod, convw8, wc, wa, wo, ebd512)
    dpq = lax.empty((S, NQKV), BF16)
    gqk = []
    for g in range(N_GROUPS):
        dpq, part = _attn_bwd(projq, dpq, d_o, lse, delta, qnw2, knw2, ebd128, g)
        gqk.append(part)
    gw_nat = _matmul_acc(ht, dpn, tn=1664, tk=2048, name="grad_w_in_nat")
    gw_qkv = _matmul_acc(ht, dpq, tn=1536, tk=2048, name="grad_w_in_qkv")
    return dpn, dpq, dout, gw_nat, gw_qkv, gwc, gwa, gwo, vacc, gqk


def kernel(x, c, w_ada, b_ada, norm_w, w_in, conv_w, q_norm_w, k_norm_w, w_br_conv, w_br_attn, w_out, loss_target, m_w_ada, m_b_ada, m_norm_w, m_w_in, m_conv_w, m_q_norm_w, m_k_norm_w, m_w_br_conv, m_w_br_attn, m_w_out, v_w_ada, v_b_ada, v_norm_w, v_w_in, v_conv_w, v_q_norm_w, v_k_norm_w, v_w_br_conv, v_w_br_attn, v_w_out):
    D = D_MODEL
    S = x.shape[1]
    xs, tg = x[0], loss_target[0]
    mx, my, mc = lax.axis_index("x"), lax.axis_index("y"), lax.axis_index("c")
    chip = 2 * mx + my
    dev = 2 * chip + mc

    c_blk = jnp.concatenate([c, jnp.pad(conv_w[0], ((0, 0), (0, D - 256))), jnp.zeros((4, D), F32)], axis=0)
    b_shard = lax.dynamic_slice(b_ada, (0, chip * 768), (1, 768))
    shards = [w_in[0].astype(BF16).reshape(2, 512, 2816), w_br_conv[0].astype(BF16).reshape(2, 128, D),
              w_br_attn[0].astype(BF16).reshape(2, 256, 256), w_out[0].astype(BF16).reshape(2, 128, D)]
    gathered_in, gathered_w, cg, mod_parts, silu_c = _weights_allgather(
        shards[0].reshape(2, 2, 256, SHARD_COLS), shards[1:], c_blk, w_ada[0], b_shard)
    convw8 = jnp.pad(cg[::2, 1:4, 0:256].transpose(1, 0, 2).reshape(3, D), ((0, 5), (0, 0)))
    mod_all = mod_parts.transpose(1, 0, 2).reshape(8, 3 * D)
    mod = lax.dynamic_slice(mod_all, (dev, 0), (1, 3 * D))
    g_bc, g_ba, g_wo = [lax.dynamic_update_slice(g, s[None], (chip, 0, 0, 0))
                        for g, s in zip(gathered_w, shards[1:])]
    chipv = jnp.reshape(chip, (1,)).astype(jnp.int32)
    g_in = gathered_in.reshape(4, D, SHARD_COLS)
    own_in = shards[0].reshape(D, SHARD_COLS)
    w_nat = _weights_prep(g_in, own_in, chipv, "nat")
    w_qkv = _weights_prep(g_in, own_in, chipv, "qkv")
    wc = g_bc.reshape(D, D)
    wa = g_ba.reshape(4, ATTN_OUT, 256).transpose(1, 0, 2).reshape(ATTN_OUT, D)
    wo = g_wo.reshape(D, D)

    dpn, dpq, dout, gw_nat, gw_qkv, gwc, gwa, gwo, vacc, gqk = _local_step(
        xs, tg, mod, norm_w, w_nat, w_qkv, wc, wa, wo, convw8, q_norm_w, k_norm_w)

    def halves(a):
        k, r, cc = a.shape
        return a.reshape(k, 2, r // 2, cc).transpose(1, 0, 2, 3)

    grads = [gw_nat.reshape(2, D // 2, NAT), gw_qkv.reshape(2, D // 2, NQKV),
             halves(gwc.astype(BF16).reshape(4, 256, D)),
             halves(gwa.astype(BF16).reshape(ATTN_OUT, 4, 256).transpose(1, 0, 2)),
             halves(gwo.astype(BF16).reshape(4, 256, D))]
    recv = _pair_exchange(grads)
    core = jnp.reshape(mc, (1,)).astype(jnp.int32)
    p_in = _pair_add_to_shards(core, grads[0], recv[0], lax.empty((4, D // 2, SHARD_COLS), BF16), "nat")
    p_in = _pair_add_to_shards(core, grads[1], recv[1], p_in, "qkv")
    parts = [p_in] + [_pair_add(core, gr, rc) for gr, rc in zip(grads[2:], recv[2:])]
    dh, gathered = _dh_matmul(dpn, dpq, w_nat, w_qkv, parts)
    grad_x, st = _norm_bwd(dh, xs, dout, mod, norm_w)
    mine = [_sum4(chipv, p, q) for p, q in zip(parts, gathered)]
    theirs, sm_all = _half_exchange(mine, _pack_smalls(st, vacc, jnp.concatenate(gqk, axis=0)))

    big = {}
    for t, (nm, w, m, v) in enumerate((("w_in", w_in, m_w_in, v_w_in), ("w_br_conv", w_br_conv, m_w_br_conv, v_w_br_conv),
                                       ("w_br_attn", w_br_attn, m_w_br_attn, v_w_br_attn), ("w_out", w_out, m_w_out, v_w_out))):
        big[nm] = _adamw_halves(core, w[0], mine[t], theirs[t], m[0], v[0], "adamw_" + nm)

    dmod_all = sm_all[:, 0:3, :].reshape(8, 3 * D)
    dm_pad = jnp.pad(lax.dynamic_slice(dmod_all, (0, chip * 768), (8, 768)), ((0, 120), (0, 0)))
    conv_dev = lax.dynamic_slice(sm_all[:, 16:24, :], (0, 0, chip * 256), (8, 8, 256))
    sm_loc = jnp.concatenate([sm_all[:, 0:16, :], jnp.pad(conv_dev, ((0, 0), (0, 0), (0, D - 256))), sm_all[:, 24:32, :]], axis=1)

    def pack_vec(b3, nw, cw, qw, kw):
        z = jnp.zeros((32, D), F32)
        z = z.at[0:3, :].set(b3.reshape(3, D)).at[8:9, :].set(nw).at[16:19, 0:256].set(cw)
        return z.at[24:25, 0:HEAD_DIM].set(qw).at[25:26, 0:HEAD_DIM].set(kw)

    vec_w = pack_vec(b_ada, norm_w, conv_w[0], q_norm_w, k_norm_w)
    vec_m = pack_vec(m_b_ada, m_norm_w, m_conv_w[0], m_q_norm_w, m_k_norm_w)
    vec_v = pack_vec(v_b_ada, v_norm_w, v_conv_w[0], v_q_norm_w, v_k_norm_w)
    ct_pad = jnp.pad(silu_c.T, ((0, 0), (0, 120)))
    outs = _small_update(sm_loc, dm_pad, ct_pad, w_ada[0], m_w_ada[0], v_w_ada[0], vec_w, vec_m, vec_v)
    ada = outs[0:4]
    vec = outs[4:8]
    loss = vec[0][26, 0]

    def unpack(t):
        return {"b_ada": t[0:3, :].reshape(1, 3 * D), "norm_w": t[8:9, :], "conv_w": t[16:19, 0:256][None],
                "q_norm_w": t[24:25, 0:HEAD_DIM], "k_norm_w": t[25:26, 0:HEAD_DIM]}

    small = [unpack(t) for t in vec]
    order = ["w_ada", "b_ada", "norm_w", "w_in", "conv_w", "q_norm_w", "k_norm_w", "w_br_conv", "w_br_attn", "w_out"]
    res = [loss, grad_x[None]]
    for kind in range(4):
        for nm in order:
            if nm == "w_ada":
                res.append(ada[kind][None])
            elif nm in big:
                res.append(big[nm][kind][None])
            else:
                res.append(small[kind][nm])
    return tuple(res)
```

```python
import functools

import jax
import jax.numpy as jnp
from jax import lax
from jax.experimental import pallas as pl
from jax.experimental.pallas import tpu as pltpu

F32 = jnp.float32
BF16 = jnp.bfloat16
MESH = pl.DeviceIdType.MESH
ANY = pl.BlockSpec(memory_space=pl.ANY)

D_MODEL = 1024
HEAD_DIM = 64
N_GROUPS = 3
DILATIONS = (1, 4, 16)
ATTN_OUT = 512
EPS = 1e-6
NEG_INF = -1e30
NAT = 6656
NQKV = 4608
BA, CA, XA, ZA, ZB, GA, GB = 0, 1024, 2048, 3072, 4096, 4608, 5632
ATT_TILE = 2048
QK_SCALE = HEAD_DIM ** -0.5
LOG2E = 1.4426950408889634
LN2 = 0.6931471805599453
V7X_VMEM_BYTES = 64 * 1024 * 1024
V7X_VMEM_LIMIT = 56 * 1024 * 1024

ADAM_LR = 0.001
ADAM_B1 = 0.9
ADAM_B2 = 0.999
ADAM_EPS = 1e-08
ADAM_WD = 0.01
ADAM_STEP = 10


def _params(vmem=V7X_VMEM_LIMIT, **kw):
    return pltpu.CompilerParams(vmem_limit_bytes=vmem, **kw)


def _sigmoid(z):
    return 1.0 / (1.0 + jnp.exp(-z))


def _row(v, r):
    rows = lax.broadcasted_iota(jnp.int32, v.shape, 0)
    return jnp.sum(jnp.where(rows == r, v, 0.0), axis=0, keepdims=True)


def _coords():
    return lax.axis_index("x"), lax.axis_index("y"), lax.axis_index("c")


def _flip(v, bit):
    return 1 - v if bit else v


def _weights_allgather(big_shard, shards, c_blk, w_ada, b_shard):
    n = len(shards)
    D = D_MODEL
    cols = w_ada.shape[1]

    def body(*refs):
        big_in, ins = refs[0], refs[1:1 + n]
        c_ref, wada_ref, b_ref = refs[1 + n:4 + n]
        big_out, outs = refs[4 + n], refs[5 + n:5 + 2 * n]
        cg_ref, mp_ref, sc_ref = refs[5 + 2 * n:8 + 2 * n]
        ssem, rsem, bs, br, cs, cr, ms, mr, lsem = refs[8 + 2 * n:]
        mx, my, mc = _coords()
        me = 2 * mx + my
        me8 = 2 * me + mc
        sib = (mx, my, 1 - mc)
        chips = [(_flip(mx, j & 2), _flip(my, j & 1)) for j in range(1, 4)]
        sends = []

        local = pltpu.make_async_copy(c_ref, cg_ref.at[me8], lsem)
        local.start()
        for j in range(1, 8):
            peer = (_flip(mx, j & 4), _flip(my, j & 2), _flip(mc, j & 1))
            cp = pltpu.make_async_remote_copy(
                src_ref=c_ref, dst_ref=cg_ref.at[me8], send_sem=cs.at[j - 1], recv_sem=cr.at[j - 1],
                device_id=peer, device_id_type=MESH)
            cp.start()
            sends.append(cp)

        kx, ky, kd = 2 * (1 - mx) + my, 2 * mx + (1 - my), 2 * (1 - mx) + (1 - my)
        xn, yn = (1 - mx, my, mc), (mx, 1 - my, mc)

        def big(src_chip, q, sem, to, half=None):
            h = mc if half is None else half
            return pltpu.make_async_remote_copy(
                src_ref=big_out.at[src_chip, h, q], dst_ref=big_out.at[src_chip, h, q],
                send_sem=bs.at[sem], recv_sem=br.at[sem], device_id=to, device_id_type=MESH)

        def own(q, sem, to):
            return pltpu.make_async_remote_copy(
                src_ref=big_in.at[mc, q], dst_ref=big_out.at[me, mc, q],
                send_sem=bs.at[sem], recv_sem=br.at[sem], device_id=to, device_id_type=MESH)

        for cp in (own(0, 0, xn), own(1, 3, yn), own(1, 1, xn), own(0, 2, yn)):
            cp.start()
            sends.append(cp)
        for t in range(n):
            for j, (px, py) in enumerate(chips):
                cp = pltpu.make_async_remote_copy(
                    src_ref=ins[t].at[mc], dst_ref=outs[t].at[me, mc], send_sem=ssem.at[t, j],
                    recv_sem=rsem.at[t, j], device_id=(px, py, mc), device_id_type=MESH)
                cp.start()
                sends.append(cp)

        pieces = [(kx, 0), (kx, 1), (ky, 0), (ky, 1), (kd, 0), (kd, 1)]

        def landed(piece, sem, frm, pass_on=None):
            src_chip, q = pieces[piece]
            big(src_chip, q, sem, frm).wait_recv()
            nxt = ([] if pass_on is None else [big(src_chip, q, pass_on[0], pass_on[1])])
            nxt.append(big(src_chip, q, 6 + piece, sib))
            for cp in nxt:
                cp.start()
                sends.append(cp)

        for j in range(1, 8):
            px, py, pc = _flip(mx, j & 4), _flip(my, j & 2), _flip(mc, j & 1)
            pltpu.make_async_remote_copy(
                src_ref=c_ref, dst_ref=cg_ref.at[4 * px + 2 * py + pc], send_sem=cs.at[j - 1],
                recv_sem=cr.at[j - 1], device_id=(px, py, pc), device_id_type=MESH).wait_recv()
        local.wait()
        pick = (lax.broadcasted_iota(jnp.int32, (8, 64), 1) == 8 * lax.broadcasted_iota(jnp.int32, (8, 64), 0))
        cv = jnp.dot(pick.astype(F32), cg_ref[...].reshape(64, D), preferred_element_type=F32,
                     precision=lax.Precision.HIGHEST)
        sc = cv * _sigmoid(cv)
        sc_ref[...] = sc
        mp_ref[me] = jnp.dot(sc, wada_ref[...], preferred_element_type=F32,
                             precision=lax.Precision.HIGHEST) + b_ref[...]
        for j, (px, py) in enumerate(chips):
            cp = pltpu.make_async_remote_copy(
                src_ref=mp_ref.at[me], dst_ref=mp_ref.at[me], send_sem=ms.at[j], recv_sem=mr.at[j],
                device_id=(px, py, mc), device_id_type=MESH)
            cp.start()
            sends.append(cp)

        landed(0, 0, xn, pass_on=(4, yn))
        landed(3, 3, yn, pass_on=(5, xn))
        landed(1, 1, xn)
        landed(2, 2, yn)
        for j, (px, py) in enumerate(chips):
            pltpu.make_async_remote_copy(
                src_ref=mp_ref.at[me], dst_ref=mp_ref.at[2 * px + py], send_sem=ms.at[j], recv_sem=mr.at[j],
                device_id=(px, py, mc), device_id_type=MESH).wait_recv()
        landed(4, 4, yn)
        landed(5, 5, xn)
        for t in range(n):
            for j, (px, py) in enumerate(chips):
                src = 2 * px + py
                pltpu.make_async_remote_copy(
                    src_ref=ins[t].at[mc], dst_ref=outs[t].at[src, mc], send_sem=ssem.at[t, j],
                    recv_sem=rsem.at[t, j], device_id=(px, py, mc), device_id_type=MESH).wait_recv()
                fwd = pltpu.make_async_remote_copy(
                    src_ref=outs[t].at[src, mc], dst_ref=outs[t].at[src, mc], send_sem=ssem.at[t, 3 + j],
                    recv_sem=rsem.at[t, 3 + j], device_id=sib, device_id_type=MESH)
                fwd.start()
                sends.append(fwd)
        for t in range(n):
            for j, (px, py) in enumerate(chips):
                src = 2 * px + py
                pltpu.make_async_remote_copy(
                    src_ref=outs[t].at[src, 1 - mc], dst_ref=outs[t].at[src, 1 - mc], send_sem=ssem.at[t, 3 + j],
                    recv_sem=rsem.at[t, 3 + j], device_id=sib, device_id_type=MESH).wait_recv()
        for i, (src_chip, q) in enumerate(pieces):
            big(src_chip, q, 6 + i, sib, half=1 - mc).wait_recv()
        for cp in sends:
            cp.wait_send()

    vm = pl.BlockSpec(memory_space=pltpu.VMEM)
    outs = pl.pallas_call(
        body, name="weights_allgather",
        out_shape=[jax.ShapeDtypeStruct((4,) + big_shard.shape, big_shard.dtype)]
                  + [jax.ShapeDtypeStruct((4,) + s.shape, s.dtype) for s in shards]
                  + [jax.ShapeDtypeStruct((8,) + c_blk.shape, F32), jax.ShapeDtypeStruct((4, 8, cols), F32),
                     jax.ShapeDtypeStruct((8, D), F32)],
        in_specs=[ANY] * (n + 1) + [vm] * 3, out_specs=[ANY] * (n + 1) + [vm] * 3,
        scratch_shapes=[pltpu.SemaphoreType.DMA((n, 6)), pltpu.SemaphoreType.DMA((n, 6)),
                        pltpu.SemaphoreType.DMA((12,)), pltpu.SemaphoreType.DMA((12,)),
                        pltpu.SemaphoreType.DMA((7,)), pltpu.SemaphoreType.DMA((7,)),
                        pltpu.SemaphoreType.DMA((3,)), pltpu.SemaphoreType.DMA((3,)), pltpu.SemaphoreType.DMA(())],
        compiler_params=_params(),
    )(big_shard, *shards, c_blk, w_ada, b_shard)
    return outs[0], outs[1:1 + n], outs[1 + n], outs[2 + n], outs[3 + n]


def _pair_exchange(grads):
    n = len(grads)

    def body(*refs):
        ins, outs = refs[:n], refs[n:2 * n]
        ssem, rsem = refs[2 * n:]
        mx, my, mc = _coords()
        sib = (mx, my, 1 - mc)
        sends = []
        for t in range(n):
            cp = pltpu.make_async_remote_copy(
                src_ref=ins[t].at[1 - mc], dst_ref=outs[t], send_sem=ssem.at[t], recv_sem=rsem.at[t],
                device_id=sib, device_id_type=MESH)
            cp.start()
            sends.append(cp)
        for cp in sends:
            cp.wait_recv()
        for cp in sends:
            cp.wait_send()

    return pl.pallas_call(
        body, name="grad_pair_exchange",
        out_shape=[jax.ShapeDtypeStruct(g.shape[1:], g.dtype) for g in grads],
        in_specs=[ANY] * n, out_specs=[ANY] * n,
        scratch_shapes=[pltpu.SemaphoreType.DMA((n,)), pltpu.SemaphoreType.DMA((n,))],
    )(*grads)


def _scatter_copies(ins, outs, ssem, rsem):
    mx, my, mc = _coords()
    me = 2 * mx + my
    mine, theirs = [], []
    for t in range(len(ins)):
        for j in range(1, 4):
            px, py = _flip(mx, j & 2), _flip(my, j & 1)
            mine.append(pltpu.make_async_remote_copy(
                src_ref=ins[t].at[2 * px + py], dst_ref=outs[t].at[me], send_sem=ssem.at[t, j - 1],
                recv_sem=rsem.at[t, j - 1], device_id=(px, py, mc), device_id_type=MESH))
            theirs.append(pltpu.make_async_remote_copy(
                src_ref=ins[t].at[me], dst_ref=outs[t].at[2 * px + py], send_sem=ssem.at[t, j - 1],
                recv_sem=rsem.at[t, j - 1], device_id=(px, py, mc), device_id_type=MESH))
    return mine, theirs


def _half_exchange(halves, smalls):
    n = len(halves)

    def body(*refs):
        ins, sm_ref = refs[:n], refs[n]
        outs, all_ref = refs[n + 1:2 * n + 1], refs[2 * n + 1]
        ssem, rsem, gsem, hsem, lsem = refs[2 * n + 2:]
        mx, my, mc = _coords()
        me = 4 * mx + 2 * my + mc
        sib = (mx, my, 1 - mc)
        local = pltpu.make_async_copy(sm_ref, all_ref.at[me], lsem)
        local.start()
        sends = []
        for t in range(n):
            rc = pltpu.make_async_remote_copy(
                src_ref=ins[t], dst_ref=outs[t], send_sem=ssem.at[t], recv_sem=rsem.at[t],
                device_id=sib, device_id_type=MESH)
            rc.start()
            sends.append(rc)
        gathers = []
        for j in range(1, 8):
            peer = (_flip(mx, j & 4), _flip(my, j & 2), _flip(mc, j & 1))
            cp = pltpu.make_async_remote_copy(
                src_ref=sm_ref, dst_ref=all_ref.at[me], send_sem=gsem.at[j - 1], recv_sem=hsem.at[j - 1],
                device_id=peer, device_id_type=MESH)
            cp.start()
            gathers.append(cp)
        for cp in sends:
            cp.wait_recv()
        for j in range(1, 8):
            px, py, pc = _flip(mx, j & 4), _flip(my, j & 2), _flip(mc, j & 1)
            pltpu.make_async_remote_copy(
                src_ref=sm_ref, dst_ref=all_ref.at[4 * px + 2 * py + pc], send_sem=gsem.at[j - 1],
                recv_sem=hsem.at[j - 1], device_id=(px, py, pc), device_id_type=MESH).wait_recv()
        for cp in sends + gathers:
            cp.wait_send()
        local.wait()

    vm = pl.BlockSpec(memory_space=pltpu.VMEM)
    outs = pl.pallas_call(
        body, name="grad_half_exchange",
        out_shape=[jax.ShapeDtypeStruct(h.shape, h.dtype) for h in halves]
                  + [jax.ShapeDtypeStruct((8,) + smalls.shape, smalls.dtype)],
        in_specs=[ANY] * n + [vm], out_specs=[ANY] * n + [vm],
        scratch_shapes=[pltpu.SemaphoreType.DMA((n,)), pltpu.SemaphoreType.DMA((n,)),
                        pltpu.SemaphoreType.DMA((7,)), pltpu.SemaphoreType.DMA((7,)), pltpu.SemaphoreType.DMA(())],
    )(*halves, smalls)
    return outs[:n], outs[n]


def _row_tile(rows, cols, bytes_per_row_elem=4, budget=2 * 1024 * 1024):
    t = rows
    while t % 2 == 0 and t > 16 and t * cols * bytes_per_row_elem > budget:
        t //= 2
    return t


def _pair_add(core, g, r):
    _, _, rh, cc = g.shape
    tr = _row_tile(rh, cc)

    def body(core_ref, g_ref, r_ref, o_ref):
        o_ref[...] = (g_ref[...].astype(F32) + r_ref[...].astype(F32)).astype(o_ref.dtype)

    return pl.pallas_call(
        body, name="grad_pair_add",
        grid_spec=pltpu.PrefetchScalarGridSpec(
            num_scalar_prefetch=1, grid=(4, rh // tr),
            in_specs=[pl.BlockSpec((None, None, tr, cc), lambda k, i, c: (c[0], k, i, 0)),
                      pl.BlockSpec((None, tr, cc), lambda k, i, c: (k, i, 0))],
            out_specs=pl.BlockSpec((None, tr, cc), lambda k, i, c: (k, i, 0))),
        out_shape=jax.ShapeDtypeStruct(r.shape, BF16),
        compiler_params=_params(),
    )(core, g, r)


def _sum4(chip, p, q):
    _, rh, cc = q.shape
    tr = _row_tile(rh, cc)

    def body(chip_ref, p_ref, q1_ref, q2_ref, q3_ref, o_ref):
        o_ref[...] = ((p_ref[...].astype(F32) + q1_ref[...].astype(F32)) + q2_ref[...].astype(F32)) + q3_ref[...].astype(F32)

    def other(j):
        return pl.BlockSpec((None, tr, cc), lambda i, c: (jnp.bitwise_xor(c[0], j), i, 0))

    return pl.pallas_call(
        body, name="grad_sum4",
        grid_spec=pltpu.PrefetchScalarGridSpec(
            num_scalar_prefetch=1, grid=(rh // tr,),
            in_specs=[pl.BlockSpec((None, tr, cc), lambda i, c: (c[0], i, 0)), other(1), other(2), other(3)],
            out_specs=pl.BlockSpec((tr, cc), lambda i, c: (i, 0))),
        out_shape=jax.ShapeDtypeStruct((rh, cc), F32),
        compiler_params=_params(),
    )(chip, p, q, q, q)


def _adamw_math(w, g, m, v):
    m = ADAM_B1 * m + (1.0 - ADAM_B1) * g
    v = ADAM_B2 * v + (1.0 - ADAM_B2) * (g * g)
    m_hat = m / (1.0 - ADAM_B1 ** ADAM_STEP)
    v_hat = v / (1.0 - ADAM_B2 ** ADAM_STEP)
    delta = -ADAM_LR * (m_hat / (jnp.sqrt(v_hat) + ADAM_EPS) + ADAM_WD * w)
    return delta, m, v


def _adamw_halves(core, w, mine, recv, m, v, name):
    rows, cols = w.shape
    rh = rows // 2
    tr = _row_tile(rh, cols, budget=1024 * 1024)
    nh = rh // tr

    def body(core_ref, w_ref, a_ref, b_ref, m_ref, v_ref, g_ref, d_ref, mo_ref, vo_ref):
        g = jnp.where(pl.program_id(0) == core_ref[0], a_ref[...], b_ref[...])
        d, mn, vn = _adamw_math(w_ref[...], g, m_ref[...], v_ref[...])
        g_ref[...] = g
        d_ref[...] = d
        mo_ref[...] = mn
        vo_ref[...] = vn

    full = pl.BlockSpec((tr, cols), lambda h, i, c: (h * nh + i, 0))
    half = pl.BlockSpec((tr, cols), lambda h, i, c: (i, 0))
    sd = jax.ShapeDtypeStruct((rows, cols), F32)
    return pl.pallas_call(
        body, name=name,
        grid_spec=pltpu.PrefetchScalarGridSpec(
            num_scalar_prefetch=1, grid=(2, nh),
            in_specs=[full, half, half, full, full], out_specs=[full] * 4),
        out_shape=[sd, sd, sd, sd], compiler_params=_params(),
    )(core, w, mine, recv, m, v)


def _inproj_nat(x, mod, norm_w, w, *, tm=1024, tn=1664):
    S, D = x.shape
    N = w.shape[1]
    nt, nj = S // tm, N // tn
    chunk = 256

    def body(x_ref, mod_ref, nw_ref, w_ref, proj_ref, h_ref, ht_ref, h_a, ht_a, h_b, ht_b):
        i, j = pl.program_id(0), pl.program_id(1)

        def make_h(h_s, ht_s):
            shift = mod_ref[:, 0:D]
            scale1 = 1.0 + mod_ref[:, D:2 * D]
            for r in range(tm // chunk):
                xv = x_ref[pl.ds(r * chunk, chunk), :]
                ms = jnp.mean(xv * xv, axis=-1, keepdims=True)
                h = (xv * lax.rsqrt(ms + EPS) * nw_ref[...]) * scale1 + shift
                h_s[pl.ds(r * chunk, chunk), :] = h.astype(BF16)
                ht_s[:, pl.ds(r * chunk, chunk)] = h.T.astype(BF16)

        @pl.when((i == 0) & (j == 0))
        def _():
            make_h(h_a, ht_a)

        def step(cur, nxt):
            @pl.when(j == 0)
            def _():
                h_ref[...] = cur[0][...]
                ht_ref[...] = cur[1][...]

            def project():
                proj_ref[...] = jnp.dot(cur[0][...], w_ref[...], preferred_element_type=F32).astype(BF16)

            ahead = (j == nj - 1) & (i < nt - 1)

            @pl.when(ahead)
            def _():
                project()
                make_h(*nxt)

            @pl.when(jnp.logical_not(ahead))
            def _():
                project()

        @pl.when(i % 2 == 0)
        def _():
            step((h_a, ht_a), (h_b, ht_b))

        @pl.when(i % 2 == 1)
        def _():
            step((h_b, ht_b), (h_a, ht_a))

    def x_tile(i, j):
        return (jnp.where((i == 0) & (j == 0), 0, jnp.minimum(i + 1, nt - 1)), 0)

    return pl.pallas_call(
        body, name="inproj_nat", grid=(nt, nj),
        in_specs=[pl.BlockSpec((tm, D), x_tile),
                  pl.BlockSpec((1, 3 * D), lambda i, j: (0, 0)),
                  pl.BlockSpec((1, D), lambda i, j: (0, 0)),
                  pl.BlockSpec((D, tn), lambda i, j: (0, j))],
        out_specs=[pl.BlockSpec((tm, tn), lambda i, j: (i, j)),
                   pl.BlockSpec((tm, D), lambda i, j: (i, 0)),
                   pl.BlockSpec((D, tm), lambda i, j: (0, i))],
        out_shape=[jax.ShapeDtypeStruct((S, N), BF16), jax.ShapeDtypeStruct((S, D), BF16),
                   jax.ShapeDtypeStruct((D, S), BF16)],
        scratch_shapes=[pltpu.VMEM((tm, D), BF16), pltpu.VMEM((D, tm), BF16),
                        pltpu.VMEM((tm, D), BF16), pltpu.VMEM((D, tm), BF16)],
        compiler_params=_params(),
    )(x, mod, norm_w, w)


def _matmul(a, b, *, tm, tn, name):
    M, K = a.shape
    N = b.shape[1]

    def body(a_ref, b_ref, o_ref):
        o_ref[...] = jnp.dot(a_ref[...], b_ref[...], preferred_element_type=F32).astype(o_ref.dtype)

    return pl.pallas_call(
        body, name=name, grid=(M // tm, N // tn),
        in_specs=[pl.BlockSpec((tm, K), lambda i, j: (i, 0)), pl.BlockSpec((K, tn), lambda i, j: (0, j))],
        out_specs=pl.BlockSpec((tm, tn), lambda i, j: (i, j)),
        out_shape=jax.ShapeDtypeStruct((M, N), BF16), compiler_params=_params(),
    )(a, b)


def _matmul_acc(a, b, *, tn, tk, name):
    M, K = a.shape
    N = b.shape[1]
    nk = K // tk

    def body(a_ref, b_ref, o_ref, acc_ref):
        k = pl.program_id(1)

        @pl.when(k == 0)
        def _():
            acc_ref[...] = jnp.zeros_like(acc_ref)

        acc_ref[...] += jnp.dot(a_ref[...], b_ref[...], preferred_element_type=F32)

        @pl.when(k == nk - 1)
        def _():
            o_ref[...] = acc_ref[...].astype(o_ref.dtype)

    return pl.pallas_call(
        body, name=name, grid=(N // tn, nk),
        in_specs=[pl.BlockSpec((M, tk), lambda j, k: (0, k)), pl.BlockSpec((tk, tn), lambda j, k: (k, j))],
        out_specs=pl.BlockSpec((M, tn), lambda j, k: (0, j)),
        out_shape=jax.ShapeDtypeStruct((M, N), BF16),
        scratch_shapes=[pltpu.VMEM((M, tn), F32)], compiler_params=_params(),
    )(a, b)


SHARD_COLS = 2816


def _column_tables(part):
    if part == "nat":
        bw = 256
        orig = [j * bw if j < 16 else 8704 + (j - 16) * bw for j in range(NAT // bw)]
    else:
        bw = 128
        orig = []
        for jj in range(NQKV // bw):
            g, p, t = jj // 12, (jj % 12) // 3, jj % 3
            orig.append(4096 + t * 1536 + g * 512 + p * 128)
    tk = jnp.array([o // SHARD_COLS for o in orig], jnp.int32)
    tc = jnp.array([(o % SHARD_COLS) // bw for o in orig], jnp.int32)
    return tk, tc, bw


def _weights_prep(gath, own, chipv, part):
    D = D_MODEL
    tk, tc, bw = _column_tables(part)
    n = tk.shape[0]
    per_step = 2
    assert n % per_step == 0

    def body(tk_ref, tc_ref, chip_ref, *refs):
        w_ref = refs[-1]
        for u in range(per_step):
            g_ref, o_ref = refs[2 * u], refs[2 * u + 1]
            mine = tk_ref[pl.program_id(0) * per_step + u] == chip_ref[0]
            w_ref[:, u * bw:(u + 1) * bw] = jnp.where(mine, o_ref[...], g_ref[...])

    def gath_spec(u):
        def idx(j, tk, tc, ch):
            k = tk[j * per_step + u]
            return (jnp.where(k == ch[0], (k + 1) % 4, k), 0, tc[j * per_step + u])
        return pl.BlockSpec((None, D, bw), idx)

    def own_spec(u):
        return pl.BlockSpec((D, bw), lambda j, tk, tc, ch: (0, tc[j * per_step + u]))

    specs = []
    for u in range(per_step):
        specs += [gath_spec(u), own_spec(u)]
    return pl.pallas_call(
        body, name="weights_prep_" + part,
        grid_spec=pltpu.PrefetchScalarGridSpec(
            num_scalar_prefetch=3, grid=(n // per_step,),
            in_specs=specs,
            out_specs=pl.BlockSpec((D, per_step * bw), lambda j, tk, tc, ch: (0, j))),
        out_shape=jax.ShapeDtypeStruct((D, n * bw), BF16),
        compiler_params=_params(),
    )(tk, tc, chipv, *([gath, own] * per_step))


def _pair_add_to_shards(core, gw, r, into, part):
    D = D_MODEL
    tk, tc, bw = _column_tables(part)
    n = tk.shape[0]

    def body(tk_ref, tc_ref, core_ref, g_ref, r_ref, into_ref, o_ref):
        o_ref[...] = (g_ref[...].astype(F32) + r_ref[...].astype(F32)).astype(BF16)

    return pl.pallas_call(
        body, name="grad_pair_add_" + part,
        grid_spec=pltpu.PrefetchScalarGridSpec(
            num_scalar_prefetch=3, grid=(n,),
            in_specs=[pl.BlockSpec((None, D // 2, bw), lambda j, tk, tc, c: (c[0], 0, j)),
                      pl.BlockSpec((D // 2, bw), lambda j, tk, tc, c: (0, j)), ANY],
            out_specs=pl.BlockSpec((None, D // 2, bw), lambda j, tk, tc, c: (tk[j], 0, tc[j]))),
        out_shape=jax.ShapeDtypeStruct(into.shape, BF16),
        input_output_aliases={5: 0},
        compiler_params=_params(),
    )(tk, tc, core, gw, r, into)


def _head_norm(xb, w, ebd):
    x = xb.astype(F32)
    ss = jnp.dot((x * x).astype(BF16), ebd, preferred_element_type=F32)
    return x * lax.rsqrt(ss * (1.0 / HEAD_DIM) + EPS) * w


def _window_mask(no_prev):
    qi = lax.broadcasted_iota(jnp.int32, (128, 256), 0)
    kc = lax.broadcasted_iota(jnp.int32, (128, 256), 1)
    ok = (kc >= qi) & (kc <= qi + 128)
    if no_prev is not None:
        ok = ok & ((kc >= 128) | jnp.logical_not(no_prev))
    return ok


def _lane_masks():
    lane = lax.broadcasted_iota(jnp.int32, (1, 128), 1)
    return [(lane < HEAD_DIM).astype(F32), (lane >= HEAD_DIM).astype(F32)]


def _head_col(v, mh):
    return jnp.sum(v * mh, axis=1, keepdims=True) * (1.0 / HEAD_DIM)


def _attn_fwd(projq, qnw2, knw2, ebd, g):
    S = projq.shape[0]
    d = DILATIONS[g]
    T = ATT_TILE
    nt = S // T
    nb = T // (128 * d)
    sdt = BF16 if d == 1 else F32

    def body(cur_ref, qw_ref, kw_ref, ebd_ref, o_ref, lse_ref, qs_s, ks_s, vs_s):
        t = pl.program_id(1)
        e = ebd_ref[...]

        @pl.when(t == 0)
        def _():
            ks_s[pl.ds(0, T), :] = jnp.zeros((T, 128), sdt)
            vs_s[pl.ds(0, T), :] = jnp.zeros((T, 128), sdt)

        @pl.when(t > 0)
        def _():
            ks_s[pl.ds(0, T), :] = ks_s[pl.ds(T, T), :]
            vs_s[pl.ds(0, T), :] = vs_s[pl.ds(T, T), :]

        qs_s[...] = (_head_norm(cur_ref[:, 0:128], qw_ref[...], e) * (QK_SCALE * LOG2E)).astype(sdt)
        ks_s[pl.ds(T, T), :] = _head_norm(cur_ref[:, 128:256], kw_ref[...], e).astype(sdt)
        vs_s[pl.ds(T, T), :] = cur_ref[:, 256:384].astype(sdt)
        m0, m1 = _lane_masks()
        first = t == 0
        for r in range(d):
            for bb in range(nb):
                q0 = r + bb * 128 * d
                k0 = T + r + (bb - 1) * 128 * d
                qs = qs_s[pl.ds(q0, 128, stride=d), :].astype(F32)
                kb = ks_s[pl.ds(k0, 256, stride=d), :].astype(BF16)
                vb = vs_s[pl.ds(k0, 256, stride=d), :].astype(BF16)
                ok = _window_mask(first if bb == 0 else None)
                ok2 = jnp.concatenate([ok, ok], axis=0)
                q2 = jnp.concatenate([qs * m0, qs * m1], axis=0).astype(BF16)
                s = lax.dot_general(q2, kb, (((1,), (1,)), ((), ())), preferred_element_type=F32)
                s = jnp.where(ok2, s, NEG_INF)
                mx = jnp.max(s, axis=-1, keepdims=True)
                p = jnp.exp2(s - mx)
                l = jnp.sum(p, axis=-1, keepdims=True)
                o = jnp.dot(p.astype(BF16), vb, preferred_element_type=F32) / l
                lse = mx * LN2 + jnp.log(l)
                o_ref[pl.ds(q0, 128, stride=d), :] = o[0:128] * m0 + o[128:256] * m1
                lse_ref[pl.ds(q0, 128, stride=d), :] = lse[0:128] * m0 + lse[128:256] * m1

    return pl.pallas_call(
        body, name=f"attn_fwd_g{g}", grid=(4, nt),
        in_specs=[pl.BlockSpec((T, 384), lambda p, t: (t, g * 4 + p)),
                  pl.BlockSpec((1, 128), lambda p, t: (0, 0)),
                  pl.BlockSpec((1, 128), lambda p, t: (0, 0)),
                  pl.BlockSpec((128, 128), lambda p, t: (0, 0))],
        out_specs=[pl.BlockSpec((T, 128), lambda p, t: (t, p)), pl.BlockSpec((T, 128), lambda p, t: (t, p))],
        out_shape=[jax.ShapeDtypeStruct((S, ATTN_OUT), F32), jax.ShapeDtypeStruct((S, ATTN_OUT), F32)],
        scratch_shapes=[pltpu.VMEM((T, 128), sdt), pltpu.VMEM((2 * T, 128), sdt), pltpu.VMEM((2 * T, 128), sdt)],
        compiler_params=_params(),
    )(projq, qnw2, knw2, ebd)


def _attn_bwd(projq, dprojq, d_o, lse, delta, qnw2, knw2, ebd, g):
    S = projq.shape[0]
    d = DILATIONS[g]
    T = ATT_TILE
    nt = S // T
    nb = T // (128 * d)
    sdt = F32

    def norm_bwd(raw_b, dy, w, e):
        x = raw_b.astype(F32)
        ss = jnp.dot((x * x).astype(BF16), e, preferred_element_type=F32)
        rstd = lax.rsqrt(ss * (1.0 / HEAD_DIM) + EPS)
        xh = x * rstd
        gw = jnp.sum(dy * xh, axis=0, keepdims=True)
        dxh = dy * w
        mean = jnp.dot((dxh * xh).astype(BF16), e, preferred_element_type=F32) * (1.0 / HEAD_DIM)
        return rstd * (dxh - xh * mean), gw

    nt_dims = (((1,), (1,)), ((), ()))
    tn_dims = (((0,), (0,)), ((), ()))

    def unit_stacked(qs, dos, ls, dls, kb, vb, ok, m0, m1):
        ok2 = jnp.concatenate([ok, ok], axis=0)
        q2 = jnp.concatenate([qs * m0, qs * m1], axis=0).astype(BF16)
        do2 = jnp.concatenate([dos * m0, dos * m1], axis=0).astype(BF16)
        lcol = jnp.concatenate([_head_col(ls, m0), _head_col(ls, m1)], axis=0)
        dcol = jnp.concatenate([_head_col(dls, m0), _head_col(dls, m1)], axis=0)
        s = jnp.where(ok2, lax.dot_general(q2, kb, nt_dims, preferred_element_type=F32), NEG_INF)
        p = jnp.exp2(s - lcol * LOG2E)
        dp = lax.dot_general(do2, vb, nt_dims, preferred_element_type=F32)
        ds = (p * (dp - dcol)).astype(BF16)
        dq2 = jnp.dot(ds, kb, preferred_element_type=F32)
        dk = lax.dot_general(ds, q2, tn_dims, preferred_element_type=F32)
        dv = lax.dot_general(p.astype(BF16), do2, tn_dims, preferred_element_type=F32)
        return dq2[0:128] * m0 + dq2[128:256] * m1, dk, dv

    def unit_per_head(qs, dos, ls, dls, kb, vb, ok, m0, m1):
        dq_t = dk_t = dv_t = None
        for mh in (m0, m1):
            qh = (qs * mh).astype(BF16)
            doh = (dos * mh).astype(BF16)
            s = jnp.where(ok, lax.dot_general(qh, kb, nt_dims, preferred_element_type=F32), NEG_INF)
            p = jnp.exp2(s - _head_col(ls, mh) * LOG2E)
            dp = lax.dot_general(doh, vb, nt_dims, preferred_element_type=F32)
            ds = (p * (dp - _head_col(dls, mh))).astype(BF16)
            dq = jnp.dot(ds, kb, preferred_element_type=F32) * mh
            dk = lax.dot_general(ds, qh, tn_dims, preferred_element_type=F32)
            dv = lax.dot_general(p.astype(BF16), doh, tn_dims, preferred_element_type=F32)
            dq_t = dq if dq_t is None else dq_t + dq
            dk_t = dk if dk_t is None else dk_t + dk
            dv_t = dv if dv_t is None else dv_t + dv
        return dq_t, dk_t, dv_t

    unit = unit_per_head if d == 1 else unit_stacked

    def body(cur_ref, prev_ref, do_ref, lse_ref, dl_ref, qw_ref, kw_ref, ebd_ref, dp_in_ref,
             out_ref, gqk_ref, qs_s, ks_s, vs_s, do_s, dq_cur, dq_prev, dk_acc, dv_acc):
        i = pl.program_id(1)
        e = ebd_ref[...]
        m0, m1 = _lane_masks()

        @pl.when(i == 0)
        def _():
            dk_acc[...] = jnp.zeros_like(dk_acc)
            dv_acc[...] = jnp.zeros_like(dv_acc)
            dq_prev[...] = jnp.zeros_like(dq_prev)
            gqk_ref[...] = jnp.zeros_like(gqk_ref)

        @pl.when(i < nt)
        def _():
            qs_s[...] = _head_norm(cur_ref[:, 0:128], qw_ref[...], e) * (QK_SCALE * LOG2E)
            ks_s[pl.ds(0, T), :] = _head_norm(prev_ref[:, 128:256], kw_ref[...], e)
            ks_s[pl.ds(T, T), :] = _head_norm(cur_ref[:, 128:256], kw_ref[...], e)
            vs_s[pl.ds(0, T), :] = prev_ref[:, 256:384].astype(F32)
            vs_s[pl.ds(T, T), :] = cur_ref[:, 256:384].astype(F32)
            do_s[...] = do_ref[...].astype(F32)
            first = i == 0
            for r in range(d):
                own_k = own_v = None
                for bb in range(nb):
                    q0 = r + bb * 128 * d
                    k0 = T + r + (bb - 1) * 128 * d
                    qs = qs_s[pl.ds(q0, 128, stride=d), :]
                    dos = do_s[pl.ds(q0, 128, stride=d), :]
                    ls = lse_ref[pl.ds(q0, 128, stride=d), :]
                    dls = dl_ref[pl.ds(q0, 128, stride=d), :]
                    kb = ks_s[pl.ds(k0, 256, stride=d), :].astype(BF16)
                    vb = vs_s[pl.ds(k0, 256, stride=d), :].astype(BF16)
                    ok = _window_mask(first if bb == 0 else None)
                    dq_t, dk_t, dv_t = unit(qs, dos, ls, dls, kb, vb, ok, m0, m1)
                    dq_cur[pl.ds(q0, 128, stride=d), :] = dq_t
                    before = pl.ds(k0, 128, stride=d)
                    if bb == 0:
                        dk_acc[before, :] = dk_acc[before, :] + dk_t[0:128]
                        dv_acc[before, :] = dv_acc[before, :] + dv_t[0:128]
                    else:
                        dk_acc[before, :] = own_k + dk_t[0:128]
                        dv_acc[before, :] = own_v + dv_t[0:128]
                    own_k, own_v = dk_t[128:256], dv_t[128:256]
                tail = pl.ds(T + r + (nb - 1) * 128 * d, 128, stride=d)
                dk_acc[tail, :] = own_k
                dv_acc[tail, :] = own_v

        @pl.when(i >= 1)
        def _():
            dq_raw, gq = norm_bwd(prev_ref[:, 0:128], dq_prev[...] * QK_SCALE, qw_ref[...], e)
            dk_raw, gk = norm_bwd(prev_ref[:, 128:256], dk_acc[pl.ds(0, T), :] * LN2, kw_ref[...], e)
            out_ref[:, 0:128] = dq_raw.astype(BF16)
            out_ref[:, 128:256] = dk_raw.astype(BF16)
            out_ref[:, 256:384] = dv_acc[pl.ds(0, T), :].astype(BF16)
            gqk_ref[0:1, :] += gq
            gqk_ref[1:2, :] += gk

        dq_prev[...] = dq_cur[...]
        dk_acc[pl.ds(0, T), :] = dk_acc[pl.ds(T, T), :]
        dv_acc[pl.ds(0, T), :] = dv_acc[pl.ds(T, T), :]

    last = nt - 1
    qtile = lambda p, i: (jnp.minimum(i, last), p)
    return pl.pallas_call(
        body, name=f"attn_bwd_g{g}", grid=(4, nt + 1),
        in_specs=[pl.BlockSpec((T, 384), lambda p, i: (jnp.minimum(i, last), g * 4 + p)),
                  pl.BlockSpec((T, 384), lambda p, i: (jnp.maximum(i - 1, 0), g * 4 + p)),
                  pl.BlockSpec((T, 128), qtile), pl.BlockSpec((T, 128), qtile), pl.BlockSpec((T, 128), qtile),
                  pl.BlockSpec((1, 128), lambda p, i: (0, 0)),
                  pl.BlockSpec((1, 128), lambda p, i: (0, 0)),
                  pl.BlockSpec((128, 128), lambda p, i: (0, 0)),
                  ANY],
        out_specs=[pl.BlockSpec((T, 384), lambda p, i: (jnp.maximum(i - 1, 0), g * 4 + p)),
                   pl.BlockSpec((None, 8, 128), lambda p, i: (p, 0, 0))],
        out_shape=[jax.ShapeDtypeStruct(dprojq.shape, BF16), jax.ShapeDtypeStruct((4, 8, 128), F32)],
        scratch_shapes=[pltpu.VMEM((T, 128), sdt), pltpu.VMEM((2 * T, 128), sdt), pltpu.VMEM((2 * T, 128), sdt),
                        pltpu.VMEM((T, 128), sdt), pltpu.VMEM((T, 128), F32), pltpu.VMEM((T, 128), F32),
                        pltpu.VMEM((2 * T, 128), F32), pltpu.VMEM((2 * T, 128), F32)],
        input_output_aliases={8: 0},
        compiler_params=_params(),
    )(projq, projq, d_o, lse, delta, qnw2, knw2, ebd, dprojq)


def _mid(projn, o_g, lse_g, x, tgt, mod, convw8, wc, wa, wo, ebd, *, tm=256):
    S, D = x.shape
    nt = S // tm
    nst = max(1, 256 // tm)
    assert nt % nst == 0
    hb = tm // 16

    def body(pn_ref, hc_ref, hx_ref, o0_ref, o1_ref, o2_ref, l0_ref, l1_ref, l2_ref, x_ref, t_ref, mod_ref,
             cw_ref, ebd_ref, wc_hbm, wa_hbm, wo_hbm,
             dpn_ref, do_ref, lse_ref, dl_ref, dout_ref, vacc_ref, gwo_hbm, gwc_hbm, gwa_hbm,
             wc_s, wa_s, wo_s, gwo_s, gwc_s, gwa_s, carry_s, *stashes):
        i = pl.program_id(0)
        ti = nt - 1 - i
        par = i % nst
        stash = pl.ds(pl.multiple_of(par * tm, tm), tm)

        @pl.when(i == 0)
        def _():
            for src, dst in ((wc_hbm, wc_s), (wa_hbm, wa_s), (wo_hbm, wo_s)):
                pltpu.sync_copy(src, dst)
            gwo_s[...] = jnp.zeros_like(gwo_s)
            gwc_s[...] = jnp.zeros_like(gwc_s)
            gwa_s[...] = jnp.zeros_like(gwa_s)
            carry_s[...] = jnp.zeros_like(carry_s)
            vacc_ref[...] = jnp.zeros_like(vacc_ref)

        rows = lax.broadcasted_iota(jnp.int32, (tm, D), 0)
        w0, w1, w2 = cw_ref[0:1, :], cw_ref[1:2, :], cw_ref[2:3, :]
        gate = mod_ref[:, 2 * D:3 * D]

        b_a = pn_ref[:, BA:BA + D].astype(F32)
        c_a = pn_ref[:, CA:CA + D].astype(F32)
        x_a = pn_ref[:, XA:XA + D].astype(F32)
        z_a = pn_ref[:, ZA:ZA + D].astype(F32)
        u = c_a * x_a
        has_prev = (ti > 0).astype(F32)
        uh = hc_ref[...].astype(F32) * hx_ref[...].astype(F32) * has_prev
        h14, h15 = _row(uh, 14), _row(uh, 15)
        u_m1 = jnp.where(rows == 0, h15, pltpu.roll(u, 1, 0))
        u_m2 = jnp.where(rows == 0, h14, jnp.where(rows == 1, h15, pltpu.roll(u, 2, 0)))
        conv = w0 * u_m2 + w1 * u_m1 + w2 * u
        sg_a = _sigmoid(z_a)
        sz_a = z_a * sg_a
        y_a = b_a * conv * sz_a
        y_ab = y_a.astype(BF16)
        pa = jnp.dot(y_ab, wc_s[...], preferred_element_type=F32)

        l0, l1, l2 = l0_ref[...], l1_ref[...], l2_ref[...]
        mxl = jnp.maximum(jnp.maximum(l0, l1), l2)
        e0, e1, e2 = jnp.exp(l0 - mxl), jnp.exp(l1 - mxl), jnp.exp(l2 - mxl)
        den = e0 + e1 + e2
        attn = (e0 * o0_ref[...] + e1 * o1_ref[...] + e2 * o2_ref[...]) / den
        lse_ref[...] = mxl + jnp.log(den)
        z_b = pn_ref[:, ZB:ZB + ATTN_OUT].astype(F32)
        sg_b = _sigmoid(z_b)
        sz_b = z_b * sg_b
        y_b = attn * sz_b
        y_bb = y_b.astype(BF16)
        pb = jnp.dot(y_bb, wa_s[...], preferred_element_type=F32)

        s_a = _sigmoid(pn_ref[:, GA:GA + D].astype(F32))
        s_b = _sigmoid(pn_ref[:, GB:GB + D].astype(F32))
        merged = s_a * pa + s_b * pb
        merged_b = merged.astype(BF16)
        mo = jnp.dot(merged_b, wo_s[...], preferred_element_type=F32)
        diff = x_ref[...] + gate * mo - t_ref[...]
        dout = diff * (1.0 / D)
        dout_ref[...] = dout
        vacc_ref[4:5, :] += jnp.sum(diff * diff, axis=0, keepdims=True)
        vacc_ref[0:1, :] += jnp.sum(dout * mo, axis=0, keepdims=True)

        d_mo = (dout * gate).astype(BF16)
        nt_dims = (((1,), (1,)), ((), ()))
        dmerged = lax.dot_general(d_mo, wo_s[...], nt_dims, preferred_element_type=F32)
        dpa = (dmerged * s_a).astype(BF16)
        dpb = (dmerged * s_b).astype(BF16)
        dpn_ref[:, GA:GA + D] = (dmerged * pa * s_a * (1.0 - s_a)).astype(BF16)
        dpn_ref[:, GB:GB + D] = (dmerged * pb * s_b * (1.0 - s_b)).astype(BF16)
        tn = (((0,), (0,)), ((), ()))
        if nst == 1:
            gwo_s[...] += lax.dot_general(merged_b, d_mo, tn, preferred_element_type=F32)
            gwc_s[...] += lax.dot_general(y_ab, dpa, tn, preferred_element_type=F32)
            gwa_s[...] += lax.dot_general(y_bb, dpb, tn, preferred_element_type=F32)
        else:
            st_m, st_d, st_ya, st_pa, st_yb, st_pb = stashes
            st_m[stash, :] = merged_b
            st_d[stash, :] = d_mo
            st_ya[stash, :] = y_ab
            st_pa[stash, :] = dpa
            st_yb[stash, :] = y_bb
            st_pb[stash, :] = dpb

            @pl.when(par == nst - 1)
            def _():
                gwo_s[...] += lax.dot_general(st_m[...], st_d[...], tn, preferred_element_type=F32)
                gwc_s[...] += lax.dot_general(st_ya[...], st_pa[...], tn, preferred_element_type=F32)
                gwa_s[...] += lax.dot_general(st_yb[...], st_pb[...], tn, preferred_element_type=F32)

        dy_a = lax.dot_general(dpa, wc_s[...], nt_dims, preferred_element_type=F32)
        dy_b = lax.dot_general(dpb, wa_s[...], nt_dims, preferred_element_type=F32)

        dattn = dy_b * sz_b
        dpn_ref[:, ZB:ZB + ATTN_OUT] = (dy_b * attn * sg_b * (1.0 + z_b * (1.0 - sg_b))).astype(BF16)
        do_ref[...] = dattn.astype(BF16)
        dl_ref[...] = jnp.dot((dattn * attn).astype(BF16), ebd_ref[...], preferred_element_type=F32)

        dpn_ref[:, BA:BA + D] = (dy_a * conv * sz_a).astype(BF16)
        dpn_ref[:, ZA:ZA + D] = (dy_a * b_a * conv * sg_a * (1.0 + z_a * (1.0 - sg_a))).astype(BF16)
        dconv = dy_a * b_a * sz_a
        vacc_ref[1:2, :] += jnp.sum(dconv * u_m2, axis=0, keepdims=True)
        vacc_ref[2:3, :] += jnp.sum(dconv * u_m1, axis=0, keepdims=True)
        vacc_ref[3:4, :] += jnp.sum(dconv * u, axis=0, keepdims=True)
        nxt = carry_s[...]
        n0, n1 = _row(nxt, 0), _row(nxt, 1)
        dc_p1 = jnp.where(rows == tm - 1, n0, pltpu.roll(dconv, tm - 1, 0))
        dc_p2 = jnp.where(rows == tm - 1, n1, jnp.where(rows == tm - 2, n0, pltpu.roll(dconv, tm - 2, 0)))
        du = w2 * dconv + w1 * dc_p1 + w0 * dc_p2
        carry_s[...] = dconv[0:8, :]
        dpn_ref[:, CA:CA + D] = (du * x_a).astype(BF16)
        dpn_ref[:, XA:XA + D] = (du * c_a).astype(BF16)

        @pl.when(i == nt - 1)
        def _():
            pltpu.sync_copy(gwo_s, gwo_hbm)
            pltpu.sync_copy(gwc_s, gwc_hbm)
            pltpu.sync_copy(gwa_s, gwa_hbm)

    rev = lambda i: (nt - 1 - i, 0)
    halo = lambda col: pl.BlockSpec((16, D), lambda i: (jnp.maximum((nt - 1 - i) * hb - 1, 0), col))
    tile512 = pl.BlockSpec((tm, ATTN_OUT), rev)
    tileD = pl.BlockSpec((tm, D), rev)
    const = lambda shape: pl.BlockSpec(shape, lambda i: (0, 0))
    return pl.pallas_call(
        body, name="mid_fwd_bwd", grid=(nt,),
        in_specs=[pl.BlockSpec((tm, NAT), rev), halo(CA // D), halo(XA // D)] + [tile512] * 6
                 + [tileD, tileD, const((1, 3 * D)), const((8, D)), const((ATTN_OUT, ATTN_OUT))] + [ANY] * 3,
        out_specs=[pl.BlockSpec((tm, NAT), rev), tile512, tile512, tile512, tileD, const((8, D)), ANY, ANY, ANY],
        out_shape=[jax.ShapeDtypeStruct((S, NAT), BF16), jax.ShapeDtypeStruct((S, ATTN_OUT), BF16),
                   jax.ShapeDtypeStruct((S, ATTN_OUT), F32), jax.ShapeDtypeStruct((S, ATTN_OUT), F32),
                   jax.ShapeDtypeStruct((S, D), F32), jax.ShapeDtypeStruct((8, D), F32),
                   jax.ShapeDtypeStruct((D, D), F32), jax.ShapeDtypeStruct((D, D), F32),
                   jax.ShapeDtypeStruct((ATTN_OUT, D), F32)],
        scratch_shapes=[pltpu.VMEM((D, D), BF16), pltpu.VMEM((ATTN_OUT, D), BF16), pltpu.VMEM((D, D), BF16),
                        pltpu.VMEM((D, D), F32), pltpu.VMEM((D, D), F32), pltpu.VMEM((ATTN_OUT, D), F32),
                        pltpu.VMEM((8, D), F32)]
                       + ([pltpu.VMEM((nst * tm, D), BF16)] * 4
                          + [pltpu.VMEM((nst * tm, ATTN_OUT), BF16), pltpu.VMEM((nst * tm, D), BF16)] if nst > 1 else []),
        compiler_params=_params(vmem=60 * 1024 * 1024),
    )(projn, projn, projn, *o_g, *lse_g, x, tgt, mod, convw8, ebd, wc, wa, wo)


def _dh_matmul(dpn, dpq, w_nat, w_qkv, parts, *, tm=512):
    S = dpn.shape[0]
    D = D_MODEL
    n = len(parts)
    nt = S // tm

    def body(dn_ref, dq_ref, wn_hbm, wq_hbm, *rest):
        p_hbm, rest = rest[:n], rest[n:]
        dh_ref = rest[0]
        q_hbm = rest[1:1 + n]
        wn_s, wq_s = rest[1 + n:3 + n]
        sems = rest[3 + n:]

        @pl.when(pl.program_id(0) == 0)
        def _():
            if n:
                mine, _ = _scatter_copies(p_hbm, q_hbm, *sems)
                for cp in mine:
                    cp.start()
            pltpu.sync_copy(wn_hbm, wn_s)
            pltpu.sync_copy(wq_hbm, wq_s)

        if n:
            @pl.when(pl.program_id(0) == nt - 1)
            def _():
                mine, theirs = _scatter_copies(p_hbm, q_hbm, *sems)
                for cp in theirs:
                    cp.wait_recv()
                for cp in mine:
                    cp.wait_send()

        nt_dims = (((1,), (1,)), ((), ()))
        dh = (lax.dot_general(dn_ref[...], wn_s[...], nt_dims, preferred_element_type=F32)
              + lax.dot_general(dq_ref[...], wq_s[...], nt_dims, preferred_element_type=F32))
        dh_ref[...] = dh.astype(BF16)

    row = lambda w: pl.BlockSpec((tm, w), lambda i: (i, 0))
    outs = pl.pallas_call(
        body, name="dh_matmul", grid=(nt,),
        in_specs=[row(NAT), row(NQKV), ANY, ANY] + [ANY] * n,
        out_specs=[row(D)] + [ANY] * n,
        out_shape=[jax.ShapeDtypeStruct((S, D), BF16)] + [jax.ShapeDtypeStruct(p.shape, p.dtype) for p in parts],
        scratch_shapes=[pltpu.VMEM((D, NAT), BF16), pltpu.VMEM((D, NQKV), BF16)]
                       + ([pltpu.SemaphoreType.DMA((n, 3)), pltpu.SemaphoreType.DMA((n, 3))] if n else []),
        compiler_params=_params(vmem=60 * 1024 * 1024),
    )(dpn, dpq, w_nat, w_qkv, *parts)
    return outs[0], outs[1:]


def _norm_bwd(dh, x, dout, mod, norm_w, *, tm=512):
    S, D = x.shape

    def body(dh_ref, x_ref, dout_ref, mod_ref, nw_ref, gx_ref, st_ref):
        @pl.when(pl.program_id(0) == 0)
        def _():
            st_ref[...] = jnp.zeros_like(st_ref)

        dh = dh_ref[...].astype(F32)
        scale1 = 1.0 + mod_ref[:, D:2 * D]
        nw = nw_ref[...]
        xv = x_ref[...]
        rstd = lax.rsqrt(jnp.mean(xv * xv, axis=-1, keepdims=True) + EPS)
        xh = xv * rstd
        dhs = dh * scale1
        dxh = dhs * nw
        dx = rstd * (dxh - xh * jnp.mean(dxh * xh, axis=-1, keepdims=True))
        gx_ref[...] = dout_ref[...] + dx
        st_ref[0:1, :] += jnp.sum(dh, axis=0, keepdims=True)
        st_ref[1:2, :] += jnp.sum(dh * (xh * nw), axis=0, keepdims=True)
        st_ref[2:3, :] += jnp.sum(dhs * xh, axis=0, keepdims=True)

    row = pl.BlockSpec((tm, D), lambda i: (i, 0))
    return pl.pallas_call(
        body, name="norm_bwd", grid=(S // tm,),
        in_specs=[row, row, row, pl.BlockSpec((1, 3 * D), lambda i: (0, 0)), pl.BlockSpec((1, D), lambda i: (0, 0))],
        out_specs=[row, pl.BlockSpec((8, D), lambda i: (0, 0))],
        out_shape=[jax.ShapeDtypeStruct((S, D), F32), jax.ShapeDtypeStruct((8, D), F32)],
        compiler_params=_params(),
    )(dh, x, dout, mod, norm_w)


def _pack_smalls(st, vacc, gqk):
    D = D_MODEL

    def body(st_ref, va_ref, gqk_ref, o_ref):
        o_ref[...] = jnp.zeros_like(o_ref)
        o_ref[0:2, :] = st_ref[0:2, :]
        o_ref[2:3, :] = va_ref[0:1, :]
        o_ref[8:9, :] = st_ref[2:3, :]
        o_ref[16:19, :] = va_ref[1:4, :]
        tot = gqk_ref[0]
        for k in range(1, gqk_ref.shape[0]):
            tot = tot + gqk_ref[k]
        tot = tot + pltpu.roll(tot, HEAD_DIM, 1)
        o_ref[24:32, 0:128] = tot
        loss = jnp.sum(va_ref[4:5, :], axis=1, keepdims=True) * (0.5 / D)
        o_ref[26:27, :] = jnp.broadcast_to(loss, (1, D))

    return pl.pallas_call(
        body, name="pack_smalls", out_shape=jax.ShapeDtypeStruct((32, D), F32),
        in_specs=[pl.BlockSpec(memory_space=pltpu.VMEM)] * 3,
        out_specs=pl.BlockSpec(memory_space=pltpu.VMEM), compiler_params=_params(),
    )(st, vacc, gqk)


def _small_update(sm_loc, dm_pad, ct_pad, ada, vecs):
    D = D_MODEL
    nv = len(vecs)
    places = [(0, 3, D), (8, 1, D), (16, 3, 256), (24, 1, HEAD_DIM), (25, 1, HEAD_DIM)]

    def body(*refs):
        sm_ref, dm_ref, ct_ref = refs[0:3]
        wmv = [refs[3 + 3 * t:6 + 3 * t] for t in range(nv + 1)]
        outs = refs[3 + 3 * (nv + 1):]
        res = [outs[4 * t:4 * t + 4] for t in range(nv + 1)]
        loss_ref, tot_s = outs[4 * (nv + 1)], outs[4 * (nv + 1) + 1]

        def update(g, w, m, v, out, at=(slice(None), slice(None))):
            d, mn, vn = _adamw_math(w[at], g, m[at], v[at])
            for ref, val in zip(out, (g, d, mn, vn)):
                ref[at] = val

        gw = jnp.dot(ct_ref[...], dm_ref[...], preferred_element_type=F32, precision=lax.Precision.HIGHEST)
        update(gw, *wmv[0], res[0])
        tot = sm_ref[0]
        for k in range(1, 8):
            tot = tot + sm_ref[k]
        tot_s[...] = tot
        for k in range(3):
            update(tot_s[k:k + 1, :], *wmv[1], res[1], at=(slice(None), slice(k * D, (k + 1) * D)))
        for t in range(1, nv):
            r0, nr, nl = places[t]
            update(tot_s[r0:r0 + nr, 0:nl], *wmv[1 + t], res[1 + t])
        loss_ref[...] = jnp.broadcast_to(tot_s[26:27, 0:128], (8, 128))

    vm = pl.BlockSpec(memory_space=pltpu.VMEM)
    groups = [ada] + list(vecs)
    out_shape = []
    for w, _, _ in groups:
        out_shape += [jax.ShapeDtypeStruct(w.shape, F32)] * 4
    out_shape.append(jax.ShapeDtypeStruct((8, 128), F32))
    flat = [a for grp in groups for a in grp]
    outs = pl.pallas_call(
        body, name="small_update", out_shape=out_shape,
        in_specs=[vm] * (3 + len(flat)), out_specs=[vm] * len(out_shape),
        scratch_shapes=[pltpu.VMEM((32, D), F32)], compiler_params=_params(),
    )(sm_loc, dm_pad, ct_pad, *flat)
    return [outs[4 * t:4 * t + 4] for t in range(nv + 1)], outs[-1]


def _local_step(xs, tg, mod, norm_w, w_nat, w_qkv, wc, wa, wo, convw8, q_norm_w, k_norm_w):
    S = xs.shape[0]
    lane = jnp.arange(128)
    ebd128 = (lane[:, None] // HEAD_DIM == lane[None, :] // HEAD_DIM).astype(BF16)
    lane5 = jnp.arange(ATTN_OUT)
    ebd512 = (lane5[:, None] // HEAD_DIM == lane5[None, :] // HEAD_DIM).astype(BF16)
    qnw2 = jnp.tile(q_norm_w, (1, 2))
    knw2 = jnp.tile(k_norm_w, (1, 2))

    projn, h, ht = _inproj_nat(xs, mod, norm_w, w_nat)
    projq = _matmul(h, w_qkv, tm=1024, tn=1536, name="inproj_qkv")
    o_g, lse_g = [], []
    for g in range(N_GROUPS):
        o, l = _attn_fwd(projq, qnw2, knw2, ebd128, g)
        o_g.append(o)
        lse_g.append(l)

    dpn, d_o, lse, delta, dout, vacc, gwo, gwc, gwa = _mid(
        projn, o_g, lse_g, xs, tg, mod, convw8, wc, wa, wo, ebd512)
    dpq = lax.empty((S, NQKV), BF16)
    gqk = []
    for g in range(N_GROUPS):
        dpq, part = _attn_bwd(projq, dpq, d_o, lse, delta, qnw2, knw2, ebd128, g)
        gqk.append(part)
    gw_nat = _matmul_acc(ht, dpn, tn=1664, tk=2048, name="grad_w_in_nat")
    gw_qkv = _matmul_acc(ht, dpq, tn=1536, tk=2048, name="grad_w_in_qkv")
    return dpn, dpq, dout, gw_nat, gw_qkv, gwc, gwa, gwo, vacc, gqk


def kernel(x, c, w_ada, b_ada, norm_w, w_in, conv_w, q_norm_w, k_norm_w, w_br_conv, w_br_attn, w_out, loss_target, m_w_ada, m_b_ada, m_norm_w, m_w_in, m_conv_w, m_q_norm_w, m_k_norm_w, m_w_br_conv, m_w_br_attn, m_w_out, v_w_ada, v_b_ada, v_norm_w, v_w_in, v_conv_w, v_q_norm_w, v_k_norm_w, v_w_br_conv, v_w_br_attn, v_w_out):
    D = D_MODEL
    S = x.shape[1]
    xs, tg = x[0], loss_target[0]
    mx, my, mc = lax.axis_index("x"), lax.axis_index("y"), lax.axis_index("c")
    chip = 2 * mx + my
    dev = 2 * chip + mc

    c_blk = jnp.concatenate([c, jnp.pad(conv_w[0], ((0, 0), (0, D - 256))), jnp.zeros((4, D), F32)], axis=0)
    b_shard = lax.dynamic_slice(b_ada, (0, chip * 768), (1, 768))
    shards = [w_in[0].astype(BF16).reshape(2, 512, 2816), w_br_conv[0].astype(BF16).reshape(2, 128, D),
              w_br_attn[0].astype(BF16).reshape(2, 256, 256), w_out[0].astype(BF16).reshape(2, 128, D)]
    gathered_in, gathered_w, cg, mod_parts, silu_c = _weights_allgather(
        shards[0].reshape(2, 2, 256, SHARD_COLS), shards[1:], c_blk, w_ada[0], b_shard)
    convw8 = jnp.pad(cg[::2, 1:4, 0:256].transpose(1, 0, 2).reshape(3, D), ((0, 5), (0, 0)))
    mod_all = mod_parts.transpose(1, 0, 2).reshape(8, 3 * D)
    mod = lax.dynamic_slice(mod_all, (dev, 0), (1, 3 * D))
    g_bc, g_ba, g_wo = [lax.dynamic_update_slice(g, s[None], (chip, 0, 0, 0))
                        for g, s in zip(gathered_w, shards[1:])]
    chipv = jnp.reshape(chip, (1,)).astype(jnp.int32)
    g_in = gathered_in.reshape(4, D, SHARD_COLS)
    own_in = shards[0].reshape(D, SHARD_COLS)
    w_nat = _weights_prep(g_in, own_in, chipv, "nat")
    w_qkv = _weights_prep(g_in, own_in, chipv, "qkv")
    wc = g_bc.reshape(D, D)
    wa = g_ba.reshape(4, ATTN_OUT, 256).transpose(1, 0, 2).reshape(ATTN_OUT, D)
    wo = g_wo.reshape(D, D)

    dpn, dpq, dout, gw_nat, gw_qkv, gwc, gwa, gwo, vacc, gqk = _local_step(
        xs, tg, mod, norm_w, w_nat, w_qkv, wc, wa, wo, convw8, q_norm_w, k_norm_w)

    def halves(a):
        k, r, cc = a.shape
        return a.reshape(k, 2, r // 2, cc).transpose(1, 0, 2, 3)

    grads = [gw_nat.reshape(2, D // 2, NAT), gw_qkv.reshape(2, D // 2, NQKV),
             halves(gwc.astype(BF16).reshape(4, 256, D)),
             halves(gwa.astype(BF16).reshape(ATTN_OUT, 4, 256).transpose(1, 0, 2)),
             halves(gwo.astype(BF16).reshape(4, 256, D))]
    recv = _pair_exchange(grads)
    core = jnp.reshape(mc, (1,)).astype(jnp.int32)
    p_in = _pair_add_to_shards(core, grads[0], recv[0], lax.empty((4, D // 2, SHARD_COLS), BF16), "nat")
    p_in = _pair_add_to_shards(core, grads[1], recv[1], p_in, "qkv")
    parts = [p_in] + [_pair_add(core, gr, rc) for gr, rc in zip(grads[2:], recv[2:])]
    dh, gathered = _dh_matmul(dpn, dpq, w_nat, w_qkv, parts)
    grad_x, st = _norm_bwd(dh, xs, dout, mod, norm_w)
    mine = [_sum4(chipv, p, q) for p, q in zip(parts, gathered)]
    theirs, sm_all = _half_exchange(mine, _pack_smalls(st, vacc, jnp.concatenate(gqk, axis=0)))

    big = {}
    for t, (nm, w, m, v) in enumerate((("w_in", w_in, m_w_in, v_w_in), ("w_br_conv", w_br_conv, m_w_br_conv, v_w_br_conv),
                                       ("w_br_attn", w_br_attn, m_w_br_attn, v_w_br_attn), ("w_out", w_out, m_w_out, v_w_out))):
        big[nm] = _adamw_halves(core, w[0], mine[t], theirs[t], m[0], v[0], "adamw_" + nm)

    dmod_all = sm_all[:, 0:3, :].reshape(8, 3 * D)
    dm_pad = jnp.pad(lax.dynamic_slice(dmod_all, (0, chip * 768), (8, 768)), ((0, 120), (0, 0)))
    conv_dev = lax.dynamic_slice(sm_all[:, 16:24, :], (0, 0, chip * 256), (8, 8, 256))
    sm_loc = jnp.concatenate([sm_all[:, 0:16, :], jnp.pad(conv_dev, ((0, 0), (0, 0), (0, D - 256))), sm_all[:, 24:32, :]], axis=1)

    ct_pad = jnp.pad(silu_c.T, ((0, 0), (0, 120)))
    small_names = ["b_ada", "norm_w", "conv_w", "q_norm_w", "k_norm_w"]
    vecs = [(b_ada, m_b_ada, v_b_ada), (norm_w, m_norm_w, v_norm_w), (conv_w[0], m_conv_w[0], v_conv_w[0]),
            (q_norm_w, m_q_norm_w, v_q_norm_w), (k_norm_w, m_k_norm_w, v_k_norm_w)]
    updated, loss_blk = _small_update(sm_loc, dm_pad, ct_pad, (w_ada[0], m_w_ada[0], v_w_ada[0]), vecs)
    small = {"w_ada": [a[None] for a in updated[0]]}
    for nm, four in zip(small_names, updated[1:]):
        small[nm] = [a[None] for a in four] if nm == "conv_w" else list(four)

    order = ["w_ada", "b_ada", "norm_w", "w_in", "conv_w", "q_norm_w", "k_norm_w", "w_br_conv", "w_br_attn", "w_out"]
    res = [loss_blk[0, 0], grad_x[None]]
    for kind in range(4):
        for nm in order:
            res.append(big[nm][kind][None] if nm in big else small[nm][kind])
    return tuple(res)
```

```python
import functools

import jax
import jax.numpy as jnp
from jax import lax
from jax.experimental import pallas as pl
from jax.experimental.pallas import tpu as pltpu

F32 = jnp.float32
BF16 = jnp.bfloat16
MESH = pl.DeviceIdType.MESH
ANY = pl.BlockSpec(memory_space=pl.ANY)

D_MODEL = 1024
HEAD_DIM = 64
N_GROUPS = 3
DILATIONS = (1, 4, 16)
ATTN_OUT = 512
EPS = 1e-6
NEG_INF = -1e30
NAT = 6656
NQKV = 4608
BA, CA, XA, ZA, ZB, GA, GB = 0, 1024, 2048, 3072, 4096, 4608, 5632
ATT_TILE = 2048
QK_SCALE = HEAD_DIM ** -0.5
LOG2E = 1.4426950408889634
LN2 = 0.6931471805599453
V7X_VMEM_BYTES = 64 * 1024 * 1024
V7X_VMEM_LIMIT = 56 * 1024 * 1024

ADAM_LR = 0.001
ADAM_B1 = 0.9
ADAM_B2 = 0.999
ADAM_EPS = 1e-08
ADAM_WD = 0.01
ADAM_STEP = 10


def _params(vmem=V7X_VMEM_LIMIT, **kw):
    return pltpu.CompilerParams(vmem_limit_bytes=vmem, **kw)


def _sigmoid(z):
    return 1.0 / (1.0 + jnp.exp(-z))


def _row(v, r):
    rows = lax.broadcasted_iota(jnp.int32, v.shape, 0)
    return jnp.sum(jnp.where(rows == r, v, 0.0), axis=0, keepdims=True)


def _coords():
    return lax.axis_index("x"), lax.axis_index("y"), lax.axis_index("c")


def _flip(v, bit):
    return 1 - v if bit else v


def _weights_allgather(big_shard, shards, c_blk, w_ada, b_shard):
    n = len(shards)
    D = D_MODEL
    cols = w_ada.shape[1]

    def body(*refs):
        big_in, ins = refs[0], refs[1:1 + n]
        c_ref, wada_ref, b_ref = refs[1 + n:4 + n]
        big_out, outs = refs[4 + n], refs[5 + n:5 + 2 * n]
        cg_ref, mp_ref, sc_ref = refs[5 + 2 * n:8 + 2 * n]
        sem_refs = refs[8 + 2 * n:]
        ssem, rsem = sem_refs[:2] if n else (None, None)
        bs, br, cs, cr, ms, mr, lsem = sem_refs[2 if n else 0:]
        mx, my, mc = _coords()
        me = 2 * mx + my
        me8 = 2 * me + mc
        sib = (mx, my, 1 - mc)
        chips = [(_flip(mx, j & 2), _flip(my, j & 1)) for j in range(1, 4)]
        sends = []

        local = pltpu.make_async_copy(c_ref, cg_ref.at[me8], lsem)
        local.start()
        for j in range(1, 8):
            peer = (_flip(mx, j & 4), _flip(my, j & 2), _flip(mc, j & 1))
            cp = pltpu.make_async_remote_copy(
                src_ref=c_ref, dst_ref=cg_ref.at[me8], send_sem=cs.at[j - 1], recv_sem=cr.at[j - 1],
                device_id=peer, device_id_type=MESH)
            cp.start()
            sends.append(cp)

        kx, ky, kd = 2 * (1 - mx) + my, 2 * mx + (1 - my), 2 * (1 - mx) + (1 - my)
        xn, yn = (1 - mx, my, mc), (mx, 1 - my, mc)

        def big(src_chip, q, sem, to, half=None):
            h = mc if half is None else half
            return pltpu.make_async_remote_copy(
                src_ref=big_out.at[src_chip, h, q], dst_ref=big_out.at[src_chip, h, q],
                send_sem=bs.at[sem], recv_sem=br.at[sem], device_id=to, device_id_type=MESH)

        def own(q, sem, to):
            return pltpu.make_async_remote_copy(
                src_ref=big_in.at[mc, q], dst_ref=big_out.at[me, mc, q],
                send_sem=bs.at[sem], recv_sem=br.at[sem], device_id=to, device_id_type=MESH)

        for cp in (own(0, 0, xn), own(1, 3, yn), own(1, 1, xn), own(0, 2, yn)):
            cp.start()
            sends.append(cp)
        for t in range(n):
            for j, (px, py) in enumerate(chips):
                cp = pltpu.make_async_remote_copy(
                    src_ref=ins[t].at[mc], dst_ref=outs[t].at[me, mc], send_sem=ssem.at[t, j],
                    recv_sem=rsem.at[t, j], device_id=(px, py, mc), device_id_type=MESH)
                cp.start()
                sends.append(cp)

        pieces = [(kx, 0), (kx, 1), (ky, 0), (ky, 1), (kd, 0), (kd, 1)]

        def landed(piece, sem, frm, pass_on=None):
            src_chip, q = pieces[piece]
            big(src_chip, q, sem, frm).wait_recv()
            nxt = ([] if pass_on is None else [big(src_chip, q, pass_on[0], pass_on[1])])
            nxt.append(big(src_chip, q, 6 + piece, sib))
            for cp in nxt:
                cp.start()
                sends.append(cp)

        for j in range(1, 8):
            px, py, pc = _flip(mx, j & 4), _flip(my, j & 2), _flip(mc, j & 1)
            pltpu.make_async_remote_copy(
                src_ref=c_ref, dst_ref=cg_ref.at[4 * px + 2 * py + pc], send_sem=cs.at[j - 1],
                recv_sem=cr.at[j - 1], device_id=(px, py, pc), device_id_type=MESH).wait_recv()
        local.wait()
        pick = (lax.broadcasted_iota(jnp.int32, (8, 64), 1) == 8 * lax.broadcasted_iota(jnp.int32, (8, 64), 0))
        cv = jnp.dot(pick.astype(F32), cg_ref[...].reshape(64, D), preferred_element_type=F32,
                     precision=lax.Precision.HIGHEST)
        sc = cv * _sigmoid(cv)
        sc_ref[...] = sc
        mp_ref[me] = jnp.dot(sc, wada_ref[...], preferred_element_type=F32,
                             precision=lax.Precision.HIGHEST) + b_ref[...]
        for j, (px, py) in enumerate(chips):
            cp = pltpu.make_async_remote_copy(
                src_ref=mp_ref.at[me], dst_ref=mp_ref.at[me], send_sem=ms.at[j], recv_sem=mr.at[j],
                device_id=(px, py, mc), device_id_type=MESH)
            cp.start()
            sends.append(cp)

        landed(0, 0, xn, pass_on=(4, yn))
        landed(3, 3, yn, pass_on=(5, xn))
        landed(1, 1, xn)
        landed(2, 2, yn)
        for j, (px, py) in enumerate(chips):
            pltpu.make_async_remote_copy(
                src_ref=mp_ref.at[me], dst_ref=mp_ref.at[2 * px + py], send_sem=ms.at[j], recv_sem=mr.at[j],
                device_id=(px, py, mc), device_id_type=MESH).wait_recv()
        landed(4, 4, yn)
        landed(5, 5, xn)
        for t in range(n):
            for j, (px, py) in enumerate(chips):
                src = 2 * px + py
                pltpu.make_async_remote_copy(
                    src_ref=ins[t].at[mc], dst_ref=outs[t].at[src, mc], send_sem=ssem.at[t, j],
                    recv_sem=rsem.at[t, j], device_id=(px, py, mc), device_id_type=MESH).wait_recv()
                fwd = pltpu.make_async_remote_copy(
                    src_ref=outs[t].at[src, mc], dst_ref=outs[t].at[src, mc], send_sem=ssem.at[t, 3 + j],
                    recv_sem=rsem.at[t, 3 + j], device_id=sib, device_id_type=MESH)
                fwd.start()
                sends.append(fwd)
        for t in range(n):
            for j, (px, py) in enumerate(chips):
                src = 2 * px + py
                pltpu.make_async_remote_copy(
                    src_ref=outs[t].at[src, 1 - mc], dst_ref=outs[t].at[src, 1 - mc], send_sem=ssem.at[t, 3 + j],
                    recv_sem=rsem.at[t, 3 + j], device_id=sib, device_id_type=MESH).wait_recv()
        for i, (src_chip, q) in enumerate(pieces):
            big(src_chip, q, 6 + i, sib, half=1 - mc).wait_recv()
        for cp in sends:
            cp.wait_send()

    vm = pl.BlockSpec(memory_space=pltpu.VMEM)
    outs = pl.pallas_call(
        body, name="weights_allgather",
        out_shape=[jax.ShapeDtypeStruct((4,) + big_shard.shape, big_shard.dtype)]
                  + [jax.ShapeDtypeStruct((4,) + s.shape, s.dtype) for s in shards]
                  + [jax.ShapeDtypeStruct((8,) + c_blk.shape, F32), jax.ShapeDtypeStruct((4, 8, cols), F32),
                     jax.ShapeDtypeStruct((8, D), F32)],
        in_specs=[ANY] * (n + 1) + [vm] * 3, out_specs=[ANY] * (n + 1) + [vm] * 3,
        scratch_shapes=([pltpu.SemaphoreType.DMA((n, 6)), pltpu.SemaphoreType.DMA((n, 6))] if n else [])
                       + [pltpu.SemaphoreType.DMA((12,)), pltpu.SemaphoreType.DMA((12,)),
                          pltpu.SemaphoreType.DMA((7,)), pltpu.SemaphoreType.DMA((7,)),
                          pltpu.SemaphoreType.DMA((3,)), pltpu.SemaphoreType.DMA((3,)), pltpu.SemaphoreType.DMA(())],
        compiler_params=_params(),
    )(big_shard, *shards, c_blk, w_ada, b_shard)
    return outs[0], outs[1:1 + n], outs[1 + n], outs[2 + n], outs[3 + n]


def _shard_gather_copies(ins, outs, ssem, rsem):
    mx, my, mc = _coords()
    me = 2 * mx + my
    sib = (mx, my, 1 - mc)
    send, recv, fwd, fwd_recv = [], [], [], []
    for t in range(len(ins)):
        for j in range(3):
            px, py = _flip(mx, (j + 1) & 2), _flip(my, (j + 1) & 1)
            src = 2 * px + py
            far = dict(send_sem=ssem.at[t, j], recv_sem=rsem.at[t, j], device_id=(px, py, mc), device_id_type=MESH)
            near = dict(send_sem=ssem.at[t, 3 + j], recv_sem=rsem.at[t, 3 + j], device_id=sib, device_id_type=MESH)
            send.append(pltpu.make_async_remote_copy(src_ref=ins[t].at[mc], dst_ref=outs[t].at[me, mc], **far))
            recv.append(pltpu.make_async_remote_copy(src_ref=ins[t].at[mc], dst_ref=outs[t].at[src, mc], **far))
            fwd.append(pltpu.make_async_remote_copy(src_ref=outs[t].at[src, mc], dst_ref=outs[t].at[src, mc], **near))
            fwd_recv.append(pltpu.make_async_remote_copy(
                src_ref=outs[t].at[src, 1 - mc], dst_ref=outs[t].at[src, 1 - mc], **near))
    return send, recv, fwd, fwd_recv


def _pair_exchange(grads):
    n = len(grads)

    def body(*refs):
        ins, outs = refs[:n], refs[n:2 * n]
        ssem, rsem = refs[2 * n:]
        mx, my, mc = _coords()
        sib = (mx, my, 1 - mc)
        sends = []
        for t in range(n):
            cp = pltpu.make_async_remote_copy(
                src_ref=ins[t].at[1 - mc], dst_ref=outs[t], send_sem=ssem.at[t], recv_sem=rsem.at[t],
                device_id=sib, device_id_type=MESH)
            cp.start()
            sends.append(cp)
        for cp in sends:
            cp.wait_recv()
        for cp in sends:
            cp.wait_send()

    return pl.pallas_call(
        body, name="grad_pair_exchange",
        out_shape=[jax.ShapeDtypeStruct(g.shape[1:], g.dtype) for g in grads],
        in_specs=[ANY] * n, out_specs=[ANY] * n,
        scratch_shapes=[pltpu.SemaphoreType.DMA((n,)), pltpu.SemaphoreType.DMA((n,))],
    )(*grads)


def _scatter_copies(ins, outs, ssem, rsem):
    mx, my, mc = _coords()
    me = 2 * mx + my
    mine, theirs = [], []
    for t in range(len(ins)):
        for j in range(1, 4):
            px, py = _flip(mx, j & 2), _flip(my, j & 1)
            mine.append(pltpu.make_async_remote_copy(
                src_ref=ins[t].at[2 * px + py], dst_ref=outs[t].at[me], send_sem=ssem.at[t, j - 1],
                recv_sem=rsem.at[t, j - 1], device_id=(px, py, mc), device_id_type=MESH))
            theirs.append(pltpu.make_async_remote_copy(
                src_ref=ins[t].at[me], dst_ref=outs[t].at[2 * px + py], send_sem=ssem.at[t, j - 1],
                recv_sem=rsem.at[t, j - 1], device_id=(px, py, mc), device_id_type=MESH))
    return mine, theirs


def _half_exchange(halves, smalls):
    n = len(halves)

    def body(*refs):
        ins, sm_ref = refs[:n], refs[n]
        outs, all_ref = refs[n + 1:2 * n + 1], refs[2 * n + 1]
        ssem, rsem, gsem, hsem, lsem = refs[2 * n + 2:]
        mx, my, mc = _coords()
        me = 4 * mx + 2 * my + mc
        sib = (mx, my, 1 - mc)
        local = pltpu.make_async_copy(sm_ref, all_ref.at[me], lsem)
        local.start()
        sends = []
        for t in range(n):
            rc = pltpu.make_async_remote_copy(
                src_ref=ins[t], dst_ref=outs[t], send_sem=ssem.at[t], recv_sem=rsem.at[t],
                device_id=sib, device_id_type=MESH)
            rc.start()
            sends.append(rc)
        gathers = []
        for j in range(1, 8):
            peer = (_flip(mx, j & 4), _flip(my, j & 2), _flip(mc, j & 1))
            cp = pltpu.make_async_remote_copy(
                src_ref=sm_ref, dst_ref=all_ref.at[me], send_sem=gsem.at[j - 1], recv_sem=hsem.at[j - 1],
                device_id=peer, device_id_type=MESH)
            cp.start()
            gathers.append(cp)
        for cp in sends:
            cp.wait_recv()
        for j in range(1, 8):
            px, py, pc = _flip(mx, j & 4), _flip(my, j & 2), _flip(mc, j & 1)
            pltpu.make_async_remote_copy(
                src_ref=sm_ref, dst_ref=all_ref.at[4 * px + 2 * py + pc], send_sem=gsem.at[j - 1],
                recv_sem=hsem.at[j - 1], device_id=(px, py, pc), device_id_type=MESH).wait_recv()
        for cp in sends + gathers:
            cp.wait_send()
        local.wait()

    vm = pl.BlockSpec(memory_space=pltpu.VMEM)
    outs = pl.pallas_call(
        body, name="grad_half_exchange",
        out_shape=[jax.ShapeDtypeStruct(h.shape, h.dtype) for h in halves]
                  + [jax.ShapeDtypeStruct((8,) + smalls.shape, smalls.dtype)],
        in_specs=[ANY] * n + [vm], out_specs=[ANY] * n + [vm],
        scratch_shapes=[pltpu.SemaphoreType.DMA((n,)), pltpu.SemaphoreType.DMA((n,)),
                        pltpu.SemaphoreType.DMA((7,)), pltpu.SemaphoreType.DMA((7,)), pltpu.SemaphoreType.DMA(())],
    )(*halves, smalls)
    return outs[:n], outs[n]


def _row_tile(rows, cols, bytes_per_row_elem=4, budget=2 * 1024 * 1024):
    t = rows
    while t % 2 == 0 and t > 16 and t * cols * bytes_per_row_elem > budget:
        t //= 2
    return t


def _pair_add(core, g, r):
    _, _, rh, cc = g.shape
    tr = _row_tile(rh, cc)

    def body(core_ref, g_ref, r_ref, o_ref):
        o_ref[...] = (g_ref[...].astype(F32) + r_ref[...].astype(F32)).astype(o_ref.dtype)

    return pl.pallas_call(
        body, name="grad_pair_add",
        grid_spec=pltpu.PrefetchScalarGridSpec(
            num_scalar_prefetch=1, grid=(4, rh // tr),
            in_specs=[pl.BlockSpec((None, None, tr, cc), lambda k, i, c: (c[0], k, i, 0)),
                      pl.BlockSpec((None, tr, cc), lambda k, i, c: (k, i, 0))],
            out_specs=pl.BlockSpec((None, tr, cc), lambda k, i, c: (k, i, 0))),
        out_shape=jax.ShapeDtypeStruct(r.shape, BF16),
        compiler_params=_params(),
    )(core, g, r)


def _sum4(chip, p, q):
    _, rh, cc = q.shape
    tr = _row_tile(rh, cc)

    def body(chip_ref, p_ref, q1_ref, q2_ref, q3_ref, o_ref):
        o_ref[...] = ((p_ref[...].astype(F32) + q1_ref[...].astype(F32)) + q2_ref[...].astype(F32)) + q3_ref[...].astype(F32)

    def other(j):
        return pl.BlockSpec((None, tr, cc), lambda i, c: (jnp.bitwise_xor(c[0], j), i, 0))

    return pl.pallas_call(
        body, name="grad_sum4",
        grid_spec=pltpu.PrefetchScalarGridSpec(
            num_scalar_prefetch=1, grid=(rh // tr,),
            in_specs=[pl.BlockSpec((None, tr, cc), lambda i, c: (c[0], i, 0)), other(1), other(2), other(3)],
            out_specs=pl.BlockSpec((tr, cc), lambda i, c: (i, 0))),
        out_shape=jax.ShapeDtypeStruct((rh, cc), F32),
        compiler_params=_params(),
    )(chip, p, q, q, q)


def _adamw_math(w, g, m, v):
    m = ADAM_B1 * m + (1.0 - ADAM_B1) * g
    v = ADAM_B2 * v + (1.0 - ADAM_B2) * (g * g)
    m_hat = m / (1.0 - ADAM_B1 ** ADAM_STEP)
    v_hat = v / (1.0 - ADAM_B2 ** ADAM_STEP)
    delta = -ADAM_LR * (m_hat / (jnp.sqrt(v_hat) + ADAM_EPS) + ADAM_WD * w)
    return delta, m, v


def _adamw_halves(core, w, mine, recv, m, v, name):
    rows, cols = w.shape
    rh = rows // 2
    tr = _row_tile(rh, cols, budget=1024 * 1024)
    nh = rh // tr

    def body(core_ref, w_ref, a_ref, b_ref, m_ref, v_ref, g_ref, d_ref, mo_ref, vo_ref):
        g = jnp.where(pl.program_id(0) == core_ref[0], a_ref[...], b_ref[...])
        d, mn, vn = _adamw_math(w_ref[...], g, m_ref[...], v_ref[...])
        g_ref[...] = g
        d_ref[...] = d
        mo_ref[...] = mn
        vo_ref[...] = vn

    full = pl.BlockSpec((tr, cols), lambda h, i, c: (h * nh + i, 0))
    half = pl.BlockSpec((tr, cols), lambda h, i, c: (i, 0))
    sd = jax.ShapeDtypeStruct((rows, cols), F32)
    return pl.pallas_call(
        body, name=name,
        grid_spec=pltpu.PrefetchScalarGridSpec(
            num_scalar_prefetch=1, grid=(2, nh),
            in_specs=[full, half, half, full, full], out_specs=[full] * 4),
        out_shape=[sd, sd, sd, sd], compiler_params=_params(),
    )(core, w, mine, recv, m, v)


def _inproj_nat(x, mod, norm_w, w, *, tm=1024, tn=1664):
    S, D = x.shape
    N = w.shape[1]
    nt, nj = S // tm, N // tn
    chunk = 256

    def body(x_ref, mod_ref, nw_ref, w_ref, proj_ref, h_ref, ht_ref, h_a, ht_a, h_b, ht_b):
        i, j = pl.program_id(0), pl.program_id(1)

        def make_h(h_s, ht_s):
            shift = mod_ref[:, 0:D]
            scale1 = 1.0 + mod_ref[:, D:2 * D]
            for r in range(tm // chunk):
                xv = x_ref[pl.ds(r * chunk, chunk), :]
                ms = jnp.mean(xv * xv, axis=-1, keepdims=True)
                h = (xv * lax.rsqrt(ms + EPS) * nw_ref[...]) * scale1 + shift
                h_s[pl.ds(r * chunk, chunk), :] = h.astype(BF16)
                ht_s[:, pl.ds(r * chunk, chunk)] = h.T.astype(BF16)

        @pl.when((i == 0) & (j == 0))
        def _():
            make_h(h_a, ht_a)

        def step(cur, nxt):
            @pl.when(j == 0)
            def _():
                h_ref[...] = cur[0][...]
                ht_ref[...] = cur[1][...]

            def project():
                proj_ref[...] = jnp.dot(cur[0][...], w_ref[...], preferred_element_type=F32).astype(BF16)

            ahead = (j == nj - 1) & (i < nt - 1)

            @pl.when(ahead)
            def _():
                project()
                make_h(*nxt)

            @pl.when(jnp.logical_not(ahead))
            def _():
                project()

        @pl.when(i % 2 == 0)
        def _():
            step((h_a, ht_a), (h_b, ht_b))

        @pl.when(i % 2 == 1)
        def _():
            step((h_b, ht_b), (h_a, ht_a))

    def x_tile(i, j):
        return (jnp.where((i == 0) & (j == 0), 0, jnp.minimum(i + 1, nt - 1)), 0)

    return pl.pallas_call(
        body, name="inproj_nat", grid=(nt, nj),
        in_specs=[pl.BlockSpec((tm, D), x_tile),
                  pl.BlockSpec((1, 3 * D), lambda i, j: (0, 0)),
                  pl.BlockSpec((1, D), lambda i, j: (0, 0)),
                  pl.BlockSpec((D, tn), lambda i, j: (0, j))],
        out_specs=[pl.BlockSpec((tm, tn), lambda i, j: (i, j)),
                   pl.BlockSpec((tm, D), lambda i, j: (i, 0)),
                   pl.BlockSpec((D, tm), lambda i, j: (0, i))],
        out_shape=[jax.ShapeDtypeStruct((S, N), BF16), jax.ShapeDtypeStruct((S, D), BF16),
                   jax.ShapeDtypeStruct((D, S), BF16)],
        scratch_shapes=[pltpu.VMEM((tm, D), BF16), pltpu.VMEM((D, tm), BF16),
                        pltpu.VMEM((tm, D), BF16), pltpu.VMEM((D, tm), BF16)],
        compiler_params=_params(),
    )(x, mod, norm_w, w)


def _inproj_qkv(a, b, shards, *, tm=1024, tn=1536):
    M, K = a.shape
    N = b.shape[1]
    n = len(shards)
    ni, nj = M // tm, N // tn

    def body(a_ref, b_ref, *rest):
        ins, o_ref, outs, sems = rest[:n], rest[n], rest[n + 1:2 * n + 1], rest[2 * n + 1:]
        i, j = pl.program_id(0), pl.program_id(1)
        if n:
            @pl.when((i == 0) & (j == 0))
            def _():
                for cp in _shard_gather_copies(ins, outs, *sems)[0]:
                    cp.start()

        o_ref[...] = jnp.dot(a_ref[...], b_ref[...], preferred_element_type=F32).astype(o_ref.dtype)

        if n:
            @pl.when((i == ni - 1) & (j == nj - 1))
            def _():
                send, recv, fwd, fwd_recv = _shard_gather_copies(ins, outs, *sems)
                for r, f in zip(recv, fwd):
                    r.wait_recv()
                    f.start()
                for r in fwd_recv:
                    r.wait_recv()
                for cp in send + fwd:
                    cp.wait_send()

    outs = pl.pallas_call(
        body, name="inproj_qkv", grid=(ni, nj),
        in_specs=[pl.BlockSpec((tm, K), lambda i, j: (i, 0)), pl.BlockSpec((K, tn), lambda i, j: (0, j))] + [ANY] * n,
        out_specs=[pl.BlockSpec((tm, tn), lambda i, j: (i, j))] + [ANY] * n,
        out_shape=[jax.ShapeDtypeStruct((M, N), BF16)] + [jax.ShapeDtypeStruct((4,) + s.shape, s.dtype) for s in shards],
        scratch_shapes=[pltpu.SemaphoreType.DMA((n, 6)), pltpu.SemaphoreType.DMA((n, 6))] if n else [],
        compiler_params=_params(),
    )(a, b, *shards)
    return outs[0], outs[1:]


def _matmul_acc(a, b, *, tn, tk, name):
    M, K = a.shape
    N = b.shape[1]
    nk = K // tk

    def body(a_ref, b_ref, o_ref, acc_ref):
        k = pl.program_id(1)

        @pl.when(k == 0)
        def _():
            acc_ref[...] = jnp.zeros_like(acc_ref)

        acc_ref[...] += jnp.dot(a_ref[...], b_ref[...], preferred_element_type=F32)

        @pl.when(k == nk - 1)
        def _():
            o_ref[...] = acc_ref[...].astype(o_ref.dtype)

    return pl.pallas_call(
        body, name=name, grid=(N // tn, nk),
        in_specs=[pl.BlockSpec((M, tk), lambda j, k: (0, k)), pl.BlockSpec((tk, tn), lambda j, k: (k, j))],
        out_specs=pl.BlockSpec((M, tn), lambda j, k: (0, j)),
        out_shape=jax.ShapeDtypeStruct((M, N), BF16),
        scratch_shapes=[pltpu.VMEM((M, tn), F32)], compiler_params=_params(),
    )(a, b)


SHARD_COLS = 2816


def _column_tables(part):
    if part == "nat":
        bw = 256
        orig = [j * bw if j < 16 else 8704 + (j - 16) * bw for j in range(NAT // bw)]
    else:
        bw = 128
        orig = []
        for jj in range(NQKV // bw):
            g, p, t = jj // 12, (jj % 12) // 3, jj % 3
            orig.append(4096 + t * 1536 + g * 512 + p * 128)
    tk = jnp.array([o // SHARD_COLS for o in orig], jnp.int32)
    tc = jnp.array([(o % SHARD_COLS) // bw for o in orig], jnp.int32)
    return tk, tc, bw


def _weights_prep(gath, own, chipv, part):
    D = D_MODEL
    tk, tc, bw = _column_tables(part)
    n = tk.shape[0]
    per_step = 2
    assert n % per_step == 0

    def body(tk_ref, tc_ref, chip_ref, *refs):
        w_ref = refs[-1]
        for u in range(per_step):
            g_ref, o_ref = refs[2 * u], refs[2 * u + 1]
            mine = tk_ref[pl.program_id(0) * per_step + u] == chip_ref[0]
            w_ref[:, u * bw:(u + 1) * bw] = jnp.where(mine, o_ref[...], g_ref[...])

    def gath_spec(u):
        def idx(j, tk, tc, ch):
            k = tk[j * per_step + u]
            return (jnp.where(k == ch[0], (k + 1) % 4, k), 0, tc[j * per_step + u])
        return pl.BlockSpec((None, D, bw), idx)

    def own_spec(u):
        return pl.BlockSpec((D, bw), lambda j, tk, tc, ch: (0, tc[j * per_step + u]))

    specs = []
    for u in range(per_step):
        specs += [gath_spec(u), own_spec(u)]
    return pl.pallas_call(
        body, name="weights_prep_" + part,
        grid_spec=pltpu.PrefetchScalarGridSpec(
            num_scalar_prefetch=3, grid=(n // per_step,),
            in_specs=specs,
            out_specs=pl.BlockSpec((D, per_step * bw), lambda j, tk, tc, ch: (0, j))),
        out_shape=jax.ShapeDtypeStruct((D, n * bw), BF16),
        compiler_params=_params(),
    )(tk, tc, chipv, *([gath, own] * per_step))


def _pair_add_to_shards(core, gw, r, into, part):
    D = D_MODEL
    tk, tc, bw = _column_tables(part)
    n = tk.shape[0]

    def body(tk_ref, tc_ref, core_ref, g_ref, r_ref, into_ref, o_ref):
        o_ref[...] = (g_ref[...].astype(F32) + r_ref[...].astype(F32)).astype(BF16)

    return pl.pallas_call(
        body, name="grad_pair_add_" + part,
        grid_spec=pltpu.PrefetchScalarGridSpec(
            num_scalar_prefetch=3, grid=(n,),
            in_specs=[pl.BlockSpec((None, D // 2, bw), lambda j, tk, tc, c: (c[0], 0, j)),
                      pl.BlockSpec((D // 2, bw), lambda j, tk, tc, c: (0, j)), ANY],
            out_specs=pl.BlockSpec((None, D // 2, bw), lambda j, tk, tc, c: (tk[j], 0, tc[j]))),
        out_shape=jax.ShapeDtypeStruct(into.shape, BF16),
        input_output_aliases={5: 0},
        compiler_params=_params(),
    )(tk, tc, core, gw, r, into)


def _head_norm(xb, w, ebd):
    x = xb.astype(F32)
    ss = jnp.dot((x * x).astype(BF16), ebd, preferred_element_type=F32)
    return x * lax.rsqrt(ss * (1.0 / HEAD_DIM) + EPS) * w


def _window_mask(no_prev):
    qi = lax.broadcasted_iota(jnp.int32, (128, 256), 0)
    kc = lax.broadcasted_iota(jnp.int32, (128, 256), 1)
    ok = (kc >= qi) & (kc <= qi + 128)
    if no_prev is not None:
        ok = ok & ((kc >= 128) | jnp.logical_not(no_prev))
    return ok


def _lane_masks():
    lane = lax.broadcasted_iota(jnp.int32, (1, 128), 1)
    return [(lane < HEAD_DIM).astype(F32), (lane >= HEAD_DIM).astype(F32)]


def _head_col(v, mh):
    return jnp.sum(v * mh, axis=1, keepdims=True) * (1.0 / HEAD_DIM)


def _attn_fwd(projq, qnw2, knw2, ebd, g):
    S = projq.shape[0]
    d = DILATIONS[g]
    T = ATT_TILE
    nt = S // T
    nb = T // (128 * d)
    sdt = BF16 if d == 1 else F32

    def body(cur_ref, qw_ref, kw_ref, ebd_ref, o_ref, lse_ref, qs_s, ks_s, vs_s):
        t = pl.program_id(1)
        e = ebd_ref[...]

        @pl.when(t == 0)
        def _():
            ks_s[pl.ds(0, T), :] = jnp.zeros((T, 128), sdt)
            vs_s[pl.ds(0, T), :] = jnp.zeros((T, 128), sdt)

        @pl.when(t > 0)
        def _():
            ks_s[pl.ds(0, T), :] = ks_s[pl.ds(T, T), :]
            vs_s[pl.ds(0, T), :] = vs_s[pl.ds(T, T), :]

        qs_s[...] = (_head_norm(cur_ref[:, 0:128], qw_ref[...], e) * (QK_SCALE * LOG2E)).astype(sdt)
        ks_s[pl.ds(T, T), :] = _head_norm(cur_ref[:, 128:256], kw_ref[...], e).astype(sdt)
        vs_s[pl.ds(T, T), :] = cur_ref[:, 256:384].astype(sdt)
        m0, m1 = _lane_masks()
        first = t == 0
        for r in range(d):
            for bb in range(nb):
                q0 = r + bb * 128 * d
                k0 = T + r + (bb - 1) * 128 * d
                qs = qs_s[pl.ds(q0, 128, stride=d), :].astype(F32)
                kb = ks_s[pl.ds(k0, 256, stride=d), :].astype(BF16)
                vb = vs_s[pl.ds(k0, 256, stride=d), :].astype(BF16)
                ok = _window_mask(first if bb == 0 else None)
                ok2 = jnp.concatenate([ok, ok], axis=0)
                q2 = jnp.concatenate([qs * m0, qs * m1], axis=0).astype(BF16)
                s = lax.dot_general(q2, kb, (((1,), (1,)), ((), ())), preferred_element_type=F32)
                s = jnp.where(ok2, s, NEG_INF)
                mx = jnp.max(s, axis=-1, keepdims=True)
                p = jnp.exp2(s - mx)
                l = jnp.sum(p, axis=-1, keepdims=True)
                o = jnp.dot(p.astype(BF16), vb, preferred_element_type=F32) / l
                lse = mx * LN2 + jnp.log(l)
                o_ref[pl.ds(q0, 128, stride=d), :] = o[0:128] * m0 + o[128:256] * m1
                lse_ref[pl.ds(q0, 128, stride=d), :] = lse[0:128] * m0 + lse[128:256] * m1

    return pl.pallas_call(
        body, name=f"attn_fwd_g{g}", grid=(4, nt),
        in_specs=[pl.BlockSpec((T, 384), lambda p, t: (t, g * 4 + p)),
                  pl.BlockSpec((1, 128), lambda p, t: (0, 0)),
                  pl.BlockSpec((1, 128), lambda p, t: (0, 0)),
                  pl.BlockSpec((128, 128), lambda p, t: (0, 0))],
        out_specs=[pl.BlockSpec((T, 128), lambda p, t: (t, p)), pl.BlockSpec((T, 128), lambda p, t: (t, p))],
        out_shape=[jax.ShapeDtypeStruct((S, ATTN_OUT), F32), jax.ShapeDtypeStruct((S, ATTN_OUT), F32)],
        scratch_shapes=[pltpu.VMEM((T, 128), sdt), pltpu.VMEM((2 * T, 128), sdt), pltpu.VMEM((2 * T, 128), sdt)],
        compiler_params=_params(),
    )(projq, qnw2, knw2, ebd)


def _attn_bwd(projq, dprojq, d_o, lse, delta, qnw2, knw2, ebd, g):
    S = projq.shape[0]
    d = DILATIONS[g]
    T = ATT_TILE
    nt = S // T
    nb = T // (128 * d)
    sdt = F32

    def norm_bwd(raw_b, dy, w, e):
        x = raw_b.astype(F32)
        ss = jnp.dot((x * x).astype(BF16), e, preferred_element_type=F32)
        rstd = lax.rsqrt(ss * (1.0 / HEAD_DIM) + EPS)
        xh = x * rstd
        gw = jnp.sum(dy * xh, axis=0, keepdims=True)
        dxh = dy * w
        mean = jnp.dot((dxh * xh).astype(BF16), e, preferred_element_type=F32) * (1.0 / HEAD_DIM)
        return rstd * (dxh - xh * mean), gw

    nt_dims = (((1,), (1,)), ((), ()))
    tn_dims = (((0,), (0,)), ((), ()))

    def unit_stacked(qs, dos, ls, dls, kb, vb, ok, m0, m1):
        ok2 = jnp.concatenate([ok, ok], axis=0)
        q2 = jnp.concatenate([qs * m0, qs * m1], axis=0).astype(BF16)
        do2 = jnp.concatenate([dos * m0, dos * m1], axis=0).astype(BF16)
        lcol = jnp.concatenate([_head_col(ls, m0), _head_col(ls, m1)], axis=0)
        dcol = jnp.concatenate([_head_col(dls, m0), _head_col(dls, m1)], axis=0)
        s = jnp.where(ok2, lax.dot_general(q2, kb, nt_dims, preferred_element_type=F32), NEG_INF)
        p = jnp.exp2(s - lcol * LOG2E)
        dp = lax.dot_general(do2, vb, nt_dims, preferred_element_type=F32)
        ds = (p * (dp - dcol)).astype(BF16)
        dq2 = jnp.dot(ds, kb, preferred_element_type=F32)
        dk = lax.dot_general(ds, q2, tn_dims, preferred_element_type=F32)
        dv = lax.dot_general(p.astype(BF16), do2, tn_dims, preferred_element_type=F32)
        return dq2[0:128] * m0 + dq2[128:256] * m1, dk, dv

    def unit_per_head(qs, dos, ls, dls, kb, vb, ok, m0, m1):
        dq_t = dk_t = dv_t = None
        for mh in (m0, m1):
            qh = (qs * mh).astype(BF16)
            doh = (dos * mh).astype(BF16)
            s = jnp.where(ok, lax.dot_general(qh, kb, nt_dims, preferred_element_type=F32), NEG_INF)
            p = jnp.exp2(s - _head_col(ls, mh) * LOG2E)
            dp = lax.dot_general(doh, vb, nt_dims, preferred_element_type=F32)
            ds = (p * (dp - _head_col(dls, mh))).astype(BF16)
            dq = jnp.dot(ds, kb, preferred_element_type=F32) * mh
            dk = lax.dot_general(ds, qh, tn_dims, preferred_element_type=F32)
            dv = lax.dot_general(p.astype(BF16), doh, tn_dims, preferred_element_type=F32)
            dq_t = dq if dq_t is None else dq_t + dq
            dk_t = dk if dk_t is None else dk_t + dk
            dv_t = dv if dv_t is None else dv_t + dv
        return dq_t, dk_t, dv_t

    unit = unit_per_head if d == 1 else unit_stacked

    def body(cur_ref, prev_ref, do_ref, lse_ref, dl_ref, qw_ref, kw_ref, ebd_ref, dp_in_ref,
             out_ref, gqk_ref, qs_s, ks_s, vs_s, do_s, dq_cur, dq_prev, dk_acc, dv_acc):
        i = pl.program_id(1)
        e = ebd_ref[...]
        m0, m1 = _lane_masks()

        @pl.when(i == 0)
        def _():
            dk_acc[...] = jnp.zeros_like(dk_acc)
            dv_acc[...] = jnp.zeros_like(dv_acc)
            dq_prev[...] = jnp.zeros_like(dq_prev)
            gqk_ref[...] = jnp.zeros_like(gqk_ref)

        @pl.when(i < nt)
        def _():
            qs_s[...] = _head_norm(cur_ref[:, 0:128], qw_ref[...], e) * (QK_SCALE * LOG2E)
            ks_s[pl.ds(0, T), :] = _head_norm(prev_ref[:, 128:256], kw_ref[...], e)
            ks_s[pl.ds(T, T), :] = _head_norm(cur_ref[:, 128:256], kw_ref[...], e)
            vs_s[pl.ds(0, T), :] = prev_ref[:, 256:384].astype(F32)
            vs_s[pl.ds(T, T), :] = cur_ref[:, 256:384].astype(F32)
            do_s[...] = do_ref[...].astype(F32)
            first = i == 0
            for r in range(d):
                own_k = own_v = None
                for bb in range(nb):
                    q0 = r + bb * 128 * d
                    k0 = T + r + (bb - 1) * 128 * d
                    qs = qs_s[pl.ds(q0, 128, stride=d), :]
                    dos = do_s[pl.ds(q0, 128, stride=d), :]
                    ls = lse_ref[pl.ds(q0, 128, stride=d), :]
                    dls = dl_ref[pl.ds(q0, 128, stride=d), :]
                    kb = ks_s[pl.ds(k0, 256, stride=d), :].astype(BF16)
                    vb = vs_s[pl.ds(k0, 256, stride=d), :].astype(BF16)
                    ok = _window_mask(first if bb == 0 else None)
                    dq_t, dk_t, dv_t = unit(qs, dos, ls, dls, kb, vb, ok, m0, m1)
                    dq_cur[pl.ds(q0, 128, stride=d), :] = dq_t
                    before = pl.ds(k0, 128, stride=d)
                    if bb == 0:
                        dk_acc[before, :] = dk_acc[before, :] + dk_t[0:128]
                        dv_acc[before, :] = dv_acc[before, :] + dv_t[0:128]
                    else:
                        dk_acc[before, :] = own_k + dk_t[0:128]
                        dv_acc[before, :] = own_v + dv_t[0:128]
                    own_k, own_v = dk_t[128:256], dv_t[128:256]
                tail = pl.ds(T + r + (nb - 1) * 128 * d, 128, stride=d)
                dk_acc[tail, :] = own_k
                dv_acc[tail, :] = own_v

        @pl.when(i >= 1)
        def _():
            dq_raw, gq = norm_bwd(prev_ref[:, 0:128], dq_prev[...] * QK_SCALE, qw_ref[...], e)
            dk_raw, gk = norm_bwd(prev_ref[:, 128:256], dk_acc[pl.ds(0, T), :] * LN2, kw_ref[...], e)
            out_ref[:, 0:128] = dq_raw.astype(BF16)
            out_ref[:, 128:256] = dk_raw.astype(BF16)
            out_ref[:, 256:384] = dv_acc[pl.ds(0, T), :].astype(BF16)
            gqk_ref[0:1, :] += gq
            gqk_ref[1:2, :] += gk

        dq_prev[...] = dq_cur[...]
        dk_acc[pl.ds(0, T), :] = dk_acc[pl.ds(T, T), :]
        dv_acc[pl.ds(0, T), :] = dv_acc[pl.ds(T, T), :]

    last = nt - 1
    qtile = lambda p, i: (jnp.minimum(i, last), p)
    return pl.pallas_call(
        body, name=f"attn_bwd_g{g}", grid=(4, nt + 1),
        in_specs=[pl.BlockSpec((T, 384), lambda p, i: (jnp.minimum(i, last), g * 4 + p)),
                  pl.BlockSpec((T, 384), lambda p, i: (jnp.maximum(i - 1, 0), g * 4 + p)),
                  pl.BlockSpec((T, 128), qtile), pl.BlockSpec((T, 128), qtile), pl.BlockSpec((T, 128), qtile),
                  pl.BlockSpec((1, 128), lambda p, i: (0, 0)),
                  pl.BlockSpec((1, 128), lambda p, i: (0, 0)),
                  pl.BlockSpec((128, 128), lambda p, i: (0, 0)),
                  ANY],
        out_specs=[pl.BlockSpec((T, 384), lambda p, i: (jnp.maximum(i - 1, 0), g * 4 + p)),
                   pl.BlockSpec((None, 8, 128), lambda p, i: (p, 0, 0))],
        out_shape=[jax.ShapeDtypeStruct(dprojq.shape, BF16), jax.ShapeDtypeStruct((4, 8, 128), F32)],
        scratch_shapes=[pltpu.VMEM((T, 128), sdt), pltpu.VMEM((2 * T, 128), sdt), pltpu.VMEM((2 * T, 128), sdt),
                        pltpu.VMEM((T, 128), sdt), pltpu.VMEM((T, 128), F32), pltpu.VMEM((T, 128), F32),
                        pltpu.VMEM((2 * T, 128), F32), pltpu.VMEM((2 * T, 128), F32)],
        input_output_aliases={8: 0},
        compiler_params=_params(),
    )(projq, projq, d_o, lse, delta, qnw2, knw2, ebd, dprojq)


def _mid(projn, o_g, lse_g, x, tgt, mod, convw8, wc, wa, wo, ebd, *, tm=256):
    S, D = x.shape
    nt = S // tm
    nst = max(1, 256 // tm)
    assert nt % nst == 0
    hb = tm // 16

    def body(pn_ref, hc_ref, hx_ref, o0_ref, o1_ref, o2_ref, l0_ref, l1_ref, l2_ref, x_ref, t_ref, mod_ref,
             cw_ref, ebd_ref, wc_hbm, wa_hbm, wo_hbm,
             dpn_ref, do_ref, lse_ref, dl_ref, dout_ref, vacc_ref, gwo_hbm, gwc_hbm, gwa_hbm,
             wc_s, wa_s, wo_s, gwo_s, gwc_s, gwa_s, carry_s, *stashes):
        i = pl.program_id(0)
        ti = nt - 1 - i
        par = i % nst
        stash = pl.ds(pl.multiple_of(par * tm, tm), tm)

        @pl.when(i == 0)
        def _():
            for src, dst in ((wc_hbm, wc_s), (wa_hbm, wa_s), (wo_hbm, wo_s)):
                pltpu.sync_copy(src, dst)
            gwo_s[...] = jnp.zeros_like(gwo_s)
            gwc_s[...] = jnp.zeros_like(gwc_s)
            gwa_s[...] = jnp.zeros_like(gwa_s)
            carry_s[...] = jnp.zeros_like(carry_s)
            vacc_ref[...] = jnp.zeros_like(vacc_ref)

        rows = lax.broadcasted_iota(jnp.int32, (tm, D), 0)
        w0, w1, w2 = cw_ref[0:1, :], cw_ref[1:2, :], cw_ref[2:3, :]
        gate = mod_ref[:, 2 * D:3 * D]

        b_a = pn_ref[:, BA:BA + D].astype(F32)
        c_a = pn_ref[:, CA:CA + D].astype(F32)
        x_a = pn_ref[:, XA:XA + D].astype(F32)
        z_a = pn_ref[:, ZA:ZA + D].astype(F32)
        u = c_a * x_a
        has_prev = (ti > 0).astype(F32)
        uh = hc_ref[...].astype(F32) * hx_ref[...].astype(F32) * has_prev
        h14, h15 = _row(uh, 14), _row(uh, 15)
        u_m1 = jnp.where(rows == 0, h15, pltpu.roll(u, 1, 0))
        u_m2 = jnp.where(rows == 0, h14, jnp.where(rows == 1, h15, pltpu.roll(u, 2, 0)))
        conv = w0 * u_m2 + w1 * u_m1 + w2 * u
        sg_a = _sigmoid(z_a)
        sz_a = z_a * sg_a
        y_a = b_a * conv * sz_a
        y_ab = y_a.astype(BF16)
        pa = jnp.dot(y_ab, wc_s[...], preferred_element_type=F32)

        l0, l1, l2 = l0_ref[...], l1_ref[...], l2_ref[...]
        mxl = jnp.maximum(jnp.maximum(l0, l1), l2)
        e0, e1, e2 = jnp.exp(l0 - mxl), jnp.exp(l1 - mxl), jnp.exp(l2 - mxl)
        den = e0 + e1 + e2
        attn = (e0 * o0_ref[...] + e1 * o1_ref[...] + e2 * o2_ref[...]) / den
        lse_ref[...] = mxl + jnp.log(den)
        z_b = pn_ref[:, ZB:ZB + ATTN_OUT].astype(F32)
        sg_b = _sigmoid(z_b)
        sz_b = z_b * sg_b
        y_b = attn * sz_b
        y_bb = y_b.astype(BF16)
        pb = jnp.dot(y_bb, wa_s[...], preferred_element_type=F32)

        s_a = _sigmoid(pn_ref[:, GA:GA + D].astype(F32))
        s_b = _sigmoid(pn_ref[:, GB:GB + D].astype(F32))
        merged = s_a * pa + s_b * pb
        merged_b = merged.astype(BF16)
        mo = jnp.dot(merged_b, wo_s[...], preferred_element_type=F32)
        diff = x_ref[...] + gate * mo - t_ref[...]
        dout = diff * (1.0 / D)
        dout_ref[...] = dout
        vacc_ref[4:5, :] += jnp.sum(diff * diff, axis=0, keepdims=True)
        vacc_ref[0:1, :] += jnp.sum(dout * mo, axis=0, keepdims=True)

        d_mo = (dout * gate).astype(BF16)
        nt_dims = (((1,), (1,)), ((), ()))
        dmerged = lax.dot_general(d_mo, wo_s[...], nt_dims, preferred_element_type=F32)
        dpa = (dmerged * s_a).astype(BF16)
        dpb = (dmerged * s_b).astype(BF16)
        dpn_ref[:, GA:GA + D] = (dmerged * pa * s_a * (1.0 - s_a)).astype(BF16)
        dpn_ref[:, GB:GB + D] = (dmerged * pb * s_b * (1.0 - s_b)).astype(BF16)
        tn = (((0,), (0,)), ((), ()))
        if nst == 1:
            gwo_s[...] += lax.dot_general(merged_b, d_mo, tn, preferred_element_type=F32)
            gwc_s[...] += lax.dot_general(y_ab, dpa, tn, preferred_element_type=F32)
            gwa_s[...] += lax.dot_general(y_bb, dpb, tn, preferred_element_type=F32)
        else:
            st_m, st_d, st_ya, st_pa, st_yb, st_pb = stashes
            st_m[stash, :] = merged_b
            st_d[stash, :] = d_mo
            st_ya[stash, :] = y_ab
            st_pa[stash, :] = dpa
            st_yb[stash, :] = y_bb
            st_pb[stash, :] = dpb

            @pl.when(par == nst - 1)
            def _():
                gwo_s[...] += lax.dot_general(st_m[...], st_d[...], tn, preferred_element_type=F32)
                gwc_s[...] += lax.dot_general(st_ya[...], st_pa[...], tn, preferred_element_type=F32)
                gwa_s[...] += lax.dot_general(st_yb[...], st_pb[...], tn, preferred_element_type=F32)

        dy_a = lax.dot_general(dpa, wc_s[...], nt_dims, preferred_element_type=F32)
        dy_b = lax.dot_general(dpb, wa_s[...], nt_dims, preferred_element_type=F32)

        dattn = dy_b * sz_b
        dpn_ref[:, ZB:ZB + ATTN_OUT] = (dy_b * attn * sg_b * (1.0 + z_b * (1.0 - sg_b))).astype(BF16)
        do_ref[...] = dattn.astype(BF16)
        dl_ref[...] = jnp.dot((dattn * attn).astype(BF16), ebd_ref[...], preferred_element_type=F32)

        dpn_ref[:, BA:BA + D] = (dy_a * conv * sz_a).astype(BF16)
        dpn_ref[:, ZA:ZA + D] = (dy_a * b_a * conv * sg_a * (1.0 + z_a * (1.0 - sg_a))).astype(BF16)
        dconv = dy_a * b_a * sz_a
        vacc_ref[1:2, :] += jnp.sum(dconv * u_m2, axis=0, keepdims=True)
        vacc_ref[2:3, :] += jnp.sum(dconv * u_m1, axis=0, keepdims=True)
        vacc_ref[3:4, :] += jnp.sum(dconv * u, axis=0, keepdims=True)
        nxt = carry_s[...]
        n0, n1 = _row(nxt, 0), _row(nxt, 1)
        dc_p1 = jnp.where(rows == tm - 1, n0, pltpu.roll(dconv, tm - 1, 0))
        dc_p2 = jnp.where(rows == tm - 1, n1, jnp.where(rows == tm - 2, n0, pltpu.roll(dconv, tm - 2, 0)))
        du = w2 * dconv + w1 * dc_p1 + w0 * dc_p2
        carry_s[...] = dconv[0:8, :]
        dpn_ref[:, CA:CA + D] = (du * x_a).astype(BF16)
        dpn_ref[:, XA:XA + D] = (du * c_a).astype(BF16)

        @pl.when(i == nt - 1)
        def _():
            pltpu.sync_copy(gwo_s, gwo_hbm)
            pltpu.sync_copy(gwc_s, gwc_hbm)
            pltpu.sync_copy(gwa_s, gwa_hbm)

    rev = lambda i: (nt - 1 - i, 0)
    halo = lambda col: pl.BlockSpec((16, D), lambda i: (jnp.maximum((nt - 1 - i) * hb - 1, 0), col))
    tile512 = pl.BlockSpec((tm, ATTN_OUT), rev)
    tileD = pl.BlockSpec((tm, D), rev)
    const = lambda shape: pl.BlockSpec(shape, lambda i: (0, 0))
    return pl.pallas_call(
        body, name="mid_fwd_bwd", grid=(nt,),
        in_specs=[pl.BlockSpec((tm, NAT), rev), halo(CA // D), halo(XA // D)] + [tile512] * 6
                 + [tileD, tileD, const((1, 3 * D)), const((8, D)), const((ATTN_OUT, ATTN_OUT))] + [ANY] * 3,
        out_specs=[pl.BlockSpec((tm, NAT), rev), tile512, tile512, tile512, tileD, const((8, D)), ANY, ANY, ANY],
        out_shape=[jax.ShapeDtypeStruct((S, NAT), BF16), jax.ShapeDtypeStruct((S, ATTN_OUT), BF16),
                   jax.ShapeDtypeStruct((S, ATTN_OUT), F32), jax.ShapeDtypeStruct((S, ATTN_OUT), F32),
                   jax.ShapeDtypeStruct((S, D), F32), jax.ShapeDtypeStruct((8, D), F32),
                   jax.ShapeDtypeStruct((D, D), F32), jax.ShapeDtypeStruct((D, D), F32),
                   jax.ShapeDtypeStruct((ATTN_OUT, D), F32)],
        scratch_shapes=[pltpu.VMEM((D, D), BF16), pltpu.VMEM((ATTN_OUT, D), BF16), pltpu.VMEM((D, D), BF16),
                        pltpu.VMEM((D, D), F32), pltpu.VMEM((D, D), F32), pltpu.VMEM((ATTN_OUT, D), F32),
                        pltpu.VMEM((8, D), F32)]
                       + ([pltpu.VMEM((nst * tm, D), BF16)] * 4
                          + [pltpu.VMEM((nst * tm, ATTN_OUT), BF16), pltpu.VMEM((nst * tm, D), BF16)] if nst > 1 else []),
        compiler_params=_params(vmem=60 * 1024 * 1024),
    )(projn, projn, projn, *o_g, *lse_g, x, tgt, mod, convw8, ebd, wc, wa, wo)


def _dh_matmul(dpn, dpq, w_nat, w_qkv, parts, *, tm=512):
    S = dpn.shape[0]
    D = D_MODEL
    n = len(parts)
    nt = S // tm

    def body(dn_ref, dq_ref, wn_hbm, wq_hbm, *rest):
        p_hbm, rest = rest[:n], rest[n:]
        dh_ref = rest[0]
        q_hbm = rest[1:1 + n]
        wn_s, wq_s = rest[1 + n:3 + n]
        sems = rest[3 + n:]

        @pl.when(pl.program_id(0) == 0)
        def _():
            if n:
                mine, _ = _scatter_copies(p_hbm, q_hbm, *sems)
                for cp in mine:
                    cp.start()
            pltpu.sync_copy(wn_hbm, wn_s)
            pltpu.sync_copy(wq_hbm, wq_s)

        if n:
            @pl.when(pl.program_id(0) == nt - 1)
            def _():
                mine, theirs = _scatter_copies(p_hbm, q_hbm, *sems)
                for cp in theirs:
                    cp.wait_recv()
                for cp in mine:
                    cp.wait_send()

        nt_dims = (((1,), (1,)), ((), ()))
        dh = (lax.dot_general(dn_ref[...], wn_s[...], nt_dims, preferred_element_type=F32)
              + lax.dot_general(dq_ref[...], wq_s[...], nt_dims, preferred_element_type=F32))
        dh_ref[...] = dh.astype(BF16)

    row = lambda w: pl.BlockSpec((tm, w), lambda i: (i, 0))
    outs = pl.pallas_call(
        body, name="dh_matmul", grid=(nt,),
        in_specs=[row(NAT), row(NQKV), ANY, ANY] + [ANY] * n,
        out_specs=[row(D)] + [ANY] * n,
        out_shape=[jax.ShapeDtypeStruct((S, D), BF16)] + [jax.ShapeDtypeStruct(p.shape, p.dtype) for p in parts],
        scratch_shapes=[pltpu.VMEM((D, NAT), BF16), pltpu.VMEM((D, NQKV), BF16)]
                       + ([pltpu.SemaphoreType.DMA((n, 3)), pltpu.SemaphoreType.DMA((n, 3))] if n else []),
        compiler_params=_params(vmem=60 * 1024 * 1024),
    )(dpn, dpq, w_nat, w_qkv, *parts)
    return outs[0], outs[1:]


def _norm_bwd(dh, x, dout, mod, norm_w, *, tm=512):
    S, D = x.shape

    def body(dh_ref, x_ref, dout_ref, mod_ref, nw_ref, gx_ref, st_ref):
        @pl.when(pl.program_id(0) == 0)
        def _():
            st_ref[...] = jnp.zeros_like(st_ref)

        dh = dh_ref[...].astype(F32)
        scale1 = 1.0 + mod_ref[:, D:2 * D]
        nw = nw_ref[...]
        xv = x_ref[...]
        rstd = lax.rsqrt(jnp.mean(xv * xv, axis=-1, keepdims=True) + EPS)
        xh = xv * rstd
        dhs = dh * scale1
        dxh = dhs * nw
        dx = rstd * (dxh - xh * jnp.mean(dxh * xh, axis=-1, keepdims=True))
        gx_ref[...] = dout_ref[...] + dx
        st_ref[0:1, :] += jnp.sum(dh, axis=0, keepdims=True)
        st_ref[1:2, :] += jnp.sum(dh * (xh * nw), axis=0, keepdims=True)
        st_ref[2:3, :] += jnp.sum(dhs * xh, axis=0, keepdims=True)

    row = pl.BlockSpec((tm, D), lambda i: (i, 0))
    return pl.pallas_call(
        body, name="norm_bwd", grid=(S // tm,),
        in_specs=[row, row, row, pl.BlockSpec((1, 3 * D), lambda i: (0, 0)), pl.BlockSpec((1, D), lambda i: (0, 0))],
        out_specs=[row, pl.BlockSpec((8, D), lambda i: (0, 0))],
        out_shape=[jax.ShapeDtypeStruct((S, D), F32), jax.ShapeDtypeStruct((8, D), F32)],
        compiler_params=_params(),
    )(dh, x, dout, mod, norm_w)


def _pack_smalls(st, vacc, gqk):
    D = D_MODEL

    def body(st_ref, va_ref, gqk_ref, o_ref):
        o_ref[...] = jnp.zeros_like(o_ref)
        o_ref[0:2, :] = st_ref[0:2, :]
        o_ref[2:3, :] = va_ref[0:1, :]
        o_ref[8:9, :] = st_ref[2:3, :]
        o_ref[16:19, :] = va_ref[1:4, :]
        tot = gqk_ref[0]
        for k in range(1, gqk_ref.shape[0]):
            tot = tot + gqk_ref[k]
        tot = tot + pltpu.roll(tot, HEAD_DIM, 1)
        o_ref[24:32, 0:128] = tot
        loss = jnp.sum(va_ref[4:5, :], axis=1, keepdims=True) * (0.5 / D)
        o_ref[26:27, :] = jnp.broadcast_to(loss, (1, D))

    return pl.pallas_call(
        body, name="pack_smalls", out_shape=jax.ShapeDtypeStruct((32, D), F32),
        in_specs=[pl.BlockSpec(memory_space=pltpu.VMEM)] * 3,
        out_specs=pl.BlockSpec(memory_space=pltpu.VMEM), compiler_params=_params(),
    )(st, vacc, gqk)


def _small_update(sm_loc, dm_pad, ct_pad, ada, vecs):
    D = D_MODEL
    nv = len(vecs)
    places = [(0, 3, D), (8, 1, D), (16, 3, 256), (24, 1, HEAD_DIM), (25, 1, HEAD_DIM)]

    def body(*refs):
        sm_ref, dm_ref, ct_ref = refs[0:3]
        wmv = [refs[3 + 3 * t:6 + 3 * t] for t in range(nv + 1)]
        outs = refs[3 + 3 * (nv + 1):]
        res = [outs[4 * t:4 * t + 4] for t in range(nv + 1)]
        loss_ref, tot_s = outs[4 * (nv + 1)], outs[4 * (nv + 1) + 1]

        def update(g, w, m, v, out, at=(slice(None), slice(None))):
            d, mn, vn = _adamw_math(w[at], g, m[at], v[at])
            for ref, val in zip(out, (g, d, mn, vn)):
                ref[at] = val

        gw = jnp.dot(ct_ref[...], dm_ref[...], preferred_element_type=F32, precision=lax.Precision.HIGHEST)
        update(gw, *wmv[0], res[0])
        tot = sm_ref[0]
        for k in range(1, 8):
            tot = tot + sm_ref[k]
        tot_s[...] = tot
        for k in range(3):
            update(tot_s[k:k + 1, :], *wmv[1], res[1], at=(slice(None), slice(k * D, (k + 1) * D)))
        for t in range(1, nv):
            r0, nr, nl = places[t]
            update(tot_s[r0:r0 + nr, 0:nl], *wmv[1 + t], res[1 + t])
        loss_ref[...] = jnp.broadcast_to(tot_s[26:27, 0:128], (8, 128))

    vm = pl.BlockSpec(memory_space=pltpu.VMEM)
    groups = [ada] + list(vecs)
    out_shape = []
    for w, _, _ in groups:
        out_shape += [jax.ShapeDtypeStruct(w.shape, F32)] * 4
    out_shape.append(jax.ShapeDtypeStruct((8, 128), F32))
    flat = [a for grp in groups for a in grp]
    outs = pl.pallas_call(
        body, name="small_update", out_shape=out_shape,
        in_specs=[vm] * (3 + len(flat)), out_specs=[vm] * len(out_shape),
        scratch_shapes=[pltpu.VMEM((32, D), F32)], compiler_params=_params(),
    )(sm_loc, dm_pad, ct_pad, *flat)
    return [outs[4 * t:4 * t + 4] for t in range(nv + 1)], outs[-1]


def _later_weights(gathered, shards, chip):
    D = D_MODEL
    g_bc, g_ba, g_wo = [lax.dynamic_update_slice(g, s[None], (chip, 0, 0, 0)) for g, s in zip(gathered, shards)]
    wa = g_ba.reshape(4, ATTN_OUT, 256).transpose(1, 0, 2).reshape(ATTN_OUT, D)
    return g_bc.reshape(D, D), wa, g_wo.reshape(D, D)


def _local_step(xs, tg, mod, norm_w, w_nat, w_qkv, later, convw8, q_norm_w, k_norm_w):
    S = xs.shape[0]
    lane = jnp.arange(128)
    ebd128 = (lane[:, None] // HEAD_DIM == lane[None, :] // HEAD_DIM).astype(BF16)
    lane5 = jnp.arange(ATTN_OUT)
    ebd512 = (lane5[:, None] // HEAD_DIM == lane5[None, :] // HEAD_DIM).astype(BF16)
    qnw2 = jnp.tile(q_norm_w, (1, 2))
    knw2 = jnp.tile(k_norm_w, (1, 2))

    projn, h, ht = _inproj_nat(xs, mod, norm_w, w_nat)
    if len(later) == 2:
        projq, gathered = _inproj_qkv(h, w_qkv, later[0])
        wc, wa, wo = _later_weights(gathered, *later)
    else:
        projq, _ = _inproj_qkv(h, w_qkv, [])
        wc, wa, wo = later
    o_g, lse_g = [], []
    for g in range(N_GROUPS):
        o, l = _attn_fwd(projq, qnw2, knw2, ebd128, g)
        o_g.append(o)
        lse_g.append(l)

    dpn, d_o, lse, delta, dout, vacc, gwo, gwc, gwa = _mid(
        projn, o_g, lse_g, xs, tg, mod, convw8, wc, wa, wo, ebd512)
    dpq = lax.empty((S, NQKV), BF16)
    gqk = []
    for g in range(N_GROUPS):
        dpq, part = _attn_bwd(projq, dpq, d_o, lse, delta, qnw2, knw2, ebd128, g)
        gqk.append(part)
    gw_nat = _matmul_acc(ht, dpn, tn=1664, tk=2048, name="grad_w_in_nat")
    gw_qkv = _matmul_acc(ht, dpq, tn=1536, tk=2048, name="grad_w_in_qkv")
    return dpn, dpq, dout, gw_nat, gw_qkv, gwc, gwa, gwo, vacc, gqk


def kernel(x, c, w_ada, b_ada, norm_w, w_in, conv_w, q_norm_w, k_norm_w, w_br_conv, w_br_attn, w_out, loss_target, m_w_ada, m_b_ada, m_norm_w, m_w_in, m_conv_w, m_q_norm_w, m_k_norm_w, m_w_br_conv, m_w_br_attn, m_w_out, v_w_ada, v_b_ada, v_norm_w, v_w_in, v_conv_w, v_q_norm_w, v_k_norm_w, v_w_br_conv, v_w_br_attn, v_w_out):
    D = D_MODEL
    S = x.shape[1]
    xs, tg = x[0], loss_target[0]
    mx, my, mc = lax.axis_index("x"), lax.axis_index("y"), lax.axis_index("c")
    chip = 2 * mx + my
    dev = 2 * chip + mc

    c_blk = jnp.concatenate([c, jnp.pad(conv_w[0], ((0, 0), (0, D - 256))), jnp.zeros((4, D), F32)], axis=0)
    b_shard = lax.dynamic_slice(b_ada, (0, chip * 768), (1, 768))
    shards = [w_in[0].astype(BF16).reshape(2, 512, 2816), w_br_conv[0].astype(BF16).reshape(2, 128, D),
              w_br_attn[0].astype(BF16).reshape(2, 256, 256), w_out[0].astype(BF16).reshape(2, 128, D)]
    gathered_in, _, cg, mod_parts, silu_c = _weights_allgather(
        shards[0].reshape(2, 2, 256, SHARD_COLS), [], c_blk, w_ada[0], b_shard)
    convw8 = jnp.pad(cg[::2, 1:4, 0:256].transpose(1, 0, 2).reshape(3, D), ((0, 5), (0, 0)))
    mod_all = mod_parts.transpose(1, 0, 2).reshape(8, 3 * D)
    mod = lax.dynamic_slice(mod_all, (dev, 0), (1, 3 * D))
    chipv = jnp.reshape(chip, (1,)).astype(jnp.int32)
    g_in = gathered_in.reshape(4, D, SHARD_COLS)
    own_in = shards[0].reshape(D, SHARD_COLS)
    w_nat = _weights_prep(g_in, own_in, chipv, "nat")
    w_qkv = _weights_prep(g_in, own_in, chipv, "qkv")

    dpn, dpq, dout, gw_nat, gw_qkv, gwc, gwa, gwo, vacc, gqk = _local_step(
        xs, tg, mod, norm_w, w_nat, w_qkv, (shards[1:], chip), convw8, q_norm_w, k_norm_w)

    def halves(a):
        k, r, cc = a.shape
        return a.reshape(k, 2, r // 2, cc).transpose(1, 0, 2, 3)

    grads = [gw_nat.reshape(2, D // 2, NAT), gw_qkv.reshape(2, D // 2, NQKV),
             halves(gwc.astype(BF16).reshape(4, 256, D)),
             halves(gwa.astype(BF16).reshape(ATTN_OUT, 4, 256).transpose(1, 0, 2)),
             halves(gwo.astype(BF16).reshape(4, 256, D))]
    recv = _pair_exchange(grads)
    core = jnp.reshape(mc, (1,)).astype(jnp.int32)
    p_in = _pair_add_to_shards(core, grads[0], recv[0], lax.empty((4, D // 2, SHARD_COLS), BF16), "nat")
    p_in = _pair_add_to_shards(core, grads[1], recv[1], p_in, "qkv")
    parts = [p_in] + [_pair_add(core, gr, rc) for gr, rc in zip(grads[2:], recv[2:])]
    dh, gathered = _dh_matmul(dpn, dpq, w_nat, w_qkv, parts)
    grad_x, st = _norm_bwd(dh, xs, dout, mod, norm_w)
    mine = [_sum4(chipv, p, q) for p, q in zip(parts, gathered)]
    theirs, sm_all = _half_exchange(mine, _pack_smalls(st, vacc, jnp.concatenate(gqk, axis=0)))

    big = {}
    for t, (nm, w, m, v) in enumerate((("w_in", w_in, m_w_in, v_w_in), ("w_br_conv", w_br_conv, m_w_br_conv, v_w_br_conv),
                                       ("w_br_attn", w_br_attn, m_w_br_attn, v_w_br_attn), ("w_out", w_out, m_w_out, v_w_out))):
        big[nm] = _adamw_halves(core, w[0], mine[t], theirs[t], m[0], v[0], "adamw_" + nm)

    dmod_all = sm_all[:, 0:3, :].reshape(8, 3 * D)
    dm_pad = jnp.pad(lax.dynamic_slice(dmod_all, (0, chip * 768), (8, 768)), ((0, 120), (0, 0)))
    conv_dev = lax.dynamic_slice(sm_all[:, 16:24, :], (0, 0, chip * 256), (8, 8, 256))
    sm_loc = jnp.concatenate([sm_all[:, 0:16, :], jnp.pad(conv_dev, ((0, 0), (0, 0), (0, D - 256))), sm_all[:, 24:32, :]], axis=1)

    ct_pad = jnp.pad(silu_c.T, ((0, 0), (0, 120)))
    small_names = ["b_ada", "norm_w", "conv_w", "q_norm_w", "k_norm_w"]
    vecs = [(b_ada, m_b_ada, v_b_ada), (norm_w, m_norm_w, v_norm_w), (conv_w[0], m_conv_w[0], v_conv_w[0]),
            (q_norm_w, m_q_norm_w, v_q_norm_w), (k_norm_w, m_k_norm_w, v_k_norm_w)]
    updated, loss_blk = _small_update(sm_loc, dm_pad, ct_pad, (w_ada[0], m_w_ada[0], v_w_ada[0]), vecs)
    small = {"w_ada": [a[None] for a in updated[0]]}
    for nm, four in zip(small_names, updated[1:]):
        small[nm] = [a[None] for a in four] if nm == "conv_w" else list(four)

    order = ["w_ada", "b_ada", "norm_w", "w_in", "conv_w", "q_norm_w", "k_norm_w", "w_br_conv", "w_br_attn", "w_out"]
    res = [loss_blk[0, 0], grad_x[None]]
    for kind in range(4):
        for nm in order:
            res.append(big[nm][kind][None] if nm in big else small[nm][kind])
    return tuple(res)
```

```python
import jax
import jax.numpy as jnp
from jax import lax
from jax.experimental import pallas as pl
from jax.experimental.pallas import tpu as pltpu

F32 = jnp.float32
BF16 = jnp.bfloat16
MESH = pl.DeviceIdType.MESH
ANY = pl.BlockSpec(memory_space=pl.ANY)

D_MODEL = 1024
HEAD_DIM = 64
N_GROUPS = 3
DILATIONS = (1, 4, 16)
ATTN_OUT = 512
EPS = 1e-6
NEG_INF = -1e30
NAT = 6656
NQKV = 4608
BA, CA, XA, ZA, ZB, GA, GB = 0, 1024, 2048, 3072, 4096, 4608, 5632
ATT_TILE = 2048
QK_SCALE = HEAD_DIM ** -0.5
LOG2E = 1.4426950408889634
LN2 = 0.6931471805599453
V7X_VMEM_LIMIT = 56 * 1024 * 1024
V7X_VMEM_LIMIT_RESIDENT = 60 * 1024 * 1024

ADAM_LR = 0.001
ADAM_B1 = 0.9
ADAM_B2 = 0.999
ADAM_EPS = 1e-08
ADAM_WD = 0.01
ADAM_STEP = 10


def _params(vmem=V7X_VMEM_LIMIT, **kw):
    return pltpu.CompilerParams(vmem_limit_bytes=vmem, **kw)


def _sigmoid(z):
    return 1.0 / (1.0 + jnp.exp(-z))


def _row(v, r):
    rows = lax.broadcasted_iota(jnp.int32, v.shape, 0)
    return jnp.sum(jnp.where(rows == r, v, 0.0), axis=0, keepdims=True)


def _coords():
    return lax.axis_index("x"), lax.axis_index("y"), lax.axis_index("c")


def _flip(v, bit):
    return 1 - v if bit else v


def _weights_allgather(big_shard, shards, c_blk, w_ada, b_shard):
    n = len(shards)
    D = D_MODEL
    cols = w_ada.shape[1]

    def body(*refs):
        big_in, ins = refs[0], refs[1:1 + n]
        c_ref, wada_ref, b_ref = refs[1 + n:4 + n]
        big_out, outs = refs[4 + n], refs[5 + n:5 + 2 * n]
        cg_ref, mp_ref, sc_ref = refs[5 + 2 * n:8 + 2 * n]
        sem_refs = refs[8 + 2 * n:]
        ssem, rsem = sem_refs[:2] if n else (None, None)
        bs, br, cs, cr, ms, mr, lsem = sem_refs[2 if n else 0:]
        mx, my, mc = _coords()
        me = 2 * mx + my
        me8 = 2 * me + mc
        sib = (mx, my, 1 - mc)
        chips = [(_flip(mx, j & 2), _flip(my, j & 1)) for j in range(1, 4)]
        sends = []

        local = pltpu.make_async_copy(c_ref, cg_ref.at[me8], lsem)
        local.start()
        for j in range(1, 8):
            peer = (_flip(mx, j & 4), _flip(my, j & 2), _flip(mc, j & 1))
            cp = pltpu.make_async_remote_copy(
                src_ref=c_ref, dst_ref=cg_ref.at[me8], send_sem=cs.at[j - 1], recv_sem=cr.at[j - 1],
                device_id=peer, device_id_type=MESH)
            cp.start()
            sends.append(cp)

        kx, ky, kd = 2 * (1 - mx) + my, 2 * mx + (1 - my), 2 * (1 - mx) + (1 - my)
        xn, yn = (1 - mx, my, mc), (mx, 1 - my, mc)

        def big(src_chip, q, sem, to, half=None):
            h = mc if half is None else half
            return pltpu.make_async_remote_copy(
                src_ref=big_out.at[src_chip, h, q], dst_ref=big_out.at[src_chip, h, q],
                send_sem=bs.at[sem], recv_sem=br.at[sem], device_id=to, device_id_type=MESH)

        def own(q, sem, to):
            return pltpu.make_async_remote_copy(
                src_ref=big_in.at[mc, q], dst_ref=big_out.at[me, mc, q],
                send_sem=bs.at[sem], recv_sem=br.at[sem], device_id=to, device_id_type=MESH)

        for cp in (own(0, 0, xn), own(1, 3, yn), own(1, 1, xn), own(0, 2, yn)):
            cp.start()
            sends.append(cp)
        for t in range(n):
            for j, (px, py) in enumerate(chips):
                cp = pltpu.make_async_remote_copy(
                    src_ref=ins[t].at[mc], dst_ref=outs[t].at[me, mc], send_sem=ssem.at[t, j],
                    recv_sem=rsem.at[t, j], device_id=(px, py, mc), device_id_type=MESH)
                cp.start()
                sends.append(cp)

        pieces = [(kx, 0), (kx, 1), (ky, 0), (ky, 1), (kd, 0), (kd, 1)]

        def landed(piece, sem, frm, pass_on=None):
            src_chip, q = pieces[piece]
            big(src_chip, q, sem, frm).wait_recv()
            nxt = ([] if pass_on is None else [big(src_chip, q, pass_on[0], pass_on[1])])
            nxt.append(big(src_chip, q, 6 + piece, sib))
            for cp in nxt:
                cp.start()
                sends.append(cp)

        for j in range(1, 8):
            px, py, pc = _flip(mx, j & 4), _flip(my, j & 2), _flip(mc, j & 1)
            pltpu.make_async_remote_copy(
                src_ref=c_ref, dst_ref=cg_ref.at[4 * px + 2 * py + pc], send_sem=cs.at[j - 1],
                recv_sem=cr.at[j - 1], device_id=(px, py, pc), device_id_type=MESH).wait_recv()
        local.wait()
        pick = (lax.broadcasted_iota(jnp.int32, (8, 64), 1) == 8 * lax.broadcasted_iota(jnp.int32, (8, 64), 0))
        cv = jnp.dot(pick.astype(F32), cg_ref[...].reshape(64, D), preferred_element_type=F32,
                     precision=lax.Precision.HIGHEST)
        sc = cv * _sigmoid(cv)
        sc_ref[...] = sc
        mp_ref[me] = jnp.dot(sc, wada_ref[...], preferred_element_type=F32,
                             precision=lax.Precision.HIGHEST) + b_ref[...]
        for j, (px, py) in enumerate(chips):
            cp = pltpu.make_async_remote_copy(
                src_ref=mp_ref.at[me], dst_ref=mp_ref.at[me], send_sem=ms.at[j], recv_sem=mr.at[j],
                device_id=(px, py, mc), device_id_type=MESH)
            cp.start()
            sends.append(cp)

        landed(0, 0, xn, pass_on=(4, yn))
        landed(3, 3, yn, pass_on=(5, xn))
        landed(1, 1, xn)
        landed(2, 2, yn)
        for j, (px, py) in enumerate(chips):
            pltpu.make_async_remote_copy(
                src_ref=mp_ref.at[me], dst_ref=mp_ref.at[2 * px + py], send_sem=ms.at[j], recv_sem=mr.at[j],
                device_id=(px, py, mc), device_id_type=MESH).wait_recv()
        landed(4, 4, yn)
        landed(5, 5, xn)
        for t in range(n):
            for j, (px, py) in enumerate(chips):
                src = 2 * px + py
                pltpu.make_async_remote_copy(
                    src_ref=ins[t].at[mc], dst_ref=outs[t].at[src, mc], send_sem=ssem.at[t, j],
                    recv_sem=rsem.at[t, j], device_id=(px, py, mc), device_id_type=MESH).wait_recv()
                fwd = pltpu.make_async_remote_copy(
                    src_ref=outs[t].at[src, mc], dst_ref=outs[t].at[src, mc], send_sem=ssem.at[t, 3 + j],
                    recv_sem=rsem.at[t, 3 + j], device_id=sib, device_id_type=MESH)
                fwd.start()
                sends.append(fwd)
        for t in range(n):
            for j, (px, py) in enumerate(chips):
                src = 2 * px + py
                pltpu.make_async_remote_copy(
                    src_ref=outs[t].at[src, 1 - mc], dst_ref=outs[t].at[src, 1 - mc], send_sem=ssem.at[t, 3 + j],
                    recv_sem=rsem.at[t, 3 + j], device_id=sib, device_id_type=MESH).wait_recv()
        for i, (src_chip, q) in enumerate(pieces):
            big(src_chip, q, 6 + i, sib, half=1 - mc).wait_recv()
        for cp in sends:
            cp.wait_send()

    vm = pl.BlockSpec(memory_space=pltpu.VMEM)
    outs = pl.pallas_call(
        body, name="weights_allgather",
        out_shape=[jax.ShapeDtypeStruct((4,) + big_shard.shape, big_shard.dtype)]
                  + [jax.ShapeDtypeStruct((4,) + s.shape, s.dtype) for s in shards]
                  + [jax.ShapeDtypeStruct((8,) + c_blk.shape, F32), jax.ShapeDtypeStruct((4, 8, cols), F32),
                     jax.ShapeDtypeStruct((8, D), F32)],
        in_specs=[ANY] * (n + 1) + [vm] * 3, out_specs=[ANY] * (n + 1) + [vm] * 3,
        scratch_shapes=([pltpu.SemaphoreType.DMA((n, 6)), pltpu.SemaphoreType.DMA((n, 6))] if n else [])
                       + [pltpu.SemaphoreType.DMA((12,)), pltpu.SemaphoreType.DMA((12,)),
                          pltpu.SemaphoreType.DMA((7,)), pltpu.SemaphoreType.DMA((7,)),
                          pltpu.SemaphoreType.DMA((3,)), pltpu.SemaphoreType.DMA((3,)), pltpu.SemaphoreType.DMA(())],
        compiler_params=_params(),
    )(big_shard, *shards, c_blk, w_ada, b_shard)
    return outs[0], outs[1:1 + n], outs[1 + n], outs[2 + n], outs[3 + n]


def _shard_gather_copies(ins, outs, ssem, rsem):
    mx, my, mc = _coords()
    me = 2 * mx + my
    sib = (mx, my, 1 - mc)
    send, recv, fwd, fwd_recv = [], [], [], []
    for t in range(len(ins)):
        for j in range(3):
            px, py = _flip(mx, (j + 1) & 2), _flip(my, (j + 1) & 1)
            src = 2 * px + py
            far = dict(send_sem=ssem.at[t, j], recv_sem=rsem.at[t, j], device_id=(px, py, mc), device_id_type=MESH)
            near = dict(send_sem=ssem.at[t, 3 + j], recv_sem=rsem.at[t, 3 + j], device_id=sib, device_id_type=MESH)
            send.append(pltpu.make_async_remote_copy(src_ref=ins[t].at[mc], dst_ref=outs[t].at[me, mc], **far))
            recv.append(pltpu.make_async_remote_copy(src_ref=ins[t].at[mc], dst_ref=outs[t].at[src, mc], **far))
            fwd.append(pltpu.make_async_remote_copy(src_ref=outs[t].at[src, mc], dst_ref=outs[t].at[src, mc], **near))
            fwd_recv.append(pltpu.make_async_remote_copy(
                src_ref=outs[t].at[src, 1 - mc], dst_ref=outs[t].at[src, 1 - mc], **near))
    return send, recv, fwd, fwd_recv


def _pair_exchange(grads):
    n = len(grads)

    def body(*refs):
        ins, outs = refs[:n], refs[n:2 * n]
        ssem, rsem = refs[2 * n:]
        mx, my, mc = _coords()
        sib = (mx, my, 1 - mc)
        sends = []
        for t in range(n):
            cp = pltpu.make_async_remote_copy(
                src_ref=ins[t].at[1 - mc], dst_ref=outs[t], send_sem=ssem.at[t], recv_sem=rsem.at[t],
                device_id=sib, device_id_type=MESH)
            cp.start()
            sends.append(cp)
        for cp in sends:
            cp.wait_recv()
        for cp in sends:
            cp.wait_send()

    return pl.pallas_call(
        body, name="grad_pair_exchange",
        out_shape=[jax.ShapeDtypeStruct(g.shape[1:], g.dtype) for g in grads],
        in_specs=[ANY] * n, out_specs=[ANY] * n,
        scratch_shapes=[pltpu.SemaphoreType.DMA((n,)), pltpu.SemaphoreType.DMA((n,))],
    )(*grads)


def _scatter_copies(ins, outs, ssem, rsem):
    mx, my, mc = _coords()
    me = 2 * mx + my
    mine, theirs = [], []
    for t in range(len(ins)):
        for j in range(1, 4):
            px, py = _flip(mx, j & 2), _flip(my, j & 1)
            mine.append(pltpu.make_async_remote_copy(
                src_ref=ins[t].at[2 * px + py], dst_ref=outs[t].at[me], send_sem=ssem.at[t, j - 1],
                recv_sem=rsem.at[t, j - 1], device_id=(px, py, mc), device_id_type=MESH))
            theirs.append(pltpu.make_async_remote_copy(
                src_ref=ins[t].at[me], dst_ref=outs[t].at[2 * px + py], send_sem=ssem.at[t, j - 1],
                recv_sem=rsem.at[t, j - 1], device_id=(px, py, mc), device_id_type=MESH))
    return mine, theirs


def _half_exchange(halves, smalls):
    n = len(halves)

    def body(*refs):
        ins, sm_ref = refs[:n], refs[n]
        outs, all_ref = refs[n + 1:2 * n + 1], refs[2 * n + 1]
        ssem, rsem, gsem, hsem, lsem = refs[2 * n + 2:]
        mx, my, mc = _coords()
        me = 4 * mx + 2 * my + mc
        sib = (mx, my, 1 - mc)
        local = pltpu.make_async_copy(sm_ref, all_ref.at[me], lsem)
        local.start()
        sends = []
        for t in range(n):
            rc = pltpu.make_async_remote_copy(
                src_ref=ins[t], dst_ref=outs[t], send_sem=ssem.at[t], recv_sem=rsem.at[t],
                device_id=sib, device_id_type=MESH)
            rc.start()
            sends.append(rc)
        gathers = []
        for j in range(1, 8):
            peer = (_flip(mx, j & 4), _flip(my, j & 2), _flip(mc, j & 1))
            cp = pltpu.make_async_remote_copy(
                src_ref=sm_ref, dst_ref=all_ref.at[me], send_sem=gsem.at[j - 1], recv_sem=hsem.at[j - 1],
                device_id=peer, device_id_type=MESH)
            cp.start()
            gathers.append(cp)
        for cp in sends:
            cp.wait_recv()
        for j in range(1, 8):
            px, py, pc = _flip(mx, j & 4), _flip(my, j & 2), _flip(mc, j & 1)
            pltpu.make_async_remote_copy(
                src_ref=sm_ref, dst_ref=all_ref.at[4 * px + 2 * py + pc], send_sem=gsem.at[j - 1],
                recv_sem=hsem.at[j - 1], device_id=(px, py, pc), device_id_type=MESH).wait_recv()
        for cp in sends + gathers:
            cp.wait_send()
        local.wait()

    vm = pl.BlockSpec(memory_space=pltpu.VMEM)
    outs = pl.pallas_call(
        body, name="grad_half_exchange",
        out_shape=[jax.ShapeDtypeStruct(h.shape, h.dtype) for h in halves]
                  + [jax.ShapeDtypeStruct((8,) + smalls.shape, smalls.dtype)],
        in_specs=[ANY] * n + [vm], out_specs=[ANY] * n + [vm],
        scratch_shapes=[pltpu.SemaphoreType.DMA((n,)), pltpu.SemaphoreType.DMA((n,)),
                        pltpu.SemaphoreType.DMA((7,)), pltpu.SemaphoreType.DMA((7,)), pltpu.SemaphoreType.DMA(())],
    )(*halves, smalls)
    return outs[:n], outs[n]


def _row_tile(rows, cols, bytes_per_row_elem=4, budget=2 * 1024 * 1024):
    t = rows
    while t % 2 == 0 and t > 16 and t * cols * bytes_per_row_elem > budget:
        t //= 2
    return t


def _pair_add(core, g, r):
    _, _, rh, cc = g.shape
    tr = _row_tile(rh, cc)

    def body(core_ref, g_ref, r_ref, o_ref):
        o_ref[...] = (g_ref[...].astype(F32) + r_ref[...].astype(F32)).astype(o_ref.dtype)

    return pl.pallas_call(
        body, name="grad_pair_add",
        grid_spec=pltpu.PrefetchScalarGridSpec(
            num_scalar_prefetch=1, grid=(4, rh // tr),
            in_specs=[pl.BlockSpec((None, None, tr, cc), lambda k, i, c: (c[0], k, i, 0)),
                      pl.BlockSpec((None, tr, cc), lambda k, i, c: (k, i, 0))],
            out_specs=pl.BlockSpec((None, tr, cc), lambda k, i, c: (k, i, 0))),
        out_shape=jax.ShapeDtypeStruct(r.shape, BF16),
        compiler_params=_params(),
    )(core, g, r)


def _sum4(chip, p, q):
    _, rh, cc = q.shape
    tr = _row_tile(rh, cc)

    def body(chip_ref, p_ref, q1_ref, q2_ref, q3_ref, o_ref):
        o_ref[...] = ((p_ref[...].astype(F32) + q1_ref[...].astype(F32)) + q2_ref[...].astype(F32)) + q3_ref[...].astype(F32)

    def other(j):
        return pl.BlockSpec((None, tr, cc), lambda i, c: (jnp.bitwise_xor(c[0], j), i, 0))

    return pl.pallas_call(
        body, name="grad_sum4",
        grid_spec=pltpu.PrefetchScalarGridSpec(
            num_scalar_prefetch=1, grid=(rh // tr,),
            in_specs=[pl.BlockSpec((None, tr, cc), lambda i, c: (c[0], i, 0)), other(1), other(2), other(3)],
            out_specs=pl.BlockSpec((tr, cc), lambda i, c: (i, 0))),
        out_shape=jax.ShapeDtypeStruct((rh, cc), F32),
        compiler_params=_params(),
    )(chip, p, q, q, q)


def _adamw_math(w, g, m, v):
    m = ADAM_B1 * m + (1.0 - ADAM_B1) * g
    v = ADAM_B2 * v + (1.0 - ADAM_B2) * (g * g)
    m_hat = m / (1.0 - ADAM_B1 ** ADAM_STEP)
    v_hat = v / (1.0 - ADAM_B2 ** ADAM_STEP)
    delta = -ADAM_LR * (m_hat / (jnp.sqrt(v_hat) + ADAM_EPS) + ADAM_WD * w)
    return delta, m, v


def _adamw_halves(core, w, mine, recv, m, v, name):
    rows, cols = w.shape
    rh = rows // 2
    tr = _row_tile(rh, cols, budget=1024 * 1024)
    nh = rh // tr

    def body(core_ref, w_ref, a_ref, b_ref, m_ref, v_ref, g_ref, d_ref, mo_ref, vo_ref):
        g = jnp.where(pl.program_id(0) == core_ref[0], a_ref[...], b_ref[...])
        d, mn, vn = _adamw_math(w_ref[...], g, m_ref[...], v_ref[...])
        g_ref[...] = g
        d_ref[...] = d
        mo_ref[...] = mn
        vo_ref[...] = vn

    full = pl.BlockSpec((tr, cols), lambda h, i, c: (h * nh + i, 0))
    half = pl.BlockSpec((tr, cols), lambda h, i, c: (i, 0))
    sd = jax.ShapeDtypeStruct((rows, cols), F32)
    return pl.pallas_call(
        body, name=name,
        grid_spec=pltpu.PrefetchScalarGridSpec(
            num_scalar_prefetch=1, grid=(2, nh),
            in_specs=[full, half, half, full, full], out_specs=[full] * 4),
        out_shape=[sd, sd, sd, sd], compiler_params=_params(),
    )(core, w, mine, recv, m, v)


def _inproj_nat(x, mod, norm_w, w, *, tm=1024, tn=1664):
    S, D = x.shape
    N = w.shape[1]
    nt, nj = S // tm, N // tn
    chunk = 256

    def body(x_ref, mod_ref, nw_ref, w_ref, proj_ref, h_ref, ht_ref, h_a, ht_a, h_b, ht_b):
        i, j = pl.program_id(0), pl.program_id(1)

        def make_h(h_s, ht_s):
            shift = mod_ref[:, 0:D]
            scale1 = 1.0 + mod_ref[:, D:2 * D]
            for r in range(tm // chunk):
                xv = x_ref[pl.ds(r * chunk, chunk), :]
                ms = jnp.mean(xv * xv, axis=-1, keepdims=True)
                h = (xv * lax.rsqrt(ms + EPS) * nw_ref[...]) * scale1 + shift
                h_s[pl.ds(r * chunk, chunk), :] = h.astype(BF16)
                ht_s[:, pl.ds(r * chunk, chunk)] = h.T.astype(BF16)

        @pl.when((i == 0) & (j == 0))
        def _():
            make_h(h_a, ht_a)

        def step(cur, nxt):
            @pl.when(j == 0)
            def _():
                h_ref[...] = cur[0][...]
                ht_ref[...] = cur[1][...]

            def project():
                proj_ref[...] = jnp.dot(cur[0][...], w_ref[...], preferred_element_type=F32).astype(BF16)

            ahead = (j == nj - 1) & (i < nt - 1)

            @pl.when(ahead)
            def _():
                project()
                make_h(*nxt)

            @pl.when(jnp.logical_not(ahead))
            def _():
                project()

        @pl.when(i % 2 == 0)
        def _():
            step((h_a, ht_a), (h_b, ht_b))

        @pl.when(i % 2 == 1)
        def _():
            step((h_b, ht_b), (h_a, ht_a))

    def x_tile(i, j):
        return (jnp.where((i == 0) & (j == 0), 0, jnp.minimum(i + 1, nt - 1)), 0)

    return pl.pallas_call(
        body, name="inproj_nat", grid=(nt, nj),
        in_specs=[pl.BlockSpec((tm, D), x_tile),
                  pl.BlockSpec((1, 3 * D), lambda i, j: (0, 0)),
                  pl.BlockSpec((1, D), lambda i, j: (0, 0)),
                  pl.BlockSpec((D, tn), lambda i, j: (0, j))],
        out_specs=[pl.BlockSpec((tm, tn), lambda i, j: (i, j)),
                   pl.BlockSpec((tm, D), lambda i, j: (i, 0)),
                   pl.BlockSpec((D, tm), lambda i, j: (0, i))],
        out_shape=[jax.ShapeDtypeStruct((S, N), BF16), jax.ShapeDtypeStruct((S, D), BF16),
                   jax.ShapeDtypeStruct((D, S), BF16)],
        scratch_shapes=[pltpu.VMEM((tm, D), BF16), pltpu.VMEM((D, tm), BF16),
                        pltpu.VMEM((tm, D), BF16), pltpu.VMEM((D, tm), BF16)],
        compiler_params=_params(),
    )(x, mod, norm_w, w)


def _inproj_qkv(a, b, shards, *, tm=1024, tn=1536):
    M, K = a.shape
    N = b.shape[1]
    n = len(shards)
    ni, nj = M // tm, N // tn

    def body(a_ref, b_ref, *rest):
        ins, o_ref, outs, sems = rest[:n], rest[n], rest[n + 1:2 * n + 1], rest[2 * n + 1:]
        i, j = pl.program_id(0), pl.program_id(1)
        if n:
            @pl.when((i == 0) & (j == 0))
            def _():
                for cp in _shard_gather_copies(ins, outs, *sems)[0]:
                    cp.start()

        o_ref[...] = jnp.dot(a_ref[...], b_ref[...], preferred_element_type=F32).astype(o_ref.dtype)

        if n:
            @pl.when((i == ni - 1) & (j == nj - 1))
            def _():
                send, recv, fwd, fwd_recv = _shard_gather_copies(ins, outs, *sems)
                for r, f in zip(recv, fwd):
                    r.wait_recv()
                    f.start()
                for r in fwd_recv:
                    r.wait_recv()
                for cp in send + fwd:
                    cp.wait_send()

    outs = pl.pallas_call(
        body, name="inproj_qkv", grid=(ni, nj),
        in_specs=[pl.BlockSpec((tm, K), lambda i, j: (i, 0)), pl.BlockSpec((K, tn), lambda i, j: (0, j))] + [ANY] * n,
        out_specs=[pl.BlockSpec((tm, tn), lambda i, j: (i, j))] + [ANY] * n,
        out_shape=[jax.ShapeDtypeStruct((M, N), BF16)] + [jax.ShapeDtypeStruct((4,) + s.shape, s.dtype) for s in shards],
        scratch_shapes=[pltpu.SemaphoreType.DMA((n, 6)), pltpu.SemaphoreType.DMA((n, 6))] if n else [],
        compiler_params=_params(),
    )(a, b, *shards)
    return outs[0], outs[1:]


def _matmul_acc(a, b, *, tn, tk, name):
    M, K = a.shape
    N = b.shape[1]
    nk = K // tk

    def body(a_ref, b_ref, o_ref, acc_ref):
        k = pl.program_id(1)

        @pl.when(k == 0)
        def _():
            acc_ref[...] = jnp.zeros_like(acc_ref)

        acc_ref[...] += jnp.dot(a_ref[...], b_ref[...], preferred_element_type=F32)

        @pl.when(k == nk - 1)
        def _():
            o_ref[...] = acc_ref[...].astype(o_ref.dtype)

    return pl.pallas_call(
        body, name=name, grid=(N // tn, nk),
        in_specs=[pl.BlockSpec((M, tk), lambda j, k: (0, k)), pl.BlockSpec((tk, tn), lambda j, k: (k, j))],
        out_specs=pl.BlockSpec((M, tn), lambda j, k: (0, j)),
        out_shape=jax.ShapeDtypeStruct((M, N), BF16),
        scratch_shapes=[pltpu.VMEM((M, tn), F32)], compiler_params=_params(),
    )(a, b)


SHARD_COLS = 2816


def _column_tables(part):
    if part == "nat":
        bw = 256
        orig = [j * bw if j < 16 else 8704 + (j - 16) * bw for j in range(NAT // bw)]
    else:
        bw = 128
        orig = []
        for jj in range(NQKV // bw):
            g, p, t = jj // 12, (jj % 12) // 3, jj % 3
            orig.append(4096 + t * 1536 + g * 512 + p * 128)
    tk = jnp.array([o // SHARD_COLS for o in orig], jnp.int32)
    tc = jnp.array([(o % SHARD_COLS) // bw for o in orig], jnp.int32)
    return tk, tc, bw


def _weights_prep(gath, own, chipv, part):
    D = D_MODEL
    tk, tc, bw = _column_tables(part)
    n = tk.shape[0]
    per_step = 2 if part == "nat" else 4
    assert n % per_step == 0

    def body(tk_ref, tc_ref, chip_ref, *refs):
        w_ref = refs[-1]
        for u in range(per_step):
            g_ref, o_ref = refs[2 * u], refs[2 * u + 1]
            mine = tk_ref[pl.program_id(0) * per_step + u] == chip_ref[0]
            w_ref[:, u * bw:(u + 1) * bw] = jnp.where(mine, o_ref[...], g_ref[...])

    def gath_spec(u):
        def idx(j, tk, tc, ch):
            k = tk[j * per_step + u]
            return (jnp.where(k == ch[0], (k + 1) % 4, k), 0, tc[j * per_step + u])
        return pl.BlockSpec((None, D, bw), idx)

    def own_spec(u):
        return pl.BlockSpec((D, bw), lambda j, tk, tc, ch: (0, tc[j * per_step + u]))

    specs = []
    for u in range(per_step):
        specs += [gath_spec(u), own_spec(u)]
    return pl.pallas_call(
        body, name="weights_prep_" + part,
        grid_spec=pltpu.PrefetchScalarGridSpec(
            num_scalar_prefetch=3, grid=(n // per_step,),
            in_specs=specs,
            out_specs=pl.BlockSpec((D, per_step * bw), lambda j, tk, tc, ch: (0, j))),
        out_shape=jax.ShapeDtypeStruct((D, n * bw), BF16),
        compiler_params=_params(),
    )(tk, tc, chipv, *([gath, own] * per_step))


def _pair_add_to_shards(core, gw, r, into, part):
    D = D_MODEL
    tk, tc, bw = _column_tables(part)
    n = tk.shape[0]

    def body(tk_ref, tc_ref, core_ref, g_ref, r_ref, into_ref, o_ref):
        o_ref[...] = (g_ref[...].astype(F32) + r_ref[...].astype(F32)).astype(BF16)

    return pl.pallas_call(
        body, name="grad_pair_add_" + part,
        grid_spec=pltpu.PrefetchScalarGridSpec(
            num_scalar_prefetch=3, grid=(n,),
            in_specs=[pl.BlockSpec((None, D // 2, bw), lambda j, tk, tc, c: (c[0], 0, j)),
                      pl.BlockSpec((D // 2, bw), lambda j, tk, tc, c: (0, j)), ANY],
            out_specs=pl.BlockSpec((None, D // 2, bw), lambda j, tk, tc, c: (tk[j], 0, tc[j]))),
        out_shape=jax.ShapeDtypeStruct(into.shape, BF16),
        input_output_aliases={5: 0},
        compiler_params=_params(),
    )(tk, tc, core, gw, r, into)


def _head_norm(xb, w, ebd):
    x = xb.astype(F32)
    ss = jnp.dot((x * x).astype(BF16), ebd, preferred_element_type=F32)
    return x * lax.rsqrt(ss * (1.0 / HEAD_DIM) + EPS) * w


def _window_mask(no_prev):
    qi = lax.broadcasted_iota(jnp.int32, (128, 256), 0)
    kc = lax.broadcasted_iota(jnp.int32, (128, 256), 1)
    ok = (kc >= qi) & (kc <= qi + 128)
    if no_prev is not None:
        ok = ok & ((kc >= 128) | jnp.logical_not(no_prev))
    return ok


def _lane_masks():
    lane = lax.broadcasted_iota(jnp.int32, (1, 128), 1)
    return [(lane < HEAD_DIM).astype(F32), (lane >= HEAD_DIM).astype(F32)]


def _head_col(v, mh):
    return jnp.sum(v * mh, axis=1, keepdims=True) * (1.0 / HEAD_DIM)


def _attn_fwd(projq, qnw2, knw2, ebd, g):
    S = projq.shape[0]
    d = DILATIONS[g]
    T = ATT_TILE
    nt = S // T
    nb = T // (128 * d)
    sdt = BF16 if d == 1 else F32

    def body(cur_ref, qw_ref, kw_ref, ebd_ref, o_ref, lse_ref, qs_s, ks_s, vs_s):
        t = pl.program_id(1)
        e = ebd_ref[...]

        @pl.when(t == 0)
        def _():
            ks_s[pl.ds(0, T), :] = jnp.zeros((T, 128), sdt)
            vs_s[pl.ds(0, T), :] = jnp.zeros((T, 128), sdt)

        @pl.when(t > 0)
        def _():
            ks_s[pl.ds(0, T), :] = ks_s[pl.ds(T, T), :]
            vs_s[pl.ds(0, T), :] = vs_s[pl.ds(T, T), :]

        qs_s[...] = (_head_norm(cur_ref[:, 0:128], qw_ref[...], e) * (QK_SCALE * LOG2E)).astype(sdt)
        ks_s[pl.ds(T, T), :] = _head_norm(cur_ref[:, 128:256], kw_ref[...], e).astype(sdt)
        vs_s[pl.ds(T, T), :] = cur_ref[:, 256:384].astype(sdt)
        m0, m1 = _lane_masks()
        first = t == 0
        for r in range(d):
            for bb in range(nb):
                q0 = r + bb * 128 * d
                k0 = T + r + (bb - 1) * 128 * d
                qs = qs_s[pl.ds(q0, 128, stride=d), :].astype(F32)
                kb = ks_s[pl.ds(k0, 256, stride=d), :].astype(BF16)
                vb = vs_s[pl.ds(k0, 256, stride=d), :].astype(BF16)
                ok = _window_mask(first if bb == 0 else None)
                ok2 = jnp.concatenate([ok, ok], axis=0)
                q2 = jnp.concatenate([qs * m0, qs * m1], axis=0).astype(BF16)
                s = lax.dot_general(q2, kb, (((1,), (1,)), ((), ())), preferred_element_type=F32)
                s = jnp.where(ok2, s, NEG_INF)
                mx = jnp.max(s, axis=-1, keepdims=True)
                p = jnp.exp2(s - mx)
                l = jnp.sum(p, axis=-1, keepdims=True)
                o = jnp.dot(p.astype(BF16), vb, preferred_element_type=F32) / l
                lse = mx * LN2 + jnp.log(l)
                o_ref[pl.ds(q0, 128, stride=d), :] = o[0:128] * m0 + o[128:256] * m1
                lse_ref[pl.ds(q0, 128, stride=d), :] = lse[0:128] * m0 + lse[128:256] * m1

    return pl.pallas_call(
        body, name=f"attn_fwd_g{g}", grid=(4, nt),
        in_specs=[pl.BlockSpec((T, 384), lambda p, t: (t, g * 4 + p)),
                  pl.BlockSpec((1, 128), lambda p, t: (0, 0)),
                  pl.BlockSpec((1, 128), lambda p, t: (0, 0)),
                  pl.BlockSpec((128, 128), lambda p, t: (0, 0))],
        out_specs=[pl.BlockSpec((T, 128), lambda p, t: (t, p)), pl.BlockSpec((T, 128), lambda p, t: (t, p))],
        out_shape=[jax.ShapeDtypeStruct((S, ATTN_OUT), F32), jax.ShapeDtypeStruct((S, ATTN_OUT), F32)],
        scratch_shapes=[pltpu.VMEM((T, 128), sdt), pltpu.VMEM((2 * T, 128), sdt), pltpu.VMEM((2 * T, 128), sdt)],
        compiler_params=_params(),
    )(projq, qnw2, knw2, ebd)


def _attn_bwd(projq, dprojq, d_o, lse, delta, qnw2, knw2, ebd, g):
    S = projq.shape[0]
    d = DILATIONS[g]
    T = ATT_TILE
    nt = S // T
    nb = T // (128 * d)
    sdt = F32

    def norm_bwd(raw_b, dy, w, e):
        x = raw_b.astype(F32)
        ss = jnp.dot((x * x).astype(BF16), e, preferred_element_type=F32)
        rstd = lax.rsqrt(ss * (1.0 / HEAD_DIM) + EPS)
        xh = x * rstd
        gw = jnp.sum(dy * xh, axis=0, keepdims=True)
        dxh = dy * w
        mean = jnp.dot((dxh * xh).astype(BF16), e, preferred_element_type=F32) * (1.0 / HEAD_DIM)
        return rstd * (dxh - xh * mean), gw

    nt_dims = (((1,), (1,)), ((), ()))
    tn_dims = (((0,), (0,)), ((), ()))

    def unit_stacked(qs, dos, ls, dls, kb, vb, ok, m0, m1):
        ok2 = jnp.concatenate([ok, ok], axis=0)
        q2 = jnp.concatenate([qs * m0, qs * m1], axis=0).astype(BF16)
        do2 = jnp.concatenate([dos * m0, dos * m1], axis=0).astype(BF16)
        lcol = jnp.concatenate([_head_col(ls, m0), _head_col(ls, m1)], axis=0)
        dcol = jnp.concatenate([_head_col(dls, m0), _head_col(dls, m1)], axis=0)
        s = jnp.where(ok2, lax.dot_general(q2, kb, nt_dims, preferred_element_type=F32), NEG_INF)
        p = jnp.exp2(s - lcol * LOG2E)
        dp = lax.dot_general(do2, vb, nt_dims, preferred_element_type=F32)
        ds = (p * (dp - dcol)).astype(BF16)
        dq2 = jnp.dot(ds, kb, preferred_element_type=F32)
        dk = lax.dot_general(ds, q2, tn_dims, preferred_element_type=F32)
        dv = lax.dot_general(p.astype(BF16), do2, tn_dims, preferred_element_type=F32)
        return dq2[0:128] * m0 + dq2[128:256] * m1, dk, dv

    def unit_per_head(qs, dos, ls, dls, kb, vb, ok, m0, m1):
        dq_t = dk_t = dv_t = None
        for mh in (m0, m1):
            qh = (qs * mh).astype(BF16)
            doh = (dos * mh).astype(BF16)
            s = jnp.where(ok, lax.dot_general(qh, kb, nt_dims, preferred_element_type=F32), NEG_INF)
            p = jnp.exp2(s - _head_col(ls, mh) * LOG2E)
            dp = lax.dot_general(doh, vb, nt_dims, preferred_element_type=F32)
            ds = (p * (dp - _head_col(dls, mh))).astype(BF16)
            dq = jnp.dot(ds, kb, preferred_element_type=F32) * mh
            dk = lax.dot_general(ds, qh, tn_dims, preferred_element_type=F32)
            dv = lax.dot_general(p.astype(BF16), doh, tn_dims, preferred_element_type=F32)
            dq_t = dq if dq_t is None else dq_t + dq
            dk_t = dk if dk_t is None else dk_t + dk
            dv_t = dv if dv_t is None else dv_t + dv
        return dq_t, dk_t, dv_t

    unit = unit_per_head if d == 1 else unit_stacked

    def body(cur_ref, prev_ref, do_ref, lse_ref, dl_ref, qw_ref, kw_ref, ebd_ref, dp_in_ref,
             out_ref, gqk_ref, qs_s, ks_s, vs_s, do_s, dq_cur, dq_prev, dk_acc, dv_acc):
        i = pl.program_id(1)
        e = ebd_ref[...]
        m0, m1 = _lane_masks()

        @pl.when(i == 0)
        def _():
            dk_acc[...] = jnp.zeros_like(dk_acc)
            dv_acc[...] = jnp.zeros_like(dv_acc)
            dq_prev[...] = jnp.zeros_like(dq_prev)
            gqk_ref[...] = jnp.zeros_like(gqk_ref)

        @pl.when(i < nt)
        def _():
            qs_s[...] = _head_norm(cur_ref[:, 0:128], qw_ref[...], e) * (QK_SCALE * LOG2E)
            ks_s[pl.ds(0, T), :] = _head_norm(prev_ref[:, 128:256], kw_ref[...], e)
            ks_s[pl.ds(T, T), :] = _head_norm(cur_ref[:, 128:256], kw_ref[...], e)
            vs_s[pl.ds(0, T), :] = prev_ref[:, 256:384].astype(F32)
            vs_s[pl.ds(T, T), :] = cur_ref[:, 256:384].astype(F32)
            do_s[...] = do_ref[...].astype(F32)
            first = i == 0
            for r in range(d):
                own_k = own_v = None
                for bb in range(nb):
                    q0 = r + bb * 128 * d
                    k0 = T + r + (bb - 1) * 128 * d
                    qs = qs_s[pl.ds(q0, 128, stride=d), :]
                    dos = do_s[pl.ds(q0, 128, stride=d), :]
                    ls = lse_ref[pl.ds(q0, 128, stride=d), :]
                    dls = dl_ref[pl.ds(q0, 128, stride=d), :]
                    kb = ks_s[pl.ds(k0, 256, stride=d), :].astype(BF16)
                    vb = vs_s[pl.ds(k0, 256, stride=d), :].astype(BF16)
                    ok = _window_mask(first if bb == 0 else None)
                    dq_t, dk_t, dv_t = unit(qs, dos, ls, dls, kb, vb, ok, m0, m1)
                    dq_cur[pl.ds(q0, 128, stride=d), :] = dq_t
                    before = pl.ds(k0, 128, stride=d)
                    if bb == 0:
                        dk_acc[before, :] = dk_acc[before, :] + dk_t[0:128]
                        dv_acc[before, :] = dv_acc[before, :] + dv_t[0:128]
                    else:
                        dk_acc[before, :] = own_k + dk_t[0:128]
                        dv_acc[before, :] = own_v + dv_t[0:128]
                    own_k, own_v = dk_t[128:256], dv_t[128:256]
                tail = pl.ds(T + r + (nb - 1) * 128 * d, 128, stride=d)
                dk_acc[tail, :] = own_k
                dv_acc[tail, :] = own_v

        @pl.when(i >= 1)
        def _():
            dq_raw, gq = norm_bwd(prev_ref[:, 0:128], dq_prev[...] * QK_SCALE, qw_ref[...], e)
            dk_raw, gk = norm_bwd(prev_ref[:, 128:256], dk_acc[pl.ds(0, T), :] * LN2, kw_ref[...], e)
            out_ref[:, 0:128] = dq_raw.astype(BF16)
            out_ref[:, 128:256] = dk_raw.astype(BF16)
            out_ref[:, 256:384] = dv_acc[pl.ds(0, T), :].astype(BF16)
            gqk_ref[0:1, :] += gq
            gqk_ref[1:2, :] += gk

        dq_prev[...] = dq_cur[...]
        dk_acc[pl.ds(0, T), :] = dk_acc[pl.ds(T, T), :]
        dv_acc[pl.ds(0, T), :] = dv_acc[pl.ds(T, T), :]

    last = nt - 1
    qtile = lambda p, i: (jnp.minimum(i, last), p)
    return pl.pallas_call(
        body, name=f"attn_bwd_g{g}", grid=(4, nt + 1),
        in_specs=[pl.BlockSpec((T, 384), lambda p, i: (jnp.minimum(i, last), g * 4 + p)),
                  pl.BlockSpec((T, 384), lambda p, i: (jnp.maximum(i - 1, 0), g * 4 + p)),
                  pl.BlockSpec((T, 128), qtile), pl.BlockSpec((T, 128), qtile), pl.BlockSpec((T, 128), qtile),
                  pl.BlockSpec((1, 128), lambda p, i: (0, 0)),
                  pl.BlockSpec((1, 128), lambda p, i: (0, 0)),
                  pl.BlockSpec((128, 128), lambda p, i: (0, 0)),
                  ANY],
        out_specs=[pl.BlockSpec((T, 384), lambda p, i: (jnp.maximum(i - 1, 0), g * 4 + p)),
                   pl.BlockSpec((None, 8, 128), lambda p, i: (p, 0, 0))],
        out_shape=[jax.ShapeDtypeStruct(dprojq.shape, BF16), jax.ShapeDtypeStruct((4, 8, 128), F32)],
        scratch_shapes=[pltpu.VMEM((T, 128), sdt), pltpu.VMEM((2 * T, 128), sdt), pltpu.VMEM((2 * T, 128), sdt),
                        pltpu.VMEM((T, 128), sdt), pltpu.VMEM((T, 128), F32), pltpu.VMEM((T, 128), F32),
                        pltpu.VMEM((2 * T, 128), F32), pltpu.VMEM((2 * T, 128), F32)],
        input_output_aliases={8: 0},
        compiler_params=_params(),
    )(projq, projq, d_o, lse, delta, qnw2, knw2, ebd, dprojq)


def _mid(projn, o_g, lse_g, x, tgt, mod, convw8, wc, wa, wo, ebd, *, tm=256):
    S, D = x.shape
    nt = S // tm
    nst = max(1, 256 // tm)
    assert nt % nst == 0
    hb = tm // 16

    def body(pn_ref, hc_ref, hx_ref, o0_ref, o1_ref, o2_ref, l0_ref, l1_ref, l2_ref, x_ref, t_ref, mod_ref,
             cw_ref, ebd_ref, wc_hbm, wa_hbm, wo_hbm,
             dpn_ref, do_ref, lse_ref, dl_ref, dout_ref, vacc_ref, gwo_hbm, gwc_hbm, gwa_hbm,
             wc_s, wa_s, wo_s, gwo_s, gwc_s, gwa_s, carry_s, *stashes):
        i = pl.program_id(0)
        ti = nt - 1 - i
        par = i % nst
        stash = pl.ds(pl.multiple_of(par * tm, tm), tm)

        @pl.when(i == 0)
        def _():
            for src, dst in ((wc_hbm, wc_s), (wa_hbm, wa_s), (wo_hbm, wo_s)):
                pltpu.sync_copy(src, dst)
            gwo_s[...] = jnp.zeros_like(gwo_s)
            gwc_s[...] = jnp.zeros_like(gwc_s)
            gwa_s[...] = jnp.zeros_like(gwa_s)
            carry_s[...] = jnp.zeros_like(carry_s)
            vacc_ref[...] = jnp.zeros_like(vacc_ref)

        rows = lax.broadcasted_iota(jnp.int32, (tm, D), 0)
        w0, w1, w2 = cw_ref[0:1, :], cw_ref[1:2, :], cw_ref[2:3, :]
        gate = mod_ref[:, 2 * D:3 * D]

        b_a = pn_ref[:, BA:BA + D].astype(F32)
        c_a = pn_ref[:, CA:CA + D].astype(F32)
        x_a = pn_ref[:, XA:XA + D].astype(F32)
        z_a = pn_ref[:, ZA:ZA + D].astype(F32)
        u = c_a * x_a
        has_prev = (ti > 0).astype(F32)
        uh = hc_ref[...].astype(F32) * hx_ref[...].astype(F32) * has_prev
        h14, h15 = _row(uh, 14), _row(uh, 15)
        u_m1 = jnp.where(rows == 0, h15, pltpu.roll(u, 1, 0))
        u_m2 = jnp.where(rows == 0, h14, jnp.where(rows == 1, h15, pltpu.roll(u, 2, 0)))
        conv = w0 * u_m2 + w1 * u_m1 + w2 * u
        sg_a = _sigmoid(z_a)
        sz_a = z_a * sg_a
        y_a = b_a * conv * sz_a
        y_ab = y_a.astype(BF16)
        pa = jnp.dot(y_ab, wc_s[...], preferred_element_type=F32)

        l0, l1, l2 = l0_ref[...], l1_ref[...], l2_ref[...]
        mxl = jnp.maximum(jnp.maximum(l0, l1), l2)
        e0, e1, e2 = jnp.exp(l0 - mxl), jnp.exp(l1 - mxl), jnp.exp(l2 - mxl)
        den = e0 + e1 + e2
        attn = (e0 * o0_ref[...] + e1 * o1_ref[...] + e2 * o2_ref[...]) / den
        lse_ref[...] = mxl + jnp.log(den)
        z_b = pn_ref[:, ZB:ZB + ATTN_OUT].astype(F32)
        sg_b = _sigmoid(z_b)
        sz_b = z_b * sg_b
        y_b = attn * sz_b
        y_bb = y_b.astype(BF16)
        pb = jnp.dot(y_bb, wa_s[...], preferred_element_type=F32)

        s_a = _sigmoid(pn_ref[:, GA:GA + D].astype(F32))
        s_b = _sigmoid(pn_ref[:, GB:GB + D].astype(F32))
        merged = s_a * pa + s_b * pb
        merged_b = merged.astype(BF16)
        mo = jnp.dot(merged_b, wo_s[...], preferred_element_type=F32)
        diff = x_ref[...] + gate * mo - t_ref[...]
        dout = diff * (1.0 / D)
        dout_ref[...] = dout.astype(BF16)
        vacc_ref[4:5, :] += jnp.sum(diff * diff, axis=0, keepdims=True)
        vacc_ref[0:1, :] += jnp.sum(dout * mo, axis=0, keepdims=True)

        d_mo = (dout * gate).astype(BF16)
        nt_dims = (((1,), (1,)), ((), ()))
        dmerged = lax.dot_general(d_mo, wo_s[...], nt_dims, preferred_element_type=F32)
        dpa = (dmerged * s_a).astype(BF16)
        dpb = (dmerged * s_b).astype(BF16)
        dpn_ref[:, GA:GA + D] = (dmerged * pa * s_a * (1.0 - s_a)).astype(BF16)
        dpn_ref[:, GB:GB + D] = (dmerged * pb * s_b * (1.0 - s_b)).astype(BF16)
        tn = (((0,), (0,)), ((), ()))
        if nst == 1:
            gwo_s[...] += lax.dot_general(merged_b, d_mo, tn, preferred_element_type=F32)
            gwc_s[...] += lax.dot_general(y_ab, dpa, tn, preferred_element_type=F32)
            gwa_s[...] += lax.dot_general(y_bb, dpb, tn, preferred_element_type=F32)
        else:
            st_m, st_d, st_ya, st_pa, st_yb, st_pb = stashes
            st_m[stash, :] = merged_b
            st_d[stash, :] = d_mo
            st_ya[stash, :] = y_ab
            st_pa[stash, :] = dpa
            st_yb[stash, :] = y_bb
            st_pb[stash, :] = dpb

            @pl.when(par == nst - 1)
            def _():
                gwo_s[...] += lax.dot_general(st_m[...], st_d[...], tn, preferred_element_type=F32)
                gwc_s[...] += lax.dot_general(st_ya[...], st_pa[...], tn, preferred_element_type=F32)
                gwa_s[...] += lax.dot_general(st_yb[...], st_pb[...], tn, preferred_element_type=F32)

        dy_a = lax.dot_general(dpa, wc_s[...], nt_dims, preferred_element_type=F32)
        dy_b = lax.dot_general(dpb, wa_s[...], nt_dims, preferred_element_type=F32)

        dattn = dy_b * sz_b
        dpn_ref[:, ZB:ZB + ATTN_OUT] = (dy_b * attn * sg_b * (1.0 + z_b * (1.0 - sg_b))).astype(BF16)
        do_ref[...] = dattn.astype(BF16)
        dl_ref[...] = jnp.dot((dattn * attn).astype(BF16), ebd_ref[...], preferred_element_type=F32)

        dpn_ref[:, BA:BA + D] = (dy_a * conv * sz_a).astype(BF16)
        dpn_ref[:, ZA:ZA + D] = (dy_a * b_a * conv * sg_a * (1.0 + z_a * (1.0 - sg_a))).astype(BF16)
        dconv = dy_a * b_a * sz_a
        vacc_ref[1:2, :] += jnp.sum(dconv * u_m2, axis=0, keepdims=True)
        vacc_ref[2:3, :] += jnp.sum(dconv * u_m1, axis=0, keepdims=True)
        vacc_ref[3:4, :] += jnp.sum(dconv * u, axis=0, keepdims=True)
        nxt = carry_s[...]
        n0, n1 = _row(nxt, 0), _row(nxt, 1)
        dc_p1 = jnp.where(rows == tm - 1, n0, pltpu.roll(dconv, tm - 1, 0))
        dc_p2 = jnp.where(rows == tm - 1, n1, jnp.where(rows == tm - 2, n0, pltpu.roll(dconv, tm - 2, 0)))
        du = w2 * dconv + w1 * dc_p1 + w0 * dc_p2
        carry_s[...] = dconv[0:8, :]
        dpn_ref[:, CA:CA + D] = (du * x_a).astype(BF16)
        dpn_ref[:, XA:XA + D] = (du * c_a).astype(BF16)

        @pl.when(i == nt - 1)
        def _():
            pltpu.sync_copy(gwo_s, gwo_hbm)
            pltpu.sync_copy(gwc_s, gwc_hbm)
            pltpu.sync_copy(gwa_s, gwa_hbm)

    rev = lambda i: (nt - 1 - i, 0)
    halo = lambda col: pl.BlockSpec((16, D), lambda i: (jnp.maximum((nt - 1 - i) * hb - 1, 0), col))
    tile512 = pl.BlockSpec((tm, ATTN_OUT), rev)
    tileD = pl.BlockSpec((tm, D), rev)
    const = lambda shape: pl.BlockSpec(shape, lambda i: (0, 0))
    return pl.pallas_call(
        body, name="mid_fwd_bwd", grid=(nt,),
        in_specs=[pl.BlockSpec((tm, NAT), rev), halo(CA // D), halo(XA // D)] + [tile512] * 6
                 + [tileD, tileD, const((1, 3 * D)), const((8, D)), const((ATTN_OUT, ATTN_OUT))] + [ANY] * 3,
        out_specs=[pl.BlockSpec((tm, NAT), rev), tile512, tile512, tile512, tileD, const((8, D)), ANY, ANY, ANY],
        out_shape=[jax.ShapeDtypeStruct((S, NAT), BF16), jax.ShapeDtypeStruct((S, ATTN_OUT), BF16),
                   jax.ShapeDtypeStruct((S, ATTN_OUT), F32), jax.ShapeDtypeStruct((S, ATTN_OUT), F32),
                   jax.ShapeDtypeStruct((S, D), BF16), jax.ShapeDtypeStruct((8, D), F32),
                   jax.ShapeDtypeStruct((D, D), F32), jax.ShapeDtypeStruct((D, D), F32),
                   jax.ShapeDtypeStruct((ATTN_OUT, D), F32)],
        scratch_shapes=[pltpu.VMEM((D, D), BF16), pltpu.VMEM((ATTN_OUT, D), BF16), pltpu.VMEM((D, D), BF16),
                        pltpu.VMEM((D, D), F32), pltpu.VMEM((D, D), F32), pltpu.VMEM((ATTN_OUT, D), F32),
                        pltpu.VMEM((8, D), F32)]
                       + ([pltpu.VMEM((nst * tm, D), BF16)] * 4
                          + [pltpu.VMEM((nst * tm, ATTN_OUT), BF16), pltpu.VMEM((nst * tm, D), BF16)] if nst > 1 else []),
        compiler_params=_params(vmem=V7X_VMEM_LIMIT_RESIDENT),
    )(projn, projn, projn, *o_g, *lse_g, x, tgt, mod, convw8, ebd, wc, wa, wo)


def _dh_matmul(dpn, dpq, w_nat, w_qkv, parts, *, tm=512):
    S = dpn.shape[0]
    D = D_MODEL
    n = len(parts)
    nt = S // tm

    def body(dn_ref, dq_ref, wn_hbm, wq_hbm, *rest):
        p_hbm, rest = rest[:n], rest[n:]
        dh_ref = rest[0]
        q_hbm = rest[1:1 + n]
        wn_s, wq_s = rest[1 + n:3 + n]
        sems = rest[3 + n:]

        @pl.when(pl.program_id(0) == 0)
        def _():
            if n:
                mine, _ = _scatter_copies(p_hbm, q_hbm, *sems)
                for cp in mine:
                    cp.start()
            pltpu.sync_copy(wn_hbm, wn_s)
            pltpu.sync_copy(wq_hbm, wq_s)

        if n:
            @pl.when(pl.program_id(0) == nt - 1)
            def _():
                mine, theirs = _scatter_copies(p_hbm, q_hbm, *sems)
                for cp in theirs:
                    cp.wait_recv()
                for cp in mine:
                    cp.wait_send()

        nt_dims = (((1,), (1,)), ((), ()))
        dh = (lax.dot_general(dn_ref[...], wn_s[...], nt_dims, preferred_element_type=F32)
              + lax.dot_general(dq_ref[...], wq_s[...], nt_dims, preferred_element_type=F32))
        dh_ref[...] = dh.astype(BF16)

    row = lambda w: pl.BlockSpec((tm, w), lambda i: (i, 0))
    outs = pl.pallas_call(
        body, name="dh_matmul", grid=(nt,),
        in_specs=[row(NAT), row(NQKV), ANY, ANY] + [ANY] * n,
        out_specs=[row(D)] + [ANY] * n,
        out_shape=[jax.ShapeDtypeStruct((S, D), BF16)] + [jax.ShapeDtypeStruct(p.shape, p.dtype) for p in parts],
        scratch_shapes=[pltpu.VMEM((D, NAT), BF16), pltpu.VMEM((D, NQKV), BF16)]
                       + ([pltpu.SemaphoreType.DMA((n, 3)), pltpu.SemaphoreType.DMA((n, 3))] if n else []),
        compiler_params=_params(vmem=V7X_VMEM_LIMIT_RESIDENT),
    )(dpn, dpq, w_nat, w_qkv, *parts)
    return outs[0], outs[1:]


def _norm_bwd(dh, x, dout, mod, norm_w, *, tm=512):
    S, D = x.shape

    def body(dh_ref, x_ref, dout_ref, mod_ref, nw_ref, gx_ref, st_ref):
        @pl.when(pl.program_id(0) == 0)
        def _():
            st_ref[...] = jnp.zeros_like(st_ref)

        dh = dh_ref[...].astype(F32)
        scale1 = 1.0 + mod_ref[:, D:2 * D]
        nw = nw_ref[...]
        xv = x_ref[...]
        rstd = lax.rsqrt(jnp.mean(xv * xv, axis=-1, keepdims=True) + EPS)
        xh = xv * rstd
        dhs = dh * scale1
        dxh = dhs * nw
        dx = rstd * (dxh - xh * jnp.mean(dxh * xh, axis=-1, keepdims=True))
        gx_ref[...] = dout_ref[...].astype(F32) + dx
        st_ref[0:1, :] += jnp.sum(dh, axis=0, keepdims=True)
        st_ref[1:2, :] += jnp.sum(dh * (xh * nw), axis=0, keepdims=True)
        st_ref[2:3, :] += jnp.sum(dhs * xh, axis=0, keepdims=True)

    row = pl.BlockSpec((tm, D), lambda i: (i, 0))
    return pl.pallas_call(
        body, name="norm_bwd", grid=(S // tm,),
        in_specs=[row, row, row, pl.BlockSpec((1, 3 * D), lambda i: (0, 0)), pl.BlockSpec((1, D), lambda i: (0, 0))],
        out_specs=[row, pl.BlockSpec((8, D), lambda i: (0, 0))],
        out_shape=[jax.ShapeDtypeStruct((S, D), F32), jax.ShapeDtypeStruct((8, D), F32)],
        compiler_params=_params(),
    )(dh, x, dout, mod, norm_w)


def _pack_smalls(st, vacc, gqk):
    D = D_MODEL

    def body(st_ref, va_ref, gqk_ref, o_ref):
        o_ref[...] = jnp.zeros_like(o_ref)
        o_ref[0:2, :] = st_ref[0:2, :]
        o_ref[2:3, :] = va_ref[0:1, :]
        o_ref[8:9, :] = st_ref[2:3, :]
        o_ref[16:19, :] = va_ref[1:4, :]
        tot = gqk_ref[0]
        for k in range(1, gqk_ref.shape[0]):
            tot = tot + gqk_ref[k]
        tot = tot + pltpu.roll(tot, HEAD_DIM, 1)
        o_ref[24:32, 0:128] = tot
        loss = jnp.sum(va_ref[4:5, :], axis=1, keepdims=True) * (0.5 / D)
        o_ref[26:27, :] = jnp.broadcast_to(loss, (1, D))

    return pl.pallas_call(
        body, name="pack_smalls", out_shape=jax.ShapeDtypeStruct((32, D), F32),
        in_specs=[pl.BlockSpec(memory_space=pltpu.VMEM)] * 3,
        out_specs=pl.BlockSpec(memory_space=pltpu.VMEM), compiler_params=_params(),
    )(st, vacc, gqk)


def _small_update(sm_loc, dm_pad, ct_pad, ada, vecs):
    D = D_MODEL
    nv = len(vecs)
    places = [(0, 3, D), (8, 1, D), (16, 3, 256), (24, 1, HEAD_DIM), (25, 1, HEAD_DIM)]

    def body(*refs):
        sm_ref, dm_ref, ct_ref = refs[0:3]
        wmv = [refs[3 + 3 * t:6 + 3 * t] for t in range(nv + 1)]
        outs = refs[3 + 3 * (nv + 1):]
        res = [outs[4 * t:4 * t + 4] for t in range(nv + 1)]
        loss_ref, tot_s = outs[4 * (nv + 1)], outs[4 * (nv + 1) + 1]

        def update(g, w, m, v, out, at=(slice(None), slice(None))):
            d, mn, vn = _adamw_math(w[at], g, m[at], v[at])
            for ref, val in zip(out, (g, d, mn, vn)):
                ref[at] = val

        gw = jnp.dot(ct_ref[...], dm_ref[...], preferred_element_type=F32, precision=lax.Precision.HIGHEST)
        update(gw, *wmv[0], res[0])
        tot = sm_ref[0]
        for k in range(1, 8):
            tot = tot + sm_ref[k]
        tot_s[...] = tot
        for k in range(3):
            update(tot_s[k:k + 1, :], *wmv[1], res[1], at=(slice(None), slice(k * D, (k + 1) * D)))
        for t in range(1, nv):
            r0, nr, nl = places[t]
            update(tot_s[r0:r0 + nr, 0:nl], *wmv[1 + t], res[1 + t])
        loss_ref[...] = jnp.broadcast_to(tot_s[26:27, 0:128], (8, 128))

    vm = pl.BlockSpec(memory_space=pltpu.VMEM)
    groups = [ada] + list(vecs)
    out_shape = []
    for w, _, _ in groups:
        out_shape += [jax.ShapeDtypeStruct(w.shape, F32)] * 4
    out_shape.append(jax.ShapeDtypeStruct((8, 128), F32))
    flat = [a for grp in groups for a in grp]
    outs = pl.pallas_call(
        body, name="small_update", out_shape=out_shape,
        in_specs=[vm] * (3 + len(flat)), out_specs=[vm] * len(out_shape),
        scratch_shapes=[pltpu.VMEM((32, D), F32)], compiler_params=_params(),
    )(sm_loc, dm_pad, ct_pad, *flat)
    return [outs[4 * t:4 * t + 4] for t in range(nv + 1)], outs[-1]


def _later_weights(gathered, shards, chip):
    D = D_MODEL
    g_bc, g_ba, g_wo = [lax.dynamic_update_slice(g, s[None], (chip, 0, 0, 0)) for g, s in zip(gathered, shards)]
    wa = g_ba.reshape(4, ATTN_OUT, 256).transpose(1, 0, 2).reshape(ATTN_OUT, D)
    return g_bc.reshape(D, D), wa, g_wo.reshape(D, D)


def _local_step(xs, tg, mod, norm_w, w_nat, w_qkv, later, convw8, q_norm_w, k_norm_w):
    S = xs.shape[0]
    lane = jnp.arange(128)
    ebd128 = (lane[:, None] // HEAD_DIM == lane[None, :] // HEAD_DIM).astype(BF16)
    lane5 = jnp.arange(ATTN_OUT)
    ebd512 = (lane5[:, None] // HEAD_DIM == lane5[None, :] // HEAD_DIM).astype(BF16)
    qnw2 = jnp.tile(q_norm_w, (1, 2))
    knw2 = jnp.tile(k_norm_w, (1, 2))

    projn, h, ht = _inproj_nat(xs, mod, norm_w, w_nat)
    if len(later) == 2:
        projq, gathered = _inproj_qkv(h, w_qkv, later[0])
        wc, wa, wo = _later_weights(gathered, *later)
    else:
        projq, _ = _inproj_qkv(h, w_qkv, [])
        wc, wa, wo = later
    o_g, lse_g = [], []
    for g in range(N_GROUPS):
        o, l = _attn_fwd(projq, qnw2, knw2, ebd128, g)
        o_g.append(o)
        lse_g.append(l)

    dpn, d_o, lse, delta, dout, vacc, gwo, gwc, gwa = _mid(
        projn, o_g, lse_g, xs, tg, mod, convw8, wc, wa, wo, ebd512)
    dpq = lax.empty((S, NQKV), BF16)
    gqk = []
    for g in range(N_GROUPS):
        dpq, part = _attn_bwd(projq, dpq, d_o, lse, delta, qnw2, knw2, ebd128, g)
        gqk.append(part)
    gw_nat = _matmul_acc(ht, dpn, tn=1664, tk=2048, name="grad_w_in_nat")
    gw_qkv = _matmul_acc(ht, dpq, tn=1536, tk=2048, name="grad_w_in_qkv")
    return dpn, dpq, dout, gw_nat, gw_qkv, gwc, gwa, gwo, vacc, gqk


def kernel(x, c, w_ada, b_ada, norm_w, w_in, conv_w, q_norm_w, k_norm_w, w_br_conv, w_br_attn, w_out, loss_target, m_w_ada, m_b_ada, m_norm_w, m_w_in, m_conv_w, m_q_norm_w, m_k_norm_w, m_w_br_conv, m_w_br_attn, m_w_out, v_w_ada, v_b_ada, v_norm_w, v_w_in, v_conv_w, v_q_norm_w, v_k_norm_w, v_w_br_conv, v_w_br_attn, v_w_out):
    D = D_MODEL
    S = x.shape[1]
    xs, tg = x[0], loss_target[0]
    mx, my, mc = lax.axis_index("x"), lax.axis_index("y"), lax.axis_index("c")
    chip = 2 * mx + my
    dev = 2 * chip + mc

    c_blk = jnp.concatenate([c, jnp.pad(conv_w[0], ((0, 0), (0, D - 256))), jnp.zeros((4, D), F32)], axis=0)
    b_shard = lax.dynamic_slice(b_ada, (0, chip * 768), (1, 768))
    shards = [w_in[0].astype(BF16).reshape(2, 512, 2816), w_br_conv[0].astype(BF16).reshape(2, 128, D),
              w_br_attn[0].astype(BF16).reshape(2, 256, 256), w_out[0].astype(BF16).reshape(2, 128, D)]
    gathered_in, _, cg, mod_parts, silu_c = _weights_allgather(
        shards[0].reshape(2, 2, 256, SHARD_COLS), [], c_blk, w_ada[0], b_shard)
    convw8 = jnp.pad(cg[::2, 1:4, 0:256].transpose(1, 0, 2).reshape(3, D), ((0, 5), (0, 0)))
    mod_all = mod_parts.transpose(1, 0, 2).reshape(8, 3 * D)
    mod = lax.dynamic_slice(mod_all, (dev, 0), (1, 3 * D))
    chipv = jnp.reshape(chip, (1,)).astype(jnp.int32)
    g_in = gathered_in.reshape(4, D, SHARD_COLS)
    own_in = shards[0].reshape(D, SHARD_COLS)
    w_nat = _weights_prep(g_in, own_in, chipv, "nat")
    w_qkv = _weights_prep(g_in, own_in, chipv, "qkv")

    dpn, dpq, dout, gw_nat, gw_qkv, gwc, gwa, gwo, vacc, gqk = _local_step(
        xs, tg, mod, norm_w, w_nat, w_qkv, (shards[1:], chip), convw8, q_norm_w, k_norm_w)

    def halves(a):
        k, r, cc = a.shape
        return a.reshape(k, 2, r // 2, cc).transpose(1, 0, 2, 3)

    grads = [gw_nat.reshape(2, D // 2, NAT), gw_qkv.reshape(2, D // 2, NQKV),
             halves(gwc.astype(BF16).reshape(4, 256, D)),
             halves(gwa.astype(BF16).reshape(ATTN_OUT, 4, 256).transpose(1, 0, 2)),
             halves(gwo.astype(BF16).reshape(4, 256, D))]
    recv = _pair_exchange(grads)
    core = jnp.reshape(mc, (1,)).astype(jnp.int32)
    p_in = _pair_add_to_shards(core, grads[0], recv[0], lax.empty((4, D // 2, SHARD_COLS), BF16), "nat")
    p_in = _pair_add_to_shards(core, grads[1], recv[1], p_in, "qkv")
    parts = [p_in] + [_pair_add(core, gr, rc) for gr, rc in zip(grads[2:], recv[2:])]
    dh, gathered = _dh_matmul(dpn, dpq, w_nat, w_qkv, parts)
    grad_x, st = _norm_bwd(dh, xs, dout, mod, norm_w)
    mine = [_sum4(chipv, p, q) for p, q in zip(parts, gathered)]
    theirs, sm_all = _half_exchange(mine, _pack_smalls(st, vacc, jnp.concatenate(gqk, axis=0)))

    big = {}
    for t, (nm, w, m, v) in enumerate((("w_in", w_in, m_w_in, v_w_in), ("w_br_conv", w_br_conv, m_w_br_conv, v_w_br_conv),
                                       ("w_br_attn", w_br_attn, m_w_br_attn, v_w_br_attn), ("w_out", w_out, m_w_out, v_w_out))):
        big[nm] = _adamw_halves(core, w[0], mine[t], theirs[t], m[0], v[0], "adamw_" + nm)

    dmod_all = sm_all[:, 0:3, :].reshape(8, 3 * D)
    dm_pad = jnp.pad(lax.dynamic_slice(dmod_all, (0, chip * 768), (8, 768)), ((0, 120), (0, 0)))
    conv_dev = lax.dynamic_slice(sm_all[:, 16:24, :], (0, 0, chip * 256), (8, 8, 256))
    sm_loc = jnp.concatenate([sm_all[:, 0:16, :], jnp.pad(conv_dev, ((0, 0), (0, 0), (0, D - 256))), sm_all[:, 24:32, :]], axis=1)

    ct_pad = jnp.pad(silu_c.T, ((0, 0), (0, 120)))
    small_names = ["b_ada", "norm_w", "conv_w", "q_norm_w", "k_norm_w"]
    vecs = [(b_ada, m_b_ada, v_b_ada), (norm_w, m_norm_w, v_norm_w), (conv_w[0], m_conv_w[0], v_conv_w[0]),
            (q_norm_w, m_q_norm_w, v_q_norm_w), (k_norm_w, m_k_norm_w, v_k_norm_w)]
    updated, loss_blk = _small_update(sm_loc, dm_pad, ct_pad, (w_ada[0], m_w_ada[0], v_w_ada[0]), vecs)
    small = {"w_ada": [a[None] for a in updated[0]]}
    for nm, four in zip(small_names, updated[1:]):
        small[nm] = [a[None] for a in four] if nm == "conv_w" else list(four)

    order = ["w_ada", "b_ada", "norm_w", "w_in", "conv_w", "q_norm_w", "k_norm_w", "w_br_conv", "w_br_attn", "w_out"]
    res = [loss_blk[0, 0], grad_x[None]]
    for kind in range(4):
        for nm in order:
            res.append(big[nm][kind][None] if nm in big else small[nm][kind])
    return tuple(res)
```

```python
import jax
import jax.numpy as jnp
from jax import lax
from jax.experimental import pallas as pl
from jax.experimental.pallas import tpu as pltpu

F32 = jnp.float32
BF16 = jnp.bfloat16
MESH = pl.DeviceIdType.MESH
ANY = pl.BlockSpec(memory_space=pl.ANY)

D_MODEL = 1024
HEAD_DIM = 64
N_GROUPS = 3
DILATIONS = (1, 4, 16)
ATTN_OUT = 512
EPS = 1e-6
NEG_INF = -1e30
NAT = 6656
NQKV = 4608
BA, CA, XA, ZA, ZB, GA, GB = 0, 1024, 2048, 3072, 4096, 4608, 5632
ATT_TILE = 2048
QK_SCALE = HEAD_DIM ** -0.5
LOG2E = 1.4426950408889634
LN2 = 0.6931471805599453
V7X_VMEM_LIMIT = 56 * 1024 * 1024
V7X_VMEM_LIMIT_RESIDENT = 60 * 1024 * 1024

ADAM_LR = 0.001
ADAM_B1 = 0.9
ADAM_B2 = 0.999
ADAM_EPS = 1e-08
ADAM_WD = 0.01
ADAM_STEP = 10


def _params(vmem=V7X_VMEM_LIMIT, **kw):
    return pltpu.CompilerParams(vmem_limit_bytes=vmem, **kw)


def _sigmoid(z):
    return 1.0 / (1.0 + jnp.exp(-z))


def _row(v, r):
    rows = lax.broadcasted_iota(jnp.int32, v.shape, 0)
    return jnp.sum(jnp.where(rows == r, v, 0.0), axis=0, keepdims=True)


def _coords():
    return lax.axis_index("x"), lax.axis_index("y"), lax.axis_index("c")


def _flip(v, bit):
    return 1 - v if bit else v


def _weights_allgather(big_shard, shards, c_blk, w_ada, b_shard):
    n = len(shards)
    D = D_MODEL
    cols = w_ada.shape[1]

    def body(*refs):
        big_in, ins = refs[0], refs[1:1 + n]
        c_ref, wada_ref, b_ref = refs[1 + n:4 + n]
        big_out, outs = refs[4 + n], refs[5 + n:5 + 2 * n]
        cg_ref, mp_ref, sc_ref = refs[5 + 2 * n:8 + 2 * n]
        sem_refs = refs[8 + 2 * n:]
        ssem, rsem = sem_refs[:2] if n else (None, None)
        bs, br, cs, cr, ms, mr, lsem = sem_refs[2 if n else 0:]
        mx, my, mc = _coords()
        me = 2 * mx + my
        me8 = 2 * me + mc
        sib = (mx, my, 1 - mc)
        chips = [(_flip(mx, j & 2), _flip(my, j & 1)) for j in range(1, 4)]
        sends = []

        local = pltpu.make_async_copy(c_ref, cg_ref.at[me8], lsem)
        local.start()
        for j in range(1, 8):
            peer = (_flip(mx, j & 4), _flip(my, j & 2), _flip(mc, j & 1))
            cp = pltpu.make_async_remote_copy(
                src_ref=c_ref, dst_ref=cg_ref.at[me8], send_sem=cs.at[j - 1], recv_sem=cr.at[j - 1],
                device_id=peer, device_id_type=MESH)
            cp.start()
            sends.append(cp)

        kx, ky, kd = 2 * (1 - mx) + my, 2 * mx + (1 - my), 2 * (1 - mx) + (1 - my)
        xn, yn = (1 - mx, my, mc), (mx, 1 - my, mc)

        def big(src_chip, q, sem, to, half=None):
            h = mc if half is None else half
            return pltpu.make_async_remote_copy(
                src_ref=big_out.at[src_chip, h, q], dst_ref=big_out.at[src_chip, h, q],
                send_sem=bs.at[sem], recv_sem=br.at[sem], device_id=to, device_id_type=MESH)

        def own(q, sem, to):
            return pltpu.make_async_remote_copy(
                src_ref=big_in.at[mc, q], dst_ref=big_out.at[me, mc, q],
                send_sem=bs.at[sem], recv_sem=br.at[sem], device_id=to, device_id_type=MESH)

        for cp in (own(0, 0, xn), own(1, 3, yn), own(1, 1, xn), own(0, 2, yn)):
            cp.start()
            sends.append(cp)
        for t in range(n):
            for j, (px, py) in enumerate(chips):
                cp = pltpu.make_async_remote_copy(
                    src_ref=ins[t].at[mc], dst_ref=outs[t].at[me, mc], send_sem=ssem.at[t, j],
                    recv_sem=rsem.at[t, j], device_id=(px, py, mc), device_id_type=MESH)
                cp.start()
                sends.append(cp)

        pieces = [(kx, 0), (kx, 1), (ky, 0), (ky, 1), (kd, 0), (kd, 1)]

        def landed(piece, sem, frm, pass_on=None):
            src_chip, q = pieces[piece]
            big(src_chip, q, sem, frm).wait_recv()
            nxt = ([] if pass_on is None else [big(src_chip, q, pass_on[0], pass_on[1])])
            nxt.append(big(src_chip, q, 6 + piece, sib))
            for cp in nxt:
                cp.start()
                sends.append(cp)

        for j in range(1, 8):
            px, py, pc = _flip(mx, j & 4), _flip(my, j & 2), _flip(mc, j & 1)
            pltpu.make_async_remote_copy(
                src_ref=c_ref, dst_ref=cg_ref.at[4 * px + 2 * py + pc], send_sem=cs.at[j - 1],
                recv_sem=cr.at[j - 1], device_id=(px, py, pc), device_id_type=MESH).wait_recv()
        local.wait()
        pick = (lax.broadcasted_iota(jnp.int32, (8, 64), 1) == 8 * lax.broadcasted_iota(jnp.int32, (8, 64), 0))
        cv = jnp.dot(pick.astype(F32), cg_ref[...].reshape(64, D), preferred_element_type=F32,
                     precision=lax.Precision.HIGHEST)
        sc = cv * _sigmoid(cv)
        sc_ref[...] = sc
        mp_ref[me] = jnp.dot(sc, wada_ref[...], preferred_element_type=F32,
                             precision=lax.Precision.HIGHEST) + b_ref[...]
        for j, (px, py) in enumerate(chips):
            cp = pltpu.make_async_remote_copy(
                src_ref=mp_ref.at[me], dst_ref=mp_ref.at[me], send_sem=ms.at[j], recv_sem=mr.at[j],
                device_id=(px, py, mc), device_id_type=MESH)
            cp.start()
            sends.append(cp)

        landed(0, 0, xn, pass_on=(4, yn))
        landed(3, 3, yn, pass_on=(5, xn))
        landed(1, 1, xn)
        landed(2, 2, yn)
        for j, (px, py) in enumerate(chips):
            pltpu.make_async_remote_copy(
                src_ref=mp_ref.at[me], dst_ref=mp_ref.at[2 * px + py], send_sem=ms.at[j], recv_sem=mr.at[j],
                device_id=(px, py, mc), device_id_type=MESH).wait_recv()
        landed(4, 4, yn)
        landed(5, 5, xn)
        for t in range(n):
            for j, (px, py) in enumerate(chips):
                src = 2 * px + py
                pltpu.make_async_remote_copy(
                    src_ref=ins[t].at[mc], dst_ref=outs[t].at[src, mc], send_sem=ssem.at[t, j],
                    recv_sem=rsem.at[t, j], device_id=(px, py, mc), device_id_type=MESH).wait_recv()
                fwd = pltpu.make_async_remote_copy(
                    src_ref=outs[t].at[src, mc], dst_ref=outs[t].at[src, mc], send_sem=ssem.at[t, 3 + j],
                    recv_sem=rsem.at[t, 3 + j], device_id=sib, device_id_type=MESH)
                fwd.start()
                sends.append(fwd)
        for t in range(n):
            for j, (px, py) in enumerate(chips):
                src = 2 * px + py
                pltpu.make_async_remote_copy(
                    src_ref=outs[t].at[src, 1 - mc], dst_ref=outs[t].at[src, 1 - mc], send_sem=ssem.at[t, 3 + j],
                    recv_sem=rsem.at[t, 3 + j], device_id=sib, device_id_type=MESH).wait_recv()
        for i, (src_chip, q) in enumerate(pieces):
            big(src_chip, q, 6 + i, sib, half=1 - mc).wait_recv()
        for cp in sends:
            cp.wait_send()

    vm = pl.BlockSpec(memory_space=pltpu.VMEM)
    outs = pl.pallas_call(
        body, name="weights_allgather",
        out_shape=[jax.ShapeDtypeStruct((4,) + big_shard.shape, big_shard.dtype)]
                  + [jax.ShapeDtypeStruct((4,) + s.shape, s.dtype) for s in shards]
                  + [jax.ShapeDtypeStruct((8,) + c_blk.shape, F32), jax.ShapeDtypeStruct((4, 8, cols), F32),
                     jax.ShapeDtypeStruct((8, D), F32)],
        in_specs=[ANY] * (n + 1) + [vm] * 3, out_specs=[ANY] * (n + 1) + [vm] * 3,
        scratch_shapes=([pltpu.SemaphoreType.DMA((n, 6)), pltpu.SemaphoreType.DMA((n, 6))] if n else [])
                       + [pltpu.SemaphoreType.DMA((12,)), pltpu.SemaphoreType.DMA((12,)),
                          pltpu.SemaphoreType.DMA((7,)), pltpu.SemaphoreType.DMA((7,)),
                          pltpu.SemaphoreType.DMA((3,)), pltpu.SemaphoreType.DMA((3,)), pltpu.SemaphoreType.DMA(())],
        compiler_params=_params(),
    )(big_shard, *shards, c_blk, w_ada, b_shard)
    return outs[0], outs[1:1 + n], outs[1 + n], outs[2 + n], outs[3 + n]


def _shard_gather_copies(ins, outs, ssem, rsem):
    mx, my, mc = _coords()
    me = 2 * mx + my
    sib = (mx, my, 1 - mc)
    send, recv, fwd, fwd_recv = [], [], [], []
    for t in range(len(ins)):
        for j in range(3):
            px, py = _flip(mx, (j + 1) & 2), _flip(my, (j + 1) & 1)
            src = 2 * px + py
            far = dict(send_sem=ssem.at[t, j], recv_sem=rsem.at[t, j], device_id=(px, py, mc), device_id_type=MESH)
            near = dict(send_sem=ssem.at[t, 3 + j], recv_sem=rsem.at[t, 3 + j], device_id=sib, device_id_type=MESH)
            send.append(pltpu.make_async_remote_copy(src_ref=ins[t].at[mc], dst_ref=outs[t].at[me, mc], **far))
            recv.append(pltpu.make_async_remote_copy(src_ref=ins[t].at[mc], dst_ref=outs[t].at[src, mc], **far))
            fwd.append(pltpu.make_async_remote_copy(src_ref=outs[t].at[src, mc], dst_ref=outs[t].at[src, mc], **near))
            fwd_recv.append(pltpu.make_async_remote_copy(
                src_ref=outs[t].at[src, 1 - mc], dst_ref=outs[t].at[src, 1 - mc], **near))
    return send, recv, fwd, fwd_recv


def _pair_exchange(grads):
    n = len(grads)

    def body(*refs):
        ins, outs = refs[:n], refs[n:2 * n]
        ssem, rsem = refs[2 * n:]
        mx, my, mc = _coords()
        sib = (mx, my, 1 - mc)
        sends = []
        for t in range(n):
            cp = pltpu.make_async_remote_copy(
                src_ref=ins[t].at[1 - mc], dst_ref=outs[t], send_sem=ssem.at[t], recv_sem=rsem.at[t],
                device_id=sib, device_id_type=MESH)
            cp.start()
            sends.append(cp)
        for cp in sends:
            cp.wait_recv()
        for cp in sends:
            cp.wait_send()

    return pl.pallas_call(
        body, name="grad_pair_exchange",
        out_shape=[jax.ShapeDtypeStruct(g.shape[1:], g.dtype) for g in grads],
        in_specs=[ANY] * n, out_specs=[ANY] * n,
        scratch_shapes=[pltpu.SemaphoreType.DMA((n,)), pltpu.SemaphoreType.DMA((n,))],
    )(*grads)


def _scatter_copies(ins, outs, ssem, rsem):
    mx, my, mc = _coords()
    me = 2 * mx + my
    mine, theirs = [], []
    for t in range(len(ins)):
        for j in range(1, 4):
            px, py = _flip(mx, j & 2), _flip(my, j & 1)
            mine.append(pltpu.make_async_remote_copy(
                src_ref=ins[t].at[2 * px + py], dst_ref=outs[t].at[me], send_sem=ssem.at[t, j - 1],
                recv_sem=rsem.at[t, j - 1], device_id=(px, py, mc), device_id_type=MESH))
            theirs.append(pltpu.make_async_remote_copy(
                src_ref=ins[t].at[me], dst_ref=outs[t].at[2 * px + py], send_sem=ssem.at[t, j - 1],
                recv_sem=rsem.at[t, j - 1], device_id=(px, py, mc), device_id_type=MESH))
    return mine, theirs


def _half_exchange(halves, smalls):
    n = len(halves)

    def body(*refs):
        ins, sm_ref = refs[:n], refs[n]
        outs, all_ref = refs[n + 1:2 * n + 1], refs[2 * n + 1]
        ssem, rsem, gsem, hsem, lsem = refs[2 * n + 2:]
        mx, my, mc = _coords()
        me = 4 * mx + 2 * my + mc
        sib = (mx, my, 1 - mc)
        local = pltpu.make_async_copy(sm_ref, all_ref.at[me], lsem)
        local.start()
        sends = []
        for t in range(n):
            rc = pltpu.make_async_remote_copy(
                src_ref=ins[t], dst_ref=outs[t], send_sem=ssem.at[t], recv_sem=rsem.at[t],
                device_id=sib, device_id_type=MESH)
            rc.start()
            sends.append(rc)
        gathers = []
        for j in range(1, 8):
            peer = (_flip(mx, j & 4), _flip(my, j & 2), _flip(mc, j & 1))
            cp = pltpu.make_async_remote_copy(
                src_ref=sm_ref, dst_ref=all_ref.at[me], send_sem=gsem.at[j - 1], recv_sem=hsem.at[j - 1],
                device_id=peer, device_id_type=MESH)
            cp.start()
            gathers.append(cp)
        for cp in sends:
            cp.wait_recv()
        for j in range(1, 8):
            px, py, pc = _flip(mx, j & 4), _flip(my, j & 2), _flip(mc, j & 1)
            pltpu.make_async_remote_copy(
                src_ref=sm_ref, dst_ref=all_ref.at[4 * px + 2 * py + pc], send_sem=gsem.at[j - 1],
                recv_sem=hsem.at[j - 1], device_id=(px, py, pc), device_id_type=MESH).wait_recv()
        for cp in sends + gathers:
            cp.wait_send()
        local.wait()

    vm = pl.BlockSpec(memory_space=pltpu.VMEM)
    outs = pl.pallas_call(
        body, name="grad_half_exchange",
        out_shape=[jax.ShapeDtypeStruct(h.shape, h.dtype) for h in halves]
                  + [jax.ShapeDtypeStruct((8,) + smalls.shape, smalls.dtype)],
        in_specs=[ANY] * n + [vm], out_specs=[ANY] * n + [vm],
        scratch_shapes=[pltpu.SemaphoreType.DMA((n,)), pltpu.SemaphoreType.DMA((n,)),
                        pltpu.SemaphoreType.DMA((7,)), pltpu.SemaphoreType.DMA((7,)), pltpu.SemaphoreType.DMA(())],
    )(*halves, smalls)
    return outs[:n], outs[n]


def _row_tile(rows, cols, bytes_per_row_elem=4, budget=2 * 1024 * 1024):
    t = rows
    while t % 2 == 0 and t > 16 and t * cols * bytes_per_row_elem > budget:
        t //= 2
    return t


def _pair_add(core, g, r):
    _, _, rh, cc = g.shape
    tr = _row_tile(rh, cc)

    def body(core_ref, g_ref, r_ref, o_ref):
        o_ref[...] = (g_ref[...].astype(F32) + r_ref[...].astype(F32)).astype(o_ref.dtype)

    return pl.pallas_call(
        body, name="grad_pair_add",
        grid_spec=pltpu.PrefetchScalarGridSpec(
            num_scalar_prefetch=1, grid=(4, rh // tr),
            in_specs=[pl.BlockSpec((None, None, tr, cc), lambda k, i, c: (c[0], k, i, 0)),
                      pl.BlockSpec((None, tr, cc), lambda k, i, c: (k, i, 0))],
            out_specs=pl.BlockSpec((None, tr, cc), lambda k, i, c: (k, i, 0))),
        out_shape=jax.ShapeDtypeStruct(r.shape, BF16),
        compiler_params=_params(),
    )(core, g, r)


def _sum4(chip, p, q):
    _, rh, cc = q.shape
    tr = _row_tile(rh, cc)

    def body(chip_ref, p_ref, q1_ref, q2_ref, q3_ref, o_ref):
        o_ref[...] = ((p_ref[...].astype(F32) + q1_ref[...].astype(F32)) + q2_ref[...].astype(F32)) + q3_ref[...].astype(F32)

    def other(j):
        return pl.BlockSpec((None, tr, cc), lambda i, c: (jnp.bitwise_xor(c[0], j), i, 0))

    return pl.pallas_call(
        body, name="grad_sum4",
        grid_spec=pltpu.PrefetchScalarGridSpec(
            num_scalar_prefetch=1, grid=(rh // tr,),
            in_specs=[pl.BlockSpec((None, tr, cc), lambda i, c: (c[0], i, 0)), other(1), other(2), other(3)],
            out_specs=pl.BlockSpec((tr, cc), lambda i, c: (i, 0))),
        out_shape=jax.ShapeDtypeStruct((rh, cc), F32),
        compiler_params=_params(),
    )(chip, p, q, q, q)


def _adamw_math(w, g, m, v):
    m = ADAM_B1 * m + (1.0 - ADAM_B1) * g
    v = ADAM_B2 * v + (1.0 - ADAM_B2) * (g * g)
    m_hat = m / (1.0 - ADAM_B1 ** ADAM_STEP)
    v_hat = v / (1.0 - ADAM_B2 ** ADAM_STEP)
    delta = -ADAM_LR * (m_hat / (jnp.sqrt(v_hat) + ADAM_EPS) + ADAM_WD * w)
    return delta, m, v


def _adamw_halves(core, w, mine, recv, m, v, name):
    rows, cols = w.shape
    rh = rows // 2
    tr = _row_tile(rh, cols, budget=1024 * 1024)
    nh = rh // tr

    def body(core_ref, w_ref, a_ref, b_ref, m_ref, v_ref, g_ref, d_ref, mo_ref, vo_ref):
        g = jnp.where(pl.program_id(0) == core_ref[0], a_ref[...], b_ref[...])
        d, mn, vn = _adamw_math(w_ref[...], g, m_ref[...], v_ref[...])
        g_ref[...] = g
        d_ref[...] = d
        mo_ref[...] = mn
        vo_ref[...] = vn

    full = pl.BlockSpec((tr, cols), lambda h, i, c: (h * nh + i, 0))
    half = pl.BlockSpec((tr, cols), lambda h, i, c: (i, 0))
    sd = jax.ShapeDtypeStruct((rows, cols), F32)
    return pl.pallas_call(
        body, name=name,
        grid_spec=pltpu.PrefetchScalarGridSpec(
            num_scalar_prefetch=1, grid=(2, nh),
            in_specs=[full, half, half, full, full], out_specs=[full] * 4),
        out_shape=[sd, sd, sd, sd], compiler_params=_params(),
    )(core, w, mine, recv, m, v)


def _inproj_nat(x, mod, norm_w, w, *, tm=1024, tn=1664):
    S, D = x.shape
    N = w.shape[1]
    nt, nj = S // tm, N // tn
    chunk = 256

    def body(x_ref, mod_ref, nw_ref, w_ref, proj_ref, h_ref, ht_ref, h_a, ht_a, h_b, ht_b):
        i, j = pl.program_id(0), pl.program_id(1)

        def make_h(h_s, ht_s):
            shift = mod_ref[:, 0:D]
            scale1 = 1.0 + mod_ref[:, D:2 * D]
            for r in range(tm // chunk):
                xv = x_ref[pl.ds(r * chunk, chunk), :]
                ms = jnp.mean(xv * xv, axis=-1, keepdims=True)
                h = (xv * lax.rsqrt(ms + EPS) * nw_ref[...]) * scale1 + shift
                h_s[pl.ds(r * chunk, chunk), :] = h.astype(BF16)
                ht_s[:, pl.ds(r * chunk, chunk)] = h.T.astype(BF16)

        @pl.when((i == 0) & (j == 0))
        def _():
            make_h(h_a, ht_a)

        def step(cur, nxt):
            @pl.when(j == 0)
            def _():
                h_ref[...] = cur[0][...]
                ht_ref[...] = cur[1][...]

            def project():
                proj_ref[...] = jnp.dot(cur[0][...], w_ref[...], preferred_element_type=F32).astype(BF16)

            ahead = (j == nj - 1) & (i < nt - 1)

            @pl.when(ahead)
            def _():
                project()
                make_h(*nxt)

            @pl.when(jnp.logical_not(ahead))
            def _():
                project()

        @pl.when(i % 2 == 0)
        def _():
            step((h_a, ht_a), (h_b, ht_b))

        @pl.when(i % 2 == 1)
        def _():
            step((h_b, ht_b), (h_a, ht_a))

    def x_tile(i, j):
        return (jnp.where((i == 0) & (j == 0), 0, jnp.minimum(i + 1, nt - 1)), 0)

    return pl.pallas_call(
        body, name="inproj_nat", grid=(nt, nj),
        in_specs=[pl.BlockSpec((tm, D), x_tile),
                  pl.BlockSpec((1, 3 * D), lambda i, j: (0, 0)),
                  pl.BlockSpec((1, D), lambda i, j: (0, 0)),
                  pl.BlockSpec((D, tn), lambda i, j: (0, j))],
        out_specs=[pl.BlockSpec((tm, tn), lambda i, j: (i, j)),
                   pl.BlockSpec((tm, D), lambda i, j: (i, 0)),
                   pl.BlockSpec((D, tm), lambda i, j: (0, i))],
        out_shape=[jax.ShapeDtypeStruct((S, N), BF16), jax.ShapeDtypeStruct((S, D), BF16),
                   jax.ShapeDtypeStruct((D, S), BF16)],
        scratch_shapes=[pltpu.VMEM((tm, D), BF16), pltpu.VMEM((D, tm), BF16),
                        pltpu.VMEM((tm, D), BF16), pltpu.VMEM((D, tm), BF16)],
        compiler_params=_params(),
    )(x, mod, norm_w, w)


def _inproj_qkv(a, b, shards, *, tm=1024, tn=1536):
    M, K = a.shape
    N = b.shape[1]
    n = len(shards)
    ni, nj = M // tm, N // tn

    def body(a_ref, b_ref, *rest):
        ins, o_ref, outs, sems = rest[:n], rest[n], rest[n + 1:2 * n + 1], rest[2 * n + 1:]
        i, j = pl.program_id(0), pl.program_id(1)
        if n:
            @pl.when((i == 0) & (j == 0))
            def _():
                for cp in _shard_gather_copies(ins, outs, *sems)[0]:
                    cp.start()

        o_ref[...] = jnp.dot(a_ref[...], b_ref[...], preferred_element_type=F32).astype(o_ref.dtype)

        if n:
            @pl.when((i == ni - 1) & (j == nj - 1))
            def _():
                send, recv, fwd, fwd_recv = _shard_gather_copies(ins, outs, *sems)
                for r, f in zip(recv, fwd):
                    r.wait_recv()
                    f.start()
                for r in fwd_recv:
                    r.wait_recv()
                for cp in send + fwd:
                    cp.wait_send()

    outs = pl.pallas_call(
        body, name="inproj_qkv", grid=(ni, nj),
        in_specs=[pl.BlockSpec((tm, K), lambda i, j: (i, 0)), pl.BlockSpec((K, tn), lambda i, j: (0, j))] + [ANY] * n,
        out_specs=[pl.BlockSpec((tm, tn), lambda i, j: (i, j))] + [ANY] * n,
        out_shape=[jax.ShapeDtypeStruct((M, N), BF16)] + [jax.ShapeDtypeStruct((4,) + s.shape, s.dtype) for s in shards],
        scratch_shapes=[pltpu.SemaphoreType.DMA((n, 6)), pltpu.SemaphoreType.DMA((n, 6))] if n else [],
        compiler_params=_params(),
    )(a, b, *shards)
    return outs[0], outs[1:]


def _matmul_acc(a, b, *, tn, tk, name, exchange=None):
    M, K = a.shape
    N = b.shape[1]
    nj, nk = N // tn, K // tk
    hosts = exchange is not None

    def body(a_ref, b_ref, *rest):
        if hosts:
            x_hbm, o_ref, r_hbm, acc_ref, ssem, rsem = rest
        else:
            o_ref, acc_ref = rest
        j, k = pl.program_id(0), pl.program_id(1)

        def to_sibling():
            mx, my, mc = _coords()
            return pltpu.make_async_remote_copy(src_ref=x_hbm.at[1 - mc], dst_ref=r_hbm, send_sem=ssem, recv_sem=rsem,
                                                device_id=(mx, my, 1 - mc), device_id_type=MESH)

        if hosts:
            @pl.when((j == 0) & (k == 0))
            def _():
                to_sibling().start()

        @pl.when(k == 0)
        def _():
            acc_ref[...] = jnp.zeros_like(acc_ref)

        acc_ref[...] += jnp.dot(a_ref[...], b_ref[...], preferred_element_type=F32)

        @pl.when(k == nk - 1)
        def _():
            o_ref[...] = acc_ref[...].astype(o_ref.dtype)

        if hosts:
            @pl.when((j == nj - 1) & (k == nk - 1))
            def _():
                cp = to_sibling()
                cp.wait_recv()
                cp.wait_send()

    prod = jax.ShapeDtypeStruct((M, N), BF16)
    out = pl.pallas_call(
        body, name=name, grid=(nj, nk),
        in_specs=[pl.BlockSpec((M, tk), lambda j, k: (0, k)), pl.BlockSpec((tk, tn), lambda j, k: (k, j))]
                 + ([ANY] if hosts else []),
        out_specs=[pl.BlockSpec((M, tn), lambda j, k: (0, j)), ANY] if hosts else pl.BlockSpec((M, tn), lambda j, k: (0, j)),
        out_shape=[prod, jax.ShapeDtypeStruct(exchange.shape[1:], exchange.dtype)] if hosts else prod,
        scratch_shapes=[pltpu.VMEM((M, tn), F32)]
                       + ([pltpu.SemaphoreType.DMA(()), pltpu.SemaphoreType.DMA(())] if hosts else []),
        compiler_params=_params(),
    )(a, b, *([exchange] if hosts else []))
    return out


SHARD_COLS = 2816


def _column_tables(part):
    if part == "nat":
        bw = 256
        orig = [j * bw if j < 16 else 8704 + (j - 16) * bw for j in range(NAT // bw)]
    else:
        bw = 128
        orig = []
        for jj in range(NQKV // bw):
            g, p, t = jj // 12, (jj % 12) // 3, jj % 3
            orig.append(4096 + t * 1536 + g * 512 + p * 128)
    tk = jnp.array([o // SHARD_COLS for o in orig], jnp.int32)
    tc = jnp.array([(o % SHARD_COLS) // bw for o in orig], jnp.int32)
    return tk, tc, bw


def _weights_prep(gath, own, chipv, part):
    D = D_MODEL
    tk, tc, bw = _column_tables(part)
    n = tk.shape[0]
    per_step = 2 if part == "nat" else 4
    assert n % per_step == 0

    def body(tk_ref, tc_ref, chip_ref, *refs):
        w_ref = refs[-1]
        for u in range(per_step):
            g_ref, o_ref = refs[2 * u], refs[2 * u + 1]
            mine = tk_ref[pl.program_id(0) * per_step + u] == chip_ref[0]
            w_ref[:, u * bw:(u + 1) * bw] = jnp.where(mine, o_ref[...], g_ref[...])

    def gath_spec(u):
        def idx(j, tk, tc, ch):
            k = tk[j * per_step + u]
            return (jnp.where(k == ch[0], (k + 1) % 4, k), 0, tc[j * per_step + u])
        return pl.BlockSpec((None, D, bw), idx)

    def own_spec(u):
        return pl.BlockSpec((D, bw), lambda j, tk, tc, ch: (0, tc[j * per_step + u]))

    specs = []
    for u in range(per_step):
        specs += [gath_spec(u), own_spec(u)]
    return pl.pallas_call(
        body, name="weights_prep_" + part,
        grid_spec=pltpu.PrefetchScalarGridSpec(
            num_scalar_prefetch=3, grid=(n // per_step,),
            in_specs=specs,
            out_specs=pl.BlockSpec((D, per_step * bw), lambda j, tk, tc, ch: (0, j))),
        out_shape=jax.ShapeDtypeStruct((D, n * bw), BF16),
        compiler_params=_params(),
    )(tk, tc, chipv, *([gath, own] * per_step))


def _pair_add_to_shards(core, gw, r, into, part):
    D = D_MODEL
    tk, tc, bw = _column_tables(part)
    n = tk.shape[0]

    def body(tk_ref, tc_ref, core_ref, g_ref, r_ref, into_ref, o_ref):
        o_ref[...] = (g_ref[...].astype(F32) + r_ref[...].astype(F32)).astype(BF16)

    return pl.pallas_call(
        body, name="grad_pair_add_" + part,
        grid_spec=pltpu.PrefetchScalarGridSpec(
            num_scalar_prefetch=3, grid=(n,),
            in_specs=[pl.BlockSpec((None, D // 2, bw), lambda j, tk, tc, c: (c[0], 0, j)),
                      pl.BlockSpec((D // 2, bw), lambda j, tk, tc, c: (0, j)), ANY],
            out_specs=pl.BlockSpec((None, D // 2, bw), lambda j, tk, tc, c: (tk[j], 0, tc[j]))),
        out_shape=jax.ShapeDtypeStruct(into.shape, BF16),
        input_output_aliases={5: 0},
        compiler_params=_params(),
    )(tk, tc, core, gw, r, into)


def _head_norm(xb, w, ebd):
    x = xb.astype(F32)
    ss = jnp.dot((x * x).astype(BF16), ebd, preferred_element_type=F32)
    return x * lax.rsqrt(ss * (1.0 / HEAD_DIM) + EPS) * w


def _window_mask(no_prev):
    qi = lax.broadcasted_iota(jnp.int32, (128, 256), 0)
    kc = lax.broadcasted_iota(jnp.int32, (128, 256), 1)
    ok = (kc >= qi) & (kc <= qi + 128)
    if no_prev is not None:
        ok = ok & ((kc >= 128) | jnp.logical_not(no_prev))
    return ok


def _lane_masks():
    lane = lax.broadcasted_iota(jnp.int32, (1, 128), 1)
    return [(lane < HEAD_DIM).astype(F32), (lane >= HEAD_DIM).astype(F32)]


def _head_col(v, mh):
    return jnp.sum(v * mh, axis=1, keepdims=True) * (1.0 / HEAD_DIM)


def _attn_fwd(projq, qnw2, knw2, ebd, g):
    S = projq.shape[0]
    d = DILATIONS[g]
    T = ATT_TILE
    nt = S // T
    nb = T // (128 * d)
    sdt = BF16 if d == 1 else F32

    def body(cur_ref, qw_ref, kw_ref, ebd_ref, o_ref, lse_ref, qs_s, ks_s, vs_s):
        t = pl.program_id(1)
        e = ebd_ref[...]

        @pl.when(t == 0)
        def _():
            ks_s[pl.ds(0, T), :] = jnp.zeros((T, 128), sdt)
            vs_s[pl.ds(0, T), :] = jnp.zeros((T, 128), sdt)

        @pl.when(t > 0)
        def _():
            ks_s[pl.ds(0, T), :] = ks_s[pl.ds(T, T), :]
            vs_s[pl.ds(0, T), :] = vs_s[pl.ds(T, T), :]

        qs_s[...] = (_head_norm(cur_ref[:, 0:128], qw_ref[...], e) * (QK_SCALE * LOG2E)).astype(sdt)
        ks_s[pl.ds(T, T), :] = _head_norm(cur_ref[:, 128:256], kw_ref[...], e).astype(sdt)
        vs_s[pl.ds(T, T), :] = cur_ref[:, 256:384].astype(sdt)
        m0, m1 = _lane_masks()
        first = t == 0
        for r in range(d):
            for bb in range(nb):
                q0 = r + bb * 128 * d
                k0 = T + r + (bb - 1) * 128 * d
                qs = qs_s[pl.ds(q0, 128, stride=d), :].astype(F32)
                kb = ks_s[pl.ds(k0, 256, stride=d), :].astype(BF16)
                vb = vs_s[pl.ds(k0, 256, stride=d), :].astype(BF16)
                ok = _window_mask(first if bb == 0 else None)
                ok2 = jnp.concatenate([ok, ok], axis=0)
                q2 = jnp.concatenate([qs * m0, qs * m1], axis=0).astype(BF16)
                s = lax.dot_general(q2, kb, (((1,), (1,)), ((), ())), preferred_element_type=F32)
                s = jnp.where(ok2, s, NEG_INF)
                mx = jnp.max(s, axis=-1, keepdims=True)
                p = jnp.exp2(s - mx)
                l = jnp.sum(p, axis=-1, keepdims=True)
                o = jnp.dot(p.astype(BF16), vb, preferred_element_type=F32) / l
                lse = mx * LN2 + jnp.log(l)
                o_ref[pl.ds(q0, 128, stride=d), :] = o[0:128] * m0 + o[128:256] * m1
                lse_ref[pl.ds(q0, 128, stride=d), :] = lse[0:128] * m0 + lse[128:256] * m1

    return pl.pallas_call(
        body, name=f"attn_fwd_g{g}", grid=(4, nt),
        in_specs=[pl.BlockSpec((T, 384), lambda p, t: (t, g * 4 + p)),
                  pl.BlockSpec((1, 128), lambda p, t: (0, 0)),
                  pl.BlockSpec((1, 128), lambda p, t: (0, 0)),
                  pl.BlockSpec((128, 128), lambda p, t: (0, 0))],
        out_specs=[pl.BlockSpec((T, 128), lambda p, t: (t, p)), pl.BlockSpec((T, 128), lambda p, t: (t, p))],
        out_shape=[jax.ShapeDtypeStruct((S, ATTN_OUT), F32), jax.ShapeDtypeStruct((S, ATTN_OUT), F32)],
        scratch_shapes=[pltpu.VMEM((T, 128), sdt), pltpu.VMEM((2 * T, 128), sdt), pltpu.VMEM((2 * T, 128), sdt)],
        compiler_params=_params(),
    )(projq, qnw2, knw2, ebd)


def _attn_bwd(projq, dprojq, d_o, lse, delta, qnw2, knw2, ebd, g):
    S = projq.shape[0]
    d = DILATIONS[g]
    T = ATT_TILE
    nt = S // T
    nb = T // (128 * d)
    sdt = F32

    def norm_bwd(raw_b, dy, w, e):
        x = raw_b.astype(F32)
        ss = jnp.dot((x * x).astype(BF16), e, preferred_element_type=F32)
        rstd = lax.rsqrt(ss * (1.0 / HEAD_DIM) + EPS)
        xh = x * rstd
        gw = jnp.sum(dy * xh, axis=0, keepdims=True)
        dxh = dy * w
        mean = jnp.dot((dxh * xh).astype(BF16), e, preferred_element_type=F32) * (1.0 / HEAD_DIM)
        return rstd * (dxh - xh * mean), gw

    nt_dims = (((1,), (1,)), ((), ()))
    tn_dims = (((0,), (0,)), ((), ()))

    def unit_stacked(qs, dos, ls, dls, kb, vb, ok, m0, m1):
        ok2 = jnp.concatenate([ok, ok], axis=0)
        q2 = jnp.concatenate([qs * m0, qs * m1], axis=0).astype(BF16)
        do2 = jnp.concatenate([dos * m0, dos * m1], axis=0).astype(BF16)
        lcol = jnp.concatenate([_head_col(ls, m0), _head_col(ls, m1)], axis=0)
        dcol = jnp.concatenate([_head_col(dls, m0), _head_col(dls, m1)], axis=0)
        s = jnp.where(ok2, lax.dot_general(q2, kb, nt_dims, preferred_element_type=F32), NEG_INF)
        p = jnp.exp2(s - lcol * LOG2E)
        dp = lax.dot_general(do2, vb, nt_dims, preferred_element_type=F32)
        ds = (p * (dp - dcol)).astype(BF16)
        dq2 = jnp.dot(ds, kb, preferred_element_type=F32)
        dk = lax.dot_general(ds, q2, tn_dims, preferred_element_type=F32)
        dv = lax.dot_general(p.astype(BF16), do2, tn_dims, preferred_element_type=F32)
        return dq2[0:128] * m0 + dq2[128:256] * m1, dk, dv

    def unit_per_head(qs, dos, ls, dls, kb, vb, ok, m0, m1):
        dq_t = dk_t = dv_t = None
        for mh in (m0, m1):
            qh = (qs * mh).astype(BF16)
            doh = (dos * mh).astype(BF16)
            s = jnp.where(ok, lax.dot_general(qh, kb, nt_dims, preferred_element_type=F32), NEG_INF)
            p = jnp.exp2(s - _head_col(ls, mh) * LOG2E)
            dp = lax.dot_general(doh, vb, nt_dims, preferred_element_type=F32)
            ds = (p * (dp - _head_col(dls, mh))).astype(BF16)
            dq = jnp.dot(ds, kb, preferred_element_type=F32) * mh
            dk = lax.dot_general(ds, qh, tn_dims, preferred_element_type=F32)
            dv = lax.dot_general(p.astype(BF16), doh, tn_dims, preferred_element_type=F32)
            dq_t = dq if dq_t is None else dq_t + dq
            dk_t = dk if dk_t is None else dk_t + dk
            dv_t = dv if dv_t is None else dv_t + dv
        return dq_t, dk_t, dv_t

    unit = unit_per_head if d == 1 else unit_stacked

    def body(cur_ref, prev_ref, do_ref, lse_ref, dl_ref, qw_ref, kw_ref, ebd_ref, dp_in_ref,
             out_ref, gqk_ref, qs_s, ks_s, vs_s, do_s, dq_cur, dq_prev, dk_acc, dv_acc):
        i = pl.program_id(1)
        e = ebd_ref[...]
        m0, m1 = _lane_masks()

        @pl.when(i == 0)
        def _():
            dk_acc[...] = jnp.zeros_like(dk_acc)
            dv_acc[...] = jnp.zeros_like(dv_acc)
            dq_prev[...] = jnp.zeros_like(dq_prev)
            gqk_ref[...] = jnp.zeros_like(gqk_ref)

        @pl.when(i < nt)
        def _():
            qs_s[...] = _head_norm(cur_ref[:, 0:128], qw_ref[...], e) * (QK_SCALE * LOG2E)
            ks_s[pl.ds(0, T), :] = _head_norm(prev_ref[:, 128:256], kw_ref[...], e)
            ks_s[pl.ds(T, T), :] = _head_norm(cur_ref[:, 128:256], kw_ref[...], e)
            vs_s[pl.ds(0, T), :] = prev_ref[:, 256:384].astype(F32)
            vs_s[pl.ds(T, T), :] = cur_ref[:, 256:384].astype(F32)
            do_s[...] = do_ref[...].astype(F32)
            first = i == 0
            for r in range(d):
                own_k = own_v = None
                for bb in range(nb):
                    q0 = r + bb * 128 * d
                    k0 = T + r + (bb - 1) * 128 * d
                    qs = qs_s[pl.ds(q0, 128, stride=d), :]
                    dos = do_s[pl.ds(q0, 128, stride=d), :]
                    ls = lse_ref[pl.ds(q0, 128, stride=d), :]
                    dls = dl_ref[pl.ds(q0, 128, stride=d), :]
                    kb = ks_s[pl.ds(k0, 256, stride=d), :].astype(BF16)
                    vb = vs_s[pl.ds(k0, 256, stride=d), :].astype(BF16)
                    ok = _window_mask(first if bb == 0 else None)
                    dq_t, dk_t, dv_t = unit(qs, dos, ls, dls, kb, vb, ok, m0, m1)
                    dq_cur[pl.ds(q0, 128, stride=d), :] = dq_t
                    before = pl.ds(k0, 128, stride=d)
                    if bb == 0:
                        dk_acc[before, :] = dk_acc[before, :] + dk_t[0:128]
                        dv_acc[before, :] = dv_acc[before, :] + dv_t[0:128]
                    else:
                        dk_acc[before, :] = own_k + dk_t[0:128]
                        dv_acc[before, :] = own_v + dv_t[0:128]
                    own_k, own_v = dk_t[128:256], dv_t[128:256]
                tail = pl.ds(T + r + (nb - 1) * 128 * d, 128, stride=d)
                dk_acc[tail, :] = own_k
                dv_acc[tail, :] = own_v

        @pl.when(i >= 1)
        def _():
            dq_raw, gq = norm_bwd(prev_ref[:, 0:128], dq_prev[...] * QK_SCALE, qw_ref[...], e)
            dk_raw, gk = norm_bwd(prev_ref[:, 128:256], dk_acc[pl.ds(0, T), :] * LN2, kw_ref[...], e)
            out_ref[:, 0:128] = dq_raw.astype(BF16)
            out_ref[:, 128:256] = dk_raw.astype(BF16)
            out_ref[:, 256:384] = dv_acc[pl.ds(0, T), :].astype(BF16)
            gqk_ref[0:1, :] += gq
            gqk_ref[1:2, :] += gk

        dq_prev[...] = dq_cur[...]
        dk_acc[pl.ds(0, T), :] = dk_acc[pl.ds(T, T), :]
        dv_acc[pl.ds(0, T), :] = dv_acc[pl.ds(T, T), :]

    last = nt - 1
    qtile = lambda p, i: (jnp.minimum(i, last), p)
    return pl.pallas_call(
        body, name=f"attn_bwd_g{g}", grid=(4, nt + 1),
        in_specs=[pl.BlockSpec((T, 384), lambda p, i: (jnp.minimum(i, last), g * 4 + p)),
                  pl.BlockSpec((T, 384), lambda p, i: (jnp.maximum(i - 1, 0), g * 4 + p)),
                  pl.BlockSpec((T, 128), qtile), pl.BlockSpec((T, 128), qtile), pl.BlockSpec((T, 128), qtile),
                  pl.BlockSpec((1, 128), lambda p, i: (0, 0)),
                  pl.BlockSpec((1, 128), lambda p, i: (0, 0)),
                  pl.BlockSpec((128, 128), lambda p, i: (0, 0)),
                  ANY],
        out_specs=[pl.BlockSpec((T, 384), lambda p, i: (jnp.maximum(i - 1, 0), g * 4 + p)),
                   pl.BlockSpec((None, 8, 128), lambda p, i: (p, 0, 0))],
        out_shape=[jax.ShapeDtypeStruct(dprojq.shape, BF16), jax.ShapeDtypeStruct((4, 8, 128), F32)],
        scratch_shapes=[pltpu.VMEM((T, 128), sdt), pltpu.VMEM((2 * T, 128), sdt), pltpu.VMEM((2 * T, 128), sdt),
                        pltpu.VMEM((T, 128), sdt), pltpu.VMEM((T, 128), F32), pltpu.VMEM((T, 128), F32),
                        pltpu.VMEM((2 * T, 128), F32), pltpu.VMEM((2 * T, 128), F32)],
        input_output_aliases={8: 0},
        compiler_params=_params(),
    )(projq, projq, d_o, lse, delta, qnw2, knw2, ebd, dprojq)


def _mid(projn, o_g, lse_g, x, tgt, mod, convw8, wc, wa, wo, ebd, *, tm=256):
    S, D = x.shape
    nt = S // tm
    nst = max(1, 256 // tm)
    assert nt % nst == 0
    hb = tm // 16

    def body(pn_ref, hc_ref, hx_ref, o0_ref, o1_ref, o2_ref, l0_ref, l1_ref, l2_ref, x_ref, t_ref, mod_ref,
             cw_ref, ebd_ref, wc_hbm, wa_hbm, wo_hbm,
             dpn_ref, do_ref, lse_ref, dl_ref, dout_ref, vacc_ref, gwo_hbm, gwc_hbm, gwa_hbm,
             wc_s, wa_s, wo_s, gwo_s, gwc_s, gwa_s, carry_s, *stashes):
        i = pl.program_id(0)
        ti = nt - 1 - i
        par = i % nst
        stash = pl.ds(pl.multiple_of(par * tm, tm), tm)

        @pl.when(i == 0)
        def _():
            for src, dst in ((wc_hbm, wc_s), (wa_hbm, wa_s), (wo_hbm, wo_s)):
                pltpu.sync_copy(src, dst)
            gwo_s[...] = jnp.zeros_like(gwo_s)
            gwc_s[...] = jnp.zeros_like(gwc_s)
            gwa_s[...] = jnp.zeros_like(gwa_s)
            carry_s[...] = jnp.zeros_like(carry_s)
            vacc_ref[...] = jnp.zeros_like(vacc_ref)

        rows = lax.broadcasted_iota(jnp.int32, (tm, D), 0)
        w0, w1, w2 = cw_ref[0:1, :], cw_ref[1:2, :], cw_ref[2:3, :]
        gate = mod_ref[:, 2 * D:3 * D]

        b_a = pn_ref[:, BA:BA + D].astype(F32)
        c_a = pn_ref[:, CA:CA + D].astype(F32)
        x_a = pn_ref[:, XA:XA + D].astype(F32)
        z_a = pn_ref[:, ZA:ZA + D].astype(F32)
        u = c_a * x_a
        has_prev = (ti > 0).astype(F32)
        uh = hc_ref[...].astype(F32) * hx_ref[...].astype(F32) * has_prev
        h14, h15 = _row(uh, 14), _row(uh, 15)
        u_m1 = jnp.where(rows == 0, h15, pltpu.roll(u, 1, 0))
        u_m2 = jnp.where(rows == 0, h14, jnp.where(rows == 1, h15, pltpu.roll(u, 2, 0)))
        conv = w0 * u_m2 + w1 * u_m1 + w2 * u
        sg_a = _sigmoid(z_a)
        sz_a = z_a * sg_a
        y_a = b_a * conv * sz_a
        y_ab = y_a.astype(BF16)
        pa = jnp.dot(y_ab, wc_s[...], preferred_element_type=F32)

        l0, l1, l2 = l0_ref[...], l1_ref[...], l2_ref[...]
        mxl = jnp.maximum(jnp.maximum(l0, l1), l2)
        e0, e1, e2 = jnp.exp(l0 - mxl), jnp.exp(l1 - mxl), jnp.exp(l2 - mxl)
        den = e0 + e1 + e2
        attn = (e0 * o0_ref[...] + e1 * o1_ref[...] + e2 * o2_ref[...]) / den
        lse_ref[...] = mxl + jnp.log(den)
        z_b = pn_ref[:, ZB:ZB + ATTN_OUT].astype(F32)
        sg_b = _sigmoid(z_b)
        sz_b = z_b * sg_b
        y_b = attn * sz_b
        y_bb = y_b.astype(BF16)
        pb = jnp.dot(y_bb, wa_s[...], preferred_element_type=F32)

        s_a = _sigmoid(pn_ref[:, GA:GA + D].astype(F32))
        s_b = _sigmoid(pn_ref[:, GB:GB + D].astype(F32))
        merged = s_a * pa + s_b * pb
        merged_b = merged.astype(BF16)
        mo = jnp.dot(merged_b, wo_s[...], preferred_element_type=F32)
        diff = x_ref[...] + gate * mo - t_ref[...]
        dout = diff * (1.0 / D)
        dout_ref[...] = dout.astype(BF16)
        vacc_ref[4:5, :] += jnp.sum(diff * diff, axis=0, keepdims=True)
        vacc_ref[0:1, :] += jnp.sum(dout * mo, axis=0, keepdims=True)

        d_mo = (dout * gate).astype(BF16)
        nt_dims = (((1,), (1,)), ((), ()))
        dmerged = lax.dot_general(d_mo, wo_s[...], nt_dims, preferred_element_type=F32)
        dpa = (dmerged * s_a).astype(BF16)
        dpb = (dmerged * s_b).astype(BF16)
        dpn_ref[:, GA:GA + D] = (dmerged * pa * s_a * (1.0 - s_a)).astype(BF16)
        dpn_ref[:, GB:GB + D] = (dmerged * pb * s_b * (1.0 - s_b)).astype(BF16)
        tn = (((0,), (0,)), ((), ()))
        if nst == 1:
            gwo_s[...] += lax.dot_general(merged_b, d_mo, tn, preferred_element_type=F32)
            gwc_s[...] += lax.dot_general(y_ab, dpa, tn, preferred_element_type=F32)
            gwa_s[...] += lax.dot_general(y_bb, dpb, tn, preferred_element_type=F32)
        else:
            st_m, st_d, st_ya, st_pa, st_yb, st_pb = stashes
            st_m[stash, :] = merged_b
            st_d[stash, :] = d_mo
            st_ya[stash, :] = y_ab
            st_pa[stash, :] = dpa
            st_yb[stash, :] = y_bb
            st_pb[stash, :] = dpb

            @pl.when(par == nst - 1)
            def _():
                gwo_s[...] += lax.dot_general(st_m[...], st_d[...], tn, preferred_element_type=F32)
                gwc_s[...] += lax.dot_general(st_ya[...], st_pa[...], tn, preferred_element_type=F32)
                gwa_s[...] += lax.dot_general(st_yb[...], st_pb[...], tn, preferred_element_type=F32)

        dy_a = lax.dot_general(dpa, wc_s[...], nt_dims, preferred_element_type=F32)
        dy_b = lax.dot_general(dpb, wa_s[...], nt_dims, preferred_element_type=F32)

        dattn = dy_b * sz_b
        dpn_ref[:, ZB:ZB + ATTN_OUT] = (dy_b * attn * sg_b * (1.0 + z_b * (1.0 - sg_b))).astype(BF16)
        do_ref[...] = dattn.astype(BF16)
        dl_ref[...] = jnp.dot((dattn * attn).astype(BF16), ebd_ref[...], preferred_element_type=F32)

        dpn_ref[:, BA:BA + D] = (dy_a * conv * sz_a).astype(BF16)
        dpn_ref[:, ZA:ZA + D] = (dy_a * b_a * conv * sg_a * (1.0 + z_a * (1.0 - sg_a))).astype(BF16)
        dconv = dy_a * b_a * sz_a
        vacc_ref[1:2, :] += jnp.sum(dconv * u_m2, axis=0, keepdims=True)
        vacc_ref[2:3, :] += jnp.sum(dconv * u_m1, axis=0, keepdims=True)
        vacc_ref[3:4, :] += jnp.sum(dconv * u, axis=0, keepdims=True)
        nxt = carry_s[...]
        n0, n1 = _row(nxt, 0), _row(nxt, 1)
        dc_p1 = jnp.where(rows == tm - 1, n0, pltpu.roll(dconv, tm - 1, 0))
        dc_p2 = jnp.where(rows == tm - 1, n1, jnp.where(rows == tm - 2, n0, pltpu.roll(dconv, tm - 2, 0)))
        du = w2 * dconv + w1 * dc_p1 + w0 * dc_p2
        carry_s[...] = dconv[0:8, :]
        dpn_ref[:, CA:CA + D] = (du * x_a).astype(BF16)
        dpn_ref[:, XA:XA + D] = (du * c_a).astype(BF16)

        @pl.when(i == nt - 1)
        def _():
            pltpu.sync_copy(gwo_s, gwo_hbm)
            pltpu.sync_copy(gwc_s, gwc_hbm)
            pltpu.sync_copy(gwa_s, gwa_hbm)

    rev = lambda i: (nt - 1 - i, 0)
    halo = lambda col: pl.BlockSpec((16, D), lambda i: (jnp.maximum((nt - 1 - i) * hb - 1, 0), col))
    tile512 = pl.BlockSpec((tm, ATTN_OUT), rev)
    tileD = pl.BlockSpec((tm, D), rev)
    const = lambda shape: pl.BlockSpec(shape, lambda i: (0, 0))
    return pl.pallas_call(
        body, name="mid_fwd_bwd", grid=(nt,),
        in_specs=[pl.BlockSpec((tm, NAT), rev), halo(CA // D), halo(XA // D)] + [tile512] * 6
                 + [tileD, tileD, const((1, 3 * D)), const((8, D)), const((ATTN_OUT, ATTN_OUT))] + [ANY] * 3,
        out_specs=[pl.BlockSpec((tm, NAT), rev), tile512, tile512, tile512, tileD, const((8, D)), ANY, ANY, ANY],
        out_shape=[jax.ShapeDtypeStruct((S, NAT), BF16), jax.ShapeDtypeStruct((S, ATTN_OUT), BF16),
                   jax.ShapeDtypeStruct((S, ATTN_OUT), F32), jax.ShapeDtypeStruct((S, ATTN_OUT), F32),
                   jax.ShapeDtypeStruct((S, D), BF16), jax.ShapeDtypeStruct((8, D), F32),
                   jax.ShapeDtypeStruct((D, D), F32), jax.ShapeDtypeStruct((D, D), F32),
                   jax.ShapeDtypeStruct((ATTN_OUT, D), F32)],
        scratch_shapes=[pltpu.VMEM((D, D), BF16), pltpu.VMEM((ATTN_OUT, D), BF16), pltpu.VMEM((D, D), BF16),
                        pltpu.VMEM((D, D), F32), pltpu.VMEM((D, D), F32), pltpu.VMEM((ATTN_OUT, D), F32),
                        pltpu.VMEM((8, D), F32)]
                       + ([pltpu.VMEM((nst * tm, D), BF16)] * 4
                          + [pltpu.VMEM((nst * tm, ATTN_OUT), BF16), pltpu.VMEM((nst * tm, D), BF16)] if nst > 1 else []),
        compiler_params=_params(vmem=V7X_VMEM_LIMIT_RESIDENT),
    )(projn, projn, projn, *o_g, *lse_g, x, tgt, mod, convw8, ebd, wc, wa, wo)


def _dh_matmul(dpn, dpq, w_nat, w_qkv, parts, *, tm=512):
    S = dpn.shape[0]
    D = D_MODEL
    n = len(parts)
    nt = S // tm

    def body(dn_ref, dq_ref, wn_hbm, wq_hbm, *rest):
        p_hbm, rest = rest[:n], rest[n:]
        dh_ref = rest[0]
        q_hbm = rest[1:1 + n]
        wn_s, wq_s = rest[1 + n:3 + n]
        sems = rest[3 + n:]

        @pl.when(pl.program_id(0) == 0)
        def _():
            if n:
                mine, _ = _scatter_copies(p_hbm, q_hbm, *sems)
                for cp in mine:
                    cp.start()
            pltpu.sync_copy(wn_hbm, wn_s)
            pltpu.sync_copy(wq_hbm, wq_s)

        if n:
            @pl.when(pl.program_id(0) == nt - 1)
            def _():
                mine, theirs = _scatter_copies(p_hbm, q_hbm, *sems)
                for cp in theirs:
                    cp.wait_recv()
                for cp in mine:
                    cp.wait_send()

        nt_dims = (((1,), (1,)), ((), ()))
        dh = (lax.dot_general(dn_ref[...], wn_s[...], nt_dims, preferred_element_type=F32)
              + lax.dot_general(dq_ref[...], wq_s[...], nt_dims, preferred_element_type=F32))
        dh_ref[...] = dh.astype(BF16)

    row = lambda w: pl.BlockSpec((tm, w), lambda i: (i, 0))
    outs = pl.pallas_call(
        body, name="dh_matmul", grid=(nt,),
        in_specs=[row(NAT), row(NQKV), ANY, ANY] + [ANY] * n,
        out_specs=[row(D)] + [ANY] * n,
        out_shape=[jax.ShapeDtypeStruct((S, D), BF16)] + [jax.ShapeDtypeStruct(p.shape, p.dtype) for p in parts],
        scratch_shapes=[pltpu.VMEM((D, NAT), BF16), pltpu.VMEM((D, NQKV), BF16)]
                       + ([pltpu.SemaphoreType.DMA((n, 3)), pltpu.SemaphoreType.DMA((n, 3))] if n else []),
        compiler_params=_params(vmem=V7X_VMEM_LIMIT_RESIDENT),
    )(dpn, dpq, w_nat, w_qkv, *parts)
    return outs[0], outs[1:]


def _norm_bwd(dh, x, dout, mod, norm_w, *, tm=512):
    S, D = x.shape

    def body(dh_ref, x_ref, dout_ref, mod_ref, nw_ref, gx_ref, st_ref):
        @pl.when(pl.program_id(0) == 0)
        def _():
            st_ref[...] = jnp.zeros_like(st_ref)

        dh = dh_ref[...].astype(F32)
        scale1 = 1.0 + mod_ref[:, D:2 * D]
        nw = nw_ref[...]
        xv = x_ref[...]
        rstd = lax.rsqrt(jnp.mean(xv * xv, axis=-1, keepdims=True) + EPS)
        xh = xv * rstd
        dhs = dh * scale1
        dxh = dhs * nw
        dx = rstd * (dxh - xh * jnp.mean(dxh * xh, axis=-1, keepdims=True))
        gx_ref[...] = dout_ref[...].astype(F32) + dx
        st_ref[0:1, :] += jnp.sum(dh, axis=0, keepdims=True)
        st_ref[1:2, :] += jnp.sum(dh * (xh * nw), axis=0, keepdims=True)
        st_ref[2:3, :] += jnp.sum(dhs * xh, axis=0, keepdims=True)

    row = pl.BlockSpec((tm, D), lambda i: (i, 0))
    return pl.pallas_call(
        body, name="norm_bwd", grid=(S // tm,),
        in_specs=[row, row, row, pl.BlockSpec((1, 3 * D), lambda i: (0, 0)), pl.BlockSpec((1, D), lambda i: (0, 0))],
        out_specs=[row, pl.BlockSpec((8, D), lambda i: (0, 0))],
        out_shape=[jax.ShapeDtypeStruct((S, D), F32), jax.ShapeDtypeStruct((8, D), F32)],
        compiler_params=_params(),
    )(dh, x, dout, mod, norm_w)


def _pack_smalls(st, vacc, gqk):
    D = D_MODEL

    def body(st_ref, va_ref, gqk_ref, o_ref):
        o_ref[...] = jnp.zeros_like(o_ref)
        o_ref[0:2, :] = st_ref[0:2, :]
        o_ref[2:3, :] = va_ref[0:1, :]
        o_ref[8:9, :] = st_ref[2:3, :]
        o_ref[16:19, :] = va_ref[1:4, :]
        tot = gqk_ref[0]
        for k in range(1, gqk_ref.shape[0]):
            tot = tot + gqk_ref[k]
        tot = tot + pltpu.roll(tot, HEAD_DIM, 1)
        o_ref[24:32, 0:128] = tot
        loss = jnp.sum(va_ref[4:5, :], axis=1, keepdims=True) * (0.5 / D)
        o_ref[26:27, :] = jnp.broadcast_to(loss, (1, D))

    return pl.pallas_call(
        body, name="pack_smalls", out_shape=jax.ShapeDtypeStruct((32, D), F32),
        in_specs=[pl.BlockSpec(memory_space=pltpu.VMEM)] * 3,
        out_specs=pl.BlockSpec(memory_space=pltpu.VMEM), compiler_params=_params(),
    )(st, vacc, gqk)


def _small_update(sm_loc, dm_pad, ct_pad, ada, vecs):
    D = D_MODEL
    nv = len(vecs)
    places = [(0, 3, D), (8, 1, D), (16, 3, 256), (24, 1, HEAD_DIM), (25, 1, HEAD_DIM)]

    def body(*refs):
        sm_ref, dm_ref, ct_ref = refs[0:3]
        wmv = [refs[3 + 3 * t:6 + 3 * t] for t in range(nv + 1)]
        outs = refs[3 + 3 * (nv + 1):]
        res = [outs[4 * t:4 * t + 4] for t in range(nv + 1)]
        loss_ref, tot_s = outs[4 * (nv + 1)], outs[4 * (nv + 1) + 1]

        def update(g, w, m, v, out, at=(slice(None), slice(None))):
            d, mn, vn = _adamw_math(w[at], g, m[at], v[at])
            for ref, val in zip(out, (g, d, mn, vn)):
                ref[at] = val

        gw = jnp.dot(ct_ref[...], dm_ref[...], preferred_element_type=F32, precision=lax.Precision.HIGHEST)
        update(gw, *wmv[0], res[0])
        tot = sm_ref[0]
        for k in range(1, 8):
            tot = tot + sm_ref[k]
        tot_s[...] = tot
        for k in range(3):
            update(tot_s[k:k + 1, :], *wmv[1], res[1], at=(slice(None), slice(k * D, (k + 1) * D)))
        for t in range(1, nv):
            r0, nr, nl = places[t]
            update(tot_s[r0:r0 + nr, 0:nl], *wmv[1 + t], res[1 + t])
        loss_ref[...] = jnp.broadcast_to(tot_s[26:27, 0:128], (8, 128))

    vm = pl.BlockSpec(memory_space=pltpu.VMEM)
    groups = [ada] + list(vecs)
    out_shape = []
    for w, _, _ in groups:
        out_shape += [jax.ShapeDtypeStruct(w.shape, F32)] * 4
    out_shape.append(jax.ShapeDtypeStruct((8, 128), F32))
    flat = [a for grp in groups for a in grp]
    outs = pl.pallas_call(
        body, name="small_update", out_shape=out_shape,
        in_specs=[vm] * (3 + len(flat)), out_specs=[vm] * len(out_shape),
        scratch_shapes=[pltpu.VMEM((32, D), F32)], compiler_params=_params(),
    )(sm_loc, dm_pad, ct_pad, *flat)
    return [outs[4 * t:4 * t + 4] for t in range(nv + 1)], outs[-1]


def _later_weights(gathered, shards, chip):
    D = D_MODEL
    g_bc, g_ba, g_wo = [lax.dynamic_update_slice(g, s[None], (chip, 0, 0, 0)) for g, s in zip(gathered, shards)]
    wa = g_ba.reshape(4, ATTN_OUT, 256).transpose(1, 0, 2).reshape(ATTN_OUT, D)
    return g_bc.reshape(D, D), wa, g_wo.reshape(D, D)


def _local_step(xs, tg, mod, norm_w, w_nat, w_qkv, later, convw8, q_norm_w, k_norm_w):
    S = xs.shape[0]
    lane = jnp.arange(128)
    ebd128 = (lane[:, None] // HEAD_DIM == lane[None, :] // HEAD_DIM).astype(BF16)
    lane5 = jnp.arange(ATTN_OUT)
    ebd512 = (lane5[:, None] // HEAD_DIM == lane5[None, :] // HEAD_DIM).astype(BF16)
    qnw2 = jnp.tile(q_norm_w, (1, 2))
    knw2 = jnp.tile(k_norm_w, (1, 2))

    projn, h, ht = _inproj_nat(xs, mod, norm_w, w_nat)
    if len(later) == 2:
        projq, gathered = _inproj_qkv(h, w_qkv, later[0])
        wc, wa, wo = _later_weights(gathered, *later)
    else:
        projq, _ = _inproj_qkv(h, w_qkv, [])
        wc, wa, wo = later
    o_g, lse_g = [], []
    for g in range(N_GROUPS):
        o, l = _attn_fwd(projq, qnw2, knw2, ebd128, g)
        o_g.append(o)
        lse_g.append(l)

    dpn, d_o, lse, delta, dout, vacc, gwo, gwc, gwa = _mid(
        projn, o_g, lse_g, xs, tg, mod, convw8, wc, wa, wo, ebd512)
    dpq = lax.empty((S, NQKV), BF16)
    gqk = []
    for g in range(N_GROUPS):
        dpq, part = _attn_bwd(projq, dpq, d_o, lse, delta, qnw2, knw2, ebd128, g)
        gqk.append(part)
    return dpn, dpq, dout, ht, gwc, gwa, gwo, vacc, gqk


def kernel(x, c, w_ada, b_ada, norm_w, w_in, conv_w, q_norm_w, k_norm_w, w_br_conv, w_br_attn, w_out, loss_target, m_w_ada, m_b_ada, m_norm_w, m_w_in, m_conv_w, m_q_norm_w, m_k_norm_w, m_w_br_conv, m_w_br_attn, m_w_out, v_w_ada, v_b_ada, v_norm_w, v_w_in, v_conv_w, v_q_norm_w, v_k_norm_w, v_w_br_conv, v_w_br_attn, v_w_out):
    D = D_MODEL
    S = x.shape[1]
    xs, tg = x[0], loss_target[0]
    mx, my, mc = lax.axis_index("x"), lax.axis_index("y"), lax.axis_index("c")
    chip = 2 * mx + my
    dev = 2 * chip + mc

    c_blk = jnp.concatenate([c, jnp.pad(conv_w[0], ((0, 0), (0, D - 256))), jnp.zeros((4, D), F32)], axis=0)
    b_shard = lax.dynamic_slice(b_ada, (0, chip * 768), (1, 768))
    shards = [w_in[0].astype(BF16).reshape(2, 512, 2816), w_br_conv[0].astype(BF16).reshape(2, 128, D),
              w_br_attn[0].astype(BF16).reshape(2, 256, 256), w_out[0].astype(BF16).reshape(2, 128, D)]
    gathered_in, _, cg, mod_parts, silu_c = _weights_allgather(
        shards[0].reshape(2, 2, 256, SHARD_COLS), [], c_blk, w_ada[0], b_shard)
    convw8 = jnp.pad(cg[::2, 1:4, 0:256].transpose(1, 0, 2).reshape(3, D), ((0, 5), (0, 0)))
    mod_all = mod_parts.transpose(1, 0, 2).reshape(8, 3 * D)
    mod = lax.dynamic_slice(mod_all, (dev, 0), (1, 3 * D))
    chipv = jnp.reshape(chip, (1,)).astype(jnp.int32)
    g_in = gathered_in.reshape(4, D, SHARD_COLS)
    own_in = shards[0].reshape(D, SHARD_COLS)
    w_nat = _weights_prep(g_in, own_in, chipv, "nat")
    w_qkv = _weights_prep(g_in, own_in, chipv, "qkv")

    dpn, dpq, dout, ht, gwc, gwa, gwo, vacc, gqk = _local_step(
        xs, tg, mod, norm_w, w_nat, w_qkv, (shards[1:], chip), convw8, q_norm_w, k_norm_w)

    def halves(a):
        k, r, cc = a.shape
        return a.reshape(k, 2, r // 2, cc).transpose(1, 0, 2, 3)

    gw_nat = _matmul_acc(ht, dpn, tn=1664, tk=2048, name="grad_w_in_nat").reshape(2, D // 2, NAT)
    gw_qkv, recv_nat = _matmul_acc(ht, dpq, tn=1536, tk=2048, name="grad_w_in_qkv", exchange=gw_nat)
    grads = [gw_nat, gw_qkv.reshape(2, D // 2, NQKV),
             halves(gwc.astype(BF16).reshape(4, 256, D)),
             halves(gwa.astype(BF16).reshape(ATTN_OUT, 4, 256).transpose(1, 0, 2)),
             halves(gwo.astype(BF16).reshape(4, 256, D))]
    recv = [recv_nat] + list(_pair_exchange(grads[1:]))
    core = jnp.reshape(mc, (1,)).astype(jnp.int32)
    p_in = _pair_add_to_shards(core, grads[0], recv[0], lax.empty((4, D // 2, SHARD_COLS), BF16), "nat")
    p_in = _pair_add_to_shards(core, grads[1], recv[1], p_in, "qkv")
    parts = [p_in] + [_pair_add(core, gr, rc) for gr, rc in zip(grads[2:], recv[2:])]
    dh, gathered = _dh_matmul(dpn, dpq, w_nat, w_qkv, parts)
    grad_x, st = _norm_bwd(dh, xs, dout, mod, norm_w)
    mine = [_sum4(chipv, p, q) for p, q in zip(parts, gathered)]
    theirs, sm_all = _half_exchange(mine, _pack_smalls(st, vacc, jnp.concatenate(gqk, axis=0)))

    big = {}
    for t, (nm, w, m, v) in enumerate((("w_in", w_in, m_w_in, v_w_in), ("w_br_conv", w_br_conv, m_w_br_conv, v_w_br_conv),
                                       ("w_br_attn", w_br_attn, m_w_br_attn, v_w_br_attn), ("w_out", w_out, m_w_out, v_w_out))):
        big[nm] = _adamw_halves(core, w[0], mine[t], theirs[t], m[0], v[0], "adamw_" + nm)

    dmod_all = sm_all[:, 0:3, :].reshape(8, 3 * D)
    dm_pad = jnp.pad(lax.dynamic_slice(dmod_all, (0, chip * 768), (8, 768)), ((0, 120), (0, 0)))
    conv_dev = lax.dynamic_slice(sm_all[:, 16:24, :], (0, 0, chip * 256), (8, 8, 256))
    sm_loc = jnp.concatenate([sm_all[:, 0:16, :], jnp.pad(conv_dev, ((0, 0), (0, 0), (0, D - 256))), sm_all[:, 24:32, :]], axis=1)

    ct_pad = jnp.pad(silu_c.T, ((0, 0), (0, 120)))
    small_names = ["b_ada", "norm_w", "conv_w", "q_norm_w", "k_norm_w"]
    vecs = [(b_ada, m_b_ada, v_b_ada), (norm_w, m_norm_w, v_norm_w), (conv_w[0], m_conv_w[0], v_conv_w[0]),
            (q_norm_w, m_q_norm_w, v_q_norm_w), (k_norm_w, m_k_norm_w, v_k_norm_w)]
    updated, loss_blk = _small_update(sm_loc, dm_pad, ct_pad, (w_ada[0], m_w_ada[0], v_w_ada[0]), vecs)
    small = {"w_ada": [a[None] for a in updated[0]]}
    for nm, four in zip(small_names, updated[1:]):
        small[nm] = [a[None] for a in four] if nm == "conv_w" else list(four)

    order = ["w_ada", "b_ada", "norm_w", "w_in", "conv_w", "q_norm_w", "k_norm_w", "w_br_conv", "w_br_attn", "w_out"]
    res = [loss_blk[0, 0], grad_x[None]]
    for kind in range(4):
        for nm in order:
            res.append(big[nm][kind][None] if nm in big else small[nm][kind])
    return tuple(res)
```

```python
import jax
import jax.numpy as jnp
from jax import lax
from jax.experimental import pallas as pl
from jax.experimental.pallas import tpu as pltpu

F32 = jnp.float32
BF16 = jnp.bfloat16
MESH = pl.DeviceIdType.MESH
ANY = pl.BlockSpec(memory_space=pl.ANY)

D_MODEL = 1024
HEAD_DIM = 64
N_GROUPS = 3
DILATIONS = (1, 4, 16)
ATTN_OUT = 512
EPS = 1e-6
NEG_INF = -1e30
NAT = 6656
NQKV = 4608
BA, CA, XA, ZA, ZB, GA, GB = 0, 1024, 2048, 3072, 4096, 4608, 5632
ATT_TILE = 2048
QK_SCALE = HEAD_DIM ** -0.5
LOG2E = 1.4426950408889634
LN2 = 0.6931471805599453
V7X_VMEM_LIMIT = 56 * 1024 * 1024
V7X_VMEM_LIMIT_RESIDENT = 60 * 1024 * 1024

ADAM_LR = 0.001
ADAM_B1 = 0.9
ADAM_B2 = 0.999
ADAM_EPS = 1e-08
ADAM_WD = 0.01
ADAM_STEP = 10


def _params(vmem=V7X_VMEM_LIMIT, **kw):
    return pltpu.CompilerParams(vmem_limit_bytes=vmem, **kw)


def _sigmoid(z):
    return 1.0 / (1.0 + jnp.exp(-z))


def _row(v, r):
    rows = lax.broadcasted_iota(jnp.int32, v.shape, 0)
    return jnp.sum(jnp.where(rows == r, v, 0.0), axis=0, keepdims=True)


def _coords():
    return lax.axis_index("x"), lax.axis_index("y"), lax.axis_index("c")


def _flip(v, bit):
    return 1 - v if bit else v


def _weights_allgather(big_shard, shards, c_blk, w_ada, b_shard):
    n = len(shards)
    D = D_MODEL
    cols = w_ada.shape[1]

    def body(*refs):
        big_in, ins = refs[0], refs[1:1 + n]
        c_ref, wada_ref, b_ref = refs[1 + n:4 + n]
        big_out, outs = refs[4 + n], refs[5 + n:5 + 2 * n]
        cg_ref, mp_ref, sc_ref = refs[5 + 2 * n:8 + 2 * n]
        sem_refs = refs[8 + 2 * n:]
        ssem, rsem = sem_refs[:2] if n else (None, None)
        bs, br, cs, cr, ms, mr, lsem = sem_refs[2 if n else 0:]
        mx, my, mc = _coords()
        me = 2 * mx + my
        me8 = 2 * me + mc
        sib = (mx, my, 1 - mc)
        chips = [(_flip(mx, j & 2), _flip(my, j & 1)) for j in range(1, 4)]
        sends = []

        local = pltpu.make_async_copy(c_ref, cg_ref.at[me8], lsem)
        local.start()
        for j in range(1, 8):
            peer = (_flip(mx, j & 4), _flip(my, j & 2), _flip(mc, j & 1))
            cp = pltpu.make_async_remote_copy(
                src_ref=c_ref, dst_ref=cg_ref.at[me8], send_sem=cs.at[j - 1], recv_sem=cr.at[j - 1],
                device_id=peer, device_id_type=MESH)
            cp.start()
            sends.append(cp)

        kx, ky, kd = 2 * (1 - mx) + my, 2 * mx + (1 - my), 2 * (1 - mx) + (1 - my)
        xn, yn = (1 - mx, my, mc), (mx, 1 - my, mc)

        def big(src_chip, q, sem, to, half=None):
            h = mc if half is None else half
            return pltpu.make_async_remote_copy(
                src_ref=big_out.at[src_chip, h, q], dst_ref=big_out.at[src_chip, h, q],
                send_sem=bs.at[sem], recv_sem=br.at[sem], device_id=to, device_id_type=MESH)

        def own(q, sem, to):
            return pltpu.make_async_remote_copy(
                src_ref=big_in.at[mc, q], dst_ref=big_out.at[me, mc, q],
                send_sem=bs.at[sem], recv_sem=br.at[sem], device_id=to, device_id_type=MESH)

        for cp in (own(0, 0, xn), own(1, 3, yn), own(1, 1, xn), own(0, 2, yn)):
            cp.start()
            sends.append(cp)
        for t in range(n):
            for j, (px, py) in enumerate(chips):
                cp = pltpu.make_async_remote_copy(
                    src_ref=ins[t].at[mc], dst_ref=outs[t].at[me, mc], send_sem=ssem.at[t, j],
                    recv_sem=rsem.at[t, j], device_id=(px, py, mc), device_id_type=MESH)
                cp.start()
                sends.append(cp)

        pieces = [(kx, 0), (kx, 1), (ky, 0), (ky, 1), (kd, 0), (kd, 1)]

        def landed(piece, sem, frm, pass_on=None):
            src_chip, q = pieces[piece]
            big(src_chip, q, sem, frm).wait_recv()
            nxt = ([] if pass_on is None else [big(src_chip, q, pass_on[0], pass_on[1])])
            nxt.append(big(src_chip, q, 6 + piece, sib))
            for cp in nxt:
                cp.start()
                sends.append(cp)

        for j in range(1, 8):
            px, py, pc = _flip(mx, j & 4), _flip(my, j & 2), _flip(mc, j & 1)
            pltpu.make_async_remote_copy(
                src_ref=c_ref, dst_ref=cg_ref.at[4 * px + 2 * py + pc], send_sem=cs.at[j - 1],
                recv_sem=cr.at[j - 1], device_id=(px, py, pc), device_id_type=MESH).wait_recv()
        local.wait()
        pick = (lax.broadcasted_iota(jnp.int32, (8, 64), 1) == 8 * lax.broadcasted_iota(jnp.int32, (8, 64), 0))
        cv = jnp.dot(pick.astype(F32), cg_ref[...].reshape(64, D), preferred_element_type=F32,
                     precision=lax.Precision.HIGHEST)
        sc = cv * _sigmoid(cv)
        sc_ref[...] = sc
        mp_ref[me] = jnp.dot(sc, wada_ref[...], preferred_element_type=F32,
                             precision=lax.Precision.HIGHEST) + b_ref[...]
        for j, (px, py) in enumerate(chips):
            cp = pltpu.make_async_remote_copy(
                src_ref=mp_ref.at[me], dst_ref=mp_ref.at[me], send_sem=ms.at[j], recv_sem=mr.at[j],
                device_id=(px, py, mc), device_id_type=MESH)
            cp.start()
            sends.append(cp)

        landed(0, 0, xn, pass_on=(4, yn))
        landed(3, 3, yn, pass_on=(5, xn))
        landed(1, 1, xn)
        landed(2, 2, yn)
        for j, (px, py) in enumerate(chips):
            pltpu.make_async_remote_copy(
                src_ref=mp_ref.at[me], dst_ref=mp_ref.at[2 * px + py], send_sem=ms.at[j], recv_sem=mr.at[j],
                device_id=(px, py, mc), device_id_type=MESH).wait_recv()
        landed(4, 4, yn)
        landed(5, 5, xn)
        for t in range(n):
            for j, (px, py) in enumerate(chips):
                src = 2 * px + py
                pltpu.make_async_remote_copy(
                    src_ref=ins[t].at[mc], dst_ref=outs[t].at[src, mc], send_sem=ssem.at[t, j],
                    recv_sem=rsem.at[t, j], device_id=(px, py, mc), device_id_type=MESH).wait_recv()
                fwd = pltpu.make_async_remote_copy(
                    src_ref=outs[t].at[src, mc], dst_ref=outs[t].at[src, mc], send_sem=ssem.at[t, 3 + j],
                    recv_sem=rsem.at[t, 3 + j], device_id=sib, device_id_type=MESH)
                fwd.start()
                sends.append(fwd)
        for t in range(n):
            for j, (px, py) in enumerate(chips):
                src = 2 * px + py
                pltpu.make_async_remote_copy(
                    src_ref=outs[t].at[src, 1 - mc], dst_ref=outs[t].at[src, 1 - mc], send_sem=ssem.at[t, 3 + j],
                    recv_sem=rsem.at[t, 3 + j], device_id=sib, device_id_type=MESH).wait_recv()
        for i, (src_chip, q) in enumerate(pieces):
            big(src_chip, q, 6 + i, sib, half=1 - mc).wait_recv()
        for cp in sends:
            cp.wait_send()

    vm = pl.BlockSpec(memory_space=pltpu.VMEM)
    outs = pl.pallas_call(
        body, name="weights_allgather",
        out_shape=[jax.ShapeDtypeStruct((4,) + big_shard.shape, big_shard.dtype)]
                  + [jax.ShapeDtypeStruct((4,) + s.shape, s.dtype) for s in shards]
                  + [jax.ShapeDtypeStruct((8,) + c_blk.shape, F32), jax.ShapeDtypeStruct((4, 8, cols), F32),
                     jax.ShapeDtypeStruct((8, D), F32)],
        in_specs=[ANY] * (n + 1) + [vm] * 3, out_specs=[ANY] * (n + 1) + [vm] * 3,
        scratch_shapes=([pltpu.SemaphoreType.DMA((n, 6)), pltpu.SemaphoreType.DMA((n, 6))] if n else [])
                       + [pltpu.SemaphoreType.DMA((12,)), pltpu.SemaphoreType.DMA((12,)),
                          pltpu.SemaphoreType.DMA((7,)), pltpu.SemaphoreType.DMA((7,)),
                          pltpu.SemaphoreType.DMA((3,)), pltpu.SemaphoreType.DMA((3,)), pltpu.SemaphoreType.DMA(())],
        compiler_params=_params(),
    )(big_shard, *shards, c_blk, w_ada, b_shard)
    return outs[0], outs[1:1 + n], outs[1 + n], outs[2 + n], outs[3 + n]


def _shard_gather_copies(ins, outs, ssem, rsem):
    mx, my, mc = _coords()
    me = 2 * mx + my
    sib = (mx, my, 1 - mc)
    send, recv, fwd, fwd_recv = [], [], [], []
    for t in range(len(ins)):
        for j in range(3):
            px, py = _flip(mx, (j + 1) & 2), _flip(my, (j + 1) & 1)
            src = 2 * px + py
            far = dict(send_sem=ssem.at[t, j], recv_sem=rsem.at[t, j], device_id=(px, py, mc), device_id_type=MESH)
            near = dict(send_sem=ssem.at[t, 3 + j], recv_sem=rsem.at[t, 3 + j], device_id=sib, device_id_type=MESH)
            send.append(pltpu.make_async_remote_copy(src_ref=ins[t].at[mc], dst_ref=outs[t].at[me, mc], **far))
            recv.append(pltpu.make_async_remote_copy(src_ref=ins[t].at[mc], dst_ref=outs[t].at[src, mc], **far))
            fwd.append(pltpu.make_async_remote_copy(src_ref=outs[t].at[src, mc], dst_ref=outs[t].at[src, mc], **near))
            fwd_recv.append(pltpu.make_async_remote_copy(
                src_ref=outs[t].at[src, 1 - mc], dst_ref=outs[t].at[src, 1 - mc], **near))
    return send, recv, fwd, fwd_recv


def _pair_exchange(grads):
    n = len(grads)

    def body(*refs):
        ins, outs = refs[:n], refs[n:2 * n]
        ssem, rsem = refs[2 * n:]
        mx, my, mc = _coords()
        sib = (mx, my, 1 - mc)
        sends = []
        for t in range(n):
            cp = pltpu.make_async_remote_copy(
                src_ref=ins[t].at[1 - mc], dst_ref=outs[t], send_sem=ssem.at[t], recv_sem=rsem.at[t],
                device_id=sib, device_id_type=MESH)
            cp.start()
            sends.append(cp)
        for cp in sends:
            cp.wait_recv()
        for cp in sends:
            cp.wait_send()

    return pl.pallas_call(
        body, name="grad_pair_exchange",
        out_shape=[jax.ShapeDtypeStruct(g.shape[1:], g.dtype) for g in grads],
        in_specs=[ANY] * n, out_specs=[ANY] * n,
        scratch_shapes=[pltpu.SemaphoreType.DMA((n,)), pltpu.SemaphoreType.DMA((n,))],
    )(*grads)


def _scatter_copies(ins, outs, ssem, rsem):
    mx, my, mc = _coords()
    me = 2 * mx + my
    mine, theirs = [], []
    for t in range(len(ins)):
        for j in range(1, 4):
            px, py = _flip(mx, j & 2), _flip(my, j & 1)
            mine.append(pltpu.make_async_remote_copy(
                src_ref=ins[t].at[2 * px + py], dst_ref=outs[t].at[me], send_sem=ssem.at[t, j - 1],
                recv_sem=rsem.at[t, j - 1], device_id=(px, py, mc), device_id_type=MESH))
            theirs.append(pltpu.make_async_remote_copy(
                src_ref=ins[t].at[me], dst_ref=outs[t].at[2 * px + py], send_sem=ssem.at[t, j - 1],
                recv_sem=rsem.at[t, j - 1], device_id=(px, py, mc), device_id_type=MESH))
    return mine, theirs


def _half_exchange(halves, smalls):
    n = len(halves)

    def body(*refs):
        ins, sm_ref = refs[:n], refs[n]
        outs, all_ref = refs[n + 1:2 * n + 1], refs[2 * n + 1]
        ssem, rsem, gsem, hsem, lsem = refs[2 * n + 2:]
        mx, my, mc = _coords()
        me = 4 * mx + 2 * my + mc
        sib = (mx, my, 1 - mc)
        local = pltpu.make_async_copy(sm_ref, all_ref.at[me], lsem)
        local.start()
        sends = []
        for t in range(n):
            rc = pltpu.make_async_remote_copy(
                src_ref=ins[t], dst_ref=outs[t], send_sem=ssem.at[t], recv_sem=rsem.at[t],
                device_id=sib, device_id_type=MESH)
            rc.start()
            sends.append(rc)
        gathers = []
        for j in range(1, 8):
            peer = (_flip(mx, j & 4), _flip(my, j & 2), _flip(mc, j & 1))
            cp = pltpu.make_async_remote_copy(
                src_ref=sm_ref, dst_ref=all_ref.at[me], send_sem=gsem.at[j - 1], recv_sem=hsem.at[j - 1],
                device_id=peer, device_id_type=MESH)
            cp.start()
            gathers.append(cp)
        for cp in sends:
            cp.wait_recv()
        for j in range(1, 8):
            px, py, pc = _flip(mx, j & 4), _flip(my, j & 2), _flip(mc, j & 1)
            pltpu.make_async_remote_copy(
                src_ref=sm_ref, dst_ref=all_ref.at[4 * px + 2 * py + pc], send_sem=gsem.at[j - 1],
                recv_sem=hsem.at[j - 1], device_id=(px, py, pc), device_id_type=MESH).wait_recv()
        for cp in sends + gathers:
            cp.wait_send()
        local.wait()

    vm = pl.BlockSpec(memory_space=pltpu.VMEM)
    outs = pl.pallas_call(
        body, name="grad_half_exchange",
        out_shape=[jax.ShapeDtypeStruct(h.shape, h.dtype) for h in halves]
                  + [jax.ShapeDtypeStruct((8,) + smalls.shape, smalls.dtype)],
        in_specs=[ANY] * n + [vm], out_specs=[ANY] * n + [vm],
        scratch_shapes=[pltpu.SemaphoreType.DMA((n,)), pltpu.SemaphoreType.DMA((n,)),
                        pltpu.SemaphoreType.DMA((7,)), pltpu.SemaphoreType.DMA((7,)), pltpu.SemaphoreType.DMA(())],
    )(*halves, smalls)
    return outs[:n], outs[n]


def _row_tile(rows, cols, bytes_per_row_elem=4, budget=2 * 1024 * 1024):
    t = rows
    while t % 2 == 0 and t > 16 and t * cols * bytes_per_row_elem > budget:
        t //= 2
    return t


def _pair_add(core, g, r):
    _, _, rh, cc = g.shape
    tr = _row_tile(rh, cc)

    def body(core_ref, g_ref, r_ref, o_ref):
        o_ref[...] = (g_ref[...].astype(F32) + r_ref[...].astype(F32)).astype(o_ref.dtype)

    return pl.pallas_call(
        body, name="grad_pair_add",
        grid_spec=pltpu.PrefetchScalarGridSpec(
            num_scalar_prefetch=1, grid=(4, rh // tr),
            in_specs=[pl.BlockSpec((None, None, tr, cc), lambda k, i, c: (c[0], k, i, 0)),
                      pl.BlockSpec((None, tr, cc), lambda k, i, c: (k, i, 0))],
            out_specs=pl.BlockSpec((None, tr, cc), lambda k, i, c: (k, i, 0))),
        out_shape=jax.ShapeDtypeStruct(r.shape, BF16),
        compiler_params=_params(),
    )(core, g, r)


def _sum4(chip, p, q):
    _, rh, cc = q.shape
    tr = _row_tile(rh, cc)

    def body(chip_ref, p_ref, q1_ref, q2_ref, q3_ref, o_ref):
        o_ref[...] = ((p_ref[...].astype(F32) + q1_ref[...].astype(F32)) + q2_ref[...].astype(F32)) + q3_ref[...].astype(F32)

    def other(j):
        return pl.BlockSpec((None, tr, cc), lambda i, c: (jnp.bitwise_xor(c[0], j), i, 0))

    return pl.pallas_call(
        body, name="grad_sum4",
        grid_spec=pltpu.PrefetchScalarGridSpec(
            num_scalar_prefetch=1, grid=(rh // tr,),
            in_specs=[pl.BlockSpec((None, tr, cc), lambda i, c: (c[0], i, 0)), other(1), other(2), other(3)],
            out_specs=pl.BlockSpec((tr, cc), lambda i, c: (i, 0))),
        out_shape=jax.ShapeDtypeStruct((rh, cc), F32),
        compiler_params=_params(),
    )(chip, p, q, q, q)


def _adamw_math(w, g, m, v):
    m = ADAM_B1 * m + (1.0 - ADAM_B1) * g
    v = ADAM_B2 * v + (1.0 - ADAM_B2) * (g * g)
    m_hat = m / (1.0 - ADAM_B1 ** ADAM_STEP)
    v_hat = v / (1.0 - ADAM_B2 ** ADAM_STEP)
    delta = -ADAM_LR * (m_hat / (jnp.sqrt(v_hat) + ADAM_EPS) + ADAM_WD * w)
    return delta, m, v


def _adamw_halves(core, w, mine, recv, m, v, name):
    rows, cols = w.shape
    rh = rows // 2
    tr = _row_tile(rh, cols, budget=1024 * 1024)
    nh = rh // tr

    def body(core_ref, w_ref, a_ref, b_ref, m_ref, v_ref, g_ref, d_ref, mo_ref, vo_ref):
        g = jnp.where(pl.program_id(0) == core_ref[0], a_ref[...], b_ref[...])
        d, mn, vn = _adamw_math(w_ref[...], g, m_ref[...], v_ref[...])
        g_ref[...] = g
        d_ref[...] = d
        mo_ref[...] = mn
        vo_ref[...] = vn

    full = pl.BlockSpec((tr, cols), lambda h, i, c: (h * nh + i, 0))
    half = pl.BlockSpec((tr, cols), lambda h, i, c: (i, 0))
    sd = jax.ShapeDtypeStruct((rows, cols), F32)
    return pl.pallas_call(
        body, name=name,
        grid_spec=pltpu.PrefetchScalarGridSpec(
            num_scalar_prefetch=1, grid=(2, nh),
            in_specs=[full, half, half, full, full], out_specs=[full] * 4),
        out_shape=[sd, sd, sd, sd], compiler_params=_params(),
    )(core, w, mine, recv, m, v)


def _inproj_nat(x, mod, norm_w, w, *, tm=1024, tn=1664):
    S, D = x.shape
    N = w.shape[1]
    nt, nj = S // tm, N // tn
    chunk = 256

    def body(x_ref, mod_ref, nw_ref, w_ref, proj_ref, h_ref, ht_ref, h_a, ht_a, h_b, ht_b):
        i, j = pl.program_id(0), pl.program_id(1)

        def make_h(h_s, ht_s):
            shift = mod_ref[:, 0:D]
            scale1 = 1.0 + mod_ref[:, D:2 * D]
            for r in range(tm // chunk):
                xv = x_ref[pl.ds(r * chunk, chunk), :]
                ms = jnp.mean(xv * xv, axis=-1, keepdims=True)
                h = (xv * lax.rsqrt(ms + EPS) * nw_ref[...]) * scale1 + shift
                h_s[pl.ds(r * chunk, chunk), :] = h.astype(BF16)
                ht_s[:, pl.ds(r * chunk, chunk)] = h.T.astype(BF16)

        @pl.when((i == 0) & (j == 0))
        def _():
            make_h(h_a, ht_a)

        def step(cur, nxt):
            @pl.when(j == 0)
            def _():
                h_ref[...] = cur[0][...]
                ht_ref[...] = cur[1][...]

            def project():
                proj_ref[...] = jnp.dot(cur[0][...], w_ref[...], preferred_element_type=F32).astype(BF16)

            ahead = (j == nj - 1) & (i < nt - 1)

            @pl.when(ahead)
            def _():
                project()
                make_h(*nxt)

            @pl.when(jnp.logical_not(ahead))
            def _():
                project()

        @pl.when(i % 2 == 0)
        def _():
            step((h_a, ht_a), (h_b, ht_b))

        @pl.when(i % 2 == 1)
        def _():
            step((h_b, ht_b), (h_a, ht_a))

    def x_tile(i, j):
        return (jnp.where((i == 0) & (j == 0), 0, jnp.minimum(i + 1, nt - 1)), 0)

    return pl.pallas_call(
        body, name="inproj_nat", grid=(nt, nj),
        in_specs=[pl.BlockSpec((tm, D), x_tile),
                  pl.BlockSpec((1, 3 * D), lambda i, j: (0, 0)),
                  pl.BlockSpec((1, D), lambda i, j: (0, 0)),
                  pl.BlockSpec((D, tn), lambda i, j: (0, j))],
        out_specs=[pl.BlockSpec((tm, tn), lambda i, j: (i, j)),
                   pl.BlockSpec((tm, D), lambda i, j: (i, 0)),
                   pl.BlockSpec((D, tm), lambda i, j: (0, i))],
        out_shape=[jax.ShapeDtypeStruct((S, N), BF16), jax.ShapeDtypeStruct((S, D), BF16),
                   jax.ShapeDtypeStruct((D, S), BF16)],
        scratch_shapes=[pltpu.VMEM((tm, D), BF16), pltpu.VMEM((D, tm), BF16),
                        pltpu.VMEM((tm, D), BF16), pltpu.VMEM((D, tm), BF16)],
        compiler_params=_params(),
    )(x, mod, norm_w, w)


def _inproj_qkv(a, b, shards, *, tm=1024, tn=1536):
    M, K = a.shape
    N = b.shape[1]
    n = len(shards)
    ni, nj = M // tm, N // tn

    def body(a_ref, b_ref, *rest):
        ins, o_ref, outs, sems = rest[:n], rest[n], rest[n + 1:2 * n + 1], rest[2 * n + 1:]
        i, j = pl.program_id(0), pl.program_id(1)
        if n:
            @pl.when((i == 0) & (j == 0))
            def _():
                for cp in _shard_gather_copies(ins, outs, *sems)[0]:
                    cp.start()

        o_ref[...] = jnp.dot(a_ref[...], b_ref[...], preferred_element_type=F32).astype(o_ref.dtype)

        if n:
            @pl.when((i == ni - 1) & (j == nj - 1))
            def _():
                send, recv, fwd, fwd_recv = _shard_gather_copies(ins, outs, *sems)
                for r, f in zip(recv, fwd):
                    r.wait_recv()
                    f.start()
                for r in fwd_recv:
                    r.wait_recv()
                for cp in send + fwd:
                    cp.wait_send()

    outs = pl.pallas_call(
        body, name="inproj_qkv", grid=(ni, nj),
        in_specs=[pl.BlockSpec((tm, K), lambda i, j: (i, 0)), pl.BlockSpec((K, tn), lambda i, j: (0, j))] + [ANY] * n,
        out_specs=[pl.BlockSpec((tm, tn), lambda i, j: (i, j))] + [ANY] * n,
        out_shape=[jax.ShapeDtypeStruct((M, N), BF16)] + [jax.ShapeDtypeStruct((4,) + s.shape, s.dtype) for s in shards],
        scratch_shapes=[pltpu.SemaphoreType.DMA((n, 6)), pltpu.SemaphoreType.DMA((n, 6))] if n else [],
        compiler_params=_params(),
    )(a, b, *shards)
    return outs[0], outs[1:]


def _matmul_lhs_resident(a, b, *, tn, name, exchange=None):
    M, K = a.shape
    N = b.shape[1]
    nj = N // tn
    hosts = exchange is not None

    def body(a_hbm, b_ref, *rest):
        if hosts:
            x_hbm, o_ref, r_hbm, a_s, ssem, rsem = rest
        else:
            o_ref, a_s = rest
        j = pl.program_id(0)

        def to_sibling():
            mx, my, mc = _coords()
            return pltpu.make_async_remote_copy(src_ref=x_hbm.at[1 - mc], dst_ref=r_hbm, send_sem=ssem, recv_sem=rsem,
                                                device_id=(mx, my, 1 - mc), device_id_type=MESH)

        @pl.when(j == 0)
        def _():
            if hosts:
                to_sibling().start()
            pltpu.sync_copy(a_hbm, a_s)

        o_ref[...] = jnp.dot(a_s[...], b_ref[...], preferred_element_type=F32).astype(o_ref.dtype)

        if hosts:
            @pl.when(j == nj - 1)
            def _():
                cp = to_sibling()
                cp.wait_recv()
                cp.wait_send()

    prod = jax.ShapeDtypeStruct((M, N), BF16)
    out = pl.pallas_call(
        body, name=name, grid=(nj,),
        in_specs=[ANY, pl.BlockSpec((K, tn), lambda j: (0, j))] + ([ANY] if hosts else []),
        out_specs=[pl.BlockSpec((M, tn), lambda j: (0, j)), ANY] if hosts else pl.BlockSpec((M, tn), lambda j: (0, j)),
        out_shape=[prod, jax.ShapeDtypeStruct(exchange.shape[1:], exchange.dtype)] if hosts else prod,
        scratch_shapes=[pltpu.VMEM((M, K), BF16)]
                       + ([pltpu.SemaphoreType.DMA(()), pltpu.SemaphoreType.DMA(())] if hosts else []),
        compiler_params=_params(),
    )(a, b, *([exchange] if hosts else []))
    return out


SHARD_COLS = 2816


def _column_tables(part):
    if part == "nat":
        bw = 256
        orig = [j * bw if j < 16 else 8704 + (j - 16) * bw for j in range(NAT // bw)]
    else:
        bw = 128
        orig = []
        for jj in range(NQKV // bw):
            g, p, t = jj // 12, (jj % 12) // 3, jj % 3
            orig.append(4096 + t * 1536 + g * 512 + p * 128)
    tk = jnp.array([o // SHARD_COLS for o in orig], jnp.int32)
    tc = jnp.array([(o % SHARD_COLS) // bw for o in orig], jnp.int32)
    return tk, tc, bw


def _weights_prep(gath, own, chipv, part):
    D = D_MODEL
    tk, tc, bw = _column_tables(part)
    n = tk.shape[0]
    per_step = 2 if part == "nat" else 4
    assert n % per_step == 0

    def body(tk_ref, tc_ref, chip_ref, *refs):
        w_ref = refs[-1]
        for u in range(per_step):
            g_ref, o_ref = refs[2 * u], refs[2 * u + 1]
            mine = tk_ref[pl.program_id(0) * per_step + u] == chip_ref[0]
            w_ref[:, u * bw:(u + 1) * bw] = jnp.where(mine, o_ref[...], g_ref[...])

    def gath_spec(u):
        def idx(j, tk, tc, ch):
            k = tk[j * per_step + u]
            return (jnp.where(k == ch[0], (k + 1) % 4, k), 0, tc[j * per_step + u])
        return pl.BlockSpec((None, D, bw), idx)

    def own_spec(u):
        return pl.BlockSpec((D, bw), lambda j, tk, tc, ch: (0, tc[j * per_step + u]))

    specs = []
    for u in range(per_step):
        specs += [gath_spec(u), own_spec(u)]
    return pl.pallas_call(
        body, name="weights_prep_" + part,
        grid_spec=pltpu.PrefetchScalarGridSpec(
            num_scalar_prefetch=3, grid=(n // per_step,),
            in_specs=specs,
            out_specs=pl.BlockSpec((D, per_step * bw), lambda j, tk, tc, ch: (0, j))),
        out_shape=jax.ShapeDtypeStruct((D, n * bw), BF16),
        compiler_params=_params(),
    )(tk, tc, chipv, *([gath, own] * per_step))


def _pair_add_to_shards(core, gw, r, into, part):
    D = D_MODEL
    tk, tc, bw = _column_tables(part)
    n = tk.shape[0]

    def body(tk_ref, tc_ref, core_ref, g_ref, r_ref, into_ref, o_ref):
        o_ref[...] = (g_ref[...].astype(F32) + r_ref[...].astype(F32)).astype(BF16)

    return pl.pallas_call(
        body, name="grad_pair_add_" + part,
        grid_spec=pltpu.PrefetchScalarGridSpec(
            num_scalar_prefetch=3, grid=(n,),
            in_specs=[pl.BlockSpec((None, D // 2, bw), lambda j, tk, tc, c: (c[0], 0, j)),
                      pl.BlockSpec((D // 2, bw), lambda j, tk, tc, c: (0, j)), ANY],
            out_specs=pl.BlockSpec((None, D // 2, bw), lambda j, tk, tc, c: (tk[j], 0, tc[j]))),
        out_shape=jax.ShapeDtypeStruct(into.shape, BF16),
        input_output_aliases={5: 0},
        compiler_params=_params(),
    )(tk, tc, core, gw, r, into)


def _head_norm(xb, w, ebd):
    x = xb.astype(F32)
    ss = jnp.dot((x * x).astype(BF16), ebd, preferred_element_type=F32)
    return x * lax.rsqrt(ss * (1.0 / HEAD_DIM) + EPS) * w


def _window_mask(no_prev):
    qi = lax.broadcasted_iota(jnp.int32, (128, 256), 0)
    kc = lax.broadcasted_iota(jnp.int32, (128, 256), 1)
    ok = (kc >= qi) & (kc <= qi + 128)
    if no_prev is not None:
        ok = ok & ((kc >= 128) | jnp.logical_not(no_prev))
    return ok


def _lane_masks():
    lane = lax.broadcasted_iota(jnp.int32, (1, 128), 1)
    return [(lane < HEAD_DIM).astype(F32), (lane >= HEAD_DIM).astype(F32)]


def _head_col(v, mh):
    return jnp.sum(v * mh, axis=1, keepdims=True) * (1.0 / HEAD_DIM)


def _attn_fwd(projq, qnw2, knw2, ebd, g):
    S = projq.shape[0]
    d = DILATIONS[g]
    T = ATT_TILE
    nt = S // T
    nb = T // (128 * d)
    sdt = BF16 if d == 1 else F32

    def body(cur_ref, qw_ref, kw_ref, ebd_ref, o_ref, lse_ref, qs_s, ks_s, vs_s):
        t = pl.program_id(1)
        e = ebd_ref[...]

        @pl.when(t == 0)
        def _():
            ks_s[pl.ds(0, T), :] = jnp.zeros((T, 128), sdt)
            vs_s[pl.ds(0, T), :] = jnp.zeros((T, 128), sdt)

        @pl.when(t > 0)
        def _():
            ks_s[pl.ds(0, T), :] = ks_s[pl.ds(T, T), :]
            vs_s[pl.ds(0, T), :] = vs_s[pl.ds(T, T), :]

        qs_s[...] = (_head_norm(cur_ref[:, 0:128], qw_ref[...], e) * (QK_SCALE * LOG2E)).astype(sdt)
        ks_s[pl.ds(T, T), :] = _head_norm(cur_ref[:, 128:256], kw_ref[...], e).astype(sdt)
        vs_s[pl.ds(T, T), :] = cur_ref[:, 256:384].astype(sdt)
        m0, m1 = _lane_masks()
        first = t == 0
        for r in range(d):
            for bb in range(nb):
                q0 = r + bb * 128 * d
                k0 = T + r + (bb - 1) * 128 * d
                qs = qs_s[pl.ds(q0, 128, stride=d), :].astype(F32)
                kb = ks_s[pl.ds(k0, 256, stride=d), :].astype(BF16)
                vb = vs_s[pl.ds(k0, 256, stride=d), :].astype(BF16)
                ok = _window_mask(first if bb == 0 else None)
                ok2 = jnp.concatenate([ok, ok], axis=0)
                q2 = jnp.concatenate([qs * m0, qs * m1], axis=0).astype(BF16)
                s = lax.dot_general(q2, kb, (((1,), (1,)), ((), ())), preferred_element_type=F32)
                s = jnp.where(ok2, s, NEG_INF)
                mx = jnp.max(s, axis=-1, keepdims=True)
                p = jnp.exp2(s - mx)
                l = jnp.sum(p, axis=-1, keepdims=True)
                o = jnp.dot(p.astype(BF16), vb, preferred_element_type=F32) / l
                lse = mx * LN2 + jnp.log(l)
                o_ref[pl.ds(q0, 128, stride=d), :] = o[0:128] * m0 + o[128:256] * m1
                lse_ref[pl.ds(q0, 128, stride=d), :] = lse[0:128] * m0 + lse[128:256] * m1

    return pl.pallas_call(
        body, name=f"attn_fwd_g{g}", grid=(4, nt),
        in_specs=[pl.BlockSpec((T, 384), lambda p, t: (t, g * 4 + p)),
                  pl.BlockSpec((1, 128), lambda p, t: (0, 0)),
                  pl.BlockSpec((1, 128), lambda p, t: (0, 0)),
                  pl.BlockSpec((128, 128), lambda p, t: (0, 0))],
        out_specs=[pl.BlockSpec((T, 128), lambda p, t: (t, p)), pl.BlockSpec((T, 128), lambda p, t: (t, p))],
        out_shape=[jax.ShapeDtypeStruct((S, ATTN_OUT), F32), jax.ShapeDtypeStruct((S, ATTN_OUT), F32)],
        scratch_shapes=[pltpu.VMEM((T, 128), sdt), pltpu.VMEM((2 * T, 128), sdt), pltpu.VMEM((2 * T, 128), sdt)],
        compiler_params=_params(),
    )(projq, qnw2, knw2, ebd)


def _attn_bwd(projq, dprojq, d_o, lse, delta, qnw2, knw2, ebd, g):
    S = projq.shape[0]
    d = DILATIONS[g]
    T = ATT_TILE
    nt = S // T
    nb = T // (128 * d)
    sdt = F32

    def norm_bwd(raw_b, dy, w, e):
        x = raw_b.astype(F32)
        ss = jnp.dot((x * x).astype(BF16), e, preferred_element_type=F32)
        rstd = lax.rsqrt(ss * (1.0 / HEAD_DIM) + EPS)
        xh = x * rstd
        gw = jnp.sum(dy * xh, axis=0, keepdims=True)
        dxh = dy * w
        mean = jnp.dot((dxh * xh).astype(BF16), e, preferred_element_type=F32) * (1.0 / HEAD_DIM)
        return rstd * (dxh - xh * mean), gw

    nt_dims = (((1,), (1,)), ((), ()))
    tn_dims = (((0,), (0,)), ((), ()))

    def unit_stacked(qs, dos, ls, dls, kb, vb, ok, m0, m1):
        ok2 = jnp.concatenate([ok, ok], axis=0)
        q2 = jnp.concatenate([qs * m0, qs * m1], axis=0).astype(BF16)
        do2 = jnp.concatenate([dos * m0, dos * m1], axis=0).astype(BF16)
        lcol = jnp.concatenate([_head_col(ls, m0), _head_col(ls, m1)], axis=0)
        dcol = jnp.concatenate([_head_col(dls, m0), _head_col(dls, m1)], axis=0)
        s = jnp.where(ok2, lax.dot_general(q2, kb, nt_dims, preferred_element_type=F32), NEG_INF)
        p = jnp.exp2(s - lcol * LOG2E)
        dp = lax.dot_general(do2, vb, nt_dims, preferred_element_type=F32)
        ds = (p * (dp - dcol)).astype(BF16)
        dq2 = jnp.dot(ds, kb, preferred_element_type=F32)
        dk = lax.dot_general(ds, q2, tn_dims, preferred_element_type=F32)
        dv = lax.dot_general(p.astype(BF16), do2, tn_dims, preferred_element_type=F32)
        return dq2[0:128] * m0 + dq2[128:256] * m1, dk, dv

    def unit_per_head(qs, dos, ls, dls, kb, vb, ok, m0, m1):
        dq_t = dk_t = dv_t = None
        for mh in (m0, m1):
            qh = (qs * mh).astype(BF16)
            doh = (dos * mh).astype(BF16)
            s = jnp.where(ok, lax.dot_general(qh, kb, nt_dims, preferred_element_type=F32), NEG_INF)
            p = jnp.exp2(s - _head_col(ls, mh) * LOG2E)
            dp = lax.dot_general(doh, vb, nt_dims, preferred_element_type=F32)
            ds = (p * (dp - _head_col(dls, mh))).astype(BF16)
            dq = jnp.dot(ds, kb, preferred_element_type=F32) * mh
            dk = lax.dot_general(ds, qh, tn_dims, preferred_element_type=F32)
            dv = lax.dot_general(p.astype(BF16), doh, tn_dims, preferred_element_type=F32)
            dq_t = dq if dq_t is None else dq_t + dq
            dk_t = dk if dk_t is None else dk_t + dk
            dv_t = dv if dv_t is None else dv_t + dv
        return dq_t, dk_t, dv_t

    unit = unit_per_head if d == 1 else unit_stacked

    def body(cur_ref, prev_ref, do_ref, lse_ref, dl_ref, qw_ref, kw_ref, ebd_ref, dp_in_ref,
             out_ref, gqk_ref, qs_s, ks_s, vs_s, do_s, dq_cur, dq_prev, dk_acc, dv_acc):
        i = pl.program_id(1)
        e = ebd_ref[...]
        m0, m1 = _lane_masks()

        @pl.when(i == 0)
        def _():
            dk_acc[...] = jnp.zeros_like(dk_acc)
            dv_acc[...] = jnp.zeros_like(dv_acc)
            dq_prev[...] = jnp.zeros_like(dq_prev)
            gqk_ref[...] = jnp.zeros_like(gqk_ref)

        @pl.when(i < nt)
        def _():
            qs_s[...] = _head_norm(cur_ref[:, 0:128], qw_ref[...], e) * (QK_SCALE * LOG2E)
            ks_s[pl.ds(0, T), :] = _head_norm(prev_ref[:, 128:256], kw_ref[...], e)
            ks_s[pl.ds(T, T), :] = _head_norm(cur_ref[:, 128:256], kw_ref[...], e)
            vs_s[pl.ds(0, T), :] = prev_ref[:, 256:384].astype(F32)
            vs_s[pl.ds(T, T), :] = cur_ref[:, 256:384].astype(F32)
            do_s[...] = do_ref[...].astype(F32)
            first = i == 0
            for r in range(d):
                own_k = own_v = None
                for bb in range(nb):
                    q0 = r + bb * 128 * d
                    k0 = T + r + (bb - 1) * 128 * d
                    qs = qs_s[pl.ds(q0, 128, stride=d), :]
                    dos = do_s[pl.ds(q0, 128, stride=d), :]
                    ls = lse_ref[pl.ds(q0, 128, stride=d), :]
                    dls = dl_ref[pl.ds(q0, 128, stride=d), :]
                    kb = ks_s[pl.ds(k0, 256, stride=d), :].astype(BF16)
                    vb = vs_s[pl.ds(k0, 256, stride=d), :].astype(BF16)
                    ok = _window_mask(first if bb == 0 else None)
                    dq_t, dk_t, dv_t = unit(qs, dos, ls, dls, kb, vb, ok, m0, m1)
                    dq_cur[pl.ds(q0, 128, stride=d), :] = dq_t
                    before = pl.ds(k0, 128, stride=d)
                    if bb == 0:
                        dk_acc[before, :] = dk_acc[before, :] + dk_t[0:128]
                        dv_acc[before, :] = dv_acc[before, :] + dv_t[0:128]
                    else:
                        dk_acc[before, :] = own_k + dk_t[0:128]
                        dv_acc[before, :] = own_v + dv_t[0:128]
                    own_k, own_v = dk_t[128:256], dv_t[128:256]
                tail = pl.ds(T + r + (nb - 1) * 128 * d, 128, stride=d)
                dk_acc[tail, :] = own_k
                dv_acc[tail, :] = own_v

        @pl.when(i >= 1)
        def _():
            dq_raw, gq = norm_bwd(prev_ref[:, 0:128], dq_prev[...] * QK_SCALE, qw_ref[...], e)
            dk_raw, gk = norm_bwd(prev_ref[:, 128:256], dk_acc[pl.ds(0, T), :] * LN2, kw_ref[...], e)
            out_ref[:, 0:128] = dq_raw.astype(BF16)
            out_ref[:, 128:256] = dk_raw.astype(BF16)
            out_ref[:, 256:384] = dv_acc[pl.ds(0, T), :].astype(BF16)
            gqk_ref[0:1, :] += gq
            gqk_ref[1:2, :] += gk

        dq_prev[...] = dq_cur[...]
        dk_acc[pl.ds(0, T), :] = dk_acc[pl.ds(T, T), :]
        dv_acc[pl.ds(0, T), :] = dv_acc[pl.ds(T, T), :]

    last = nt - 1
    qtile = lambda p, i: (jnp.minimum(i, last), p)
    return pl.pallas_call(
        body, name=f"attn_bwd_g{g}", grid=(4, nt + 1),
        in_specs=[pl.BlockSpec((T, 384), lambda p, i: (jnp.minimum(i, last), g * 4 + p)),
                  pl.BlockSpec((T, 384), lambda p, i: (jnp.maximum(i - 1, 0), g * 4 + p)),
                  pl.BlockSpec((T, 128), qtile), pl.BlockSpec((T, 128), qtile), pl.BlockSpec((T, 128), qtile),
                  pl.BlockSpec((1, 128), lambda p, i: (0, 0)),
                  pl.BlockSpec((1, 128), lambda p, i: (0, 0)),
                  pl.BlockSpec((128, 128), lambda p, i: (0, 0)),
                  ANY],
        out_specs=[pl.BlockSpec((T, 384), lambda p, i: (jnp.maximum(i - 1, 0), g * 4 + p)),
                   pl.BlockSpec((None, 8, 128), lambda p, i: (p, 0, 0))],
        out_shape=[jax.ShapeDtypeStruct(dprojq.shape, BF16), jax.ShapeDtypeStruct((4, 8, 128), F32)],
        scratch_shapes=[pltpu.VMEM((T, 128), sdt), pltpu.VMEM((2 * T, 128), sdt), pltpu.VMEM((2 * T, 128), sdt),
                        pltpu.VMEM((T, 128), sdt), pltpu.VMEM((T, 128), F32), pltpu.VMEM((T, 128), F32),
                        pltpu.VMEM((2 * T, 128), F32), pltpu.VMEM((2 * T, 128), F32)],
        input_output_aliases={8: 0},
        compiler_params=_params(),
    )(projq, projq, d_o, lse, delta, qnw2, knw2, ebd, dprojq)


def _mid(projn, o_g, lse_g, x, tgt, mod, convw8, wc, wa, wo, ebd, *, tm=256):
    S, D = x.shape
    nt = S // tm
    nst = max(1, 256 // tm)
    assert nt % nst == 0
    hb = tm // 16

    def body(pn_ref, hc_ref, hx_ref, o0_ref, o1_ref, o2_ref, l0_ref, l1_ref, l2_ref, x_ref, t_ref, mod_ref,
             cw_ref, ebd_ref, wc_hbm, wa_hbm, wo_hbm,
             dpn_ref, do_ref, lse_ref, dl_ref, dout_ref, vacc_ref, gwo_hbm, gwc_hbm, gwa_hbm,
             wc_s, wa_s, wo_s, gwo_s, gwc_s, gwa_s, carry_s, *stashes):
        i = pl.program_id(0)
        ti = nt - 1 - i
        par = i % nst
        stash = pl.ds(pl.multiple_of(par * tm, tm), tm)

        @pl.when(i == 0)
        def _():
            for src, dst in ((wc_hbm, wc_s), (wa_hbm, wa_s), (wo_hbm, wo_s)):
                pltpu.sync_copy(src, dst)
            gwo_s[...] = jnp.zeros_like(gwo_s)
            gwc_s[...] = jnp.zeros_like(gwc_s)
            gwa_s[...] = jnp.zeros_like(gwa_s)
            carry_s[...] = jnp.zeros_like(carry_s)
            vacc_ref[...] = jnp.zeros_like(vacc_ref)

        rows = lax.broadcasted_iota(jnp.int32, (tm, D), 0)
        w0, w1, w2 = cw_ref[0:1, :], cw_ref[1:2, :], cw_ref[2:3, :]
        gate = mod_ref[:, 2 * D:3 * D]

        b_a = pn_ref[:, BA:BA + D].astype(F32)
        c_a = pn_ref[:, CA:CA + D].astype(F32)
        x_a = pn_ref[:, XA:XA + D].astype(F32)
        z_a = pn_ref[:, ZA:ZA + D].astype(F32)
        u = c_a * x_a
        has_prev = (ti > 0).astype(F32)
        uh = hc_ref[...].astype(F32) * hx_ref[...].astype(F32) * has_prev
        h14, h15 = _row(uh, 14), _row(uh, 15)
        u_m1 = jnp.where(rows == 0, h15, pltpu.roll(u, 1, 0))
        u_m2 = jnp.where(rows == 0, h14, jnp.where(rows == 1, h15, pltpu.roll(u, 2, 0)))
        conv = w0 * u_m2 + w1 * u_m1 + w2 * u
        sg_a = _sigmoid(z_a)
        sz_a = z_a * sg_a
        y_a = b_a * conv * sz_a
        y_ab = y_a.astype(BF16)
        pa = jnp.dot(y_ab, wc_s[...], preferred_element_type=F32)

        l0, l1, l2 = l0_ref[...], l1_ref[...], l2_ref[...]
        mxl = jnp.maximum(jnp.maximum(l0, l1), l2)
        e0, e1, e2 = jnp.exp(l0 - mxl), jnp.exp(l1 - mxl), jnp.exp(l2 - mxl)
        den = e0 + e1 + e2
        attn = (e0 * o0_ref[...] + e1 * o1_ref[...] + e2 * o2_ref[...]) / den
        lse_ref[...] = mxl + jnp.log(den)
        z_b = pn_ref[:, ZB:ZB + ATTN_OUT].astype(F32)
        sg_b = _sigmoid(z_b)
        sz_b = z_b * sg_b
        y_b = attn * sz_b
        y_bb = y_b.astype(BF16)
        pb = jnp.dot(y_bb, wa_s[...], preferred_element_type=F32)

        s_a = _sigmoid(pn_ref[:, GA:GA + D].astype(F32))
        s_b = _sigmoid(pn_ref[:, GB:GB + D].astype(F32))
        merged = s_a * pa + s_b * pb
        merged_b = merged.astype(BF16)
        mo = jnp.dot(merged_b, wo_s[...], preferred_element_type=F32)
        diff = x_ref[...] + gate * mo - t_ref[...]
        dout = diff * (1.0 / D)
        dout_ref[...] = dout.astype(BF16)
        vacc_ref[4:5, :] += jnp.sum(diff * diff, axis=0, keepdims=True)
        vacc_ref[0:1, :] += jnp.sum(dout * mo, axis=0, keepdims=True)

        d_mo = (dout * gate).astype(BF16)
        nt_dims = (((1,), (1,)), ((), ()))
        dmerged = lax.dot_general(d_mo, wo_s[...], nt_dims, preferred_element_type=F32)
        dpa = (dmerged * s_a).astype(BF16)
        dpb = (dmerged * s_b).astype(BF16)
        dpn_ref[:, GA:GA + D] = (dmerged * pa * s_a * (1.0 - s_a)).astype(BF16)
        dpn_ref[:, GB:GB + D] = (dmerged * pb * s_b * (1.0 - s_b)).astype(BF16)
        tn = (((0,), (0,)), ((), ()))
        if nst == 1:
            gwo_s[...] += lax.dot_general(merged_b, d_mo, tn, preferred_element_type=F32)
            gwc_s[...] += lax.dot_general(y_ab, dpa, tn, preferred_element_type=F32)
            gwa_s[...] += lax.dot_general(y_bb, dpb, tn, preferred_element_type=F32)
        else:
            st_m, st_d, st_ya, st_pa, st_yb, st_pb = stashes
            st_m[stash, :] = merged_b
            st_d[stash, :] = d_mo
            st_ya[stash, :] = y_ab
            st_pa[stash, :] = dpa
            st_yb[stash, :] = y_bb
            st_pb[stash, :] = dpb

            @pl.when(par == nst - 1)
            def _():
                gwo_s[...] += lax.dot_general(st_m[...], st_d[...], tn, preferred_element_type=F32)
                gwc_s[...] += lax.dot_general(st_ya[...], st_pa[...], tn, preferred_element_type=F32)
                gwa_s[...] += lax.dot_general(st_yb[...], st_pb[...], tn, preferred_element_type=F32)

        dy_a = lax.dot_general(dpa, wc_s[...], nt_dims, preferred_element_type=F32)
        dy_b = lax.dot_general(dpb, wa_s[...], nt_dims, preferred_element_type=F32)

        dattn = dy_b * sz_b
        dpn_ref[:, ZB:ZB + ATTN_OUT] = (dy_b * attn * sg_b * (1.0 + z_b * (1.0 - sg_b))).astype(BF16)
        do_ref[...] = dattn.astype(BF16)
        dl_ref[...] = jnp.dot((dattn * attn).astype(BF16), ebd_ref[...], preferred_element_type=F32)

        dpn_ref[:, BA:BA + D] = (dy_a * conv * sz_a).astype(BF16)
        dpn_ref[:, ZA:ZA + D] = (dy_a * b_a * conv * sg_a * (1.0 + z_a * (1.0 - sg_a))).astype(BF16)
        dconv = dy_a * b_a * sz_a
        vacc_ref[1:2, :] += jnp.sum(dconv * u_m2, axis=0, keepdims=True)
        vacc_ref[2:3, :] += jnp.sum(dconv * u_m1, axis=0, keepdims=True)
        vacc_ref[3:4, :] += jnp.sum(dconv * u, axis=0, keepdims=True)
        nxt = carry_s[...]
        n0, n1 = _row(nxt, 0), _row(nxt, 1)
        dc_p1 = jnp.where(rows == tm - 1, n0, pltpu.roll(dconv, tm - 1, 0))
        dc_p2 = jnp.where(rows == tm - 1, n1, jnp.where(rows == tm - 2, n0, pltpu.roll(dconv, tm - 2, 0)))
        du = w2 * dconv + w1 * dc_p1 + w0 * dc_p2
        carry_s[...] = dconv[0:8, :]
        dpn_ref[:, CA:CA + D] = (du * x_a).astype(BF16)
        dpn_ref[:, XA:XA + D] = (du * c_a).astype(BF16)

        @pl.when(i == nt - 1)
        def _():
            pltpu.sync_copy(gwo_s, gwo_hbm)
            pltpu.sync_copy(gwc_s, gwc_hbm)
            pltpu.sync_copy(gwa_s, gwa_hbm)

    rev = lambda i: (nt - 1 - i, 0)
    halo = lambda col: pl.BlockSpec((16, D), lambda i: (jnp.maximum((nt - 1 - i) * hb - 1, 0), col))
    tile512 = pl.BlockSpec((tm, ATTN_OUT), rev)
    tileD = pl.BlockSpec((tm, D), rev)
    const = lambda shape: pl.BlockSpec(shape, lambda i: (0, 0))
    return pl.pallas_call(
        body, name="mid_fwd_bwd", grid=(nt,),
        in_specs=[pl.BlockSpec((tm, NAT), rev), halo(CA // D), halo(XA // D)] + [tile512] * 6
                 + [tileD, tileD, const((1, 3 * D)), const((8, D)), const((ATTN_OUT, ATTN_OUT))] + [ANY] * 3,
        out_specs=[pl.BlockSpec((tm, NAT), rev), tile512, tile512, tile512, tileD, const((8, D)), ANY, ANY, ANY],
        out_shape=[jax.ShapeDtypeStruct((S, NAT), BF16), jax.ShapeDtypeStruct((S, ATTN_OUT), BF16),
                   jax.ShapeDtypeStruct((S, ATTN_OUT), F32), jax.ShapeDtypeStruct((S, ATTN_OUT), F32),
                   jax.ShapeDtypeStruct((S, D), BF16), jax.ShapeDtypeStruct((8, D), F32),
                   jax.ShapeDtypeStruct((D, D), F32), jax.ShapeDtypeStruct((D, D), F32),
                   jax.ShapeDtypeStruct((ATTN_OUT, D), F32)],
        scratch_shapes=[pltpu.VMEM((D, D), BF16), pltpu.VMEM((ATTN_OUT, D), BF16), pltpu.VMEM((D, D), BF16),
                        pltpu.VMEM((D, D), F32), pltpu.VMEM((D, D), F32), pltpu.VMEM((ATTN_OUT, D), F32),
                        pltpu.VMEM((8, D), F32)]
                       + ([pltpu.VMEM((nst * tm, D), BF16)] * 4
                          + [pltpu.VMEM((nst * tm, ATTN_OUT), BF16), pltpu.VMEM((nst * tm, D), BF16)] if nst > 1 else []),
        compiler_params=_params(vmem=V7X_VMEM_LIMIT_RESIDENT),
    )(projn, projn, projn, *o_g, *lse_g, x, tgt, mod, convw8, ebd, wc, wa, wo)


def _dh_matmul(dpn, dpq, w_nat, w_qkv, parts, *, tm=512):
    S = dpn.shape[0]
    D = D_MODEL
    n = len(parts)
    nt = S // tm

    def body(dn_ref, dq_ref, wn_hbm, wq_hbm, *rest):
        p_hbm, rest = rest[:n], rest[n:]
        dh_ref = rest[0]
        q_hbm = rest[1:1 + n]
        wn_s, wq_s = rest[1 + n:3 + n]
        sems = rest[3 + n:]

        @pl.when(pl.program_id(0) == 0)
        def _():
            if n:
                mine, _ = _scatter_copies(p_hbm, q_hbm, *sems)
                for cp in mine:
                    cp.start()
            pltpu.sync_copy(wn_hbm, wn_s)
            pltpu.sync_copy(wq_hbm, wq_s)

        if n:
            @pl.when(pl.program_id(0) == nt - 1)
            def _():
                mine, theirs = _scatter_copies(p_hbm, q_hbm, *sems)
                for cp in theirs:
                    cp.wait_recv()
                for cp in mine:
                    cp.wait_send()

        nt_dims = (((1,), (1,)), ((), ()))
        dh = (lax.dot_general(dn_ref[...], wn_s[...], nt_dims, preferred_element_type=F32)
              + lax.dot_general(dq_ref[...], wq_s[...], nt_dims, preferred_element_type=F32))
        dh_ref[...] = dh.astype(BF16)

    row = lambda w: pl.BlockSpec((tm, w), lambda i: (i, 0))
    outs = pl.pallas_call(
        body, name="dh_matmul", grid=(nt,),
        in_specs=[row(NAT), row(NQKV), ANY, ANY] + [ANY] * n,
        out_specs=[row(D)] + [ANY] * n,
        out_shape=[jax.ShapeDtypeStruct((S, D), BF16)] + [jax.ShapeDtypeStruct(p.shape, p.dtype) for p in parts],
        scratch_shapes=[pltpu.VMEM((D, NAT), BF16), pltpu.VMEM((D, NQKV), BF16)]
                       + ([pltpu.SemaphoreType.DMA((n, 3)), pltpu.SemaphoreType.DMA((n, 3))] if n else []),
        compiler_params=_params(vmem=V7X_VMEM_LIMIT_RESIDENT),
    )(dpn, dpq, w_nat, w_qkv, *parts)
    return outs[0], outs[1:]


def _norm_bwd(dh, x, dout, mod, norm_w, *, tm=512):
    S, D = x.shape

    def body(dh_ref, x_ref, dout_ref, mod_ref, nw_ref, gx_ref, st_ref):
        @pl.when(pl.program_id(0) == 0)
        def _():
            st_ref[...] = jnp.zeros_like(st_ref)

        dh = dh_ref[...].astype(F32)
        scale1 = 1.0 + mod_ref[:, D:2 * D]
        nw = nw_ref[...]
        xv = x_ref[...]
        rstd = lax.rsqrt(jnp.mean(xv * xv, axis=-1, keepdims=True) + EPS)
        xh = xv * rstd
        dhs = dh * scale1
        dxh = dhs * nw
        dx = rstd * (dxh - xh * jnp.mean(dxh * xh, axis=-1, keepdims=True))
        gx_ref[...] = dout_ref[...].astype(F32) + dx
        st_ref[0:1, :] += jnp.sum(dh, axis=0, keepdims=True)
        st_ref[1:2, :] += jnp.sum(dh * (xh * nw), axis=0, keepdims=True)
        st_ref[2:3, :] += jnp.sum(dhs * xh, axis=0, keepdims=True)

    row = pl.BlockSpec((tm, D), lambda i: (i, 0))
    return pl.pallas_call(
        body, name="norm_bwd", grid=(S // tm,),
        in_specs=[row, row, row, pl.BlockSpec((1, 3 * D), lambda i: (0, 0)), pl.BlockSpec((1, D), lambda i: (0, 0))],
        out_specs=[row, pl.BlockSpec((8, D), lambda i: (0, 0))],
        out_shape=[jax.ShapeDtypeStruct((S, D), F32), jax.ShapeDtypeStruct((8, D), F32)],
        compiler_params=_params(),
    )(dh, x, dout, mod, norm_w)


def _pack_smalls(st, vacc, gqk):
    D = D_MODEL

    def body(st_ref, va_ref, gqk_ref, o_ref):
        o_ref[...] = jnp.zeros_like(o_ref)
        o_ref[0:2, :] = st_ref[0:2, :]
        o_ref[2:3, :] = va_ref[0:1, :]
        o_ref[8:9, :] = st_ref[2:3, :]
        o_ref[16:19, :] = va_ref[1:4, :]
        tot = gqk_ref[0]
        for k in range(1, gqk_ref.shape[0]):
            tot = tot + gqk_ref[k]
        tot = tot + pltpu.roll(tot, HEAD_DIM, 1)
        o_ref[24:32, 0:128] = tot
        loss = jnp.sum(va_ref[4:5, :], axis=1, keepdims=True) * (0.5 / D)
        o_ref[26:27, :] = jnp.broadcast_to(loss, (1, D))

    return pl.pallas_call(
        body, name="pack_smalls", out_shape=jax.ShapeDtypeStruct((32, D), F32),
        in_specs=[pl.BlockSpec(memory_space=pltpu.VMEM)] * 3,
        out_specs=pl.BlockSpec(memory_space=pltpu.VMEM), compiler_params=_params(),
    )(st, vacc, gqk)


def _small_update(sm_loc, dm_pad, ct_pad, ada, vecs):
    D = D_MODEL
    nv = len(vecs)
    places = [(0, 3, D), (8, 1, D), (16, 3, 256), (24, 1, HEAD_DIM), (25, 1, HEAD_DIM)]

    def body(*refs):
        sm_ref, dm_ref, ct_ref = refs[0:3]
        wmv = [refs[3 + 3 * t:6 + 3 * t] for t in range(nv + 1)]
        outs = refs[3 + 3 * (nv + 1):]
        res = [outs[4 * t:4 * t + 4] for t in range(nv + 1)]
        loss_ref, tot_s = outs[4 * (nv + 1)], outs[4 * (nv + 1) + 1]

        def update(g, w, m, v, out, at=(slice(None), slice(None))):
            d, mn, vn = _adamw_math(w[at], g, m[at], v[at])
            for ref, val in zip(out, (g, d, mn, vn)):
                ref[at] = val

        gw = jnp.dot(ct_ref[...], dm_ref[...], preferred_element_type=F32, precision=lax.Precision.HIGHEST)
        update(gw, *wmv[0], res[0])
        tot = sm_ref[0]
        for k in range(1, 8):
            tot = tot + sm_ref[k]
        tot_s[...] = tot
        for k in range(3):
            update(tot_s[k:k + 1, :], *wmv[1], res[1], at=(slice(None), slice(k * D, (k + 1) * D)))
        for t in range(1, nv):
            r0, nr, nl = places[t]
            update(tot_s[r0:r0 + nr, 0:nl], *wmv[1 + t], res[1 + t])
        loss_ref[...] = jnp.broadcast_to(tot_s[26:27, 0:128], (8, 128))

    vm = pl.BlockSpec(memory_space=pltpu.VMEM)
    groups = [ada] + list(vecs)
    out_shape = []
    for w, _, _ in groups:
        out_shape += [jax.ShapeDtypeStruct(w.shape, F32)] * 4
    out_shape.append(jax.ShapeDtypeStruct((8, 128), F32))
    flat = [a for grp in groups for a in grp]
    outs = pl.pallas_call(
        body, name="small_update", out_shape=out_shape,
        in_specs=[vm] * (3 + len(flat)), out_specs=[vm] * len(out_shape),
        scratch_shapes=[pltpu.VMEM((32, D), F32)], compiler_params=_params(),
    )(sm_loc, dm_pad, ct_pad, *flat)
    return [outs[4 * t:4 * t + 4] for t in range(nv + 1)], outs[-1]


def _later_weights(gathered, shards, chip):
    D = D_MODEL
    g_bc, g_ba, g_wo = [lax.dynamic_update_slice(g, s[None], (chip, 0, 0, 0)) for g, s in zip(gathered, shards)]
    wa = g_ba.reshape(4, ATTN_OUT, 256).transpose(1, 0, 2).reshape(ATTN_OUT, D)
    return g_bc.reshape(D, D), wa, g_wo.reshape(D, D)


def _local_step(xs, tg, mod, norm_w, w_nat, w_qkv, later, convw8, q_norm_w, k_norm_w):
    S = xs.shape[0]
    lane = jnp.arange(128)
    ebd128 = (lane[:, None] // HEAD_DIM == lane[None, :] // HEAD_DIM).astype(BF16)
    lane5 = jnp.arange(ATTN_OUT)
    ebd512 = (lane5[:, None] // HEAD_DIM == lane5[None, :] // HEAD_DIM).astype(BF16)
    qnw2 = jnp.tile(q_norm_w, (1, 2))
    knw2 = jnp.tile(k_norm_w, (1, 2))

    projn, h, ht = _inproj_nat(xs, mod, norm_w, w_nat)
    if len(later) == 2:
        projq, gathered = _inproj_qkv(h, w_qkv, later[0])
        wc, wa, wo = _later_weights(gathered, *later)
    else:
        projq, _ = _inproj_qkv(h, w_qkv, [])
        wc, wa, wo = later
    o_g, lse_g = [], []
    for g in range(N_GROUPS):
        o, l = _attn_fwd(projq, qnw2, knw2, ebd128, g)
        o_g.append(o)
        lse_g.append(l)

    dpn, d_o, lse, delta, dout, vacc, gwo, gwc, gwa = _mid(
        projn, o_g, lse_g, xs, tg, mod, convw8, wc, wa, wo, ebd512)
    dpq = lax.empty((S, NQKV), BF16)
    gqk = []
    for g in range(N_GROUPS):
        dpq, part = _attn_bwd(projq, dpq, d_o, lse, delta, qnw2, knw2, ebd128, g)
        gqk.append(part)
    return dpn, dpq, dout, ht, gwc, gwa, gwo, vacc, gqk


def kernel(x, c, w_ada, b_ada, norm_w, w_in, conv_w, q_norm_w, k_norm_w, w_br_conv, w_br_attn, w_out, loss_target, m_w_ada, m_b_ada, m_norm_w, m_w_in, m_conv_w, m_q_norm_w, m_k_norm_w, m_w_br_conv, m_w_br_attn, m_w_out, v_w_ada, v_b_ada, v_norm_w, v_w_in, v_conv_w, v_q_norm_w, v_k_norm_w, v_w_br_conv, v_w_br_attn, v_w_out):
    D = D_MODEL
    S = x.shape[1]
    xs, tg = x[0], loss_target[0]
    mx, my, mc = lax.axis_index("x"), lax.axis_index("y"), lax.axis_index("c")
    chip = 2 * mx + my
    dev = 2 * chip + mc

    c_blk = jnp.concatenate([c, jnp.pad(conv_w[0], ((0, 0), (0, D - 256))), jnp.zeros((4, D), F32)], axis=0)
    b_shard = lax.dynamic_slice(b_ada, (0, chip * 768), (1, 768))
    shards = [w_in[0].astype(BF16).reshape(2, 512, 2816), w_br_conv[0].astype(BF16).reshape(2, 128, D),
              w_br_attn[0].astype(BF16).reshape(2, 256, 256), w_out[0].astype(BF16).reshape(2, 128, D)]
    gathered_in, _, cg, mod_parts, silu_c = _weights_allgather(
        shards[0].reshape(2, 2, 256, SHARD_COLS), [], c_blk, w_ada[0], b_shard)
    convw8 = jnp.pad(cg[::2, 1:4, 0:256].transpose(1, 0, 2).reshape(3, D), ((0, 5), (0, 0)))
    mod_all = mod_parts.transpose(1, 0, 2).reshape(8, 3 * D)
    mod = lax.dynamic_slice(mod_all, (dev, 0), (1, 3 * D))
    chipv = jnp.reshape(chip, (1,)).astype(jnp.int32)
    g_in = gathered_in.reshape(4, D, SHARD_COLS)
    own_in = shards[0].reshape(D, SHARD_COLS)
    w_nat = _weights_prep(g_in, own_in, chipv, "nat")
    w_qkv = _weights_prep(g_in, own_in, chipv, "qkv")

    dpn, dpq, dout, ht, gwc, gwa, gwo, vacc, gqk = _local_step(
        xs, tg, mod, norm_w, w_nat, w_qkv, (shards[1:], chip), convw8, q_norm_w, k_norm_w)

    def halves(a):
        k, r, cc = a.shape
        return a.reshape(k, 2, r // 2, cc).transpose(1, 0, 2, 3)

    gw_nat = _matmul_lhs_resident(ht, dpn, tn=512, name="grad_w_in_nat").reshape(2, D // 2, NAT)
    gw_qkv, recv_nat = _matmul_lhs_resident(ht, dpq, tn=512, name="grad_w_in_qkv", exchange=gw_nat)
    grads = [gw_nat, gw_qkv.reshape(2, D // 2, NQKV),
             halves(gwc.astype(BF16).reshape(4, 256, D)),
             halves(gwa.astype(BF16).reshape(ATTN_OUT, 4, 256).transpose(1, 0, 2)),
             halves(gwo.astype(BF16).reshape(4, 256, D))]
    recv = [recv_nat] + list(_pair_exchange(grads[1:]))
    core = jnp.reshape(mc, (1,)).astype(jnp.int32)
    p_in = _pair_add_to_shards(core, grads[0], recv[0], lax.empty((4, D // 2, SHARD_COLS), BF16), "nat")
    p_in = _pair_add_to_shards(core, grads[1], recv[1], p_in, "qkv")
    parts = [p_in] + [_pair_add(core, gr, rc) for gr, rc in zip(grads[2:], recv[2:])]
    dh, gathered = _dh_matmul(dpn, dpq, w_nat, w_qkv, parts)
    grad_x, st = _norm_bwd(dh, xs, dout, mod, norm_w)
    mine = [_sum4(chipv, p, q) for p, q in zip(parts, gathered)]
    theirs, sm_all = _half_exchange(mine, _pack_smalls(st, vacc, jnp.concatenate(gqk, axis=0)))

    big = {}
    for t, (nm, w, m, v) in enumerate((("w_in", w_in, m_w_in, v_w_in), ("w_br_conv", w_br_conv, m_w_br_conv, v_w_br_conv),
                                       ("w_br_attn", w_br_attn, m_w_br_attn, v_w_br_attn), ("w_out", w_out, m_w_out, v_w_out))):
        big[nm] = _adamw_halves(core, w[0], mine[t], theirs[t], m[0], v[0], "adamw_" + nm)

    dmod_all = sm_all[:, 0:3, :].reshape(8, 3 * D)
    dm_pad = jnp.pad(lax.dynamic_slice(dmod_all, (0, chip * 768), (8, 768)), ((0, 120), (0, 0)))
    conv_dev = lax.dynamic_slice(sm_all[:, 16:24, :], (0, 0, chip * 256), (8, 8, 256))
    sm_loc = jnp.concatenate([sm_all[:, 0:16, :], jnp.pad(conv_dev, ((0, 0), (0, 0), (0, D - 256))), sm_all[:, 24:32, :]], axis=1)

    ct_pad = jnp.pad(silu_c.T, ((0, 0), (0, 120)))
    small_names = ["b_ada", "norm_w", "conv_w", "q_norm_w", "k_norm_w"]
    vecs = [(b_ada, m_b_ada, v_b_ada), (norm_w, m_norm_w, v_norm_w), (conv_w[0], m_conv_w[0], v_conv_w[0]),
            (q_norm_w, m_q_norm_w, v_q_norm_w), (k_norm_w, m_k_norm_w, v_k_norm_w)]
    updated, loss_blk = _small_update(sm_loc, dm_pad, ct_pad, (w_ada[0], m_w_ada[0], v_w_ada[0]), vecs)
    small = {"w_ada": [a[None] for a in updated[0]]}
    for nm, four in zip(small_names, updated[1:]):
        small[nm] = [a[None] for a in four] if nm == "conv_w" else list(four)

    order = ["w_ada", "b_ada", "norm_w", "w_in", "conv_w", "q_norm_w", "k_norm_w", "w_br_conv", "w_br_attn", "w_out"]
    res = [loss_blk[0, 0], grad_x[None]]
    for kind in range(4):
        for nm in order:
            res.append(big[nm][kind][None] if nm in big else small[nm][kind])
    return tuple(res)
```

```python
import jax
import jax.numpy as jnp
from jax import lax
from jax.experimental import pallas as pl
from jax.experimental.pallas import tpu as pltpu

F32 = jnp.float32
BF16 = jnp.bfloat16
MESH = pl.DeviceIdType.MESH
ANY = pl.BlockSpec(memory_space=pl.ANY)

D_MODEL = 1024
HEAD_DIM = 64
N_GROUPS = 3
DILATIONS = (1, 4, 16)
ATTN_OUT = 512
EPS = 1e-6
NEG_INF = -1e30
NAT = 6656
NQKV = 4608
BA, CA, XA, ZA, ZB, GA, GB = 0, 1024, 2048, 3072, 4096, 4608, 5632
ATT_TILE = 2048
QK_SCALE = HEAD_DIM ** -0.5
LOG2E = 1.4426950408889634
LN2 = 0.6931471805599453
V7X_VMEM_LIMIT = 56 * 1024 * 1024
V7X_VMEM_LIMIT_RESIDENT = 60 * 1024 * 1024

ADAM_LR = 0.001
ADAM_B1 = 0.9
ADAM_B2 = 0.999
ADAM_EPS = 1e-08
ADAM_WD = 0.01
ADAM_STEP = 10


def _params(vmem=V7X_VMEM_LIMIT, **kw):
    return pltpu.CompilerParams(vmem_limit_bytes=vmem, **kw)


def _sigmoid(z):
    return 1.0 / (1.0 + jnp.exp(-z))


def _row(v, r):
    rows = lax.broadcasted_iota(jnp.int32, v.shape, 0)
    return jnp.sum(jnp.where(rows == r, v, 0.0), axis=0, keepdims=True)


def _coords():
    return lax.axis_index("x"), lax.axis_index("y"), lax.axis_index("c")


def _flip(v, bit):
    return 1 - v if bit else v


def _weights_allgather(big_shard, shards, c_blk, w_ada, b_shard):
    n = len(shards)
    D = D_MODEL
    cols = w_ada.shape[1]

    def body(*refs):
        big_in, ins = refs[0], refs[1:1 + n]
        c_ref, wada_ref, b_ref = refs[1 + n:4 + n]
        big_out, outs = refs[4 + n], refs[5 + n:5 + 2 * n]
        cg_ref, mp_ref, sc_ref = refs[5 + 2 * n:8 + 2 * n]
        sem_refs = refs[8 + 2 * n:]
        ssem, rsem = sem_refs[:2] if n else (None, None)
        bs, br, cs, cr, ms, mr, lsem = sem_refs[2 if n else 0:]
        mx, my, mc = _coords()
        me = 2 * mx + my
        me8 = 2 * me + mc
        sib = (mx, my, 1 - mc)
        chips = [(_flip(mx, j & 2), _flip(my, j & 1)) for j in range(1, 4)]
        sends = []

        local = pltpu.make_async_copy(c_ref, cg_ref.at[me8], lsem)
        local.start()
        for j in range(1, 8):
            peer = (_flip(mx, j & 4), _flip(my, j & 2), _flip(mc, j & 1))
            cp = pltpu.make_async_remote_copy(
                src_ref=c_ref, dst_ref=cg_ref.at[me8], send_sem=cs.at[j - 1], recv_sem=cr.at[j - 1],
                device_id=peer, device_id_type=MESH)
            cp.start()
            sends.append(cp)

        kx, ky, kd = 2 * (1 - mx) + my, 2 * mx + (1 - my), 2 * (1 - mx) + (1 - my)
        xn, yn = (1 - mx, my, mc), (mx, 1 - my, mc)

        def big(src_chip, q, sem, to, half=None):
            h = mc if half is None else half
            return pltpu.make_async_remote_copy(
                src_ref=big_out.at[src_chip, h, q], dst_ref=big_out.at[src_chip, h, q],
                send_sem=bs.at[sem], recv_sem=br.at[sem], device_id=to, device_id_type=MESH)

        def own(q, sem, to):
            return pltpu.make_async_remote_copy(
                src_ref=big_in.at[mc, q], dst_ref=big_out.at[me, mc, q],
                send_sem=bs.at[sem], recv_sem=br.at[sem], device_id=to, device_id_type=MESH)

        for cp in (own(0, 0, xn), own(1, 3, yn), own(1, 1, xn), own(0, 2, yn)):
            cp.start()
            sends.append(cp)
        for t in range(n):
            for j, (px, py) in enumerate(chips):
                cp = pltpu.make_async_remote_copy(
                    src_ref=ins[t].at[mc], dst_ref=outs[t].at[me, mc], send_sem=ssem.at[t, j],
                    recv_sem=rsem.at[t, j], device_id=(px, py, mc), device_id_type=MESH)
                cp.start()
                sends.append(cp)

        pieces = [(kx, 0), (kx, 1), (ky, 0), (ky, 1), (kd, 0), (kd, 1)]

        def landed(piece, sem, frm, pass_on=None):
            src_chip, q = pieces[piece]
            big(src_chip, q, sem, frm).wait_recv()
            nxt = ([] if pass_on is None else [big(src_chip, q, pass_on[0], pass_on[1])])
            nxt.append(big(src_chip, q, 6 + piece, sib))
            for cp in nxt:
                cp.start()
                sends.append(cp)

        for j in range(1, 8):
            px, py, pc = _flip(mx, j & 4), _flip(my, j & 2), _flip(mc, j & 1)
            pltpu.make_async_remote_copy(
                src_ref=c_ref, dst_ref=cg_ref.at[4 * px + 2 * py + pc], send_sem=cs.at[j - 1],
                recv_sem=cr.at[j - 1], device_id=(px, py, pc), device_id_type=MESH).wait_recv()
        local.wait()
        pick = (lax.broadcasted_iota(jnp.int32, (8, 64), 1) == 8 * lax.broadcasted_iota(jnp.int32, (8, 64), 0))
        cv = jnp.dot(pick.astype(F32), cg_ref[...].reshape(64, D), preferred_element_type=F32,
                     precision=lax.Precision.HIGHEST)
        sc = cv * _sigmoid(cv)
        sc_ref[...] = sc
        mp_ref[me] = jnp.dot(sc, wada_ref[...], preferred_element_type=F32,
                             precision=lax.Precision.HIGHEST) + b_ref[...]
        for j, (px, py) in enumerate(chips):
            cp = pltpu.make_async_remote_copy(
                src_ref=mp_ref.at[me], dst_ref=mp_ref.at[me], send_sem=ms.at[j], recv_sem=mr.at[j],
                device_id=(px, py, mc), device_id_type=MESH)
            cp.start()
            sends.append(cp)

        landed(0, 0, xn, pass_on=(4, yn))
        landed(3, 3, yn, pass_on=(5, xn))
        landed(1, 1, xn)
        landed(2, 2, yn)
        for j, (px, py) in enumerate(chips):
            pltpu.make_async_remote_copy(
                src_ref=mp_ref.at[me], dst_ref=mp_ref.at[2 * px + py], send_sem=ms.at[j], recv_sem=mr.at[j],
                device_id=(px, py, mc), device_id_type=MESH).wait_recv()
        landed(4, 4, yn)
        landed(5, 5, xn)
        for t in range(n):
            for j, (px, py) in enumerate(chips):
                src = 2 * px + py
                pltpu.make_async_remote_copy(
                    src_ref=ins[t].at[mc], dst_ref=outs[t].at[src, mc], send_sem=ssem.at[t, j],
                    recv_sem=rsem.at[t, j], device_id=(px, py, mc), device_id_type=MESH).wait_recv()
                fwd = pltpu.make_async_remote_copy(
                    src_ref=outs[t].at[src, mc], dst_ref=outs[t].at[src, mc], send_sem=ssem.at[t, 3 + j],
                    recv_sem=rsem.at[t, 3 + j], device_id=sib, device_id_type=MESH)
                fwd.start()
                sends.append(fwd)
        for t in range(n):
            for j, (px, py) in enumerate(chips):
                src = 2 * px + py
                pltpu.make_async_remote_copy(
                    src_ref=outs[t].at[src, 1 - mc], dst_ref=outs[t].at[src, 1 - mc], send_sem=ssem.at[t, 3 + j],
                    recv_sem=rsem.at[t, 3 + j], device_id=sib, device_id_type=MESH).wait_recv()
        for i, (src_chip, q) in enumerate(pieces):
            big(src_chip, q, 6 + i, sib, half=1 - mc).wait_recv()
        for cp in sends:
            cp.wait_send()

    vm = pl.BlockSpec(memory_space=pltpu.VMEM)
    outs = pl.pallas_call(
        body, name="weights_allgather",
        out_shape=[jax.ShapeDtypeStruct((4,) + big_shard.shape, big_shard.dtype)]
                  + [jax.ShapeDtypeStruct((4,) + s.shape, s.dtype) for s in shards]
                  + [jax.ShapeDtypeStruct((8,) + c_blk.shape, F32), jax.ShapeDtypeStruct((4, 8, cols), F32),
                     jax.ShapeDtypeStruct((8, D), F32)],
        in_specs=[ANY] * (n + 1) + [vm] * 3, out_specs=[ANY] * (n + 1) + [vm] * 3,
        scratch_shapes=([pltpu.SemaphoreType.DMA((n, 6)), pltpu.SemaphoreType.DMA((n, 6))] if n else [])
                       + [pltpu.SemaphoreType.DMA((12,)), pltpu.SemaphoreType.DMA((12,)),
                          pltpu.SemaphoreType.DMA((7,)), pltpu.SemaphoreType.DMA((7,)),
                          pltpu.SemaphoreType.DMA((3,)), pltpu.SemaphoreType.DMA((3,)), pltpu.SemaphoreType.DMA(())],
        compiler_params=_params(),
    )(big_shard, *shards, c_blk, w_ada, b_shard)
    return outs[0], outs[1:1 + n], outs[1 + n], outs[2 + n], outs[3 + n]


def _shard_gather_copies(ins, outs, ssem, rsem):
    mx, my, mc = _coords()
    me = 2 * mx + my
    sib = (mx, my, 1 - mc)
    send, recv, fwd, fwd_recv = [], [], [], []
    for t in range(len(ins)):
        for j in range(3):
            px, py = _flip(mx, (j + 1) & 2), _flip(my, (j + 1) & 1)
            src = 2 * px + py
            far = dict(send_sem=ssem.at[t, j], recv_sem=rsem.at[t, j], device_id=(px, py, mc), device_id_type=MESH)
            near = dict(send_sem=ssem.at[t, 3 + j], recv_sem=rsem.at[t, 3 + j], device_id=sib, device_id_type=MESH)
            send.append(pltpu.make_async_remote_copy(src_ref=ins[t].at[mc], dst_ref=outs[t].at[me, mc], **far))
            recv.append(pltpu.make_async_remote_copy(src_ref=ins[t].at[mc], dst_ref=outs[t].at[src, mc], **far))
            fwd.append(pltpu.make_async_remote_copy(src_ref=outs[t].at[src, mc], dst_ref=outs[t].at[src, mc], **near))
            fwd_recv.append(pltpu.make_async_remote_copy(
                src_ref=outs[t].at[src, 1 - mc], dst_ref=outs[t].at[src, 1 - mc], **near))
    return send, recv, fwd, fwd_recv


def _pair_exchange(grads):
    n = len(grads)

    def body(*refs):
        ins, outs = refs[:n], refs[n:2 * n]
        ssem, rsem = refs[2 * n:]
        mx, my, mc = _coords()
        sib = (mx, my, 1 - mc)
        sends = []
        for t in range(n):
            cp = pltpu.make_async_remote_copy(
                src_ref=ins[t].at[1 - mc], dst_ref=outs[t], send_sem=ssem.at[t], recv_sem=rsem.at[t],
                device_id=sib, device_id_type=MESH)
            cp.start()
            sends.append(cp)
        for cp in sends:
            cp.wait_recv()
        for cp in sends:
            cp.wait_send()

    return pl.pallas_call(
        body, name="grad_pair_exchange",
        out_shape=[jax.ShapeDtypeStruct(g.shape[1:], g.dtype) for g in grads],
        in_specs=[ANY] * n, out_specs=[ANY] * n,
        scratch_shapes=[pltpu.SemaphoreType.DMA((n,)), pltpu.SemaphoreType.DMA((n,))],
    )(*grads)


def _scatter_copies(ins, outs, ssem, rsem):
    mx, my, mc = _coords()
    me = 2 * mx + my
    mine, theirs = [], []
    for t in range(len(ins)):
        for j in range(1, 4):
            px, py = _flip(mx, j & 2), _flip(my, j & 1)
            mine.append(pltpu.make_async_remote_copy(
                src_ref=ins[t].at[2 * px + py], dst_ref=outs[t].at[me], send_sem=ssem.at[t, j - 1],
                recv_sem=rsem.at[t, j - 1], device_id=(px, py, mc), device_id_type=MESH))
            theirs.append(pltpu.make_async_remote_copy(
                src_ref=ins[t].at[me], dst_ref=outs[t].at[2 * px + py], send_sem=ssem.at[t, j - 1],
                recv_sem=rsem.at[t, j - 1], device_id=(px, py, mc), device_id_type=MESH))
    return mine, theirs


def _half_exchange(halves, smalls):
    n = len(halves)

    def body(*refs):
        ins, sm_ref = refs[:n], refs[n]
        outs, all_ref = refs[n + 1:2 * n + 1], refs[2 * n + 1]
        ssem, rsem, gsem, hsem, lsem = refs[2 * n + 2:]
        mx, my, mc = _coords()
        me = 4 * mx + 2 * my + mc
        sib = (mx, my, 1 - mc)
        local = pltpu.make_async_copy(sm_ref, all_ref.at[me], lsem)
        local.start()
        sends = []
        for t in range(n):
            rc = pltpu.make_async_remote_copy(
                src_ref=ins[t], dst_ref=outs[t], send_sem=ssem.at[t], recv_sem=rsem.at[t],
                device_id=sib, device_id_type=MESH)
            rc.start()
            sends.append(rc)
        gathers = []
        for j in range(1, 8):
            peer = (_flip(mx, j & 4), _flip(my, j & 2), _flip(mc, j & 1))
            cp = pltpu.make_async_remote_copy(
                src_ref=sm_ref, dst_ref=all_ref.at[me], send_sem=gsem.at[j - 1], recv_sem=hsem.at[j - 1],
                device_id=peer, device_id_type=MESH)
            cp.start()
            gathers.append(cp)
        for cp in sends:
            cp.wait_recv()
        for j in range(1, 8):
            px, py, pc = _flip(mx, j & 4), _flip(my, j & 2), _flip(mc, j & 1)
            pltpu.make_async_remote_copy(
                src_ref=sm_ref, dst_ref=all_ref.at[4 * px + 2 * py + pc], send_sem=gsem.at[j - 1],
                recv_sem=hsem.at[j - 1], device_id=(px, py, pc), device_id_type=MESH).wait_recv()
        for cp in sends + gathers:
            cp.wait_send()
        local.wait()

    vm = pl.BlockSpec(memory_space=pltpu.VMEM)
    outs = pl.pallas_call(
        body, name="grad_half_exchange",
        out_shape=[jax.ShapeDtypeStruct(h.shape, h.dtype) for h in halves]
                  + [jax.ShapeDtypeStruct((8,) + smalls.shape, smalls.dtype)],
        in_specs=[ANY] * n + [vm], out_specs=[ANY] * n + [vm],
        scratch_shapes=[pltpu.SemaphoreType.DMA((n,)), pltpu.SemaphoreType.DMA((n,)),
                        pltpu.SemaphoreType.DMA((7,)), pltpu.SemaphoreType.DMA((7,)), pltpu.SemaphoreType.DMA(())],
    )(*halves, smalls)
    return outs[:n], outs[n]


def _row_tile(rows, cols, bytes_per_row_elem=4, budget=2 * 1024 * 1024):
    t = rows
    while t % 2 == 0 and t > 16 and t * cols * bytes_per_row_elem > budget:
        t //= 2
    return t


def _pair_add(core, g, r):
    _, _, rh, cc = g.shape
    tr = _row_tile(rh, cc)

    def body(core_ref, g_ref, r_ref, o_ref):
        o_ref[...] = (g_ref[...].astype(F32) + r_ref[...].astype(F32)).astype(o_ref.dtype)

    return pl.pallas_call(
        body, name="grad_pair_add",
        grid_spec=pltpu.PrefetchScalarGridSpec(
            num_scalar_prefetch=1, grid=(4, rh // tr),
            in_specs=[pl.BlockSpec((None, None, tr, cc), lambda k, i, c: (c[0], k, i, 0)),
                      pl.BlockSpec((None, tr, cc), lambda k, i, c: (k, i, 0))],
            out_specs=pl.BlockSpec((None, tr, cc), lambda k, i, c: (k, i, 0))),
        out_shape=jax.ShapeDtypeStruct(r.shape, BF16),
        compiler_params=_params(),
    )(core, g, r)


def _sum4(chip, p, q):
    _, rh, cc = q.shape
    tr = _row_tile(rh, cc)

    def body(chip_ref, p_ref, q1_ref, q2_ref, q3_ref, o_ref):
        o_ref[...] = ((p_ref[...].astype(F32) + q1_ref[...].astype(F32)) + q2_ref[...].astype(F32)) + q3_ref[...].astype(F32)

    def other(j):
        return pl.BlockSpec((None, tr, cc), lambda i, c: (jnp.bitwise_xor(c[0], j), i, 0))

    return pl.pallas_call(
        body, name="grad_sum4",
        grid_spec=pltpu.PrefetchScalarGridSpec(
            num_scalar_prefetch=1, grid=(rh // tr,),
            in_specs=[pl.BlockSpec((None, tr, cc), lambda i, c: (c[0], i, 0)), other(1), other(2), other(3)],
            out_specs=pl.BlockSpec((tr, cc), lambda i, c: (i, 0))),
        out_shape=jax.ShapeDtypeStruct((rh, cc), F32),
        compiler_params=_params(),
    )(chip, p, q, q, q)


def _adamw_math(w, g, m, v):
    m = ADAM_B1 * m + (1.0 - ADAM_B1) * g
    v = ADAM_B2 * v + (1.0 - ADAM_B2) * (g * g)
    m_hat = m / (1.0 - ADAM_B1 ** ADAM_STEP)
    v_hat = v / (1.0 - ADAM_B2 ** ADAM_STEP)
    delta = -ADAM_LR * (m_hat / (jnp.sqrt(v_hat) + ADAM_EPS) + ADAM_WD * w)
    return delta, m, v


def _adamw_halves(core, w, mine, recv, m, v, name):
    rows, cols = w.shape
    rh = rows // 2
    tr = _row_tile(rh, cols, budget=1024 * 1024)
    nh = rh // tr

    def body(core_ref, w_ref, a_ref, b_ref, m_ref, v_ref, g_ref, d_ref, mo_ref, vo_ref):
        g = jnp.where(pl.program_id(0) == core_ref[0], a_ref[...], b_ref[...])
        d, mn, vn = _adamw_math(w_ref[...], g, m_ref[...], v_ref[...])
        g_ref[...] = g
        d_ref[...] = d
        mo_ref[...] = mn
        vo_ref[...] = vn

    full = pl.BlockSpec((tr, cols), lambda h, i, c: (h * nh + i, 0))
    mine_spec = pl.BlockSpec((tr, cols), lambda h, i, c: (jnp.where(h == c[0], i, 0), 0))
    recv_spec = pl.BlockSpec((tr, cols), lambda h, i, c: (jnp.where(h == c[0], 0, i), 0))
    sd = jax.ShapeDtypeStruct((rows, cols), F32)
    return pl.pallas_call(
        body, name=name,
        grid_spec=pltpu.PrefetchScalarGridSpec(
            num_scalar_prefetch=1, grid=(2, nh),
            in_specs=[full, mine_spec, recv_spec, full, full], out_specs=[full] * 4),
        out_shape=[sd, sd, sd, sd], compiler_params=_params(),
    )(core, w, mine, recv, m, v)


def _inproj_nat(x, mod, norm_w, w, *, tm=1024, tn=1664):
    S, D = x.shape
    N = w.shape[1]
    nt, nj = S // tm, N // tn
    chunk = 256

    def body(x_ref, mod_ref, nw_ref, w_ref, proj_ref, h_ref, ht_ref, h_a, ht_a, h_b, ht_b):
        i, j = pl.program_id(0), pl.program_id(1)

        def make_h(h_s, ht_s):
            shift = mod_ref[:, 0:D]
            scale1 = 1.0 + mod_ref[:, D:2 * D]
            for r in range(tm // chunk):
                xv = x_ref[pl.ds(r * chunk, chunk), :]
                ms = jnp.mean(xv * xv, axis=-1, keepdims=True)
                h = (xv * lax.rsqrt(ms + EPS) * nw_ref[...]) * scale1 + shift
                h_s[pl.ds(r * chunk, chunk), :] = h.astype(BF16)
                ht_s[:, pl.ds(r * chunk, chunk)] = h.T.astype(BF16)

        @pl.when((i == 0) & (j == 0))
        def _():
            make_h(h_a, ht_a)

        def step(cur, nxt):
            @pl.when(j == 0)
            def _():
                h_ref[...] = cur[0][...]
                ht_ref[...] = cur[1][...]

            def project():
                proj_ref[...] = jnp.dot(cur[0][...], w_ref[...], preferred_element_type=F32).astype(BF16)

            ahead = (j == nj - 1) & (i < nt - 1)

            @pl.when(ahead)
            def _():
                project()
                make_h(*nxt)

            @pl.when(jnp.logical_not(ahead))
            def _():
                project()

        @pl.when(i % 2 == 0)
        def _():
            step((h_a, ht_a), (h_b, ht_b))

        @pl.when(i % 2 == 1)
        def _():
            step((h_b, ht_b), (h_a, ht_a))

    def x_tile(i, j):
        return (jnp.where((i == 0) & (j == 0), 0, jnp.minimum(i + 1, nt - 1)), 0)

    return pl.pallas_call(
        body, name="inproj_nat", grid=(nt, nj),
        in_specs=[pl.BlockSpec((tm, D), x_tile),
                  pl.BlockSpec((1, 3 * D), lambda i, j: (0, 0)),
                  pl.BlockSpec((1, D), lambda i, j: (0, 0)),
                  pl.BlockSpec((D, tn), lambda i, j: (0, j))],
        out_specs=[pl.BlockSpec((tm, tn), lambda i, j: (i, j)),
                   pl.BlockSpec((tm, D), lambda i, j: (i, 0)),
                   pl.BlockSpec((D, tm), lambda i, j: (0, i))],
        out_shape=[jax.ShapeDtypeStruct((S, N), BF16), jax.ShapeDtypeStruct((S, D), BF16),
                   jax.ShapeDtypeStruct((D, S), BF16)],
        scratch_shapes=[pltpu.VMEM((tm, D), BF16), pltpu.VMEM((D, tm), BF16),
                        pltpu.VMEM((tm, D), BF16), pltpu.VMEM((D, tm), BF16)],
        compiler_params=_params(),
    )(x, mod, norm_w, w)


def _inproj_qkv(a, b, shards, *, tm=1024, tn=1536):
    M, K = a.shape
    N = b.shape[1]
    n = len(shards)
    ni, nj = M // tm, N // tn

    def body(a_ref, b_ref, *rest):
        ins, o_ref, outs, sems = rest[:n], rest[n], rest[n + 1:2 * n + 1], rest[2 * n + 1:]
        i, j = pl.program_id(0), pl.program_id(1)
        if n:
            @pl.when((i == 0) & (j == 0))
            def _():
                for cp in _shard_gather_copies(ins, outs, *sems)[0]:
                    cp.start()

        o_ref[...] = jnp.dot(a_ref[...], b_ref[...], preferred_element_type=F32).astype(o_ref.dtype)

        if n:
            @pl.when((i == ni - 1) & (j == 0))
            def _():
                _, recv, fwd, _ = _shard_gather_copies(ins, outs, *sems)
                for r, f in zip(recv, fwd):
                    r.wait_recv()
                    f.start()

            @pl.when((i == ni - 1) & (j == nj - 1))
            def _():
                send, _, fwd, fwd_recv = _shard_gather_copies(ins, outs, *sems)
                for r in fwd_recv:
                    r.wait_recv()
                for cp in send + fwd:
                    cp.wait_send()

    outs = pl.pallas_call(
        body, name="inproj_qkv", grid=(ni, nj),
        in_specs=[pl.BlockSpec((tm, K), lambda i, j: (i, 0)), pl.BlockSpec((K, tn), lambda i, j: (0, j))] + [ANY] * n,
        out_specs=[pl.BlockSpec((tm, tn), lambda i, j: (i, j))] + [ANY] * n,
        out_shape=[jax.ShapeDtypeStruct((M, N), BF16)] + [jax.ShapeDtypeStruct((4,) + s.shape, s.dtype) for s in shards],
        scratch_shapes=[pltpu.SemaphoreType.DMA((n, 6)), pltpu.SemaphoreType.DMA((n, 6))] if n else [],
        compiler_params=_params(),
    )(a, b, *shards)
    return outs[0], outs[1:]


def _matmul_lhs_resident(a, b, *, tn, name, exchange=None):
    M, K = a.shape
    N = b.shape[1]
    nj = N // tn
    hosts = exchange is not None

    def body(a_hbm, b_ref, *rest):
        if hosts:
            x_hbm, o_ref, r_hbm, a_s, ssem, rsem = rest
        else:
            o_ref, a_s = rest
        j = pl.program_id(0)

        def to_sibling():
            mx, my, mc = _coords()
            return pltpu.make_async_remote_copy(src_ref=x_hbm.at[1 - mc], dst_ref=r_hbm, send_sem=ssem, recv_sem=rsem,
                                                device_id=(mx, my, 1 - mc), device_id_type=MESH)

        @pl.when(j == 0)
        def _():
            if hosts:
                to_sibling().start()
            pltpu.sync_copy(a_hbm, a_s)

        o_ref[...] = jnp.dot(a_s[...], b_ref[...], preferred_element_type=F32).astype(o_ref.dtype)

        if hosts:
            @pl.when(j == nj - 1)
            def _():
                cp = to_sibling()
                cp.wait_recv()
                cp.wait_send()

    prod = jax.ShapeDtypeStruct((M, N), BF16)
    out = pl.pallas_call(
        body, name=name, grid=(nj,),
        in_specs=[ANY, pl.BlockSpec((K, tn), lambda j: (0, j))] + ([ANY] if hosts else []),
        out_specs=[pl.BlockSpec((M, tn), lambda j: (0, j)), ANY] if hosts else pl.BlockSpec((M, tn), lambda j: (0, j)),
        out_shape=[prod, jax.ShapeDtypeStruct(exchange.shape[1:], exchange.dtype)] if hosts else prod,
        scratch_shapes=[pltpu.VMEM((M, K), BF16)]
                       + ([pltpu.SemaphoreType.DMA(()), pltpu.SemaphoreType.DMA(())] if hosts else []),
        compiler_params=_params(),
    )(a, b, *([exchange] if hosts else []))
    return out


SHARD_COLS = 2816


def _column_tables(part):
    if part == "nat":
        bw = 256
        orig = [j * bw if j < 16 else 8704 + (j - 16) * bw for j in range(NAT // bw)]
    else:
        bw = 128
        orig = []
        for jj in range(NQKV // bw):
            g, p, t = jj // 12, (jj % 12) // 3, jj % 3
            orig.append(4096 + t * 1536 + g * 512 + p * 128)
    tk = jnp.array([o // SHARD_COLS for o in orig], jnp.int32)
    tc = jnp.array([(o % SHARD_COLS) // bw for o in orig], jnp.int32)
    return tk, tc, bw


def _weights_prep(gath, own, chipv, part):
    D = D_MODEL
    tk, tc, bw = _column_tables(part)
    n = tk.shape[0]
    per_step = 2 if part == "nat" else 4
    assert n % per_step == 0

    def body(tk_ref, tc_ref, chip_ref, *refs):
        w_ref = refs[-1]
        for u in range(per_step):
            g_ref, o_ref = refs[2 * u], refs[2 * u + 1]
            mine = tk_ref[pl.program_id(0) * per_step + u] == chip_ref[0]
            w_ref[:, u * bw:(u + 1) * bw] = jnp.where(mine, o_ref[...], g_ref[...])

    def gath_spec(u):
        def idx(j, tk, tc, ch):
            k = tk[j * per_step + u]
            return (jnp.where(k == ch[0], (k + 1) % 4, k), 0, tc[j * per_step + u])
        return pl.BlockSpec((None, D, bw), idx)

    def own_spec(u):
        return pl.BlockSpec((D, bw), lambda j, tk, tc, ch: (0, tc[j * per_step + u]))

    specs = []
    for u in range(per_step):
        specs += [gath_spec(u), own_spec(u)]
    return pl.pallas_call(
        body, name="weights_prep_" + part,
        grid_spec=pltpu.PrefetchScalarGridSpec(
            num_scalar_prefetch=3, grid=(n // per_step,),
            in_specs=specs,
            out_specs=pl.BlockSpec((D, per_step * bw), lambda j, tk, tc, ch: (0, j))),
        out_shape=jax.ShapeDtypeStruct((D, n * bw), BF16),
        compiler_params=_params(),
    )(tk, tc, chipv, *([gath, own] * per_step))


def _pair_add_to_shards(core, gw, r, into, part):
    D = D_MODEL
    tk, tc, bw = _column_tables(part)
    n = tk.shape[0]

    def body(tk_ref, tc_ref, core_ref, g_ref, r_ref, into_ref, o_ref):
        o_ref[...] = (g_ref[...].astype(F32) + r_ref[...].astype(F32)).astype(BF16)

    return pl.pallas_call(
        body, name="grad_pair_add_" + part,
        grid_spec=pltpu.PrefetchScalarGridSpec(
            num_scalar_prefetch=3, grid=(n,),
            in_specs=[pl.BlockSpec((None, D // 2, bw), lambda j, tk, tc, c: (c[0], 0, j)),
                      pl.BlockSpec((D // 2, bw), lambda j, tk, tc, c: (0, j)), ANY],
            out_specs=pl.BlockSpec((None, D // 2, bw), lambda j, tk, tc, c: (tk[j], 0, tc[j]))),
        out_shape=jax.ShapeDtypeStruct(into.shape, BF16),
        input_output_aliases={5: 0},
        compiler_params=_params(),
    )(tk, tc, core, gw, r, into)


def _head_norm(xb, w, ebd):
    x = xb.astype(F32)
    ss = jnp.dot((x * x).astype(BF16), ebd, preferred_element_type=F32)
    return x * lax.rsqrt(ss * (1.0 / HEAD_DIM) + EPS) * w


def _window_mask(no_prev):
    qi = lax.broadcasted_iota(jnp.int32, (128, 256), 0)
    kc = lax.broadcasted_iota(jnp.int32, (128, 256), 1)
    ok = (kc >= qi) & (kc <= qi + 128)
    if no_prev is not None:
        ok = ok & ((kc >= 128) | jnp.logical_not(no_prev))
    return ok


def _lane_masks():
    lane = lax.broadcasted_iota(jnp.int32, (1, 128), 1)
    return [(lane < HEAD_DIM).astype(F32), (lane >= HEAD_DIM).astype(F32)]


def _head_col(v, mh):
    return jnp.sum(v * mh, axis=1, keepdims=True) * (1.0 / HEAD_DIM)


def _attn_fwd(projq, qnw2, knw2, ebd, g):
    S = projq.shape[0]
    d = DILATIONS[g]
    T = ATT_TILE
    nt = S // T
    nb = T // (128 * d)
    sdt = BF16 if d == 1 else F32

    def body(cur_ref, qw_ref, kw_ref, ebd_ref, o_ref, lse_ref, qs_s, ks_s, vs_s):
        t = pl.program_id(1)
        e = ebd_ref[...]

        @pl.when(t == 0)
        def _():
            ks_s[pl.ds(0, T), :] = jnp.zeros((T, 128), sdt)
            vs_s[pl.ds(0, T), :] = jnp.zeros((T, 128), sdt)

        @pl.when(t > 0)
        def _():
            ks_s[pl.ds(0, T), :] = ks_s[pl.ds(T, T), :]
            vs_s[pl.ds(0, T), :] = vs_s[pl.ds(T, T), :]

        qs_s[...] = (_head_norm(cur_ref[:, 0:128], qw_ref[...], e) * (QK_SCALE * LOG2E)).astype(sdt)
        ks_s[pl.ds(T, T), :] = _head_norm(cur_ref[:, 128:256], kw_ref[...], e).astype(sdt)
        vs_s[pl.ds(T, T), :] = cur_ref[:, 256:384].astype(sdt)
        m0, m1 = _lane_masks()
        first = t == 0
        for r in range(d):
            for bb in range(nb):
                q0 = r + bb * 128 * d
                k0 = T + r + (bb - 1) * 128 * d
                qs = qs_s[pl.ds(q0, 128, stride=d), :].astype(F32)
                kb = ks_s[pl.ds(k0, 256, stride=d), :].astype(BF16)
                vb = vs_s[pl.ds(k0, 256, stride=d), :].astype(BF16)
                ok = _window_mask(first if bb == 0 else None)
                ok2 = jnp.concatenate([ok, ok], axis=0)
                q2 = jnp.concatenate([qs * m0, qs * m1], axis=0).astype(BF16)
                s = lax.dot_general(q2, kb, (((1,), (1,)), ((), ())), preferred_element_type=F32)
                s = jnp.where(ok2, s, NEG_INF)
                mx = jnp.max(s, axis=-1, keepdims=True)
                p = jnp.exp2(s - mx)
                l = jnp.sum(p, axis=-1, keepdims=True)
                o = jnp.dot(p.astype(BF16), vb, preferred_element_type=F32) / l
                lse = mx * LN2 + jnp.log(l)
                o_ref[pl.ds(q0, 128, stride=d), :] = jnp.where(m0 > 0, o[0:128], o[128:256])
                lse_ref[pl.ds(q0, 128, stride=d), :] = jnp.where(m0 > 0, lse[0:128], lse[128:256])

    return pl.pallas_call(
        body, name=f"attn_fwd_g{g}", grid=(4, nt),
        in_specs=[pl.BlockSpec((T, 384), lambda p, t: (t, g * 4 + p)),
                  pl.BlockSpec((1, 128), lambda p, t: (0, 0)),
                  pl.BlockSpec((1, 128), lambda p, t: (0, 0)),
                  pl.BlockSpec((128, 128), lambda p, t: (0, 0))],
        out_specs=[pl.BlockSpec((T, 128), lambda p, t: (t, p)), pl.BlockSpec((T, 128), lambda p, t: (t, p))],
        out_shape=[jax.ShapeDtypeStruct((S, ATTN_OUT), F32), jax.ShapeDtypeStruct((S, ATTN_OUT), F32)],
        scratch_shapes=[pltpu.VMEM((T, 128), sdt), pltpu.VMEM((2 * T, 128), sdt), pltpu.VMEM((2 * T, 128), sdt)],
        compiler_params=_params(),
    )(projq, qnw2, knw2, ebd)


def _attn_bwd(projq, dprojq, d_o, lse, delta, qnw2, knw2, ebd, g):
    S = projq.shape[0]
    d = DILATIONS[g]
    T = ATT_TILE
    nt = S // T
    nb = T // (128 * d)
    sdt = F32

    def norm_bwd(raw_b, dy, w, e):
        x = raw_b.astype(F32)
        ss = jnp.dot((x * x).astype(BF16), e, preferred_element_type=F32)
        rstd = lax.rsqrt(ss * (1.0 / HEAD_DIM) + EPS)
        xh = x * rstd
        gw = jnp.sum(dy * xh, axis=0, keepdims=True)
        dxh = dy * w
        mean = jnp.dot((dxh * xh).astype(BF16), e, preferred_element_type=F32) * (1.0 / HEAD_DIM)
        return rstd * (dxh - xh * mean), gw

    nt_dims = (((1,), (1,)), ((), ()))
    tn_dims = (((0,), (0,)), ((), ()))

    def unit_stacked(qs, dos, ls, dls, kb, vb, ok, m0, m1):
        ok2 = jnp.concatenate([ok, ok], axis=0)
        q2 = jnp.concatenate([qs * m0, qs * m1], axis=0).astype(BF16)
        do2 = jnp.concatenate([dos * m0, dos * m1], axis=0).astype(BF16)
        lcol = jnp.concatenate([_head_col(ls, m0), _head_col(ls, m1)], axis=0)
        dcol = jnp.concatenate([_head_col(dls, m0), _head_col(dls, m1)], axis=0)
        s = jnp.where(ok2, lax.dot_general(q2, kb, nt_dims, preferred_element_type=F32), NEG_INF)
        p = jnp.exp2(s - lcol * LOG2E)
        dp = lax.dot_general(do2, vb, nt_dims, preferred_element_type=F32)
        ds = (p * (dp - dcol)).astype(BF16)
        dq2 = jnp.dot(ds, kb, preferred_element_type=F32)
        dk = lax.dot_general(ds, q2, tn_dims, preferred_element_type=F32)
        dv = lax.dot_general(p.astype(BF16), do2, tn_dims, preferred_element_type=F32)
        return dq2[0:128] * m0 + dq2[128:256] * m1, dk, dv

    def unit_per_head(qs, dos, ls, dls, kb, vb, ok, m0, m1):
        dq_t = dk_t = dv_t = None
        for mh in (m0, m1):
            qh = (qs * mh).astype(BF16)
            doh = (dos * mh).astype(BF16)
            s = jnp.where(ok, lax.dot_general(qh, kb, nt_dims, preferred_element_type=F32), NEG_INF)
            p = jnp.exp2(s - _head_col(ls, mh) * LOG2E)
            dp = lax.dot_general(doh, vb, nt_dims, preferred_element_type=F32)
            ds = (p * (dp - _head_col(dls, mh))).astype(BF16)
            dq = jnp.dot(ds, kb, preferred_element_type=F32) * mh
            dk = lax.dot_general(ds, qh, tn_dims, preferred_element_type=F32)
            dv = lax.dot_general(p.astype(BF16), doh, tn_dims, preferred_element_type=F32)
            dq_t = dq if dq_t is None else dq_t + dq
            dk_t = dk if dk_t is None else dk_t + dk
            dv_t = dv if dv_t is None else dv_t + dv
        return dq_t, dk_t, dv_t

    unit = unit_per_head if d == 1 else unit_stacked

    def body(cur_ref, prev_ref, do_ref, lse_ref, dl_ref, qw_ref, kw_ref, ebd_ref, dp_in_ref,
             out_ref, gqk_ref, qs_s, ks_s, vs_s, do_s, dq_cur, dq_prev, dk_acc, dv_acc):
        i = pl.program_id(1)
        e = ebd_ref[...]
        m0, m1 = _lane_masks()

        @pl.when(i == 0)
        def _():
            dk_acc[...] = jnp.zeros_like(dk_acc)
            dv_acc[...] = jnp.zeros_like(dv_acc)
            dq_prev[...] = jnp.zeros_like(dq_prev)
            gqk_ref[...] = jnp.zeros_like(gqk_ref)

        @pl.when(i < nt)
        def _():
            qs_s[...] = _head_norm(cur_ref[:, 0:128], qw_ref[...], e) * (QK_SCALE * LOG2E)
            ks_s[pl.ds(0, T), :] = _head_norm(prev_ref[:, 128:256], kw_ref[...], e)
            ks_s[pl.ds(T, T), :] = _head_norm(cur_ref[:, 128:256], kw_ref[...], e)
            vs_s[pl.ds(0, T), :] = prev_ref[:, 256:384].astype(F32)
            vs_s[pl.ds(T, T), :] = cur_ref[:, 256:384].astype(F32)
            do_s[...] = do_ref[...].astype(F32)
            first = i == 0
            for r in range(d):
                own_k = own_v = None
                for bb in range(nb):
                    q0 = r + bb * 128 * d
                    k0 = T + r + (bb - 1) * 128 * d
                    qs = qs_s[pl.ds(q0, 128, stride=d), :]
                    dos = do_s[pl.ds(q0, 128, stride=d), :]
                    ls = lse_ref[pl.ds(q0, 128, stride=d), :]
                    dls = dl_ref[pl.ds(q0, 128, stride=d), :]
                    kb = ks_s[pl.ds(k0, 256, stride=d), :].astype(BF16)
                    vb = vs_s[pl.ds(k0, 256, stride=d), :].astype(BF16)
                    ok = _window_mask(first if bb == 0 else None)
                    dq_t, dk_t, dv_t = unit(qs, dos, ls, dls, kb, vb, ok, m0, m1)
                    dq_cur[pl.ds(q0, 128, stride=d), :] = dq_t
                    before = pl.ds(k0, 128, stride=d)
                    if bb == 0:
                        dk_acc[before, :] = dk_acc[before, :] + dk_t[0:128]
                        dv_acc[before, :] = dv_acc[before, :] + dv_t[0:128]
                    else:
                        dk_acc[before, :] = own_k + dk_t[0:128]
                        dv_acc[before, :] = own_v + dv_t[0:128]
                    own_k, own_v = dk_t[128:256], dv_t[128:256]
                tail = pl.ds(T + r + (nb - 1) * 128 * d, 128, stride=d)
                dk_acc[tail, :] = own_k
                dv_acc[tail, :] = own_v

        @pl.when(i >= 1)
        def _():
            dq_raw, gq = norm_bwd(prev_ref[:, 0:128], dq_prev[...] * QK_SCALE, qw_ref[...], e)
            dk_raw, gk = norm_bwd(prev_ref[:, 128:256], dk_acc[pl.ds(0, T), :] * LN2, kw_ref[...], e)
            out_ref[:, 0:128] = dq_raw.astype(BF16)
            out_ref[:, 128:256] = dk_raw.astype(BF16)
            out_ref[:, 256:384] = dv_acc[pl.ds(0, T), :].astype(BF16)
            gqk_ref[0:1, :] += gq
            gqk_ref[1:2, :] += gk

        dq_prev[...] = dq_cur[...]
        dk_acc[pl.ds(0, T), :] = dk_acc[pl.ds(T, T), :]
        dv_acc[pl.ds(0, T), :] = dv_acc[pl.ds(T, T), :]

    last = nt - 1
    qtile = lambda p, i: (jnp.minimum(i, last), p)
    return pl.pallas_call(
        body, name=f"attn_bwd_g{g}", grid=(4, nt + 1),
        in_specs=[pl.BlockSpec((T, 384), lambda p, i: (jnp.minimum(i, last), g * 4 + p)),
                  pl.BlockSpec((T, 384), lambda p, i: (jnp.maximum(i - 1, 0), g * 4 + p)),
                  pl.BlockSpec((T, 128), qtile), pl.BlockSpec((T, 128), qtile), pl.BlockSpec((T, 128), qtile),
                  pl.BlockSpec((1, 128), lambda p, i: (0, 0)),
                  pl.BlockSpec((1, 128), lambda p, i: (0, 0)),
                  pl.BlockSpec((128, 128), lambda p, i: (0, 0)),
                  ANY],
        out_specs=[pl.BlockSpec((T, 384), lambda p, i: (jnp.maximum(i - 1, 0), g * 4 + p)),
                   pl.BlockSpec((None, 8, 128), lambda p, i: (p, 0, 0))],
        out_shape=[jax.ShapeDtypeStruct(dprojq.shape, BF16), jax.ShapeDtypeStruct((4, 8, 128), F32)],
        scratch_shapes=[pltpu.VMEM((T, 128), sdt), pltpu.VMEM((2 * T, 128), sdt), pltpu.VMEM((2 * T, 128), sdt),
                        pltpu.VMEM((T, 128), sdt), pltpu.VMEM((T, 128), F32), pltpu.VMEM((T, 128), F32),
                        pltpu.VMEM((2 * T, 128), F32), pltpu.VMEM((2 * T, 128), F32)],
        input_output_aliases={8: 0},
        compiler_params=_params(),
    )(projq, projq, d_o, lse, delta, qnw2, knw2, ebd, dprojq)


def _mid(projn, o_g, lse_g, x, tgt, mod, convw8, wc, wa, wo, ebd, *, tm=256):
    S, D = x.shape
    nt = S // tm
    nst = max(1, 256 // tm)
    assert nt % nst == 0
    hb = tm // 16

    def body(pn_ref, hc_ref, hx_ref, o0_ref, o1_ref, o2_ref, l0_ref, l1_ref, l2_ref, x_ref, t_ref, mod_ref,
             cw_ref, ebd_ref, wc_hbm, wa_hbm, wo_hbm,
             dpn_ref, do_ref, lse_ref, dl_ref, dout_ref, vacc_ref, gwo_hbm, gwc_hbm, gwa_hbm,
             wc_s, wa_s, wo_s, gwo_s, gwc_s, gwa_s, carry_s, *stashes):
        i = pl.program_id(0)
        ti = nt - 1 - i
        par = i % nst
        stash = pl.ds(pl.multiple_of(par * tm, tm), tm)

        @pl.when(i == 0)
        def _():
            for src, dst in ((wc_hbm, wc_s), (wa_hbm, wa_s), (wo_hbm, wo_s)):
                pltpu.sync_copy(src, dst)
            gwo_s[...] = jnp.zeros_like(gwo_s)
            gwc_s[...] = jnp.zeros_like(gwc_s)
            gwa_s[...] = jnp.zeros_like(gwa_s)
            carry_s[...] = jnp.zeros_like(carry_s)
            vacc_ref[...] = jnp.zeros_like(vacc_ref)

        rows = lax.broadcasted_iota(jnp.int32, (tm, D), 0)
        w0, w1, w2 = cw_ref[0:1, :], cw_ref[1:2, :], cw_ref[2:3, :]
        gate = mod_ref[:, 2 * D:3 * D]

        b_a = pn_ref[:, BA:BA + D].astype(F32)
        c_a = pn_ref[:, CA:CA + D].astype(F32)
        x_a = pn_ref[:, XA:XA + D].astype(F32)
        z_a = pn_ref[:, ZA:ZA + D].astype(F32)
        u = c_a * x_a
        has_prev = (ti > 0).astype(F32)
        uh = hc_ref[...].astype(F32) * hx_ref[...].astype(F32) * has_prev
        h14, h15 = _row(uh, 14), _row(uh, 15)
        u_m1 = jnp.where(rows == 0, h15, pltpu.roll(u, 1, 0))
        u_m2 = jnp.where(rows == 0, h14, jnp.where(rows == 1, h15, pltpu.roll(u, 2, 0)))
        conv = w0 * u_m2 + w1 * u_m1 + w2 * u
        sg_a = _sigmoid(z_a)
        sz_a = z_a * sg_a
        y_a = b_a * conv * sz_a
        y_ab = y_a.astype(BF16)
        pa = jnp.dot(y_ab, wc_s[...], preferred_element_type=F32)

        l0, l1, l2 = l0_ref[...], l1_ref[...], l2_ref[...]
        mxl = jnp.maximum(jnp.maximum(l0, l1), l2)
        e0, e1, e2 = jnp.exp(l0 - mxl), jnp.exp(l1 - mxl), jnp.exp(l2 - mxl)
        den = e0 + e1 + e2
        attn = (e0 * o0_ref[...] + e1 * o1_ref[...] + e2 * o2_ref[...]) / den
        lse_ref[...] = mxl + jnp.log(den)
        z_b = pn_ref[:, ZB:ZB + ATTN_OUT].astype(F32)
        sg_b = _sigmoid(z_b)
        sz_b = z_b * sg_b
        y_b = attn * sz_b
        y_bb = y_b.astype(BF16)
        pb = jnp.dot(y_bb, wa_s[...], preferred_element_type=F32)

        s_a = _sigmoid(pn_ref[:, GA:GA + D].astype(F32))
        s_b = _sigmoid(pn_ref[:, GB:GB + D].astype(F32))
        merged = s_a * pa + s_b * pb
        merged_b = merged.astype(BF16)
        mo = jnp.dot(merged_b, wo_s[...], preferred_element_type=F32)
        diff = x_ref[...] + gate * mo - t_ref[...]
        dout = diff * (1.0 / D)
        dout_ref[...] = dout.astype(BF16)
        vacc_ref[4:5, :] += jnp.sum(diff * diff, axis=0, keepdims=True)
        vacc_ref[0:1, :] += jnp.sum(dout * mo, axis=0, keepdims=True)

        d_mo = (dout * gate).astype(BF16)
        nt_dims = (((1,), (1,)), ((), ()))
        dmerged = lax.dot_general(d_mo, wo_s[...], nt_dims, preferred_element_type=F32)
        dpa = (dmerged * s_a).astype(BF16)
        dpb = (dmerged * s_b).astype(BF16)
        dpn_ref[:, GA:GA + D] = (dmerged * pa * s_a * (1.0 - s_a)).astype(BF16)
        dpn_ref[:, GB:GB + D] = (dmerged * pb * s_b * (1.0 - s_b)).astype(BF16)
        tn = (((0,), (0,)), ((), ()))
        if nst == 1:
            gwo_s[...] += lax.dot_general(merged_b, d_mo, tn, preferred_element_type=F32)
            gwc_s[...] += lax.dot_general(y_ab, dpa, tn, preferred_element_type=F32)
            gwa_s[...] += lax.dot_general(y_bb, dpb, tn, preferred_element_type=F32)
        else:
            st_m, st_d, st_ya, st_pa, st_yb, st_pb = stashes
            st_m[stash, :] = merged_b
            st_d[stash, :] = d_mo
            st_ya[stash, :] = y_ab
            st_pa[stash, :] = dpa
            st_yb[stash, :] = y_bb
            st_pb[stash, :] = dpb

            @pl.when(par == nst - 1)
            def _():
                gwo_s[...] += lax.dot_general(st_m[...], st_d[...], tn, preferred_element_type=F32)
                gwc_s[...] += lax.dot_general(st_ya[...], st_pa[...], tn, preferred_element_type=F32)
                gwa_s[...] += lax.dot_general(st_yb[...], st_pb[...], tn, preferred_element_type=F32)

        dy_a = lax.dot_general(dpa, wc_s[...], nt_dims, preferred_element_type=F32)
        dy_b = lax.dot_general(dpb, wa_s[...], nt_dims, preferred_element_type=F32)

        dattn = dy_b * sz_b
        dpn_ref[:, ZB:ZB + ATTN_OUT] = (dy_b * attn * sg_b * (1.0 + z_b * (1.0 - sg_b))).astype(BF16)
        do_ref[...] = dattn.astype(BF16)
        dl_ref[...] = jnp.dot((dattn * attn).astype(BF16), ebd_ref[...], preferred_element_type=F32)

        dpn_ref[:, BA:BA + D] = (dy_a * conv * sz_a).astype(BF16)
        dpn_ref[:, ZA:ZA + D] = (dy_a * b_a * conv * sg_a * (1.0 + z_a * (1.0 - sg_a))).astype(BF16)
        dconv = dy_a * b_a * sz_a
        vacc_ref[1:2, :] += jnp.sum(dconv * u_m2, axis=0, keepdims=True)
        vacc_ref[2:3, :] += jnp.sum(dconv * u_m1, axis=0, keepdims=True)
        vacc_ref[3:4, :] += jnp.sum(dconv * u, axis=0, keepdims=True)
        nxt = carry_s[...]
        n0, n1 = _row(nxt, 0), _row(nxt, 1)
        dc_p1 = jnp.where(rows == tm - 1, n0, pltpu.roll(dconv, tm - 1, 0))
        dc_p2 = jnp.where(rows == tm - 1, n1, jnp.where(rows == tm - 2, n0, pltpu.roll(dconv, tm - 2, 0)))
        du = w2 * dconv + w1 * dc_p1 + w0 * dc_p2
        carry_s[...] = dconv[0:8, :]
        dpn_ref[:, CA:CA + D] = (du * x_a).astype(BF16)
        dpn_ref[:, XA:XA + D] = (du * c_a).astype(BF16)

        @pl.when(i == nt - 1)
        def _():
            pltpu.sync_copy(gwo_s, gwo_hbm)
            pltpu.sync_copy(gwc_s, gwc_hbm)
            pltpu.sync_copy(gwa_s, gwa_hbm)

    rev = lambda i: (nt - 1 - i, 0)
    halo = lambda col: pl.BlockSpec((16, D), lambda i: (jnp.maximum((nt - 1 - i) * hb - 1, 0), col))
    tile512 = pl.BlockSpec((tm, ATTN_OUT), rev)
    tileD = pl.BlockSpec((tm, D), rev)
    const = lambda shape: pl.BlockSpec(shape, lambda i: (0, 0))
    return pl.pallas_call(
        body, name="mid_fwd_bwd", grid=(nt,),
        in_specs=[pl.BlockSpec((tm, NAT), rev), halo(CA // D), halo(XA // D)] + [tile512] * 6
                 + [tileD, tileD, const((1, 3 * D)), const((8, D)), const((ATTN_OUT, ATTN_OUT))] + [ANY] * 3,
        out_specs=[pl.BlockSpec((tm, NAT), rev), tile512, tile512, tile512, tileD, const((8, D)), ANY, ANY, ANY],
        out_shape=[jax.ShapeDtypeStruct((S, NAT), BF16), jax.ShapeDtypeStruct((S, ATTN_OUT), BF16),
                   jax.ShapeDtypeStruct((S, ATTN_OUT), F32), jax.ShapeDtypeStruct((S, ATTN_OUT), F32),
                   jax.ShapeDtypeStruct((S, D), BF16), jax.ShapeDtypeStruct((8, D), F32),
                   jax.ShapeDtypeStruct((D, D), F32), jax.ShapeDtypeStruct((D, D), F32),
                   jax.ShapeDtypeStruct((ATTN_OUT, D), F32)],
        scratch_shapes=[pltpu.VMEM((D, D), BF16), pltpu.VMEM((ATTN_OUT, D), BF16), pltpu.VMEM((D, D), BF16),
                        pltpu.VMEM((D, D), F32), pltpu.VMEM((D, D), F32), pltpu.VMEM((ATTN_OUT, D), F32),
                        pltpu.VMEM((8, D), F32)]
                       + ([pltpu.VMEM((nst * tm, D), BF16)] * 4
                          + [pltpu.VMEM((nst * tm, ATTN_OUT), BF16), pltpu.VMEM((nst * tm, D), BF16)] if nst > 1 else []),
        compiler_params=_params(vmem=V7X_VMEM_LIMIT_RESIDENT),
    )(projn, projn, projn, *o_g, *lse_g, x, tgt, mod, convw8, ebd, wc, wa, wo)


def _dh_matmul(dpn, dpq, w_nat, w_qkv, parts, *, tm=512):
    S = dpn.shape[0]
    D = D_MODEL
    n = len(parts)
    nt = S // tm

    def body(dn_ref, dq_ref, wn_hbm, wq_hbm, *rest):
        p_hbm, rest = rest[:n], rest[n:]
        dh_ref = rest[0]
        q_hbm = rest[1:1 + n]
        wn_s, wq_s = rest[1 + n:3 + n]
        sems = rest[3 + n:]

        @pl.when(pl.program_id(0) == 0)
        def _():
            if n:
                mine, _ = _scatter_copies(p_hbm, q_hbm, *sems)
                for cp in mine:
                    cp.start()
            pltpu.sync_copy(wn_hbm, wn_s)
            pltpu.sync_copy(wq_hbm, wq_s)

        if n:
            @pl.when(pl.program_id(0) == nt - 1)
            def _():
                mine, theirs = _scatter_copies(p_hbm, q_hbm, *sems)
                for cp in theirs:
                    cp.wait_recv()
                for cp in mine:
                    cp.wait_send()

        nt_dims = (((1,), (1,)), ((), ()))
        dh = (lax.dot_general(dn_ref[...], wn_s[...], nt_dims, preferred_element_type=F32)
              + lax.dot_general(dq_ref[...], wq_s[...], nt_dims, preferred_element_type=F32))
        dh_ref[...] = dh.astype(BF16)

    row = lambda w: pl.BlockSpec((tm, w), lambda i: (i, 0))
    outs = pl.pallas_call(
        body, name="dh_matmul", grid=(nt,),
        in_specs=[row(NAT), row(NQKV), ANY, ANY] + [ANY] * n,
        out_specs=[row(D)] + [ANY] * n,
        out_shape=[jax.ShapeDtypeStruct((S, D), BF16)] + [jax.ShapeDtypeStruct(p.shape, p.dtype) for p in parts],
        scratch_shapes=[pltpu.VMEM((D, NAT), BF16), pltpu.VMEM((D, NQKV), BF16)]
                       + ([pltpu.SemaphoreType.DMA((n, 3)), pltpu.SemaphoreType.DMA((n, 3))] if n else []),
        compiler_params=_params(vmem=V7X_VMEM_LIMIT_RESIDENT),
    )(dpn, dpq, w_nat, w_qkv, *parts)
    return outs[0], outs[1:]


def _norm_bwd(dh, x, dout, mod, norm_w, *, tm=512):
    S, D = x.shape

    def body(dh_ref, x_ref, dout_ref, mod_ref, nw_ref, gx_ref, st_ref):
        @pl.when(pl.program_id(0) == 0)
        def _():
            st_ref[...] = jnp.zeros_like(st_ref)

        dh = dh_ref[...].astype(F32)
        scale1 = 1.0 + mod_ref[:, D:2 * D]
        nw = nw_ref[...]
        xv = x_ref[...]
        rstd = lax.rsqrt(jnp.mean(xv * xv, axis=-1, keepdims=True) + EPS)
        xh = xv * rstd
        dhs = dh * scale1
        dxh = dhs * nw
        dx = rstd * (dxh - xh * jnp.mean(dxh * xh, axis=-1, keepdims=True))
        gx_ref[...] = dout_ref[...].astype(F32) + dx
        st_ref[0:1, :] += jnp.sum(dh, axis=0, keepdims=True)
        st_ref[1:2, :] += jnp.sum(dh * (xh * nw), axis=0, keepdims=True)
        st_ref[2:3, :] += jnp.sum(dhs * xh, axis=0, keepdims=True)

    row = pl.BlockSpec((tm, D), lambda i: (i, 0))
    return pl.pallas_call(
        body, name="norm_bwd", grid=(S // tm,),
        in_specs=[row, row, row, pl.BlockSpec((1, 3 * D), lambda i: (0, 0)), pl.BlockSpec((1, D), lambda i: (0, 0))],
        out_specs=[row, pl.BlockSpec((8, D), lambda i: (0, 0))],
        out_shape=[jax.ShapeDtypeStruct((S, D), F32), jax.ShapeDtypeStruct((8, D), F32)],
        compiler_params=_params(),
    )(dh, x, dout, mod, norm_w)


def _pack_smalls(st, vacc, gqk):
    D = D_MODEL

    def body(st_ref, va_ref, gqk_ref, o_ref):
        o_ref[...] = jnp.zeros_like(o_ref)
        o_ref[0:2, :] = st_ref[0:2, :]
        o_ref[2:3, :] = va_ref[0:1, :]
        o_ref[8:9, :] = st_ref[2:3, :]
        o_ref[16:19, :] = va_ref[1:4, :]
        tot = gqk_ref[0]
        for k in range(1, gqk_ref.shape[0]):
            tot = tot + gqk_ref[k]
        tot = tot + pltpu.roll(tot, HEAD_DIM, 1)
        o_ref[24:32, 0:128] = tot
        loss = jnp.sum(va_ref[4:5, :], axis=1, keepdims=True) * (0.5 / D)
        o_ref[26:27, :] = jnp.broadcast_to(loss, (1, D))

    return pl.pallas_call(
        body, name="pack_smalls", out_shape=jax.ShapeDtypeStruct((32, D), F32),
        in_specs=[pl.BlockSpec(memory_space=pltpu.VMEM)] * 3,
        out_specs=pl.BlockSpec(memory_space=pltpu.VMEM), compiler_params=_params(),
    )(st, vacc, gqk)


def _small_update(sm_loc, dm_pad, ct_pad, ada, vecs):
    D = D_MODEL
    nv = len(vecs)
    places = [(0, 3, D), (8, 1, D), (16, 3, 256), (24, 1, HEAD_DIM), (25, 1, HEAD_DIM)]

    def body(*refs):
        sm_ref, dm_ref, ct_ref = refs[0:3]
        wmv = [refs[3 + 3 * t:6 + 3 * t] for t in range(nv + 1)]
        outs = refs[3 + 3 * (nv + 1):]
        res = [outs[4 * t:4 * t + 4] for t in range(nv + 1)]
        loss_ref, tot_s = outs[4 * (nv + 1)], outs[4 * (nv + 1) + 1]

        def update(g, w, m, v, out, at=(slice(None), slice(None))):
            d, mn, vn = _adamw_math(w[at], g, m[at], v[at])
            for ref, val in zip(out, (g, d, mn, vn)):
                ref[at] = val

        gw = jnp.dot(ct_ref[...], dm_ref[...], preferred_element_type=F32, precision=lax.Precision.HIGHEST)
        update(gw, *wmv[0], res[0])
        tot = sm_ref[0]
        for k in range(1, 8):
            tot = tot + sm_ref[k]
        tot_s[...] = tot
        for k in range(3):
            update(tot_s[k:k + 1, :], *wmv[1], res[1], at=(slice(None), slice(k * D, (k + 1) * D)))
        for t in range(1, nv):
            r0, nr, nl = places[t]
            update(tot_s[r0:r0 + nr, 0:nl], *wmv[1 + t], res[1 + t])
        loss_ref[...] = jnp.broadcast_to(tot_s[26:27, 0:128], (8, 128))

    vm = pl.BlockSpec(memory_space=pltpu.VMEM)
    groups = [ada] + list(vecs)
    out_shape = []
    for w, _, _ in groups:
        out_shape += [jax.ShapeDtypeStruct(w.shape, F32)] * 4
    out_shape.append(jax.ShapeDtypeStruct((8, 128), F32))
    flat = [a for grp in groups for a in grp]
    outs = pl.pallas_call(
        body, name="small_update", out_shape=out_shape,
        in_specs=[vm] * (3 + len(flat)), out_specs=[vm] * len(out_shape),
        scratch_shapes=[pltpu.VMEM((32, D), F32)], compiler_params=_params(),
    )(sm_loc, dm_pad, ct_pad, *flat)
    return [outs[4 * t:4 * t + 4] for t in range(nv + 1)], outs[-1]


def _later_weights(gathered, shards, chip):
    D = D_MODEL
    g_bc, g_ba, g_wo = [lax.dynamic_update_slice(g, s[None], (chip, 0, 0, 0)) for g, s in zip(gathered, shards)]
    wa = g_ba.reshape(4, ATTN_OUT, 256).transpose(1, 0, 2).reshape(ATTN_OUT, D)
    return g_bc.reshape(D, D), wa, g_wo.reshape(D, D)


def _local_step(xs, tg, mod, norm_w, w_nat, w_qkv, later, convw8, q_norm_w, k_norm_w):
    S = xs.shape[0]
    lane = jnp.arange(128)
    ebd128 = (lane[:, None] // HEAD_DIM == lane[None, :] // HEAD_DIM).astype(BF16)
    lane5 = jnp.arange(ATTN_OUT)
    ebd512 = (lane5[:, None] // HEAD_DIM == lane5[None, :] // HEAD_DIM).astype(BF16)
    qnw2 = jnp.tile(q_norm_w, (1, 2))
    knw2 = jnp.tile(k_norm_w, (1, 2))

    projn, h, ht = _inproj_nat(xs, mod, norm_w, w_nat)
    if len(later) == 2:
        projq, gathered = _inproj_qkv(h, w_qkv, later[0])
        wc, wa, wo = _later_weights(gathered, *later)
    else:
        projq, _ = _inproj_qkv(h, w_qkv, [])
        wc, wa, wo = later
    o_g, lse_g = [], []
    for g in range(N_GROUPS):
        o, l = _attn_fwd(projq, qnw2, knw2, ebd128, g)
        o_g.append(o)
        lse_g.append(l)

    dpn, d_o, lse, delta, dout, vacc, gwo, gwc, gwa = _mid(
        projn, o_g, lse_g, xs, tg, mod, convw8, wc, wa, wo, ebd512)
    dpq = lax.empty((S, NQKV), BF16)
    gqk = []
    for g in range(N_GROUPS):
        dpq, part = _attn_bwd(projq, dpq, d_o, lse, delta, qnw2, knw2, ebd128, g)
        gqk.append(part)
    return dpn, dpq, dout, ht, gwc, gwa, gwo, vacc, gqk


def kernel(x, c, w_ada, b_ada, norm_w, w_in, conv_w, q_norm_w, k_norm_w, w_br_conv, w_br_attn, w_out, loss_target, m_w_ada, m_b_ada, m_norm_w, m_w_in, m_conv_w, m_q_norm_w, m_k_norm_w, m_w_br_conv, m_w_br_attn, m_w_out, v_w_ada, v_b_ada, v_norm_w, v_w_in, v_conv_w, v_q_norm_w, v_k_norm_w, v_w_br_conv, v_w_br_attn, v_w_out):
    D = D_MODEL
    S = x.shape[1]
    xs, tg = x[0], loss_target[0]
    mx, my, mc = lax.axis_index("x"), lax.axis_index("y"), lax.axis_index("c")
    chip = 2 * mx + my
    dev = 2 * chip + mc

    c_blk = jnp.concatenate([c, jnp.pad(conv_w[0], ((0, 0), (0, D - 256))), jnp.zeros((4, D), F32)], axis=0)
    b_shard = lax.dynamic_slice(b_ada, (0, chip * 768), (1, 768))
    shards = [w_in[0].astype(BF16).reshape(2, 512, 2816), w_br_conv[0].astype(BF16).reshape(2, 128, D),
              w_br_attn[0].astype(BF16).reshape(2, 256, 256), w_out[0].astype(BF16).reshape(2, 128, D)]
    gathered_in, _, cg, mod_parts, silu_c = _weights_allgather(
        shards[0].reshape(2, 2, 256, SHARD_COLS), [], c_blk, w_ada[0], b_shard)
    convw8 = jnp.pad(cg[::2, 1:4, 0:256].transpose(1, 0, 2).reshape(3, D), ((0, 5), (0, 0)))
    mod_all = mod_parts.transpose(1, 0, 2).reshape(8, 3 * D)
    mod = lax.dynamic_slice(mod_all, (dev, 0), (1, 3 * D))
    chipv = jnp.reshape(chip, (1,)).astype(jnp.int32)
    g_in = gathered_in.reshape(4, D, SHARD_COLS)
    own_in = shards[0].reshape(D, SHARD_COLS)
    w_nat = _weights_prep(g_in, own_in, chipv, "nat")
    w_qkv = _weights_prep(g_in, own_in, chipv, "qkv")

    dpn, dpq, dout, ht, gwc, gwa, gwo, vacc, gqk = _local_step(
        xs, tg, mod, norm_w, w_nat, w_qkv, (shards[1:], chip), convw8, q_norm_w, k_norm_w)

    def halves(a):
        k, r, cc = a.shape
        return a.reshape(k, 2, r // 2, cc).transpose(1, 0, 2, 3)

    gw_nat = _matmul_lhs_resident(ht, dpn, tn=512, name="grad_w_in_nat").reshape(2, D // 2, NAT)
    gw_qkv, recv_nat = _matmul_lhs_resident(ht, dpq, tn=512, name="grad_w_in_qkv", exchange=gw_nat)
    grads = [gw_nat, gw_qkv.reshape(2, D // 2, NQKV),
             halves(gwc.astype(BF16).reshape(4, 256, D)),
             halves(gwa.astype(BF16).reshape(ATTN_OUT, 4, 256).transpose(1, 0, 2)),
             halves(gwo.astype(BF16).reshape(4, 256, D))]
    recv = [recv_nat] + list(_pair_exchange(grads[1:]))
    core = jnp.reshape(mc, (1,)).astype(jnp.int32)
    p_in = _pair_add_to_shards(core, grads[0], recv[0], lax.empty((4, D // 2, SHARD_COLS), BF16), "nat")
    p_in = _pair_add_to_shards(core, grads[1], recv[1], p_in, "qkv")
    parts = [p_in] + [_pair_add(core, gr, rc) for gr, rc in zip(grads[2:], recv[2:])]
    dh, gathered = _dh_matmul(dpn, dpq, w_nat, w_qkv, parts)
    grad_x, st = _norm_bwd(dh, xs, dout, mod, norm_w)
    mine = [_sum4(chipv, p, q) for p, q in zip(parts, gathered)]
    theirs, sm_all = _half_exchange(mine, _pack_smalls(st, vacc, jnp.concatenate(gqk, axis=0)))

    big = {}
    for t, (nm, w, m, v) in enumerate((("w_in", w_in, m_w_in, v_w_in), ("w_br_conv", w_br_conv, m_w_br_conv, v_w_br_conv),
                                       ("w_br_attn", w_br_attn, m_w_br_attn, v_w_br_attn), ("w_out", w_out, m_w_out, v_w_out))):
        big[nm] = _adamw_halves(core, w[0], mine[t], theirs[t], m[0], v[0], "adamw_" + nm)

    dmod_all = sm_all[:, 0:3, :].reshape(8, 3 * D)
    dm_pad = jnp.pad(lax.dynamic_slice(dmod_all, (0, chip * 768), (8, 768)), ((0, 120), (0, 0)))
    conv_dev = lax.dynamic_slice(sm_all[:, 16:24, :], (0, 0, chip * 256), (8, 8, 256))
    sm_loc = jnp.concatenate([sm_all[:, 0:16, :], jnp.pad(conv_dev, ((0, 0), (0, 0), (0, D - 256))), sm_all[:, 24:32, :]], axis=1)

    ct_pad = jnp.pad(silu_c.T, ((0, 0), (0, 120)))
    small_names = ["b_ada", "norm_w", "conv_w", "q_norm_w", "k_norm_w"]
    vecs = [(b_ada, m_b_ada, v_b_ada), (norm_w, m_norm_w, v_norm_w), (conv_w[0], m_conv_w[0], v_conv_w[0]),
            (q_norm_w, m_q_norm_w, v_q_norm_w), (k_norm_w, m_k_norm_w, v_k_norm_w)]
    updated, loss_blk = _small_update(sm_loc, dm_pad, ct_pad, (w_ada[0], m_w_ada[0], v_w_ada[0]), vecs)
    small = {"w_ada": [a[None] for a in updated[0]]}
    for nm, four in zip(small_names, updated[1:]):
        small[nm] = [a[None] for a in four] if nm == "conv_w" else list(four)

    order = ["w_ada", "b_ada", "norm_w", "w_in", "conv_w", "q_norm_w", "k_norm_w", "w_br_conv", "w_br_attn", "w_out"]
    res = [loss_blk[0, 0], grad_x[None]]
    for kind in range(4):
        for nm in order:
            res.append(big[nm][kind][None] if nm in big else small[nm][kind])
    return tuple(res)
```

```python
import jax
import jax.numpy as jnp
from jax import lax
from jax.experimental import pallas as pl
from jax.experimental.pallas import tpu as pltpu

F32 = jnp.float32
BF16 = jnp.bfloat16
MESH = pl.DeviceIdType.MESH
ANY = pl.BlockSpec(memory_space=pl.ANY)

D_MODEL = 1024
HEAD_DIM = 64
N_GROUPS = 3
DILATIONS = (1, 4, 16)
ATTN_OUT = 512
EPS = 1e-6
NEG_INF = -1e30
NAT = 6656
NQKV = 4608
BA, CA, XA, ZA, ZB, GA, GB = 0, 1024, 2048, 3072, 4096, 4608, 5632
ATT_TILE = 2048
QK_SCALE = HEAD_DIM ** -0.5
LOG2E = 1.4426950408889634
LN2 = 0.6931471805599453
V7X_VMEM_LIMIT = 56 * 1024 * 1024
V7X_VMEM_LIMIT_RESIDENT = 60 * 1024 * 1024

ADAM_LR = 0.001
ADAM_B1 = 0.9
ADAM_B2 = 0.999
ADAM_EPS = 1e-08
ADAM_WD = 0.01
ADAM_STEP = 10


def _params(vmem=V7X_VMEM_LIMIT, **kw):
    return pltpu.CompilerParams(vmem_limit_bytes=vmem, **kw)


def _sigmoid(z):
    return 1.0 / (1.0 + jnp.exp(-z))


def _row(v, r):
    rows = lax.broadcasted_iota(jnp.int32, v.shape, 0)
    return jnp.sum(jnp.where(rows == r, v, 0.0), axis=0, keepdims=True)


def _coords():
    return lax.axis_index("x"), lax.axis_index("y"), lax.axis_index("c")


def _flip(v, bit):
    return 1 - v if bit else v


def _weights_allgather(big_shard, shards, c_blk, w_ada, b_shard):
    n = len(shards)
    D = D_MODEL
    cols = w_ada.shape[1]

    def body(*refs):
        big_in, ins = refs[0], refs[1:1 + n]
        c_ref, wada_ref, b_ref = refs[1 + n:4 + n]
        big_out, outs = refs[4 + n], refs[5 + n:5 + 2 * n]
        cg_ref, mp_ref, sc_ref = refs[5 + 2 * n:8 + 2 * n]
        sem_refs = refs[8 + 2 * n:]
        ssem, rsem = sem_refs[:2] if n else (None, None)
        bs, br, cs, cr, ms, mr, lsem = sem_refs[2 if n else 0:]
        mx, my, mc = _coords()
        me = 2 * mx + my
        me8 = 2 * me + mc
        sib = (mx, my, 1 - mc)
        chips = [(_flip(mx, j & 2), _flip(my, j & 1)) for j in range(1, 4)]
        sends = []

        local = pltpu.make_async_copy(c_ref, cg_ref.at[me8], lsem)
        local.start()
        for j in range(1, 8):
            peer = (_flip(mx, j & 4), _flip(my, j & 2), _flip(mc, j & 1))
            cp = pltpu.make_async_remote_copy(
                src_ref=c_ref, dst_ref=cg_ref.at[me8], send_sem=cs.at[j - 1], recv_sem=cr.at[j - 1],
                device_id=peer, device_id_type=MESH)
            cp.start()
            sends.append(cp)

        kx, ky, kd = 2 * (1 - mx) + my, 2 * mx + (1 - my), 2 * (1 - mx) + (1 - my)
        xn, yn = (1 - mx, my, mc), (mx, 1 - my, mc)

        def big(src_chip, q, sem, to, half=None):
            h = mc if half is None else half
            return pltpu.make_async_remote_copy(
                src_ref=big_out.at[src_chip, h, q], dst_ref=big_out.at[src_chip, h, q],
                send_sem=bs.at[sem], recv_sem=br.at[sem], device_id=to, device_id_type=MESH)

        def own(q, sem, to):
            return pltpu.make_async_remote_copy(
                src_ref=big_in.at[mc, q], dst_ref=big_out.at[me, mc, q],
                send_sem=bs.at[sem], recv_sem=br.at[sem], device_id=to, device_id_type=MESH)

        for cp in (own(0, 0, xn), own(1, 3, yn), own(1, 1, xn), own(0, 2, yn)):
            cp.start()
            sends.append(cp)
        for t in range(n):
            for j, (px, py) in enumerate(chips):
                cp = pltpu.make_async_remote_copy(
                    src_ref=ins[t].at[mc], dst_ref=outs[t].at[me, mc], send_sem=ssem.at[t, j],
                    recv_sem=rsem.at[t, j], device_id=(px, py, mc), device_id_type=MESH)
                cp.start()
                sends.append(cp)

        pieces = [(kx, 0), (kx, 1), (ky, 0), (ky, 1), (kd, 0), (kd, 1)]

        def landed(piece, sem, frm, pass_on=None):
            src_chip, q = pieces[piece]
            big(src_chip, q, sem, frm).wait_recv()
            nxt = ([] if pass_on is None else [big(src_chip, q, pass_on[0], pass_on[1])])
            nxt.append(big(src_chip, q, 6 + piece, sib))
            for cp in nxt:
                cp.start()
                sends.append(cp)

        for j in range(1, 8):
            px, py, pc = _flip(mx, j & 4), _flip(my, j & 2), _flip(mc, j & 1)
            pltpu.make_async_remote_copy(
                src_ref=c_ref, dst_ref=cg_ref.at[4 * px + 2 * py + pc], send_sem=cs.at[j - 1],
                recv_sem=cr.at[j - 1], device_id=(px, py, pc), device_id_type=MESH).wait_recv()
        local.wait()
        pick = (lax.broadcasted_iota(jnp.int32, (8, 64), 1) == 8 * lax.broadcasted_iota(jnp.int32, (8, 64), 0))
        cv = jnp.dot(pick.astype(F32), cg_ref[...].reshape(64, D), preferred_element_type=F32,
                     precision=lax.Precision.HIGHEST)
        sc = cv * _sigmoid(cv)
        sc_ref[...] = sc
        mp_ref[me] = jnp.dot(sc, wada_ref[...], preferred_element_type=F32,
                             precision=lax.Precision.HIGHEST) + b_ref[...]
        for j, (px, py) in enumerate(chips):
            cp = pltpu.make_async_remote_copy(
                src_ref=mp_ref.at[me], dst_ref=mp_ref.at[me], send_sem=ms.at[j], recv_sem=mr.at[j],
                device_id=(px, py, mc), device_id_type=MESH)
            cp.start()
            sends.append(cp)

        landed(0, 0, xn, pass_on=(4, yn))
        landed(3, 3, yn, pass_on=(5, xn))
        landed(1, 1, xn)
        landed(2, 2, yn)
        for j, (px, py) in enumerate(chips):
            pltpu.make_async_remote_copy(
                src_ref=mp_ref.at[me], dst_ref=mp_ref.at[2 * px + py], send_sem=ms.at[j], recv_sem=mr.at[j],
                device_id=(px, py, mc), device_id_type=MESH).wait_recv()
        landed(4, 4, yn)
        landed(5, 5, xn)
        for t in range(n):
            for j, (px, py) in enumerate(chips):
                src = 2 * px + py
                pltpu.make_async_remote_copy(
                    src_ref=ins[t].at[mc], dst_ref=outs[t].at[src, mc], send_sem=ssem.at[t, j],
                    recv_sem=rsem.at[t, j], device_id=(px, py, mc), device_id_type=MESH).wait_recv()
                fwd = pltpu.make_async_remote_copy(
                    src_ref=outs[t].at[src, mc], dst_ref=outs[t].at[src, mc], send_sem=ssem.at[t, 3 + j],
                    recv_sem=rsem.at[t, 3 + j], device_id=sib, device_id_type=MESH)
                fwd.start()
                sends.append(fwd)
        for t in range(n):
            for j, (px, py) in enumerate(chips):
                src = 2 * px + py
                pltpu.make_async_remote_copy(
                    src_ref=outs[t].at[src, 1 - mc], dst_ref=outs[t].at[src, 1 - mc], send_sem=ssem.at[t, 3 + j],
                    recv_sem=rsem.at[t, 3 + j], device_id=sib, device_id_type=MESH).wait_recv()
        for i, (src_chip, q) in enumerate(pieces):
            big(src_chip, q, 6 + i, sib, half=1 - mc).wait_recv()
        for cp in sends:
            cp.wait_send()

    vm = pl.BlockSpec(memory_space=pltpu.VMEM)
    outs = pl.pallas_call(
        body, name="weights_allgather",
        out_shape=[jax.ShapeDtypeStruct((4,) + big_shard.shape, big_shard.dtype)]
                  + [jax.ShapeDtypeStruct((4,) + s.shape, s.dtype) for s in shards]
                  + [jax.ShapeDtypeStruct((8,) + c_blk.shape, F32), jax.ShapeDtypeStruct((4, 8, cols), F32),
                     jax.ShapeDtypeStruct((8, D), F32)],
        in_specs=[ANY] * (n + 1) + [vm] * 3, out_specs=[ANY] * (n + 1) + [vm] * 3,
        scratch_shapes=([pltpu.SemaphoreType.DMA((n, 6)), pltpu.SemaphoreType.DMA((n, 6))] if n else [])
                       + [pltpu.SemaphoreType.DMA((12,)), pltpu.SemaphoreType.DMA((12,)),
                          pltpu.SemaphoreType.DMA((7,)), pltpu.SemaphoreType.DMA((7,)),
                          pltpu.SemaphoreType.DMA((3,)), pltpu.SemaphoreType.DMA((3,)), pltpu.SemaphoreType.DMA(())],
        compiler_params=_params(),
    )(big_shard, *shards, c_blk, w_ada, b_shard)
    return outs[0], outs[1:1 + n], outs[1 + n], outs[2 + n], outs[3 + n]


def _shard_gather_copies(ins, outs, ssem, rsem):
    mx, my, mc = _coords()
    me = 2 * mx + my
    sib = (mx, my, 1 - mc)
    send, recv, fwd, fwd_recv = [], [], [], []
    for t in range(len(ins)):
        for j in range(3):
            px, py = _flip(mx, (j + 1) & 2), _flip(my, (j + 1) & 1)
            src = 2 * px + py
            far = dict(send_sem=ssem.at[t, j], recv_sem=rsem.at[t, j], device_id=(px, py, mc), device_id_type=MESH)
            near = dict(send_sem=ssem.at[t, 3 + j], recv_sem=rsem.at[t, 3 + j], device_id=sib, device_id_type=MESH)
            send.append(pltpu.make_async_remote_copy(src_ref=ins[t].at[mc], dst_ref=outs[t].at[me, mc], **far))
            recv.append(pltpu.make_async_remote_copy(src_ref=ins[t].at[mc], dst_ref=outs[t].at[src, mc], **far))
            fwd.append(pltpu.make_async_remote_copy(src_ref=outs[t].at[src, mc], dst_ref=outs[t].at[src, mc], **near))
            fwd_recv.append(pltpu.make_async_remote_copy(
                src_ref=outs[t].at[src, 1 - mc], dst_ref=outs[t].at[src, 1 - mc], **near))
    return send, recv, fwd, fwd_recv


def _pair_exchange(grads):
    n = len(grads)

    def body(*refs):
        ins, outs = refs[:n], refs[n:2 * n]
        ssem, rsem = refs[2 * n:]
        mx, my, mc = _coords()
        sib = (mx, my, 1 - mc)
        sends = []
        for t in range(n):
            cp = pltpu.make_async_remote_copy(
                src_ref=ins[t].at[1 - mc], dst_ref=outs[t], send_sem=ssem.at[t], recv_sem=rsem.at[t],
                device_id=sib, device_id_type=MESH)
            cp.start()
            sends.append(cp)
        for cp in sends:
            cp.wait_recv()
        for cp in sends:
            cp.wait_send()

    return pl.pallas_call(
        body, name="grad_pair_exchange",
        out_shape=[jax.ShapeDtypeStruct(g.shape[1:], g.dtype) for g in grads],
        in_specs=[ANY] * n, out_specs=[ANY] * n,
        scratch_shapes=[pltpu.SemaphoreType.DMA((n,)), pltpu.SemaphoreType.DMA((n,))],
    )(*grads)


def _scatter_copies(ins, outs, ssem, rsem):
    mx, my, mc = _coords()
    me = 2 * mx + my
    mine, theirs = [], []
    for t in range(len(ins)):
        for j in range(1, 4):
            px, py = _flip(mx, j & 2), _flip(my, j & 1)
            mine.append(pltpu.make_async_remote_copy(
                src_ref=ins[t].at[2 * px + py], dst_ref=outs[t].at[me], send_sem=ssem.at[t, j - 1],
                recv_sem=rsem.at[t, j - 1], device_id=(px, py, mc), device_id_type=MESH))
            theirs.append(pltpu.make_async_remote_copy(
                src_ref=ins[t].at[me], dst_ref=outs[t].at[2 * px + py], send_sem=ssem.at[t, j - 1],
                recv_sem=rsem.at[t, j - 1], device_id=(px, py, mc), device_id_type=MESH))
    return mine, theirs


def _half_exchange(halves, smalls):
    n = len(halves)

    def body(*refs):
        ins, sm_ref = refs[:n], refs[n]
        outs, all_ref = refs[n + 1:2 * n + 1], refs[2 * n + 1]
        ssem, rsem, gsem, hsem, lsem = refs[2 * n + 2:]
        mx, my, mc = _coords()
        me = 4 * mx + 2 * my + mc
        sib = (mx, my, 1 - mc)
        local = pltpu.make_async_copy(sm_ref, all_ref.at[me], lsem)
        local.start()
        sends = []
        for t in range(n):
            rc = pltpu.make_async_remote_copy(
                src_ref=ins[t], dst_ref=outs[t], send_sem=ssem.at[t], recv_sem=rsem.at[t],
                device_id=sib, device_id_type=MESH)
            rc.start()
            sends.append(rc)
        gathers = []
        for j in range(1, 8):
            peer = (_flip(mx, j & 4), _flip(my, j & 2), _flip(mc, j & 1))
            cp = pltpu.make_async_remote_copy(
                src_ref=sm_ref, dst_ref=all_ref.at[me], send_sem=gsem.at[j - 1], recv_sem=hsem.at[j - 1],
                device_id=peer, device_id_type=MESH)
            cp.start()
            gathers.append(cp)
        for cp in sends:
            cp.wait_recv()
        for j in range(1, 8):
            px, py, pc = _flip(mx, j & 4), _flip(my, j & 2), _flip(mc, j & 1)
            pltpu.make_async_remote_copy(
                src_ref=sm_ref, dst_ref=all_ref.at[4 * px + 2 * py + pc], send_sem=gsem.at[j - 1],
                recv_sem=hsem.at[j - 1], device_id=(px, py, pc), device_id_type=MESH).wait_recv()
        for cp in sends + gathers:
            cp.wait_send()
        local.wait()

    vm = pl.BlockSpec(memory_space=pltpu.VMEM)
    outs = pl.pallas_call(
        body, name="grad_half_exchange",
        out_shape=[jax.ShapeDtypeStruct(h.shape, h.dtype) for h in halves]
                  + [jax.ShapeDtypeStruct((8,) + smalls.shape, smalls.dtype)],
        in_specs=[ANY] * n + [vm], out_specs=[ANY] * n + [vm],
        scratch_shapes=[pltpu.SemaphoreType.DMA((n,)), pltpu.SemaphoreType.DMA((n,)),
                        pltpu.SemaphoreType.DMA((7,)), pltpu.SemaphoreType.DMA((7,)), pltpu.SemaphoreType.DMA(())],
    )(*halves, smalls)
    return outs[:n], outs[n]


def _row_tile(rows, cols, bytes_per_row_elem=4, budget=2 * 1024 * 1024):
    t = rows
    while t % 2 == 0 and t > 16 and t * cols * bytes_per_row_elem > budget:
        t //= 2
    return t


def _pair_add(core, g, r):
    _, _, rh, cc = g.shape
    tr = _row_tile(rh, cc)

    def body(core_ref, g_ref, r_ref, o_ref):
        o_ref[...] = (g_ref[...].astype(F32) + r_ref[...].astype(F32)).astype(o_ref.dtype)

    return pl.pallas_call(
        body, name="grad_pair_add",
        grid_spec=pltpu.PrefetchScalarGridSpec(
            num_scalar_prefetch=1, grid=(4, rh // tr),
            in_specs=[pl.BlockSpec((None, None, tr, cc), lambda k, i, c: (c[0], k, i, 0)),
                      pl.BlockSpec((None, tr, cc), lambda k, i, c: (k, i, 0))],
            out_specs=pl.BlockSpec((None, tr, cc), lambda k, i, c: (k, i, 0))),
        out_shape=jax.ShapeDtypeStruct(r.shape, BF16),
        compiler_params=_params(),
    )(core, g, r)


def _sum4(chip, p, q):
    _, rh, cc = q.shape
    tr = _row_tile(rh, cc)

    def body(chip_ref, p_ref, q1_ref, q2_ref, q3_ref, o_ref):
        o_ref[...] = ((p_ref[...].astype(F32) + q1_ref[...].astype(F32)) + q2_ref[...].astype(F32)) + q3_ref[...].astype(F32)

    def other(j):
        return pl.BlockSpec((None, tr, cc), lambda i, c: (jnp.bitwise_xor(c[0], j), i, 0))

    return pl.pallas_call(
        body, name="grad_sum4",
        grid_spec=pltpu.PrefetchScalarGridSpec(
            num_scalar_prefetch=1, grid=(rh // tr,),
            in_specs=[pl.BlockSpec((None, tr, cc), lambda i, c: (c[0], i, 0)), other(1), other(2), other(3)],
            out_specs=pl.BlockSpec((tr, cc), lambda i, c: (i, 0))),
        out_shape=jax.ShapeDtypeStruct((rh, cc), F32),
        compiler_params=_params(),
    )(chip, p, q, q, q)


def _adamw_math(w, g, m, v):
    m = ADAM_B1 * m + (1.0 - ADAM_B1) * g
    v = ADAM_B2 * v + (1.0 - ADAM_B2) * (g * g)
    m_hat = m / (1.0 - ADAM_B1 ** ADAM_STEP)
    v_hat = v / (1.0 - ADAM_B2 ** ADAM_STEP)
    delta = -ADAM_LR * (m_hat / (jnp.sqrt(v_hat) + ADAM_EPS) + ADAM_WD * w)
    return delta, m, v


def _adamw_halves(core, w, mine, recv, m, v, name):
    rows, cols = w.shape
    rh = rows // 2
    tr = _row_tile(rh, cols, budget=1024 * 1024)
    nh = rh // tr

    def body(core_ref, w_ref, a_ref, b_ref, m_ref, v_ref, g_ref, d_ref, mo_ref, vo_ref):
        g = jnp.where(pl.program_id(0) == core_ref[0], a_ref[...], b_ref[...])
        d, mn, vn = _adamw_math(w_ref[...], g, m_ref[...], v_ref[...])
        g_ref[...] = g
        d_ref[...] = d
        mo_ref[...] = mn
        vo_ref[...] = vn

    full = pl.BlockSpec((tr, cols), lambda h, i, c: (h * nh + i, 0))
    mine_spec = pl.BlockSpec((tr, cols), lambda h, i, c: (jnp.where(h == c[0], i, 0), 0))
    recv_spec = pl.BlockSpec((tr, cols), lambda h, i, c: (jnp.where(h == c[0], 0, i), 0))
    sd = jax.ShapeDtypeStruct((rows, cols), F32)
    return pl.pallas_call(
        body, name=name,
        grid_spec=pltpu.PrefetchScalarGridSpec(
            num_scalar_prefetch=1, grid=(2, nh),
            in_specs=[full, mine_spec, recv_spec, full, full], out_specs=[full] * 4),
        out_shape=[sd, sd, sd, sd], compiler_params=_params(),
    )(core, w, mine, recv, m, v)


def _inproj_nat(x, mod, norm_w, w, *, tm=1024, tn=1664):
    S, D = x.shape
    N = w.shape[1]
    nt, nj = S // tm, N // tn
    chunk = 256

    def body(x_ref, mod_ref, nw_ref, w_ref, proj_ref, h_ref, ht_ref, h_a, ht_a, h_b, ht_b):
        i, j = pl.program_id(0), pl.program_id(1)

        def make_h(h_s, ht_s):
            shift = mod_ref[:, 0:D]
            scale1 = 1.0 + mod_ref[:, D:2 * D]
            for r in range(tm // chunk):
                xv = x_ref[pl.ds(r * chunk, chunk), :]
                ms = jnp.mean(xv * xv, axis=-1, keepdims=True)
                h = (xv * lax.rsqrt(ms + EPS) * nw_ref[...]) * scale1 + shift
                h_s[pl.ds(r * chunk, chunk), :] = h.astype(BF16)
                ht_s[:, pl.ds(r * chunk, chunk)] = h.T.astype(BF16)

        @pl.when((i == 0) & (j == 0))
        def _():
            make_h(h_a, ht_a)

        def step(cur, nxt):
            @pl.when(j == 0)
            def _():
                h_ref[...] = cur[0][...]
                ht_ref[...] = cur[1][...]

            def project():
                proj_ref[...] = jnp.dot(cur[0][...], w_ref[...], preferred_element_type=F32).astype(BF16)

            ahead = (j == nj - 1) & (i < nt - 1)

            @pl.when(ahead)
            def _():
                project()
                make_h(*nxt)

            @pl.when(jnp.logical_not(ahead))
            def _():
                project()

        @pl.when(i % 2 == 0)
        def _():
            step((h_a, ht_a), (h_b, ht_b))

        @pl.when(i % 2 == 1)
        def _():
            step((h_b, ht_b), (h_a, ht_a))

    def x_tile(i, j):
        return (jnp.where((i == 0) & (j == 0), 0, jnp.minimum(i + 1, nt - 1)), 0)

    return pl.pallas_call(
        body, name="inproj_nat", grid=(nt, nj),
        in_specs=[pl.BlockSpec((tm, D), x_tile),
                  pl.BlockSpec((1, 3 * D), lambda i, j: (0, 0)),
                  pl.BlockSpec((1, D), lambda i, j: (0, 0)),
                  pl.BlockSpec((D, tn), lambda i, j: (0, j))],
        out_specs=[pl.BlockSpec((tm, tn), lambda i, j: (i, j)),
                   pl.BlockSpec((tm, D), lambda i, j: (i, 0)),
                   pl.BlockSpec((D, tm), lambda i, j: (0, i))],
        out_shape=[jax.ShapeDtypeStruct((S, N), BF16), jax.ShapeDtypeStruct((S, D), BF16),
                   jax.ShapeDtypeStruct((D, S), BF16)],
        scratch_shapes=[pltpu.VMEM((tm, D), BF16), pltpu.VMEM((D, tm), BF16),
                        pltpu.VMEM((tm, D), BF16), pltpu.VMEM((D, tm), BF16)],
        compiler_params=_params(),
    )(x, mod, norm_w, w)


def _inproj_nat_a(x, mod, norm_w, w, later_piece, *, tm=1024, tn=1664):
    S, D = x.shape
    N = w.shape[1]
    nt, nj = S // tm, N // tn
    chunk = 256

    def body(x_ref, mod_ref, nw_ref, w0_ref, w2_ref, p_hbm, part_ref, h_ref, ht_ref, g_hbm, ssem, rsem):
        i, j = pl.program_id(0), pl.program_id(1)

        @pl.when((i == 0) & (j == 0))
        def _():
            for cp in _shard_gather_copies([p_hbm], [g_hbm], ssem, rsem)[0]:
                cp.start()

        @pl.when(j == 0)
        def _():
            shift = mod_ref[:, 0:D]
            scale1 = 1.0 + mod_ref[:, D:2 * D]
            for r in range(tm // chunk):
                xv = x_ref[pl.ds(r * chunk, chunk), :]
                ms = jnp.mean(xv * xv, axis=-1, keepdims=True)
                h = (xv * lax.rsqrt(ms + EPS) * nw_ref[...]) * scale1 + shift
                h_ref[pl.ds(r * chunk, chunk), :] = h.astype(BF16)
                ht_ref[:, pl.ds(r * chunk, chunk)] = h.T.astype(BF16)

        part_ref[...] = (jnp.dot(h_ref[:, 0:256], w0_ref[...], preferred_element_type=F32)
                         + jnp.dot(h_ref[:, 512:768], w2_ref[...], preferred_element_type=F32)).astype(BF16)

        @pl.when((i == nt - 1) & (j == 0))
        def _():
            _, recv, fwd, _ = _shard_gather_copies([p_hbm], [g_hbm], ssem, rsem)
            for r, f in zip(recv, fwd):
                r.wait_recv()
                f.start()

        @pl.when((i == nt - 1) & (j == nj - 1))
        def _():
            send, _, fwd, fwd_recv = _shard_gather_copies([p_hbm], [g_hbm], ssem, rsem)
            for r in fwd_recv:
                r.wait_recv()
            for cp in send + fwd:
                cp.wait_send()

    return pl.pallas_call(
        body, name="inproj_nat_a", grid=(nt, nj),
        in_specs=[pl.BlockSpec((tm, D), lambda i, j: (i, 0)),
                  pl.BlockSpec((1, 3 * D), lambda i, j: (0, 0)),
                  pl.BlockSpec((1, D), lambda i, j: (0, 0)),
                  pl.BlockSpec((256, tn), lambda i, j: (0, j)),
                  pl.BlockSpec((256, tn), lambda i, j: (2, j)), ANY],
        out_specs=[pl.BlockSpec((tm, tn), lambda i, j: (i, j)),
                   pl.BlockSpec((tm, D), lambda i, j: (i, 0)),
                   pl.BlockSpec((D, tm), lambda i, j: (0, i)), ANY],
        out_shape=[jax.ShapeDtypeStruct((S, N), BF16), jax.ShapeDtypeStruct((S, D), BF16),
                   jax.ShapeDtypeStruct((D, S), BF16),
                   jax.ShapeDtypeStruct((4,) + later_piece.shape, later_piece.dtype)],
        scratch_shapes=[pltpu.SemaphoreType.DMA((1, 6)), pltpu.SemaphoreType.DMA((1, 6))],
        compiler_params=_params(),
    )(x, mod, norm_w, w, w, later_piece)


def _inproj_nat_b(h, partial, w, *, tm=1024, tn=1664):
    S, D = h.shape
    N = w.shape[1]

    def body(h_ref, p_ref, w1_ref, w3_ref, o_ref):
        acc = p_ref[...].astype(F32) + jnp.dot(h_ref[:, 256:512], w1_ref[...], preferred_element_type=F32)
        o_ref[...] = (acc + jnp.dot(h_ref[:, 768:1024], w3_ref[...], preferred_element_type=F32)).astype(BF16)

    return pl.pallas_call(
        body, name="inproj_nat_b", grid=(S // tm, N // tn),
        in_specs=[pl.BlockSpec((tm, D), lambda i, j: (i, 0)),
                  pl.BlockSpec((tm, tn), lambda i, j: (i, j)),
                  pl.BlockSpec((256, tn), lambda i, j: (1, j)),
                  pl.BlockSpec((256, tn), lambda i, j: (3, j))],
        out_specs=pl.BlockSpec((tm, tn), lambda i, j: (i, j)),
        out_shape=jax.ShapeDtypeStruct((S, N), BF16),
        compiler_params=_params(),
    )(h, partial, w, w)


def _inproj_qkv(a, b, shards, *, tm=1024, tn=1536):
    M, K = a.shape
    N = b.shape[1]
    n = len(shards)
    ni, nj = M // tm, N // tn

    def body(a_ref, b_ref, *rest):
        ins, o_ref, outs, sems = rest[:n], rest[n], rest[n + 1:2 * n + 1], rest[2 * n + 1:]
        i, j = pl.program_id(0), pl.program_id(1)
        if n:
            @pl.when((i == 0) & (j == 0))
            def _():
                for cp in _shard_gather_copies(ins, outs, *sems)[0]:
                    cp.start()

        o_ref[...] = jnp.dot(a_ref[...], b_ref[...], preferred_element_type=F32).astype(o_ref.dtype)

        if n:
            @pl.when((i == ni - 1) & (j == 0))
            def _():
                _, recv, fwd, _ = _shard_gather_copies(ins, outs, *sems)
                for r, f in zip(recv, fwd):
                    r.wait_recv()
                    f.start()

            @pl.when((i == ni - 1) & (j == nj - 1))
            def _():
                send, _, fwd, fwd_recv = _shard_gather_copies(ins, outs, *sems)
                for r in fwd_recv:
                    r.wait_recv()
                for cp in send + fwd:
                    cp.wait_send()

    outs = pl.pallas_call(
        body, name="inproj_qkv", grid=(ni, nj),
        in_specs=[pl.BlockSpec((tm, K), lambda i, j: (i, 0)), pl.BlockSpec((K, tn), lambda i, j: (0, j))] + [ANY] * n,
        out_specs=[pl.BlockSpec((tm, tn), lambda i, j: (i, j))] + [ANY] * n,
        out_shape=[jax.ShapeDtypeStruct((M, N), BF16)] + [jax.ShapeDtypeStruct((4,) + s.shape, s.dtype) for s in shards],
        scratch_shapes=[pltpu.SemaphoreType.DMA((n, 6)), pltpu.SemaphoreType.DMA((n, 6))] if n else [],
        compiler_params=_params(),
    )(a, b, *shards)
    return outs[0], outs[1:]


def _matmul_lhs_resident(a, b, *, tn, name, exchange=None):
    M, K = a.shape
    N = b.shape[1]
    nj = N // tn
    hosts = exchange is not None

    def body(a_hbm, b_ref, *rest):
        if hosts:
            x_hbm, o_ref, r_hbm, a_s, ssem, rsem = rest
        else:
            o_ref, a_s = rest
        j = pl.program_id(0)

        def to_sibling():
            mx, my, mc = _coords()
            return pltpu.make_async_remote_copy(src_ref=x_hbm.at[1 - mc], dst_ref=r_hbm, send_sem=ssem, recv_sem=rsem,
                                                device_id=(mx, my, 1 - mc), device_id_type=MESH)

        @pl.when(j == 0)
        def _():
            if hosts:
                to_sibling().start()
            pltpu.sync_copy(a_hbm, a_s)

        o_ref[...] = jnp.dot(a_s[...], b_ref[...], preferred_element_type=F32).astype(o_ref.dtype)

        if hosts:
            @pl.when(j == nj - 1)
            def _():
                cp = to_sibling()
                cp.wait_recv()
                cp.wait_send()

    prod = jax.ShapeDtypeStruct((M, N), BF16)
    out = pl.pallas_call(
        body, name=name, grid=(nj,),
        in_specs=[ANY, pl.BlockSpec((K, tn), lambda j: (0, j))] + ([ANY] if hosts else []),
        out_specs=[pl.BlockSpec((M, tn), lambda j: (0, j)), ANY] if hosts else pl.BlockSpec((M, tn), lambda j: (0, j)),
        out_shape=[prod, jax.ShapeDtypeStruct(exchange.shape[1:], exchange.dtype)] if hosts else prod,
        scratch_shapes=[pltpu.VMEM((M, K), BF16)]
                       + ([pltpu.SemaphoreType.DMA(()), pltpu.SemaphoreType.DMA(())] if hosts else []),
        compiler_params=_params(),
    )(a, b, *([exchange] if hosts else []))
    return out


SHARD_COLS = 2816


def _column_tables(part):
    if part == "nat":
        bw = 256
        orig = [j * bw if j < 16 else 8704 + (j - 16) * bw for j in range(NAT // bw)]
    else:
        bw = 128
        orig = []
        for jj in range(NQKV // bw):
            g, p, t = jj // 12, (jj % 12) // 3, jj % 3
            orig.append(4096 + t * 1536 + g * 512 + p * 128)
    tk = jnp.array([o // SHARD_COLS for o in orig], jnp.int32)
    tc = jnp.array([(o % SHARD_COLS) // bw for o in orig], jnp.int32)
    return tk, tc, bw


def _weights_prep(gath, own, chipv, part, piece, into):
    D = D_MODEL
    tk, tc, bw = _column_tables(part)
    n = tk.shape[0]
    per_step = 2 if part == "nat" else 4
    assert n % per_step == 0

    def body(tk_ref, tc_ref, chip_ref, *refs):
        w_ref = refs[-1]
        for u in range(per_step):
            g_ref, o_ref = refs[2 * u], refs[2 * u + 1]
            mine = tk_ref[pl.program_id(1) * per_step + u] == chip_ref[0]
            w_ref[:, u * bw:(u + 1) * bw] = jnp.where(mine, o_ref[...], g_ref[...])

    def gath_spec(u):
        def idx(h, j, tk, tc, ch):
            k = tk[j * per_step + u]
            return (jnp.where(k == ch[0], (k + 1) % 4, k), h, 0, tc[j * per_step + u])
        return pl.BlockSpec((None, None, 256, bw), idx)

    def own_spec(u):
        return pl.BlockSpec((None, 256, bw), lambda h, j, tk, tc, ch: (h, 0, tc[j * per_step + u]))

    specs = []
    for u in range(per_step):
        specs += [gath_spec(u), own_spec(u)]
    return pl.pallas_call(
        body, name=f"weights_prep_{part}_{piece}",
        grid_spec=pltpu.PrefetchScalarGridSpec(
            num_scalar_prefetch=3, grid=(2, n // per_step),
            in_specs=specs + [ANY],
            out_specs=pl.BlockSpec((256, per_step * bw), lambda h, j, tk, tc, ch: (2 * h + piece, j))),
        out_shape=jax.ShapeDtypeStruct((D, n * bw), BF16),
        input_output_aliases={3 + 2 * per_step: 0},
        compiler_params=_params(),
    )(tk, tc, chipv, *([gath, own] * per_step), into)


def _pair_add_to_shards(core, gw, r, into, part):
    D = D_MODEL
    tk, tc, bw = _column_tables(part)
    n = tk.shape[0]

    def body(tk_ref, tc_ref, core_ref, g_ref, r_ref, into_ref, o_ref):
        o_ref[...] = (g_ref[...].astype(F32) + r_ref[...].astype(F32)).astype(BF16)

    return pl.pallas_call(
        body, name="grad_pair_add_" + part,
        grid_spec=pltpu.PrefetchScalarGridSpec(
            num_scalar_prefetch=3, grid=(n,),
            in_specs=[pl.BlockSpec((None, D // 2, bw), lambda j, tk, tc, c: (c[0], 0, j)),
                      pl.BlockSpec((D // 2, bw), lambda j, tk, tc, c: (0, j)), ANY],
            out_specs=pl.BlockSpec((None, D // 2, bw), lambda j, tk, tc, c: (tk[j], 0, tc[j]))),
        out_shape=jax.ShapeDtypeStruct(into.shape, BF16),
        input_output_aliases={5: 0},
        compiler_params=_params(),
    )(tk, tc, core, gw, r, into)


def _head_norm(xb, w, ebd):
    x = xb.astype(F32)
    ss = jnp.dot((x * x).astype(BF16), ebd, preferred_element_type=F32)
    return x * lax.rsqrt(ss * (1.0 / HEAD_DIM) + EPS) * w


def _window_mask(no_prev):
    qi = lax.broadcasted_iota(jnp.int32, (128, 256), 0)
    kc = lax.broadcasted_iota(jnp.int32, (128, 256), 1)
    ok = (kc >= qi) & (kc <= qi + 128)
    if no_prev is not None:
        ok = ok & ((kc >= 128) | jnp.logical_not(no_prev))
    return ok


def _lane_masks():
    lane = lax.broadcasted_iota(jnp.int32, (1, 128), 1)
    return [(lane < HEAD_DIM).astype(F32), (lane >= HEAD_DIM).astype(F32)]


def _head_col(v, mh):
    return jnp.sum(v * mh, axis=1, keepdims=True) * (1.0 / HEAD_DIM)


def _attn_fwd(projq, qnw2, knw2, ebd, g):
    S = projq.shape[0]
    d = DILATIONS[g]
    T = ATT_TILE
    nt = S // T
    nb = T // (128 * d)
    sdt = BF16 if d == 1 else F32

    def body(cur_ref, qw_ref, kw_ref, ebd_ref, o_ref, lse_ref, qs_s, ks_s, vs_s):
        t = pl.program_id(1)
        e = ebd_ref[...]

        @pl.when(t == 0)
        def _():
            ks_s[pl.ds(0, T), :] = jnp.zeros((T, 128), sdt)
            vs_s[pl.ds(0, T), :] = jnp.zeros((T, 128), sdt)

        @pl.when(t > 0)
        def _():
            ks_s[pl.ds(0, T), :] = ks_s[pl.ds(T, T), :]
            vs_s[pl.ds(0, T), :] = vs_s[pl.ds(T, T), :]

        qs_s[...] = (_head_norm(cur_ref[:, 0:128], qw_ref[...], e) * (QK_SCALE * LOG2E)).astype(sdt)
        ks_s[pl.ds(T, T), :] = _head_norm(cur_ref[:, 128:256], kw_ref[...], e).astype(sdt)
        vs_s[pl.ds(T, T), :] = cur_ref[:, 256:384].astype(sdt)
        m0, m1 = _lane_masks()
        first = t == 0
        for r in range(d):
            for bb in range(nb):
                q0 = r + bb * 128 * d
                k0 = T + r + (bb - 1) * 128 * d
                qs = qs_s[pl.ds(q0, 128, stride=d), :].astype(F32)
                kb = ks_s[pl.ds(k0, 256, stride=d), :].astype(BF16)
                vb = vs_s[pl.ds(k0, 256, stride=d), :].astype(BF16)
                ok = _window_mask(first if bb == 0 else None)
                ok2 = jnp.concatenate([ok, ok], axis=0)
                q2 = jnp.concatenate([qs * m0, qs * m1], axis=0).astype(BF16)
                s = lax.dot_general(q2, kb, (((1,), (1,)), ((), ())), preferred_element_type=F32)
                s = jnp.where(ok2, s, NEG_INF)
                mx = jnp.max(s, axis=-1, keepdims=True)
                p = jnp.exp2(s - mx)
                l = jnp.sum(p, axis=-1, keepdims=True)
                o = jnp.dot(p.astype(BF16), vb, preferred_element_type=F32) / l
                lse = mx * LN2 + jnp.log(l)
                o_ref[pl.ds(q0, 128, stride=d), :] = jnp.where(m0 > 0, o[0:128], o[128:256])
                lse_ref[pl.ds(q0, 128, stride=d), :] = jnp.where(m0 > 0, lse[0:128], lse[128:256])

    return pl.pallas_call(
        body, name=f"attn_fwd_g{g}", grid=(4, nt),
        in_specs=[pl.BlockSpec((T, 384), lambda p, t: (t, g * 4 + p)),
                  pl.BlockSpec((1, 128), lambda p, t: (0, 0)),
                  pl.BlockSpec((1, 128), lambda p, t: (0, 0)),
                  pl.BlockSpec((128, 128), lambda p, t: (0, 0))],
        out_specs=[pl.BlockSpec((T, 128), lambda p, t: (t, p)), pl.BlockSpec((T, 128), lambda p, t: (t, p))],
        out_shape=[jax.ShapeDtypeStruct((S, ATTN_OUT), F32), jax.ShapeDtypeStruct((S, ATTN_OUT), F32)],
        scratch_shapes=[pltpu.VMEM((T, 128), sdt), pltpu.VMEM((2 * T, 128), sdt), pltpu.VMEM((2 * T, 128), sdt)],
        compiler_params=_params(),
    )(projq, qnw2, knw2, ebd)


def _attn_bwd(projq, dprojq, d_o, lse, delta, qnw2, knw2, ebd, g):
    S = projq.shape[0]
    d = DILATIONS[g]
    T = ATT_TILE
    nt = S // T
    nb = T // (128 * d)
    sdt = F32

    def norm_bwd(raw_b, dy, w, e):
        x = raw_b.astype(F32)
        ss = jnp.dot((x * x).astype(BF16), e, preferred_element_type=F32)
        rstd = lax.rsqrt(ss * (1.0 / HEAD_DIM) + EPS)
        xh = x * rstd
        gw = jnp.sum(dy * xh, axis=0, keepdims=True)
        dxh = dy * w
        mean = jnp.dot((dxh * xh).astype(BF16), e, preferred_element_type=F32) * (1.0 / HEAD_DIM)
        return rstd * (dxh - xh * mean), gw

    nt_dims = (((1,), (1,)), ((), ()))
    tn_dims = (((0,), (0,)), ((), ()))

    def unit_stacked(qs, dos, ls, dls, kb, vb, ok, m0, m1):
        ok2 = jnp.concatenate([ok, ok], axis=0)
        q2 = jnp.concatenate([qs * m0, qs * m1], axis=0).astype(BF16)
        do2 = jnp.concatenate([dos * m0, dos * m1], axis=0).astype(BF16)
        lcol = jnp.concatenate([_head_col(ls, m0), _head_col(ls, m1)], axis=0)
        dcol = jnp.concatenate([_head_col(dls, m0), _head_col(dls, m1)], axis=0)
        s = jnp.where(ok2, lax.dot_general(q2, kb, nt_dims, preferred_element_type=F32), NEG_INF)
        p = jnp.exp2(s - lcol * LOG2E)
        dp = lax.dot_general(do2, vb, nt_dims, preferred_element_type=F32)
        ds = (p * (dp - dcol)).astype(BF16)
        dq2 = jnp.dot(ds, kb, preferred_element_type=F32)
        dk = lax.dot_general(ds, q2, tn_dims, preferred_element_type=F32)
        dv = lax.dot_general(p.astype(BF16), do2, tn_dims, preferred_element_type=F32)
        return dq2[0:128] * m0 + dq2[128:256] * m1, dk, dv

    def unit_per_head(qs, dos, ls, dls, kb, vb, ok, m0, m1):
        dq_t = dk_t = dv_t = None
        for mh in (m0, m1):
            qh = (qs * mh).astype(BF16)
            doh = (dos * mh).astype(BF16)
            s = jnp.where(ok, lax.dot_general(qh, kb, nt_dims, preferred_element_type=F32), NEG_INF)
            p = jnp.exp2(s - _head_col(ls, mh) * LOG2E)
            dp = lax.dot_general(doh, vb, nt_dims, preferred_element_type=F32)
            ds = (p * (dp - _head_col(dls, mh))).astype(BF16)
            dq = jnp.dot(ds, kb, preferred_element_type=F32) * mh
            dk = lax.dot_general(ds, qh, tn_dims, preferred_element_type=F32)
            dv = lax.dot_general(p.astype(BF16), doh, tn_dims, preferred_element_type=F32)
            dq_t = dq if dq_t is None else dq_t + dq
            dk_t = dk if dk_t is None else dk_t + dk
            dv_t = dv if dv_t is None else dv_t + dv
        return dq_t, dk_t, dv_t

    unit = unit_per_head if d == 1 else unit_stacked

    def body(cur_ref, prev_ref, do_ref, lse_ref, dl_ref, qw_ref, kw_ref, ebd_ref, dp_in_ref,
             out_ref, gqk_ref, qs_s, ks_s, vs_s, do_s, dq_cur, dq_prev, dk_acc, dv_acc):
        i = pl.program_id(1)
        e = ebd_ref[...]
        m0, m1 = _lane_masks()

        @pl.when(i == 0)
        def _():
            dk_acc[...] = jnp.zeros_like(dk_acc)
            dv_acc[...] = jnp.zeros_like(dv_acc)
            dq_prev[...] = jnp.zeros_like(dq_prev)
            gqk_ref[...] = jnp.zeros_like(gqk_ref)

        @pl.when(i < nt)
        def _():
            qs_s[...] = _head_norm(cur_ref[:, 0:128], qw_ref[...], e) * (QK_SCALE * LOG2E)
            ks_s[pl.ds(0, T), :] = _head_norm(prev_ref[:, 128:256], kw_ref[...], e)
            ks_s[pl.ds(T, T), :] = _head_norm(cur_ref[:, 128:256], kw_ref[...], e)
            vs_s[pl.ds(0, T), :] = prev_ref[:, 256:384].astype(F32)
            vs_s[pl.ds(T, T), :] = cur_ref[:, 256:384].astype(F32)
            do_s[...] = do_ref[...].astype(F32)
            first = i == 0
            for r in range(d):
                own_k = own_v = None
                for bb in range(nb):
                    q0 = r + bb * 128 * d
                    k0 = T + r + (bb - 1) * 128 * d
                    qs = qs_s[pl.ds(q0, 128, stride=d), :]
                    dos = do_s[pl.ds(q0, 128, stride=d), :]
                    ls = lse_ref[pl.ds(q0, 128, stride=d), :]
                    dls = dl_ref[pl.ds(q0, 128, stride=d), :]
                    kb = ks_s[pl.ds(k0, 256, stride=d), :].astype(BF16)
                    vb = vs_s[pl.ds(k0, 256, stride=d), :].astype(BF16)
                    ok = _window_mask(first if bb == 0 else None)
                    dq_t, dk_t, dv_t = unit(qs, dos, ls, dls, kb, vb, ok, m0, m1)
                    dq_cur[pl.ds(q0, 128, stride=d), :] = dq_t
                    before = pl.ds(k0, 128, stride=d)
                    if bb == 0:
                        dk_acc[before, :] = dk_acc[before, :] + dk_t[0:128]
                        dv_acc[before, :] = dv_acc[before, :] + dv_t[0:128]
                    else:
                        dk_acc[before, :] = own_k + dk_t[0:128]
                        dv_acc[before, :] = own_v + dv_t[0:128]
                    own_k, own_v = dk_t[128:256], dv_t[128:256]
                tail = pl.ds(T + r + (nb - 1) * 128 * d, 128, stride=d)
                dk_acc[tail, :] = own_k
                dv_acc[tail, :] = own_v

        @pl.when(i >= 1)
        def _():
            dq_raw, gq = norm_bwd(prev_ref[:, 0:128], dq_prev[...] * QK_SCALE, qw_ref[...], e)
            dk_raw, gk = norm_bwd(prev_ref[:, 128:256], dk_acc[pl.ds(0, T), :] * LN2, kw_ref[...], e)
            out_ref[:, 0:128] = dq_raw.astype(BF16)
            out_ref[:, 128:256] = dk_raw.astype(BF16)
            out_ref[:, 256:384] = dv_acc[pl.ds(0, T), :].astype(BF16)
            gqk_ref[0:1, :] += gq
            gqk_ref[1:2, :] += gk

        dq_prev[...] = dq_cur[...]
        dk_acc[pl.ds(0, T), :] = dk_acc[pl.ds(T, T), :]
        dv_acc[pl.ds(0, T), :] = dv_acc[pl.ds(T, T), :]

    last = nt - 1
    qtile = lambda p, i: (jnp.minimum(i, last), p)
    return pl.pallas_call(
        body, name=f"attn_bwd_g{g}", grid=(4, nt + 1),
        in_specs=[pl.BlockSpec((T, 384), lambda p, i: (jnp.minimum(i, last), g * 4 + p)),
                  pl.BlockSpec((T, 384), lambda p, i: (jnp.maximum(i - 1, 0), g * 4 + p)),
                  pl.BlockSpec((T, 128), qtile), pl.BlockSpec((T, 128), qtile), pl.BlockSpec((T, 128), qtile),
                  pl.BlockSpec((1, 128), lambda p, i: (0, 0)),
                  pl.BlockSpec((1, 128), lambda p, i: (0, 0)),
                  pl.BlockSpec((128, 128), lambda p, i: (0, 0)),
                  ANY],
        out_specs=[pl.BlockSpec((T, 384), lambda p, i: (jnp.maximum(i - 1, 0), g * 4 + p)),
                   pl.BlockSpec((None, 8, 128), lambda p, i: (p, 0, 0))],
        out_shape=[jax.ShapeDtypeStruct(dprojq.shape, BF16), jax.ShapeDtypeStruct((4, 8, 128), F32)],
        scratch_shapes=[pltpu.VMEM((T, 128), sdt), pltpu.VMEM((2 * T, 128), sdt), pltpu.VMEM((2 * T, 128), sdt),
                        pltpu.VMEM((T, 128), sdt), pltpu.VMEM((T, 128), F32), pltpu.VMEM((T, 128), F32),
                        pltpu.VMEM((2 * T, 128), F32), pltpu.VMEM((2 * T, 128), F32)],
        input_output_aliases={8: 0},
        compiler_params=_params(),
    )(projq, projq, d_o, lse, delta, qnw2, knw2, ebd, dprojq)


def _mid(projn, o_g, lse_g, x, tgt, mod, convw8, wc, wa, wo, ebd, *, tm=256):
    S, D = x.shape
    nt = S // tm
    nst = max(1, 256 // tm)
    assert nt % nst == 0
    hb = tm // 16

    def body(pn_ref, hc_ref, hx_ref, o0_ref, o1_ref, o2_ref, l0_ref, l1_ref, l2_ref, x_ref, t_ref, mod_ref,
             cw_ref, ebd_ref, wc_hbm, wa_hbm, wo_hbm,
             dpn_ref, do_ref, lse_ref, dl_ref, dout_ref, vacc_ref, gwo_hbm, gwc_hbm, gwa_hbm,
             wc_s, wa_s, wo_s, gwo_s, gwc_s, gwa_s, carry_s, *stashes):
        i = pl.program_id(0)
        ti = nt - 1 - i
        par = i % nst
        stash = pl.ds(pl.multiple_of(par * tm, tm), tm)

        @pl.when(i == 0)
        def _():
            for src, dst in ((wc_hbm, wc_s), (wa_hbm, wa_s), (wo_hbm, wo_s)):
                pltpu.sync_copy(src, dst)
            gwo_s[...] = jnp.zeros_like(gwo_s)
            gwc_s[...] = jnp.zeros_like(gwc_s)
            gwa_s[...] = jnp.zeros_like(gwa_s)
            carry_s[...] = jnp.zeros_like(carry_s)
            vacc_ref[...] = jnp.zeros_like(vacc_ref)

        rows = lax.broadcasted_iota(jnp.int32, (tm, D), 0)
        w0, w1, w2 = cw_ref[0:1, :], cw_ref[1:2, :], cw_ref[2:3, :]
        gate = mod_ref[:, 2 * D:3 * D]

        b_a = pn_ref[:, BA:BA + D].astype(F32)
        c_a = pn_ref[:, CA:CA + D].astype(F32)
        x_a = pn_ref[:, XA:XA + D].astype(F32)
        z_a = pn_ref[:, ZA:ZA + D].astype(F32)
        u = c_a * x_a
        has_prev = (ti > 0).astype(F32)
        uh = hc_ref[...].astype(F32) * hx_ref[...].astype(F32) * has_prev
        h14, h15 = _row(uh, 14), _row(uh, 15)
        u_m1 = jnp.where(rows == 0, h15, pltpu.roll(u, 1, 0))
        u_m2 = jnp.where(rows == 0, h14, jnp.where(rows == 1, h15, pltpu.roll(u, 2, 0)))
        conv = w0 * u_m2 + w1 * u_m1 + w2 * u
        sg_a = _sigmoid(z_a)
        sz_a = z_a * sg_a
        y_a = b_a * conv * sz_a
        y_ab = y_a.astype(BF16)
        pa = jnp.dot(y_ab, wc_s[...], preferred_element_type=F32)

        l0, l1, l2 = l0_ref[...], l1_ref[...], l2_ref[...]
        mxl = jnp.maximum(jnp.maximum(l0, l1), l2)
        e0, e1, e2 = jnp.exp(l0 - mxl), jnp.exp(l1 - mxl), jnp.exp(l2 - mxl)
        den = e0 + e1 + e2
        attn = (e0 * o0_ref[...] + e1 * o1_ref[...] + e2 * o2_ref[...]) / den
        lse_ref[...] = mxl + jnp.log(den)
        z_b = pn_ref[:, ZB:ZB + ATTN_OUT].astype(F32)
        sg_b = _sigmoid(z_b)
        sz_b = z_b * sg_b
        y_b = attn * sz_b
        y_bb = y_b.astype(BF16)
        pb = jnp.dot(y_bb, wa_s[...], preferred_element_type=F32)

        s_a = _sigmoid(pn_ref[:, GA:GA + D].astype(F32))
        s_b = _sigmoid(pn_ref[:, GB:GB + D].astype(F32))
        merged = s_a * pa + s_b * pb
        merged_b = merged.astype(BF16)
        mo = jnp.dot(merged_b, wo_s[...], preferred_element_type=F32)
        diff = x_ref[...] + gate * mo - t_ref[...]
        dout = diff * (1.0 / D)
        dout_ref[...] = dout.astype(BF16)
        vacc_ref[4:5, :] += jnp.sum(diff * diff, axis=0, keepdims=True)
        vacc_ref[0:1, :] += jnp.sum(dout * mo, axis=0, keepdims=True)

        d_mo = (dout * gate).astype(BF16)
        nt_dims = (((1,), (1,)), ((), ()))
        dmerged = lax.dot_general(d_mo, wo_s[...], nt_dims, preferred_element_type=F32)
        dpa = (dmerged * s_a).astype(BF16)
        dpb = (dmerged * s_b).astype(BF16)
        dpn_ref[:, GA:GA + D] = (dmerged * pa * s_a * (1.0 - s_a)).astype(BF16)
        dpn_ref[:, GB:GB + D] = (dmerged * pb * s_b * (1.0 - s_b)).astype(BF16)
        tn = (((0,), (0,)), ((), ()))
        if nst == 1:
            gwo_s[...] += lax.dot_general(merged_b, d_mo, tn, preferred_element_type=F32)
            gwc_s[...] += lax.dot_general(y_ab, dpa, tn, preferred_element_type=F32)
            gwa_s[...] += lax.dot_general(y_bb, dpb, tn, preferred_element_type=F32)
        else:
            st_m, st_d, st_ya, st_pa, st_yb, st_pb = stashes
            st_m[stash, :] = merged_b
            st_d[stash, :] = d_mo
            st_ya[stash, :] = y_ab
            st_pa[stash, :] = dpa
            st_yb[stash, :] = y_bb
            st_pb[stash, :] = dpb

            @pl.when(par == nst - 1)
            def _():
                gwo_s[...] += lax.dot_general(st_m[...], st_d[...], tn, preferred_element_type=F32)
                gwc_s[...] += lax.dot_general(st_ya[...], st_pa[...], tn, preferred_element_type=F32)
                gwa_s[...] += lax.dot_general(st_yb[...], st_pb[...], tn, preferred_element_type=F32)

        dy_a = lax.dot_general(dpa, wc_s[...], nt_dims, preferred_element_type=F32)
        dy_b = lax.dot_general(dpb, wa_s[...], nt_dims, preferred_element_type=F32)

        dattn = dy_b * sz_b
        dpn_ref[:, ZB:ZB + ATTN_OUT] = (dy_b * attn * sg_b * (1.0 + z_b * (1.0 - sg_b))).astype(BF16)
        do_ref[...] = dattn.astype(BF16)
        dl_ref[...] = jnp.dot((dattn * attn).astype(BF16), ebd_ref[...], preferred_element_type=F32)

        dpn_ref[:, BA:BA + D] = (dy_a * conv * sz_a).astype(BF16)
        dpn_ref[:, ZA:ZA + D] = (dy_a * b_a * conv * sg_a * (1.0 + z_a * (1.0 - sg_a))).astype(BF16)
        dconv = dy_a * b_a * sz_a
        vacc_ref[1:2, :] += jnp.sum(dconv * u_m2, axis=0, keepdims=True)
        vacc_ref[2:3, :] += jnp.sum(dconv * u_m1, axis=0, keepdims=True)
        vacc_ref[3:4, :] += jnp.sum(dconv * u, axis=0, keepdims=True)
        nxt = carry_s[...]
        n0, n1 = _row(nxt, 0), _row(nxt, 1)
        dc_p1 = jnp.where(rows == tm - 1, n0, pltpu.roll(dconv, tm - 1, 0))
        dc_p2 = jnp.where(rows == tm - 1, n1, jnp.where(rows == tm - 2, n0, pltpu.roll(dconv, tm - 2, 0)))
        du = w2 * dconv + w1 * dc_p1 + w0 * dc_p2
        carry_s[...] = dconv[0:8, :]
        dpn_ref[:, CA:CA + D] = (du * x_a).astype(BF16)
        dpn_ref[:, XA:XA + D] = (du * c_a).astype(BF16)

        @pl.when(i == nt - 1)
        def _():
            pltpu.sync_copy(gwo_s, gwo_hbm)
            pltpu.sync_copy(gwc_s, gwc_hbm)
            pltpu.sync_copy(gwa_s, gwa_hbm)

    rev = lambda i: (nt - 1 - i, 0)
    halo = lambda col: pl.BlockSpec((16, D), lambda i: (jnp.maximum((nt - 1 - i) * hb - 1, 0), col))
    tile512 = pl.BlockSpec((tm, ATTN_OUT), rev)
    tileD = pl.BlockSpec((tm, D), rev)
    const = lambda shape: pl.BlockSpec(shape, lambda i: (0, 0))
    return pl.pallas_call(
        body, name="mid_fwd_bwd", grid=(nt,),
        in_specs=[pl.BlockSpec((tm, NAT), rev), halo(CA // D), halo(XA // D)] + [tile512] * 6
                 + [tileD, tileD, const((1, 3 * D)), const((8, D)), const((ATTN_OUT, ATTN_OUT))] + [ANY] * 3,
        out_specs=[pl.BlockSpec((tm, NAT), rev), tile512, tile512, tile512, tileD, const((8, D)), ANY, ANY, ANY],
        out_shape=[jax.ShapeDtypeStruct((S, NAT), BF16), jax.ShapeDtypeStruct((S, ATTN_OUT), BF16),
                   jax.ShapeDtypeStruct((S, ATTN_OUT), F32), jax.ShapeDtypeStruct((S, ATTN_OUT), F32),
                   jax.ShapeDtypeStruct((S, D), BF16), jax.ShapeDtypeStruct((8, D), F32),
                   jax.ShapeDtypeStruct((D, D), F32), jax.ShapeDtypeStruct((D, D), F32),
                   jax.ShapeDtypeStruct((ATTN_OUT, D), F32)],
        scratch_shapes=[pltpu.VMEM((D, D), BF16), pltpu.VMEM((ATTN_OUT, D), BF16), pltpu.VMEM((D, D), BF16),
                        pltpu.VMEM((D, D), F32), pltpu.VMEM((D, D), F32), pltpu.VMEM((ATTN_OUT, D), F32),
                        pltpu.VMEM((8, D), F32)]
                       + ([pltpu.VMEM((nst * tm, D), BF16)] * 4
                          + [pltpu.VMEM((nst * tm, ATTN_OUT), BF16), pltpu.VMEM((nst * tm, D), BF16)] if nst > 1 else []),
        compiler_params=_params(vmem=V7X_VMEM_LIMIT_RESIDENT),
    )(projn, projn, projn, *o_g, *lse_g, x, tgt, mod, convw8, ebd, wc, wa, wo)


def _dh_matmul(dpn, dpq, w_nat, w_qkv, parts, *, tm=512):
    S = dpn.shape[0]
    D = D_MODEL
    n = len(parts)
    nt = S // tm

    def body(dn_ref, dq_ref, wn_hbm, wq_hbm, *rest):
        p_hbm, rest = rest[:n], rest[n:]
        dh_ref = rest[0]
        q_hbm = rest[1:1 + n]
        wn_s, wq_s = rest[1 + n:3 + n]
        sems = rest[3 + n:]

        @pl.when(pl.program_id(0) == 0)
        def _():
            if n:
                mine, _ = _scatter_copies(p_hbm, q_hbm, *sems)
                for cp in mine:
                    cp.start()
            pltpu.sync_copy(wn_hbm, wn_s)
            pltpu.sync_copy(wq_hbm, wq_s)

        if n:
            @pl.when(pl.program_id(0) == nt - 1)
            def _():
                mine, theirs = _scatter_copies(p_hbm, q_hbm, *sems)
                for cp in theirs:
                    cp.wait_recv()
                for cp in mine:
                    cp.wait_send()

        nt_dims = (((1,), (1,)), ((), ()))
        dh = (lax.dot_general(dn_ref[...], wn_s[...], nt_dims, preferred_element_type=F32)
              + lax.dot_general(dq_ref[...], wq_s[...], nt_dims, preferred_element_type=F32))
        dh_ref[...] = dh.astype(BF16)

    row = lambda w: pl.BlockSpec((tm, w), lambda i: (i, 0))
    outs = pl.pallas_call(
        body, name="dh_matmul", grid=(nt,),
        in_specs=[row(NAT), row(NQKV), ANY, ANY] + [ANY] * n,
        out_specs=[row(D)] + [ANY] * n,
        out_shape=[jax.ShapeDtypeStruct((S, D), BF16)] + [jax.ShapeDtypeStruct(p.shape, p.dtype) for p in parts],
        scratch_shapes=[pltpu.VMEM((D, NAT), BF16), pltpu.VMEM((D, NQKV), BF16)]
                       + ([pltpu.SemaphoreType.DMA((n, 3)), pltpu.SemaphoreType.DMA((n, 3))] if n else []),
        compiler_params=_params(vmem=V7X_VMEM_LIMIT_RESIDENT),
    )(dpn, dpq, w_nat, w_qkv, *parts)
    return outs[0], outs[1:]


def _norm_bwd(dh, x, dout, mod, norm_w, *, tm=512):
    S, D = x.shape

    def body(dh_ref, x_ref, dout_ref, mod_ref, nw_ref, gx_ref, st_ref):
        @pl.when(pl.program_id(0) == 0)
        def _():
            st_ref[...] = jnp.zeros_like(st_ref)

        dh = dh_ref[...].astype(F32)
        scale1 = 1.0 + mod_ref[:, D:2 * D]
        nw = nw_ref[...]
        xv = x_ref[...]
        rstd = lax.rsqrt(jnp.mean(xv * xv, axis=-1, keepdims=True) + EPS)
        xh = xv * rstd
        dhs = dh * scale1
        dxh = dhs * nw
        dx = rstd * (dxh - xh * jnp.mean(dxh * xh, axis=-1, keepdims=True))
        gx_ref[...] = dout_ref[...].astype(F32) + dx
        st_ref[0:1, :] += jnp.sum(dh, axis=0, keepdims=True)
        st_ref[1:2, :] += jnp.sum(dh * (xh * nw), axis=0, keepdims=True)
        st_ref[2:3, :] += jnp.sum(dhs * xh, axis=0, keepdims=True)

    row = pl.BlockSpec((tm, D), lambda i: (i, 0))
    return pl.pallas_call(
        body, name="norm_bwd", grid=(S // tm,),
        in_specs=[row, row, row, pl.BlockSpec((1, 3 * D), lambda i: (0, 0)), pl.BlockSpec((1, D), lambda i: (0, 0))],
        out_specs=[row, pl.BlockSpec((8, D), lambda i: (0, 0))],
        out_shape=[jax.ShapeDtypeStruct((S, D), F32), jax.ShapeDtypeStruct((8, D), F32)],
        compiler_params=_params(),
    )(dh, x, dout, mod, norm_w)


def _pack_smalls(st, vacc, gqk):
    D = D_MODEL

    def body(st_ref, va_ref, gqk_ref, o_ref):
        o_ref[...] = jnp.zeros_like(o_ref)
        o_ref[0:2, :] = st_ref[0:2, :]
        o_ref[2:3, :] = va_ref[0:1, :]
        o_ref[8:9, :] = st_ref[2:3, :]
        o_ref[16:19, :] = va_ref[1:4, :]
        tot = gqk_ref[0]
        for k in range(1, gqk_ref.shape[0]):
            tot = tot + gqk_ref[k]
        tot = tot + pltpu.roll(tot, HEAD_DIM, 1)
        o_ref[24:32, 0:128] = tot
        loss = jnp.sum(va_ref[4:5, :], axis=1, keepdims=True) * (0.5 / D)
        o_ref[26:27, :] = jnp.broadcast_to(loss, (1, D))

    return pl.pallas_call(
        body, name="pack_smalls", out_shape=jax.ShapeDtypeStruct((32, D), F32),
        in_specs=[pl.BlockSpec(memory_space=pltpu.VMEM)] * 3,
        out_specs=pl.BlockSpec(memory_space=pltpu.VMEM), compiler_params=_params(),
    )(st, vacc, gqk)


def _small_update(sm_loc, dm_pad, ct_pad, ada, vecs):
    D = D_MODEL
    nv = len(vecs)
    places = [(0, 3, D), (8, 1, D), (16, 3, 256), (24, 1, HEAD_DIM), (25, 1, HEAD_DIM)]

    def body(*refs):
        sm_ref, dm_ref, ct_ref = refs[0:3]
        wmv = [refs[3 + 3 * t:6 + 3 * t] for t in range(nv + 1)]
        outs = refs[3 + 3 * (nv + 1):]
        res = [outs[4 * t:4 * t + 4] for t in range(nv + 1)]
        loss_ref, tot_s = outs[4 * (nv + 1)], outs[4 * (nv + 1) + 1]

        def update(g, w, m, v, out, at=(slice(None), slice(None))):
            d, mn, vn = _adamw_math(w[at], g, m[at], v[at])
            for ref, val in zip(out, (g, d, mn, vn)):
                ref[at] = val

        gw = jnp.dot(ct_ref[...], dm_ref[...], preferred_element_type=F32, precision=lax.Precision.HIGHEST)
        update(gw, *wmv[0], res[0])
        tot = sm_ref[0]
        for k in range(1, 8):
            tot = tot + sm_ref[k]
        tot_s[...] = tot
        for k in range(3):
            update(tot_s[k:k + 1, :], *wmv[1], res[1], at=(slice(None), slice(k * D, (k + 1) * D)))
        for t in range(1, nv):
            r0, nr, nl = places[t]
            update(tot_s[r0:r0 + nr, 0:nl], *wmv[1 + t], res[1 + t])
        loss_ref[...] = jnp.broadcast_to(tot_s[26:27, 0:128], (8, 128))

    vm = pl.BlockSpec(memory_space=pltpu.VMEM)
    groups = [ada] + list(vecs)
    out_shape = []
    for w, _, _ in groups:
        out_shape += [jax.ShapeDtypeStruct(w.shape, F32)] * 4
    out_shape.append(jax.ShapeDtypeStruct((8, 128), F32))
    flat = [a for grp in groups for a in grp]
    outs = pl.pallas_call(
        body, name="small_update", out_shape=out_shape,
        in_specs=[vm] * (3 + len(flat)), out_specs=[vm] * len(out_shape),
        scratch_shapes=[pltpu.VMEM((32, D), F32)], compiler_params=_params(),
    )(sm_loc, dm_pad, ct_pad, *flat)
    return [outs[4 * t:4 * t + 4] for t in range(nv + 1)], outs[-1]


def _later_weights(gathered, shards, chip):
    D = D_MODEL
    g_bc, g_ba, g_wo = [lax.dynamic_update_slice(g, s[None], (chip, 0, 0, 0)) for g, s in zip(gathered, shards)]
    wa = g_ba.reshape(4, ATTN_OUT, 256).transpose(1, 0, 2).reshape(ATTN_OUT, D)
    return g_bc.reshape(D, D), wa, g_wo.reshape(D, D)


def _local_step(xs, tg, mod, proj_nat, w_qkv, later, convw8, q_norm_w, k_norm_w):
    S = xs.shape[0]
    lane = jnp.arange(128)
    ebd128 = (lane[:, None] // HEAD_DIM == lane[None, :] // HEAD_DIM).astype(BF16)
    lane5 = jnp.arange(ATTN_OUT)
    ebd512 = (lane5[:, None] // HEAD_DIM == lane5[None, :] // HEAD_DIM).astype(BF16)
    qnw2 = jnp.tile(q_norm_w, (1, 2))
    knw2 = jnp.tile(k_norm_w, (1, 2))

    projn, h, ht = proj_nat
    if len(later) == 2:
        projq, gathered = _inproj_qkv(h, w_qkv, later[0])
        wc, wa, wo = _later_weights(gathered, *later)
    else:
        projq, _ = _inproj_qkv(h, w_qkv, [])
        wc, wa, wo = later
    o_g, lse_g = [], []
    for g in range(N_GROUPS):
        o, l = _attn_fwd(projq, qnw2, knw2, ebd128, g)
        o_g.append(o)
        lse_g.append(l)

    dpn, d_o, lse, delta, dout, vacc, gwo, gwc, gwa = _mid(
        projn, o_g, lse_g, xs, tg, mod, convw8, wc, wa, wo, ebd512)
    dpq = lax.empty((S, NQKV), BF16)
    gqk = []
    for g in range(N_GROUPS):
        dpq, part = _attn_bwd(projq, dpq, d_o, lse, delta, qnw2, knw2, ebd128, g)
        gqk.append(part)
    return dpn, dpq, dout, ht, gwc, gwa, gwo, vacc, gqk


def kernel(x, c, w_ada, b_ada, norm_w, w_in, conv_w, q_norm_w, k_norm_w, w_br_conv, w_br_attn, w_out, loss_target, m_w_ada, m_b_ada, m_norm_w, m_w_in, m_conv_w, m_q_norm_w, m_k_norm_w, m_w_br_conv, m_w_br_attn, m_w_out, v_w_ada, v_b_ada, v_norm_w, v_w_in, v_conv_w, v_q_norm_w, v_k_norm_w, v_w_br_conv, v_w_br_attn, v_w_out):
    D = D_MODEL
    S = x.shape[1]
    xs, tg = x[0], loss_target[0]
    mx, my, mc = lax.axis_index("x"), lax.axis_index("y"), lax.axis_index("c")
    chip = 2 * mx + my
    dev = 2 * chip + mc

    c_blk = jnp.concatenate([c, jnp.pad(conv_w[0], ((0, 0), (0, D - 256))), jnp.zeros((4, D), F32)], axis=0)
    b_shard = lax.dynamic_slice(b_ada, (0, chip * 768), (1, 768))
    shards = [w_in[0].astype(BF16).reshape(2, 512, 2816), w_br_conv[0].astype(BF16).reshape(2, 128, D),
              w_br_attn[0].astype(BF16).reshape(2, 256, 256), w_out[0].astype(BF16).reshape(2, 128, D)]
    quarters = shards[0].reshape(2, 2, 256, SHARD_COLS)
    piece_a, piece_b = quarters[:, 0], quarters[:, 1]
    gath_a, _, cg, mod_parts, silu_c = _weights_allgather(
        piece_a.reshape(2, 2, 128, SHARD_COLS), [], c_blk, w_ada[0], b_shard)
    gath_a = gath_a.reshape(4, 2, 256, SHARD_COLS)
    convw8 = jnp.pad(cg[::2, 1:4, 0:256].transpose(1, 0, 2).reshape(3, D), ((0, 5), (0, 0)))
    mod_all = mod_parts.transpose(1, 0, 2).reshape(8, 3 * D)
    mod = lax.dynamic_slice(mod_all, (dev, 0), (1, 3 * D))
    chipv = jnp.reshape(chip, (1,)).astype(jnp.int32)
    w_nat = _weights_prep(gath_a, piece_a, chipv, "nat", 0, lax.empty((D, NAT), BF16))
    w_qkv = _weights_prep(gath_a, piece_a, chipv, "qkv", 0, lax.empty((D, NQKV), BF16))
    partial, h, ht, gath_b = _inproj_nat_a(xs, mod, norm_w, w_nat, piece_b)
    w_nat = _weights_prep(gath_b, piece_b, chipv, "nat", 1, w_nat)
    w_qkv = _weights_prep(gath_b, piece_b, chipv, "qkv", 1, w_qkv)
    projn = _inproj_nat_b(h, partial, w_nat)

    dpn, dpq, dout, ht, gwc, gwa, gwo, vacc, gqk = _local_step(
        xs, tg, mod, (projn, h, ht), w_qkv, (shards[1:], chip), convw8, q_norm_w, k_norm_w)

    def halves(a):
        k, r, cc = a.shape
        return a.reshape(k, 2, r // 2, cc).transpose(1, 0, 2, 3)

    gw_nat = _matmul_lhs_resident(ht, dpn, tn=512, name="grad_w_in_nat").reshape(2, D // 2, NAT)
    gw_qkv, recv_nat = _matmul_lhs_resident(ht, dpq, tn=512, name="grad_w_in_qkv", exchange=gw_nat)
    grads = [gw_nat, gw_qkv.reshape(2, D // 2, NQKV),
             halves(gwc.astype(BF16).reshape(4, 256, D)),
             halves(gwa.astype(BF16).reshape(ATTN_OUT, 4, 256).transpose(1, 0, 2)),
             halves(gwo.astype(BF16).reshape(4, 256, D))]
    recv = [recv_nat] + list(_pair_exchange(grads[1:]))
    core = jnp.reshape(mc, (1,)).astype(jnp.int32)
    p_in = _pair_add_to_shards(core, grads[0], recv[0], lax.empty((4, D // 2, SHARD_COLS), BF16), "nat")
    p_in = _pair_add_to_shards(core, grads[1], recv[1], p_in, "qkv")
    parts = [p_in] + [_pair_add(core, gr, rc) for gr, rc in zip(grads[2:], recv[2:])]
    dh, gathered = _dh_matmul(dpn, dpq, w_nat, w_qkv, parts)
    grad_x, st = _norm_bwd(dh, xs, dout, mod, norm_w)
    mine = [_sum4(chipv, p, q) for p, q in zip(parts, gathered)]
    theirs, sm_all = _half_exchange(mine, _pack_smalls(st, vacc, jnp.concatenate(gqk, axis=0)))

    big = {}
    for t, (nm, w, m, v) in enumerate((("w_in", w_in, m_w_in, v_w_in), ("w_br_conv", w_br_conv, m_w_br_conv, v_w_br_conv),
                                       ("w_br_attn", w_br_attn, m_w_br_attn, v_w_br_attn), ("w_out", w_out, m_w_out, v_w_out))):
        big[nm] = _adamw_halves(core, w[0], mine[t], theirs[t], m[0], v[0], "adamw_" + nm)

    dmod_all = sm_all[:, 0:3, :].reshape(8, 3 * D)
    dm_pad = jnp.pad(lax.dynamic_slice(dmod_all, (0, chip * 768), (8, 768)), ((0, 120), (0, 0)))
    conv_dev = lax.dynamic_slice(sm_all[:, 16:24, :], (0, 0, chip * 256), (8, 8, 256))
    sm_loc = jnp.concatenate([sm_all[:, 0:16, :], jnp.pad(conv_dev, ((0, 0), (0, 0), (0, D - 256))), sm_all[:, 24:32, :]], axis=1)

    ct_pad = jnp.pad(silu_c.T, ((0, 0), (0, 120)))
    small_names = ["b_ada", "norm_w", "conv_w", "q_norm_w", "k_norm_w"]
    vecs = [(b_ada, m_b_ada, v_b_ada), (norm_w, m_norm_w, v_norm_w), (conv_w[0], m_conv_w[0], v_conv_w[0]),
            (q_norm_w, m_q_norm_w, v_q_norm_w), (k_norm_w, m_k_norm_w, v_k_norm_w)]
    updated, loss_blk = _small_update(sm_loc, dm_pad, ct_pad, (w_ada[0], m_w_ada[0], v_w_ada[0]), vecs)
    small = {"w_ada": [a[None] for a in updated[0]]}
    for nm, four in zip(small_names, updated[1:]):
        small[nm] = [a[None] for a in four] if nm == "conv_w" else list(four)

    order = ["w_ada", "b_ada", "norm_w", "w_in", "conv_w", "q_norm_w", "k_norm_w", "w_br_conv", "w_br_attn", "w_out"]
    res = [loss_blk[0, 0], grad_x[None]]
    for kind in range(4):
        for nm in order:
            res.append(big[nm][kind][None] if nm in big else small[nm][kind])
    return tuple(res)
```

```python
import jax
import jax.numpy as jnp
from jax import lax
from jax.experimental import pallas as pl
from jax.experimental.pallas import tpu as pltpu

F32 = jnp.float32
BF16 = jnp.bfloat16
MESH = pl.DeviceIdType.MESH
ANY = pl.BlockSpec(memory_space=pl.ANY)

D_MODEL = 1024
HEAD_DIM = 64
N_GROUPS = 3
DILATIONS = (1, 4, 16)
ATTN_OUT = 512
EPS = 1e-6
NEG_INF = -1e30
NAT = 6656
NQKV = 4608
BA, CA, XA, ZA, ZB, GA, GB = 0, 1024, 2048, 3072, 4096, 4608, 5632
ATT_TILE = 2048
QK_SCALE = HEAD_DIM ** -0.5
LOG2E = 1.4426950408889634
LN2 = 0.6931471805599453
V7X_VMEM_LIMIT = 56 * 1024 * 1024
V7X_VMEM_LIMIT_RESIDENT = 60 * 1024 * 1024

ADAM_LR = 0.001
ADAM_B1 = 0.9
ADAM_B2 = 0.999
ADAM_EPS = 1e-08
ADAM_WD = 0.01
ADAM_STEP = 10


def _params(vmem=V7X_VMEM_LIMIT, **kw):
    return pltpu.CompilerParams(vmem_limit_bytes=vmem, **kw)


def _sigmoid(z):
    return 1.0 / (1.0 + jnp.exp(-z))


def _row(v, r):
    rows = lax.broadcasted_iota(jnp.int32, v.shape, 0)
    return jnp.sum(jnp.where(rows == r, v, 0.0), axis=0, keepdims=True)


def _coords():
    return lax.axis_index("x"), lax.axis_index("y"), lax.axis_index("c")


def _flip(v, bit):
    return 1 - v if bit else v


def _weights_allgather(big_shard, shards, c_blk, w_ada, b_shard):
    n = len(shards)
    D = D_MODEL
    cols = w_ada.shape[1]

    def body(*refs):
        big_in, ins = refs[0], refs[1:1 + n]
        c_ref, wada_ref, b_ref = refs[1 + n:4 + n]
        big_out, outs = refs[4 + n], refs[5 + n:5 + 2 * n]
        cg_ref, mp_ref, sc_ref = refs[5 + 2 * n:8 + 2 * n]
        sem_refs = refs[8 + 2 * n:]
        ssem, rsem = sem_refs[:2] if n else (None, None)
        bs, br, cs, cr, ms, mr, lsem = sem_refs[2 if n else 0:]
        mx, my, mc = _coords()
        me = 2 * mx + my
        me8 = 2 * me + mc
        sib = (mx, my, 1 - mc)
        chips = [(_flip(mx, j & 2), _flip(my, j & 1)) for j in range(1, 4)]
        sends = []

        local = pltpu.make_async_copy(c_ref, cg_ref.at[me8], lsem)
        local.start()
        for j in range(1, 8):
            peer = (_flip(mx, j & 4), _flip(my, j & 2), _flip(mc, j & 1))
            cp = pltpu.make_async_remote_copy(
                src_ref=c_ref, dst_ref=cg_ref.at[me8], send_sem=cs.at[j - 1], recv_sem=cr.at[j - 1],
                device_id=peer, device_id_type=MESH)
            cp.start()
            sends.append(cp)

        kx, ky, kd = 2 * (1 - mx) + my, 2 * mx + (1 - my), 2 * (1 - mx) + (1 - my)
        xn, yn = (1 - mx, my, mc), (mx, 1 - my, mc)

        def big(src_chip, q, sem, to, half=None):
            h = mc if half is None else half
            return pltpu.make_async_remote_copy(
                src_ref=big_out.at[src_chip, h, q], dst_ref=big_out.at[src_chip, h, q],
                send_sem=bs.at[sem], recv_sem=br.at[sem], device_id=to, device_id_type=MESH)

        def own(q, sem, to):
            return pltpu.make_async_remote_copy(
                src_ref=big_in.at[mc, q], dst_ref=big_out.at[me, mc, q],
                send_sem=bs.at[sem], recv_sem=br.at[sem], device_id=to, device_id_type=MESH)

        for cp in (own(0, 0, xn), own(1, 3, yn), own(1, 1, xn), own(0, 2, yn)):
            cp.start()
            sends.append(cp)
        for t in range(n):
            for j, (px, py) in enumerate(chips):
                cp = pltpu.make_async_remote_copy(
                    src_ref=ins[t].at[mc], dst_ref=outs[t].at[me, mc], send_sem=ssem.at[t, j],
                    recv_sem=rsem.at[t, j], device_id=(px, py, mc), device_id_type=MESH)
                cp.start()
                sends.append(cp)

        pieces = [(kx, 0), (kx, 1), (ky, 0), (ky, 1), (kd, 0), (kd, 1)]

        def landed(piece, sem, frm, pass_on=None):
            src_chip, q = pieces[piece]
            big(src_chip, q, sem, frm).wait_recv()
            nxt = ([] if pass_on is None else [big(src_chip, q, pass_on[0], pass_on[1])])
            nxt.append(big(src_chip, q, 6 + piece, sib))
            for cp in nxt:
                cp.start()
                sends.append(cp)

        for j in range(1, 8):
            px, py, pc = _flip(mx, j & 4), _flip(my, j & 2), _flip(mc, j & 1)
            pltpu.make_async_remote_copy(
                src_ref=c_ref, dst_ref=cg_ref.at[4 * px + 2 * py + pc], send_sem=cs.at[j - 1],
                recv_sem=cr.at[j - 1], device_id=(px, py, pc), device_id_type=MESH).wait_recv()
        local.wait()
        pick = (lax.broadcasted_iota(jnp.int32, (8, 64), 1) == 8 * lax.broadcasted_iota(jnp.int32, (8, 64), 0))
        cv = jnp.dot(pick.astype(F32), cg_ref[...].reshape(64, D), preferred_element_type=F32,
                     precision=lax.Precision.HIGHEST)
        sc = cv * _sigmoid(cv)
        sc_ref[...] = sc
        mp_ref[me] = jnp.dot(sc, wada_ref[...], preferred_element_type=F32,
                             precision=lax.Precision.HIGHEST) + b_ref[...]
        for j, (px, py) in enumerate(chips):
            cp = pltpu.make_async_remote_copy(
                src_ref=mp_ref.at[me], dst_ref=mp_ref.at[me], send_sem=ms.at[j], recv_sem=mr.at[j],
                device_id=(px, py, mc), device_id_type=MESH)
            cp.start()
            sends.append(cp)

        landed(0, 0, xn, pass_on=(4, yn))
        landed(3, 3, yn, pass_on=(5, xn))
        landed(1, 1, xn)
        landed(2, 2, yn)
        for j, (px, py) in enumerate(chips):
            pltpu.make_async_remote_copy(
                src_ref=mp_ref.at[me], dst_ref=mp_ref.at[2 * px + py], send_sem=ms.at[j], recv_sem=mr.at[j],
                device_id=(px, py, mc), device_id_type=MESH).wait_recv()
        landed(4, 4, yn)
        landed(5, 5, xn)
        for t in range(n):
            for j, (px, py) in enumerate(chips):
                src = 2 * px + py
                pltpu.make_async_remote_copy(
                    src_ref=ins[t].at[mc], dst_ref=outs[t].at[src, mc], send_sem=ssem.at[t, j],
                    recv_sem=rsem.at[t, j], device_id=(px, py, mc), device_id_type=MESH).wait_recv()
                fwd = pltpu.make_async_remote_copy(
                    src_ref=outs[t].at[src, mc], dst_ref=outs[t].at[src, mc], send_sem=ssem.at[t, 3 + j],
                    recv_sem=rsem.at[t, 3 + j], device_id=sib, device_id_type=MESH)
                fwd.start()
                sends.append(fwd)
        for t in range(n):
            for j, (px, py) in enumerate(chips):
                src = 2 * px + py
                pltpu.make_async_remote_copy(
                    src_ref=outs[t].at[src, 1 - mc], dst_ref=outs[t].at[src, 1 - mc], send_sem=ssem.at[t, 3 + j],
                    recv_sem=rsem.at[t, 3 + j], device_id=sib, device_id_type=MESH).wait_recv()
        for i, (src_chip, q) in enumerate(pieces):
            big(src_chip, q, 6 + i, sib, half=1 - mc).wait_recv()
        for cp in sends:
            cp.wait_send()

    vm = pl.BlockSpec(memory_space=pltpu.VMEM)
    outs = pl.pallas_call(
        body, name="weights_allgather",
        out_shape=[jax.ShapeDtypeStruct((4,) + big_shard.shape, big_shard.dtype)]
                  + [jax.ShapeDtypeStruct((4,) + s.shape, s.dtype) for s in shards]
                  + [jax.ShapeDtypeStruct((8,) + c_blk.shape, F32), jax.ShapeDtypeStruct((4, 8, cols), F32),
                     jax.ShapeDtypeStruct((8, D), F32)],
        in_specs=[ANY] * (n + 1) + [vm] * 3, out_specs=[ANY] * (n + 1) + [vm] * 3,
        scratch_shapes=([pltpu.SemaphoreType.DMA((n, 6)), pltpu.SemaphoreType.DMA((n, 6))] if n else [])
                       + [pltpu.SemaphoreType.DMA((12,)), pltpu.SemaphoreType.DMA((12,)),
                          pltpu.SemaphoreType.DMA((7,)), pltpu.SemaphoreType.DMA((7,)),
                          pltpu.SemaphoreType.DMA((3,)), pltpu.SemaphoreType.DMA((3,)), pltpu.SemaphoreType.DMA(())],
        compiler_params=_params(),
    )(big_shard, *shards, c_blk, w_ada, b_shard)
    return outs[0], outs[1:1 + n], outs[1 + n], outs[2 + n], outs[3 + n]


def _shard_gather_copies(ins, outs, ssem, rsem):
    mx, my, mc = _coords()
    me = 2 * mx + my
    sib = (mx, my, 1 - mc)
    send, recv, fwd, fwd_recv = [], [], [], []
    for t in range(len(ins)):
        for j in range(3):
            px, py = _flip(mx, (j + 1) & 2), _flip(my, (j + 1) & 1)
            src = 2 * px + py
            far = dict(send_sem=ssem.at[t, j], recv_sem=rsem.at[t, j], device_id=(px, py, mc), device_id_type=MESH)
            near = dict(send_sem=ssem.at[t, 3 + j], recv_sem=rsem.at[t, 3 + j], device_id=sib, device_id_type=MESH)
            send.append(pltpu.make_async_remote_copy(src_ref=ins[t].at[mc], dst_ref=outs[t].at[me, mc], **far))
            recv.append(pltpu.make_async_remote_copy(src_ref=ins[t].at[mc], dst_ref=outs[t].at[src, mc], **far))
            fwd.append(pltpu.make_async_remote_copy(src_ref=outs[t].at[src, mc], dst_ref=outs[t].at[src, mc], **near))
            fwd_recv.append(pltpu.make_async_remote_copy(
                src_ref=outs[t].at[src, 1 - mc], dst_ref=outs[t].at[src, 1 - mc], **near))
    return send, recv, fwd, fwd_recv


def _pair_exchange(grads):
    n = len(grads)

    def body(*refs):
        ins, outs = refs[:n], refs[n:2 * n]
        ssem, rsem = refs[2 * n:]
        mx, my, mc = _coords()
        sib = (mx, my, 1 - mc)
        sends = []
        for t in range(n):
            cp = pltpu.make_async_remote_copy(
                src_ref=ins[t].at[1 - mc], dst_ref=outs[t], send_sem=ssem.at[t], recv_sem=rsem.at[t],
                device_id=sib, device_id_type=MESH)
            cp.start()
            sends.append(cp)
        for cp in sends:
            cp.wait_recv()
        for cp in sends:
            cp.wait_send()

    return pl.pallas_call(
        body, name="grad_pair_exchange",
        out_shape=[jax.ShapeDtypeStruct(g.shape[1:], g.dtype) for g in grads],
        in_specs=[ANY] * n, out_specs=[ANY] * n,
        scratch_shapes=[pltpu.SemaphoreType.DMA((n,)), pltpu.SemaphoreType.DMA((n,))],
    )(*grads)


def _scatter_copies(ins, outs, ssem, rsem):
    mx, my, mc = _coords()
    me = 2 * mx + my
    mine, theirs = [], []
    for t in range(len(ins)):
        for j in range(1, 4):
            px, py = _flip(mx, j & 2), _flip(my, j & 1)
            mine.append(pltpu.make_async_remote_copy(
                src_ref=ins[t].at[2 * px + py], dst_ref=outs[t].at[me], send_sem=ssem.at[t, j - 1],
                recv_sem=rsem.at[t, j - 1], device_id=(px, py, mc), device_id_type=MESH))
            theirs.append(pltpu.make_async_remote_copy(
                src_ref=ins[t].at[me], dst_ref=outs[t].at[2 * px + py], send_sem=ssem.at[t, j - 1],
                recv_sem=rsem.at[t, j - 1], device_id=(px, py, mc), device_id_type=MESH))
    return mine, theirs


def _half_exchange(halves, smalls):
    n = len(halves)

    def body(*refs):
        ins, sm_ref = refs[:n], refs[n]
        outs, all_ref = refs[n + 1:2 * n + 1], refs[2 * n + 1]
        ssem, rsem, gsem, hsem, lsem = refs[2 * n + 2:]
        mx, my, mc = _coords()
        me = 4 * mx + 2 * my + mc
        sib = (mx, my, 1 - mc)
        local = pltpu.make_async_copy(sm_ref, all_ref.at[me], lsem)
        local.start()
        sends = []
        for t in range(n):
            rc = pltpu.make_async_remote_copy(
                src_ref=ins[t], dst_ref=outs[t], send_sem=ssem.at[t], recv_sem=rsem.at[t],
                device_id=sib, device_id_type=MESH)
            rc.start()
            sends.append(rc)
        gathers = []
        for j in range(1, 8):
            peer = (_flip(mx, j & 4), _flip(my, j & 2), _flip(mc, j & 1))
            cp = pltpu.make_async_remote_copy(
                src_ref=sm_ref, dst_ref=all_ref.at[me], send_sem=gsem.at[j - 1], recv_sem=hsem.at[j - 1],
                device_id=peer, device_id_type=MESH)
            cp.start()
            gathers.append(cp)
        for cp in sends:
            cp.wait_recv()
        for j in range(1, 8):
            px, py, pc = _flip(mx, j & 4), _flip(my, j & 2), _flip(mc, j & 1)
            pltpu.make_async_remote_copy(
                src_ref=sm_ref, dst_ref=all_ref.at[4 * px + 2 * py + pc], send_sem=gsem.at[j - 1],
                recv_sem=hsem.at[j - 1], device_id=(px, py, pc), device_id_type=MESH).wait_recv()
        for cp in sends + gathers:
            cp.wait_send()
        local.wait()

    vm = pl.BlockSpec(memory_space=pltpu.VMEM)
    outs = pl.pallas_call(
        body, name="grad_half_exchange",
        out_shape=[jax.ShapeDtypeStruct(h.shape, h.dtype) for h in halves]
                  + [jax.ShapeDtypeStruct((8,) + smalls.shape, smalls.dtype)],
        in_specs=[ANY] * n + [vm], out_specs=[ANY] * n + [vm],
        scratch_shapes=[pltpu.SemaphoreType.DMA((n,)), pltpu.SemaphoreType.DMA((n,)),
                        pltpu.SemaphoreType.DMA((7,)), pltpu.SemaphoreType.DMA((7,)), pltpu.SemaphoreType.DMA(())],
    )(*halves, smalls)
    return outs[:n], outs[n]


def _row_tile(rows, cols, bytes_per_row_elem=4, budget=2 * 1024 * 1024):
    t = rows
    while t % 2 == 0 and t > 16 and t * cols * bytes_per_row_elem > budget:
        t //= 2
    return t


def _pair_add(core, g, r):
    _, _, rh, cc = g.shape
    tr = _row_tile(rh, cc)

    def body(core_ref, g_ref, r_ref, o_ref):
        o_ref[...] = (g_ref[...].astype(F32) + r_ref[...].astype(F32)).astype(o_ref.dtype)

    return pl.pallas_call(
        body, name="grad_pair_add",
        grid_spec=pltpu.PrefetchScalarGridSpec(
            num_scalar_prefetch=1, grid=(4, rh // tr),
            in_specs=[pl.BlockSpec((None, None, tr, cc), lambda k, i, c: (c[0], k, i, 0)),
                      pl.BlockSpec((None, tr, cc), lambda k, i, c: (k, i, 0))],
            out_specs=pl.BlockSpec((None, tr, cc), lambda k, i, c: (k, i, 0))),
        out_shape=jax.ShapeDtypeStruct(r.shape, BF16),
        compiler_params=_params(),
    )(core, g, r)


def _sum4(chip, p, q):
    _, rh, cc = q.shape
    tr = _row_tile(rh, cc)

    def body(chip_ref, p_ref, q1_ref, q2_ref, q3_ref, o_ref):
        o_ref[...] = ((p_ref[...].astype(F32) + q1_ref[...].astype(F32)) + q2_ref[...].astype(F32)) + q3_ref[...].astype(F32)

    def other(j):
        return pl.BlockSpec((None, tr, cc), lambda i, c: (jnp.bitwise_xor(c[0], j), i, 0))

    return pl.pallas_call(
        body, name="grad_sum4",
        grid_spec=pltpu.PrefetchScalarGridSpec(
            num_scalar_prefetch=1, grid=(rh // tr,),
            in_specs=[pl.BlockSpec((None, tr, cc), lambda i, c: (c[0], i, 0)), other(1), other(2), other(3)],
            out_specs=pl.BlockSpec((tr, cc), lambda i, c: (i, 0))),
        out_shape=jax.ShapeDtypeStruct((rh, cc), F32),
        compiler_params=_params(),
    )(chip, p, q, q, q)


def _adamw_math(w, g, m, v):
    m = ADAM_B1 * m + (1.0 - ADAM_B1) * g
    v = ADAM_B2 * v + (1.0 - ADAM_B2) * (g * g)
    m_hat = m / (1.0 - ADAM_B1 ** ADAM_STEP)
    v_hat = v / (1.0 - ADAM_B2 ** ADAM_STEP)
    delta = -ADAM_LR * (m_hat / (jnp.sqrt(v_hat) + ADAM_EPS) + ADAM_WD * w)
    return delta, m, v


def _adamw_halves(core, w, mine, recv, m, v, name):
    rows, cols = w.shape
    rh = rows // 2
    tr = _row_tile(rh, cols, budget=1024 * 1024)
    nh = rh // tr

    def body(core_ref, w_ref, a_ref, b_ref, m_ref, v_ref, g_ref, d_ref, mo_ref, vo_ref):
        g = jnp.where(pl.program_id(0) == core_ref[0], a_ref[...], b_ref[...])
        d, mn, vn = _adamw_math(w_ref[...], g, m_ref[...], v_ref[...])
        g_ref[...] = g
        d_ref[...] = d
        mo_ref[...] = mn
        vo_ref[...] = vn

    full = pl.BlockSpec((tr, cols), lambda h, i, c: (h * nh + i, 0))
    mine_spec = pl.BlockSpec((tr, cols), lambda h, i, c: (jnp.where(h == c[0], i, 0), 0))
    recv_spec = pl.BlockSpec((tr, cols), lambda h, i, c: (jnp.where(h == c[0], 0, i), 0))
    sd = jax.ShapeDtypeStruct((rows, cols), F32)
    return pl.pallas_call(
        body, name=name,
        grid_spec=pltpu.PrefetchScalarGridSpec(
            num_scalar_prefetch=1, grid=(2, nh),
            in_specs=[full, mine_spec, recv_spec, full, full], out_specs=[full] * 4),
        out_shape=[sd, sd, sd, sd], compiler_params=_params(),
    )(core, w, mine, recv, m, v)


def _inproj_nat(x, mod, norm_w, w, *, tm=1024, tn=1664):
    S, D = x.shape
    N = w.shape[1]
    nt, nj = S // tm, N // tn
    chunk = 256

    def body(x_ref, mod_ref, nw_ref, w_ref, proj_ref, h_ref, ht_ref, h_a, ht_a, h_b, ht_b):
        i, j = pl.program_id(0), pl.program_id(1)

        def make_h(h_s, ht_s):
            shift = mod_ref[:, 0:D]
            scale1 = 1.0 + mod_ref[:, D:2 * D]
            for r in range(tm // chunk):
                xv = x_ref[pl.ds(r * chunk, chunk), :]
                ms = jnp.mean(xv * xv, axis=-1, keepdims=True)
                h = (xv * lax.rsqrt(ms + EPS) * nw_ref[...]) * scale1 + shift
                h_s[pl.ds(r * chunk, chunk), :] = h.astype(BF16)
                ht_s[:, pl.ds(r * chunk, chunk)] = h.T.astype(BF16)

        @pl.when((i == 0) & (j == 0))
        def _():
            make_h(h_a, ht_a)

        def step(cur, nxt):
            @pl.when(j == 0)
            def _():
                h_ref[...] = cur[0][...]
                ht_ref[...] = cur[1][...]

            def project():
                proj_ref[...] = jnp.dot(cur[0][...], w_ref[...], preferred_element_type=F32).astype(BF16)

            ahead = (j == nj - 1) & (i < nt - 1)

            @pl.when(ahead)
            def _():
                project()
                make_h(*nxt)

            @pl.when(jnp.logical_not(ahead))
            def _():
                project()

        @pl.when(i % 2 == 0)
        def _():
            step((h_a, ht_a), (h_b, ht_b))

        @pl.when(i % 2 == 1)
        def _():
            step((h_b, ht_b), (h_a, ht_a))

    def x_tile(i, j):
        return (jnp.where((i == 0) & (j == 0), 0, jnp.minimum(i + 1, nt - 1)), 0)

    return pl.pallas_call(
        body, name="inproj_nat", grid=(nt, nj),
        in_specs=[pl.BlockSpec((tm, D), x_tile),
                  pl.BlockSpec((1, 3 * D), lambda i, j: (0, 0)),
                  pl.BlockSpec((1, D), lambda i, j: (0, 0)),
                  pl.BlockSpec((D, tn), lambda i, j: (0, j))],
        out_specs=[pl.BlockSpec((tm, tn), lambda i, j: (i, j)),
                   pl.BlockSpec((tm, D), lambda i, j: (i, 0)),
                   pl.BlockSpec((D, tm), lambda i, j: (0, i))],
        out_shape=[jax.ShapeDtypeStruct((S, N), BF16), jax.ShapeDtypeStruct((S, D), BF16),
                   jax.ShapeDtypeStruct((D, S), BF16)],
        scratch_shapes=[pltpu.VMEM((tm, D), BF16), pltpu.VMEM((D, tm), BF16),
                        pltpu.VMEM((tm, D), BF16), pltpu.VMEM((D, tm), BF16)],
        compiler_params=_params(),
    )(x, mod, norm_w, w)


def _inproj_nat_a(x, mod, norm_w, w, later_piece, *, tm=1024, tn=1664):
    S, D = x.shape
    N = w.shape[1]
    nt, nj = S // tm, N // tn
    chunk = 256

    def body(x_ref, mod_ref, nw_ref, w0_ref, p_hbm, part_ref, h_ref, ht_ref, g_hbm, ssem, rsem):
        i, j = pl.program_id(0), pl.program_id(1)

        @pl.when((i == 0) & (j == 0))
        def _():
            for cp in _shard_gather_copies([p_hbm], [g_hbm], ssem, rsem)[0]:
                cp.start()

        @pl.when(j == 0)
        def _():
            shift = mod_ref[:, 0:D]
            scale1 = 1.0 + mod_ref[:, D:2 * D]
            for r in range(tm // chunk):
                xv = x_ref[pl.ds(r * chunk, chunk), :]
                ms = jnp.mean(xv * xv, axis=-1, keepdims=True)
                h = (xv * lax.rsqrt(ms + EPS) * nw_ref[...]) * scale1 + shift
                h_ref[pl.ds(r * chunk, chunk), :] = h.astype(BF16)
                ht_ref[:, pl.ds(r * chunk, chunk)] = h.T.astype(BF16)

        part_ref[...] = jnp.dot(h_ref[:, 0:D // 2], w0_ref[...], preferred_element_type=F32).astype(BF16)

        @pl.when((i == nt - 1) & (j == 0))
        def _():
            _, recv, fwd, _ = _shard_gather_copies([p_hbm], [g_hbm], ssem, rsem)
            for r, f in zip(recv, fwd):
                r.wait_recv()
                f.start()

        @pl.when((i == nt - 1) & (j == nj - 1))
        def _():
            send, _, fwd, fwd_recv = _shard_gather_copies([p_hbm], [g_hbm], ssem, rsem)
            for r in fwd_recv:
                r.wait_recv()
            for cp in send + fwd:
                cp.wait_send()

    return pl.pallas_call(
        body, name="inproj_nat_a", grid=(nt, nj),
        in_specs=[pl.BlockSpec((tm, D), lambda i, j: (i, 0)),
                  pl.BlockSpec((1, 3 * D), lambda i, j: (0, 0)),
                  pl.BlockSpec((1, D), lambda i, j: (0, 0)),
                  pl.BlockSpec((D // 2, tn), lambda i, j: (0, j)), ANY],
        out_specs=[pl.BlockSpec((tm, tn), lambda i, j: (i, j)),
                   pl.BlockSpec((tm, D), lambda i, j: (i, 0)),
                   pl.BlockSpec((D, tm), lambda i, j: (0, i)), ANY],
        out_shape=[jax.ShapeDtypeStruct((S, N), BF16), jax.ShapeDtypeStruct((S, D), BF16),
                   jax.ShapeDtypeStruct((D, S), BF16),
                   jax.ShapeDtypeStruct((4,) + later_piece.shape, later_piece.dtype)],
        scratch_shapes=[pltpu.SemaphoreType.DMA((1, 6)), pltpu.SemaphoreType.DMA((1, 6))],
        compiler_params=_params(),
    )(x, mod, norm_w, w, later_piece)


def _inproj_nat_b(h, partial, w, *, tm=1024, tn=1664):
    S, D = h.shape
    N = w.shape[1]

    def body(h_ref, p_ref, w1_ref, o_ref):
        o_ref[...] = (p_ref[...].astype(F32)
                      + jnp.dot(h_ref[...], w1_ref[...], preferred_element_type=F32)).astype(BF16)

    return pl.pallas_call(
        body, name="inproj_nat_b", grid=(S // tm, N // tn),
        in_specs=[pl.BlockSpec((tm, D // 2), lambda i, j: (i, 1)),
                  pl.BlockSpec((tm, tn), lambda i, j: (i, j)),
                  pl.BlockSpec((D // 2, tn), lambda i, j: (1, j))],
        out_specs=pl.BlockSpec((tm, tn), lambda i, j: (i, j)),
        out_shape=jax.ShapeDtypeStruct((S, N), BF16),
        compiler_params=_params(),
    )(h, partial, w)


def _inproj_qkv(a, b, shards, *, tm=1024, tn=1536):
    M, K = a.shape
    N = b.shape[1]
    n = len(shards)
    ni, nj = M // tm, N // tn

    def body(a_ref, b_ref, *rest):
        ins, o_ref, outs, sems = rest[:n], rest[n], rest[n + 1:2 * n + 1], rest[2 * n + 1:]
        i, j = pl.program_id(0), pl.program_id(1)
        if n:
            @pl.when((i == 0) & (j == 0))
            def _():
                for cp in _shard_gather_copies(ins, outs, *sems)[0]:
                    cp.start()

        o_ref[...] = jnp.dot(a_ref[...], b_ref[...], preferred_element_type=F32).astype(o_ref.dtype)

        if n:
            @pl.when((i == ni - 1) & (j == 0))
            def _():
                _, recv, fwd, _ = _shard_gather_copies(ins, outs, *sems)
                for r, f in zip(recv, fwd):
                    r.wait_recv()
                    f.start()

            @pl.when((i == ni - 1) & (j == nj - 1))
            def _():
                send, _, fwd, fwd_recv = _shard_gather_copies(ins, outs, *sems)
                for r in fwd_recv:
                    r.wait_recv()
                for cp in send + fwd:
                    cp.wait_send()

    outs = pl.pallas_call(
        body, name="inproj_qkv", grid=(ni, nj),
        in_specs=[pl.BlockSpec((tm, K), lambda i, j: (i, 0)), pl.BlockSpec((K, tn), lambda i, j: (0, j))] + [ANY] * n,
        out_specs=[pl.BlockSpec((tm, tn), lambda i, j: (i, j))] + [ANY] * n,
        out_shape=[jax.ShapeDtypeStruct((M, N), BF16)] + [jax.ShapeDtypeStruct((4,) + s.shape, s.dtype) for s in shards],
        scratch_shapes=[pltpu.SemaphoreType.DMA((n, 6)), pltpu.SemaphoreType.DMA((n, 6))] if n else [],
        compiler_params=_params(),
    )(a, b, *shards)
    return outs[0], outs[1:]


def _matmul_lhs_resident(a, b, *, tn, name, exchange=None):
    M, K = a.shape
    N = b.shape[1]
    nj = N // tn
    hosts = exchange is not None

    def body(a_hbm, b_ref, *rest):
        if hosts:
            x_hbm, o_ref, r_hbm, a_s, ssem, rsem = rest
        else:
            o_ref, a_s = rest
        j = pl.program_id(0)

        def to_sibling():
            mx, my, mc = _coords()
            return pltpu.make_async_remote_copy(src_ref=x_hbm.at[1 - mc], dst_ref=r_hbm, send_sem=ssem, recv_sem=rsem,
                                                device_id=(mx, my, 1 - mc), device_id_type=MESH)

        @pl.when(j == 0)
        def _():
            if hosts:
                to_sibling().start()
            pltpu.sync_copy(a_hbm, a_s)

        o_ref[...] = jnp.dot(a_s[...], b_ref[...], preferred_element_type=F32).astype(o_ref.dtype)

        if hosts:
            @pl.when(j == nj - 1)
            def _():
                cp = to_sibling()
                cp.wait_recv()
                cp.wait_send()

    prod = jax.ShapeDtypeStruct((M, N), BF16)
    out = pl.pallas_call(
        body, name=name, grid=(nj,),
        in_specs=[ANY, pl.BlockSpec((K, tn), lambda j: (0, j))] + ([ANY] if hosts else []),
        out_specs=[pl.BlockSpec((M, tn), lambda j: (0, j)), ANY] if hosts else pl.BlockSpec((M, tn), lambda j: (0, j)),
        out_shape=[prod, jax.ShapeDtypeStruct(exchange.shape[1:], exchange.dtype)] if hosts else prod,
        scratch_shapes=[pltpu.VMEM((M, K), BF16)]
                       + ([pltpu.SemaphoreType.DMA(()), pltpu.SemaphoreType.DMA(())] if hosts else []),
        compiler_params=_params(),
    )(a, b, *([exchange] if hosts else []))
    return out


SHARD_COLS = 2816


def _column_tables(part):
    if part == "nat":
        bw = 256
        orig = [j * bw if j < 16 else 8704 + (j - 16) * bw for j in range(NAT // bw)]
    else:
        bw = 128
        orig = []
        for jj in range(NQKV // bw):
            g, p, t = jj // 12, (jj % 12) // 3, jj % 3
            orig.append(4096 + t * 1536 + g * 512 + p * 128)
    tk = jnp.array([o // SHARD_COLS for o in orig], jnp.int32)
    tc = jnp.array([(o % SHARD_COLS) // bw for o in orig], jnp.int32)
    return tk, tc, bw


def _weights_prep(gath, own, chipv, part, piece, into):
    D = D_MODEL
    tk, tc, bw = _column_tables(part)
    n = tk.shape[0]
    per_step = 2 if part == "nat" else 4
    assert n % per_step == 0

    def body(tk_ref, tc_ref, chip_ref, *refs):
        w_ref = refs[-1]
        for u in range(per_step):
            g_ref, o_ref = refs[2 * u], refs[2 * u + 1]
            mine = tk_ref[pl.program_id(0) * per_step + u] == chip_ref[0]
            for h in range(2):
                w_ref[h * 256:(h + 1) * 256, u * bw:(u + 1) * bw] = jnp.where(mine, o_ref[h], g_ref[h])

    def gath_spec(u):
        def idx(j, tk, tc, ch):
            k = tk[j * per_step + u]
            return (jnp.where(k == ch[0], (k + 1) % 4, k), 0, 0, tc[j * per_step + u])
        return pl.BlockSpec((None, 2, 256, bw), idx)

    def own_spec(u):
        return pl.BlockSpec((2, 256, bw), lambda j, tk, tc, ch: (0, 0, tc[j * per_step + u]))

    specs = []
    for u in range(per_step):
        specs += [gath_spec(u), own_spec(u)]
    return pl.pallas_call(
        body, name=f"weights_prep_{part}_{piece}",
        grid_spec=pltpu.PrefetchScalarGridSpec(
            num_scalar_prefetch=3, grid=(n // per_step,),
            in_specs=specs + [ANY],
            out_specs=pl.BlockSpec((512, per_step * bw), lambda j, tk, tc, ch: (piece, j))),
        out_shape=jax.ShapeDtypeStruct((D, n * bw), BF16),
        input_output_aliases={3 + 2 * per_step: 0},
        compiler_params=_params(),
    )(tk, tc, chipv, *([gath, own] * per_step), into)


def _pair_add_to_shards(core, gw, r, into, part):
    D = D_MODEL
    tk, tc, bw = _column_tables(part)
    n = tk.shape[0]

    def body(tk_ref, tc_ref, core_ref, g_ref, r_ref, into_ref, o_ref):
        o_ref[...] = (g_ref[...].astype(F32) + r_ref[...].astype(F32)).astype(BF16)

    return pl.pallas_call(
        body, name="grad_pair_add_" + part,
        grid_spec=pltpu.PrefetchScalarGridSpec(
            num_scalar_prefetch=3, grid=(n,),
            in_specs=[pl.BlockSpec((None, D // 2, bw), lambda j, tk, tc, c: (c[0], 0, j)),
                      pl.BlockSpec((D // 2, bw), lambda j, tk, tc, c: (0, j)), ANY],
            out_specs=pl.BlockSpec((None, D // 2, bw), lambda j, tk, tc, c: (tk[j], 0, tc[j]))),
        out_shape=jax.ShapeDtypeStruct(into.shape, BF16),
        input_output_aliases={5: 0},
        compiler_params=_params(),
    )(tk, tc, core, gw, r, into)


def _head_norm(xb, w, ebd):
    x = xb.astype(F32)
    ss = jnp.dot((x * x).astype(BF16), ebd, preferred_element_type=F32)
    return x * lax.rsqrt(ss * (1.0 / HEAD_DIM) + EPS) * w


def _window_mask(no_prev):
    qi = lax.broadcasted_iota(jnp.int32, (128, 256), 0)
    kc = lax.broadcasted_iota(jnp.int32, (128, 256), 1)
    ok = (kc >= qi) & (kc <= qi + 128)
    if no_prev is not None:
        ok = ok & ((kc >= 128) | jnp.logical_not(no_prev))
    return ok


def _lane_masks():
    lane = lax.broadcasted_iota(jnp.int32, (1, 128), 1)
    return [(lane < HEAD_DIM).astype(F32), (lane >= HEAD_DIM).astype(F32)]


def _head_col(v, mh):
    return jnp.sum(v * mh, axis=1, keepdims=True) * (1.0 / HEAD_DIM)


def _attn_fwd(projq, qnw2, knw2, ebd, g):
    S = projq.shape[0]
    d = DILATIONS[g]
    T = ATT_TILE
    nt = S // T
    nb = T // (128 * d)
    sdt = BF16 if d == 1 else F32

    def body(cur_ref, qw_ref, kw_ref, ebd_ref, o_ref, lse_ref, qs_s, ks_s, vs_s):
        t = pl.program_id(1)
        e = ebd_ref[...]

        @pl.when(t == 0)
        def _():
            ks_s[pl.ds(0, T), :] = jnp.zeros((T, 128), sdt)
            vs_s[pl.ds(0, T), :] = jnp.zeros((T, 128), sdt)

        @pl.when(t > 0)
        def _():
            ks_s[pl.ds(0, T), :] = ks_s[pl.ds(T, T), :]
            vs_s[pl.ds(0, T), :] = vs_s[pl.ds(T, T), :]

        qs_s[...] = (_head_norm(cur_ref[:, 0:128], qw_ref[...], e) * (QK_SCALE * LOG2E)).astype(sdt)
        ks_s[pl.ds(T, T), :] = _head_norm(cur_ref[:, 128:256], kw_ref[...], e).astype(sdt)
        vs_s[pl.ds(T, T), :] = cur_ref[:, 256:384].astype(sdt)
        m0, m1 = _lane_masks()
        first = t == 0
        for r in range(d):
            for bb in range(nb):
                q0 = r + bb * 128 * d
                k0 = T + r + (bb - 1) * 128 * d
                qs = qs_s[pl.ds(q0, 128, stride=d), :].astype(F32)
                kb = ks_s[pl.ds(k0, 256, stride=d), :].astype(BF16)
                vb = vs_s[pl.ds(k0, 256, stride=d), :].astype(BF16)
                ok = _window_mask(first if bb == 0 else None)
                ok2 = jnp.concatenate([ok, ok], axis=0)
                q2 = jnp.concatenate([qs * m0, qs * m1], axis=0).astype(BF16)
                s = lax.dot_general(q2, kb, (((1,), (1,)), ((), ())), preferred_element_type=F32)
                s = jnp.where(ok2, s, NEG_INF)
                mx = jnp.max(s, axis=-1, keepdims=True)
                p = jnp.exp2(s - mx)
                l = jnp.sum(p, axis=-1, keepdims=True)
                o = jnp.dot(p.astype(BF16), vb, preferred_element_type=F32) / l
                lse = mx * LN2 + jnp.log(l)
                o_ref[pl.ds(q0, 128, stride=d), :] = jnp.where(m0 > 0, o[0:128], o[128:256])
                lse_ref[pl.ds(q0, 128, stride=d), :] = jnp.where(m0 > 0, lse[0:128], lse[128:256])

    return pl.pallas_call(
        body, name=f"attn_fwd_g{g}", grid=(4, nt),
        in_specs=[pl.BlockSpec((T, 384), lambda p, t: (t, g * 4 + p)),
                  pl.BlockSpec((1, 128), lambda p, t: (0, 0)),
                  pl.BlockSpec((1, 128), lambda p, t: (0, 0)),
                  pl.BlockSpec((128, 128), lambda p, t: (0, 0))],
        out_specs=[pl.BlockSpec((T, 128), lambda p, t: (t, p)), pl.BlockSpec((T, 128), lambda p, t: (t, p))],
        out_shape=[jax.ShapeDtypeStruct((S, ATTN_OUT), F32), jax.ShapeDtypeStruct((S, ATTN_OUT), F32)],
        scratch_shapes=[pltpu.VMEM((T, 128), sdt), pltpu.VMEM((2 * T, 128), sdt), pltpu.VMEM((2 * T, 128), sdt)],
        compiler_params=_params(),
    )(projq, qnw2, knw2, ebd)


def _attn_bwd(projq, dprojq, d_o, lse, delta, qnw2, knw2, ebd, g):
    S = projq.shape[0]
    d = DILATIONS[g]
    T = ATT_TILE
    nt = S // T
    nb = T // (128 * d)
    sdt = F32

    def norm_bwd(raw_b, dy, w, e):
        x = raw_b.astype(F32)
        ss = jnp.dot((x * x).astype(BF16), e, preferred_element_type=F32)
        rstd = lax.rsqrt(ss * (1.0 / HEAD_DIM) + EPS)
        xh = x * rstd
        gw = jnp.sum(dy * xh, axis=0, keepdims=True)
        dxh = dy * w
        mean = jnp.dot((dxh * xh).astype(BF16), e, preferred_element_type=F32) * (1.0 / HEAD_DIM)
        return rstd * (dxh - xh * mean), gw

    nt_dims = (((1,), (1,)), ((), ()))
    tn_dims = (((0,), (0,)), ((), ()))

    def unit_stacked(qs, dos, ls, dls, kb, vb, ok, m0, m1):
        ok2 = jnp.concatenate([ok, ok], axis=0)
        q2 = jnp.concatenate([qs * m0, qs * m1], axis=0).astype(BF16)
        do2 = jnp.concatenate([dos * m0, dos * m1], axis=0).astype(BF16)
        lcol = jnp.concatenate([_head_col(ls, m0), _head_col(ls, m1)], axis=0)
        dcol = jnp.concatenate([_head_col(dls, m0), _head_col(dls, m1)], axis=0)
        s = jnp.where(ok2, lax.dot_general(q2, kb, nt_dims, preferred_element_type=F32), NEG_INF)
        p = jnp.exp2(s - lcol * LOG2E)
        dp = lax.dot_general(do2, vb, nt_dims, preferred_element_type=F32)
        ds = (p * (dp - dcol)).astype(BF16)
        dq2 = jnp.dot(ds, kb, preferred_element_type=F32)
        dk = lax.dot_general(ds, q2, tn_dims, preferred_element_type=F32)
        dv = lax.dot_general(p.astype(BF16), do2, tn_dims, preferred_element_type=F32)
        return dq2[0:128] * m0 + dq2[128:256] * m1, dk, dv

    def unit_per_head(qs, dos, ls, dls, kb, vb, ok, m0, m1):
        dq_t = dk_t = dv_t = None
        for mh in (m0, m1):
            qh = (qs * mh).astype(BF16)
            doh = (dos * mh).astype(BF16)
            s = jnp.where(ok, lax.dot_general(qh, kb, nt_dims, preferred_element_type=F32), NEG_INF)
            p = jnp.exp2(s - _head_col(ls, mh) * LOG2E)
            dp = lax.dot_general(doh, vb, nt_dims, preferred_element_type=F32)
            ds = (p * (dp - _head_col(dls, mh))).astype(BF16)
            dq = jnp.dot(ds, kb, preferred_element_type=F32) * mh
            dk = lax.dot_general(ds, qh, tn_dims, preferred_element_type=F32)
            dv = lax.dot_general(p.astype(BF16), doh, tn_dims, preferred_element_type=F32)
            dq_t = dq if dq_t is None else dq_t + dq
            dk_t = dk if dk_t is None else dk_t + dk
            dv_t = dv if dv_t is None else dv_t + dv
        return dq_t, dk_t, dv_t

    unit = unit_per_head if d == 1 else unit_stacked

    def body(cur_ref, prev_ref, do_ref, lse_ref, dl_ref, qw_ref, kw_ref, ebd_ref, dp_in_ref,
             out_ref, gqk_ref, qs_s, ks_s, vs_s, do_s, dq_cur, dq_prev, dk_acc, dv_acc):
        i = pl.program_id(1)
        e = ebd_ref[...]
        m0, m1 = _lane_masks()

        @pl.when(i == 0)
        def _():
            dk_acc[...] = jnp.zeros_like(dk_acc)
            dv_acc[...] = jnp.zeros_like(dv_acc)
            dq_prev[...] = jnp.zeros_like(dq_prev)
            gqk_ref[...] = jnp.zeros_like(gqk_ref)

        @pl.when(i < nt)
        def _():
            qs_s[...] = _head_norm(cur_ref[:, 0:128], qw_ref[...], e) * (QK_SCALE * LOG2E)
            ks_s[pl.ds(0, T), :] = _head_norm(prev_ref[:, 128:256], kw_ref[...], e)
            ks_s[pl.ds(T, T), :] = _head_norm(cur_ref[:, 128:256], kw_ref[...], e)
            vs_s[pl.ds(0, T), :] = prev_ref[:, 256:384].astype(F32)
            vs_s[pl.ds(T, T), :] = cur_ref[:, 256:384].astype(F32)
            do_s[...] = do_ref[...].astype(F32)
            first = i == 0
            for r in range(d):
                own_k = own_v = None
                for bb in range(nb):
                    q0 = r + bb * 128 * d
                    k0 = T + r + (bb - 1) * 128 * d
                    qs = qs_s[pl.ds(q0, 128, stride=d), :]
                    dos = do_s[pl.ds(q0, 128, stride=d), :]
                    ls = lse_ref[pl.ds(q0, 128, stride=d), :]
                    dls = dl_ref[pl.ds(q0, 128, stride=d), :]
                    kb = ks_s[pl.ds(k0, 256, stride=d), :].astype(BF16)
                    vb = vs_s[pl.ds(k0, 256, stride=d), :].astype(BF16)
                    ok = _window_mask(first if bb == 0 else None)
                    dq_t, dk_t, dv_t = unit(qs, dos, ls, dls, kb, vb, ok, m0, m1)
                    dq_cur[pl.ds(q0, 128, stride=d), :] = dq_t
                    before = pl.ds(k0, 128, stride=d)
                    if bb == 0:
                        dk_acc[before, :] = dk_acc[before, :] + dk_t[0:128]
                        dv_acc[before, :] = dv_acc[before, :] + dv_t[0:128]
                    else:
                        dk_acc[before, :] = own_k + dk_t[0:128]
                        dv_acc[before, :] = own_v + dv_t[0:128]
                    own_k, own_v = dk_t[128:256], dv_t[128:256]
                tail = pl.ds(T + r + (nb - 1) * 128 * d, 128, stride=d)
                dk_acc[tail, :] = own_k
                dv_acc[tail, :] = own_v

        @pl.when(i >= 1)
        def _():
            dq_raw, gq = norm_bwd(prev_ref[:, 0:128], dq_prev[...] * QK_SCALE, qw_ref[...], e)
            dk_raw, gk = norm_bwd(prev_ref[:, 128:256], dk_acc[pl.ds(0, T), :] * LN2, kw_ref[...], e)
            out_ref[:, 0:128] = dq_raw.astype(BF16)
            out_ref[:, 128:256] = dk_raw.astype(BF16)
            out_ref[:, 256:384] = dv_acc[pl.ds(0, T), :].astype(BF16)
            gqk_ref[0:1, :] += gq
            gqk_ref[1:2, :] += gk

        dq_prev[...] = dq_cur[...]
        dk_acc[pl.ds(0, T), :] = dk_acc[pl.ds(T, T), :]
        dv_acc[pl.ds(0, T), :] = dv_acc[pl.ds(T, T), :]

    last = nt - 1
    qtile = lambda p, i: (jnp.minimum(i, last), p)
    return pl.pallas_call(
        body, name=f"attn_bwd_g{g}", grid=(4, nt + 1),
        in_specs=[pl.BlockSpec((T, 384), lambda p, i: (jnp.minimum(i, last), g * 4 + p)),
                  pl.BlockSpec((T, 384), lambda p, i: (jnp.maximum(i - 1, 0), g * 4 + p)),
                  pl.BlockSpec((T, 128), qtile), pl.BlockSpec((T, 128), qtile), pl.BlockSpec((T, 128), qtile),
                  pl.BlockSpec((1, 128), lambda p, i: (0, 0)),
                  pl.BlockSpec((1, 128), lambda p, i: (0, 0)),
                  pl.BlockSpec((128, 128), lambda p, i: (0, 0)),
                  ANY],
        out_specs=[pl.BlockSpec((T, 384), lambda p, i: (jnp.maximum(i - 1, 0), g * 4 + p)),
                   pl.BlockSpec((None, 8, 128), lambda p, i: (p, 0, 0))],
        out_shape=[jax.ShapeDtypeStruct(dprojq.shape, BF16), jax.ShapeDtypeStruct((4, 8, 128), F32)],
        scratch_shapes=[pltpu.VMEM((T, 128), sdt), pltpu.VMEM((2 * T, 128), sdt), pltpu.VMEM((2 * T, 128), sdt),
                        pltpu.VMEM((T, 128), sdt), pltpu.VMEM((T, 128), F32), pltpu.VMEM((T, 128), F32),
                        pltpu.VMEM((2 * T, 128), F32), pltpu.VMEM((2 * T, 128), F32)],
        input_output_aliases={8: 0},
        compiler_params=_params(),
    )(projq, projq, d_o, lse, delta, qnw2, knw2, ebd, dprojq)


def _mid(projn, o_g, lse_g, x, tgt, mod, convw8, wc, wa, wo, ebd, *, tm=256):
    S, D = x.shape
    nt = S // tm
    nst = max(1, 256 // tm)
    assert nt % nst == 0
    hb = tm // 16

    def body(pn_ref, hc_ref, hx_ref, o0_ref, o1_ref, o2_ref, l0_ref, l1_ref, l2_ref, x_ref, t_ref, mod_ref,
             cw_ref, ebd_ref, wc_hbm, wa_hbm, wo_hbm,
             dpn_ref, do_ref, lse_ref, dl_ref, dout_ref, vacc_ref, gwo_hbm, gwc_hbm, gwa_hbm,
             wc_s, wa_s, wo_s, gwo_s, gwc_s, gwa_s, carry_s, *stashes):
        i = pl.program_id(0)
        ti = nt - 1 - i
        par = i % nst
        stash = pl.ds(pl.multiple_of(par * tm, tm), tm)

        @pl.when(i == 0)
        def _():
            for src, dst in ((wc_hbm, wc_s), (wa_hbm, wa_s), (wo_hbm, wo_s)):
                pltpu.sync_copy(src, dst)
            gwo_s[...] = jnp.zeros_like(gwo_s)
            gwc_s[...] = jnp.zeros_like(gwc_s)
            gwa_s[...] = jnp.zeros_like(gwa_s)
            carry_s[...] = jnp.zeros_like(carry_s)
            vacc_ref[...] = jnp.zeros_like(vacc_ref)

        rows = lax.broadcasted_iota(jnp.int32, (tm, D), 0)
        w0, w1, w2 = cw_ref[0:1, :], cw_ref[1:2, :], cw_ref[2:3, :]
        gate = mod_ref[:, 2 * D:3 * D]

        b_a = pn_ref[:, BA:BA + D].astype(F32)
        c_a = pn_ref[:, CA:CA + D].astype(F32)
        x_a = pn_ref[:, XA:XA + D].astype(F32)
        z_a = pn_ref[:, ZA:ZA + D].astype(F32)
        u = c_a * x_a
        has_prev = (ti > 0).astype(F32)
        uh = hc_ref[...].astype(F32) * hx_ref[...].astype(F32) * has_prev
        h14, h15 = _row(uh, 14), _row(uh, 15)
        u_m1 = jnp.where(rows == 0, h15, pltpu.roll(u, 1, 0))
        u_m2 = jnp.where(rows == 0, h14, jnp.where(rows == 1, h15, pltpu.roll(u, 2, 0)))
        conv = w0 * u_m2 + w1 * u_m1 + w2 * u
        sg_a = _sigmoid(z_a)
        sz_a = z_a * sg_a
        y_a = b_a * conv * sz_a
        y_ab = y_a.astype(BF16)
        pa = jnp.dot(y_ab, wc_s[...], preferred_element_type=F32)

        l0, l1, l2 = l0_ref[...], l1_ref[...], l2_ref[...]
        mxl = jnp.maximum(jnp.maximum(l0, l1), l2)
        e0, e1, e2 = jnp.exp(l0 - mxl), jnp.exp(l1 - mxl), jnp.exp(l2 - mxl)
        den = e0 + e1 + e2
        attn = (e0 * o0_ref[...] + e1 * o1_ref[...] + e2 * o2_ref[...]) / den
        lse_ref[...] = mxl + jnp.log(den)
        z_b = pn_ref[:, ZB:ZB + ATTN_OUT].astype(F32)
        sg_b = _sigmoid(z_b)
        sz_b = z_b * sg_b
        y_b = attn * sz_b
        y_bb = y_b.astype(BF16)
        pb = jnp.dot(y_bb, wa_s[...], preferred_element_type=F32)

        s_a = _sigmoid(pn_ref[:, GA:GA + D].astype(F32))
        s_b = _sigmoid(pn_ref[:, GB:GB + D].astype(F32))
        merged = s_a * pa + s_b * pb
        merged_b = merged.astype(BF16)
        mo = jnp.dot(merged_b, wo_s[...], preferred_element_type=F32)
        diff = x_ref[...] + gate * mo - t_ref[...]
        dout = diff * (1.0 / D)
        dout_ref[...] = dout.astype(BF16)
        vacc_ref[4:5, :] += jnp.sum(diff * diff, axis=0, keepdims=True)
        vacc_ref[0:1, :] += jnp.sum(dout * mo, axis=0, keepdims=True)

        d_mo = (dout * gate).astype(BF16)
        nt_dims = (((1,), (1,)), ((), ()))
        dmerged = lax.dot_general(d_mo, wo_s[...], nt_dims, preferred_element_type=F32)
        dpa = (dmerged * s_a).astype(BF16)
        dpb = (dmerged * s_b).astype(BF16)
        dpn_ref[:, GA:GA + D] = (dmerged * pa * s_a * (1.0 - s_a)).astype(BF16)
        dpn_ref[:, GB:GB + D] = (dmerged * pb * s_b * (1.0 - s_b)).astype(BF16)
        tn = (((0,), (0,)), ((), ()))
        if nst == 1:
            gwo_s[...] += lax.dot_general(merged_b, d_mo, tn, preferred_element_type=F32)
            gwc_s[...] += lax.dot_general(y_ab, dpa, tn, preferred_element_type=F32)
            gwa_s[...] += lax.dot_general(y_bb, dpb, tn, preferred_element_type=F32)
        else:
            st_m, st_d, st_ya, st_pa, st_yb, st_pb = stashes
            st_m[stash, :] = merged_b
            st_d[stash, :] = d_mo
            st_ya[stash, :] = y_ab
            st_pa[stash, :] = dpa
            st_yb[stash, :] = y_bb
            st_pb[stash, :] = dpb

            @pl.when(par == nst - 1)
            def _():
                gwo_s[...] += lax.dot_general(st_m[...], st_d[...], tn, preferred_element_type=F32)
                gwc_s[...] += lax.dot_general(st_ya[...], st_pa[...], tn, preferred_element_type=F32)
                gwa_s[...] += lax.dot_general(st_yb[...], st_pb[...], tn, preferred_element_type=F32)

        dy_a = lax.dot_general(dpa, wc_s[...], nt_dims, preferred_element_type=F32)
        dy_b = lax.dot_general(dpb, wa_s[...], nt_dims, preferred_element_type=F32)

        dattn = dy_b * sz_b
        dpn_ref[:, ZB:ZB + ATTN_OUT] = (dy_b * attn * sg_b * (1.0 + z_b * (1.0 - sg_b))).astype(BF16)
        do_ref[...] = dattn.astype(BF16)
        dl_ref[...] = jnp.dot((dattn * attn).astype(BF16), ebd_ref[...], preferred_element_type=F32)

        dpn_ref[:, BA:BA + D] = (dy_a * conv * sz_a).astype(BF16)
        dpn_ref[:, ZA:ZA + D] = (dy_a * b_a * conv * sg_a * (1.0 + z_a * (1.0 - sg_a))).astype(BF16)
        dconv = dy_a * b_a * sz_a
        vacc_ref[1:2, :] += jnp.sum(dconv * u_m2, axis=0, keepdims=True)
        vacc_ref[2:3, :] += jnp.sum(dconv * u_m1, axis=0, keepdims=True)
        vacc_ref[3:4, :] += jnp.sum(dconv * u, axis=0, keepdims=True)
        nxt = carry_s[...]
        n0, n1 = _row(nxt, 0), _row(nxt, 1)
        dc_p1 = jnp.where(rows == tm - 1, n0, pltpu.roll(dconv, tm - 1, 0))
        dc_p2 = jnp.where(rows == tm - 1, n1, jnp.where(rows == tm - 2, n0, pltpu.roll(dconv, tm - 2, 0)))
        du = w2 * dconv + w1 * dc_p1 + w0 * dc_p2
        carry_s[...] = dconv[0:8, :]
        dpn_ref[:, CA:CA + D] = (du * x_a).astype(BF16)
        dpn_ref[:, XA:XA + D] = (du * c_a).astype(BF16)

        @pl.when(i == nt - 1)
        def _():
            pltpu.sync_copy(gwo_s, gwo_hbm)
            pltpu.sync_copy(gwc_s, gwc_hbm)
            pltpu.sync_copy(gwa_s, gwa_hbm)

    rev = lambda i: (nt - 1 - i, 0)
    halo = lambda col: pl.BlockSpec((16, D), lambda i: (jnp.maximum((nt - 1 - i) * hb - 1, 0), col))
    tile512 = pl.BlockSpec((tm, ATTN_OUT), rev)
    tileD = pl.BlockSpec((tm, D), rev)
    const = lambda shape: pl.BlockSpec(shape, lambda i: (0, 0))
    return pl.pallas_call(
        body, name="mid_fwd_bwd", grid=(nt,),
        in_specs=[pl.BlockSpec((tm, NAT), rev), halo(CA // D), halo(XA // D)] + [tile512] * 6
                 + [tileD, tileD, const((1, 3 * D)), const((8, D)), const((ATTN_OUT, ATTN_OUT))] + [ANY] * 3,
        out_specs=[pl.BlockSpec((tm, NAT), rev), tile512, tile512, tile512, tileD, const((8, D)), ANY, ANY, ANY],
        out_shape=[jax.ShapeDtypeStruct((S, NAT), BF16), jax.ShapeDtypeStruct((S, ATTN_OUT), BF16),
                   jax.ShapeDtypeStruct((S, ATTN_OUT), F32), jax.ShapeDtypeStruct((S, ATTN_OUT), F32),
                   jax.ShapeDtypeStruct((S, D), BF16), jax.ShapeDtypeStruct((8, D), F32),
                   jax.ShapeDtypeStruct((D, D), F32), jax.ShapeDtypeStruct((D, D), F32),
                   jax.ShapeDtypeStruct((ATTN_OUT, D), F32)],
        scratch_shapes=[pltpu.VMEM((D, D), BF16), pltpu.VMEM((ATTN_OUT, D), BF16), pltpu.VMEM((D, D), BF16),
                        pltpu.VMEM((D, D), F32), pltpu.VMEM((D, D), F32), pltpu.VMEM((ATTN_OUT, D), F32),
                        pltpu.VMEM((8, D), F32)]
                       + ([pltpu.VMEM((nst * tm, D), BF16)] * 4
                          + [pltpu.VMEM((nst * tm, ATTN_OUT), BF16), pltpu.VMEM((nst * tm, D), BF16)] if nst > 1 else []),
        compiler_params=_params(vmem=V7X_VMEM_LIMIT_RESIDENT),
    )(projn, projn, projn, *o_g, *lse_g, x, tgt, mod, convw8, ebd, wc, wa, wo)


def _dh_matmul(dpn, dpq, w_nat, w_qkv, parts, *, tm=512):
    S = dpn.shape[0]
    D = D_MODEL
    n = len(parts)
    nt = S // tm

    def body(dn_ref, dq_ref, wn_hbm, wq_hbm, *rest):
        p_hbm, rest = rest[:n], rest[n:]
        dh_ref = rest[0]
        q_hbm = rest[1:1 + n]
        wn_s, wq_s = rest[1 + n:3 + n]
        sems = rest[3 + n:]

        @pl.when(pl.program_id(0) == 0)
        def _():
            if n:
                mine, _ = _scatter_copies(p_hbm, q_hbm, *sems)
                for cp in mine:
                    cp.start()
            pltpu.sync_copy(wn_hbm, wn_s)
            pltpu.sync_copy(wq_hbm, wq_s)

        if n:
            @pl.when(pl.program_id(0) == nt - 1)
            def _():
                mine, theirs = _scatter_copies(p_hbm, q_hbm, *sems)
                for cp in theirs:
                    cp.wait_recv()
                for cp in mine:
                    cp.wait_send()

        nt_dims = (((1,), (1,)), ((), ()))
        dh = (lax.dot_general(dn_ref[...], wn_s[...], nt_dims, preferred_element_type=F32)
              + lax.dot_general(dq_ref[...], wq_s[...], nt_dims, preferred_element_type=F32))
        dh_ref[...] = dh.astype(BF16)

    row = lambda w: pl.BlockSpec((tm, w), lambda i: (i, 0))
    outs = pl.pallas_call(
        body, name="dh_matmul", grid=(nt,),
        in_specs=[row(NAT), row(NQKV), ANY, ANY] + [ANY] * n,
        out_specs=[row(D)] + [ANY] * n,
        out_shape=[jax.ShapeDtypeStruct((S, D), BF16)] + [jax.ShapeDtypeStruct(p.shape, p.dtype) for p in parts],
        scratch_shapes=[pltpu.VMEM((D, NAT), BF16), pltpu.VMEM((D, NQKV), BF16)]
                       + ([pltpu.SemaphoreType.DMA((n, 3)), pltpu.SemaphoreType.DMA((n, 3))] if n else []),
        compiler_params=_params(vmem=V7X_VMEM_LIMIT_RESIDENT),
    )(dpn, dpq, w_nat, w_qkv, *parts)
    return outs[0], outs[1:]


def _norm_bwd(dh, x, dout, mod, norm_w, *, tm=512):
    S, D = x.shape

    def body(dh_ref, x_ref, dout_ref, mod_ref, nw_ref, gx_ref, st_ref):
        @pl.when(pl.program_id(0) == 0)
        def _():
            st_ref[...] = jnp.zeros_like(st_ref)

        dh = dh_ref[...].astype(F32)
        scale1 = 1.0 + mod_ref[:, D:2 * D]
        nw = nw_ref[...]
        xv = x_ref[...]
        rstd = lax.rsqrt(jnp.mean(xv * xv, axis=-1, keepdims=True) + EPS)
        xh = xv * rstd
        dhs = dh * scale1
        dxh = dhs * nw
        dx = rstd * (dxh - xh * jnp.mean(dxh * xh, axis=-1, keepdims=True))
        gx_ref[...] = dout_ref[...].astype(F32) + dx
        st_ref[0:1, :] += jnp.sum(dh, axis=0, keepdims=True)
        st_ref[1:2, :] += jnp.sum(dh * (xh * nw), axis=0, keepdims=True)
        st_ref[2:3, :] += jnp.sum(dhs * xh, axis=0, keepdims=True)

    row = pl.BlockSpec((tm, D), lambda i: (i, 0))
    return pl.pallas_call(
        body, name="norm_bwd", grid=(S // tm,),
        in_specs=[row, row, row, pl.BlockSpec((1, 3 * D), lambda i: (0, 0)), pl.BlockSpec((1, D), lambda i: (0, 0))],
        out_specs=[row, pl.BlockSpec((8, D), lambda i: (0, 0))],
        out_shape=[jax.ShapeDtypeStruct((S, D), F32), jax.ShapeDtypeStruct((8, D), F32)],
        compiler_params=_params(),
    )(dh, x, dout, mod, norm_w)


def _pack_smalls(st, vacc, gqk):
    D = D_MODEL

    def body(st_ref, va_ref, gqk_ref, o_ref):
        o_ref[...] = jnp.zeros_like(o_ref)
        o_ref[0:2, :] = st_ref[0:2, :]
        o_ref[2:3, :] = va_ref[0:1, :]
        o_ref[8:9, :] = st_ref[2:3, :]
        o_ref[16:19, :] = va_ref[1:4, :]
        tot = gqk_ref[0]
        for k in range(1, gqk_ref.shape[0]):
            tot = tot + gqk_ref[k]
        tot = tot + pltpu.roll(tot, HEAD_DIM, 1)
        o_ref[24:32, 0:128] = tot
        loss = jnp.sum(va_ref[4:5, :], axis=1, keepdims=True) * (0.5 / D)
        o_ref[26:27, :] = jnp.broadcast_to(loss, (1, D))

    return pl.pallas_call(
        body, name="pack_smalls", out_shape=jax.ShapeDtypeStruct((32, D), F32),
        in_specs=[pl.BlockSpec(memory_space=pltpu.VMEM)] * 3,
        out_specs=pl.BlockSpec(memory_space=pltpu.VMEM), compiler_params=_params(),
    )(st, vacc, gqk)


def _small_update(sm_loc, dm_pad, ct_pad, ada, vecs):
    D = D_MODEL
    nv = len(vecs)
    places = [(0, 3, D), (8, 1, D), (16, 3, 256), (24, 1, HEAD_DIM), (25, 1, HEAD_DIM)]

    def body(*refs):
        sm_ref, dm_ref, ct_ref = refs[0:3]
        wmv = [refs[3 + 3 * t:6 + 3 * t] for t in range(nv + 1)]
        outs = refs[3 + 3 * (nv + 1):]
        res = [outs[4 * t:4 * t + 4] for t in range(nv + 1)]
        loss_ref, tot_s = outs[4 * (nv + 1)], outs[4 * (nv + 1) + 1]

        def update(g, w, m, v, out, at=(slice(None), slice(None))):
            d, mn, vn = _adamw_math(w[at], g, m[at], v[at])
            for ref, val in zip(out, (g, d, mn, vn)):
                ref[at] = val

        gw = jnp.dot(ct_ref[...], dm_ref[...], preferred_element_type=F32, precision=lax.Precision.HIGHEST)
        update(gw, *wmv[0], res[0])
        tot = sm_ref[0]
        for k in range(1, 8):
            tot = tot + sm_ref[k]
        tot_s[...] = tot
        for k in range(3):
            update(tot_s[k:k + 1, :], *wmv[1], res[1], at=(slice(None), slice(k * D, (k + 1) * D)))
        for t in range(1, nv):
            r0, nr, nl = places[t]
            update(tot_s[r0:r0 + nr, 0:nl], *wmv[1 + t], res[1 + t])
        loss_ref[...] = jnp.broadcast_to(tot_s[26:27, 0:128], (8, 128))

    vm = pl.BlockSpec(memory_space=pltpu.VMEM)
    groups = [ada] + list(vecs)
    out_shape = []
    for w, _, _ in groups:
        out_shape += [jax.ShapeDtypeStruct(w.shape, F32)] * 4
    out_shape.append(jax.ShapeDtypeStruct((8, 128), F32))
    flat = [a for grp in groups for a in grp]
    outs = pl.pallas_call(
        body, name="small_update", out_shape=out_shape,
        in_specs=[vm] * (3 + len(flat)), out_specs=[vm] * len(out_shape),
        scratch_shapes=[pltpu.VMEM((32, D), F32)], compiler_params=_params(),
    )(sm_loc, dm_pad, ct_pad, *flat)
    return [outs[4 * t:4 * t + 4] for t in range(nv + 1)], outs[-1]


def _later_weights(gathered, shards, chip):
    D = D_MODEL
    g_bc, g_ba, g_wo = [lax.dynamic_update_slice(g, s[None], (chip, 0, 0, 0)) for g, s in zip(gathered, shards)]
    wa = g_ba.reshape(4, ATTN_OUT, 256).transpose(1, 0, 2).reshape(ATTN_OUT, D)
    return g_bc.reshape(D, D), wa, g_wo.reshape(D, D)


def _local_step(xs, tg, mod, proj_nat, w_qkv, later, convw8, q_norm_w, k_norm_w):
    S = xs.shape[0]
    lane = jnp.arange(128)
    ebd128 = (lane[:, None] // HEAD_DIM == lane[None, :] // HEAD_DIM).astype(BF16)
    lane5 = jnp.arange(ATTN_OUT)
    ebd512 = (lane5[:, None] // HEAD_DIM == lane5[None, :] // HEAD_DIM).astype(BF16)
    qnw2 = jnp.tile(q_norm_w, (1, 2))
    knw2 = jnp.tile(k_norm_w, (1, 2))

    projn, h, ht = proj_nat
    if len(later) == 2:
        projq, gathered = _inproj_qkv(h, w_qkv, later[0])
        wc, wa, wo = _later_weights(gathered, *later)
    else:
        projq, _ = _inproj_qkv(h, w_qkv, [])
        wc, wa, wo = later
    o_g, lse_g = [], []
    for g in range(N_GROUPS):
        o, l = _attn_fwd(projq, qnw2, knw2, ebd128, g)
        o_g.append(o)
        lse_g.append(l)

    dpn, d_o, lse, delta, dout, vacc, gwo, gwc, gwa = _mid(
        projn, o_g, lse_g, xs, tg, mod, convw8, wc, wa, wo, ebd512)
    dpq = lax.empty((S, NQKV), BF16)
    gqk = []
    for g in range(N_GROUPS):
        dpq, part = _attn_bwd(projq, dpq, d_o, lse, delta, qnw2, knw2, ebd128, g)
        gqk.append(part)
    return dpn, dpq, dout, ht, gwc, gwa, gwo, vacc, gqk


def kernel(x, c, w_ada, b_ada, norm_w, w_in, conv_w, q_norm_w, k_norm_w, w_br_conv, w_br_attn, w_out, loss_target, m_w_ada, m_b_ada, m_norm_w, m_w_in, m_conv_w, m_q_norm_w, m_k_norm_w, m_w_br_conv, m_w_br_attn, m_w_out, v_w_ada, v_b_ada, v_norm_w, v_w_in, v_conv_w, v_q_norm_w, v_k_norm_w, v_w_br_conv, v_w_br_attn, v_w_out):
    D = D_MODEL
    S = x.shape[1]
    xs, tg = x[0], loss_target[0]
    mx, my, mc = lax.axis_index("x"), lax.axis_index("y"), lax.axis_index("c")
    chip = 2 * mx + my
    dev = 2 * chip + mc

    c_blk = jnp.concatenate([c, jnp.pad(conv_w[0], ((0, 0), (0, D - 256))), jnp.zeros((4, D), F32)], axis=0)
    b_shard = lax.dynamic_slice(b_ada, (0, chip * 768), (1, 768))
    shards = [w_in[0].astype(BF16).reshape(2, 512, 2816), w_br_conv[0].astype(BF16).reshape(2, 128, D),
              w_br_attn[0].astype(BF16).reshape(2, 256, 256), w_out[0].astype(BF16).reshape(2, 128, D)]
    piece_a = shards[0][0].reshape(2, 256, SHARD_COLS)
    piece_b = shards[0][1].reshape(2, 256, SHARD_COLS)
    gath_a, _, cg, mod_parts, silu_c = _weights_allgather(
        piece_a.reshape(2, 2, 128, SHARD_COLS), [], c_blk, w_ada[0], b_shard)
    gath_a = gath_a.reshape(4, 2, 256, SHARD_COLS)
    convw8 = jnp.pad(cg[::2, 1:4, 0:256].transpose(1, 0, 2).reshape(3, D), ((0, 5), (0, 0)))
    mod_all = mod_parts.transpose(1, 0, 2).reshape(8, 3 * D)
    mod = lax.dynamic_slice(mod_all, (dev, 0), (1, 3 * D))
    chipv = jnp.reshape(chip, (1,)).astype(jnp.int32)
    w_nat = _weights_prep(gath_a, piece_a, chipv, "nat", 0, lax.empty((D, NAT), BF16))
    w_qkv = _weights_prep(gath_a, piece_a, chipv, "qkv", 0, lax.empty((D, NQKV), BF16))
    partial, h, ht, gath_b = _inproj_nat_a(xs, mod, norm_w, w_nat, piece_b)
    w_nat = _weights_prep(gath_b, piece_b, chipv, "nat", 1, w_nat)
    w_qkv = _weights_prep(gath_b, piece_b, chipv, "qkv", 1, w_qkv)
    projn = _inproj_nat_b(h, partial, w_nat)

    dpn, dpq, dout, ht, gwc, gwa, gwo, vacc, gqk = _local_step(
        xs, tg, mod, (projn, h, ht), w_qkv, (shards[1:], chip), convw8, q_norm_w, k_norm_w)

    def halves(a):
        k, r, cc = a.shape
        return a.reshape(k, 2, r // 2, cc).transpose(1, 0, 2, 3)

    gw_nat = _matmul_lhs_resident(ht, dpn, tn=512, name="grad_w_in_nat").reshape(2, D // 2, NAT)
    gw_qkv, recv_nat = _matmul_lhs_resident(ht, dpq, tn=512, name="grad_w_in_qkv", exchange=gw_nat)
    grads = [gw_nat, gw_qkv.reshape(2, D // 2, NQKV),
             halves(gwc.astype(BF16).reshape(4, 256, D)),
             halves(gwa.astype(BF16).reshape(ATTN_OUT, 4, 256).transpose(1, 0, 2)),
             halves(gwo.astype(BF16).reshape(4, 256, D))]
    recv = [recv_nat] + list(_pair_exchange(grads[1:]))
    core = jnp.reshape(mc, (1,)).astype(jnp.int32)
    p_in = _pair_add_to_shards(core, grads[0], recv[0], lax.empty((4, D // 2, SHARD_COLS), BF16), "nat")
    p_in = _pair_add_to_shards(core, grads[1], recv[1], p_in, "qkv")
    parts = [p_in] + [_pair_add(core, gr, rc) for gr, rc in zip(grads[2:], recv[2:])]
    dh, gathered = _dh_matmul(dpn, dpq, w_nat, w_qkv, parts)
    grad_x, st = _norm_bwd(dh, xs, dout, mod, norm_w)
    mine = [_sum4(chipv, p, q) for p, q in zip(parts, gathered)]
    theirs, sm_all = _half_exchange(mine, _pack_smalls(st, vacc, jnp.concatenate(gqk, axis=0)))

    big = {}
    for t, (nm, w, m, v) in enumerate((("w_in", w_in, m_w_in, v_w_in), ("w_br_conv", w_br_conv, m_w_br_conv, v_w_br_conv),
                                       ("w_br_attn", w_br_attn, m_w_br_attn, v_w_br_attn), ("w_out", w_out, m_w_out, v_w_out))):
        big[nm] = _adamw_halves(core, w[0], mine[t], theirs[t], m[0], v[0], "adamw_" + nm)

    dmod_all = sm_all[:, 0:3, :].reshape(8, 3 * D)
    dm_pad = jnp.pad(lax.dynamic_slice(dmod_all, (0, chip * 768), (8, 768)), ((0, 120), (0, 0)))
    conv_dev = lax.dynamic_slice(sm_all[:, 16:24, :], (0, 0, chip * 256), (8, 8, 256))
    sm_loc = jnp.concatenate([sm_all[:, 0:16, :], jnp.pad(conv_dev, ((0, 0), (0, 0), (0, D - 256))), sm_all[:, 24:32, :]], axis=1)

    ct_pad = jnp.pad(silu_c.T, ((0, 0), (0, 120)))
    small_names = ["b_ada", "norm_w", "conv_w", "q_norm_w", "k_norm_w"]
    vecs = [(b_ada, m_b_ada, v_b_ada), (norm_w, m_norm_w, v_norm_w), (conv_w[0], m_conv_w[0], v_conv_w[0]),
            (q_norm_w, m_q_norm_w, v_q_norm_w), (k_norm_w, m_k_norm_w, v_k_norm_w)]
    updated, loss_blk = _small_update(sm_loc, dm_pad, ct_pad, (w_ada[0], m_w_ada[0], v_w_ada[0]), vecs)
    small = {"w_ada": [a[None] for a in updated[0]]}
    for nm, four in zip(small_names, updated[1:]):
        small[nm] = [a[None] for a in four] if nm == "conv_w" else list(four)

    order = ["w_ada", "b_ada", "norm_w", "w_in", "conv_w", "q_norm_w", "k_norm_w", "w_br_conv", "w_br_attn", "w_out"]
    res = [loss_blk[0, 0], grad_x[None]]
    for kind in range(4):
        for nm in order:
            res.append(big[nm][kind][None] if nm in big else small[nm][kind])
    return tuple(res)
```

```python
import jax
import jax.numpy as jnp
from jax import lax
from jax.experimental import pallas as pl
from jax.experimental.pallas import tpu as pltpu

F32 = jnp.float32
BF16 = jnp.bfloat16
MESH = pl.DeviceIdType.MESH
ANY = pl.BlockSpec(memory_space=pl.ANY)

D_MODEL = 1024
HEAD_DIM = 64
N_GROUPS = 3
DILATIONS = (1, 4, 16)
ATTN_OUT = 512
EPS = 1e-6
NEG_INF = -1e30
NAT = 6656
NQKV = 4608
BA, CA, XA, ZA, ZB, GA, GB = 0, 1024, 2048, 3072, 4096, 4608, 5632
ATT_TILE = 2048
QK_SCALE = HEAD_DIM ** -0.5
LOG2E = 1.4426950408889634
LN2 = 0.6931471805599453
V7X_VMEM_LIMIT = 56 * 1024 * 1024
V7X_VMEM_LIMIT_RESIDENT = 60 * 1024 * 1024

ADAM_LR = 0.001
ADAM_B1 = 0.9
ADAM_B2 = 0.999
ADAM_EPS = 1e-08
ADAM_WD = 0.01
ADAM_STEP = 10


def _params(vmem=V7X_VMEM_LIMIT, **kw):
    return pltpu.CompilerParams(vmem_limit_bytes=vmem, **kw)


def _sigmoid(z):
    return 1.0 / (1.0 + jnp.exp(-z))


def _row(v, r):
    rows = lax.broadcasted_iota(jnp.int32, v.shape, 0)
    return jnp.sum(jnp.where(rows == r, v, 0.0), axis=0, keepdims=True)


def _coords():
    return lax.axis_index("x"), lax.axis_index("y"), lax.axis_index("c")


def _flip(v, bit):
    return 1 - v if bit else v


def _weights_allgather(big_shard, shards, c_blk, w_ada, b_shard):
    n = len(shards)
    D = D_MODEL
    cols = w_ada.shape[1]

    def body(*refs):
        big_in, ins = refs[0], refs[1:1 + n]
        c_ref, wada_ref, b_ref = refs[1 + n:4 + n]
        big_out, outs = refs[4 + n], refs[5 + n:5 + 2 * n]
        cg_ref, mp_ref, sc_ref = refs[5 + 2 * n:8 + 2 * n]
        sem_refs = refs[8 + 2 * n:]
        ssem, rsem = sem_refs[:2] if n else (None, None)
        bs, br, cs, cr, ms, mr, lsem = sem_refs[2 if n else 0:]
        mx, my, mc = _coords()
        me = 2 * mx + my
        me8 = 2 * me + mc
        sib = (mx, my, 1 - mc)
        chips = [(_flip(mx, j & 2), _flip(my, j & 1)) for j in range(1, 4)]
        sends = []

        local = pltpu.make_async_copy(c_ref, cg_ref.at[me8], lsem)
        local.start()
        for j in range(1, 8):
            peer = (_flip(mx, j & 4), _flip(my, j & 2), _flip(mc, j & 1))
            cp = pltpu.make_async_remote_copy(
                src_ref=c_ref, dst_ref=cg_ref.at[me8], send_sem=cs.at[j - 1], recv_sem=cr.at[j - 1],
                device_id=peer, device_id_type=MESH)
            cp.start()
            sends.append(cp)

        kx, ky, kd = 2 * (1 - mx) + my, 2 * mx + (1 - my), 2 * (1 - mx) + (1 - my)
        xn, yn = (1 - mx, my, mc), (mx, 1 - my, mc)

        def big(src_chip, q, sem, to, half=None):
            h = mc if half is None else half
            return pltpu.make_async_remote_copy(
                src_ref=big_out.at[src_chip, h, q], dst_ref=big_out.at[src_chip, h, q],
                send_sem=bs.at[sem], recv_sem=br.at[sem], device_id=to, device_id_type=MESH)

        def own(q, sem, to):
            return pltpu.make_async_remote_copy(
                src_ref=big_in.at[mc, q], dst_ref=big_out.at[me, mc, q],
                send_sem=bs.at[sem], recv_sem=br.at[sem], device_id=to, device_id_type=MESH)

        for cp in (own(0, 0, xn), own(1, 3, yn), own(1, 1, xn), own(0, 2, yn)):
            cp.start()
            sends.append(cp)
        for t in range(n):
            for j, (px, py) in enumerate(chips):
                cp = pltpu.make_async_remote_copy(
                    src_ref=ins[t].at[mc], dst_ref=outs[t].at[me, mc], send_sem=ssem.at[t, j],
                    recv_sem=rsem.at[t, j], device_id=(px, py, mc), device_id_type=MESH)
                cp.start()
                sends.append(cp)

        pieces = [(kx, 0), (kx, 1), (ky, 0), (ky, 1), (kd, 0), (kd, 1)]

        def landed(piece, sem, frm, pass_on=None):
            src_chip, q = pieces[piece]
            big(src_chip, q, sem, frm).wait_recv()
            nxt = ([] if pass_on is None else [big(src_chip, q, pass_on[0], pass_on[1])])
            nxt.append(big(src_chip, q, 6 + piece, sib))
            for cp in nxt:
                cp.start()
                sends.append(cp)

        for j in range(1, 8):
            px, py, pc = _flip(mx, j & 4), _flip(my, j & 2), _flip(mc, j & 1)
            pltpu.make_async_remote_copy(
                src_ref=c_ref, dst_ref=cg_ref.at[4 * px + 2 * py + pc], send_sem=cs.at[j - 1],
                recv_sem=cr.at[j - 1], device_id=(px, py, pc), device_id_type=MESH).wait_recv()
        local.wait()
        pick = (lax.broadcasted_iota(jnp.int32, (8, 64), 1) == 8 * lax.broadcasted_iota(jnp.int32, (8, 64), 0))
        cv = jnp.dot(pick.astype(F32), cg_ref[...].reshape(64, D), preferred_element_type=F32,
                     precision=lax.Precision.HIGHEST)
        sc = cv * _sigmoid(cv)
        sc_ref[...] = sc
        mp_ref[me] = jnp.dot(sc, wada_ref[...], preferred_element_type=F32,
                             precision=lax.Precision.HIGHEST) + b_ref[...]
        for j, (px, py) in enumerate(chips):
            cp = pltpu.make_async_remote_copy(
                src_ref=mp_ref.at[me], dst_ref=mp_ref.at[me], send_sem=ms.at[j], recv_sem=mr.at[j],
                device_id=(px, py, mc), device_id_type=MESH)
            cp.start()
            sends.append(cp)

        landed(0, 0, xn, pass_on=(4, yn))
        landed(3, 3, yn, pass_on=(5, xn))
        landed(1, 1, xn)
        landed(2, 2, yn)
        for j, (px, py) in enumerate(chips):
            pltpu.make_async_remote_copy(
                src_ref=mp_ref.at[me], dst_ref=mp_ref.at[2 * px + py], send_sem=ms.at[j], recv_sem=mr.at[j],
                device_id=(px, py, mc), device_id_type=MESH).wait_recv()
        landed(4, 4, yn)
        landed(5, 5, xn)
        for t in range(n):
            for j, (px, py) in enumerate(chips):
                src = 2 * px + py
                pltpu.make_async_remote_copy(
                    src_ref=ins[t].at[mc], dst_ref=outs[t].at[src, mc], send_sem=ssem.at[t, j],
                    recv_sem=rsem.at[t, j], device_id=(px, py, mc), device_id_type=MESH).wait_recv()
                fwd = pltpu.make_async_remote_copy(
                    src_ref=outs[t].at[src, mc], dst_ref=outs[t].at[src, mc], send_sem=ssem.at[t, 3 + j],
                    recv_sem=rsem.at[t, 3 + j], device_id=sib, device_id_type=MESH)
                fwd.start()
                sends.append(fwd)
        for t in range(n):
            for j, (px, py) in enumerate(chips):
                src = 2 * px + py
                pltpu.make_async_remote_copy(
                    src_ref=outs[t].at[src, 1 - mc], dst_ref=outs[t].at[src, 1 - mc], send_sem=ssem.at[t, 3 + j],
                    recv_sem=rsem.at[t, 3 + j], device_id=sib, device_id_type=MESH).wait_recv()
        for i, (src_chip, q) in enumerate(pieces):
            big(src_chip, q, 6 + i, sib, half=1 - mc).wait_recv()
        for cp in sends:
            cp.wait_send()

    vm = pl.BlockSpec(memory_space=pltpu.VMEM)
    outs = pl.pallas_call(
        body, name="weights_allgather",
        out_shape=[jax.ShapeDtypeStruct((4,) + big_shard.shape, big_shard.dtype)]
                  + [jax.ShapeDtypeStruct((4,) + s.shape, s.dtype) for s in shards]
                  + [jax.ShapeDtypeStruct((8,) + c_blk.shape, F32), jax.ShapeDtypeStruct((4, 8, cols), F32),
                     jax.ShapeDtypeStruct((8, D), F32)],
        in_specs=[ANY] * (n + 1) + [vm] * 3, out_specs=[ANY] * (n + 1) + [vm] * 3,
        scratch_shapes=([pltpu.SemaphoreType.DMA((n, 6)), pltpu.SemaphoreType.DMA((n, 6))] if n else [])
                       + [pltpu.SemaphoreType.DMA((12,)), pltpu.SemaphoreType.DMA((12,)),
                          pltpu.SemaphoreType.DMA((7,)), pltpu.SemaphoreType.DMA((7,)),
                          pltpu.SemaphoreType.DMA((3,)), pltpu.SemaphoreType.DMA((3,)), pltpu.SemaphoreType.DMA(())],
        compiler_params=_params(),
    )(big_shard, *shards, c_blk, w_ada, b_shard)
    return outs[0], outs[1:1 + n], outs[1 + n], outs[2 + n], outs[3 + n]


def _shard_gather_copies(ins, outs, ssem, rsem):
    mx, my, mc = _coords()
    me = 2 * mx + my
    sib = (mx, my, 1 - mc)
    send, recv, fwd, fwd_recv = [], [], [], []
    for t in range(len(ins)):
        for j in range(3):
            px, py = _flip(mx, (j + 1) & 2), _flip(my, (j + 1) & 1)
            src = 2 * px + py
            far = dict(send_sem=ssem.at[t, j], recv_sem=rsem.at[t, j], device_id=(px, py, mc), device_id_type=MESH)
            near = dict(send_sem=ssem.at[t, 3 + j], recv_sem=rsem.at[t, 3 + j], device_id=sib, device_id_type=MESH)
            send.append(pltpu.make_async_remote_copy(src_ref=ins[t].at[mc], dst_ref=outs[t].at[me, mc], **far))
            recv.append(pltpu.make_async_remote_copy(src_ref=ins[t].at[mc], dst_ref=outs[t].at[src, mc], **far))
            fwd.append(pltpu.make_async_remote_copy(src_ref=outs[t].at[src, mc], dst_ref=outs[t].at[src, mc], **near))
            fwd_recv.append(pltpu.make_async_remote_copy(
                src_ref=outs[t].at[src, 1 - mc], dst_ref=outs[t].at[src, 1 - mc], **near))
    return send, recv, fwd, fwd_recv


def _pair_exchange(grads):
    n = len(grads)

    def body(*refs):
        ins, outs = refs[:n], refs[n:2 * n]
        ssem, rsem = refs[2 * n:]
        mx, my, mc = _coords()
        sib = (mx, my, 1 - mc)
        sends = []
        for t in range(n):
            cp = pltpu.make_async_remote_copy(
                src_ref=ins[t].at[1 - mc], dst_ref=outs[t], send_sem=ssem.at[t], recv_sem=rsem.at[t],
                device_id=sib, device_id_type=MESH)
            cp.start()
            sends.append(cp)
        for cp in sends:
            cp.wait_recv()
        for cp in sends:
            cp.wait_send()

    return pl.pallas_call(
        body, name="grad_pair_exchange",
        out_shape=[jax.ShapeDtypeStruct(g.shape[1:], g.dtype) for g in grads],
        in_specs=[ANY] * n, out_specs=[ANY] * n,
        scratch_shapes=[pltpu.SemaphoreType.DMA((n,)), pltpu.SemaphoreType.DMA((n,))],
    )(*grads)


def _scatter_copies(ins, outs, ssem, rsem):
    mx, my, mc = _coords()
    me = 2 * mx + my
    mine, theirs = [], []
    for t in range(len(ins)):
        for j in range(1, 4):
            px, py = _flip(mx, j & 2), _flip(my, j & 1)
            mine.append(pltpu.make_async_remote_copy(
                src_ref=ins[t].at[2 * px + py], dst_ref=outs[t].at[me], send_sem=ssem.at[t, j - 1],
                recv_sem=rsem.at[t, j - 1], device_id=(px, py, mc), device_id_type=MESH))
            theirs.append(pltpu.make_async_remote_copy(
                src_ref=ins[t].at[me], dst_ref=outs[t].at[2 * px + py], send_sem=ssem.at[t, j - 1],
                recv_sem=rsem.at[t, j - 1], device_id=(px, py, mc), device_id_type=MESH))
    return mine, theirs


def _half_exchange(halves, smalls):
    n = len(halves)

    def body(*refs):
        ins, sm_ref = refs[:n], refs[n]
        outs, all_ref = refs[n + 1:2 * n + 1], refs[2 * n + 1]
        ssem, rsem, gsem, hsem, lsem = refs[2 * n + 2:]
        mx, my, mc = _coords()
        me = 4 * mx + 2 * my + mc
        sib = (mx, my, 1 - mc)
        local = pltpu.make_async_copy(sm_ref, all_ref.at[me], lsem)
        local.start()
        sends = []
        for t in range(n):
            rc = pltpu.make_async_remote_copy(
                src_ref=ins[t], dst_ref=outs[t], send_sem=ssem.at[t], recv_sem=rsem.at[t],
                device_id=sib, device_id_type=MESH)
            rc.start()
            sends.append(rc)
        gathers = []
        for j in range(1, 8):
            peer = (_flip(mx, j & 4), _flip(my, j & 2), _flip(mc, j & 1))
            cp = pltpu.make_async_remote_copy(
                src_ref=sm_ref, dst_ref=all_ref.at[me], send_sem=gsem.at[j - 1], recv_sem=hsem.at[j - 1],
                device_id=peer, device_id_type=MESH)
            cp.start()
            gathers.append(cp)
        for cp in sends:
            cp.wait_recv()
        for j in range(1, 8):
            px, py, pc = _flip(mx, j & 4), _flip(my, j & 2), _flip(mc, j & 1)
            pltpu.make_async_remote_copy(
                src_ref=sm_ref, dst_ref=all_ref.at[4 * px + 2 * py + pc], send_sem=gsem.at[j - 1],
                recv_sem=hsem.at[j - 1], device_id=(px, py, pc), device_id_type=MESH).wait_recv()
        for cp in sends + gathers:
            cp.wait_send()
        local.wait()

    vm = pl.BlockSpec(memory_space=pltpu.VMEM)
    outs = pl.pallas_call(
        body, name="grad_half_exchange",
        out_shape=[jax.ShapeDtypeStruct(h.shape, h.dtype) for h in halves]
                  + [jax.ShapeDtypeStruct((8,) + smalls.shape, smalls.dtype)],
        in_specs=[ANY] * n + [vm], out_specs=[ANY] * n + [vm],
        scratch_shapes=[pltpu.SemaphoreType.DMA((n,)), pltpu.SemaphoreType.DMA((n,)),
                        pltpu.SemaphoreType.DMA((7,)), pltpu.SemaphoreType.DMA((7,)), pltpu.SemaphoreType.DMA(())],
    )(*halves, smalls)
    return outs[:n], outs[n]


def _row_tile(rows, cols, bytes_per_row_elem=4, budget=2 * 1024 * 1024):
    t = rows
    while t % 2 == 0 and t > 16 and t * cols * bytes_per_row_elem > budget:
        t //= 2
    return t


def _pair_add(core, g, r):
    _, _, rh, cc = g.shape
    tr = _row_tile(rh, cc)

    def body(core_ref, g_ref, r_ref, o_ref):
        o_ref[...] = (g_ref[...].astype(F32) + r_ref[...].astype(F32)).astype(o_ref.dtype)

    return pl.pallas_call(
        body, name="grad_pair_add",
        grid_spec=pltpu.PrefetchScalarGridSpec(
            num_scalar_prefetch=1, grid=(4, rh // tr),
            in_specs=[pl.BlockSpec((None, None, tr, cc), lambda k, i, c: (c[0], k, i, 0)),
                      pl.BlockSpec((None, tr, cc), lambda k, i, c: (k, i, 0))],
            out_specs=pl.BlockSpec((None, tr, cc), lambda k, i, c: (k, i, 0))),
        out_shape=jax.ShapeDtypeStruct(r.shape, BF16),
        compiler_params=_params(),
    )(core, g, r)


def _sum4(chip, p, q):
    _, rh, cc = q.shape
    tr = _row_tile(rh, cc)

    def body(chip_ref, p_ref, q1_ref, q2_ref, q3_ref, o_ref):
        o_ref[...] = ((p_ref[...].astype(F32) + q1_ref[...].astype(F32)) + q2_ref[...].astype(F32)) + q3_ref[...].astype(F32)

    def other(j):
        return pl.BlockSpec((None, tr, cc), lambda i, c: (jnp.bitwise_xor(c[0], j), i, 0))

    return pl.pallas_call(
        body, name="grad_sum4",
        grid_spec=pltpu.PrefetchScalarGridSpec(
            num_scalar_prefetch=1, grid=(rh // tr,),
            in_specs=[pl.BlockSpec((None, tr, cc), lambda i, c: (c[0], i, 0)), other(1), other(2), other(3)],
            out_specs=pl.BlockSpec((tr, cc), lambda i, c: (i, 0))),
        out_shape=jax.ShapeDtypeStruct((rh, cc), F32),
        compiler_params=_params(),
    )(chip, p, q, q, q)


def _adamw_math(w, g, m, v):
    m = ADAM_B1 * m + (1.0 - ADAM_B1) * g
    v = ADAM_B2 * v + (1.0 - ADAM_B2) * (g * g)
    m_hat = m / (1.0 - ADAM_B1 ** ADAM_STEP)
    v_hat = v / (1.0 - ADAM_B2 ** ADAM_STEP)
    delta = -ADAM_LR * (m_hat / (jnp.sqrt(v_hat) + ADAM_EPS) + ADAM_WD * w)
    return delta, m, v


def _adamw_halves(core, w, mine, recv, m, v, name):
    rows, cols = w.shape
    rh = rows // 2
    tr = _row_tile(rh, cols, budget=1024 * 1024)
    nh = rh // tr

    def body(core_ref, w_ref, a_ref, b_ref, m_ref, v_ref, g_ref, d_ref, mo_ref, vo_ref):
        g = jnp.where(pl.program_id(0) == core_ref[0], a_ref[...], b_ref[...])
        d, mn, vn = _adamw_math(w_ref[...], g, m_ref[...], v_ref[...])
        g_ref[...] = g
        d_ref[...] = d
        mo_ref[...] = mn
        vo_ref[...] = vn

    full = pl.BlockSpec((tr, cols), lambda h, i, c: (h * nh + i, 0))
    mine_spec = pl.BlockSpec((tr, cols), lambda h, i, c: (jnp.where(h == c[0], i, 0), 0))
    recv_spec = pl.BlockSpec((tr, cols), lambda h, i, c: (jnp.where(h == c[0], 0, i), 0))
    sd = jax.ShapeDtypeStruct((rows, cols), F32)
    return pl.pallas_call(
        body, name=name,
        grid_spec=pltpu.PrefetchScalarGridSpec(
            num_scalar_prefetch=1, grid=(2, nh),
            in_specs=[full, mine_spec, recv_spec, full, full], out_specs=[full] * 4),
        out_shape=[sd, sd, sd, sd], compiler_params=_params(),
    )(core, w, mine, recv, m, v)


def _inproj_nat(x, mod, norm_w, w, *, tm=1024, tn=1664):
    S, D = x.shape
    N = w.shape[1]
    nt, nj = S // tm, N // tn
    chunk = 256

    def body(x_ref, mod_ref, nw_ref, w_ref, proj_ref, h_ref, ht_ref, h_a, ht_a, h_b, ht_b):
        i, j = pl.program_id(0), pl.program_id(1)

        def make_h(h_s, ht_s):
            shift = mod_ref[:, 0:D]
            scale1 = 1.0 + mod_ref[:, D:2 * D]
            for r in range(tm // chunk):
                xv = x_ref[pl.ds(r * chunk, chunk), :]
                ms = jnp.mean(xv * xv, axis=-1, keepdims=True)
                h = (xv * lax.rsqrt(ms + EPS) * nw_ref[...]) * scale1 + shift
                h_s[pl.ds(r * chunk, chunk), :] = h.astype(BF16)
                ht_s[:, pl.ds(r * chunk, chunk)] = h.T.astype(BF16)

        @pl.when((i == 0) & (j == 0))
        def _():
            make_h(h_a, ht_a)

        def step(cur, nxt):
            @pl.when(j == 0)
            def _():
                h_ref[...] = cur[0][...]
                ht_ref[...] = cur[1][...]

            def project():
                proj_ref[...] = jnp.dot(cur[0][...], w_ref[...], preferred_element_type=F32).astype(BF16)

            ahead = (j == nj - 1) & (i < nt - 1)

            @pl.when(ahead)
            def _():
                project()
                make_h(*nxt)

            @pl.when(jnp.logical_not(ahead))
            def _():
                project()

        @pl.when(i % 2 == 0)
        def _():
            step((h_a, ht_a), (h_b, ht_b))

        @pl.when(i % 2 == 1)
        def _():
            step((h_b, ht_b), (h_a, ht_a))

    def x_tile(i, j):
        return (jnp.where((i == 0) & (j == 0), 0, jnp.minimum(i + 1, nt - 1)), 0)

    return pl.pallas_call(
        body, name="inproj_nat", grid=(nt, nj),
        in_specs=[pl.BlockSpec((tm, D), x_tile),
                  pl.BlockSpec((1, 3 * D), lambda i, j: (0, 0)),
                  pl.BlockSpec((1, D), lambda i, j: (0, 0)),
                  pl.BlockSpec((D, tn), lambda i, j: (0, j))],
        out_specs=[pl.BlockSpec((tm, tn), lambda i, j: (i, j)),
                   pl.BlockSpec((tm, D), lambda i, j: (i, 0)),
                   pl.BlockSpec((D, tm), lambda i, j: (0, i))],
        out_shape=[jax.ShapeDtypeStruct((S, N), BF16), jax.ShapeDtypeStruct((S, D), BF16),
                   jax.ShapeDtypeStruct((D, S), BF16)],
        scratch_shapes=[pltpu.VMEM((tm, D), BF16), pltpu.VMEM((D, tm), BF16),
                        pltpu.VMEM((tm, D), BF16), pltpu.VMEM((D, tm), BF16)],
        compiler_params=_params(),
    )(x, mod, norm_w, w)


def _inproj_qkv(a, b, shards, *, tm=1024, tn=1536):
    M, K = a.shape
    N = b.shape[1]
    n = len(shards)
    ni, nj = M // tm, N // tn

    def body(a_ref, b_ref, *rest):
        ins, o_ref, outs, sems = rest[:n], rest[n], rest[n + 1:2 * n + 1], rest[2 * n + 1:]
        i, j = pl.program_id(0), pl.program_id(1)
        if n:
            @pl.when((i == 0) & (j == 0))
            def _():
                for cp in _shard_gather_copies(ins, outs, *sems)[0]:
                    cp.start()

        o_ref[...] = jnp.dot(a_ref[...], b_ref[...], preferred_element_type=F32).astype(o_ref.dtype)

        if n:
            @pl.when((i == ni - 1) & (j == 0))
            def _():
                _, recv, fwd, _ = _shard_gather_copies(ins, outs, *sems)
                for r, f in zip(recv, fwd):
                    r.wait_recv()
                    f.start()

            @pl.when((i == ni - 1) & (j == nj - 1))
            def _():
                send, _, fwd, fwd_recv = _shard_gather_copies(ins, outs, *sems)
                for r in fwd_recv:
                    r.wait_recv()
                for cp in send + fwd:
                    cp.wait_send()

    outs = pl.pallas_call(
        body, name="inproj_qkv", grid=(ni, nj),
        in_specs=[pl.BlockSpec((tm, K), lambda i, j: (i, 0)), pl.BlockSpec((K, tn), lambda i, j: (0, j))] + [ANY] * n,
        out_specs=[pl.BlockSpec((tm, tn), lambda i, j: (i, j))] + [ANY] * n,
        out_shape=[jax.ShapeDtypeStruct((M, N), BF16)] + [jax.ShapeDtypeStruct((4,) + s.shape, s.dtype) for s in shards],
        scratch_shapes=[pltpu.SemaphoreType.DMA((n, 6)), pltpu.SemaphoreType.DMA((n, 6))] if n else [],
        compiler_params=_params(),
    )(a, b, *shards)
    return outs[0], outs[1:]


def _matmul_lhs_resident(a, b, *, tn, name, exchange=None):
    M, K = a.shape
    N = b.shape[1]
    nj = N // tn
    hosts = exchange is not None

    def body(a_hbm, b_ref, *rest):
        if hosts:
            x_hbm, o_ref, r_hbm, a_s, ssem, rsem = rest
        else:
            o_ref, a_s = rest
        j = pl.program_id(0)

        def to_sibling():
            mx, my, mc = _coords()
            return pltpu.make_async_remote_copy(src_ref=x_hbm.at[1 - mc], dst_ref=r_hbm, send_sem=ssem, recv_sem=rsem,
                                                device_id=(mx, my, 1 - mc), device_id_type=MESH)

        @pl.when(j == 0)
        def _():
            if hosts:
                to_sibling().start()
            pltpu.sync_copy(a_hbm, a_s)

        o_ref[...] = jnp.dot(a_s[...], b_ref[...], preferred_element_type=F32).astype(o_ref.dtype)

        if hosts:
            @pl.when(j == nj - 1)
            def _():
                cp = to_sibling()
                cp.wait_recv()
                cp.wait_send()

    prod = jax.ShapeDtypeStruct((M, N), BF16)
    out = pl.pallas_call(
        body, name=name, grid=(nj,),
        in_specs=[ANY, pl.BlockSpec((K, tn), lambda j: (0, j))] + ([ANY] if hosts else []),
        out_specs=[pl.BlockSpec((M, tn), lambda j: (0, j)), ANY] if hosts else pl.BlockSpec((M, tn), lambda j: (0, j)),
        out_shape=[prod, jax.ShapeDtypeStruct(exchange.shape[1:], exchange.dtype)] if hosts else prod,
        scratch_shapes=[pltpu.VMEM((M, K), BF16)]
                       + ([pltpu.SemaphoreType.DMA(()), pltpu.SemaphoreType.DMA(())] if hosts else []),
        compiler_params=_params(),
    )(a, b, *([exchange] if hosts else []))
    return out


SHARD_COLS = 2816


def _column_tables(part):
    if part == "nat":
        bw = 256
        orig = [j * bw if j < 16 else 8704 + (j - 16) * bw for j in range(NAT // bw)]
    else:
        bw = 128
        orig = []
        for jj in range(NQKV // bw):
            g, p, t = jj // 12, (jj % 12) // 3, jj % 3
            orig.append(4096 + t * 1536 + g * 512 + p * 128)
    tk = jnp.array([o // SHARD_COLS for o in orig], jnp.int32)
    tc = jnp.array([(o % SHARD_COLS) // bw for o in orig], jnp.int32)
    return tk, tc, bw


def _weights_prep(gath, own, chipv, part):
    D = D_MODEL
    tk, tc, bw = _column_tables(part)
    n = tk.shape[0]
    per_step = 2 if part == "nat" else 4
    assert n % per_step == 0

    def body(tk_ref, tc_ref, chip_ref, *refs):
        w_ref = refs[-1]
        for u in range(per_step):
            g_ref, o_ref = refs[2 * u], refs[2 * u + 1]
            mine = tk_ref[pl.program_id(0) * per_step + u] == chip_ref[0]
            w_ref[:, u * bw:(u + 1) * bw] = jnp.where(mine, o_ref[...], g_ref[...])

    def gath_spec(u):
        def idx(j, tk, tc, ch):
            k = tk[j * per_step + u]
            mine = k == ch[0]
            return (jnp.where(mine, (ch[0] + 1) % 4, k), 0, jnp.where(mine, 0, tc[j * per_step + u]))
        return pl.BlockSpec((None, D, bw), idx)

    def own_spec(u):
        def idx(j, tk, tc, ch):
            return (0, jnp.where(tk[j * per_step + u] == ch[0], tc[j * per_step + u], 0))
        return pl.BlockSpec((D, bw), idx)

    specs = []
    for u in range(per_step):
        specs += [gath_spec(u), own_spec(u)]
    return pl.pallas_call(
        body, name="weights_prep_" + part,
        grid_spec=pltpu.PrefetchScalarGridSpec(
            num_scalar_prefetch=3, grid=(n // per_step,),
            in_specs=specs,
            out_specs=pl.BlockSpec((D, per_step * bw), lambda j, tk, tc, ch: (0, j))),
        out_shape=jax.ShapeDtypeStruct((D, n * bw), BF16),
        compiler_params=_params(),
    )(tk, tc, chipv, *([gath, own] * per_step))


def _pair_add_to_shards(core, gw, r, into, part):
    D = D_MODEL
    tk, tc, bw = _column_tables(part)
    n = tk.shape[0]

    def body(tk_ref, tc_ref, core_ref, g_ref, r_ref, into_ref, o_ref):
        o_ref[...] = (g_ref[...].astype(F32) + r_ref[...].astype(F32)).astype(BF16)

    return pl.pallas_call(
        body, name="grad_pair_add_" + part,
        grid_spec=pltpu.PrefetchScalarGridSpec(
            num_scalar_prefetch=3, grid=(n,),
            in_specs=[pl.BlockSpec((None, D // 2, bw), lambda j, tk, tc, c: (c[0], 0, j)),
                      pl.BlockSpec((D // 2, bw), lambda j, tk, tc, c: (0, j)), ANY],
            out_specs=pl.BlockSpec((None, D // 2, bw), lambda j, tk, tc, c: (tk[j], 0, tc[j]))),
        out_shape=jax.ShapeDtypeStruct(into.shape, BF16),
        input_output_aliases={5: 0},
        compiler_params=_params(),
    )(tk, tc, core, gw, r, into)


def _head_norm(xb, w, ebd):
    x = xb.astype(F32)
    ss = jnp.dot((x * x).astype(BF16), ebd, preferred_element_type=F32)
    return x * lax.rsqrt(ss * (1.0 / HEAD_DIM) + EPS) * w


def _window_mask(no_prev):
    qi = lax.broadcasted_iota(jnp.int32, (128, 256), 0)
    kc = lax.broadcasted_iota(jnp.int32, (128, 256), 1)
    ok = (kc >= qi) & (kc <= qi + 128)
    if no_prev is not None:
        ok = ok & ((kc >= 128) | jnp.logical_not(no_prev))
    return ok


def _lane_masks():
    lane = lax.broadcasted_iota(jnp.int32, (1, 128), 1)
    return [(lane < HEAD_DIM).astype(F32), (lane >= HEAD_DIM).astype(F32)]


def _head_col(v, mh):
    return jnp.sum(v * mh, axis=1, keepdims=True) * (1.0 / HEAD_DIM)


def _attn_fwd(projq, qnw2, knw2, ebd, g):
    S = projq.shape[0]
    d = DILATIONS[g]
    T = ATT_TILE
    nt = S // T
    nb = T // (128 * d)
    sdt = BF16 if d == 1 else F32

    def body(cur_ref, qw_ref, kw_ref, ebd_ref, o_ref, lse_ref, qs_s, ks_s, vs_s):
        t = pl.program_id(1)
        e = ebd_ref[...]

        @pl.when(t == 0)
        def _():
            ks_s[pl.ds(0, T), :] = jnp.zeros((T, 128), sdt)
            vs_s[pl.ds(0, T), :] = jnp.zeros((T, 128), sdt)

        @pl.when(t > 0)
        def _():
            ks_s[pl.ds(0, T), :] = ks_s[pl.ds(T, T), :]
            vs_s[pl.ds(0, T), :] = vs_s[pl.ds(T, T), :]

        qs_s[...] = (_head_norm(cur_ref[:, 0:128], qw_ref[...], e) * (QK_SCALE * LOG2E)).astype(sdt)
        ks_s[pl.ds(T, T), :] = _head_norm(cur_ref[:, 128:256], kw_ref[...], e).astype(sdt)
        vs_s[pl.ds(T, T), :] = cur_ref[:, 256:384].astype(sdt)
        m0, m1 = _lane_masks()
        first = t == 0
        for r in range(d):
            for bb in range(nb):
                q0 = r + bb * 128 * d
                k0 = T + r + (bb - 1) * 128 * d
                qs = qs_s[pl.ds(q0, 128, stride=d), :].astype(F32)
                kb = ks_s[pl.ds(k0, 256, stride=d), :].astype(BF16)
                vb = vs_s[pl.ds(k0, 256, stride=d), :].astype(BF16)
                ok = _window_mask(first if bb == 0 else None)
                ok2 = jnp.concatenate([ok, ok], axis=0)
                q2 = jnp.concatenate([qs * m0, qs * m1], axis=0).astype(BF16)
                s = lax.dot_general(q2, kb, (((1,), (1,)), ((), ())), preferred_element_type=F32)
                s = jnp.where(ok2, s, NEG_INF)
                mx = jnp.max(s, axis=-1, keepdims=True)
                p = jnp.exp2(s - mx)
                l = jnp.sum(p, axis=-1, keepdims=True)
                o = jnp.dot(p.astype(BF16), vb, preferred_element_type=F32) / l
                lse = mx * LN2 + jnp.log(l)
                o_ref[pl.ds(q0, 128, stride=d), :] = jnp.where(m0 > 0, o[0:128], o[128:256])
                lse_ref[pl.ds(q0, 128, stride=d), :] = jnp.where(m0 > 0, lse[0:128], lse[128:256])

    return pl.pallas_call(
        body, name=f"attn_fwd_g{g}", grid=(4, nt),
        in_specs=[pl.BlockSpec((T, 384), lambda p, t: (t, g * 4 + p)),
                  pl.BlockSpec((1, 128), lambda p, t: (0, 0)),
                  pl.BlockSpec((1, 128), lambda p, t: (0, 0)),
                  pl.BlockSpec((128, 128), lambda p, t: (0, 0))],
        out_specs=[pl.BlockSpec((T, 128), lambda p, t: (t, p)), pl.BlockSpec((T, 128), lambda p, t: (t, p))],
        out_shape=[jax.ShapeDtypeStruct((S, ATTN_OUT), F32), jax.ShapeDtypeStruct((S, ATTN_OUT), F32)],
        scratch_shapes=[pltpu.VMEM((T, 128), sdt), pltpu.VMEM((2 * T, 128), sdt), pltpu.VMEM((2 * T, 128), sdt)],
        compiler_params=_params(),
    )(projq, qnw2, knw2, ebd)


def _attn_bwd(projq, dprojq, d_o, lse, delta, qnw2, knw2, ebd, g):
    S = projq.shape[0]
    d = DILATIONS[g]
    T = ATT_TILE
    nt = S // T
    nb = T // (128 * d)
    sdt = F32

    def norm_bwd(raw_b, dy, w, e):
        x = raw_b.astype(F32)
        ss = jnp.dot((x * x).astype(BF16), e, preferred_element_type=F32)
        rstd = lax.rsqrt(ss * (1.0 / HEAD_DIM) + EPS)
        xh = x * rstd
        gw = jnp.sum(dy * xh, axis=0, keepdims=True)
        dxh = dy * w
        mean = jnp.dot((dxh * xh).astype(BF16), e, preferred_element_type=F32) * (1.0 / HEAD_DIM)
        return rstd * (dxh - xh * mean), gw

    nt_dims = (((1,), (1,)), ((), ()))
    tn_dims = (((0,), (0,)), ((), ()))

    def unit_stacked(qs, dos, ls, dls, kb, vb, ok, m0, m1):
        ok2 = jnp.concatenate([ok, ok], axis=0)
        q2 = jnp.concatenate([qs * m0, qs * m1], axis=0).astype(BF16)
        do2 = jnp.concatenate([dos * m0, dos * m1], axis=0).astype(BF16)
        lcol = jnp.concatenate([_head_col(ls, m0), _head_col(ls, m1)], axis=0)
        dcol = jnp.concatenate([_head_col(dls, m0), _head_col(dls, m1)], axis=0)
        s = jnp.where(ok2, lax.dot_general(q2, kb, nt_dims, preferred_element_type=F32), NEG_INF)
        p = jnp.exp2(s - lcol * LOG2E)
        dp = lax.dot_general(do2, vb, nt_dims, preferred_element_type=F32)
        ds = (p * (dp - dcol)).astype(BF16)
        dq2 = jnp.dot(ds, kb, preferred_element_type=F32)
        dk = lax.dot_general(ds, q2, tn_dims, preferred_element_type=F32)
        dv = lax.dot_general(p.astype(BF16), do2, tn_dims, preferred_element_type=F32)
        return dq2[0:128] * m0 + dq2[128:256] * m1, dk, dv

    def unit_per_head(qs, dos, ls, dls, kb, vb, ok, m0, m1):
        dq_t = dk_t = dv_t = None
        for mh in (m0, m1):
            qh = (qs * mh).astype(BF16)
            doh = (dos * mh).astype(BF16)
            s = jnp.where(ok, lax.dot_general(qh, kb, nt_dims, preferred_element_type=F32), NEG_INF)
            p = jnp.exp2(s - _head_col(ls, mh) * LOG2E)
            dp = lax.dot_general(doh, vb, nt_dims, preferred_element_type=F32)
            ds = (p * (dp - _head_col(dls, mh))).astype(BF16)
            dq = jnp.dot(ds, kb, preferred_element_type=F32) * mh
            dk = lax.dot_general(ds, qh, tn_dims, preferred_element_type=F32)
            dv = lax.dot_general(p.astype(BF16), doh, tn_dims, preferred_element_type=F32)
            dq_t = dq if dq_t is None else dq_t + dq
            dk_t = dk if dk_t is None else dk_t + dk
            dv_t = dv if dv_t is None else dv_t + dv
        return dq_t, dk_t, dv_t

    unit = unit_per_head if d == 1 else unit_stacked

    def body(cur_ref, prev_ref, do_ref, lse_ref, dl_ref, qw_ref, kw_ref, ebd_ref, dp_in_ref,
             out_ref, gqk_ref, qs_s, ks_s, vs_s, do_s, dq_cur, dq_prev, dk_acc, dv_acc):
        i = pl.program_id(1)
        e = ebd_ref[...]
        m0, m1 = _lane_masks()

        @pl.when(i == 0)
        def _():
            dk_acc[...] = jnp.zeros_like(dk_acc)
            dv_acc[...] = jnp.zeros_like(dv_acc)
            dq_prev[...] = jnp.zeros_like(dq_prev)
            gqk_ref[...] = jnp.zeros_like(gqk_ref)

        @pl.when(i < nt)
        def _():
            qs_s[...] = _head_norm(cur_ref[:, 0:128], qw_ref[...], e) * (QK_SCALE * LOG2E)
            ks_s[pl.ds(0, T), :] = _head_norm(prev_ref[:, 128:256], kw_ref[...], e)
            ks_s[pl.ds(T, T), :] = _head_norm(cur_ref[:, 128:256], kw_ref[...], e)
            vs_s[pl.ds(0, T), :] = prev_ref[:, 256:384].astype(F32)
            vs_s[pl.ds(T, T), :] = cur_ref[:, 256:384].astype(F32)
            do_s[...] = do_ref[...].astype(F32)
            first = i == 0
            for r in range(d):
                own_k = own_v = None
                for bb in range(nb):
                    q0 = r + bb * 128 * d
                    k0 = T + r + (bb - 1) * 128 * d
                    qs = qs_s[pl.ds(q0, 128, stride=d), :]
                    dos = do_s[pl.ds(q0, 128, stride=d), :]
                    ls = lse_ref[pl.ds(q0, 128, stride=d), :]
                    dls = dl_ref[pl.ds(q0, 128, stride=d), :]
                    kb = ks_s[pl.ds(k0, 256, stride=d), :].astype(BF16)
                    vb = vs_s[pl.ds(k0, 256, stride=d), :].astype(BF16)
                    ok = _window_mask(first if bb == 0 else None)
                    dq_t, dk_t, dv_t = unit(qs, dos, ls, dls, kb, vb, ok, m0, m1)
                    dq_cur[pl.ds(q0, 128, stride=d), :] = dq_t
                    before = pl.ds(k0, 128, stride=d)
                    if bb == 0:
                        dk_acc[before, :] = dk_acc[before, :] + dk_t[0:128]
                        dv_acc[before, :] = dv_acc[before, :] + dv_t[0:128]
                    else:
                        dk_acc[before, :] = own_k + dk_t[0:128]
                        dv_acc[before, :] = own_v + dv_t[0:128]
                    own_k, own_v = dk_t[128:256], dv_t[128:256]
                tail = pl.ds(T + r + (nb - 1) * 128 * d, 128, stride=d)
                dk_acc[tail, :] = own_k
                dv_acc[tail, :] = own_v

        @pl.when(i >= 1)
        def _():
            dq_raw, gq = norm_bwd(prev_ref[:, 0:128], dq_prev[...] * QK_SCALE, qw_ref[...], e)
            dk_raw, gk = norm_bwd(prev_ref[:, 128:256], dk_acc[pl.ds(0, T), :] * LN2, kw_ref[...], e)
            out_ref[:, 0:128] = dq_raw.astype(BF16)
            out_ref[:, 128:256] = dk_raw.astype(BF16)
            out_ref[:, 256:384] = dv_acc[pl.ds(0, T), :].astype(BF16)
            gqk_ref[0:1, :] += gq
            gqk_ref[1:2, :] += gk

        dq_prev[...] = dq_cur[...]
        dk_acc[pl.ds(0, T), :] = dk_acc[pl.ds(T, T), :]
        dv_acc[pl.ds(0, T), :] = dv_acc[pl.ds(T, T), :]

    last = nt - 1
    qtile = lambda p, i: (jnp.minimum(i, last), p)
    return pl.pallas_call(
        body, name=f"attn_bwd_g{g}", grid=(4, nt + 1),
        in_specs=[pl.BlockSpec((T, 384), lambda p, i: (jnp.minimum(i, last), g * 4 + p)),
                  pl.BlockSpec((T, 384), lambda p, i: (jnp.maximum(i - 1, 0), g * 4 + p)),
                  pl.BlockSpec((T, 128), qtile), pl.BlockSpec((T, 128), qtile), pl.BlockSpec((T, 128), qtile),
                  pl.BlockSpec((1, 128), lambda p, i: (0, 0)),
                  pl.BlockSpec((1, 128), lambda p, i: (0, 0)),
                  pl.BlockSpec((128, 128), lambda p, i: (0, 0)),
                  ANY],
        out_specs=[pl.BlockSpec((T, 384), lambda p, i: (jnp.maximum(i - 1, 0), g * 4 + p)),
                   pl.BlockSpec((None, 8, 128), lambda p, i: (p, 0, 0))],
        out_shape=[jax.ShapeDtypeStruct(dprojq.shape, BF16), jax.ShapeDtypeStruct((4, 8, 128), F32)],
        scratch_shapes=[pltpu.VMEM((T, 128), sdt), pltpu.VMEM((2 * T, 128), sdt), pltpu.VMEM((2 * T, 128), sdt),
                        pltpu.VMEM((T, 128), sdt), pltpu.VMEM((T, 128), F32), pltpu.VMEM((T, 128), F32),
                        pltpu.VMEM((2 * T, 128), F32), pltpu.VMEM((2 * T, 128), F32)],
        input_output_aliases={8: 0},
        compiler_params=_params(),
    )(projq, projq, d_o, lse, delta, qnw2, knw2, ebd, dprojq)


def _mid(projn, o_g, lse_g, x, tgt, mod, convw8, wc, wa, wo, ebd, *, tm=256):
    S, D = x.shape
    nt = S // tm
    nst = max(1, 256 // tm)
    assert nt % nst == 0
    hb = tm // 16

    def body(pn_ref, hc_ref, hx_ref, o0_ref, o1_ref, o2_ref, l0_ref, l1_ref, l2_ref, x_ref, t_ref, mod_ref,
             cw_ref, ebd_ref, wc_hbm, wa_hbm, wo_hbm,
             dpn_ref, do_ref, lse_ref, dl_ref, dout_ref, vacc_ref, gwo_hbm, gwc_hbm, gwa_hbm,
             wc_s, wa_s, wo_s, gwo_s, gwc_s, gwa_s, carry_s, *stashes):
        i = pl.program_id(0)
        ti = nt - 1 - i
        par = i % nst
        stash = pl.ds(pl.multiple_of(par * tm, tm), tm)

        @pl.when(i == 0)
        def _():
            for src, dst in ((wc_hbm, wc_s), (wa_hbm, wa_s), (wo_hbm, wo_s)):
                pltpu.sync_copy(src, dst)
            gwo_s[...] = jnp.zeros_like(gwo_s)
            gwc_s[...] = jnp.zeros_like(gwc_s)
            gwa_s[...] = jnp.zeros_like(gwa_s)
            carry_s[...] = jnp.zeros_like(carry_s)
            vacc_ref[...] = jnp.zeros_like(vacc_ref)

        rows = lax.broadcasted_iota(jnp.int32, (tm, D), 0)
        w0, w1, w2 = cw_ref[0:1, :], cw_ref[1:2, :], cw_ref[2:3, :]
        gate = mod_ref[:, 2 * D:3 * D]

        b_a = pn_ref[:, BA:BA + D].astype(F32)
        c_a = pn_ref[:, CA:CA + D].astype(F32)
        x_a = pn_ref[:, XA:XA + D].astype(F32)
        z_a = pn_ref[:, ZA:ZA + D].astype(F32)
        u = c_a * x_a
        has_prev = (ti > 0).astype(F32)
        uh = hc_ref[...].astype(F32) * hx_ref[...].astype(F32) * has_prev
        h14, h15 = _row(uh, 14), _row(uh, 15)
        u_m1 = jnp.where(rows == 0, h15, pltpu.roll(u, 1, 0))
        u_m2 = jnp.where(rows == 0, h14, jnp.where(rows == 1, h15, pltpu.roll(u, 2, 0)))
        conv = w0 * u_m2 + w1 * u_m1 + w2 * u
        sg_a = _sigmoid(z_a)
        sz_a = z_a * sg_a
        y_a = b_a * conv * sz_a
        y_ab = y_a.astype(BF16)
        pa = jnp.dot(y_ab, wc_s[...], preferred_element_type=F32)

        l0, l1, l2 = l0_ref[...], l1_ref[...], l2_ref[...]
        mxl = jnp.maximum(jnp.maximum(l0, l1), l2)
        e0, e1, e2 = jnp.exp(l0 - mxl), jnp.exp(l1 - mxl), jnp.exp(l2 - mxl)
        den = e0 + e1 + e2
        attn = (e0 * o0_ref[...] + e1 * o1_ref[...] + e2 * o2_ref[...]) / den
        lse_ref[...] = mxl + jnp.log(den)
        z_b = pn_ref[:, ZB:ZB + ATTN_OUT].astype(F32)
        sg_b = _sigmoid(z_b)
        sz_b = z_b * sg_b
        y_b = attn * sz_b
        y_bb = y_b.astype(BF16)
        pb = jnp.dot(y_bb, wa_s[...], preferred_element_type=F32)

        s_a = _sigmoid(pn_ref[:, GA:GA + D].astype(F32))
        s_b = _sigmoid(pn_ref[:, GB:GB + D].astype(F32))
        merged = s_a * pa + s_b * pb
        merged_b = merged.astype(BF16)
        mo = jnp.dot(merged_b, wo_s[...], preferred_element_type=F32)
        diff = x_ref[...] + gate * mo - t_ref[...]
        dout = diff * (1.0 / D)
        dout_ref[...] = dout.astype(BF16)
        vacc_ref[4:5, :] += jnp.sum(diff * diff, axis=0, keepdims=True)
        vacc_ref[0:1, :] += jnp.sum(dout * mo, axis=0, keepdims=True)

        d_mo = (dout * gate).astype(BF16)
        nt_dims = (((1,), (1,)), ((), ()))
        dmerged = lax.dot_general(d_mo, wo_s[...], nt_dims, preferred_element_type=F32)
        dpa = (dmerged * s_a).astype(BF16)
        dpb = (dmerged * s_b).astype(BF16)
        dpn_ref[:, GA:GA + D] = (dmerged * pa * s_a * (1.0 - s_a)).astype(BF16)
        dpn_ref[:, GB:GB + D] = (dmerged * pb * s_b * (1.0 - s_b)).astype(BF16)
        tn = (((0,), (0,)), ((), ()))
        if nst == 1:
            gwo_s[...] += lax.dot_general(merged_b, d_mo, tn, preferred_element_type=F32)
            gwc_s[...] += lax.dot_general(y_ab, dpa, tn, preferred_element_type=F32)
            gwa_s[...] += lax.dot_general(y_bb, dpb, tn, preferred_element_type=F32)
        else:
            st_m, st_d, st_ya, st_pa, st_yb, st_pb = stashes
            st_m[stash, :] = merged_b
            st_d[stash, :] = d_mo
            st_ya[stash, :] = y_ab
            st_pa[stash, :] = dpa
            st_yb[stash, :] = y_bb
            st_pb[stash, :] = dpb

            @pl.when(par == nst - 1)
            def _():
                gwo_s[...] += lax.dot_general(st_m[...], st_d[...], tn, preferred_element_type=F32)
                gwc_s[...] += lax.dot_general(st_ya[...], st_pa[...], tn, preferred_element_type=F32)
                gwa_s[...] += lax.dot_general(st_yb[...], st_pb[...], tn, preferred_element_type=F32)

        dy_a = lax.dot_general(dpa, wc_s[...], nt_dims, preferred_element_type=F32)
        dy_b = lax.dot_general(dpb, wa_s[...], nt_dims, preferred_element_type=F32)

        dattn = dy_b * sz_b
        dpn_ref[:, ZB:ZB + ATTN_OUT] = (dy_b * attn * sg_b * (1.0 + z_b * (1.0 - sg_b))).astype(BF16)
        do_ref[...] = dattn.astype(BF16)
        dl_ref[...] = jnp.dot((dattn * attn).astype(BF16), ebd_ref[...], preferred_element_type=F32)

        dpn_ref[:, BA:BA + D] = (dy_a * conv * sz_a).astype(BF16)
        dpn_ref[:, ZA:ZA + D] = (dy_a * b_a * conv * sg_a * (1.0 + z_a * (1.0 - sg_a))).astype(BF16)
        dconv = dy_a * b_a * sz_a
        vacc_ref[1:2, :] += jnp.sum(dconv * u_m2, axis=0, keepdims=True)
        vacc_ref[2:3, :] += jnp.sum(dconv * u_m1, axis=0, keepdims=True)
        vacc_ref[3:4, :] += jnp.sum(dconv * u, axis=0, keepdims=True)
        nxt = carry_s[...]
        n0, n1 = _row(nxt, 0), _row(nxt, 1)
        dc_p1 = jnp.where(rows == tm - 1, n0, pltpu.roll(dconv, tm - 1, 0))
        dc_p2 = jnp.where(rows == tm - 1, n1, jnp.where(rows == tm - 2, n0, pltpu.roll(dconv, tm - 2, 0)))
        du = w2 * dconv + w1 * dc_p1 + w0 * dc_p2
        carry_s[...] = dconv[0:8, :]
        dpn_ref[:, CA:CA + D] = (du * x_a).astype(BF16)
        dpn_ref[:, XA:XA + D] = (du * c_a).astype(BF16)

        @pl.when(i == nt - 1)
        def _():
            pltpu.sync_copy(gwo_s, gwo_hbm)
            pltpu.sync_copy(gwc_s, gwc_hbm)
            pltpu.sync_copy(gwa_s, gwa_hbm)

    rev = lambda i: (nt - 1 - i, 0)
    halo = lambda col: pl.BlockSpec((16, D), lambda i: (jnp.maximum((nt - 1 - i) * hb - 1, 0), col))
    tile512 = pl.BlockSpec((tm, ATTN_OUT), rev)
    tileD = pl.BlockSpec((tm, D), rev)
    const = lambda shape: pl.BlockSpec(shape, lambda i: (0, 0))
    return pl.pallas_call(
        body, name="mid_fwd_bwd", grid=(nt,),
        in_specs=[pl.BlockSpec((tm, NAT), rev), halo(CA // D), halo(XA // D)] + [tile512] * 6
                 + [tileD, tileD, const((1, 3 * D)), const((8, D)), const((ATTN_OUT, ATTN_OUT))] + [ANY] * 3,
        out_specs=[pl.BlockSpec((tm, NAT), rev), tile512, tile512, tile512, tileD, const((8, D)), ANY, ANY, ANY],
        out_shape=[jax.ShapeDtypeStruct((S, NAT), BF16), jax.ShapeDtypeStruct((S, ATTN_OUT), BF16),
                   jax.ShapeDtypeStruct((S, ATTN_OUT), F32), jax.ShapeDtypeStruct((S, ATTN_OUT), F32),
                   jax.ShapeDtypeStruct((S, D), BF16), jax.ShapeDtypeStruct((8, D), F32),
                   jax.ShapeDtypeStruct((D, D), F32), jax.ShapeDtypeStruct((D, D), F32),
                   jax.ShapeDtypeStruct((ATTN_OUT, D), F32)],
        scratch_shapes=[pltpu.VMEM((D, D), BF16), pltpu.VMEM((ATTN_OUT, D), BF16), pltpu.VMEM((D, D), BF16),
                        pltpu.VMEM((D, D), F32), pltpu.VMEM((D, D), F32), pltpu.VMEM((ATTN_OUT, D), F32),
                        pltpu.VMEM((8, D), F32)]
                       + ([pltpu.VMEM((nst * tm, D), BF16)] * 4
                          + [pltpu.VMEM((nst * tm, ATTN_OUT), BF16), pltpu.VMEM((nst * tm, D), BF16)] if nst > 1 else []),
        compiler_params=_params(vmem=V7X_VMEM_LIMIT_RESIDENT),
    )(projn, projn, projn, *o_g, *lse_g, x, tgt, mod, convw8, ebd, wc, wa, wo)


def _dh_matmul(dpn, dpq, w_nat, w_qkv, parts, *, tm=512):
    S = dpn.shape[0]
    D = D_MODEL
    n = len(parts)
    nt = S // tm

    def body(dn_ref, dq_ref, wn_hbm, wq_hbm, *rest):
        p_hbm, rest = rest[:n], rest[n:]
        dh_ref = rest[0]
        q_hbm = rest[1:1 + n]
        wn_s, wq_s = rest[1 + n:3 + n]
        sems = rest[3 + n:]

        @pl.when(pl.program_id(0) == 0)
        def _():
            if n:
                mine, _ = _scatter_copies(p_hbm, q_hbm, *sems)
                for cp in mine:
                    cp.start()
            pltpu.sync_copy(wn_hbm, wn_s)
            pltpu.sync_copy(wq_hbm, wq_s)

        if n:
            @pl.when(pl.program_id(0) == nt - 1)
            def _():
                mine, theirs = _scatter_copies(p_hbm, q_hbm, *sems)
                for cp in theirs:
                    cp.wait_recv()
                for cp in mine:
                    cp.wait_send()

        nt_dims = (((1,), (1,)), ((), ()))
        dh = (lax.dot_general(dn_ref[...], wn_s[...], nt_dims, preferred_element_type=F32)
              + lax.dot_general(dq_ref[...], wq_s[...], nt_dims, preferred_element_type=F32))
        dh_ref[...] = dh.astype(BF16)

    row = lambda w: pl.BlockSpec((tm, w), lambda i: (i, 0))
    outs = pl.pallas_call(
        body, name="dh_matmul", grid=(nt,),
        in_specs=[row(NAT), row(NQKV), ANY, ANY] + [ANY] * n,
        out_specs=[row(D)] + [ANY] * n,
        out_shape=[jax.ShapeDtypeStruct((S, D), BF16)] + [jax.ShapeDtypeStruct(p.shape, p.dtype) for p in parts],
        scratch_shapes=[pltpu.VMEM((D, NAT), BF16), pltpu.VMEM((D, NQKV), BF16)]
                       + ([pltpu.SemaphoreType.DMA((n, 3)), pltpu.SemaphoreType.DMA((n, 3))] if n else []),
        compiler_params=_params(vmem=V7X_VMEM_LIMIT_RESIDENT),
    )(dpn, dpq, w_nat, w_qkv, *parts)
    return outs[0], outs[1:]


def _norm_bwd(dh, x, dout, mod, norm_w, *, tm=512):
    S, D = x.shape

    def body(dh_ref, x_ref, dout_ref, mod_ref, nw_ref, gx_ref, st_ref):
        @pl.when(pl.program_id(0) == 0)
        def _():
            st_ref[...] = jnp.zeros_like(st_ref)

        dh = dh_ref[...].astype(F32)
        scale1 = 1.0 + mod_ref[:, D:2 * D]
        nw = nw_ref[...]
        xv = x_ref[...]
        rstd = lax.rsqrt(jnp.mean(xv * xv, axis=-1, keepdims=True) + EPS)
        xh = xv * rstd
        dhs = dh * scale1
        dxh = dhs * nw
        dx = rstd * (dxh - xh * jnp.mean(dxh * xh, axis=-1, keepdims=True))
        gx_ref[...] = dout_ref[...].astype(F32) + dx
        st_ref[0:1, :] += jnp.sum(dh, axis=0, keepdims=True)
        st_ref[1:2, :] += jnp.sum(dh * (xh * nw), axis=0, keepdims=True)
        st_ref[2:3, :] += jnp.sum(dhs * xh, axis=0, keepdims=True)

    row = pl.BlockSpec((tm, D), lambda i: (i, 0))
    return pl.pallas_call(
        body, name="norm_bwd", grid=(S // tm,),
        in_specs=[row, row, row, pl.BlockSpec((1, 3 * D), lambda i: (0, 0)), pl.BlockSpec((1, D), lambda i: (0, 0))],
        out_specs=[row, pl.BlockSpec((8, D), lambda i: (0, 0))],
        out_shape=[jax.ShapeDtypeStruct((S, D), F32), jax.ShapeDtypeStruct((8, D), F32)],
        compiler_params=_params(),
    )(dh, x, dout, mod, norm_w)


def _pack_smalls(st, vacc, gqk):
    D = D_MODEL

    def body(st_ref, va_ref, gqk_ref, o_ref):
        o_ref[...] = jnp.zeros_like(o_ref)
        o_ref[0:2, :] = st_ref[0:2, :]
        o_ref[2:3, :] = va_ref[0:1, :]
        o_ref[8:9, :] = st_ref[2:3, :]
        o_ref[16:19, :] = va_ref[1:4, :]
        tot = gqk_ref[0]
        for k in range(1, gqk_ref.shape[0]):
            tot = tot + gqk_ref[k]
        tot = tot + pltpu.roll(tot, HEAD_DIM, 1)
        o_ref[24:32, 0:128] = tot
        loss = jnp.sum(va_ref[4:5, :], axis=1, keepdims=True) * (0.5 / D)
        o_ref[26:27, :] = jnp.broadcast_to(loss, (1, D))

    return pl.pallas_call(
        body, name="pack_smalls", out_shape=jax.ShapeDtypeStruct((32, D), F32),
        in_specs=[pl.BlockSpec(memory_space=pltpu.VMEM)] * 3,
        out_specs=pl.BlockSpec(memory_space=pltpu.VMEM), compiler_params=_params(),
    )(st, vacc, gqk)


def _small_update(sm_loc, dm_pad, ct_pad, ada, vecs):
    D = D_MODEL
    nv = len(vecs)
    places = [(0, 3, D), (8, 1, D), (16, 3, 256), (24, 1, HEAD_DIM), (25, 1, HEAD_DIM)]

    def body(*refs):
        sm_ref, dm_ref, ct_ref = refs[0:3]
        wmv = [refs[3 + 3 * t:6 + 3 * t] for t in range(nv + 1)]
        outs = refs[3 + 3 * (nv + 1):]
        res = [outs[4 * t:4 * t + 4] for t in range(nv + 1)]
        loss_ref, tot_s = outs[4 * (nv + 1)], outs[4 * (nv + 1) + 1]

        def update(g, w, m, v, out, at=(slice(None), slice(None))):
            d, mn, vn = _adamw_math(w[at], g, m[at], v[at])
            for ref, val in zip(out, (g, d, mn, vn)):
                ref[at] = val

        gw = jnp.dot(ct_ref[...], dm_ref[...], preferred_element_type=F32, precision=lax.Precision.HIGHEST)
        update(gw, *wmv[0], res[0])
        tot = sm_ref[0]
        for k in range(1, 8):
            tot = tot + sm_ref[k]
        tot_s[...] = tot
        for k in range(3):
            update(tot_s[k:k + 1, :], *wmv[1], res[1], at=(slice(None), slice(k * D, (k + 1) * D)))
        for t in range(1, nv):
            r0, nr, nl = places[t]
            update(tot_s[r0:r0 + nr, 0:nl], *wmv[1 + t], res[1 + t])
        loss_ref[...] = jnp.broadcast_to(tot_s[26:27, 0:128], (8, 128))

    vm = pl.BlockSpec(memory_space=pltpu.VMEM)
    groups = [ada] + list(vecs)
    out_shape = []
    for w, _, _ in groups:
        out_shape += [jax.ShapeDtypeStruct(w.shape, F32)] * 4
    out_shape.append(jax.ShapeDtypeStruct((8, 128), F32))
    flat = [a for grp in groups for a in grp]
    outs = pl.pallas_call(
        body, name="small_update", out_shape=out_shape,
        in_specs=[vm] * (3 + len(flat)), out_specs=[vm] * len(out_shape),
        scratch_shapes=[pltpu.VMEM((32, D), F32)], compiler_params=_params(),
    )(sm_loc, dm_pad, ct_pad, *flat)
    return [outs[4 * t:4 * t + 4] for t in range(nv + 1)], outs[-1]


def _later_weights(gathered, shards, chip):
    D = D_MODEL
    g_bc, g_ba, g_wo = [lax.dynamic_update_slice(g, s[None], (chip, 0, 0, 0)) for g, s in zip(gathered, shards)]
    wa = g_ba.reshape(4, ATTN_OUT, 256).transpose(1, 0, 2).reshape(ATTN_OUT, D)
    return g_bc.reshape(D, D), wa, g_wo.reshape(D, D)


def _local_step(xs, tg, mod, norm_w, w_nat, w_qkv, later, convw8, q_norm_w, k_norm_w):
    S = xs.shape[0]
    lane = jnp.arange(128)
    ebd128 = (lane[:, None] // HEAD_DIM == lane[None, :] // HEAD_DIM).astype(BF16)
    lane5 = jnp.arange(ATTN_OUT)
    ebd512 = (lane5[:, None] // HEAD_DIM == lane5[None, :] // HEAD_DIM).astype(BF16)
    qnw2 = jnp.tile(q_norm_w, (1, 2))
    knw2 = jnp.tile(k_norm_w, (1, 2))

    projn, h, ht = _inproj_nat(xs, mod, norm_w, w_nat)
    if len(later) == 2:
        projq, gathered = _inproj_qkv(h, w_qkv, later[0])
        wc, wa, wo = _later_weights(gathered, *later)
    else:
        projq, _ = _inproj_qkv(h, w_qkv, [])
        wc, wa, wo = later
    o_g, lse_g = [], []
    for g in range(N_GROUPS):
        o, l = _attn_fwd(projq, qnw2, knw2, ebd128, g)
        o_g.append(o)
        lse_g.append(l)

    dpn, d_o, lse, delta, dout, vacc, gwo, gwc, gwa = _mid(
        projn, o_g, lse_g, xs, tg, mod, convw8, wc, wa, wo, ebd512)
    dpq = lax.empty((S, NQKV), BF16)
    gqk = []
    for g in range(N_GROUPS):
        dpq, part = _attn_bwd(projq, dpq, d_o, lse, delta, qnw2, knw2, ebd128, g)
        gqk.append(part)
    return dpn, dpq, dout, ht, gwc, gwa, gwo, vacc, gqk


def kernel(x, c, w_ada, b_ada, norm_w, w_in, conv_w, q_norm_w, k_norm_w, w_br_conv, w_br_attn, w_out, loss_target, m_w_ada, m_b_ada, m_norm_w, m_w_in, m_conv_w, m_q_norm_w, m_k_norm_w, m_w_br_conv, m_w_br_attn, m_w_out, v_w_ada, v_b_ada, v_norm_w, v_w_in, v_conv_w, v_q_norm_w, v_k_norm_w, v_w_br_conv, v_w_br_attn, v_w_out):
    D = D_MODEL
    S = x.shape[1]
    xs, tg = x[0], loss_target[0]
    mx, my, mc = lax.axis_index("x"), lax.axis_index("y"), lax.axis_index("c")
    chip = 2 * mx + my
    dev = 2 * chip + mc

    c_blk = jnp.concatenate([c, jnp.pad(conv_w[0], ((0, 0), (0, D - 256))), jnp.zeros((4, D), F32)], axis=0)
    b_shard = lax.dynamic_slice(b_ada, (0, chip * 768), (1, 768))
    shards = [w_in[0].astype(BF16).reshape(2, 512, 2816), w_br_conv[0].astype(BF16).reshape(2, 128, D),
              w_br_attn[0].astype(BF16).reshape(2, 256, 256), w_out[0].astype(BF16).reshape(2, 128, D)]
    gathered_in, _, cg, mod_parts, silu_c = _weights_allgather(
        shards[0].reshape(2, 2, 256, SHARD_COLS), [], c_blk, w_ada[0], b_shard)
    convw8 = jnp.pad(cg[::2, 1:4, 0:256].transpose(1, 0, 2).reshape(3, D), ((0, 5), (0, 0)))
    mod_all = mod_parts.transpose(1, 0, 2).reshape(8, 3 * D)
    mod = lax.dynamic_slice(mod_all, (dev, 0), (1, 3 * D))
    chipv = jnp.reshape(chip, (1,)).astype(jnp.int32)
    g_in = gathered_in.reshape(4, D, SHARD_COLS)
    own_in = shards[0].reshape(D, SHARD_COLS)
    w_nat = _weights_prep(g_in, own_in, chipv, "nat")
    w_qkv = _weights_prep(g_in, own_in, chipv, "qkv")

    dpn, dpq, dout, ht, gwc, gwa, gwo, vacc, gqk = _local_step(
        xs, tg, mod, norm_w, w_nat, w_qkv, (shards[1:], chip), convw8, q_norm_w, k_norm_w)

    def halves(a):
        k, r, cc = a.shape
        return a.reshape(k, 2, r // 2, cc).transpose(1, 0, 2, 3)

    gw_nat = _matmul_lhs_resident(ht, dpn, tn=512, name="grad_w_in_nat").reshape(2, D // 2, NAT)
    gw_qkv, recv_nat = _matmul_lhs_resident(ht, dpq, tn=512, name="grad_w_in_qkv", exchange=gw_nat)
    grads = [gw_nat, gw_qkv.reshape(2, D // 2, NQKV),
             halves(gwc.astype(BF16).reshape(4, 256, D)),
             halves(gwa.astype(BF16).reshape(ATTN_OUT, 4, 256).transpose(1, 0, 2)),
             halves(gwo.astype(BF16).reshape(4, 256, D))]
    recv = [recv_nat] + list(_pair_exchange(grads[1:]))
    core = jnp.reshape(mc, (1,)).astype(jnp.int32)
    p_in = _pair_add_to_shards(core, grads[0], recv[0], lax.empty((4, D // 2, SHARD_COLS), BF16), "nat")
    p_in = _pair_add_to_shards(core, grads[1], recv[1], p_in, "qkv")
    parts = [p_in] + [_pair_add(core, gr, rc) for gr, rc in zip(grads[2:], recv[2:])]
    dh, gathered = _dh_matmul(dpn, dpq, w_nat, w_qkv, parts)
    grad_x, st = _norm_bwd(dh, xs, dout, mod, norm_w)
    mine = [_sum4(chipv, p, q) for p, q in zip(parts, gathered)]
    theirs, sm_all = _half_exchange(mine, _pack_smalls(st, vacc, jnp.concatenate(gqk, axis=0)))

    big = {}
    for t, (nm, w, m, v) in enumerate((("w_in", w_in, m_w_in, v_w_in), ("w_br_conv", w_br_conv, m_w_br_conv, v_w_br_conv),
                                       ("w_br_attn", w_br_attn, m_w_br_attn, v_w_br_attn), ("w_out", w_out, m_w_out, v_w_out))):
        big[nm] = _adamw_halves(core, w[0], mine[t], theirs[t], m[0], v[0], "adamw_" + nm)

    dmod_all = sm_all[:, 0:3, :].reshape(8, 3 * D)
    dm_pad = jnp.pad(lax.dynamic_slice(dmod_all, (0, chip * 768), (8, 768)), ((0, 120), (0, 0)))
    conv_dev = lax.dynamic_slice(sm_all[:, 16:24, :], (0, 0, chip * 256), (8, 8, 256))
    sm_loc = jnp.concatenate([sm_all[:, 0:16, :], jnp.pad(conv_dev, ((0, 0), (0, 0), (0, D - 256))), sm_all[:, 24:32, :]], axis=1)

    ct_pad = jnp.pad(silu_c.T, ((0, 0), (0, 120)))
    small_names = ["b_ada", "norm_w", "conv_w", "q_norm_w", "k_norm_w"]
    vecs = [(b_ada, m_b_ada, v_b_ada), (norm_w, m_norm_w, v_norm_w), (conv_w[0], m_conv_w[0], v_conv_w[0]),
            (q_norm_w, m_q_norm_w, v_q_norm_w), (k_norm_w, m_k_norm_w, v_k_norm_w)]
    updated, loss_blk = _small_update(sm_loc, dm_pad, ct_pad, (w_ada[0], m_w_ada[0], v_w_ada[0]), vecs)
    small = {"w_ada": [a[None] for a in updated[0]]}
    for nm, four in zip(small_names, updated[1:]):
        small[nm] = [a[None] for a in four] if nm == "conv_w" else list(four)

    order = ["w_ada", "b_ada", "norm_w", "w_in", "conv_w", "q_norm_w", "k_norm_w", "w_br_conv", "w_br_attn", "w_out"]
    res = [loss_blk[0, 0], grad_x[None]]
    for kind in range(4):
        for nm in order:
            res.append(big[nm][kind][None] if nm in big else small[nm][kind])
    return tuple(res)
```

```python
import jax
import jax.numpy as jnp
from jax import lax
from jax.experimental import pallas as pl
from jax.experimental.pallas import tpu as pltpu

F32 = jnp.float32
BF16 = jnp.bfloat16
MESH = pl.DeviceIdType.MESH
ANY = pl.BlockSpec(memory_space=pl.ANY)

D_MODEL = 1024
HEAD_DIM = 64
N_GROUPS = 3
DILATIONS = (1, 4, 16)
ATTN_OUT = 512
EPS = 1e-6
NEG_INF = -1e30
NAT = 6656
NQKV = 4608
BA, CA, XA, ZA, ZB, GA, GB = 0, 1024, 2048, 3072, 4096, 4608, 5632
ATT_TILE = 2048
QK_SCALE = HEAD_DIM ** -0.5
LOG2E = 1.4426950408889634
LN2 = 0.6931471805599453
V7X_VMEM_LIMIT = 56 * 1024 * 1024
V7X_VMEM_LIMIT_RESIDENT = 60 * 1024 * 1024

ADAM_LR = 0.001
ADAM_B1 = 0.9
ADAM_B2 = 0.999
ADAM_EPS = 1e-08
ADAM_WD = 0.01
ADAM_STEP = 10


def _params(vmem=V7X_VMEM_LIMIT, **kw):
    return pltpu.CompilerParams(vmem_limit_bytes=vmem, **kw)


def _sigmoid(z):
    return 1.0 / (1.0 + jnp.exp(-z))


def _row(v, r):
    rows = lax.broadcasted_iota(jnp.int32, v.shape, 0)
    return jnp.sum(jnp.where(rows == r, v, 0.0), axis=0, keepdims=True)


def _coords():
    return lax.axis_index("x"), lax.axis_index("y"), lax.axis_index("c")


def _flip(v, bit):
    return 1 - v if bit else v


def _weights_allgather(big_shard, shards, c_blk, w_ada, b_shard):
    n = len(shards)
    D = D_MODEL
    cols = w_ada.shape[1]

    def body(*refs):
        big_in, ins = refs[0], refs[1:1 + n]
        c_ref, wada_ref, b_ref = refs[1 + n:4 + n]
        big_out, outs = refs[4 + n], refs[5 + n:5 + 2 * n]
        cg_ref, mp_ref, sc_ref = refs[5 + 2 * n:8 + 2 * n]
        sem_refs = refs[8 + 2 * n:]
        ssem, rsem = sem_refs[:2] if n else (None, None)
        bs, br, cs, cr, ms, mr, lsem = sem_refs[2 if n else 0:]
        mx, my, mc = _coords()
        me = 2 * mx + my
        me8 = 2 * me + mc
        sib = (mx, my, 1 - mc)
        chips = [(_flip(mx, j & 2), _flip(my, j & 1)) for j in range(1, 4)]
        sends = []

        local = pltpu.make_async_copy(c_ref, cg_ref.at[me8], lsem)
        local.start()
        for j in range(1, 8):
            peer = (_flip(mx, j & 4), _flip(my, j & 2), _flip(mc, j & 1))
            cp = pltpu.make_async_remote_copy(
                src_ref=c_ref, dst_ref=cg_ref.at[me8], send_sem=cs.at[j - 1], recv_sem=cr.at[j - 1],
                device_id=peer, device_id_type=MESH)
            cp.start()
            sends.append(cp)

        kx, ky, kd = 2 * (1 - mx) + my, 2 * mx + (1 - my), 2 * (1 - mx) + (1 - my)
        xn, yn = (1 - mx, my, mc), (mx, 1 - my, mc)

        def big(src_chip, q, sem, to, half=None):
            h = mc if half is None else half
            return pltpu.make_async_remote_copy(
                src_ref=big_out.at[src_chip, h, q], dst_ref=big_out.at[src_chip, h, q],
                send_sem=bs.at[sem], recv_sem=br.at[sem], device_id=to, device_id_type=MESH)

        def own(q, sem, to):
            return pltpu.make_async_remote_copy(
                src_ref=big_in.at[mc, q], dst_ref=big_out.at[me, mc, q],
                send_sem=bs.at[sem], recv_sem=br.at[sem], device_id=to, device_id_type=MESH)

        for cp in (own(0, 0, xn), own(1, 3, yn), own(1, 1, xn), own(0, 2, yn)):
            cp.start()
            sends.append(cp)
        for t in range(n):
            for j, (px, py) in enumerate(chips):
                cp = pltpu.make_async_remote_copy(
                    src_ref=ins[t].at[mc], dst_ref=outs[t].at[me, mc], send_sem=ssem.at[t, j],
                    recv_sem=rsem.at[t, j], device_id=(px, py, mc), device_id_type=MESH)
                cp.start()
                sends.append(cp)

        pieces = [(kx, 0), (kx, 1), (ky, 0), (ky, 1), (kd, 0), (kd, 1)]

        def landed(piece, sem, frm, pass_on=None):
            src_chip, q = pieces[piece]
            big(src_chip, q, sem, frm).wait_recv()
            nxt = ([] if pass_on is None else [big(src_chip, q, pass_on[0], pass_on[1])])
            nxt.append(big(src_chip, q, 6 + piece, sib))
            for cp in nxt:
                cp.start()
                sends.append(cp)

        for j in range(1, 8):
            px, py, pc = _flip(mx, j & 4), _flip(my, j & 2), _flip(mc, j & 1)
            pltpu.make_async_remote_copy(
                src_ref=c_ref, dst_ref=cg_ref.at[4 * px + 2 * py + pc], send_sem=cs.at[j - 1],
                recv_sem=cr.at[j - 1], device_id=(px, py, pc), device_id_type=MESH).wait_recv()
        local.wait()
        pick = (lax.broadcasted_iota(jnp.int32, (8, 64), 1) == 8 * lax.broadcasted_iota(jnp.int32, (8, 64), 0))
        cv = jnp.dot(pick.astype(F32), cg_ref[...].reshape(64, D), preferred_element_type=F32,
                     precision=lax.Precision.HIGHEST)
        sc = cv * _sigmoid(cv)
        sc_ref[...] = sc
        mp_ref[me] = jnp.dot(sc, wada_ref[...], preferred_element_type=F32,
                             precision=lax.Precision.HIGHEST) + b_ref[...]
        for j, (px, py) in enumerate(chips):
            cp = pltpu.make_async_remote_copy(
                src_ref=mp_ref.at[me], dst_ref=mp_ref.at[me], send_sem=ms.at[j], recv_sem=mr.at[j],
                device_id=(px, py, mc), device_id_type=MESH)
            cp.start()
            sends.append(cp)

        landed(0, 0, xn, pass_on=(4, yn))
        landed(3, 3, yn, pass_on=(5, xn))
        landed(1, 1, xn)
        landed(2, 2, yn)
        for j, (px, py) in enumerate(chips):
            pltpu.make_async_remote_copy(
                src_ref=mp_ref.at[me], dst_ref=mp_ref.at[2 * px + py], send_sem=ms.at[j], recv_sem=mr.at[j],
                device_id=(px, py, mc), device_id_type=MESH).wait_recv()
        landed(4, 4, yn)
        landed(5, 5, xn)
        for t in range(n):
            for j, (px, py) in enumerate(chips):
                src = 2 * px + py
                pltpu.make_async_remote_copy(
                    src_ref=ins[t].at[mc], dst_ref=outs[t].at[src, mc], send_sem=ssem.at[t, j],
                    recv_sem=rsem.at[t, j], device_id=(px, py, mc), device_id_type=MESH).wait_recv()
                fwd = pltpu.make_async_remote_copy(
                    src_ref=outs[t].at[src, mc], dst_ref=outs[t].at[src, mc], send_sem=ssem.at[t, 3 + j],
                    recv_sem=rsem.at[t, 3 + j], device_id=sib, device_id_type=MESH)
                fwd.start()
                sends.append(fwd)
        for t in range(n):
            for j, (px, py) in enumerate(chips):
                src = 2 * px + py
                pltpu.make_async_remote_copy(
                    src_ref=outs[t].at[src, 1 - mc], dst_ref=outs[t].at[src, 1 - mc], send_sem=ssem.at[t, 3 + j],
                    recv_sem=rsem.at[t, 3 + j], device_id=sib, device_id_type=MESH).wait_recv()
        for i, (src_chip, q) in enumerate(pieces):
            big(src_chip, q, 6 + i, sib, half=1 - mc).wait_recv()
        for cp in sends:
            cp.wait_send()

    vm = pl.BlockSpec(memory_space=pltpu.VMEM)
    outs = pl.pallas_call(
        body, name="weights_allgather",
        out_shape=[jax.ShapeDtypeStruct((4,) + big_shard.shape, big_shard.dtype)]
                  + [jax.ShapeDtypeStruct((4,) + s.shape, s.dtype) for s in shards]
                  + [jax.ShapeDtypeStruct((8,) + c_blk.shape, F32), jax.ShapeDtypeStruct((4, 8, cols), F32),
                     jax.ShapeDtypeStruct((8, D), F32)],
        in_specs=[ANY] * (n + 1) + [vm] * 3, out_specs=[ANY] * (n + 1) + [vm] * 3,
        scratch_shapes=([pltpu.SemaphoreType.DMA((n, 6)), pltpu.SemaphoreType.DMA((n, 6))] if n else [])
                       + [pltpu.SemaphoreType.DMA((12,)), pltpu.SemaphoreType.DMA((12,)),
                          pltpu.SemaphoreType.DMA((7,)), pltpu.SemaphoreType.DMA((7,)),
                          pltpu.SemaphoreType.DMA((3,)), pltpu.SemaphoreType.DMA((3,)), pltpu.SemaphoreType.DMA(())],
        compiler_params=_params(),
    )(big_shard, *shards, c_blk, w_ada, b_shard)
    return outs[0], outs[1:1 + n], outs[1 + n], outs[2 + n], outs[3 + n]


def _shard_gather_copies(ins, outs, ssem, rsem):
    mx, my, mc = _coords()
    me = 2 * mx + my
    sib = (mx, my, 1 - mc)
    send, recv, fwd, fwd_recv = [], [], [], []
    for t in range(len(ins)):
        for j in range(3):
            px, py = _flip(mx, (j + 1) & 2), _flip(my, (j + 1) & 1)
            src = 2 * px + py
            far = dict(send_sem=ssem.at[t, j], recv_sem=rsem.at[t, j], device_id=(px, py, mc), device_id_type=MESH)
            near = dict(send_sem=ssem.at[t, 3 + j], recv_sem=rsem.at[t, 3 + j], device_id=sib, device_id_type=MESH)
            send.append(pltpu.make_async_remote_copy(src_ref=ins[t].at[mc], dst_ref=outs[t].at[me, mc], **far))
            recv.append(pltpu.make_async_remote_copy(src_ref=ins[t].at[mc], dst_ref=outs[t].at[src, mc], **far))
            fwd.append(pltpu.make_async_remote_copy(src_ref=outs[t].at[src, mc], dst_ref=outs[t].at[src, mc], **near))
            fwd_recv.append(pltpu.make_async_remote_copy(
                src_ref=outs[t].at[src, 1 - mc], dst_ref=outs[t].at[src, 1 - mc], **near))
    return send, recv, fwd, fwd_recv


def _pair_exchange(grads):
    n = len(grads)

    def body(*refs):
        ins, outs = refs[:n], refs[n:2 * n]
        ssem, rsem = refs[2 * n:]
        mx, my, mc = _coords()
        sib = (mx, my, 1 - mc)
        sends = []
        for t in range(n):
            cp = pltpu.make_async_remote_copy(
                src_ref=ins[t].at[1 - mc], dst_ref=outs[t], send_sem=ssem.at[t], recv_sem=rsem.at[t],
                device_id=sib, device_id_type=MESH)
            cp.start()
            sends.append(cp)
        for cp in sends:
            cp.wait_recv()
        for cp in sends:
            cp.wait_send()

    return pl.pallas_call(
        body, name="grad_pair_exchange",
        out_shape=[jax.ShapeDtypeStruct(g.shape[1:], g.dtype) for g in grads],
        in_specs=[ANY] * n, out_specs=[ANY] * n,
        scratch_shapes=[pltpu.SemaphoreType.DMA((n,)), pltpu.SemaphoreType.DMA((n,))],
    )(*grads)


def _scatter_copies(ins, outs, ssem, rsem):
    mx, my, mc = _coords()
    me = 2 * mx + my
    mine, theirs = [], []
    for t in range(len(ins)):
        for j in range(1, 4):
            px, py = _flip(mx, j & 2), _flip(my, j & 1)
            mine.append(pltpu.make_async_remote_copy(
                src_ref=ins[t].at[2 * px + py], dst_ref=outs[t].at[me], send_sem=ssem.at[t, j - 1],
                recv_sem=rsem.at[t, j - 1], device_id=(px, py, mc), device_id_type=MESH))
            theirs.append(pltpu.make_async_remote_copy(
                src_ref=ins[t].at[me], dst_ref=outs[t].at[2 * px + py], send_sem=ssem.at[t, j - 1],
                recv_sem=rsem.at[t, j - 1], device_id=(px, py, mc), device_id_type=MESH))
    return mine, theirs


def _half_exchange(halves, smalls):
    n = len(halves)

    def body(*refs):
        ins, sm_ref = refs[:n], refs[n]
        outs, all_ref = refs[n + 1:2 * n + 1], refs[2 * n + 1]
        ssem, rsem, gsem, hsem, lsem = refs[2 * n + 2:]
        mx, my, mc = _coords()
        me = 4 * mx + 2 * my + mc
        sib = (mx, my, 1 - mc)
        local = pltpu.make_async_copy(sm_ref, all_ref.at[me], lsem)
        local.start()
        sends = []
        for t in range(n):
            rc = pltpu.make_async_remote_copy(
                src_ref=ins[t], dst_ref=outs[t], send_sem=ssem.at[t], recv_sem=rsem.at[t],
                device_id=sib, device_id_type=MESH)
            rc.start()
            sends.append(rc)
        gathers = []
        for j in range(1, 8):
            peer = (_flip(mx, j & 4), _flip(my, j & 2), _flip(mc, j & 1))
            cp = pltpu.make_async_remote_copy(
                src_ref=sm_ref, dst_ref=all_ref.at[me], send_sem=gsem.at[j - 1], recv_sem=hsem.at[j - 1],
                device_id=peer, device_id_type=MESH)
            cp.start()
            gathers.append(cp)
        for cp in sends:
            cp.wait_recv()
        for j in range(1, 8):
            px, py, pc = _flip(mx, j & 4), _flip(my, j & 2), _flip(mc, j & 1)
            pltpu.make_async_remote_copy(
                src_ref=sm_ref, dst_ref=all_ref.at[4 * px + 2 * py + pc], send_sem=gsem.at[j - 1],
                recv_sem=hsem.at[j - 1], device_id=(px, py, pc), device_id_type=MESH).wait_recv()
        for cp in sends + gathers:
            cp.wait_send()
        local.wait()

    vm = pl.BlockSpec(memory_space=pltpu.VMEM)
    outs = pl.pallas_call(
        body, name="grad_half_exchange",
        out_shape=[jax.ShapeDtypeStruct(h.shape, h.dtype) for h in halves]
                  + [jax.ShapeDtypeStruct((8,) + smalls.shape, smalls.dtype)],
        in_specs=[ANY] * n + [vm], out_specs=[ANY] * n + [vm],
        scratch_shapes=[pltpu.SemaphoreType.DMA((n,)), pltpu.SemaphoreType.DMA((n,)),
                        pltpu.SemaphoreType.DMA((7,)), pltpu.SemaphoreType.DMA((7,)), pltpu.SemaphoreType.DMA(())],
    )(*halves, smalls)
    return outs[:n], outs[n]


def _row_tile(rows, cols, bytes_per_row_elem=4, budget=2 * 1024 * 1024):
    t = rows
    while t % 2 == 0 and t > 16 and t * cols * bytes_per_row_elem > budget:
        t //= 2
    return t


def _pair_add(core, g, r):
    _, _, rh, cc = g.shape
    tr = _row_tile(rh, cc)

    def body(core_ref, g_ref, r_ref, o_ref):
        o_ref[...] = (g_ref[...].astype(F32) + r_ref[...].astype(F32)).astype(o_ref.dtype)

    return pl.pallas_call(
        body, name="grad_pair_add",
        grid_spec=pltpu.PrefetchScalarGridSpec(
            num_scalar_prefetch=1, grid=(4, rh // tr),
            in_specs=[pl.BlockSpec((None, None, tr, cc), lambda k, i, c: (c[0], k, i, 0)),
                      pl.BlockSpec((None, tr, cc), lambda k, i, c: (k, i, 0))],
            out_specs=pl.BlockSpec((None, tr, cc), lambda k, i, c: (k, i, 0))),
        out_shape=jax.ShapeDtypeStruct(r.shape, BF16),
        compiler_params=_params(),
    )(core, g, r)


def _sum4(chip, p, q):
    _, rh, cc = q.shape
    tr = _row_tile(rh, cc)

    def body(chip_ref, p_ref, q1_ref, q2_ref, q3_ref, o_ref):
        o_ref[...] = ((p_ref[...].astype(F32) + q1_ref[...].astype(F32)) + q2_ref[...].astype(F32)) + q3_ref[...].astype(F32)

    def other(j):
        return pl.BlockSpec((None, tr, cc), lambda i, c: (jnp.bitwise_xor(c[0], j), i, 0))

    return pl.pallas_call(
        body, name="grad_sum4",
        grid_spec=pltpu.PrefetchScalarGridSpec(
            num_scalar_prefetch=1, grid=(rh // tr,),
            in_specs=[pl.BlockSpec((None, tr, cc), lambda i, c: (c[0], i, 0)), other(1), other(2), other(3)],
            out_specs=pl.BlockSpec((tr, cc), lambda i, c: (i, 0))),
        out_shape=jax.ShapeDtypeStruct((rh, cc), F32),
        compiler_params=_params(),
    )(chip, p, q, q, q)


def _adamw_math(w, g, m, v):
    m = ADAM_B1 * m + (1.0 - ADAM_B1) * g
    v = ADAM_B2 * v + (1.0 - ADAM_B2) * (g * g)
    m_hat = m / (1.0 - ADAM_B1 ** ADAM_STEP)
    v_hat = v / (1.0 - ADAM_B2 ** ADAM_STEP)
    delta = -ADAM_LR * (m_hat / (jnp.sqrt(v_hat) + ADAM_EPS) + ADAM_WD * w)
    return delta, m, v


def _adamw_halves(core, w, mine, recv, m, v, name):
    rows, cols = w.shape
    rh = rows // 2
    tr = _row_tile(rh, cols, budget=1024 * 1024)
    nh = rh // tr

    def body(core_ref, w_ref, a_ref, b_ref, m_ref, v_ref, g_ref, d_ref, mo_ref, vo_ref):
        g = jnp.where(pl.program_id(0) == core_ref[0], a_ref[...], b_ref[...])
        d, mn, vn = _adamw_math(w_ref[...], g, m_ref[...], v_ref[...])
        g_ref[...] = g
        d_ref[...] = d
        mo_ref[...] = mn
        vo_ref[...] = vn

    full = pl.BlockSpec((tr, cols), lambda h, i, c: (h * nh + i, 0))
    mine_spec = pl.BlockSpec((tr, cols), lambda h, i, c: (jnp.where(h == c[0], i, 0), 0))
    recv_spec = pl.BlockSpec((tr, cols), lambda h, i, c: (jnp.where(h == c[0], 0, i), 0))
    sd = jax.ShapeDtypeStruct((rows, cols), F32)
    return pl.pallas_call(
        body, name=name,
        grid_spec=pltpu.PrefetchScalarGridSpec(
            num_scalar_prefetch=1, grid=(2, nh),
            in_specs=[full, mine_spec, recv_spec, full, full], out_specs=[full] * 4),
        out_shape=[sd, sd, sd, sd], compiler_params=_params(),
    )(core, w, mine, recv, m, v)


def _inproj_nat(x, mod, norm_w, w, *, tm=1024, tn=1664):
    S, D = x.shape
    N = w.shape[1]
    nt, nj = S // tm, N // tn
    chunk = 256

    def body(x_ref, mod_ref, nw_ref, w_ref, proj_ref, h_ref, ht_ref, h_a, ht_a, h_b, ht_b):
        i, j = pl.program_id(0), pl.program_id(1)

        def make_h(h_s, ht_s):
            shift = mod_ref[:, 0:D]
            scale1 = 1.0 + mod_ref[:, D:2 * D]
            for r in range(tm // chunk):
                xv = x_ref[pl.ds(r * chunk, chunk), :]
                ms = jnp.mean(xv * xv, axis=-1, keepdims=True)
                h = (xv * lax.rsqrt(ms + EPS) * nw_ref[...]) * scale1 + shift
                h_s[pl.ds(r * chunk, chunk), :] = h.astype(BF16)
                ht_s[:, pl.ds(r * chunk, chunk)] = h.T.astype(BF16)

        @pl.when((i == 0) & (j == 0))
        def _():
            make_h(h_a, ht_a)

        def step(cur, nxt):
            @pl.when(j == 0)
            def _():
                h_ref[...] = cur[0][...]
                ht_ref[...] = cur[1][...]

            def project():
                proj_ref[...] = jnp.dot(cur[0][...], w_ref[...], preferred_element_type=F32).astype(BF16)

            ahead = (j == nj - 1) & (i < nt - 1)

            @pl.when(ahead)
            def _():
                project()
                make_h(*nxt)

            @pl.when(jnp.logical_not(ahead))
            def _():
                project()

        @pl.when(i % 2 == 0)
        def _():
            step((h_a, ht_a), (h_b, ht_b))

        @pl.when(i % 2 == 1)
        def _():
            step((h_b, ht_b), (h_a, ht_a))

    def x_tile(i, j):
        return (jnp.where((i == 0) & (j == 0), 0, jnp.minimum(i + 1, nt - 1)), 0)

    return pl.pallas_call(
        body, name="inproj_nat", grid=(nt, nj),
        in_specs=[pl.BlockSpec((tm, D), x_tile),
                  pl.BlockSpec((1, 3 * D), lambda i, j: (0, 0)),
                  pl.BlockSpec((1, D), lambda i, j: (0, 0)),
                  pl.BlockSpec((D, tn), lambda i, j: (0, j))],
        out_specs=[pl.BlockSpec((tm, tn), lambda i, j: (i, j)),
                   pl.BlockSpec((tm, D), lambda i, j: (i, 0)),
                   pl.BlockSpec((D, tm), lambda i, j: (0, i))],
        out_shape=[jax.ShapeDtypeStruct((S, N), BF16), jax.ShapeDtypeStruct((S, D), BF16),
                   jax.ShapeDtypeStruct((D, S), BF16)],
        scratch_shapes=[pltpu.VMEM((tm, D), BF16), pltpu.VMEM((D, tm), BF16),
                        pltpu.VMEM((tm, D), BF16), pltpu.VMEM((D, tm), BF16)],
        compiler_params=_params(),
    )(x, mod, norm_w, w)


def _inproj_qkv(a, b, shards, *, tm=1024, tn=1536):
    M, K = a.shape
    N = b.shape[1]
    n = len(shards)
    ni, nj = M // tm, N // tn

    def body(a_ref, b_ref, *rest):
        ins, o_ref, outs, sems = rest[:n], rest[n], rest[n + 1:2 * n + 1], rest[2 * n + 1:]
        i, j = pl.program_id(0), pl.program_id(1)
        if n:
            @pl.when((i == 0) & (j == 0))
            def _():
                for cp in _shard_gather_copies(ins, outs, *sems)[0]:
                    cp.start()

        o_ref[...] = jnp.dot(a_ref[...], b_ref[...], preferred_element_type=F32).astype(o_ref.dtype)

        if n:
            @pl.when((i == ni - 1) & (j == 0))
            def _():
                _, recv, fwd, _ = _shard_gather_copies(ins, outs, *sems)
                for r, f in zip(recv, fwd):
                    r.wait_recv()
                    f.start()

            @pl.when((i == ni - 1) & (j == nj - 1))
            def _():
                send, _, fwd, fwd_recv = _shard_gather_copies(ins, outs, *sems)
                for r in fwd_recv:
                    r.wait_recv()
                for cp in send + fwd:
                    cp.wait_send()

    outs = pl.pallas_call(
        body, name="inproj_qkv", grid=(ni, nj),
        in_specs=[pl.BlockSpec((tm, K), lambda i, j: (i, 0)), pl.BlockSpec((K, tn), lambda i, j: (0, j))] + [ANY] * n,
        out_specs=[pl.BlockSpec((tm, tn), lambda i, j: (i, j))] + [ANY] * n,
        out_shape=[jax.ShapeDtypeStruct((M, N), BF16)] + [jax.ShapeDtypeStruct((4,) + s.shape, s.dtype) for s in shards],
        scratch_shapes=[pltpu.SemaphoreType.DMA((n, 6)), pltpu.SemaphoreType.DMA((n, 6))] if n else [],
        compiler_params=_params(),
    )(a, b, *shards)
    return outs[0], outs[1:]


def _matmul_lhs_resident(a, b, *, tn, name, exchange=None):
    M, K = a.shape
    N = b.shape[1]
    nj = N // tn
    hosts = exchange is not None

    def body(a_hbm, b_ref, *rest):
        if hosts:
            x_hbm, o_ref, r_hbm, a_s, ssem, rsem = rest
        else:
            o_ref, a_s = rest
        j = pl.program_id(0)

        def to_sibling():
            mx, my, mc = _coords()
            return pltpu.make_async_remote_copy(src_ref=x_hbm.at[1 - mc], dst_ref=r_hbm, send_sem=ssem, recv_sem=rsem,
                                                device_id=(mx, my, 1 - mc), device_id_type=MESH)

        @pl.when(j == 0)
        def _():
            if hosts:
                to_sibling().start()
            pltpu.sync_copy(a_hbm, a_s)

        o_ref[...] = jnp.dot(a_s[...], b_ref[...], preferred_element_type=F32).astype(o_ref.dtype)

        if hosts:
            @pl.when(j == nj - 1)
            def _():
                cp = to_sibling()
                cp.wait_recv()
                cp.wait_send()

    prod = jax.ShapeDtypeStruct((M, N), BF16)
    out = pl.pallas_call(
        body, name=name, grid=(nj,),
        in_specs=[ANY, pl.BlockSpec((K, tn), lambda j: (0, j))] + ([ANY] if hosts else []),
        out_specs=[pl.BlockSpec((M, tn), lambda j: (0, j)), ANY] if hosts else pl.BlockSpec((M, tn), lambda j: (0, j)),
        out_shape=[prod, jax.ShapeDtypeStruct(exchange.shape[1:], exchange.dtype)] if hosts else prod,
        scratch_shapes=[pltpu.VMEM((M, K), BF16)]
                       + ([pltpu.SemaphoreType.DMA(()), pltpu.SemaphoreType.DMA(())] if hosts else []),
        compiler_params=_params(),
    )(a, b, *([exchange] if hosts else []))
    return out


SHARD_COLS = 2816


def _column_tables(part):
    if part == "nat":
        bw = 256
        orig = [j * bw if j < 16 else 8704 + (j - 16) * bw for j in range(NAT // bw)]
    else:
        bw = 128
        orig = []
        for jj in range(NQKV // bw):
            g, p, t = jj // 12, (jj % 12) // 3, jj % 3
            orig.append(4096 + t * 1536 + g * 512 + p * 128)
    tk = jnp.array([o // SHARD_COLS for o in orig], jnp.int32)
    tc = jnp.array([(o % SHARD_COLS) // bw for o in orig], jnp.int32)
    return tk, tc, bw


def _weights_prep(gath, own, chipv, part):
    D = D_MODEL
    tk, tc, bw = _column_tables(part)
    n = tk.shape[0]
    per_step = 2 if part == "nat" else 4
    assert n % per_step == 0

    def body(tk_ref, tc_ref, chip_ref, *refs):
        w_ref = refs[-1]
        for u in range(per_step):
            g_ref, o_ref = refs[2 * u], refs[2 * u + 1]
            mine = tk_ref[pl.program_id(0) * per_step + u] == chip_ref[0]
            w_ref[:, u * bw:(u + 1) * bw] = jnp.where(mine, o_ref[...], g_ref[...])

    def gath_spec(u):
        def idx(j, tk, tc, ch):
            k = tk[j * per_step + u]
            mine = k == ch[0]
            return (jnp.where(mine, (ch[0] + 1) % 4, k), 0, jnp.where(mine, 0, tc[j * per_step + u]))
        return pl.BlockSpec((None, D, bw), idx)

    def own_spec(u):
        def idx(j, tk, tc, ch):
            return (0, jnp.where(tk[j * per_step + u] == ch[0], tc[j * per_step + u], 0))
        return pl.BlockSpec((D, bw), idx)

    specs = []
    for u in range(per_step):
        specs += [gath_spec(u), own_spec(u)]
    return pl.pallas_call(
        body, name="weights_prep_" + part,
        grid_spec=pltpu.PrefetchScalarGridSpec(
            num_scalar_prefetch=3, grid=(n // per_step,),
            in_specs=specs,
            out_specs=pl.BlockSpec((D, per_step * bw), lambda j, tk, tc, ch: (0, j))),
        out_shape=jax.ShapeDtypeStruct((D, n * bw), BF16),
        compiler_params=_params(),
    )(tk, tc, chipv, *([gath, own] * per_step))


def _pair_add_to_shards(core, gw, r, into, part):
    D = D_MODEL
    tk, tc, bw = _column_tables(part)
    n = tk.shape[0]
    if part == "qkv":
        per = 2
        src = [[g * 12 + (2 * pp + u) * 3 + t for u in range(per)] for g in range(3) for t in range(3) for pp in range(2)]
    else:
        per = 1
        src = [[j] for j in range(n)]
    first = jnp.array([s[0] for s in src], jnp.int32)
    ts = [jnp.array([s[u] for s in src], jnp.int32) for u in range(per)]
    ok, oc = tk[first], tc[first] // per

    def body(*refs):
        ins, o_ref = refs[3 + per:3 + 3 * per], refs[-1]
        for u in range(per):
            g_ref, r_ref = ins[2 * u], ins[2 * u + 1]
            o_ref[:, u * bw:(u + 1) * bw] = (g_ref[...].astype(F32) + r_ref[...].astype(F32)).astype(BF16)

    specs = []
    for u in range(per):
        specs += [pl.BlockSpec((None, D // 2, bw), lambda j, ok, oc, c, *ts, u=u: (c[0], 0, ts[u][j])),
                  pl.BlockSpec((D // 2, bw), lambda j, ok, oc, c, *ts, u=u: (0, ts[u][j]))]
    return pl.pallas_call(
        body, name="grad_pair_add_" + part,
        grid_spec=pltpu.PrefetchScalarGridSpec(
            num_scalar_prefetch=3 + per, grid=(len(src),),
            in_specs=specs + [ANY],
            out_specs=pl.BlockSpec((None, D // 2, bw * per), lambda j, ok, oc, c, *ts: (ok[j], 0, oc[j]))),
        out_shape=jax.ShapeDtypeStruct(into.shape, BF16),
        input_output_aliases={3 + 3 * per: 0},
        compiler_params=_params(),
    )(ok, oc, core, *ts, *([gw, r] * per), into)


def _head_norm(xb, w, ebd):
    x = xb.astype(F32)
    ss = jnp.dot((x * x).astype(BF16), ebd, preferred_element_type=F32)
    return x * lax.rsqrt(ss * (1.0 / HEAD_DIM) + EPS) * w


def _window_mask(no_prev):
    qi = lax.broadcasted_iota(jnp.int32, (128, 256), 0)
    kc = lax.broadcasted_iota(jnp.int32, (128, 256), 1)
    ok = (kc >= qi) & (kc <= qi + 128)
    if no_prev is not None:
        ok = ok & ((kc >= 128) | jnp.logical_not(no_prev))
    return ok


def _lane_masks():
    lane = lax.broadcasted_iota(jnp.int32, (1, 128), 1)
    return [(lane < HEAD_DIM).astype(F32), (lane >= HEAD_DIM).astype(F32)]


def _head_col(v, mh):
    return jnp.sum(v * mh, axis=1, keepdims=True) * (1.0 / HEAD_DIM)


def _attn_fwd(projq, qnw2, knw2, ebd, g):
    S = projq.shape[0]
    d = DILATIONS[g]
    T = ATT_TILE
    nt = S // T
    nb = T // (128 * d)
    sdt = BF16 if d == 1 else F32

    def body(cur_ref, qw_ref, kw_ref, ebd_ref, o_ref, lse_ref, qs_s, ks_s, vs_s):
        t = pl.program_id(1)
        e = ebd_ref[...]

        @pl.when(t == 0)
        def _():
            ks_s[pl.ds(0, T), :] = jnp.zeros((T, 128), sdt)
            vs_s[pl.ds(0, T), :] = jnp.zeros((T, 128), sdt)

        @pl.when(t > 0)
        def _():
            ks_s[pl.ds(0, T), :] = ks_s[pl.ds(T, T), :]
            vs_s[pl.ds(0, T), :] = vs_s[pl.ds(T, T), :]

        qs_s[...] = (_head_norm(cur_ref[:, 0:128], qw_ref[...], e) * (QK_SCALE * LOG2E)).astype(sdt)
        ks_s[pl.ds(T, T), :] = _head_norm(cur_ref[:, 128:256], kw_ref[...], e).astype(sdt)
        vs_s[pl.ds(T, T), :] = cur_ref[:, 256:384].astype(sdt)
        m0, m1 = _lane_masks()
        first = t == 0
        for r in range(d):
            for bb in range(nb):
                q0 = r + bb * 128 * d
                k0 = T + r + (bb - 1) * 128 * d
                qs = qs_s[pl.ds(q0, 128, stride=d), :].astype(F32)
                kb = ks_s[pl.ds(k0, 256, stride=d), :].astype(BF16)
                vb = vs_s[pl.ds(k0, 256, stride=d), :].astype(BF16)
                ok = _window_mask(first if bb == 0 else None)
                ok2 = jnp.concatenate([ok, ok], axis=0)
                q2 = jnp.concatenate([qs * m0, qs * m1], axis=0).astype(BF16)
                s = lax.dot_general(q2, kb, (((1,), (1,)), ((), ())), preferred_element_type=F32)
                s = jnp.where(ok2, s, NEG_INF)
                mx = jnp.max(s, axis=-1, keepdims=True)
                p = jnp.exp2(s - mx)
                l = jnp.sum(p, axis=-1, keepdims=True)
                o = jnp.dot(p.astype(BF16), vb, preferred_element_type=F32) / l
                lse = mx * LN2 + jnp.log(l)
                o_ref[pl.ds(q0, 128, stride=d), :] = jnp.where(m0 > 0, o[0:128], o[128:256])
                lse_ref[pl.ds(q0, 128, stride=d), :] = jnp.where(m0 > 0, lse[0:128], lse[128:256])

    return pl.pallas_call(
        body, name=f"attn_fwd_g{g}", grid=(4, nt),
        in_specs=[pl.BlockSpec((T, 384), lambda p, t: (t, g * 4 + p)),
                  pl.BlockSpec((1, 128), lambda p, t: (0, 0)),
                  pl.BlockSpec((1, 128), lambda p, t: (0, 0)),
                  pl.BlockSpec((128, 128), lambda p, t: (0, 0))],
        out_specs=[pl.BlockSpec((T, 128), lambda p, t: (t, p)), pl.BlockSpec((T, 128), lambda p, t: (t, p))],
        out_shape=[jax.ShapeDtypeStruct((S, ATTN_OUT), F32), jax.ShapeDtypeStruct((S, ATTN_OUT), F32)],
        scratch_shapes=[pltpu.VMEM((T, 128), sdt), pltpu.VMEM((2 * T, 128), sdt), pltpu.VMEM((2 * T, 128), sdt)],
        compiler_params=_params(),
    )(projq, qnw2, knw2, ebd)


def _attn_bwd(projq, dprojq, d_o, lse, delta, qnw2, knw2, ebd, g):
    S = projq.shape[0]
    d = DILATIONS[g]
    T = ATT_TILE
    nt = S // T
    nb = T // (128 * d)
    sdt = F32

    def norm_bwd(raw_b, dy, w, e):
        x = raw_b.astype(F32)
        ss = jnp.dot((x * x).astype(BF16), e, preferred_element_type=F32)
        rstd = lax.rsqrt(ss * (1.0 / HEAD_DIM) + EPS)
        xh = x * rstd
        gw = jnp.sum(dy * xh, axis=0, keepdims=True)
        dxh = dy * w
        mean = jnp.dot((dxh * xh).astype(BF16), e, preferred_element_type=F32) * (1.0 / HEAD_DIM)
        return rstd * (dxh - xh * mean), gw

    nt_dims = (((1,), (1,)), ((), ()))
    tn_dims = (((0,), (0,)), ((), ()))

    def unit_stacked(qs, dos, ls, dls, kb, vb, ok, m0, m1):
        ok2 = jnp.concatenate([ok, ok], axis=0)
        q2 = jnp.concatenate([qs * m0, qs * m1], axis=0).astype(BF16)
        do2 = jnp.concatenate([dos * m0, dos * m1], axis=0).astype(BF16)
        lcol = jnp.concatenate([_head_col(ls, m0), _head_col(ls, m1)], axis=0)
        dcol = jnp.concatenate([_head_col(dls, m0), _head_col(dls, m1)], axis=0)
        s = jnp.where(ok2, lax.dot_general(q2, kb, nt_dims, preferred_element_type=F32), NEG_INF)
        p = jnp.exp2(s - lcol * LOG2E)
        dp = lax.dot_general(do2, vb, nt_dims, preferred_element_type=F32)
        ds = (p * (dp - dcol)).astype(BF16)
        dq2 = jnp.dot(ds, kb, preferred_element_type=F32)
        dk = lax.dot_general(ds, q2, tn_dims, preferred_element_type=F32)
        dv = lax.dot_general(p.astype(BF16), do2, tn_dims, preferred_element_type=F32)
        return dq2[0:128] * m0 + dq2[128:256] * m1, dk, dv

    def unit_per_head(qs, dos, ls, dls, kb, vb, ok, m0, m1):
        dq_t = dk_t = dv_t = None
        for mh in (m0, m1):
            qh = (qs * mh).astype(BF16)
            doh = (dos * mh).astype(BF16)
            s = jnp.where(ok, lax.dot_general(qh, kb, nt_dims, preferred_element_type=F32), NEG_INF)
            p = jnp.exp2(s - _head_col(ls, mh) * LOG2E)
            dp = lax.dot_general(doh, vb, nt_dims, preferred_element_type=F32)
            ds = (p * (dp - _head_col(dls, mh))).astype(BF16)
            dq = jnp.dot(ds, kb, preferred_element_type=F32) * mh
            dk = lax.dot_general(ds, qh, tn_dims, preferred_element_type=F32)
            dv = lax.dot_general(p.astype(BF16), doh, tn_dims, preferred_element_type=F32)
            dq_t = dq if dq_t is None else dq_t + dq
            dk_t = dk if dk_t is None else dk_t + dk
            dv_t = dv if dv_t is None else dv_t + dv
        return dq_t, dk_t, dv_t

    unit = unit_per_head if d == 1 else unit_stacked

    def body(cur_ref, prev_ref, do_ref, lse_ref, dl_ref, qw_ref, kw_ref, ebd_ref, dp_in_ref,
             out_ref, gqk_ref, qs_s, ks_s, vs_s, do_s, dq_cur, dq_prev, dk_acc, dv_acc):
        i = pl.program_id(1)
        e = ebd_ref[...]
        m0, m1 = _lane_masks()

        @pl.when(i == 0)
        def _():
            dk_acc[...] = jnp.zeros_like(dk_acc)
            dv_acc[...] = jnp.zeros_like(dv_acc)
            dq_prev[...] = jnp.zeros_like(dq_prev)
            gqk_ref[...] = jnp.zeros_like(gqk_ref)

        @pl.when(i < nt)
        def _():
            qs_s[...] = _head_norm(cur_ref[:, 0:128], qw_ref[...], e) * (QK_SCALE * LOG2E)
            ks_s[pl.ds(0, T), :] = _head_norm(prev_ref[:, 128:256], kw_ref[...], e)
            ks_s[pl.ds(T, T), :] = _head_norm(cur_ref[:, 128:256], kw_ref[...], e)
            vs_s[pl.ds(0, T), :] = prev_ref[:, 256:384].astype(F32)
            vs_s[pl.ds(T, T), :] = cur_ref[:, 256:384].astype(F32)
            do_s[...] = do_ref[...].astype(F32)
            first = i == 0
            for r in range(d):
                own_k = own_v = None
                for bb in range(nb):
                    q0 = r + bb * 128 * d
                    k0 = T + r + (bb - 1) * 128 * d
                    qs = qs_s[pl.ds(q0, 128, stride=d), :]
                    dos = do_s[pl.ds(q0, 128, stride=d), :]
                    ls = lse_ref[pl.ds(q0, 128, stride=d), :]
                    dls = dl_ref[pl.ds(q0, 128, stride=d), :]
                    kb = ks_s[pl.ds(k0, 256, stride=d), :].astype(BF16)
                    vb = vs_s[pl.ds(k0, 256, stride=d), :].astype(BF16)
                    ok = _window_mask(first if bb == 0 else None)
                    dq_t, dk_t, dv_t = unit(qs, dos, ls, dls, kb, vb, ok, m0, m1)
                    dq_cur[pl.ds(q0, 128, stride=d), :] = dq_t
                    before = pl.ds(k0, 128, stride=d)
                    if bb == 0:
                        dk_acc[before, :] = dk_acc[before, :] + dk_t[0:128]
                        dv_acc[before, :] = dv_acc[before, :] + dv_t[0:128]
                    else:
                        dk_acc[before, :] = own_k + dk_t[0:128]
                        dv_acc[before, :] = own_v + dv_t[0:128]
                    own_k, own_v = dk_t[128:256], dv_t[128:256]
                tail = pl.ds(T + r + (nb - 1) * 128 * d, 128, stride=d)
                dk_acc[tail, :] = own_k
                dv_acc[tail, :] = own_v

        @pl.when(i >= 1)
        def _():
            dq_raw, gq = norm_bwd(prev_ref[:, 0:128], dq_prev[...] * QK_SCALE, qw_ref[...], e)
            dk_raw, gk = norm_bwd(prev_ref[:, 128:256], dk_acc[pl.ds(0, T), :] * LN2, kw_ref[...], e)
            out_ref[:, 0:128] = dq_raw.astype(BF16)
            out_ref[:, 128:256] = dk_raw.astype(BF16)
            out_ref[:, 256:384] = dv_acc[pl.ds(0, T), :].astype(BF16)
            gqk_ref[0:1, :] += gq
            gqk_ref[1:2, :] += gk

        dq_prev[...] = dq_cur[...]
        dk_acc[pl.ds(0, T), :] = dk_acc[pl.ds(T, T), :]
        dv_acc[pl.ds(0, T), :] = dv_acc[pl.ds(T, T), :]

    last = nt - 1
    qtile = lambda p, i: (jnp.minimum(i, last), p)
    return pl.pallas_call(
        body, name=f"attn_bwd_g{g}", grid=(4, nt + 1),
        in_specs=[pl.BlockSpec((T, 384), lambda p, i: (jnp.minimum(i, last), g * 4 + p)),
                  pl.BlockSpec((T, 384), lambda p, i: (jnp.maximum(i - 1, 0), g * 4 + p)),
                  pl.BlockSpec((T, 128), qtile), pl.BlockSpec((T, 128), qtile), pl.BlockSpec((T, 128), qtile),
                  pl.BlockSpec((1, 128), lambda p, i: (0, 0)),
                  pl.BlockSpec((1, 128), lambda p, i: (0, 0)),
                  pl.BlockSpec((128, 128), lambda p, i: (0, 0)),
                  ANY],
        out_specs=[pl.BlockSpec((T, 384), lambda p, i: (jnp.maximum(i - 1, 0), g * 4 + p)),
                   pl.BlockSpec((None, 8, 128), lambda p, i: (p, 0, 0))],
        out_shape=[jax.ShapeDtypeStruct(dprojq.shape, BF16), jax.ShapeDtypeStruct((4, 8, 128), F32)],
        scratch_shapes=[pltpu.VMEM((T, 128), sdt), pltpu.VMEM((2 * T, 128), sdt), pltpu.VMEM((2 * T, 128), sdt),
                        pltpu.VMEM((T, 128), sdt), pltpu.VMEM((T, 128), F32), pltpu.VMEM((T, 128), F32),
                        pltpu.VMEM((2 * T, 128), F32), pltpu.VMEM((2 * T, 128), F32)],
        input_output_aliases={8: 0},
        compiler_params=_params(),
    )(projq, projq, d_o, lse, delta, qnw2, knw2, ebd, dprojq)


def _mid(projn, o_g, lse_g, x, tgt, mod, convw8, wc, wa, wo, ebd, *, tm=256):
    S, D = x.shape
    nt = S // tm
    nst = max(1, 256 // tm)
    assert nt % nst == 0
    hb = tm // 16

    def body(pn_ref, hc_ref, hx_ref, o0_ref, o1_ref, o2_ref, l0_ref, l1_ref, l2_ref, x_ref, t_ref, mod_ref,
             cw_ref, ebd_ref, wc_hbm, wa_hbm, wo_hbm,
             dpn_ref, do_ref, lse_ref, dl_ref, dout_ref, vacc_ref, gwo_hbm, gwc_hbm, gwa_hbm,
             wc_s, wa_s, wo_s, gwo_s, gwc_s, gwa_s, carry_s, *stashes):
        i = pl.program_id(0)
        ti = nt - 1 - i
        par = i % nst
        stash = pl.ds(pl.multiple_of(par * tm, tm), tm)

        @pl.when(i == 0)
        def _():
            for src, dst in ((wc_hbm, wc_s), (wa_hbm, wa_s), (wo_hbm, wo_s)):
                pltpu.sync_copy(src, dst)
            gwo_s[...] = jnp.zeros_like(gwo_s)
            gwc_s[...] = jnp.zeros_like(gwc_s)
            gwa_s[...] = jnp.zeros_like(gwa_s)
            carry_s[...] = jnp.zeros_like(carry_s)
            vacc_ref[...] = jnp.zeros_like(vacc_ref)

        rows = lax.broadcasted_iota(jnp.int32, (tm, D), 0)
        w0, w1, w2 = cw_ref[0:1, :], cw_ref[1:2, :], cw_ref[2:3, :]
        gate = mod_ref[:, 2 * D:3 * D]

        b_a = pn_ref[:, BA:BA + D].astype(F32)
        c_a = pn_ref[:, CA:CA + D].astype(F32)
        x_a = pn_ref[:, XA:XA + D].astype(F32)
        z_a = pn_ref[:, ZA:ZA + D].astype(F32)
        u = c_a * x_a
        has_prev = (ti > 0).astype(F32)
        uh = hc_ref[...].astype(F32) * hx_ref[...].astype(F32) * has_prev
        h14, h15 = _row(uh, 14), _row(uh, 15)
        u_m1 = jnp.where(rows == 0, h15, pltpu.roll(u, 1, 0))
        u_m2 = jnp.where(rows == 0, h14, jnp.where(rows == 1, h15, pltpu.roll(u, 2, 0)))
        conv = w0 * u_m2 + w1 * u_m1 + w2 * u
        sg_a = _sigmoid(z_a)
        sz_a = z_a * sg_a
        y_a = b_a * conv * sz_a
        y_ab = y_a.astype(BF16)
        pa = jnp.dot(y_ab, wc_s[...], preferred_element_type=F32)

        l0, l1, l2 = l0_ref[...], l1_ref[...], l2_ref[...]
        mxl = jnp.maximum(jnp.maximum(l0, l1), l2)
        e0, e1, e2 = jnp.exp(l0 - mxl), jnp.exp(l1 - mxl), jnp.exp(l2 - mxl)
        den = e0 + e1 + e2
        attn = (e0 * o0_ref[...] + e1 * o1_ref[...] + e2 * o2_ref[...]) / den
        lse_ref[...] = mxl + jnp.log(den)
        z_b = pn_ref[:, ZB:ZB + ATTN_OUT].astype(F32)
        sg_b = _sigmoid(z_b)
        sz_b = z_b * sg_b
        y_b = attn * sz_b
        y_bb = y_b.astype(BF16)
        pb = jnp.dot(y_bb, wa_s[...], preferred_element_type=F32)

        s_a = _sigmoid(pn_ref[:, GA:GA + D].astype(F32))
        s_b = _sigmoid(pn_ref[:, GB:GB + D].astype(F32))
        merged = s_a * pa + s_b * pb
        merged_b = merged.astype(BF16)
        mo = jnp.dot(merged_b, wo_s[...], preferred_element_type=F32)
        diff = x_ref[...] + gate * mo - t_ref[...]
        dout = diff * (1.0 / D)
        dout_ref[...] = dout.astype(BF16)
        vacc_ref[4:5, :] += jnp.sum(diff * diff, axis=0, keepdims=True)
        vacc_ref[0:1, :] += jnp.sum(dout * mo, axis=0, keepdims=True)

        d_mo = (dout * gate).astype(BF16)
        nt_dims = (((1,), (1,)), ((), ()))
        dmerged = lax.dot_general(d_mo, wo_s[...], nt_dims, preferred_element_type=F32)
        dpa = (dmerged * s_a).astype(BF16)
        dpb = (dmerged * s_b).astype(BF16)
        dpn_ref[:, GA:GA + D] = (dmerged * pa * s_a * (1.0 - s_a)).astype(BF16)
        dpn_ref[:, GB:GB + D] = (dmerged * pb * s_b * (1.0 - s_b)).astype(BF16)
        tn = (((0,), (0,)), ((), ()))
        if nst == 1:
            gwo_s[...] += lax.dot_general(merged_b, d_mo, tn, preferred_element_type=F32)
            gwc_s[...] += lax.dot_general(y_ab, dpa, tn, preferred_element_type=F32)
            gwa_s[...] += lax.dot_general(y_bb, dpb, tn, preferred_element_type=F32)
        else:
            st_m, st_d, st_ya, st_pa, st_yb, st_pb = stashes
            st_m[stash, :] = merged_b
            st_d[stash, :] = d_mo
            st_ya[stash, :] = y_ab
            st_pa[stash, :] = dpa
            st_yb[stash, :] = y_bb
            st_pb[stash, :] = dpb

            @pl.when(par == nst - 1)
            def _():
                gwo_s[...] += lax.dot_general(st_m[...], st_d[...], tn, preferred_element_type=F32)
                gwc_s[...] += lax.dot_general(st_ya[...], st_pa[...], tn, preferred_element_type=F32)
                gwa_s[...] += lax.dot_general(st_yb[...], st_pb[...], tn, preferred_element_type=F32)

        dy_a = lax.dot_general(dpa, wc_s[...], nt_dims, preferred_element_type=F32)
        dy_b = lax.dot_general(dpb, wa_s[...], nt_dims, preferred_element_type=F32)

        dattn = dy_b * sz_b
        dpn_ref[:, ZB:ZB + ATTN_OUT] = (dy_b * attn * sg_b * (1.0 + z_b * (1.0 - sg_b))).astype(BF16)
        do_ref[...] = dattn.astype(BF16)
        dl_ref[...] = jnp.dot((dattn * attn).astype(BF16), ebd_ref[...], preferred_element_type=F32)

        dpn_ref[:, BA:BA + D] = (dy_a * conv * sz_a).astype(BF16)
        dpn_ref[:, ZA:ZA + D] = (dy_a * b_a * conv * sg_a * (1.0 + z_a * (1.0 - sg_a))).astype(BF16)
        dconv = dy_a * b_a * sz_a
        vacc_ref[1:2, :] += jnp.sum(dconv * u_m2, axis=0, keepdims=True)
        vacc_ref[2:3, :] += jnp.sum(dconv * u_m1, axis=0, keepdims=True)
        vacc_ref[3:4, :] += jnp.sum(dconv * u, axis=0, keepdims=True)
        nxt = carry_s[...]
        n0, n1 = _row(nxt, 0), _row(nxt, 1)
        dc_p1 = jnp.where(rows == tm - 1, n0, pltpu.roll(dconv, tm - 1, 0))
        dc_p2 = jnp.where(rows == tm - 1, n1, jnp.where(rows == tm - 2, n0, pltpu.roll(dconv, tm - 2, 0)))
        du = w2 * dconv + w1 * dc_p1 + w0 * dc_p2
        carry_s[...] = dconv[0:8, :]
        dpn_ref[:, CA:CA + D] = (du * x_a).astype(BF16)
        dpn_ref[:, XA:XA + D] = (du * c_a).astype(BF16)

        @pl.when(i == nt - 1)
        def _():
            pltpu.sync_copy(gwo_s, gwo_hbm)
            pltpu.sync_copy(gwc_s, gwc_hbm)
            pltpu.sync_copy(gwa_s, gwa_hbm)

    rev = lambda i: (nt - 1 - i, 0)
    halo = lambda col: pl.BlockSpec((16, D), lambda i: (jnp.maximum((nt - 1 - i) * hb - 1, 0), col))
    tile512 = pl.BlockSpec((tm, ATTN_OUT), rev)
    tileD = pl.BlockSpec((tm, D), rev)
    const = lambda shape: pl.BlockSpec(shape, lambda i: (0, 0))
    return pl.pallas_call(
        body, name="mid_fwd_bwd", grid=(nt,),
        in_specs=[pl.BlockSpec((tm, NAT), rev), halo(CA // D), halo(XA // D)] + [tile512] * 6
                 + [tileD, tileD, const((1, 3 * D)), const((8, D)), const((ATTN_OUT, ATTN_OUT))] + [ANY] * 3,
        out_specs=[pl.BlockSpec((tm, NAT), rev), tile512, tile512, tile512, tileD, const((8, D)), ANY, ANY, ANY],
        out_shape=[jax.ShapeDtypeStruct((S, NAT), BF16), jax.ShapeDtypeStruct((S, ATTN_OUT), BF16),
                   jax.ShapeDtypeStruct((S, ATTN_OUT), F32), jax.ShapeDtypeStruct((S, ATTN_OUT), F32),
                   jax.ShapeDtypeStruct((S, D), BF16), jax.ShapeDtypeStruct((8, D), F32),
                   jax.ShapeDtypeStruct((D, D), F32), jax.ShapeDtypeStruct((D, D), F32),
                   jax.ShapeDtypeStruct((ATTN_OUT, D), F32)],
        scratch_shapes=[pltpu.VMEM((D, D), BF16), pltpu.VMEM((ATTN_OUT, D), BF16), pltpu.VMEM((D, D), BF16),
                        pltpu.VMEM((D, D), F32), pltpu.VMEM((D, D), F32), pltpu.VMEM((ATTN_OUT, D), F32),
                        pltpu.VMEM((8, D), F32)]
                       + ([pltpu.VMEM((nst * tm, D), BF16)] * 4
                          + [pltpu.VMEM((nst * tm, ATTN_OUT), BF16), pltpu.VMEM((nst * tm, D), BF16)] if nst > 1 else []),
        compiler_params=_params(vmem=V7X_VMEM_LIMIT_RESIDENT),
    )(projn, projn, projn, *o_g, *lse_g, x, tgt, mod, convw8, ebd, wc, wa, wo)


def _dh_matmul(dpn, dpq, w_nat, w_qkv, parts, *, tm=512):
    S = dpn.shape[0]
    D = D_MODEL
    n = len(parts)
    nt = S // tm

    def body(dn_ref, dq_ref, wn_hbm, wq_hbm, *rest):
        p_hbm, rest = rest[:n], rest[n:]
        dh_ref = rest[0]
        q_hbm = rest[1:1 + n]
        wn_s, wq_s = rest[1 + n:3 + n]
        sems = rest[3 + n:]

        @pl.when(pl.program_id(0) == 0)
        def _():
            if n:
                mine, _ = _scatter_copies(p_hbm, q_hbm, *sems)
                for cp in mine:
                    cp.start()
            pltpu.sync_copy(wn_hbm, wn_s)
            pltpu.sync_copy(wq_hbm, wq_s)

        if n:
            @pl.when(pl.program_id(0) == nt - 1)
            def _():
                mine, theirs = _scatter_copies(p_hbm, q_hbm, *sems)
                for cp in theirs:
                    cp.wait_recv()
                for cp in mine:
                    cp.wait_send()

        nt_dims = (((1,), (1,)), ((), ()))
        dh = (lax.dot_general(dn_ref[...], wn_s[...], nt_dims, preferred_element_type=F32)
              + lax.dot_general(dq_ref[...], wq_s[...], nt_dims, preferred_element_type=F32))
        dh_ref[...] = dh.astype(BF16)

    row = lambda w: pl.BlockSpec((tm, w), lambda i: (i, 0))
    outs = pl.pallas_call(
        body, name="dh_matmul", grid=(nt,),
        in_specs=[row(NAT), row(NQKV), ANY, ANY] + [ANY] * n,
        out_specs=[row(D)] + [ANY] * n,
        out_shape=[jax.ShapeDtypeStruct((S, D), BF16)] + [jax.ShapeDtypeStruct(p.shape, p.dtype) for p in parts],
        scratch_shapes=[pltpu.VMEM((D, NAT), BF16), pltpu.VMEM((D, NQKV), BF16)]
                       + ([pltpu.SemaphoreType.DMA((n, 3)), pltpu.SemaphoreType.DMA((n, 3))] if n else []),
        compiler_params=_params(vmem=V7X_VMEM_LIMIT_RESIDENT),
    )(dpn, dpq, w_nat, w_qkv, *parts)
    return outs[0], outs[1:]


def _norm_bwd(dh, x, dout, mod, norm_w, *, tm=512):
    S, D = x.shape

    def body(dh_ref, x_ref, dout_ref, mod_ref, nw_ref, gx_ref, st_ref):
        @pl.when(pl.program_id(0) == 0)
        def _():
            st_ref[...] = jnp.zeros_like(st_ref)

        dh = dh_ref[...].astype(F32)
        scale1 = 1.0 + mod_ref[:, D:2 * D]
        nw = nw_ref[...]
        xv = x_ref[...]
        rstd = lax.rsqrt(jnp.mean(xv * xv, axis=-1, keepdims=True) + EPS)
        xh = xv * rstd
        dhs = dh * scale1
        dxh = dhs * nw
        dx = rstd * (dxh - xh * jnp.mean(dxh * xh, axis=-1, keepdims=True))
        gx_ref[...] = dout_ref[...].astype(F32) + dx
        st_ref[0:1, :] += jnp.sum(dh, axis=0, keepdims=True)
        st_ref[1:2, :] += jnp.sum(dh * (xh * nw), axis=0, keepdims=True)
        st_ref[2:3, :] += jnp.sum(dhs * xh, axis=0, keepdims=True)

    row = pl.BlockSpec((tm, D), lambda i: (i, 0))
    return pl.pallas_call(
        body, name="norm_bwd", grid=(S // tm,),
        in_specs=[row, row, row, pl.BlockSpec((1, 3 * D), lambda i: (0, 0)), pl.BlockSpec((1, D), lambda i: (0, 0))],
        out_specs=[row, pl.BlockSpec((8, D), lambda i: (0, 0))],
        out_shape=[jax.ShapeDtypeStruct((S, D), F32), jax.ShapeDtypeStruct((8, D), F32)],
        compiler_params=_params(),
    )(dh, x, dout, mod, norm_w)


def _pack_smalls(st, vacc, gqk):
    D = D_MODEL

    def body(st_ref, va_ref, gqk_ref, o_ref):
        o_ref[...] = jnp.zeros_like(o_ref)
        o_ref[0:2, :] = st_ref[0:2, :]
        o_ref[2:3, :] = va_ref[0:1, :]
        o_ref[8:9, :] = st_ref[2:3, :]
        o_ref[16:19, :] = va_ref[1:4, :]
        tot = gqk_ref[0]
        for k in range(1, gqk_ref.shape[0]):
            tot = tot + gqk_ref[k]
        tot = tot + pltpu.roll(tot, HEAD_DIM, 1)
        o_ref[24:32, 0:128] = tot
        loss = jnp.sum(va_ref[4:5, :], axis=1, keepdims=True) * (0.5 / D)
        o_ref[26:27, :] = jnp.broadcast_to(loss, (1, D))

    return pl.pallas_call(
        body, name="pack_smalls", out_shape=jax.ShapeDtypeStruct((32, D), F32),
        in_specs=[pl.BlockSpec(memory_space=pltpu.VMEM)] * 3,
        out_specs=pl.BlockSpec(memory_space=pltpu.VMEM), compiler_params=_params(),
    )(st, vacc, gqk)


def _small_update(sm_loc, dm_pad, ct_pad, ada, vecs):
    D = D_MODEL
    nv = len(vecs)
    places = [(0, 3, D), (8, 1, D), (16, 3, 256), (24, 1, HEAD_DIM), (25, 1, HEAD_DIM)]

    def body(*refs):
        sm_ref, dm_ref, ct_ref = refs[0:3]
        wmv = [refs[3 + 3 * t:6 + 3 * t] for t in range(nv + 1)]
        outs = refs[3 + 3 * (nv + 1):]
        res = [outs[4 * t:4 * t + 4] for t in range(nv + 1)]
        loss_ref, tot_s = outs[4 * (nv + 1)], outs[4 * (nv + 1) + 1]

        def update(g, w, m, v, out, at=(slice(None), slice(None))):
            d, mn, vn = _adamw_math(w[at], g, m[at], v[at])
            for ref, val in zip(out, (g, d, mn, vn)):
                ref[at] = val

        gw = jnp.dot(ct_ref[...], dm_ref[...], preferred_element_type=F32, precision=lax.Precision.HIGHEST)
        update(gw, *wmv[0], res[0])
        tot = sm_ref[0]
        for k in range(1, 8):
            tot = tot + sm_ref[k]
        tot_s[...] = tot
        for k in range(3):
            update(tot_s[k:k + 1, :], *wmv[1], res[1], at=(slice(None), slice(k * D, (k + 1) * D)))
        for t in range(1, nv):
            r0, nr, nl = places[t]
            update(tot_s[r0:r0 + nr, 0:nl], *wmv[1 + t], res[1 + t])
        loss_ref[...] = jnp.broadcast_to(tot_s[26:27, 0:128], (8, 128))

    vm = pl.BlockSpec(memory_space=pltpu.VMEM)
    groups = [ada] + list(vecs)
    out_shape = []
    for w, _, _ in groups:
        out_shape += [jax.ShapeDtypeStruct(w.shape, F32)] * 4
    out_shape.append(jax.ShapeDtypeStruct((8, 128), F32))
    flat = [a for grp in groups for a in grp]
    outs = pl.pallas_call(
        body, name="small_update", out_shape=out_shape,
        in_specs=[vm] * (3 + len(flat)), out_specs=[vm] * len(out_shape),
        scratch_shapes=[pltpu.VMEM((32, D), F32)], compiler_params=_params(),
    )(sm_loc, dm_pad, ct_pad, *flat)
    return [outs[4 * t:4 * t + 4] for t in range(nv + 1)], outs[-1]


def _later_weights(gathered, shards, chip):
    D = D_MODEL
    g_bc, g_ba, g_wo = [lax.dynamic_update_slice(g, s[None], (chip, 0, 0, 0)) for g, s in zip(gathered, shards)]
    wa = g_ba.reshape(4, ATTN_OUT, 256).transpose(1, 0, 2).reshape(ATTN_OUT, D)
    return g_bc.reshape(D, D), wa, g_wo.reshape(D, D)


def _local_step(xs, tg, mod, norm_w, w_nat, w_qkv, later, convw8, q_norm_w, k_norm_w):
    S = xs.shape[0]
    lane = jnp.arange(128)
    ebd128 = (lane[:, None] // HEAD_DIM == lane[None, :] // HEAD_DIM).astype(BF16)
    lane5 = jnp.arange(ATTN_OUT)
    ebd512 = (lane5[:, None] // HEAD_DIM == lane5[None, :] // HEAD_DIM).astype(BF16)
    qnw2 = jnp.tile(q_norm_w, (1, 2))
    knw2 = jnp.tile(k_norm_w, (1, 2))

    projn, h, ht = _inproj_nat(xs, mod, norm_w, w_nat)
    if len(later) == 2:
        projq, gathered = _inproj_qkv(h, w_qkv, later[0])
        wc, wa, wo = _later_weights(gathered, *later)
    else:
        projq, _ = _inproj_qkv(h, w_qkv, [])
        wc, wa, wo = later
    o_g, lse_g = [], []
    for g in range(N_GROUPS):
        o, l = _attn_fwd(projq, qnw2, knw2, ebd128, g)
        o_g.append(o)
        lse_g.append(l)

    dpn, d_o, lse, delta, dout, vacc, gwo, gwc, gwa = _mid(
        projn, o_g, lse_g, xs, tg, mod, convw8, wc, wa, wo, ebd512)
    dpq = lax.empty((S, NQKV), BF16)
    gqk = []
    for g in range(N_GROUPS):
        dpq, part = _attn_bwd(projq, dpq, d_o, lse, delta, qnw2, knw2, ebd128, g)
        gqk.append(part)
    return dpn, dpq, dout, ht, gwc, gwa, gwo, vacc, gqk


def kernel(x, c, w_ada, b_ada, norm_w, w_in, conv_w, q_norm_w, k_norm_w, w_br_conv, w_br_attn, w_out, loss_target, m_w_ada, m_b_ada, m_norm_w, m_w_in, m_conv_w, m_q_norm_w, m_k_norm_w, m_w_br_conv, m_w_br_attn, m_w_out, v_w_ada, v_b_ada, v_norm_w, v_w_in, v_conv_w, v_q_norm_w, v_k_norm_w, v_w_br_conv, v_w_br_attn, v_w_out):
    D = D_MODEL
    S = x.shape[1]
    xs, tg = x[0], loss_target[0]
    mx, my, mc = lax.axis_index("x"), lax.axis_index("y"), lax.axis_index("c")
    chip = 2 * mx + my
    dev = 2 * chip + mc

    c_blk = jnp.concatenate([c, jnp.pad(conv_w[0], ((0, 0), (0, D - 256))), jnp.zeros((4, D), F32)], axis=0)
    b_shard = lax.dynamic_slice(b_ada, (0, chip * 768), (1, 768))
    shards = [w_in[0].astype(BF16).reshape(2, 512, 2816), w_br_conv[0].astype(BF16).reshape(2, 128, D),
              w_br_attn[0].astype(BF16).reshape(2, 256, 256), w_out[0].astype(BF16).reshape(2, 128, D)]
    gathered_in, _, cg, mod_parts, silu_c = _weights_allgather(
        shards[0].reshape(2, 2, 256, SHARD_COLS), [], c_blk, w_ada[0], b_shard)
    convw8 = jnp.pad(cg[::2, 1:4, 0:256].transpose(1, 0, 2).reshape(3, D), ((0, 5), (0, 0)))
    mod_all = mod_parts.transpose(1, 0, 2).reshape(8, 3 * D)
    mod = lax.dynamic_slice(mod_all, (dev, 0), (1, 3 * D))
    chipv = jnp.reshape(chip, (1,)).astype(jnp.int32)
    g_in = gathered_in.reshape(4, D, SHARD_COLS)
    own_in = shards[0].reshape(D, SHARD_COLS)
    w_nat = _weights_prep(g_in, own_in, chipv, "nat")
    w_qkv = _weights_prep(g_in, own_in, chipv, "qkv")

    dpn, dpq, dout, ht, gwc, gwa, gwo, vacc, gqk = _local_step(
        xs, tg, mod, norm_w, w_nat, w_qkv, (shards[1:], chip), convw8, q_norm_w, k_norm_w)

    def halves(a):
        k, r, cc = a.shape
        return a.reshape(k, 2, r // 2, cc).transpose(1, 0, 2, 3)

    gw_nat = _matmul_lhs_resident(ht, dpn, tn=512, name="grad_w_in_nat").reshape(2, D // 2, NAT)
    gw_qkv, recv_nat = _matmul_lhs_resident(ht, dpq, tn=512, name="grad_w_in_qkv", exchange=gw_nat)
    grads = [gw_nat, gw_qkv.reshape(2, D // 2, NQKV),
             halves(gwc.astype(BF16).reshape(4, 256, D)),
             halves(gwa.astype(BF16).reshape(ATTN_OUT, 4, 256).transpose(1, 0, 2)),
             halves(gwo.astype(BF16).reshape(4, 256, D))]
    recv = [recv_nat] + list(_pair_exchange(grads[1:]))
    core = jnp.reshape(mc, (1,)).astype(jnp.int32)
    p_in = _pair_add_to_shards(core, grads[0], recv[0], lax.empty((4, D // 2, SHARD_COLS), BF16), "nat")
    p_in = _pair_add_to_shards(core, grads[1], recv[1], p_in, "qkv")
    parts = [p_in] + [_pair_add(core, gr, rc) for gr, rc in zip(grads[2:], recv[2:])]
    dh, gathered = _dh_matmul(dpn, dpq, w_nat, w_qkv, parts)
    grad_x, st = _norm_bwd(dh, xs, dout, mod, norm_w)
    mine = [_sum4(chipv, p, q) for p, q in zip(parts, gathered)]
    theirs, sm_all = _half_exchange(mine, _pack_smalls(st, vacc, jnp.concatenate(gqk, axis=0)))

    big = {}
    for t, (nm, w, m, v) in enumerate((("w_in", w_in, m_w_in, v_w_in), ("w_br_conv", w_br_conv, m_w_br_conv, v_w_br_conv),
                                       ("w_br_attn", w_br_attn, m_w_br_attn, v_w_br_attn), ("w_out", w_out, m_w_out, v_w_out))):
        big[nm] = _adamw_halves(core, w[0], mine[t], theirs[t], m[0], v[0], "adamw_" + nm)

    dmod_all = sm_all[:, 0:3, :].reshape(8, 3 * D)
    dm_pad = jnp.pad(lax.dynamic_slice(dmod_all, (0, chip * 768), (8, 768)), ((0, 120), (0, 0)))
    conv_dev = lax.dynamic_slice(sm_all[:, 16:24, :], (0, 0, chip * 256), (8, 8, 256))
    sm_loc = jnp.concatenate([sm_all[:, 0:16, :], jnp.pad(conv_dev, ((0, 0), (0, 0), (0, D - 256))), sm_all[:, 24:32, :]], axis=1)

    ct_pad = jnp.pad(silu_c.T, ((0, 0), (0, 120)))
    small_names = ["b_ada", "norm_w", "conv_w", "q_norm_w", "k_norm_w"]
    vecs = [(b_ada, m_b_ada, v_b_ada), (norm_w, m_norm_w, v_norm_w), (conv_w[0], m_conv_w[0], v_conv_w[0]),
            (q_norm_w, m_q_norm_w, v_q_norm_w), (k_norm_w, m_k_norm_w, v_k_norm_w)]
    updated, loss_blk = _small_update(sm_loc, dm_pad, ct_pad, (w_ada[0], m_w_ada[0], v_w_ada[0]), vecs)
    small = {"w_ada": [a[None] for a in updated[0]]}
    for nm, four in zip(small_names, updated[1:]):
        small[nm] = [a[None] for a in four] if nm == "conv_w" else list(four)

    order = ["w_ada", "b_ada", "norm_w", "w_in", "conv_w", "q_norm_w", "k_norm_w", "w_br_conv", "w_br_attn", "w_out"]
    res = [loss_blk[0, 0], grad_x[None]]
    for kind in range(4):
        for nm in order:
            res.append(big[nm][kind][None] if nm in big else small[nm][kind])
    return tuple(res)
```
